```python
import math
import jax, jax.numpy as jnp
from jax import lax
import numpy as np


D_MODEL = 1024
BATCH = 4
SEQ = 4096
DEPTH = 2
DEC_BATCH = 32
DEC_SEQ = 1
PAST_LEN = 8192
PAGE_SIZE = 128

N_MIXERS = 2
N_NSA_LAYERS = (DEPTH + 1) // 2
N_SSM_LAYERS = DEPTH // 2
N_HEADS = 16
HEAD_DIM = 64
N_KV_HEADS = 4
GROUP = N_HEADS // N_KV_HEADS
CMP_BLOCK = 32
CMP_STRIDE = 16
CMP_RATIO = CMP_BLOCK // CMP_STRIDE
SEL_BLOCK = 64
TOP_N = 16
WINDOW = 512
Q_CHUNK = 64
ROPE_THETA = 10000.0
Q_W = N_HEADS * HEAD_DIM
KV_W = N_KV_HEADS * HEAD_DIM
IN_W = Q_W + 6 * KV_W + 3 * N_HEADS
SSM_GROUP = 16
SSM_GROUPS = D_MODEL // SSM_GROUP
SSM_STATE = 64
D_FF = 4 * D_MODEL
EPS = 1e-6
NEG = -1e30
FORCE = 1e9

kernel_name = 'nsa_s5_hybrid_decoder_step'


def rms_norm(x, g):
    xf = x.astype(jnp.float32)
    y = xf * lax.rsqrt(jnp.mean(xf * xf, axis=-1, keepdims=True) + EPS)
    return (y * g.astype(jnp.float32)).astype(x.dtype)


def rope(x, pos):
    half = HEAD_DIM // 2
    inv = ROPE_THETA ** (-jnp.arange(half, dtype=jnp.float32) / half)
    ang = pos.astype(jnp.float32)[:, None] * inv[None, :]
    cos = jnp.cos(ang)[None, :, None, :]
    sin = jnp.sin(ang)[None, :, None, :]
    xf = x.astype(jnp.float32)
    x1, x2 = xf[..., :half], xf[..., half:]
    return jnp.concatenate([x1 * cos - x2 * sin, x2 * cos + x1 * sin], axis=-1).astype(x.dtype)


def masked_softmax(s, mask):
    s = jnp.where(mask, s.astype(jnp.float32), NEG)
    m = jnp.max(s, axis=-1, keepdims=True)
    e = jnp.where(mask, jnp.exp(s - m), 0.0)
    return e / jnp.maximum(jnp.sum(e, axis=-1, keepdims=True), 1e-20)


def compress(raw, w1, w2, pe):
    b, l = raw.shape[0], raw.shape[1]
    n_sb = l // CMP_STRIDE
    nc = n_sb - CMP_RATIO + 1
    sb = raw[:, :n_sb * CMP_STRIDE].reshape(b, n_sb, CMP_STRIDE, N_KV_HEADS, HEAD_DIM)
    w1r = w1.reshape(CMP_RATIO, CMP_STRIDE, HEAD_DIM, HEAD_DIM)
    per = pe.reshape(CMP_RATIO, CMP_STRIDE, HEAD_DIM)
    parts = []
    for r in range(CMP_RATIO):
        pr = jnp.einsum('bnskd,sde->bnke', sb + per[r][None, None, :, None, :], w1r[r])
        parts.append(pr[:, r:r + nc])
    h = sum(parts[1:], parts[0])
    return jnp.einsum('bnke,ef->bnkf', jax.nn.gelu(h), w2)


def nsa_attend(q, q_rot, gates, q_pos, kc, vc, kblk, vblk, kw, vw, kw_pos):
    b, tq = q.shape[0], q.shape[1]
    scale = HEAD_DIM ** -0.5
    qg = q.reshape(b, tq, N_KV_HEADS, GROUP, HEAD_DIM)
    qrg = q_rot.reshape(b, tq, N_KV_HEADS, GROUP, HEAD_DIM)
    nc = kc.shape[1]
    c_start = jnp.arange(nc, dtype=jnp.int32) * CMP_STRIDE
    mask_c = (c_start + CMP_BLOCK - 1)[None, :] <= q_pos[:, None]
    s_c = jnp.einsum('bqkgd,bckd->bkgqc', qg, kc) * scale
    p_c = masked_softmax(s_c, mask_c)
    o_c = jnp.einsum('bkgqc,bckd->bqkgd', p_c.astype(vc.dtype), vc)
    nsel = kblk.shape[1]
    blk = jnp.arange(nsel, dtype=jnp.int32)
    s_start = blk * SEL_BLOCK
    overlap = ((c_start[:, None] < s_start[None, :] + SEL_BLOCK)
               & (c_start[:, None] + CMP_BLOCK > s_start[None, :])).astype(jnp.float32)
    imp = jnp.einsum('bkgqc,cs->bkqs', p_c, overlap)
    cur = q_pos // SEL_BLOCK
    forced = (blk[None, :] == 0) | (blk[None, :] == cur[:, None]) | (blk[None, :] == cur[:, None] - 1)
    causal = s_start[None, :] <= q_pos[:, None]
    imp = jnp.where(forced, FORCE, jnp.where(causal, imp, NEG))
    n_top = min(TOP_N, nsel)
    _, top = lax.top_k(imp, n_top)
    gather = jax.vmap(jax.vmap(lambda blocks, idx: blocks[idx]))
    k_sel = gather(jnp.moveaxis(kblk, 3, 1), top)
    v_sel = gather(jnp.moveaxis(vblk, 3, 1), top)
    k_pos = top[..., None] * SEL_BLOCK + jnp.arange(SEL_BLOCK, dtype=jnp.int32)
    mask_s = (k_pos <= q_pos[None, None, :, None, None]).reshape(b, N_KV_HEADS, 1, tq, n_top * SEL_BLOCK)
    s_s = jnp.einsum('bqkgd,bkqnpd->bkgqnp', qrg, k_sel) * scale
    p_s = masked_softmax(s_s.reshape(b, N_KV_HEADS, GROUP, tq, n_top * SEL_BLOCK), mask_s)
    o_s = jnp.einsum('bkgqm,bkqmd->bqkgd', p_s.astype(vblk.dtype),
                     v_sel.reshape(b, N_KV_HEADS, tq, n_top * SEL_BLOCK, HEAD_DIM))
    dist = q_pos[:, None] - kw_pos[None, :]
    mask_w = (dist >= 0) & (dist <= WINDOW) & (kw_pos[None, :] >= 0)
    s_w = jnp.einsum('bqkgd,blkd->bkgql', qrg, kw) * scale
    p_w = masked_softmax(s_w, mask_w)
    o_w = jnp.einsum('bkgql,blkd->bqkgd', p_w.astype(vw.dtype), vw)
    gk = gates.reshape(b, tq, N_KV_HEADS, GROUP, 3)[..., None]
    o = gk[..., 0, :] * o_c + gk[..., 1, :] * o_s + gk[..., 2, :] * o_w
    return o.reshape(b, tq, N_HEADS, HEAD_DIM)


def nsa_project(xn, w_in, pos):
    b, t = xn.shape[0], xn.shape[1]
    proj = jnp.einsum('btd,de->bte', xn, w_in)
    q = proj[..., :Q_W].reshape(b, t, N_HEADS, HEAD_DIM)
    kv = proj[..., Q_W:Q_W + 6 * KV_W].reshape(b, t, 6, N_KV_HEADS, HEAD_DIM)
    gates = jax.nn.sigmoid(proj[..., Q_W + 6 * KV_W:].astype(jnp.float32)).astype(xn.dtype)
    gates = gates.reshape(b, t, N_HEADS, 3)
    return (q, rope(q, pos), gates, kv[:, :, 0], kv[:, :, 1], rope(kv[:, :, 2], pos), kv[:, :, 3],
            rope(kv[:, :, 4], pos), kv[:, :, 5])


def nsa_prompt(xn, w_in, w_o, cmp_w1, cmp_w2, cmp_pe):
    b, t = xn.shape[0], xn.shape[1]
    pos = jnp.arange(t, dtype=jnp.int32)
    q, q_rot, gates, k_c, v_c, k_s, v_s, k_w, v_w = nsa_project(xn, w_in, pos)
    kc = compress(k_c, cmp_w1[0], cmp_w2[0], cmp_pe[0])
    vc = compress(v_c, cmp_w1[1], cmp_w2[1], cmp_pe[1])
    nsel = t // SEL_BLOCK
    kblk = k_s.reshape(b, nsel, SEL_BLOCK, N_KV_HEADS, HEAD_DIM)
    vblk = v_s.reshape(b, nsel, SEL_BLOCK, N_KV_HEADS, HEAD_DIM)
    pad = ((0, 0), (WINDOW, 0), (0, 0), (0, 0))
    kwp = jnp.pad(k_w, pad)
    vwp = jnp.pad(v_w, pad)
    n_chunks = t // Q_CHUNK

    def to_chunks(a):
        return jnp.moveaxis(a.reshape((b, n_chunks, Q_CHUNK) + a.shape[2:]), 1, 0)

    starts = jnp.arange(n_chunks, dtype=jnp.int32) * Q_CHUNK

    def chunk_fn(args):
        qc, qrc, gc, start = args
        q_pos = start + jnp.arange(Q_CHUNK, dtype=jnp.int32)
        kw = lax.dynamic_slice_in_dim(kwp, start, Q_CHUNK + WINDOW, axis=1)
        vw = lax.dynamic_slice_in_dim(vwp, start, Q_CHUNK + WINDOW, axis=1)
        kw_pos = start - WINDOW + jnp.arange(Q_CHUNK + WINDOW, dtype=jnp.int32)
        return nsa_attend(qc, qrc, gc, q_pos, kc, vc, kblk, vblk, kw, vw, kw_pos)

    o = lax.map(chunk_fn, (to_chunks(q), to_chunks(q_rot), to_chunks(gates), starts))
    o = jnp.moveaxis(o, 0, 1).reshape(b, t, Q_W)
    y = jnp.einsum('bte,ed->btd', o, w_o)
    kv_cmp = jnp.stack([k_c, v_c], axis=2)
    kv_sel = jnp.stack([k_s, v_s], axis=2)
    kv_win = jnp.stack([kwp[:, -WINDOW:], vwp[:, -WINDOW:]], axis=2)
    return y, kv_cmp, kv_sel, kv_win


def nsa_sample(xn, kv_cmp_pages, kv_sel_pages, kv_win_buf, page_table, w_in, w_o, cmp_w1, cmp_w2, cmp_pe):
    b, t = xn.shape[0], xn.shape[1]
    past = page_table.shape[1] * PAGE_SIZE
    pos = past + jnp.arange(t, dtype=jnp.int32)
    q, q_rot, gates, k_c, v_c, k_s, v_s, k_w, v_w = nsa_project(xn, w_in, pos)

    def gather_past(pages):
        return pages[page_table].reshape(b, past, 2, N_KV_HEADS, HEAD_DIM)

    cmp_new = jnp.stack([k_c, v_c], axis=2)
    cmp_all = jnp.concatenate([gather_past(kv_cmp_pages), cmp_new], axis=1)
    kc = compress(cmp_all[:, :, 0], cmp_w1[0], cmp_w2[0], cmp_pe[0])
    vc = compress(cmp_all[:, :, 1], cmp_w1[1], cmp_w2[1], cmp_pe[1])
    sel_new = jnp.stack([k_s, v_s], axis=2)
    sel_all = jnp.concatenate([gather_past(kv_sel_pages), sel_new], axis=1)
    l = past + t
    nsel = -(-l // SEL_BLOCK)
    sel_all = jnp.pad(sel_all, ((0, 0), (0, nsel * SEL_BLOCK - l), (0, 0), (0, 0), (0, 0)))
    sel_blk = sel_all.reshape(b, nsel, SEL_BLOCK, 2, N_KV_HEADS, HEAD_DIM)
    win_all = jnp.concatenate([kv_win_buf, jnp.stack([k_w, v_w], axis=2)], axis=1)
    kw_pos = past - WINDOW + jnp.arange(WINDOW + t, dtype=jnp.int32)
    o = nsa_attend(q, q_rot, gates, pos, kc, vc, sel_blk[:, :, :, 0], sel_blk[:, :, :, 1],
                   win_all[:, :, 0], win_all[:, :, 1], kw_pos)
    y = jnp.einsum('bte,ed->btd', o.reshape(b, t, Q_W), w_o)
    return y, cmp_new, sel_new, win_all[:, -WINDOW:]


def s5_mix(xn, h0, a_re, a_im, log_dt, b_re, b_im, c_re, c_im, d_skip, w_glu, b_glu):
    b, t = xn.shape[0], xn.shape[1]
    f32 = jnp.float32
    u = xn.astype(f32).reshape(b, t, SSM_GROUPS, SSM_GROUP)
    a = lax.complex(a_re.astype(f32), a_im.astype(f32))
    dt = jnp.exp(log_dt.astype(f32))[:, None]
    a_bar = jnp.exp(a * dt)
    b_bar = ((a_bar - 1.0) / a)[:, :, None] * lax.complex(b_re.astype(f32), b_im.astype(f32))
    c = lax.complex(c_re.astype(f32), c_im.astype(f32))
    bu = jnp.einsum('gnc,btgc->btgn', b_bar, u.astype(jnp.complex64))
    h_init = lax.complex(h0[:, 0].astype(f32), h0[:, 1].astype(f32))
    bu = bu.at[:, 0].add(a_bar[None] * h_init)
    a_seq = jnp.broadcast_to(a_bar, (1, t) + a_bar.shape)

    def combine(left, right):
        a_l, b_l = left
        a_r, b_r = right
        return a_r * a_l, a_r * b_l + b_r

    _, h = lax.associative_scan(combine, (a_seq, bu), axis=1)
    y = jnp.real(jnp.einsum('gcn,btgn->btgc', c, h)) + d_skip.astype(f32).reshape(SSM_GROUPS, SSM_GROUP) * u
    y = jax.nn.gelu(y.reshape(b, t, D_MODEL))
    out = y * jax.nn.sigmoid(y @ w_glu.astype(f32) + b_glu.astype(f32))
    h_last = h[:, -1]
    state = jnp.stack([jnp.real(h_last), jnp.imag(h_last)], axis=1)
    return out.astype(xn.dtype), state


def sq_relu_mlp(x, w1, w2):
    h = jax.nn.relu(x @ w1)
    return (h * h) @ w2


def setup_inputs(seed: int = 0) -> dict:
    key = jax.random.key(seed)
    ks = jax.random.split(key, 32)
    f32 = jnp.float32
    n_pages = PAST_LEN // PAGE_SIZE
    n_used = DEC_BATCH * n_pages
    n_pool = (5 * n_used + 3) // 4

    def nrm(k, shape, s):
        return jax.random.normal(k, shape, f32) * s

    kv_page_shape = (N_NSA_LAYERS, n_pool, PAGE_SIZE, 2, N_KV_HEADS, HEAD_DIM)
    ssm_shape = (N_SSM_LAYERS, SSM_GROUPS, SSM_STATE)
    return {
        'x_prompt': nrm(ks[0], (BATCH, SEQ, D_MODEL), 1.0),
        'x_sample': nrm(ks[1], (DEC_BATCH, DEC_SEQ, D_MODEL), 1.0),
        'cache_kv_cmp': nrm(ks[2], kv_page_shape, 1.0),
        'cache_kv_sel': nrm(ks[3], kv_page_shape, 1.0),
        'cache_kv_win': nrm(ks[4], (N_NSA_LAYERS, DEC_BATCH, WINDOW, 2, N_KV_HEADS, HEAD_DIM), 1.0),
        'state_ssm': nrm(ks[5], (N_SSM_LAYERS, DEC_BATCH, 2, SSM_GROUPS, SSM_STATE), 0.1),
        'page_table': jax.random.permutation(ks[6], n_pool)[:n_used].reshape(DEC_BATCH, n_pages).astype(jnp.int32),
        'norm_g': 1.0 + nrm(ks[7], (DEPTH, 4, D_MODEL), 0.02),
        'mlp_w1': nrm(ks[8], (DEPTH, D_MODEL, D_FF), D_MODEL ** -0.5),
        'mlp_w2': nrm(ks[9], (DEPTH, D_FF, D_MODEL), D_FF ** -0.5),
        'nsa_w_in': nrm(ks[10], (N_NSA_LAYERS, D_MODEL, IN_W), D_MODEL ** -0.5),
        'nsa_w_o': nrm(ks[11], (N_NSA_LAYERS, Q_W, D_MODEL), Q_W ** -0.5),
        'nsa_cmp_w1': nrm(ks[12], (N_NSA_LAYERS, 2, CMP_BLOCK * HEAD_DIM, HEAD_DIM), (CMP_BLOCK * HEAD_DIM) ** -0.5),
        'nsa_cmp_w2': nrm(ks[13], (N_NSA_LAYERS, 2, HEAD_DIM, HEAD_DIM), HEAD_DIM ** -0.5),
        'nsa_cmp_pe': nrm(ks[14], (N_NSA_LAYERS, 2, CMP_BLOCK, HEAD_DIM), 0.1),
        's5_a_re': -0.5 + nrm(ks[15], ssm_shape, 0.01),
        's5_a_im': math.pi * jnp.arange(SSM_STATE, dtype=f32) + nrm(ks[16], ssm_shape, 0.01),
        's5_log_dt': jax.random.uniform(ks[17], (N_SSM_LAYERS, SSM_GROUPS), f32, math.log(1e-3), math.log(1e-1)),
        's5_b_re': nrm(ks[18], (N_SSM_LAYERS, SSM_GROUPS, SSM_STATE, SSM_GROUP), (2 * SSM_GROUP) ** -0.5),
        's5_b_im': nrm(ks[19], (N_SSM_LAYERS, SSM_GROUPS, SSM_STATE, SSM_GROUP), (2 * SSM_GROUP) ** -0.5),
        's5_c_re': nrm(ks[20], (N_SSM_LAYERS, SSM_GROUPS, SSM_GROUP, SSM_STATE), 0.5),
        's5_c_im': nrm(ks[21], (N_SSM_LAYERS, SSM_GROUPS, SSM_GROUP, SSM_STATE), 0.5),
        's5_d': nrm(ks[22], (N_SSM_LAYERS, D_MODEL), 1.0),
        's5_w_glu': nrm(ks[23], (N_SSM_LAYERS, D_MODEL, D_MODEL), D_MODEL ** -0.5),
        's5_b_glu': nrm(ks[24], (N_SSM_LAYERS, D_MODEL), 0.01),
    }


def reference(x_prompt, x_sample, cache_kv_cmp, cache_kv_sel, cache_kv_win, state_ssm, page_table,
              norm_g, mlp_w1, mlp_w2, nsa_w_in, nsa_w_o, nsa_cmp_w1, nsa_cmp_w2, nsa_cmp_pe,
              s5_a_re, s5_a_im, s5_log_dt, s5_b_re, s5_b_im, s5_c_re, s5_c_im, s5_d, s5_w_glu, s5_b_glu):
    hp, hs = x_prompt, x_sample
    cmp_p, cmp_s, sel_p, sel_s, win_p, win_s, ssm_p, ssm_s = [], [], [], [], [], [], [], []
    for i in range(DEPTH):
        li = i // N_MIXERS
        g = norm_g[i]
        xp, xs = rms_norm(hp, g[0]), rms_norm(hs, g[0])
        if i % N_MIXERS == 0:
            yp, kc_p, ks_p, kw_p = nsa_prompt(xp, nsa_w_in[li], nsa_w_o[li], nsa_cmp_w1[li], nsa_cmp_w2[li], nsa_cmp_pe[li])
            ys, kc_s, ks_s, kw_s = nsa_sample(xs, cache_kv_cmp[li], cache_kv_sel[li], cache_kv_win[li], page_table,
                                              nsa_w_in[li], nsa_w_o[li], nsa_cmp_w1[li], nsa_cmp_w2[li], nsa_cmp_pe[li])
            cmp_p.append(kc_p)
            cmp_s.append(kc_s)
            sel_p.append(ks_p)
            sel_s.append(ks_s)
            win_p.append(kw_p)
            win_s.append(kw_s)
        else:
            h0p = jnp.zeros((hp.shape[0], 2, SSM_GROUPS, SSM_STATE), jnp.float32)
            yp, st_p = s5_mix(xp, h0p, s5_a_re[li], s5_a_im[li], s5_log_dt[li], s5_b_re[li], s5_b_im[li],
                              s5_c_re[li], s5_c_im[li], s5_d[li], s5_w_glu[li], s5_b_glu[li])
            ys, st_s = s5_mix(xs, state_ssm[li], s5_a_re[li], s5_a_im[li], s5_log_dt[li], s5_b_re[li], s5_b_im[li],
                              s5_c_re[li], s5_c_im[li], s5_d[li], s5_w_glu[li], s5_b_glu[li])
            ssm_p.append(st_p)
            ssm_s.append(st_s)
        hp = hp + rms_norm(yp, g[1])
        hs = hs + rms_norm(ys, g[1])
        hp = hp + rms_norm(sq_relu_mlp(rms_norm(hp, g[2]), mlp_w1[i], mlp_w2[i]), g[3])
        hs = hs + rms_norm(sq_relu_mlp(rms_norm(hs, g[2]), mlp_w1[i], mlp_w2[i]), g[3])
    return (hp, hs, jnp.stack(cmp_p), jnp.stack(cmp_s), jnp.stack(sel_p), jnp.stack(sel_s),
            jnp.stack(win_p), jnp.stack(win_s), jnp.stack(ssm_p), jnp.stack(ssm_s))
```

```python
import functools

import jax
import jax.numpy as jnp
from jax import lax
from jax.experimental import pallas as pl
from jax.experimental.pallas import tpu as pltpu

N_HEADS = 16
HEAD_DIM = 64
N_KV_HEADS = 4
GROUP = N_HEADS // N_KV_HEADS
CMP_BLOCK = 32
CMP_STRIDE = 16
SEL_BLOCK = 64
TOP_N = 16
WINDOW = 512
ROPE_THETA = 10000.0
PAGE_SIZE = 128
SSM_GROUP = 16
SSM_STATE = 64
SSM_CHUNK = 16
EPS = 1e-6
NEG = -1e30
FORCE = 1e9
MASKED = -3.0e38
Q_W = N_HEADS * HEAD_DIM
KV_W = N_KV_HEADS * HEAD_DIM
LANES = 128
SUBLANES = 8
PAGE_GROUP = 16
VMEM_LIMIT = 56 * 1024 * 1024

F32 = jnp.float32
BF16 = jnp.bfloat16


def _params(*sem):
    return pltpu.CompilerParams(dimension_semantics=sem, vmem_limit_bytes=VMEM_LIMIT)


def _full(shape):
    zeros = (0,) * len(shape)
    return pl.BlockSpec(shape, lambda *_: zeros)


def _rms(x, g):
    ms = jnp.mean(x * x, axis=-1, keepdims=True)
    return x * lax.rsqrt(ms + EPS) * g


def _dot(a, b):
    return jnp.dot(a, b, preferred_element_type=F32)


def _dot_f32lhs(w, x):
    hi = x.astype(BF16)
    r1 = x - hi.astype(F32)
    mid = r1.astype(BF16)
    lo = (r1 - mid.astype(F32)).astype(BF16)
    return _dot(w, hi) + _dot(w, mid) + _dot(w, lo)


def _dot_f32rhs(x, w):
    hi = x.astype(BF16)
    r1 = x - hi.astype(F32)
    mid = r1.astype(BF16)
    lo = (r1 - mid.astype(F32)).astype(BF16)
    return _dot(hi, w) + _dot(mid, w) + _dot(lo, w)


def _rope_nat(x, cos, sin):
    half = HEAD_DIM // 2
    lane = lax.broadcasted_iota(jnp.int32, (1, LANES), 1)
    first = (lane % HEAD_DIM) < half
    outs = []
    for c in range(x.shape[1] // LANES):
        xc = x[:, c * LANES:(c + 1) * LANES]
        rot = jnp.where(first, pltpu.roll(xc, LANES - half, 1), pltpu.roll(xc, half, 1))
        outs.append(xc * cos + rot * sin)
    return jnp.concatenate(outs, axis=1)


def _inproj_kernel(x_ref, g_ref, w_ref, wg_ref, cos_ref, sin_ref, *outs, transposed):
    xb = _rms(x_ref[...], g_ref[...]).astype(BF16)
    cos = cos_ref[...]
    sin = sin_ref[...]
    scale = HEAD_DIM ** -0.5
    q = _dot(xb, w_ref[:, 0:Q_W])
    qr = _rope_nat(q, cos, sin)
    kv = _dot(xb, w_ref[:, Q_W:Q_W + 6 * KV_W])
    gates = jax.nn.sigmoid(_dot(xb, wg_ref[...]))
    k_s = _rope_nat(kv[:, 2 * KV_W:3 * KV_W], cos, sin)
    v_s = kv[:, 3 * KV_W:4 * KV_W]
    k_w = _rope_nat(kv[:, 4 * KV_W:5 * KV_W], cos, sin)
    v_w = kv[:, 5 * KV_W:6 * KV_W]
    if transposed:
        qT_ref, qrT_ref, gT_ref, kvc_ref, kvs_ref, kvw_ref, ksb_ref, kwb_ref, vsT_ref, vwT_ref = outs
        qT_ref[...] = (q * scale).T.astype(BF16)
        qrT_ref[...] = (qr * scale).T.astype(BF16)
        gT_ref[...] = gates.T
        for h in range(N_KV_HEADS):
            ksb_ref[h] = k_s[:, h * HEAD_DIM:(h + 1) * HEAD_DIM].astype(BF16)
            kwb_ref[h] = k_w[:, h * HEAD_DIM:(h + 1) * HEAD_DIM].astype(BF16)
        vsT_ref[...] = v_s.T.astype(BF16)
        vwT_ref[...] = v_w.T.astype(BF16)
    else:
        q_ref, qr_ref, gt_ref, kvc_ref, kvs_ref, kvw_ref = outs
        q_ref[...] = (q * scale).astype(BF16)
        qr_ref[...] = (qr * scale).astype(BF16)
        gt_ref[...] = gates
    kvc_ref[...] = kv[:, 0:2 * KV_W]
    kvs_ref[:, 0:KV_W] = k_s
    kvs_ref[:, KV_W:2 * KV_W] = v_s
    kvw_ref[:, 0:KV_W] = k_w
    kvw_ref[:, KV_W:2 * KV_W] = v_w


def _inproj(x, g, w_main, w_gate, cos_t, sin_t, *, tm, pos_blocks, transposed):
    rows, d = x.shape
    n = rows // tm
    row_blk = lambda w: pl.BlockSpec((tm, w), lambda i: (i, 0))
    col_blk = lambda h: pl.BlockSpec((h, tm), lambda i: (0, i))
    tab = pl.BlockSpec((tm, LANES), lambda i: (i % pos_blocks, 0))
    kv_shapes = [jax.ShapeDtypeStruct((rows, 2 * KV_W), F32)] * 3
    kv_specs = [row_blk(2 * KV_W)] * 3
    if transposed:
        kh = pl.BlockSpec((N_KV_HEADS, tm, HEAD_DIM), lambda i: (0, i, 0))
        out_shape = ([jax.ShapeDtypeStruct((Q_W, rows), BF16)] * 2 + [jax.ShapeDtypeStruct((LANES, rows), F32)]
                     + kv_shapes + [jax.ShapeDtypeStruct((N_KV_HEADS, rows, HEAD_DIM), BF16)] * 2
                     + [jax.ShapeDtypeStruct((KV_W, rows), BF16)] * 2)
        out_specs = [col_blk(Q_W)] * 2 + [col_blk(LANES)] + kv_specs + [kh] * 2 + [col_blk(KV_W)] * 2
    else:
        out_shape = ([jax.ShapeDtypeStruct((rows, Q_W), BF16)] * 2 + [jax.ShapeDtypeStruct((rows, LANES), F32)]
                     + kv_shapes)
        out_specs = [row_blk(Q_W)] * 2 + [row_blk(LANES)] + kv_specs
    return pl.pallas_call(
        functools.partial(_inproj_kernel, transposed=transposed),
        grid=(n,),
        in_specs=[row_blk(d), _full((1, d)), _full(w_main.shape), _full(w_gate.shape), tab, tab],
        out_specs=out_specs,
        out_shape=out_shape,
        compiler_params=_params("parallel"),
        name="nsa_inproj",
    )(x, g, w_main, w_gate, cos_t, sin_t)


_KV_CHUNKS = 2 * KV_W // LANES


def _compress_half(load_rows, kv, nrows, wbd_ref, pe_ref):
    accs = []
    for r in range(CMP_BLOCK // CMP_STRIDE):
        acc = jnp.zeros((nrows, KV_W), F32)
        for s in range(CMP_STRIDE):
            idx = (kv * 2 + r) * CMP_STRIDE + s
            lhs = (load_rows(s, kv) + pe_ref[idx:idx + 1, :]).astype(BF16)
            acc = acc + _dot(lhs, wbd_ref[idx])
        accs.append(acc)
    return accs


def _compress_prompt_kernel(*refs, nsb):
    x_refs = refs[:_KV_CHUNKS]
    wbd_ref, pe_ref, w2_ref, kc_ref, vcT_ref, sh_ref = refs[_KV_CHUNKS:]

    def load_rows(s, kv):
        per_half = _KV_CHUNKS // 2
        return jnp.concatenate([x_refs[kv * per_half + c][0, pl.ds(s, nsb, stride=CMP_STRIDE), :]
                                for c in range(per_half)], axis=1)

    sh_ref[nsb:nsb + SUBLANES, :] = jnp.zeros((SUBLANES, KV_W), F32)
    for kv in range(2):
        pr0, pr1 = _compress_half(load_rows, kv, nsb, wbd_ref, pe_ref)
        sh_ref[0:nsb, :] = pr1
        h = pr0 + sh_ref[pl.ds(1, nsb), :]
        out = _dot(jax.nn.gelu(h).astype(BF16), w2_ref[kv])
        if kv == 0:
            for hh in range(N_KV_HEADS):
                kc_ref[0, hh] = out[:, hh * HEAD_DIM:(hh + 1) * HEAD_DIM].astype(BF16)
        else:
            vcT_ref[0] = out.T.astype(BF16)


def _compress_prompt(kvc3, wbd, pe_t, w2bd):
    b, t, _ = kvc3.shape
    nsb = t // CMP_STRIDE
    return pl.pallas_call(
        functools.partial(_compress_prompt_kernel, nsb=nsb),
        grid=(b,),
        in_specs=[pl.BlockSpec((1, t, LANES), lambda i, c=c: (i, 0, c)) for c in range(_KV_CHUNKS)]
        + [_full(wbd.shape), _full(pe_t.shape), _full(w2bd.shape)],
        out_specs=[pl.BlockSpec((1, N_KV_HEADS, nsb, HEAD_DIM), lambda i: (i, 0, 0, 0)),
                   pl.BlockSpec((1, KV_W, nsb), lambda i: (i, 0, 0))],
        out_shape=[jax.ShapeDtypeStruct((b, N_KV_HEADS, nsb, HEAD_DIM), BF16),
                   jax.ShapeDtypeStruct((b, KV_W, nsb), BF16)],
        scratch_shapes=[pltpu.VMEM((nsb + SUBLANES, KV_W), F32)],
        compiler_params=_params("parallel"),
        name="nsa_compress_prompt",
    )(*([kvc3] * _KV_CHUNKS), wbd, pe_t, w2bd)


def _topk_mask_T(imp, n_top):
    ns, w = imp.shape
    nblk = ns // SUBLANES
    blocks = [imp[r * SUBLANES:(r + 1) * SUBLANES, :] for r in range(nblk)]
    cnts = [jnp.zeros((SUBLANES, w), F32) for _ in range(nblk)]
    sub = lax.broadcasted_iota(jnp.int32, (SUBLANES, w), 0)
    for sp in range(ns):
        row = blocks[sp // SUBLANES][sp % SUBLANES:sp % SUBLANES + 1, :]
        for r in range(nblk):
            blk = blocks[r]
            if sp < r * SUBLANES:
                beats = jnp.where(row >= blk, 1.0, 0.0)
            elif sp >= (r + 1) * SUBLANES:
                beats = jnp.where(row > blk, 1.0, 0.0)
            else:
                beats = jnp.where(sub > (sp - r * SUBLANES), jnp.where(row >= blk, 1.0, 0.0),
                                  jnp.where(row > blk, 1.0, 0.0))
            cnts[r] = cnts[r] + beats
    return jnp.concatenate([jnp.where(c < n_top, 1.0, 0.0) for c in cnts], axis=0)


def _online_chunk(carry, k_c, vT_c, qT, mask, tq):
    m, l, acc = carry
    s = _dot(k_c, qT)
    s = jnp.concatenate([jnp.where(mask, s[:, g * tq:(g + 1) * tq], MASKED) for g in range(GROUP)], axis=1)
    m_new = jnp.maximum(m, jnp.max(s, axis=0, keepdims=True))
    alpha = jnp.exp(m - m_new)
    p = jnp.exp(s - m_new)
    l = alpha * l + jnp.sum(p, axis=0, keepdims=True)
    acc = alpha * acc + _dot(vT_c, p.astype(BF16))
    return m_new, l, acc


def _softmax_init(w):
    return (jnp.full((1, w), NEG, F32), jnp.zeros((1, w), F32), jnp.zeros((HEAD_DIM, w), F32))


def _softmax_finish(carry):
    _, l, acc = carry
    return acc * (1.0 / jnp.maximum(l, 1e-20))


def _attn_prompt_kernel(qT_ref, qrT_ref, gT_ref, kc_ref, vcT_ref, ks_ref, vsT_ref, kw_ref, vwT_ref, ovT_ref,
                        o_ref, sel_ref, *, tq, tk, ck, nc, n_top):
    t0 = pl.program_id(1) * tq
    qpos = t0 + lax.broadcasted_iota(jnp.int32, (1, tq), 1)
    ncp = kc_ref.shape[2]
    ns = ovT_ref.shape[0]
    w = GROUP * tq
    blocks_per_tile = tk // SEL_BLOCK
    blocks_per_chunk = ck // SEL_BLOCK
    sel_shift = SEL_BLOCK.bit_length() - 1
    for kvh in range(N_KV_HEADS):
        heads = [kvh * GROUP + g for g in range(GROUP)]
        rows = pl.ds(kvh * HEAD_DIM, HEAD_DIM)
        qcT = jnp.concatenate([qT_ref[h * HEAD_DIM:(h + 1) * HEAD_DIM, :] for h in heads], axis=1)
        qrT = jnp.concatenate([qrT_ref[h * HEAD_DIM:(h + 1) * HEAD_DIM, :] for h in heads], axis=1)

        s = _dot(kc_ref[0, kvh], qcT)
        cidx = lax.broadcasted_iota(jnp.int32, (ncp, 1), 0)
        valid = ((cidx * CMP_STRIDE + (CMP_BLOCK - 1)) <= qpos) & (cidx < nc)
        probs = []
        for g in range(GROUP):
            sm = jnp.where(valid, s[:, g * tq:(g + 1) * tq], NEG)
            mx = jnp.max(sm, axis=0, keepdims=True)
            e = jnp.where(valid, jnp.exp(sm - mx), 0.0)
            den = jnp.maximum(jnp.sum(e, axis=0, keepdims=True), 1e-20)
            probs.append(e / den)
        ocT = _dot(vcT_ref[0, rows, :], jnp.concatenate(probs, axis=1).astype(BF16))

        psum = probs[0]
        for g in range(1, GROUP):
            psum = psum + probs[g]
        imp = _dot_f32lhs(ovT_ref[...], psum)
        sidx = lax.broadcasted_iota(jnp.int32, (ns, 1), 0)
        cur = lax.shift_right_logical(qpos, sel_shift)
        forced = (sidx == 0) | (sidx == cur) | (sidx == cur - 1)
        causal = (sidx * SEL_BLOCK) <= qpos
        imp = jnp.where(forced, FORCE, jnp.where(causal, imp, NEG))
        sel_ref[...] = _topk_mask_T(imp, n_top)

        def tile_body(j, carry, kvh=kvh, qrT=qrT, rows=rows):
            k0 = pl.multiple_of(j * tk, tk)
            selrows = sel_ref[pl.ds(pl.multiple_of(j * blocks_per_tile, blocks_per_tile), blocks_per_tile), :]
            for c in range(tk // ck):
                kst = k0 + c * ck
                kpos = kst + lax.broadcasted_iota(jnp.int32, (ck, 1), 0)
                selm = jnp.concatenate(
                    [jnp.broadcast_to(selrows[c * blocks_per_chunk + b:c * blocks_per_chunk + b + 1, :],
                                      (SEL_BLOCK, tq)) for b in range(blocks_per_chunk)], axis=0)
                mask = (selm > 0.5) & (kpos <= qpos)
                carry = _online_chunk(carry, ks_ref[kvh, pl.ds(kst, ck), :], vsT_ref[rows, pl.ds(kst, ck)],
                                      qrT, mask, tq)
            return carry

        n_tiles = (t0 + tq + tk - 1) // tk
        osT = _softmax_finish(lax.fori_loop(0, n_tiles, tile_body, _softmax_init(w)))

        ws = jnp.maximum(t0 - WINDOW, 0)
        carry = _softmax_init(w)
        for c in range((WINDOW + tq) // ck):
            kst = pl.multiple_of(ws + c * ck, ck)
            dist = qpos - (kst + lax.broadcasted_iota(jnp.int32, (ck, 1), 0))
            mask = (dist >= 0) & (dist <= WINDOW)
            carry = _online_chunk(carry, kw_ref[kvh, pl.ds(kst, ck), :], vwT_ref[rows, pl.ds(kst, ck)], qrT,
                                  mask, tq)
        owT = _softmax_finish(carry)

        def gate_row(j):
            return jnp.concatenate([gT_ref[h * 3 + j:h * 3 + j + 1, :] for h in heads], axis=1)

        oT = gate_row(0) * ocT + gate_row(1) * osT + gate_row(2) * owT
        for g, h in enumerate(heads):
            o_ref[h * HEAD_DIM:(h + 1) * HEAD_DIM, :] = oT[:, g * tq:(g + 1) * tq].astype(BF16)


def _attn_prompt(qT, qrT, gT, kc, vcT, ksb, vsT, kwb, vwT, ovT, *, batch, seq, tq, tk, ck, nc, n_top):
    nq = seq // tq
    nsb = kc.shape[2]
    col = lambda h: pl.BlockSpec((h, tq), lambda b, i: (0, b * nq + i))
    kh = pl.BlockSpec((N_KV_HEADS, seq, HEAD_DIM), lambda b, i: (0, b, 0))
    vt = pl.BlockSpec((KV_W, seq), lambda b, i: (0, b))
    return pl.pallas_call(
        functools.partial(_attn_prompt_kernel, tq=tq, tk=tk, ck=ck, nc=nc, n_top=n_top),
        grid=(batch, nq),
        in_specs=[col(Q_W), col(Q_W), col(LANES),
                  pl.BlockSpec((1, N_KV_HEADS, nsb, HEAD_DIM), lambda b, i: (b, 0, 0, 0)),
                  pl.BlockSpec((1, KV_W, nsb), lambda b, i: (b, 0, 0)),
                  kh, vt, kh, vt, _full(ovT.shape)],
        out_specs=col(Q_W),
        out_shape=jax.ShapeDtypeStruct((Q_W, batch * seq), BF16),
        scratch_shapes=[pltpu.VMEM(ovT.shape[:1] + (tq,), F32)],
        compiler_params=_params("parallel", "arbitrary"),
        name="nsa_attn_prompt",
    )(qT, qrT, gT, kc, vcT, ksb, vsT, kwb, vwT, ovT)


def _oproj_kernel(o_ref, w_ref, x_ref, g1_ref, g2_ref, h_ref, xm_ref, *, transposed):
    if transposed:
        y = lax.dot_general(o_ref[...], w_ref[...], (((0,), (0,)), ((), ())), preferred_element_type=F32)
    else:
        y = _dot(o_ref[...], w_ref[...])
    h = x_ref[...] + _rms(y, g1_ref[...])
    h_ref[...] = h
    xm_ref[...] = _rms(h, g2_ref[...]).astype(BF16)


def _oproj(o, w_o, x, g1, g2, *, tm, transposed):
    rows, d = x.shape
    o_spec = (pl.BlockSpec((Q_W, tm), lambda i: (0, i)) if transposed else pl.BlockSpec((tm, Q_W), lambda i: (i, 0)))
    row = pl.BlockSpec((tm, d), lambda i: (i, 0))
    return pl.pallas_call(
        functools.partial(_oproj_kernel, transposed=transposed),
        grid=(rows // tm,),
        in_specs=[o_spec, _full(w_o.shape), row, _full((1, d)), _full((1, d))],
        out_specs=[row, row],
        out_shape=[jax.ShapeDtypeStruct((rows, d), F32), jax.ShapeDtypeStruct((rows, d), BF16)],
        compiler_params=_params("parallel"),
        name="nsa_oproj",
    )(o, w_o, x, g1, g2)


def _mlp_kernel(xm_ref, h_ref, w1_ref, w2_ref, g3_ref, gn_ref, *rest, ff_chunk, next_norm):
    if next_norm:
        h2_ref, xn_ref, acc_ref = rest
    else:
        h2_ref, acc_ref = rest
    xm = xm_ref[...]
    for c in range(w1_ref.shape[1] // ff_chunk):
        cols = slice(c * ff_chunk, (c + 1) * ff_chunk)
        hm = jnp.maximum(_dot(xm, w1_ref[:, cols]), 0.0)
        part = _dot((hm * hm).astype(BF16), w2_ref[cols, :])
        if c == 0:
            acc_ref[...] = part
        else:
            acc_ref[...] += part
    h2 = h_ref[...] + _rms(acc_ref[...], g3_ref[...])
    h2_ref[...] = h2
    if next_norm:
        xn_ref[...] = _rms(h2, gn_ref[...])


def _mlp(xm, h, w1, w2, g3, gn, *, tm, ff_chunk, next_norm):
    rows, d = h.shape
    row = pl.BlockSpec((tm, d), lambda i: (i, 0))
    out_shape = [jax.ShapeDtypeStruct((rows, d), F32)] * (2 if next_norm else 1)
    return pl.pallas_call(
        functools.partial(_mlp_kernel, ff_chunk=ff_chunk, next_norm=next_norm),
        grid=(rows // tm,),
        in_specs=[row, row, _full(w1.shape), _full(w2.shape), _full((1, d)), _full((1, d))],
        out_specs=[row] * len(out_shape),
        out_shape=out_shape,
        scratch_shapes=[pltpu.VMEM((tm, d), F32)],
        compiler_params=_params("parallel"),
        name="sq_relu_mlp",
    )(xm, h, w1, w2, g3, gn)


def _s5_chunk_kernel(u_ref, w_ref, pre_ref, pim_ref, qre_ref, qim_ref, are_ref, aim_ref, y_ref, hfin_ref,
                     sre, sim, hre, him, *, pp, nb, nk):
    steps_per_iter = SUBLANES // nb
    for p in range(pp):
        u = u_ref[p]
        sre[p] = _dot(u, pre_ref[p])
        sim[p] = _dot(u, pim_ref[p])

    def body(it, carry):
        r0 = pl.multiple_of(it * SUBLANES, SUBLANES)
        new = []
        for p in range(pp):
            hr, hi = carry[p]
            ar = are_ref[p]
            ai = aim_ref[p]
            sr8 = sre[p, pl.ds(r0, SUBLANES), :]
            si8 = sim[p, pl.ds(r0, SUBLANES), :]
            prev_r, prev_i = [], []
            for j in range(steps_per_iter):
                prev_r.append(hr)
                prev_i.append(hi)
                sr = sr8[j * nb:(j + 1) * nb, :]
                si = si8[j * nb:(j + 1) * nb, :]
                hr, hi = ar * hr - ai * hi + sr, ar * hi + ai * hr + si
            hre[p, pl.ds(r0, SUBLANES), :] = jnp.concatenate(prev_r, axis=0)
            him[p, pl.ds(r0, SUBLANES), :] = jnp.concatenate(prev_i, axis=0)
            new.append((hr, hi))
        return tuple(new)

    zero = jnp.zeros((nb, LANES), F32)
    fin = lax.fori_loop(0, nk // steps_per_iter, body, tuple((zero, zero) for _ in range(pp)))
    for p in range(pp):
        hfin_ref[p, 0] = fin[p][0]
        hfin_ref[p, 1] = fin[p][1]
        y_ref[p] = (_dot(u_ref[p], w_ref[p]) + _dot(hre[p].astype(BF16), qre_ref[p])
                    + _dot(him[p].astype(BF16), qim_ref[p]))


def _s5_chunk(u2, ops, *, nb, nk, pp):
    npair, nch, width = u2.shape
    blk = lambda a: pl.BlockSpec((pp,) + a.shape[1:], lambda i: (i,) + (0,) * (a.ndim - 1))
    args = (u2, ops["w"], ops["p_re"], ops["p_im"], ops["q_re"], ops["q_im"], ops["a16_re"], ops["a16_im"])
    return pl.pallas_call(
        functools.partial(_s5_chunk_kernel, pp=pp, nb=nb, nk=nk),
        grid=(npair // pp,),
        in_specs=[blk(a) for a in args],
        out_specs=[pl.BlockSpec((pp, nch, width), lambda i: (i, 0, 0)),
                   pl.BlockSpec((pp, 2, nb, LANES), lambda i: (i, 0, 0, 0))],
        out_shape=[jax.ShapeDtypeStruct((npair, nch, width), F32),
                   jax.ShapeDtypeStruct((npair, 2, nb, LANES), F32)],
        scratch_shapes=[pltpu.VMEM((pp, nch, LANES), F32)] * 4,
        compiler_params=_params("parallel"),
        name="s5_chunk_scan",
    )(*args)


def _s5_step_kernel(u_ref, h0r_ref, h0i_ref, bre_ref, bim_ref, cre_ref, cim_ref, are_ref, aim_ref,
                    y_ref, hr_ref, hi_ref, *, npair):
    for p in range(npair):
        u = u_ref[p]
        ar = are_ref[p]
        ai = aim_ref[p]
        h0r = h0r_ref[p]
        h0i = h0i_ref[p]
        hr = ar * h0r - ai * h0i + _dot(u, bre_ref[p])
        hi = ar * h0i + ai * h0r + _dot(u, bim_ref[p])
        hr_ref[p] = hr
        hi_ref[p] = hi
        y_ref[p] = _dot(hr.astype(BF16), cre_ref[p]) + _dot(hi.astype(BF16), cim_ref[p])


def _s5_step(u2, h0r, h0i, ops):
    npair, rows, width = u2.shape
    args = (u2, h0r, h0i, ops["b1_re"], ops["b1_im"], ops["c1_re"], ops["c1_im"], ops["a1_re"], ops["a1_im"])
    return pl.pallas_call(
        functools.partial(_s5_step_kernel, npair=npair),
        grid=(1,),
        in_specs=[_full(a.shape) for a in args],
        out_specs=[_full((npair, rows, width)), _full(h0r.shape), _full(h0r.shape)],
        out_shape=[jax.ShapeDtypeStruct((npair, rows, width), F32), jax.ShapeDtypeStruct(h0r.shape, F32),
                   jax.ShapeDtypeStruct(h0r.shape, F32)],
        compiler_params=_params("arbitrary"),
        name="s5_single_step",
    )(*args)


def _s5_out_kernel(yc_ref, u_ref, d_ref, wg_ref, bg_ref, h_ref, g1_ref, g2_ref, h3_ref, xm_ref):
    y = jax.nn.gelu(yc_ref[...] + d_ref[...] * u_ref[...])
    out = y * jax.nn.sigmoid(_dot(y.astype(BF16), wg_ref[...]) + bg_ref[...])
    h3 = h_ref[...] + _rms(out, g1_ref[...])
    h3_ref[...] = h3
    xm_ref[...] = _rms(h3, g2_ref[...]).astype(BF16)


def _s5_out(yc, u, d_skip, w_glu, b_glu, h, g1, g2, *, tm):
    rows, d = h.shape
    row = pl.BlockSpec((tm, d), lambda i: (i, 0))
    vec = _full((1, d))
    return pl.pallas_call(
        _s5_out_kernel,
        grid=(rows // tm,),
        in_specs=[row, row, vec, _full(w_glu.shape), vec, row, vec, vec],
        out_specs=[row, row],
        out_shape=[jax.ShapeDtypeStruct((rows, d), F32), jax.ShapeDtypeStruct((rows, d), BF16)],
        compiler_params=_params("parallel"),
        name="s5_glu_out",
    )(yc, u, d_skip, w_glu, b_glu, h, g1, g2)


def _s5_operators(a_re, a_im, log_dt, b_re, b_im, c_re, c_im):
    hp = lax.Precision.HIGHEST
    g, n = a_re.shape
    npair = g // 2
    L = SSM_CHUNK
    a = lax.complex(a_re.astype(F32), a_im.astype(F32))
    dt = jnp.exp(log_dt.astype(F32))[:, None]
    a_bar = jnp.exp(a * dt)
    b_bar = ((a_bar - 1.0) / a)[:, :, None] * lax.complex(b_re.astype(F32), b_im.astype(F32))
    c = lax.complex(c_re.astype(F32), c_im.astype(F32))
    pows = [jnp.ones_like(a_bar)]
    for _ in range(L):
        pows.append(pows[-1] * a_bar)
    a_pow = jnp.stack(pows)
    kern = jnp.real(jnp.einsum("gcn,tgn,gnd->gtcd", c, a_pow[:L], b_bar, precision=hp))
    t1 = jnp.arange(L)[:, None]
    t2 = jnp.arange(L)[None, :]
    lag = jnp.clip(t2 - t1, 0, L - 1)
    wg = jnp.where((t2 >= t1)[None, :, :, None, None], kern[:, lag], 0.0)
    eye2 = jnp.eye(2, dtype=F32)
    wg = wg.reshape(npair, 2, L, L, SSM_GROUP, SSM_GROUP)
    w = jnp.einsum("pgtscd,gh->ptgdshc", wg, eye2).reshape(npair, L * 2 * SSM_GROUP, L * 2 * SSM_GROUP)
    pc = a_pow[L - 1 - jnp.arange(L)][:, :, :, None] * b_bar[None]
    pc = pc.reshape(L, npair, 2, n, SSM_GROUP)

    def p_mat(x):
        return jnp.einsum("tpgnd,gh->ptgdhn", x, eye2).reshape(npair, L * 2 * SSM_GROUP, 2 * n)

    qc = c[None] * a_pow[1:L + 1][:, :, None, :]
    qc = qc.reshape(L, npair, 2, SSM_GROUP, n)

    def q_mat(x):
        return jnp.einsum("spgcn,gh->pgnshc", x, eye2).reshape(npair, 2 * n, L * 2 * SSM_GROUP)

    def pair_row(x):
        return x.reshape(npair, 1, 2 * n)

    b1 = b_bar.reshape(npair, 2, n, SSM_GROUP)
    c1 = c.reshape(npair, 2, SSM_GROUP, n)

    def b1_mat(x):
        return jnp.einsum("pgnd,gh->pgdhn", x, eye2).reshape(npair, 2 * SSM_GROUP, 2 * n)

    def c1_mat(x):
        return jnp.einsum("pgcn,gh->pgnhc", x, eye2).reshape(npair, 2 * n, 2 * SSM_GROUP)

    return {
        "w": w.astype(BF16),
        "p_re": p_mat(jnp.real(pc)).astype(BF16), "p_im": p_mat(jnp.imag(pc)).astype(BF16),
        "q_re": q_mat(jnp.real(qc)).astype(BF16), "q_im": q_mat(-jnp.imag(qc)).astype(BF16),
        "a16_re": pair_row(jnp.real(a_pow[L])), "a16_im": pair_row(jnp.imag(a_pow[L])),
        "a1_re": pair_row(jnp.real(a_bar)), "a1_im": pair_row(jnp.imag(a_bar)),
        "b1_re": b1_mat(jnp.real(b_bar.reshape(npair, 2, n, SSM_GROUP))).astype(BF16),
        "b1_im": b1_mat(jnp.imag(b_bar.reshape(npair, 2, n, SSM_GROUP))).astype(BF16),
        "c1_re": c1_mat(jnp.real(c1)).astype(BF16), "c1_im": c1_mat(-jnp.imag(c1)).astype(BF16),
    }


def _compress_paged_kernel(pt_ref, *refs, pg, nsb):
    del pt_ref
    chunks = refs[:pg * _KV_CHUNKS]
    wbd_ref, pe_ref, w2_ref, out_ref, h0_ref, h1_ref = refs[pg * _KV_CHUNKS:]
    j = pl.program_id(1)
    sbp = PAGE_SIZE // CMP_STRIDE
    nrows = pg * sbp
    per_half = _KV_CHUNKS // 2

    def load_rows(s, kv):
        return jnp.concatenate(
            [jnp.concatenate([chunks[i * _KV_CHUNKS + kv * per_half + c][0, pl.ds(s, sbp, stride=CMP_STRIDE), :]
                              for c in range(per_half)], axis=1) for i in range(pg)], axis=0)

    @pl.when(j == 0)
    def _():
        h1_ref[:, nsb:nsb + SUBLANES, :] = jnp.zeros((2, SUBLANES, KV_W), F32)

    r0 = pl.multiple_of(j * nrows, nrows)
    for kv in range(2):
        pr0, pr1 = _compress_half(load_rows, kv, nrows, wbd_ref, pe_ref)
        h0_ref[kv, pl.ds(r0, nrows), :] = pr0
        h1_ref[kv, pl.ds(r0, nrows), :] = pr1

    @pl.when(j == pl.num_programs(1) - 1)
    def _():
        for kv in range(2):
            h = h0_ref[kv] + h1_ref[kv, pl.ds(1, nsb), :]
            out = _dot(jax.nn.gelu(h).astype(BF16), w2_ref[kv])
            out_ref[0, :, kv * KV_W:(kv + 1) * KV_W] = out.astype(BF16)


def _page_specs(pg, pages_per_sample):
    def spec(i):
        return pl.BlockSpec((1, PAGE_SIZE, 2 * KV_W), lambda b, j, pt: (pt[b * pages_per_sample + j * pg + i], 0, 0))
    return [spec(i) for i in range(pg)]


def _compress_paged(pages, pt_flat, wbd, pe_t, w2bd, *, nb, pages_per_sample, pg):
    nsb = pages_per_sample * PAGE_SIZE // CMP_STRIDE
    c3 = lambda shape: pl.BlockSpec(shape, lambda b, j, pt: (0,) * len(shape))

    def chunk_spec(i, c):
        return pl.BlockSpec((1, PAGE_SIZE, LANES),
                            lambda b, j, pt: (pt[b * pages_per_sample + j * pg + i], 0, c))

    grid_spec = pltpu.PrefetchScalarGridSpec(
        num_scalar_prefetch=1,
        grid=(nb, pages_per_sample // pg),
        in_specs=[chunk_spec(i, c) for i in range(pg) for c in range(_KV_CHUNKS)]
        + [c3(wbd.shape), c3(pe_t.shape), c3(w2bd.shape)],
        out_specs=pl.BlockSpec((1, nsb, 2 * KV_W), lambda b, j, pt: (b, 0, 0)),
        scratch_shapes=[pltpu.VMEM((2, nsb, KV_W), F32), pltpu.VMEM((2, nsb + SUBLANES, KV_W), F32)],
    )
    return pl.pallas_call(
        functools.partial(_compress_paged_kernel, pg=pg, nsb=nsb),
        grid_spec=grid_spec,
        out_shape=jax.ShapeDtypeStruct((nb, nsb, 2 * KV_W), BF16),
        compiler_params=_params("parallel", "arbitrary"),
        name="nsa_compress_paged",
    )(pt_flat, *([pages] * (pg * _KV_CHUNKS)), wbd, pe_t, w2bd)


def _topk_mask_lanes(imp, n_top, ns_valid):
    lane = lax.broadcasted_iota(jnp.int32, imp.shape, 1)
    cnt = jnp.zeros(imp.shape, F32)
    for sp in range(ns_valid):
        col = imp[:, sp:sp + 1]
        cnt = cnt + jnp.where(lane > sp, jnp.where(col >= imp, 1.0, 0.0), jnp.where(col > imp, 1.0, 0.0))
    return jnp.where((cnt < n_top) & (lane < ns_valid), 1.0, 0.0)


def _dot_nt(a, b):
    return lax.dot_general(a, b, (((1,), (1,)), ((), ())), preferred_element_type=F32)


def _attn_sample_kernel(pt_ref, *refs, pg, past, nc, ns_valid, n_top):
    del pt_ref
    pages = refs[:pg]
    (q_ref, qr_ref, g_ref, kcvc_ref, ksn_ref, kwn_ref, win_ref, ov_ref, e_ref, gs_ref, o_ref,
     m_ref, l_ref, acc_ref, sel_ref, oc_ref, ow_ref) = refs[pg:]
    j = pl.program_id(1)
    ncp = kcvc_ref.shape[1]
    nsp = ov_ref.shape[1]
    wlen = win_ref.shape[1]
    row = lax.broadcasted_iota(jnp.int32, (N_HEADS, KV_W), 0)
    lane = lax.broadcasted_iota(jnp.int32, (N_HEADS, KV_W), 1)
    own = (lane // HEAD_DIM) == (row // GROUP)

    def spread(ref):
        q = ref[0]
        return jnp.where(own, jnp.concatenate([q] * N_KV_HEADS, axis=1), jnp.zeros((N_HEADS, KV_W), BF16))

    def update(state, s, vs):
        m, l, acc = state
        m_new = jnp.maximum(m, jnp.max(s, axis=1, keepdims=True))
        alpha = jnp.exp(m - m_new)
        p = jnp.exp(s - m_new)
        l = alpha * l + jnp.sum(p, axis=1, keepdims=True)
        pv = None
        for st, sz, v in vs:
            t = _dot(p[:, st:st + sz].astype(BF16), v)
            pv = t if pv is None else pv + t
        return m_new, l, alpha * acc + pv

    def init():
        return (jnp.full((N_HEADS, 1), NEG, F32), jnp.zeros((N_HEADS, 1), F32), jnp.zeros((N_HEADS, KV_W), F32))

    def new_row_update(state, qbd, new_ref):
        r8 = lax.broadcasted_iota(jnp.int32, (SUBLANES, 2 * KV_W), 0)
        tile = jnp.where(r8 == 0, jnp.broadcast_to(new_ref[0], (SUBLANES, 2 * KV_W)), 0.0).astype(BF16)
        s = _dot_nt(qbd, tile[:, 0:KV_W])
        l8 = lax.broadcasted_iota(jnp.int32, (N_HEADS, SUBLANES), 1)
        s = jnp.where(l8 == 0, s, MASKED)
        return update(state, s, [(0, SUBLANES, tile[:, KV_W:2 * KV_W])])

    def put(state):
        m, l, acc = state
        m_ref[...] = jnp.broadcast_to(m, m_ref.shape)
        l_ref[...] = jnp.broadcast_to(l, l_ref.shape)
        acc_ref[...] = acc

    qrbd = spread(qr_ref)

    @pl.when(j == 0)
    def _():
        s = _dot_nt(spread(q_ref), kcvc_ref[0, :, 0:KV_W])
        cidx = lax.broadcasted_iota(jnp.int32, (1, ncp), 1)
        valid = ((cidx * CMP_STRIDE + (CMP_BLOCK - 1)) <= past) & (cidx < nc)
        sm = jnp.where(valid, s, NEG)
        mx = jnp.max(sm, axis=1, keepdims=True)
        e = jnp.where(valid, jnp.exp(sm - mx), 0.0)
        p = e / jnp.maximum(jnp.sum(e, axis=1, keepdims=True), 1e-20)
        oc_ref[...] = _dot(p.astype(BF16), kcvc_ref[0, :, KV_W:2 * KV_W])
        imp = _dot_f32rhs(_dot_f32lhs(gs_ref[...], p), ov_ref[...])
        sidx = lax.broadcasted_iota(jnp.int32, (1, nsp), 1)
        cur = past // SEL_BLOCK
        forced = (sidx == 0) | (sidx == cur) | (sidx == cur - 1)
        causal = (sidx * SEL_BLOCK) <= past
        imp = jnp.where(forced, FORCE, jnp.where(causal, imp, NEG))
        imp = jnp.where(sidx < ns_valid, imp, MASKED)
        sel_ref[...] = _topk_mask_lanes(imp, n_top, ns_valid)
        kw = win_ref[0, :, 0:KV_W].astype(BF16)
        vw = win_ref[0, :, KV_W:2 * KV_W].astype(BF16)
        sw = _dot_nt(qrbd, kw)
        wpos = past - wlen + lax.broadcasted_iota(jnp.int32, (1, wlen), 1)
        sw = jnp.where((wpos >= 0) & (past - wpos <= WINDOW), sw, MASKED)
        st = update(init(), sw, [(0, wlen, vw)])
        st = new_row_update(st, qrbd, kwn_ref)
        ow_ref[...] = st[2] * (1.0 / jnp.maximum(st[1], 1e-20))
        put(new_row_update(init(), qrbd, ksn_ref))

    kts = [r[0, :, 0:KV_W].astype(BF16) for r in pages]
    s = jnp.concatenate([_dot_nt(qrbd, kt) for kt in kts], axis=1)
    mask = _dot(sel_ref[...].astype(BF16), e_ref[0]) > 0.5
    s = jnp.where(mask, s, MASKED)
    vs = [(i * PAGE_SIZE, PAGE_SIZE, r[0, :, KV_W:2 * KV_W].astype(BF16)) for i, r in enumerate(pages)]
    put(update((m_ref[:, 0:1], l_ref[:, 0:1], acc_ref[...]), s, vs))

    @pl.when(j == pl.num_programs(1) - 1)
    def _():
        g = g_ref[0]
        os_ = acc_ref[...] * (1.0 / jnp.maximum(l_ref[:, 0:1], 1e-20))
        o = g[:, 0:1] * oc_ref[...] + g[:, 1:2] * os_ + g[:, 2:3] * ow_ref[...]
        o = jnp.where(own, o, 0.0)
        out = o[:, 0:HEAD_DIM]
        for h in range(1, N_KV_HEADS):
            out = out + o[:, h * HEAD_DIM:(h + 1) * HEAD_DIM]
        o_ref[0] = out.astype(BF16)


def _attn_sample(pages, pt_flat, q3, qr3, g3, kcvc, ks_new, kw_new, win, ov, e_mat, gsum, *, nb, pages_per_sample,
                 pg, past, nc, ns_valid, n_top):
    nsp = ov.shape[1]
    per_b = lambda shape: pl.BlockSpec((1,) + shape, lambda b, j, pt: (b,) + (0,) * len(shape))
    const = lambda shape: pl.BlockSpec(shape, lambda b, j, pt: (0,) * len(shape))
    grid_spec = pltpu.PrefetchScalarGridSpec(
        num_scalar_prefetch=1,
        grid=(nb, pages_per_sample // pg),
        in_specs=_page_specs(pg, pages_per_sample) + [
            per_b((N_HEADS, HEAD_DIM)), per_b((N_HEADS, HEAD_DIM)), per_b((N_HEADS, 3)),
            per_b(kcvc.shape[1:]), per_b((1, 2 * KV_W)), per_b((1, 2 * KV_W)), per_b(win.shape[1:]),
            const(ov.shape), pl.BlockSpec((1,) + e_mat.shape[1:], lambda b, j, pt: (j, 0, 0)), const(gsum.shape)],
        out_specs=per_b((N_HEADS, HEAD_DIM)),
        scratch_shapes=[pltpu.VMEM((N_HEADS, LANES), F32), pltpu.VMEM((N_HEADS, LANES), F32),
                        pltpu.VMEM((N_HEADS, KV_W), F32), pltpu.VMEM((N_HEADS, nsp), F32),
                        pltpu.VMEM((N_HEADS, KV_W), F32), pltpu.VMEM((N_HEADS, KV_W), F32)],
    )
    return pl.pallas_call(
        functools.partial(_attn_sample_kernel, pg=pg, past=past, nc=nc, ns_valid=ns_valid, n_top=n_top),
        grid_spec=grid_spec,
        out_shape=jax.ShapeDtypeStruct((nb, N_HEADS, HEAD_DIM), BF16),
        compiler_params=_params("parallel", "arbitrary"),
        name="nsa_attn_sample",
    )(pt_flat, *([pages] * pg), q3, qr3, g3, kcvc, ks_new, kw_new, win, ov, e_mat, gsum)


def _rope_tables(pos):
    half = HEAD_DIM // 2
    inv = ROPE_THETA ** (-jnp.arange(half, dtype=F32) / half)
    ang = pos.astype(F32)[:, None] * inv[None, :]
    cos = jnp.cos(ang)
    sin = jnp.sin(ang)
    reps = LANES // HEAD_DIM
    return (jnp.tile(jnp.concatenate([cos, cos], axis=1), (1, reps)),
            jnp.tile(jnp.concatenate([-sin, sin], axis=1), (1, reps)))


def _compress_weights(cmp_w1, cmp_w2, cmp_pe):
    ratio = CMP_BLOCK // CMP_STRIDE
    eye = jnp.eye(N_KV_HEADS, dtype=F32)
    w1r = cmp_w1.reshape(2, ratio, CMP_STRIDE, HEAD_DIM, HEAD_DIM)
    wbd = jnp.einsum("krsde,hg->krshdge", w1r, eye).reshape(2 * ratio * CMP_STRIDE, KV_W, KV_W).astype(BF16)
    w2bd = jnp.einsum("kef,hg->khegf", cmp_w2, eye).reshape(2, KV_W, KV_W).astype(BF16)
    pe_t = jnp.tile(cmp_pe.reshape(2 * ratio * CMP_STRIDE, HEAD_DIM), (1, N_KV_HEADS)).astype(F32)
    return wbd, pe_t, w2bd


def _overlap(nc, ncp, nsel, nsp):
    c_start = jnp.arange(ncp, dtype=jnp.int32)[:, None] * CMP_STRIDE
    s_start = jnp.arange(nsp, dtype=jnp.int32)[None, :] * SEL_BLOCK
    ov = (c_start < s_start + SEL_BLOCK) & (c_start + CMP_BLOCK > s_start)
    ov = ov & (jnp.arange(ncp)[:, None] < nc) & (jnp.arange(nsp)[None, :] < nsel)
    return ov.astype(BF16)


def _pick(*cands):
    return next(c for c in cands if c)


def _tile(n, pref):
    t = min(n, pref)
    while n % t:
        t //= 2
    return t


def kernel(x_prompt, x_sample, cache_kv_cmp, cache_kv_sel, cache_kv_win, state_ssm, page_table, norm_g, mlp_w1,
           mlp_w2, nsa_w_in, nsa_w_o, nsa_cmp_w1, nsa_cmp_w2, nsa_cmp_pe, s5_a_re, s5_a_im, s5_log_dt, s5_b_re,
           s5_b_im, s5_c_re, s5_c_im, s5_d, s5_w_glu, s5_b_glu):
    b, t, d = x_prompt.shape
    nb = x_sample.shape[0]
    pages_per_sample = page_table.shape[1]
    past = pages_per_sample * PAGE_SIZE
    rows_p = b * t
    g = norm_g.reshape(norm_g.shape[0], 4, 1, d)

    w_in = nsa_w_in[0]
    w_main = w_in[:, :Q_W + 6 * KV_W].astype(BF16)
    w_gate = jnp.pad(w_in[:, Q_W + 6 * KV_W:], ((0, 0), (0, LANES - 3 * N_HEADS))).astype(BF16)
    w_o = nsa_w_o[0].astype(BF16)
    wbd, pe_t, w2bd = _compress_weights(nsa_cmp_w1[0], nsa_cmp_w2[0], nsa_cmp_pe[0])
    w1 = mlp_w1.astype(BF16)
    w2 = mlp_w2.astype(BF16)
    w_glu = s5_w_glu[0].astype(BF16)
    ops = _s5_operators(s5_a_re[0], s5_a_im[0], s5_log_dt[0], s5_b_re[0], s5_b_im[0], s5_c_re[0], s5_c_im[0])
    d_skip = s5_d[0].reshape(1, d)
    b_glu = s5_b_glu[0].reshape(1, d)

    tm = _tile(rows_p, 512)
    ff_chunk = _tile(mlp_w1.shape[2], 1024)

    xp = x_prompt.reshape(rows_p, d)
    cos_p, sin_p = _rope_tables(jnp.arange(t, dtype=jnp.int32))
    (qT, qrT, gT, kvc, kvs, kvw, ksb, kwb, vsT, vwT) = _inproj(
        xp, g[0, 0], w_main, w_gate, cos_p, sin_p, tm=_tile(t, 512), pos_blocks=t // _tile(t, 512), transposed=True)
    nsb_p = t // CMP_STRIDE
    nc_p = nsb_p - CMP_BLOCK // CMP_STRIDE + 1
    nsel_p = t // SEL_BLOCK
    kc, vcT = _compress_prompt(kvc.reshape(b, t, 2 * KV_W), wbd, pe_t, w2bd)
    ovT = _overlap(nc_p, nsb_p, nsel_p, nsel_p).T
    tk = _tile(t, 512)
    oT = _attn_prompt(qT, qrT, gT, kc, vcT, ksb, vsT, kwb, vwT, ovT, batch=b, seq=t, tq=LANES, tk=tk, ck=LANES,
                      nc=nc_p, n_top=min(TOP_N, nsel_p))
    hp, xm = _oproj(oT, w_o, xp, g[0, 1], g[0, 2], tm=tm, transposed=True)
    hp, xn1 = _mlp(xm, hp, w1[0], w2[0], g[0, 3], g[1, 0], tm=tm, ff_chunk=ff_chunk, next_norm=True)

    npair = d // (2 * SSM_GROUP)
    nk = t // SSM_CHUNK
    pw = 2 * SSM_GROUP
    u2 = xn1.astype(BF16).reshape(b, nk, SSM_CHUNK, npair, pw).transpose(3, 1, 0, 2, 4)
    u2 = u2.reshape(npair, nk * b, SSM_CHUNK * pw)
    y2, hfin = _s5_chunk(u2, ops, nb=b, nk=nk, pp=2)
    yc = y2.reshape(npair, nk, b, SSM_CHUNK, pw).transpose(2, 1, 3, 0, 4).reshape(rows_p, d)
    hp, xm = _s5_out(yc, xn1, d_skip, w_glu, b_glu, hp, g[1, 1], g[1, 2], tm=tm)
    (hp,) = _mlp(xm, hp, w1[1], w2[1], g[1, 3], g[1, 3], tm=tm, ff_chunk=ff_chunk, next_norm=False)
    ssm_p = hfin.reshape(npair, 2, b, 2, SSM_STATE).transpose(2, 1, 0, 3, 4).reshape(b, 2, d // SSM_GROUP, SSM_STATE)

    xs = x_sample.reshape(nb, d)
    cos_s, sin_s = _rope_tables(jnp.full((nb,), past, dtype=jnp.int32))
    q_s, qr_s, gates_s, kvc_s, kvs_s, kvw_s = _inproj(
        xs, g[0, 0], w_main, w_gate, cos_s, sin_s, tm=nb, pos_blocks=1, transposed=False)
    pt_flat = page_table.reshape(-1).astype(jnp.int32)
    pg = _tile(pages_per_sample, PAGE_GROUP)
    n_pool = cache_kv_cmp.shape[1]
    cmp_pages = cache_kv_cmp[0].reshape(n_pool, PAGE_SIZE, 2 * KV_W)
    sel_pages = cache_kv_sel[0].reshape(n_pool, PAGE_SIZE, 2 * KV_W)
    kcvc = _compress_paged(cmp_pages, pt_flat, wbd, pe_t, w2bd, nb=nb, pages_per_sample=pages_per_sample, pg=pg)
    l_all = past + 1
    nsb_s = l_all // CMP_STRIDE
    nc_s = nsb_s - CMP_BLOCK // CMP_STRIDE + 1
    nsel_s = -(-l_all // SEL_BLOCK)
    nsp = -(-nsel_s // LANES) * LANES
    ov_s = _overlap(nc_s, past // CMP_STRIDE, nsel_s, nsp)
    keys_per_step = pg * PAGE_SIZE
    key_blk = (jnp.arange(past, dtype=jnp.int32) // SEL_BLOCK).reshape(past // keys_per_step, 1, keys_per_step)
    e_mat = (jnp.arange(nsp, dtype=jnp.int32)[None, :, None] == key_blk).astype(BF16)
    hh = jnp.arange(N_HEADS)
    gsum = ((hh[:, None] // GROUP) == (hh[None, :] // GROUP)).astype(BF16)
    win = cache_kv_win[0].reshape(nb, WINDOW, 2 * KV_W)
    o_s = _attn_sample(sel_pages, pt_flat, q_s.reshape(nb, N_HEADS, HEAD_DIM), qr_s.reshape(nb, N_HEADS, HEAD_DIM),
                       gates_s[:, :3 * N_HEADS].reshape(nb, N_HEADS, 3), kcvc, kvs_s.reshape(nb, 1, 2 * KV_W),
                       kvw_s.reshape(nb, 1, 2 * KV_W), win, ov_s, e_mat, gsum, nb=nb,
                       pages_per_sample=pages_per_sample, pg=pg, past=past, nc=nc_s, ns_valid=nsel_s,
                       n_top=min(TOP_N, nsel_s))
    hs, xm_s = _oproj(o_s.reshape(nb, Q_W), w_o, xs, g[0, 1], g[0, 2], tm=nb, transposed=False)
    hs, xn1_s = _mlp(xm_s, hs, w1[0], w2[0], g[0, 3], g[1, 0], tm=nb, ff_chunk=ff_chunk, next_norm=True)

    u2_s = xn1_s.astype(BF16).reshape(nb, npair, pw).transpose(1, 0, 2)
    st = state_ssm[0].reshape(nb, 2, npair, 2 * SSM_STATE).transpose(1, 2, 0, 3)
    y2_s, hr_s, hi_s = _s5_step(u2_s, st[0], st[1], ops)
    yc_s = y2_s.transpose(1, 0, 2).reshape(nb, d)
    hs, xm_s = _s5_out(yc_s, xn1_s, d_skip, w_glu, b_glu, hs, g[1, 1], g[1, 2], tm=nb)
    (hs,) = _mlp(xm_s, hs, w1[1], w2[1], g[1, 3], g[1, 3], tm=nb, ff_chunk=ff_chunk, next_norm=False)
    ssm_s = jnp.stack([hr_s, hi_s], axis=0).transpose(2, 0, 1, 3).reshape(nb, 2, d // SSM_GROUP, SSM_STATE)

    kv5 = lambda a, n, s: a.reshape(1, n, s, 2, N_KV_HEADS, HEAD_DIM)
    win_p = kvw.reshape(b, t, 2 * KV_W)[:, t - WINDOW:]
    win_s = jnp.concatenate([win[:, 1:], kvw_s.reshape(nb, 1, 2 * KV_W)], axis=1)
    return (hp.reshape(b, t, d), hs.reshape(nb, 1, d),
            kv5(kvc, b, t), kv5(kvc_s, nb, 1), kv5(kvs, b, t), kv5(kvs_s, nb, 1),
            kv5(win_p, b, WINDOW), kv5(win_s, nb, WINDOW), ssm_p[None], ssm_s[None])
```

```python
import functools

import jax
import jax.numpy as jnp
from jax import lax
from jax.experimental import pallas as pl
from jax.experimental.pallas import tpu as pltpu

N_HEADS = 16
HEAD_DIM = 64
N_KV_HEADS = 4
GROUP = N_HEADS // N_KV_HEADS
CMP_BLOCK = 32
CMP_STRIDE = 16
SEL_BLOCK = 64
TOP_N = 16
WINDOW = 512
ROPE_THETA = 10000.0
PAGE_SIZE = 128
SSM_GROUP = 16
SSM_STATE = 64
SSM_CHUNK = 16
EPS = 1e-6
NEG = -1e30
FORCE = 1e9
MASKED = -3.0e38
Q_W = N_HEADS * HEAD_DIM
KV_W = N_KV_HEADS * HEAD_DIM
LANES = 128
SUBLANES = 8
PAGE_GROUP = 16
VMEM_LIMIT = 56 * 1024 * 1024

F32 = jnp.float32
BF16 = jnp.bfloat16


def _params(*sem):
    return pltpu.CompilerParams(dimension_semantics=sem, vmem_limit_bytes=VMEM_LIMIT)


def _full(shape):
    zeros = (0,) * len(shape)
    return pl.BlockSpec(shape, lambda *_: zeros)


def _rms(x, g):
    ms = jnp.mean(x * x, axis=-1, keepdims=True)
    return x * lax.rsqrt(ms + EPS) * g


def _dot(a, b):
    return jnp.dot(a, b, preferred_element_type=F32)


def _dot_f32lhs(w, x):
    hi = x.astype(BF16)
    r1 = x - hi.astype(F32)
    mid = r1.astype(BF16)
    lo = (r1 - mid.astype(F32)).astype(BF16)
    return _dot(w, hi) + _dot(w, mid) + _dot(w, lo)


def _dot_f32rhs(x, w):
    hi = x.astype(BF16)
    r1 = x - hi.astype(F32)
    mid = r1.astype(BF16)
    lo = (r1 - mid.astype(F32)).astype(BF16)
    return _dot(hi, w) + _dot(mid, w) + _dot(lo, w)


def _rope_nat(x, cos, sin):
    half = HEAD_DIM // 2
    lane = lax.broadcasted_iota(jnp.int32, (1, LANES), 1)
    first = (lane % HEAD_DIM) < half
    outs = []
    for c in range(x.shape[1] // LANES):
        xc = x[:, c * LANES:(c + 1) * LANES]
        rot = jnp.where(first, pltpu.roll(xc, LANES - half, 1), pltpu.roll(xc, half, 1))
        outs.append(xc * cos + rot * sin)
    return jnp.concatenate(outs, axis=1)


def _inproj_kernel(x_ref, g_ref, w_ref, wg_ref, cos_ref, sin_ref, *outs, transposed):
    xb = _rms(x_ref[...], g_ref[...]).astype(BF16)
    cos = cos_ref[...]
    sin = sin_ref[...]
    scale = HEAD_DIM ** -0.5
    q = _dot(xb, w_ref[:, 0:Q_W])
    qr = _rope_nat(q, cos, sin)
    kv = _dot(xb, w_ref[:, Q_W:Q_W + 6 * KV_W])
    gates = jax.nn.sigmoid(_dot(xb, wg_ref[...]))
    k_s = _rope_nat(kv[:, 2 * KV_W:3 * KV_W], cos, sin)
    v_s = kv[:, 3 * KV_W:4 * KV_W]
    k_w = _rope_nat(kv[:, 4 * KV_W:5 * KV_W], cos, sin)
    v_w = kv[:, 5 * KV_W:6 * KV_W]
    kvc_ref = outs[3]
    kvc_ref[...] = kv[:, 0:2 * KV_W]
    if transposed:
        qT_ref, qrT_ref, gT_ref, _, kvcT_ref, kvsT_ref, kvwT_ref, ksb_ref, kwb_ref, vsT_ref, vwT_ref = outs
        qT_ref[...] = (q * scale).T.astype(BF16)
        qrT_ref[...] = (qr * scale).T.astype(BF16)
        gT_ref[...] = gates.T
        for h in range(N_KV_HEADS):
            ksb_ref[h] = k_s[:, h * HEAD_DIM:(h + 1) * HEAD_DIM].astype(BF16)
            kwb_ref[h] = k_w[:, h * HEAD_DIM:(h + 1) * HEAD_DIM].astype(BF16)
        v_sT = v_s.T
        v_wT = v_w.T
        vsT_ref[...] = v_sT.astype(BF16)
        vwT_ref[...] = v_wT.astype(BF16)
        kvcT_ref[0] = kv[:, 0:2 * KV_W].T
        kvsT_ref[0, 0:KV_W, :] = k_s.T
        kvsT_ref[0, KV_W:2 * KV_W, :] = v_sT
        kvwT_ref[0, 0:KV_W, :] = k_w.T
        kvwT_ref[0, KV_W:2 * KV_W, :] = v_wT
    else:
        q_ref, qr_ref, gt_ref, _, kvs_ref, kvw_ref = outs
        q_ref[...] = (q * scale).astype(BF16)
        qr_ref[...] = (qr * scale).astype(BF16)
        gt_ref[...] = gates
        kvs_ref[:, 0:KV_W] = k_s
        kvs_ref[:, KV_W:2 * KV_W] = v_s
        kvw_ref[:, 0:KV_W] = k_w
        kvw_ref[:, KV_W:2 * KV_W] = v_w


def _inproj(x, g, w_main, w_gate, cos_t, sin_t, *, tm, pos_blocks, transposed):
    rows, d = x.shape
    n = rows // tm
    row_blk = lambda w: pl.BlockSpec((tm, w), lambda i: (i, 0))
    col_blk = lambda h: pl.BlockSpec((h, tm), lambda i: (0, i))
    tab = pl.BlockSpec((tm, LANES), lambda i: (i % pos_blocks, 0))
    kv_nat = jax.ShapeDtypeStruct((rows, 2 * KV_W), F32)
    if transposed:
        seqs = rows // (pos_blocks * tm)
        kh = pl.BlockSpec((N_KV_HEADS, tm, HEAD_DIM), lambda i: (0, i, 0))
        kvT = pl.BlockSpec((1, 2 * KV_W, tm), lambda i: (i // pos_blocks, 0, i % pos_blocks))
        out_shape = ([jax.ShapeDtypeStruct((Q_W, rows), BF16)] * 2 + [jax.ShapeDtypeStruct((LANES, rows), F32)]
                     + [kv_nat] + [jax.ShapeDtypeStruct((seqs, 2 * KV_W, pos_blocks * tm), F32)] * 3
                     + [jax.ShapeDtypeStruct((N_KV_HEADS, rows, HEAD_DIM), BF16)] * 2
                     + [jax.ShapeDtypeStruct((KV_W, rows), BF16)] * 2)
        out_specs = ([col_blk(Q_W)] * 2 + [col_blk(LANES)] + [row_blk(2 * KV_W)] + [kvT] * 3 + [kh] * 2
                     + [col_blk(KV_W)] * 2)
    else:
        out_shape = ([jax.ShapeDtypeStruct((rows, Q_W), BF16)] * 2 + [jax.ShapeDtypeStruct((rows, LANES), F32)]
                     + [kv_nat] * 3)
        out_specs = [row_blk(Q_W)] * 2 + [row_blk(LANES)] + [row_blk(2 * KV_W)] * 3
    return pl.pallas_call(
        functools.partial(_inproj_kernel, transposed=transposed),
        grid=(n,),
        in_specs=[row_blk(d), _full((1, d)), _full(w_main.shape), _full(w_gate.shape), tab, tab],
        out_specs=out_specs,
        out_shape=out_shape,
        compiler_params=_params("parallel"),
        name="nsa_inproj",
    )(x, g, w_main, w_gate, cos_t, sin_t)


_KV_CHUNKS = 2 * KV_W // LANES


def _compress_half(load_rows, kv, nrows, wbd_ref, pe_ref):
    accs = []
    for r in range(CMP_BLOCK // CMP_STRIDE):
        acc = jnp.zeros((nrows, KV_W), F32)
        for s in range(CMP_STRIDE):
            idx = (kv * 2 + r) * CMP_STRIDE + s
            lhs = (load_rows(s, kv) + pe_ref[idx:idx + 1, :]).astype(BF16)
            acc = acc + _dot(lhs, wbd_ref[idx])
        accs.append(acc)
    return accs


def _compress_prompt_kernel(*refs, nsb):
    x_refs = refs[:_KV_CHUNKS]
    wbd_ref, pe_ref, w2_ref, kc_ref, vcT_ref, sh_ref = refs[_KV_CHUNKS:]

    def load_rows(s, kv):
        per_half = _KV_CHUNKS // 2
        return jnp.concatenate([x_refs[kv * per_half + c][0, pl.ds(s, nsb, stride=CMP_STRIDE), :]
                                for c in range(per_half)], axis=1)

    sh_ref[nsb:nsb + SUBLANES, :] = jnp.zeros((SUBLANES, KV_W), F32)
    for kv in range(2):
        pr0, pr1 = _compress_half(load_rows, kv, nsb, wbd_ref, pe_ref)
        sh_ref[0:nsb, :] = pr1
        h = pr0 + sh_ref[pl.ds(1, nsb), :]
        out = _dot(jax.nn.gelu(h).astype(BF16), w2_ref[kv])
        if kv == 0:
            for hh in range(N_KV_HEADS):
                kc_ref[0, hh] = out[:, hh * HEAD_DIM:(hh + 1) * HEAD_DIM].astype(BF16)
        else:
            vcT_ref[0] = out.T.astype(BF16)


def _compress_prompt(kvc3, wbd, pe_t, w2bd):
    b, t, _ = kvc3.shape
    nsb = t // CMP_STRIDE
    return pl.pallas_call(
        functools.partial(_compress_prompt_kernel, nsb=nsb),
        grid=(b,),
        in_specs=[pl.BlockSpec((1, t, LANES), lambda i, c=c: (i, 0, c)) for c in range(_KV_CHUNKS)]
        + [_full(wbd.shape), _full(pe_t.shape), _full(w2bd.shape)],
        out_specs=[pl.BlockSpec((1, N_KV_HEADS, nsb, HEAD_DIM), lambda i: (i, 0, 0, 0)),
                   pl.BlockSpec((1, KV_W, nsb), lambda i: (i, 0, 0))],
        out_shape=[jax.ShapeDtypeStruct((b, N_KV_HEADS, nsb, HEAD_DIM), BF16),
                   jax.ShapeDtypeStruct((b, KV_W, nsb), BF16)],
        scratch_shapes=[pltpu.VMEM((nsb + SUBLANES, KV_W), F32)],
        compiler_params=_params("parallel"),
        name="nsa_compress_prompt",
    )(*([kvc3] * _KV_CHUNKS), wbd, pe_t, w2bd)


def _topk_mask_T(imp, n_top):
    ns, w = imp.shape
    nblk = ns // SUBLANES
    blocks = [imp[r * SUBLANES:(r + 1) * SUBLANES, :] for r in range(nblk)]
    cnts = [jnp.zeros((SUBLANES, w), F32) for _ in range(nblk)]
    sub = lax.broadcasted_iota(jnp.int32, (SUBLANES, w), 0)
    for sp in range(ns):
        row = blocks[sp // SUBLANES][sp % SUBLANES:sp % SUBLANES + 1, :]
        for r in range(nblk):
            blk = blocks[r]
            if sp < r * SUBLANES:
                beats = jnp.where(row >= blk, 1.0, 0.0)
            elif sp >= (r + 1) * SUBLANES:
                beats = jnp.where(row > blk, 1.0, 0.0)
            else:
                beats = jnp.where(sub > (sp - r * SUBLANES), jnp.where(row >= blk, 1.0, 0.0),
                                  jnp.where(row > blk, 1.0, 0.0))
            cnts[r] = cnts[r] + beats
    return jnp.concatenate([jnp.where(c < n_top, 1.0, 0.0) for c in cnts], axis=0)


def _online_chunk(carry, k_c, vT_c, qT, bias):
    m, l, acc = carry
    s = _dot(k_c, qT) + bias
    m_new = jnp.maximum(m, jnp.max(s, axis=0, keepdims=True))
    alpha = jnp.exp(m - m_new)
    p = jnp.exp(s - m_new)
    l = alpha * l + jnp.sum(p, axis=0, keepdims=True)
    acc = alpha * acc + _dot(vT_c, p.astype(BF16))
    return m_new, l, acc


def _softmax_init(w):
    return (jnp.full((1, w), NEG, F32), jnp.zeros((1, w), F32), jnp.zeros((HEAD_DIM, w), F32))


def _softmax_finish(carry):
    _, l, acc = carry
    return acc * (1.0 / jnp.maximum(l, 1e-20))


def _attn_prompt_kernel(qT_ref, qrT_ref, gT_ref, kc_ref, vcT_ref, ks_ref, vsT_ref, kw_ref, vwT_ref, ovT_ref,
                        o_ref, selb_ref, oc_ref, *, tq, nc, n_top):
    ck = tq
    i = pl.program_id(1)
    t0 = i * tq
    qpos = t0 + lax.broadcasted_iota(jnp.int32, (1, tq), 1)
    ncp = kc_ref.shape[2]
    ns = ovT_ref.shape[0]
    w = GROUP * tq
    bpc = ck // SEL_BLOCK
    sel_shift = SEL_BLOCK.bit_length() - 1
    kvhs = range(N_KV_HEADS)
    heads = [[kvh * GROUP + g for g in range(GROUP)] for kvh in kvhs]
    rows = [pl.ds(kvh * HEAD_DIM, HEAD_DIM) for kvh in kvhs]
    qrT = [jnp.concatenate([qrT_ref[h * HEAD_DIM:(h + 1) * HEAD_DIM, :] for h in heads[kvh]], axis=1)
           for kvh in kvhs]
    kl = lax.broadcasted_iota(jnp.int32, (ck, tq), 0)
    ql = lax.broadcasted_iota(jnp.int32, (ck, tq), 1)
    key_le_query = jnp.where(kl <= ql, 0.0, MASKED)
    key_ge_query = jnp.where(kl >= ql, 0.0, MASKED)
    tile4 = lambda b: jnp.concatenate([b] * GROUP, axis=1)

    cidx = lax.broadcasted_iota(jnp.int32, (ncp, 1), 0)
    valid = ((cidx * CMP_STRIDE + (CMP_BLOCK - 1)) <= qpos) & (cidx < nc)
    sidx = lax.broadcasted_iota(jnp.int32, (ns, 1), 0)
    cur = lax.shift_right_logical(qpos, sel_shift)
    forced = (sidx == 0) | (sidx == cur) | (sidx == cur - 1)
    causal = (sidx * SEL_BLOCK) <= qpos
    for kvh in kvhs:
        qcT = jnp.concatenate([qT_ref[h * HEAD_DIM:(h + 1) * HEAD_DIM, :] for h in heads[kvh]], axis=1)
        s = _dot(kc_ref[0, kvh], qcT)
        probs = []
        for g in range(GROUP):
            sm = jnp.where(valid, s[:, g * tq:(g + 1) * tq], NEG)
            mx = jnp.max(sm, axis=0, keepdims=True)
            e = jnp.where(valid, jnp.exp(sm - mx), 0.0)
            den = jnp.maximum(jnp.sum(e, axis=0, keepdims=True), 1e-20)
            probs.append(e / den)
        oc_ref[kvh] = _dot(vcT_ref[0, rows[kvh], :], jnp.concatenate(probs, axis=1).astype(BF16))
        psum = probs[0]
        for g in range(1, GROUP):
            psum = psum + probs[g]
        imp = _dot_f32lhs(ovT_ref[...], psum)
        imp = jnp.where(forced, FORCE, jnp.where(causal, imp, NEG))
        selb = (_topk_mask_T(imp, n_top) - 1.0) * (-MASKED)
        for c in range(ns // bpc):
            selb_ref[kvh, c, 0:bpc, :] = selb[c * bpc:(c + 1) * bpc, :]

    n_back = WINDOW // ck
    states = [_softmax_init(w) for _ in kvhs]
    for r in range(n_back + 1):
        a = i - n_back + r
        kst = pl.multiple_of(jnp.maximum(a, 0) * ck, ck)
        if r == n_back:
            bias = key_le_query
        else:
            skip = jnp.where(a < 0, MASKED, 0.0)
            bias = (jnp.minimum(key_ge_query, skip) if r == 0 else jnp.broadcast_to(skip, (ck, tq)))
        bias = tile4(bias)
        for kvh in kvhs:
            states[kvh] = _online_chunk(states[kvh], kw_ref[kvh, pl.ds(kst, ck), :],
                                        vwT_ref[rows[kvh], pl.ds(kst, ck)], qrT[kvh], bias)
    ow = [_softmax_finish(st) for st in states]

    def sel_bias(kvh, c):
        tile = selb_ref[kvh, c]
        return jnp.concatenate([jnp.broadcast_to(tile[b:b + 1, :], (SEL_BLOCK, tq)) for b in range(bpc)], axis=0)

    def chunk_step(c, states, diagonal):
        kst = pl.multiple_of(c * ck, ck)
        out = []
        for kvh in kvhs:
            bias = sel_bias(kvh, c)
            if diagonal:
                bias = jnp.minimum(bias, key_le_query)
            out.append(_online_chunk(states[kvh], ks_ref[kvh, pl.ds(kst, ck), :],
                                     vsT_ref[rows[kvh], pl.ds(kst, ck)], qrT[kvh], tile4(bias)))
        return tuple(out)

    states = lax.fori_loop(0, i, lambda c, st: chunk_step(c, st, False), tuple(_softmax_init(w) for _ in kvhs))
    states = chunk_step(i, states, True)

    for kvh in kvhs:
        def gate_row(j, kvh=kvh):
            return jnp.concatenate([gT_ref[h * 3 + j:h * 3 + j + 1, :] for h in heads[kvh]], axis=1)

        oT = gate_row(0) * oc_ref[kvh] + gate_row(1) * _softmax_finish(states[kvh]) + gate_row(2) * ow[kvh]
        for g, h in enumerate(heads[kvh]):
            o_ref[h * HEAD_DIM:(h + 1) * HEAD_DIM, :] = oT[:, g * tq:(g + 1) * tq].astype(BF16)


def _attn_prompt(qT, qrT, gT, kc, vcT, ksb, vsT, kwb, vwT, ovT, *, batch, seq, tq, nc, n_top):
    nq = seq // tq
    nsb = kc.shape[2]
    ns = ovT.shape[0]
    col = lambda h: pl.BlockSpec((h, tq), lambda b, i: (0, b * nq + i))
    kh = pl.BlockSpec((N_KV_HEADS, seq, HEAD_DIM), lambda b, i: (0, b, 0))
    vt = pl.BlockSpec((KV_W, seq), lambda b, i: (0, b))
    return pl.pallas_call(
        functools.partial(_attn_prompt_kernel, tq=tq, nc=nc, n_top=n_top),
        grid=(batch, nq),
        in_specs=[col(Q_W), col(Q_W), col(LANES),
                  pl.BlockSpec((1, N_KV_HEADS, nsb, HEAD_DIM), lambda b, i: (b, 0, 0, 0)),
                  pl.BlockSpec((1, KV_W, nsb), lambda b, i: (b, 0, 0)),
                  kh, vt, kh, vt, _full(ovT.shape)],
        out_specs=col(Q_W),
        out_shape=jax.ShapeDtypeStruct((Q_W, batch * seq), BF16),
        scratch_shapes=[pltpu.VMEM((N_KV_HEADS, ns * SEL_BLOCK // tq, SUBLANES, tq), F32),
                        pltpu.VMEM((N_KV_HEADS, HEAD_DIM, GROUP * tq), F32)],
        compiler_params=_params("parallel", "arbitrary"),
        name="nsa_attn_prompt",
    )(qT, qrT, gT, kc, vcT, ksb, vsT, kwb, vwT, ovT)


def _oproj_kernel(o_ref, w_ref, x_ref, g1_ref, g2_ref, h_ref, xm_ref, *, transposed):
    if transposed:
        y = lax.dot_general(o_ref[...], w_ref[...], (((0,), (0,)), ((), ())), preferred_element_type=F32)
    else:
        y = _dot(o_ref[...], w_ref[...])
    h = x_ref[...] + _rms(y, g1_ref[...])
    h_ref[...] = h
    xm_ref[...] = _rms(h, g2_ref[...]).astype(BF16)


def _oproj(o, w_o, x, g1, g2, *, tm, transposed):
    rows, d = x.shape
    o_spec = (pl.BlockSpec((Q_W, tm), lambda i: (0, i)) if transposed else pl.BlockSpec((tm, Q_W), lambda i: (i, 0)))
    row = pl.BlockSpec((tm, d), lambda i: (i, 0))
    return pl.pallas_call(
        functools.partial(_oproj_kernel, transposed=transposed),
        grid=(rows // tm,),
        in_specs=[o_spec, _full(w_o.shape), row, _full((1, d)), _full((1, d))],
        out_specs=[row, row],
        out_shape=[jax.ShapeDtypeStruct((rows, d), F32), jax.ShapeDtypeStruct((rows, d), BF16)],
        compiler_params=_params("parallel"),
        name="nsa_oproj",
    )(o, w_o, x, g1, g2)


def _mlp_kernel(xm_ref, h_ref, w1_ref, w2_ref, g3_ref, gn_ref, *rest, ff_chunk, next_norm):
    if next_norm:
        h2_ref, xn_ref, acc_ref = rest
    else:
        h2_ref, acc_ref = rest
    xm = xm_ref[...]
    for c in range(w1_ref.shape[1] // ff_chunk):
        cols = slice(c * ff_chunk, (c + 1) * ff_chunk)
        hm = jnp.maximum(_dot(xm, w1_ref[:, cols]), 0.0)
        part = _dot((hm * hm).astype(BF16), w2_ref[cols, :])
        if c == 0:
            acc_ref[...] = part
        else:
            acc_ref[...] += part
    h2 = h_ref[...] + _rms(acc_ref[...], g3_ref[...])
    h2_ref[...] = h2
    if next_norm:
        xn_ref[...] = _rms(h2, gn_ref[...])


def _mlp(xm, h, w1, w2, g3, gn, *, tm, ff_chunk, next_norm):
    rows, d = h.shape
    row = pl.BlockSpec((tm, d), lambda i: (i, 0))
    out_shape = [jax.ShapeDtypeStruct((rows, d), F32)] * (2 if next_norm else 1)
    return pl.pallas_call(
        functools.partial(_mlp_kernel, ff_chunk=ff_chunk, next_norm=next_norm),
        grid=(rows // tm,),
        in_specs=[row, row, _full(w1.shape), _full(w2.shape), _full((1, d)), _full((1, d))],
        out_specs=[row] * len(out_shape),
        out_shape=out_shape,
        scratch_shapes=[pltpu.VMEM((tm, d), F32)],
        compiler_params=_params("parallel"),
        name="sq_relu_mlp",
    )(xm, h, w1, w2, g3, gn)


def _s5_chunk_kernel(u_ref, w_ref, pre_ref, pim_ref, qre_ref, qim_ref, are_ref, aim_ref, y_ref, hfin_ref,
                     sre, sim, hre, him, *, pp, nb, nk):
    steps_per_iter = SUBLANES // nb
    for p in range(pp):
        u = u_ref[p]
        sre[p] = _dot(u, pre_ref[p])
        sim[p] = _dot(u, pim_ref[p])

    def body(it, carry):
        r0 = pl.multiple_of(it * SUBLANES, SUBLANES)
        new = []
        for p in range(pp):
            hr, hi = carry[p]
            ar = are_ref[p]
            ai = aim_ref[p]
            sr8 = sre[p, pl.ds(r0, SUBLANES), :]
            si8 = sim[p, pl.ds(r0, SUBLANES), :]
            prev_r, prev_i = [], []
            for j in range(steps_per_iter):
                prev_r.append(hr)
                prev_i.append(hi)
                sr = sr8[j * nb:(j + 1) * nb, :]
                si = si8[j * nb:(j + 1) * nb, :]
                hr, hi = ar * hr - ai * hi + sr, ar * hi + ai * hr + si
            hre[p, pl.ds(r0, SUBLANES), :] = jnp.concatenate(prev_r, axis=0)
            him[p, pl.ds(r0, SUBLANES), :] = jnp.concatenate(prev_i, axis=0)
            new.append((hr, hi))
        return tuple(new)

    zero = jnp.zeros((nb, LANES), F32)
    fin = lax.fori_loop(0, nk // steps_per_iter, body, tuple((zero, zero) for _ in range(pp)))
    for p in range(pp):
        hfin_ref[p, 0] = fin[p][0]
        hfin_ref[p, 1] = fin[p][1]
        y_ref[p] = (_dot(u_ref[p], w_ref[p]) + _dot(hre[p].astype(BF16), qre_ref[p])
                    + _dot(him[p].astype(BF16), qim_ref[p]))


def _s5_chunk(u2, ops, *, nb, nk, pp):
    npair, nch, width = u2.shape
    blk = lambda a: pl.BlockSpec((pp,) + a.shape[1:], lambda i: (i,) + (0,) * (a.ndim - 1))
    args = (u2, ops["w"], ops["p_re"], ops["p_im"], ops["q_re"], ops["q_im"], ops["a16_re"], ops["a16_im"])
    return pl.pallas_call(
        functools.partial(_s5_chunk_kernel, pp=pp, nb=nb, nk=nk),
        grid=(npair // pp,),
        in_specs=[blk(a) for a in args],
        out_specs=[pl.BlockSpec((pp, nch, width), lambda i: (i, 0, 0)),
                   pl.BlockSpec((pp, 2, nb, LANES), lambda i: (i, 0, 0, 0))],
        out_shape=[jax.ShapeDtypeStruct((npair, nch, width), F32),
                   jax.ShapeDtypeStruct((npair, 2, nb, LANES), F32)],
        scratch_shapes=[pltpu.VMEM((pp, nch, LANES), F32)] * 4,
        compiler_params=_params("parallel"),
        name="s5_chunk_scan",
    )(*args)


def _s5_step_kernel(u_ref, h0r_ref, h0i_ref, bre_ref, bim_ref, cre_ref, cim_ref, are_ref, aim_ref,
                    y_ref, hr_ref, hi_ref, *, npair):
    for p in range(npair):
        u = u_ref[p]
        ar = are_ref[p]
        ai = aim_ref[p]
        h0r = h0r_ref[p]
        h0i = h0i_ref[p]
        hr = ar * h0r - ai * h0i + _dot(u, bre_ref[p])
        hi = ar * h0i + ai * h0r + _dot(u, bim_ref[p])
        hr_ref[p] = hr
        hi_ref[p] = hi
        y_ref[p] = _dot(hr.astype(BF16), cre_ref[p]) + _dot(hi.astype(BF16), cim_ref[p])


def _s5_step(u2, h0r, h0i, ops):
    npair, rows, width = u2.shape
    args = (u2, h0r, h0i, ops["b1_re"], ops["b1_im"], ops["c1_re"], ops["c1_im"], ops["a1_re"], ops["a1_im"])
    return pl.pallas_call(
        functools.partial(_s5_step_kernel, npair=npair),
        grid=(1,),
        in_specs=[_full(a.shape) for a in args],
        out_specs=[_full((npair, rows, width)), _full(h0r.shape), _full(h0r.shape)],
        out_shape=[jax.ShapeDtypeStruct((npair, rows, width), F32), jax.ShapeDtypeStruct(h0r.shape, F32),
                   jax.ShapeDtypeStruct(h0r.shape, F32)],
        compiler_params=_params("arbitrary"),
        name="s5_single_step",
    )(*args)


def _s5_out_kernel(yc_ref, u_ref, d_ref, wg_ref, bg_ref, h_ref, g1_ref, g2_ref, h3_ref, xm_ref):
    y = jax.nn.gelu(yc_ref[...] + d_ref[...] * u_ref[...])
    out = y * jax.nn.sigmoid(_dot(y.astype(BF16), wg_ref[...]) + bg_ref[...])
    h3 = h_ref[...] + _rms(out, g1_ref[...])
    h3_ref[...] = h3
    xm_ref[...] = _rms(h3, g2_ref[...]).astype(BF16)


def _s5_out(yc, u, d_skip, w_glu, b_glu, h, g1, g2, *, tm):
    rows, d = h.shape
    row = pl.BlockSpec((tm, d), lambda i: (i, 0))
    vec = _full((1, d))
    return pl.pallas_call(
        _s5_out_kernel,
        grid=(rows // tm,),
        in_specs=[row, row, vec, _full(w_glu.shape), vec, row, vec, vec],
        out_specs=[row, row],
        out_shape=[jax.ShapeDtypeStruct((rows, d), F32), jax.ShapeDtypeStruct((rows, d), BF16)],
        compiler_params=_params("parallel"),
        name="s5_glu_out",
    )(yc, u, d_skip, w_glu, b_glu, h, g1, g2)


def _s5_operators(a_re, a_im, log_dt, b_re, b_im, c_re, c_im):
    hp = lax.Precision.HIGHEST
    g, n = a_re.shape
    npair = g // 2
    L = SSM_CHUNK
    a = lax.complex(a_re.astype(F32), a_im.astype(F32))
    dt = jnp.exp(log_dt.astype(F32))[:, None]
    a_bar = jnp.exp(a * dt)
    b_bar = ((a_bar - 1.0) / a)[:, :, None] * lax.complex(b_re.astype(F32), b_im.astype(F32))
    c = lax.complex(c_re.astype(F32), c_im.astype(F32))
    pows = [jnp.ones_like(a_bar)]
    for _ in range(L):
        pows.append(pows[-1] * a_bar)
    a_pow = jnp.stack(pows)
    kern = jnp.real(jnp.einsum("gcn,tgn,gnd->gtcd", c, a_pow[:L], b_bar, precision=hp))
    t1 = jnp.arange(L)[:, None]
    t2 = jnp.arange(L)[None, :]
    lag = jnp.clip(t2 - t1, 0, L - 1)
    wg = jnp.where((t2 >= t1)[None, :, :, None, None], kern[:, lag], 0.0)
    eye2 = jnp.eye(2, dtype=F32)
    wg = wg.reshape(npair, 2, L, L, SSM_GROUP, SSM_GROUP)
    w = jnp.einsum("pgtscd,gh->ptgdshc", wg, eye2).reshape(npair, L * 2 * SSM_GROUP, L * 2 * SSM_GROUP)
    pc = a_pow[L - 1 - jnp.arange(L)][:, :, :, None] * b_bar[None]
    pc = pc.reshape(L, npair, 2, n, SSM_GROUP)

    def p_mat(x):
        return jnp.einsum("tpgnd,gh->ptgdhn", x, eye2).reshape(npair, L * 2 * SSM_GROUP, 2 * n)

    qc = c[None] * a_pow[1:L + 1][:, :, None, :]
    qc = qc.reshape(L, npair, 2, SSM_GROUP, n)

    def q_mat(x):
        return jnp.einsum("spgcn,gh->pgnshc", x, eye2).reshape(npair, 2 * n, L * 2 * SSM_GROUP)

    def pair_row(x):
        return x.reshape(npair, 1, 2 * n)

    b1 = b_bar.reshape(npair, 2, n, SSM_GROUP)
    c1 = c.reshape(npair, 2, SSM_GROUP, n)

    def b1_mat(x):
        return jnp.einsum("pgnd,gh->pgdhn", x, eye2).reshape(npair, 2 * SSM_GROUP, 2 * n)

    def c1_mat(x):
        return jnp.einsum("pgcn,gh->pgnhc", x, eye2).reshape(npair, 2 * n, 2 * SSM_GROUP)

    return {
        "w": w.astype(BF16),
        "p_re": p_mat(jnp.real(pc)).astype(BF16), "p_im": p_mat(jnp.imag(pc)).astype(BF16),
        "q_re": q_mat(jnp.real(qc)).astype(BF16), "q_im": q_mat(-jnp.imag(qc)).astype(BF16),
        "a16_re": pair_row(jnp.real(a_pow[L])), "a16_im": pair_row(jnp.imag(a_pow[L])),
        "a1_re": pair_row(jnp.real(a_bar)), "a1_im": pair_row(jnp.imag(a_bar)),
        "b1_re": b1_mat(jnp.real(b_bar.reshape(npair, 2, n, SSM_GROUP))).astype(BF16),
        "b1_im": b1_mat(jnp.imag(b_bar.reshape(npair, 2, n, SSM_GROUP))).astype(BF16),
        "c1_re": c1_mat(jnp.real(c1)).astype(BF16), "c1_im": c1_mat(-jnp.imag(c1)).astype(BF16),
    }


def _compress_paged_kernel(pt_ref, *refs, pg, nsb):
    del pt_ref
    pages = refs[:pg]
    wbd_ref, pe_ref, w2_ref, out_ref, h0_ref, h1_ref, xs_ref = refs[pg:]
    j = pl.program_id(1)
    sbp = PAGE_SIZE // CMP_STRIDE
    nrows = pg * sbp
    per_half = _KV_CHUNKS // 2

    for i, r in enumerate(pages):
        for kv in range(2):
            x = r[0, kv].T
            for c in range(per_half):
                xs_ref[kv * per_half + c, i * PAGE_SIZE:(i + 1) * PAGE_SIZE, :] = x[:, c * LANES:(c + 1) * LANES]

    def load_rows(s, kv):
        return jnp.concatenate([xs_ref[kv * per_half + c, pl.ds(s, nrows, stride=CMP_STRIDE), :]
                                for c in range(per_half)], axis=1)

    @pl.when(j == 0)
    def _():
        h1_ref[:, nsb:nsb + SUBLANES, :] = jnp.zeros((2, SUBLANES, KV_W), F32)

    r0 = pl.multiple_of(j * nrows, nrows)
    for kv in range(2):
        pr0, pr1 = _compress_half(load_rows, kv, nrows, wbd_ref, pe_ref)
        h0_ref[kv, pl.ds(r0, nrows), :] = pr0
        h1_ref[kv, pl.ds(r0, nrows), :] = pr1

    @pl.when(j == pl.num_programs(1) - 1)
    def _():
        for kv in range(2):
            h = h0_ref[kv] + h1_ref[kv, pl.ds(1, nsb), :]
            out = _dot(jax.nn.gelu(h).astype(BF16), w2_ref[kv])
            out_ref[0, :, kv * KV_W:(kv + 1) * KV_W] = out.astype(BF16)


def _page_specs(pg, pages_per_sample):
    def spec(i):
        return pl.BlockSpec((1, 2, KV_W, PAGE_SIZE),
                            lambda b, j, pt: (pt[b * pages_per_sample + j * pg + i], 0, 0, 0))
    return [spec(i) for i in range(pg)]


def _compress_paged(pages, pt_flat, wbd, pe_t, w2bd, *, nb, pages_per_sample, pg):
    nsb = pages_per_sample * PAGE_SIZE // CMP_STRIDE
    c3 = lambda shape: pl.BlockSpec(shape, lambda b, j, pt: (0,) * len(shape))
    grid_spec = pltpu.PrefetchScalarGridSpec(
        num_scalar_prefetch=1,
        grid=(nb, pages_per_sample // pg),
        in_specs=_page_specs(pg, pages_per_sample) + [c3(wbd.shape), c3(pe_t.shape), c3(w2bd.shape)],
        out_specs=pl.BlockSpec((1, nsb, 2 * KV_W), lambda b, j, pt: (b, 0, 0)),
        scratch_shapes=[pltpu.VMEM((2, nsb, KV_W), F32), pltpu.VMEM((2, nsb + SUBLANES, KV_W), F32),
                        pltpu.VMEM((_KV_CHUNKS, pg * PAGE_SIZE, LANES), F32)],
    )
    return pl.pallas_call(
        functools.partial(_compress_paged_kernel, pg=pg, nsb=nsb),
        grid_spec=grid_spec,
        out_shape=jax.ShapeDtypeStruct((nb, nsb, 2 * KV_W), BF16),
        compiler_params=_params("parallel", "arbitrary"),
        name="nsa_compress_paged",
    )(pt_flat, *([pages] * pg), wbd, pe_t, w2bd)


def _topk_mask_lanes(imp, n_top, ns_valid):
    lane = lax.broadcasted_iota(jnp.int32, imp.shape, 1)
    cnt = jnp.zeros(imp.shape, F32)
    for sp in range(ns_valid):
        col = imp[:, sp:sp + 1]
        cnt = cnt + jnp.where(lane > sp, jnp.where(col >= imp, 1.0, 0.0), jnp.where(col > imp, 1.0, 0.0))
    return jnp.where((cnt < n_top) & (lane < ns_valid), 1.0, 0.0)


def _dot_nt(a, b):
    return lax.dot_general(a, b, (((1,), (1,)), ((), ())), preferred_element_type=F32)


def _attn_sample_kernel(pt_ref, *refs, pg, past, nc, ns_valid, n_top):
    del pt_ref
    pages = refs[:pg]
    (q_ref, qr_ref, g_ref, kcvc_ref, ksn_ref, kwn_ref, win_ref, ov_ref, e_ref, gs_ref, o_ref,
     m_ref, l_ref, acc_ref, sel_ref, oc_ref, ow_ref) = refs[pg:]
    j = pl.program_id(1)
    ncp = kcvc_ref.shape[1]
    nsp = ov_ref.shape[1]
    wlen = win_ref.shape[3]
    row = lax.broadcasted_iota(jnp.int32, (N_HEADS, KV_W), 0)
    lane = lax.broadcasted_iota(jnp.int32, (N_HEADS, KV_W), 1)
    own = (lane // HEAD_DIM) == (row // GROUP)

    def spread(ref):
        q = ref[0]
        return jnp.where(own, jnp.concatenate([q] * N_KV_HEADS, axis=1), jnp.zeros((N_HEADS, KV_W), BF16))

    def update(state, s, vs):
        m, l, acc = state
        m_new = jnp.maximum(m, jnp.max(s, axis=1, keepdims=True))
        alpha = jnp.exp(m - m_new)
        p = jnp.exp(s - m_new)
        l = alpha * l + jnp.sum(p, axis=1, keepdims=True)
        pv = None
        for st, sz, v, feature_major in vs:
            pb = p[:, st:st + sz].astype(BF16)
            t = _dot_nt(pb, v) if feature_major else _dot(pb, v)
            pv = t if pv is None else pv + t
        return m_new, l, alpha * acc + pv

    def init():
        return (jnp.full((N_HEADS, 1), NEG, F32), jnp.zeros((N_HEADS, 1), F32), jnp.zeros((N_HEADS, KV_W), F32))

    def new_row_update(state, qbd, new_ref):
        r8 = lax.broadcasted_iota(jnp.int32, (SUBLANES, 2 * KV_W), 0)
        tile = jnp.where(r8 == 0, jnp.broadcast_to(new_ref[0], (SUBLANES, 2 * KV_W)), 0.0).astype(BF16)
        s = _dot_nt(qbd, tile[:, 0:KV_W])
        l8 = lax.broadcasted_iota(jnp.int32, (N_HEADS, SUBLANES), 1)
        s = jnp.where(l8 == 0, s, MASKED)
        return update(state, s, [(0, SUBLANES, tile[:, KV_W:2 * KV_W], False)])

    def put(state):
        m, l, acc = state
        m_ref[...] = jnp.broadcast_to(m, m_ref.shape)
        l_ref[...] = jnp.broadcast_to(l, l_ref.shape)
        acc_ref[...] = acc

    qrbd = spread(qr_ref)

    @pl.when(j == 0)
    def _():
        s = _dot_nt(spread(q_ref), kcvc_ref[0, :, 0:KV_W])
        cidx = lax.broadcasted_iota(jnp.int32, (1, ncp), 1)
        valid = ((cidx * CMP_STRIDE + (CMP_BLOCK - 1)) <= past) & (cidx < nc)
        sm = jnp.where(valid, s, NEG)
        mx = jnp.max(sm, axis=1, keepdims=True)
        e = jnp.where(valid, jnp.exp(sm - mx), 0.0)
        p = e / jnp.maximum(jnp.sum(e, axis=1, keepdims=True), 1e-20)
        oc_ref[...] = _dot(p.astype(BF16), kcvc_ref[0, :, KV_W:2 * KV_W])
        imp = _dot_f32rhs(_dot_f32lhs(gs_ref[...], p), ov_ref[...])
        sidx = lax.broadcasted_iota(jnp.int32, (1, nsp), 1)
        cur = past // SEL_BLOCK
        forced = (sidx == 0) | (sidx == cur) | (sidx == cur - 1)
        causal = (sidx * SEL_BLOCK) <= past
        imp = jnp.where(forced, FORCE, jnp.where(causal, imp, NEG))
        imp = jnp.where(sidx < ns_valid, imp, MASKED)
        sel_ref[...] = _topk_mask_lanes(imp, n_top, ns_valid)
        sw = _dot(qrbd, win_ref[0, 0].astype(BF16))
        wpos = past - wlen + lax.broadcasted_iota(jnp.int32, (1, wlen), 1)
        sw = jnp.where((wpos >= 0) & (past - wpos <= WINDOW), sw, MASKED)
        st = update(init(), sw, [(0, wlen, win_ref[0, 1].astype(BF16), True)])
        st = new_row_update(st, qrbd, kwn_ref)
        ow_ref[...] = st[2] * (1.0 / jnp.maximum(st[1], 1e-20))
        put(new_row_update(init(), qrbd, ksn_ref))

    s = jnp.concatenate([_dot(qrbd, r[0, 0].astype(BF16)) for r in pages], axis=1)
    mask = _dot(sel_ref[...].astype(BF16), e_ref[0]) > 0.5
    s = jnp.where(mask, s, MASKED)
    vs = [(i * PAGE_SIZE, PAGE_SIZE, r[0, 1].astype(BF16), True) for i, r in enumerate(pages)]
    put(update((m_ref[:, 0:1], l_ref[:, 0:1], acc_ref[...]), s, vs))

    @pl.when(j == pl.num_programs(1) - 1)
    def _():
        g = g_ref[0]
        os_ = acc_ref[...] * (1.0 / jnp.maximum(l_ref[:, 0:1], 1e-20))
        o = g[:, 0:1] * oc_ref[...] + g[:, 1:2] * os_ + g[:, 2:3] * ow_ref[...]
        o = jnp.where(own, o, 0.0)
        out = o[:, 0:HEAD_DIM]
        for h in range(1, N_KV_HEADS):
            out = out + o[:, h * HEAD_DIM:(h + 1) * HEAD_DIM]
        o_ref[0] = out.astype(BF16)


def _attn_sample(pages, pt_flat, q3, qr3, g3, kcvc, ks_new, kw_new, win, ov, e_mat, gsum, *, nb, pages_per_sample,
                 pg, past, nc, ns_valid, n_top):
    nsp = ov.shape[1]
    per_b = lambda shape: pl.BlockSpec((1,) + shape, lambda b, j, pt: (b,) + (0,) * len(shape))
    const = lambda shape: pl.BlockSpec(shape, lambda b, j, pt: (0,) * len(shape))
    grid_spec = pltpu.PrefetchScalarGridSpec(
        num_scalar_prefetch=1,
        grid=(nb, pages_per_sample // pg),
        in_specs=_page_specs(pg, pages_per_sample) + [
            per_b((N_HEADS, HEAD_DIM)), per_b((N_HEADS, HEAD_DIM)), per_b((N_HEADS, 3)),
            per_b(kcvc.shape[1:]), per_b((1, 2 * KV_W)), per_b((1, 2 * KV_W)), per_b(win.shape[1:]),
            const(ov.shape), pl.BlockSpec((1,) + e_mat.shape[1:], lambda b, j, pt: (j, 0, 0)), const(gsum.shape)],
        out_specs=per_b((N_HEADS, HEAD_DIM)),
        scratch_shapes=[pltpu.VMEM((N_HEADS, LANES), F32), pltpu.VMEM((N_HEADS, LANES), F32),
                        pltpu.VMEM((N_HEADS, KV_W), F32), pltpu.VMEM((N_HEADS, nsp), F32),
                        pltpu.VMEM((N_HEADS, KV_W), F32), pltpu.VMEM((N_HEADS, KV_W), F32)],
    )
    return pl.pallas_call(
        functools.partial(_attn_sample_kernel, pg=pg, past=past, nc=nc, ns_valid=ns_valid, n_top=n_top),
        grid_spec=grid_spec,
        out_shape=jax.ShapeDtypeStruct((nb, N_HEADS, HEAD_DIM), BF16),
        compiler_params=_params("parallel", "arbitrary"),
        name="nsa_attn_sample",
    )(pt_flat, *([pages] * pg), q3, qr3, g3, kcvc, ks_new, kw_new, win, ov, e_mat, gsum)


def _rope_tables(pos):
    half = HEAD_DIM // 2
    inv = ROPE_THETA ** (-jnp.arange(half, dtype=F32) / half)
    ang = pos.astype(F32)[:, None] * inv[None, :]
    cos = jnp.cos(ang)
    sin = jnp.sin(ang)
    reps = LANES // HEAD_DIM
    return (jnp.tile(jnp.concatenate([cos, cos], axis=1), (1, reps)),
            jnp.tile(jnp.concatenate([-sin, sin], axis=1), (1, reps)))


def _compress_weights(cmp_w1, cmp_w2, cmp_pe):
    ratio = CMP_BLOCK // CMP_STRIDE
    eye = jnp.eye(N_KV_HEADS, dtype=F32)
    w1r = cmp_w1.reshape(2, ratio, CMP_STRIDE, HEAD_DIM, HEAD_DIM)
    wbd = jnp.einsum("krsde,hg->krshdge", w1r, eye).reshape(2 * ratio * CMP_STRIDE, KV_W, KV_W).astype(BF16)
    w2bd = jnp.einsum("kef,hg->khegf", cmp_w2, eye).reshape(2, KV_W, KV_W).astype(BF16)
    pe_t = jnp.tile(cmp_pe.reshape(2 * ratio * CMP_STRIDE, HEAD_DIM), (1, N_KV_HEADS)).astype(F32)
    return wbd, pe_t, w2bd


def _overlap(nc, ncp, nsel, nsp):
    c_start = jnp.arange(ncp, dtype=jnp.int32)[:, None] * CMP_STRIDE
    s_start = jnp.arange(nsp, dtype=jnp.int32)[None, :] * SEL_BLOCK
    ov = (c_start < s_start + SEL_BLOCK) & (c_start + CMP_BLOCK > s_start)
    ov = ov & (jnp.arange(ncp)[:, None] < nc) & (jnp.arange(nsp)[None, :] < nsel)
    return ov.astype(BF16)


def _pick(*cands):
    return next(c for c in cands if c)


def _tile(n, pref):
    t = min(n, pref)
    while n % t:
        t //= 2
    return t


def kernel(x_prompt, x_sample, cache_kv_cmp, cache_kv_sel, cache_kv_win, state_ssm, page_table, norm_g, mlp_w1,
           mlp_w2, nsa_w_in, nsa_w_o, nsa_cmp_w1, nsa_cmp_w2, nsa_cmp_pe, s5_a_re, s5_a_im, s5_log_dt, s5_b_re,
           s5_b_im, s5_c_re, s5_c_im, s5_d, s5_w_glu, s5_b_glu):
    b, t, d = x_prompt.shape
    nb = x_sample.shape[0]
    pages_per_sample = page_table.shape[1]
    past = pages_per_sample * PAGE_SIZE
    rows_p = b * t
    g = norm_g.reshape(norm_g.shape[0], 4, 1, d)

    w_in = nsa_w_in[0]
    w_main = w_in[:, :Q_W + 6 * KV_W].astype(BF16)
    w_gate = jnp.pad(w_in[:, Q_W + 6 * KV_W:], ((0, 0), (0, LANES - 3 * N_HEADS))).astype(BF16)
    w_o = nsa_w_o[0].astype(BF16)
    wbd, pe_t, w2bd = _compress_weights(nsa_cmp_w1[0], nsa_cmp_w2[0], nsa_cmp_pe[0])
    w1 = mlp_w1.astype(BF16)
    w2 = mlp_w2.astype(BF16)
    w_glu = s5_w_glu[0].astype(BF16)
    ops = _s5_operators(s5_a_re[0], s5_a_im[0], s5_log_dt[0], s5_b_re[0], s5_b_im[0], s5_c_re[0], s5_c_im[0])
    d_skip = s5_d[0].reshape(1, d)
    b_glu = s5_b_glu[0].reshape(1, d)

    tm = _tile(rows_p, 512)
    ff_chunk = _tile(mlp_w1.shape[2], 1024)

    xp = x_prompt.reshape(rows_p, d)
    cos_p, sin_p = _rope_tables(jnp.arange(t, dtype=jnp.int32))
    (qT, qrT, gT, kvc, kvcT, kvsT, kvwT, ksb, kwb, vsT, vwT) = _inproj(
        xp, g[0, 0], w_main, w_gate, cos_p, sin_p, tm=_tile(t, 512), pos_blocks=t // _tile(t, 512), transposed=True)
    nsb_p = t // CMP_STRIDE
    nc_p = nsb_p - CMP_BLOCK // CMP_STRIDE + 1
    nsel_p = t // SEL_BLOCK
    kc, vcT = _compress_prompt(kvc.reshape(b, t, 2 * KV_W), wbd, pe_t, w2bd)
    ovT = _overlap(nc_p, nsb_p, nsel_p, nsel_p).T
    oT = _attn_prompt(qT, qrT, gT, kc, vcT, ksb, vsT, kwb, vwT, ovT, batch=b, seq=t, tq=LANES, nc=nc_p,
                      n_top=min(TOP_N, nsel_p))
    hp, xm = _oproj(oT, w_o, xp, g[0, 1], g[0, 2], tm=tm, transposed=True)
    hp, xn1 = _mlp(xm, hp, w1[0], w2[0], g[0, 3], g[1, 0], tm=tm, ff_chunk=ff_chunk, next_norm=True)

    npair = d // (2 * SSM_GROUP)
    nk = t // SSM_CHUNK
    pw = 2 * SSM_GROUP
    u2 = xn1.astype(BF16).reshape(b, nk, SSM_CHUNK, npair, pw).transpose(3, 1, 0, 2, 4)
    u2 = u2.reshape(npair, nk * b, SSM_CHUNK * pw)
    y2, hfin = _s5_chunk(u2, ops, nb=b, nk=nk, pp=2)
    yc = y2.reshape(npair, nk, b, SSM_CHUNK, pw).transpose(2, 1, 3, 0, 4).reshape(rows_p, d)
    hp, xm = _s5_out(yc, xn1, d_skip, w_glu, b_glu, hp, g[1, 1], g[1, 2], tm=tm)
    (hp,) = _mlp(xm, hp, w1[1], w2[1], g[1, 3], g[1, 3], tm=tm, ff_chunk=ff_chunk, next_norm=False)
    ssm_p = hfin.reshape(npair, 2, b, 2, SSM_STATE).transpose(2, 1, 0, 3, 4).reshape(b, 2, d // SSM_GROUP, SSM_STATE)

    xs = x_sample.reshape(nb, d)
    cos_s, sin_s = _rope_tables(jnp.full((nb,), past, dtype=jnp.int32))
    q_s, qr_s, gates_s, kvc_s, kvs_s, kvw_s = _inproj(
        xs, g[0, 0], w_main, w_gate, cos_s, sin_s, tm=nb, pos_blocks=1, transposed=False)
    pt_flat = page_table.reshape(-1).astype(jnp.int32)
    pg = _tile(pages_per_sample, PAGE_GROUP)
    n_pool = cache_kv_cmp.shape[1]
    feature_major = lambda c, n, s: c.transpose(0, 2, 3, 4, 1).reshape(n, 2, KV_W, s)
    cmp_pages = feature_major(cache_kv_cmp[0], n_pool, PAGE_SIZE)
    sel_pages = feature_major(cache_kv_sel[0], n_pool, PAGE_SIZE)
    kcvc = _compress_paged(cmp_pages, pt_flat, wbd, pe_t, w2bd, nb=nb, pages_per_sample=pages_per_sample, pg=pg)
    l_all = past + 1
    nsb_s = l_all // CMP_STRIDE
    nc_s = nsb_s - CMP_BLOCK // CMP_STRIDE + 1
    nsel_s = -(-l_all // SEL_BLOCK)
    nsp = -(-nsel_s // LANES) * LANES
    ov_s = _overlap(nc_s, past // CMP_STRIDE, nsel_s, nsp)
    keys_per_step = pg * PAGE_SIZE
    key_blk = (jnp.arange(past, dtype=jnp.int32) // SEL_BLOCK).reshape(past // keys_per_step, 1, keys_per_step)
    e_mat = (jnp.arange(nsp, dtype=jnp.int32)[None, :, None] == key_blk).astype(BF16)
    hh = jnp.arange(N_HEADS)
    gsum = ((hh[:, None] // GROUP) == (hh[None, :] // GROUP)).astype(BF16)
    win = feature_major(cache_kv_win[0], nb, WINDOW)
    o_s = _attn_sample(sel_pages, pt_flat, q_s.reshape(nb, N_HEADS, HEAD_DIM), qr_s.reshape(nb, N_HEADS, HEAD_DIM),
                       gates_s[:, :3 * N_HEADS].reshape(nb, N_HEADS, 3), kcvc, kvs_s.reshape(nb, 1, 2 * KV_W),
                       kvw_s.reshape(nb, 1, 2 * KV_W), win, ov_s, e_mat, gsum, nb=nb,
                       pages_per_sample=pages_per_sample, pg=pg, past=past, nc=nc_s, ns_valid=nsel_s,
                       n_top=min(TOP_N, nsel_s))
    hs, xm_s = _oproj(o_s.reshape(nb, Q_W), w_o, xs, g[0, 1], g[0, 2], tm=nb, transposed=False)
    hs, xn1_s = _mlp(xm_s, hs, w1[0], w2[0], g[0, 3], g[1, 0], tm=nb, ff_chunk=ff_chunk, next_norm=True)

    u2_s = xn1_s.astype(BF16).reshape(nb, npair, pw).transpose(1, 0, 2)
    st = state_ssm[0].reshape(nb, 2, npair, 2 * SSM_STATE).transpose(1, 2, 0, 3)
    y2_s, hr_s, hi_s = _s5_step(u2_s, st[0], st[1], ops)
    yc_s = y2_s.transpose(1, 0, 2).reshape(nb, d)
    hs, xm_s = _s5_out(yc_s, xn1_s, d_skip, w_glu, b_glu, hs, g[1, 1], g[1, 2], tm=nb)
    (hs,) = _mlp(xm_s, hs, w1[1], w2[1], g[1, 3], g[1, 3], tm=nb, ff_chunk=ff_chunk, next_norm=False)
    ssm_s = jnp.stack([hr_s, hi_s], axis=0).transpose(2, 0, 1, 3).reshape(nb, 2, d // SSM_GROUP, SSM_STATE)

    kv5 = lambda a, n, s: a.reshape(1, n, s, 2, N_KV_HEADS, HEAD_DIM)
    from_fm = lambda a, n, s: a.reshape(n, 2, N_KV_HEADS, HEAD_DIM, s).transpose(0, 4, 1, 2, 3)[None]
    win_s = jnp.concatenate([win[..., 1:], kvw_s.reshape(nb, 2, KV_W, 1)], axis=-1)
    return (hp.reshape(b, t, d), hs.reshape(nb, 1, d),
            from_fm(kvcT, b, t), kv5(kvc_s, nb, 1), from_fm(kvsT, b, t), kv5(kvs_s, nb, 1),
            from_fm(kvwT[:, :, t - WINDOW:], b, WINDOW), from_fm(win_s, nb, WINDOW), ssm_p[None], ssm_s[None])
```

```python
import functools

import jax
import jax.numpy as jnp
from jax import lax
from jax.experimental import pallas as pl
from jax.experimental.pallas import tpu as pltpu

N_HEADS = 16
HEAD_DIM = 64
N_KV_HEADS = 4
GROUP = N_HEADS // N_KV_HEADS
CMP_BLOCK = 32
CMP_STRIDE = 16
SEL_BLOCK = 64
TOP_N = 16
WINDOW = 512
ROPE_THETA = 10000.0
PAGE_SIZE = 128
SSM_GROUP = 16
SSM_STATE = 64
SSM_CHUNK = 16
EPS = 1e-6
NEG = -1e30
FORCE = 1e9
MASKED = -3.0e38
Q_W = N_HEADS * HEAD_DIM
KV_W = N_KV_HEADS * HEAD_DIM
LANES = 128
SUBLANES = 8
PAGE_GROUP = 16
VMEM_LIMIT = 56 * 1024 * 1024

F32 = jnp.float32
BF16 = jnp.bfloat16


def _params(*sem):
    return pltpu.CompilerParams(dimension_semantics=sem, vmem_limit_bytes=VMEM_LIMIT)


def _full(shape):
    zeros = (0,) * len(shape)
    return pl.BlockSpec(shape, lambda *_: zeros)


def _rms(x, g):
    ms = jnp.mean(x * x, axis=-1, keepdims=True)
    return x * lax.rsqrt(ms + EPS) * g


def _dot(a, b):
    return jnp.dot(a, b, preferred_element_type=F32)


def _dot_f32lhs(w, x):
    hi = x.astype(BF16)
    r1 = x - hi.astype(F32)
    mid = r1.astype(BF16)
    lo = (r1 - mid.astype(F32)).astype(BF16)
    return _dot(w, hi) + _dot(w, mid) + _dot(w, lo)


def _dot_f32rhs(x, w):
    hi = x.astype(BF16)
    r1 = x - hi.astype(F32)
    mid = r1.astype(BF16)
    lo = (r1 - mid.astype(F32)).astype(BF16)
    return _dot(hi, w) + _dot(mid, w) + _dot(lo, w)


def _rope_nat(x, cos, sin):
    half = HEAD_DIM // 2
    lane = lax.broadcasted_iota(jnp.int32, (1, LANES), 1)
    first = (lane % HEAD_DIM) < half
    outs = []
    for c in range(x.shape[1] // LANES):
        xc = x[:, c * LANES:(c + 1) * LANES]
        rot = jnp.where(first, pltpu.roll(xc, LANES - half, 1), pltpu.roll(xc, half, 1))
        outs.append(xc * cos + rot * sin)
    return jnp.concatenate(outs, axis=1)


def _inproj_kernel(x_ref, g_ref, w_ref, wg_ref, cos_ref, sin_ref, *outs, transposed):
    xb = _rms(x_ref[...], g_ref[...]).astype(BF16)
    cos = cos_ref[...]
    sin = sin_ref[...]
    scale = HEAD_DIM ** -0.5
    q = _dot(xb, w_ref[:, 0:Q_W])
    qr = _rope_nat(q, cos, sin)
    kv = _dot(xb, w_ref[:, Q_W:Q_W + 6 * KV_W])
    gates = jax.nn.sigmoid(_dot(xb, wg_ref[...]))
    k_s = _rope_nat(kv[:, 2 * KV_W:3 * KV_W], cos, sin)
    v_s = kv[:, 3 * KV_W:4 * KV_W]
    k_w = _rope_nat(kv[:, 4 * KV_W:5 * KV_W], cos, sin)
    v_w = kv[:, 5 * KV_W:6 * KV_W]
    kvc_ref = outs[3]
    kvc_ref[...] = kv[:, 0:2 * KV_W]
    if transposed:
        qT_ref, qrT_ref, gT_ref, _, kvcT_ref, kvsT_ref, kvwT_ref, ksb_ref, kwb_ref, vsT_ref, vwT_ref = outs
        qT_ref[...] = (q * scale).T.astype(BF16)
        qrT_ref[...] = (qr * scale).T.astype(BF16)
        gT_ref[...] = gates.T
        for h in range(N_KV_HEADS):
            ksb_ref[h] = k_s[:, h * HEAD_DIM:(h + 1) * HEAD_DIM].astype(BF16)
            kwb_ref[h] = k_w[:, h * HEAD_DIM:(h + 1) * HEAD_DIM].astype(BF16)
        v_sT = v_s.T
        v_wT = v_w.T
        vsT_ref[...] = v_sT.astype(BF16)
        vwT_ref[...] = v_wT.astype(BF16)
        kvcT_ref[0] = kv[:, 0:2 * KV_W].T
        kvsT_ref[0, 0:KV_W, :] = k_s.T
        kvsT_ref[0, KV_W:2 * KV_W, :] = v_sT
        kvwT_ref[0, 0:KV_W, :] = k_w.T
        kvwT_ref[0, KV_W:2 * KV_W, :] = v_wT
    else:
        q_ref, qr_ref, gt_ref, _, kvs_ref, kvw_ref = outs
        q_ref[...] = (q * scale).astype(BF16)
        qr_ref[...] = (qr * scale).astype(BF16)
        gt_ref[...] = gates
        kvs_ref[:, 0:KV_W] = k_s
        kvs_ref[:, KV_W:2 * KV_W] = v_s
        kvw_ref[:, 0:KV_W] = k_w
        kvw_ref[:, KV_W:2 * KV_W] = v_w


def _inproj(x, g, w_main, w_gate, cos_t, sin_t, *, tm, pos_blocks, transposed):
    rows, d = x.shape
    n = rows // tm
    row_blk = lambda w: pl.BlockSpec((tm, w), lambda i: (i, 0))
    col_blk = lambda h: pl.BlockSpec((h, tm), lambda i: (0, i))
    tab = pl.BlockSpec((tm, LANES), lambda i: (i % pos_blocks, 0))
    kv_nat = jax.ShapeDtypeStruct((rows, 2 * KV_W), F32)
    if transposed:
        seqs = rows // (pos_blocks * tm)
        kh = pl.BlockSpec((N_KV_HEADS, tm, HEAD_DIM), lambda i: (0, i, 0))
        kvT = pl.BlockSpec((1, 2 * KV_W, tm), lambda i: (i // pos_blocks, 0, i % pos_blocks))
        out_shape = ([jax.ShapeDtypeStruct((Q_W, rows), BF16)] * 2 + [jax.ShapeDtypeStruct((LANES, rows), F32)]
                     + [kv_nat] + [jax.ShapeDtypeStruct((seqs, 2 * KV_W, pos_blocks * tm), F32)] * 3
                     + [jax.ShapeDtypeStruct((N_KV_HEADS, rows, HEAD_DIM), BF16)] * 2
                     + [jax.ShapeDtypeStruct((KV_W, rows), BF16)] * 2)
        out_specs = ([col_blk(Q_W)] * 2 + [col_blk(LANES)] + [row_blk(2 * KV_W)] + [kvT] * 3 + [kh] * 2
                     + [col_blk(KV_W)] * 2)
    else:
        out_shape = ([jax.ShapeDtypeStruct((rows, Q_W), BF16)] * 2 + [jax.ShapeDtypeStruct((rows, LANES), F32)]
                     + [kv_nat] * 3)
        out_specs = [row_blk(Q_W)] * 2 + [row_blk(LANES)] + [row_blk(2 * KV_W)] * 3
    return pl.pallas_call(
        functools.partial(_inproj_kernel, transposed=transposed),
        grid=(n,),
        in_specs=[row_blk(d), _full((1, d)), _full(w_main.shape), _full(w_gate.shape), tab, tab],
        out_specs=out_specs,
        out_shape=out_shape,
        compiler_params=_params("parallel"),
        name="nsa_inproj",
    )(x, g, w_main, w_gate, cos_t, sin_t)


_KV_CHUNKS = 2 * KV_W // LANES


def _compress_half(load_rows, kv, nrows, wbd_ref, pe_ref):
    accs = []
    for r in range(CMP_BLOCK // CMP_STRIDE):
        acc = jnp.zeros((nrows, KV_W), F32)
        for s in range(CMP_STRIDE):
            idx = (kv * 2 + r) * CMP_STRIDE + s
            lhs = (load_rows(s, kv) + pe_ref[idx:idx + 1, :]).astype(BF16)
            acc = acc + _dot(lhs, wbd_ref[idx])
        accs.append(acc)
    return accs


def _compress_prompt_kernel(*refs, nsb):
    x_refs = refs[:_KV_CHUNKS]
    wbd_ref, pe_ref, w2_ref, kc_ref, vcT_ref, sh_ref = refs[_KV_CHUNKS:]

    def load_rows(s, kv):
        per_half = _KV_CHUNKS // 2
        return jnp.concatenate([x_refs[kv * per_half + c][0, pl.ds(s, nsb, stride=CMP_STRIDE), :]
                                for c in range(per_half)], axis=1)

    sh_ref[nsb:nsb + SUBLANES, :] = jnp.zeros((SUBLANES, KV_W), F32)
    for kv in range(2):
        pr0, pr1 = _compress_half(load_rows, kv, nsb, wbd_ref, pe_ref)
        sh_ref[0:nsb, :] = pr1
        h = pr0 + sh_ref[pl.ds(1, nsb), :]
        out = _dot(jax.nn.gelu(h).astype(BF16), w2_ref[kv])
        if kv == 0:
            for hh in range(N_KV_HEADS):
                kc_ref[0, hh] = out[:, hh * HEAD_DIM:(hh + 1) * HEAD_DIM].astype(BF16)
        else:
            vcT_ref[0] = out.T.astype(BF16)


def _compress_prompt(kvc3, wbd, pe_t, w2bd):
    b, t, _ = kvc3.shape
    nsb = t // CMP_STRIDE
    return pl.pallas_call(
        functools.partial(_compress_prompt_kernel, nsb=nsb),
        grid=(b,),
        in_specs=[pl.BlockSpec((1, t, LANES), lambda i, c=c: (i, 0, c)) for c in range(_KV_CHUNKS)]
        + [_full(wbd.shape), _full(pe_t.shape), _full(w2bd.shape)],
        out_specs=[pl.BlockSpec((1, N_KV_HEADS, nsb, HEAD_DIM), lambda i: (i, 0, 0, 0)),
                   pl.BlockSpec((1, KV_W, nsb), lambda i: (i, 0, 0))],
        out_shape=[jax.ShapeDtypeStruct((b, N_KV_HEADS, nsb, HEAD_DIM), BF16),
                   jax.ShapeDtypeStruct((b, KV_W, nsb), BF16)],
        scratch_shapes=[pltpu.VMEM((nsb + SUBLANES, KV_W), F32)],
        compiler_params=_params("parallel"),
        name="nsa_compress_prompt",
    )(*([kvc3] * _KV_CHUNKS), wbd, pe_t, w2bd)


def _topk_mask_T(imp, n_top):
    ns, w = imp.shape
    nblk = ns // SUBLANES
    blocks = [imp[r * SUBLANES:(r + 1) * SUBLANES, :] for r in range(nblk)]
    cnts = [jnp.zeros((SUBLANES, w), F32) for _ in range(nblk)]
    sub = lax.broadcasted_iota(jnp.int32, (SUBLANES, w), 0)
    for sp in range(ns):
        row = blocks[sp // SUBLANES][sp % SUBLANES:sp % SUBLANES + 1, :]
        for r in range(nblk):
            blk = blocks[r]
            if sp < r * SUBLANES:
                beats = jnp.where(row >= blk, 1.0, 0.0)
            elif sp >= (r + 1) * SUBLANES:
                beats = jnp.where(row > blk, 1.0, 0.0)
            else:
                beats = jnp.where(sub > (sp - r * SUBLANES), jnp.where(row >= blk, 1.0, 0.0),
                                  jnp.where(row > blk, 1.0, 0.0))
            cnts[r] = cnts[r] + beats
    return jnp.concatenate([jnp.where(c < n_top, 1.0, 0.0) for c in cnts], axis=0)


def _online_chunks(states, k_cs, vT_cs, qTs, biases):
    scores = [_dot(k_c, qT) for k_c, qT in zip(k_cs, qTs)]
    mids = []
    for (m, l, _), s, bias in zip(states, scores, biases):
        s = s + bias
        m_new = jnp.maximum(m, jnp.max(s, axis=0, keepdims=True))
        alpha = jnp.exp(m - m_new)
        p = jnp.exp(s - m_new)
        mids.append((m_new, alpha * l + jnp.sum(p, axis=0, keepdims=True), alpha, p.astype(BF16)))
    return tuple((m_new, l_new, alpha * acc + _dot(vT_c, p))
                 for (m_new, l_new, alpha, p), (_, _, acc), vT_c in zip(mids, states, vT_cs))


def _softmax_init(w):
    return (jnp.full((1, w), NEG, F32), jnp.zeros((1, w), F32), jnp.zeros((HEAD_DIM, w), F32))


def _softmax_finish(carry):
    _, l, acc = carry
    return acc * (1.0 / jnp.maximum(l, 1e-20))


def _attn_prompt_kernel(qT_ref, qrT_ref, gT_ref, kc_ref, vcT_ref, ks_ref, vsT_ref, kw_ref, vwT_ref, ovT_ref,
                        o_ref, selb_ref, oc_ref, *, tq, nc, n_top):
    ck = tq
    i = pl.program_id(1)
    t0 = i * tq
    qpos = t0 + lax.broadcasted_iota(jnp.int32, (1, tq), 1)
    ncp = kc_ref.shape[2]
    ns = ovT_ref.shape[0]
    w = GROUP * tq
    bpc = ck // SEL_BLOCK
    sel_shift = SEL_BLOCK.bit_length() - 1
    kvhs = range(N_KV_HEADS)
    heads = [[kvh * GROUP + g for g in range(GROUP)] for kvh in kvhs]
    rows = [pl.ds(kvh * HEAD_DIM, HEAD_DIM) for kvh in kvhs]
    qrT = [jnp.concatenate([qrT_ref[h * HEAD_DIM:(h + 1) * HEAD_DIM, :] for h in heads[kvh]], axis=1)
           for kvh in kvhs]
    kl = lax.broadcasted_iota(jnp.int32, (ck, tq), 0)
    ql = lax.broadcasted_iota(jnp.int32, (ck, tq), 1)
    key_le_query = jnp.where(kl <= ql, 0.0, MASKED)
    key_ge_query = jnp.where(kl >= ql, 0.0, MASKED)
    tile4 = lambda b: jnp.concatenate([b] * GROUP, axis=1)

    cidx = lax.broadcasted_iota(jnp.int32, (ncp, 1), 0)
    valid = ((cidx * CMP_STRIDE + (CMP_BLOCK - 1)) <= qpos) & (cidx < nc)
    sidx = lax.broadcasted_iota(jnp.int32, (ns, 1), 0)
    cur = lax.shift_right_logical(qpos, sel_shift)
    forced = (sidx == 0) | (sidx == cur) | (sidx == cur - 1)
    causal = (sidx * SEL_BLOCK) <= qpos
    cmp_scores = [
        _dot(kc_ref[0, kvh], jnp.concatenate([qT_ref[h * HEAD_DIM:(h + 1) * HEAD_DIM, :] for h in heads[kvh]], axis=1))
        for kvh in kvhs]
    for kvh in kvhs:
        s = cmp_scores[kvh]
        probs = []
        for g in range(GROUP):
            sm = jnp.where(valid, s[:, g * tq:(g + 1) * tq], NEG)
            mx = jnp.max(sm, axis=0, keepdims=True)
            e = jnp.where(valid, jnp.exp(sm - mx), 0.0)
            den = jnp.maximum(jnp.sum(e, axis=0, keepdims=True), 1e-20)
            probs.append(e / den)
        oc_ref[kvh] = _dot(vcT_ref[0, rows[kvh], :], jnp.concatenate(probs, axis=1).astype(BF16))
        psum = probs[0]
        for g in range(1, GROUP):
            psum = psum + probs[g]
        imp = _dot_f32lhs(ovT_ref[...], psum)
        imp = jnp.where(forced, FORCE, jnp.where(causal, imp, NEG))
        selb = (_topk_mask_T(imp, n_top) - 1.0) * (-MASKED)
        for c in range(ns // bpc):
            selb_ref[kvh, c, 0:bpc, :] = selb[c * bpc:(c + 1) * bpc, :]

    n_back = WINDOW // ck
    states = tuple(_softmax_init(w) for _ in kvhs)
    for r in range(n_back + 1):
        a = i - n_back + r
        kst = pl.multiple_of(jnp.maximum(a, 0) * ck, ck)
        if r == n_back:
            bias = key_le_query
        else:
            skip = jnp.where(a < 0, MASKED, 0.0)
            bias = (jnp.minimum(key_ge_query, skip) if r == 0 else jnp.broadcast_to(skip, (ck, tq)))
        states = _online_chunks(states, [kw_ref[kvh, pl.ds(kst, ck), :] for kvh in kvhs],
                                [vwT_ref[rows[kvh], pl.ds(kst, ck)] for kvh in kvhs], qrT,
                                [tile4(bias)] * N_KV_HEADS)
    ow = [_softmax_finish(st) for st in states]

    def sel_bias(kvh, c):
        tile = selb_ref[kvh, c]
        return jnp.concatenate([jnp.broadcast_to(tile[b:b + 1, :], (SEL_BLOCK, tq)) for b in range(bpc)], axis=0)

    def chunk_step(c, states, diagonal):
        kst = pl.multiple_of(c * ck, ck)
        biases = [sel_bias(kvh, c) for kvh in kvhs]
        if diagonal:
            biases = [jnp.minimum(b, key_le_query) for b in biases]
        return _online_chunks(states, [ks_ref[kvh, pl.ds(kst, ck), :] for kvh in kvhs],
                              [vsT_ref[rows[kvh], pl.ds(kst, ck)] for kvh in kvhs], qrT,
                              [tile4(b) for b in biases])

    states = lax.fori_loop(0, i, lambda c, st: chunk_step(c, st, False), tuple(_softmax_init(w) for _ in kvhs))
    states = chunk_step(i, states, True)

    for kvh in kvhs:
        def gate_row(j, kvh=kvh):
            return jnp.concatenate([gT_ref[h * 3 + j:h * 3 + j + 1, :] for h in heads[kvh]], axis=1)

        oT = gate_row(0) * oc_ref[kvh] + gate_row(1) * _softmax_finish(states[kvh]) + gate_row(2) * ow[kvh]
        for g, h in enumerate(heads[kvh]):
            o_ref[h * HEAD_DIM:(h + 1) * HEAD_DIM, :] = oT[:, g * tq:(g + 1) * tq].astype(BF16)


def _attn_prompt(qT, qrT, gT, kc, vcT, ksb, vsT, kwb, vwT, ovT, *, batch, seq, tq, nc, n_top):
    nq = seq // tq
    nsb = kc.shape[2]
    ns = ovT.shape[0]
    col = lambda h: pl.BlockSpec((h, tq), lambda b, i: (0, b * nq + i))
    kh = pl.BlockSpec((N_KV_HEADS, seq, HEAD_DIM), lambda b, i: (0, b, 0))
    vt = pl.BlockSpec((KV_W, seq), lambda b, i: (0, b))
    return pl.pallas_call(
        functools.partial(_attn_prompt_kernel, tq=tq, nc=nc, n_top=n_top),
        grid=(batch, nq),
        in_specs=[col(Q_W), col(Q_W), col(LANES),
                  pl.BlockSpec((1, N_KV_HEADS, nsb, HEAD_DIM), lambda b, i: (b, 0, 0, 0)),
                  pl.BlockSpec((1, KV_W, nsb), lambda b, i: (b, 0, 0)),
                  kh, vt, kh, vt, _full(ovT.shape)],
        out_specs=col(Q_W),
        out_shape=jax.ShapeDtypeStruct((Q_W, batch * seq), BF16),
        scratch_shapes=[pltpu.VMEM((N_KV_HEADS, ns * SEL_BLOCK // tq, SUBLANES, tq), F32),
                        pltpu.VMEM((N_KV_HEADS, HEAD_DIM, GROUP * tq), F32)],
        compiler_params=_params("parallel", "arbitrary"),
        name="nsa_attn_prompt",
    )(qT, qrT, gT, kc, vcT, ksb, vsT, kwb, vwT, ovT)


def _oproj_kernel(o_ref, w_ref, x_ref, g1_ref, g2_ref, h_ref, xm_ref, *, transposed):
    if transposed:
        y = lax.dot_general(o_ref[...], w_ref[...], (((0,), (0,)), ((), ())), preferred_element_type=F32)
    else:
        y = _dot(o_ref[...], w_ref[...])
    h = x_ref[...] + _rms(y, g1_ref[...])
    h_ref[...] = h
    xm_ref[...] = _rms(h, g2_ref[...]).astype(BF16)


def _oproj(o, w_o, x, g1, g2, *, tm, transposed):
    rows, d = x.shape
    o_spec = (pl.BlockSpec((Q_W, tm), lambda i: (0, i)) if transposed else pl.BlockSpec((tm, Q_W), lambda i: (i, 0)))
    row = pl.BlockSpec((tm, d), lambda i: (i, 0))
    return pl.pallas_call(
        functools.partial(_oproj_kernel, transposed=transposed),
        grid=(rows // tm,),
        in_specs=[o_spec, _full(w_o.shape), row, _full((1, d)), _full((1, d))],
        out_specs=[row, row],
        out_shape=[jax.ShapeDtypeStruct((rows, d), F32), jax.ShapeDtypeStruct((rows, d), BF16)],
        compiler_params=_params("parallel"),
        name="nsa_oproj",
    )(o, w_o, x, g1, g2)


def _mlp_kernel(xm_ref, h_ref, w1_ref, w2_ref, g3_ref, gn_ref, *rest, ff_chunk, next_norm):
    if next_norm:
        h2_ref, xn_ref, acc_ref = rest
    else:
        h2_ref, acc_ref = rest
    xm = xm_ref[...]
    for c in range(w1_ref.shape[1] // ff_chunk):
        cols = slice(c * ff_chunk, (c + 1) * ff_chunk)
        hm = jnp.maximum(_dot(xm, w1_ref[:, cols]), 0.0)
        part = _dot((hm * hm).astype(BF16), w2_ref[cols, :])
        if c == 0:
            acc_ref[...] = part
        else:
            acc_ref[...] += part
    h2 = h_ref[...] + _rms(acc_ref[...], g3_ref[...])
    h2_ref[...] = h2
    if next_norm:
        xn_ref[...] = _rms(h2, gn_ref[...])


def _mlp(xm, h, w1, w2, g3, gn, *, tm, ff_chunk, next_norm):
    rows, d = h.shape
    row = pl.BlockSpec((tm, d), lambda i: (i, 0))
    out_shape = [jax.ShapeDtypeStruct((rows, d), F32)] * (2 if next_norm else 1)
    return pl.pallas_call(
        functools.partial(_mlp_kernel, ff_chunk=ff_chunk, next_norm=next_norm),
        grid=(rows // tm,),
        in_specs=[row, row, _full(w1.shape), _full(w2.shape), _full((1, d)), _full((1, d))],
        out_specs=[row] * len(out_shape),
        out_shape=out_shape,
        scratch_shapes=[pltpu.VMEM((tm, d), F32)],
        compiler_params=_params("parallel"),
        name="sq_relu_mlp",
    )(xm, h, w1, w2, g3, gn)


def _s5_chunk_kernel(u_ref, w_ref, pre_ref, pim_ref, qre_ref, qim_ref, are_ref, aim_ref, y_ref, hfin_ref,
                     sre, sim, hre, him, *, pp, nb, nk):
    steps_per_iter = SUBLANES // nb
    for p in range(pp):
        u = u_ref[p]
        sre[p] = _dot(u, pre_ref[p])
        sim[p] = _dot(u, pim_ref[p])

    def body(it, carry):
        r0 = pl.multiple_of(it * SUBLANES, SUBLANES)
        new = []
        for p in range(pp):
            hr, hi = carry[p]
            ar = are_ref[p]
            ai = aim_ref[p]
            sr8 = sre[p, pl.ds(r0, SUBLANES), :]
            si8 = sim[p, pl.ds(r0, SUBLANES), :]
            prev_r, prev_i = [], []
            for j in range(steps_per_iter):
                prev_r.append(hr)
                prev_i.append(hi)
                sr = sr8[j * nb:(j + 1) * nb, :]
                si = si8[j * nb:(j + 1) * nb, :]
                hr, hi = ar * hr - ai * hi + sr, ar * hi + ai * hr + si
            hre[p, pl.ds(r0, SUBLANES), :] = jnp.concatenate(prev_r, axis=0)
            him[p, pl.ds(r0, SUBLANES), :] = jnp.concatenate(prev_i, axis=0)
            new.append((hr, hi))
        return tuple(new)

    zero = jnp.zeros((nb, LANES), F32)
    fin = lax.fori_loop(0, nk // steps_per_iter, body, tuple((zero, zero) for _ in range(pp)))
    for p in range(pp):
        hfin_ref[p, 0] = fin[p][0]
        hfin_ref[p, 1] = fin[p][1]
        y_ref[p] = (_dot(u_ref[p], w_ref[p]) + _dot(hre[p].astype(BF16), qre_ref[p])
                    + _dot(him[p].astype(BF16), qim_ref[p]))


def _s5_chunk(u2, ops, *, nb, nk, pp):
    npair, nch, width = u2.shape
    blk = lambda a: pl.BlockSpec((pp,) + a.shape[1:], lambda i: (i,) + (0,) * (a.ndim - 1))
    args = (u2, ops["w"], ops["p_re"], ops["p_im"], ops["q_re"], ops["q_im"], ops["a16_re"], ops["a16_im"])
    return pl.pallas_call(
        functools.partial(_s5_chunk_kernel, pp=pp, nb=nb, nk=nk),
        grid=(npair // pp,),
        in_specs=[blk(a) for a in args],
        out_specs=[pl.BlockSpec((pp, nch, width), lambda i: (i, 0, 0)),
                   pl.BlockSpec((pp, 2, nb, LANES), lambda i: (i, 0, 0, 0))],
        out_shape=[jax.ShapeDtypeStruct((npair, nch, width), F32),
                   jax.ShapeDtypeStruct((npair, 2, nb, LANES), F32)],
        scratch_shapes=[pltpu.VMEM((pp, nch, LANES), F32)] * 4,
        compiler_params=_params("parallel"),
        name="s5_chunk_scan",
    )(*args)


def _s5_step_kernel(u_ref, h0r_ref, h0i_ref, bre_ref, bim_ref, cre_ref, cim_ref, are_ref, aim_ref,
                    y_ref, hr_ref, hi_ref, *, npair):
    for p in range(npair):
        u = u_ref[p]
        ar = are_ref[p]
        ai = aim_ref[p]
        h0r = h0r_ref[p]
        h0i = h0i_ref[p]
        hr = ar * h0r - ai * h0i + _dot(u, bre_ref[p])
        hi = ar * h0i + ai * h0r + _dot(u, bim_ref[p])
        hr_ref[p] = hr
        hi_ref[p] = hi
        y_ref[p] = _dot(hr.astype(BF16), cre_ref[p]) + _dot(hi.astype(BF16), cim_ref[p])


def _s5_step(u2, h0r, h0i, ops):
    npair, rows, width = u2.shape
    args = (u2, h0r, h0i, ops["b1_re"], ops["b1_im"], ops["c1_re"], ops["c1_im"], ops["a1_re"], ops["a1_im"])
    return pl.pallas_call(
        functools.partial(_s5_step_kernel, npair=npair),
        grid=(1,),
        in_specs=[_full(a.shape) for a in args],
        out_specs=[_full((npair, rows, width)), _full(h0r.shape), _full(h0r.shape)],
        out_shape=[jax.ShapeDtypeStruct((npair, rows, width), F32), jax.ShapeDtypeStruct(h0r.shape, F32),
                   jax.ShapeDtypeStruct(h0r.shape, F32)],
        compiler_params=_params("arbitrary"),
        name="s5_single_step",
    )(*args)


def _s5_out_kernel(yc_ref, u_ref, d_ref, wg_ref, bg_ref, h_ref, g1_ref, g2_ref, h3_ref, xm_ref):
    y = jax.nn.gelu(yc_ref[...] + d_ref[...] * u_ref[...])
    out = y * jax.nn.sigmoid(_dot(y.astype(BF16), wg_ref[...]) + bg_ref[...])
    h3 = h_ref[...] + _rms(out, g1_ref[...])
    h3_ref[...] = h3
    xm_ref[...] = _rms(h3, g2_ref[...]).astype(BF16)


def _s5_out(yc, u, d_skip, w_glu, b_glu, h, g1, g2, *, tm):
    rows, d = h.shape
    row = pl.BlockSpec((tm, d), lambda i: (i, 0))
    vec = _full((1, d))
    return pl.pallas_call(
        _s5_out_kernel,
        grid=(rows // tm,),
        in_specs=[row, row, vec, _full(w_glu.shape), vec, row, vec, vec],
        out_specs=[row, row],
        out_shape=[jax.ShapeDtypeStruct((rows, d), F32), jax.ShapeDtypeStruct((rows, d), BF16)],
        compiler_params=_params("parallel"),
        name="s5_glu_out",
    )(yc, u, d_skip, w_glu, b_glu, h, g1, g2)


def _s5_operators(a_re, a_im, log_dt, b_re, b_im, c_re, c_im):
    hp = lax.Precision.HIGHEST
    g, n = a_re.shape
    npair = g // 2
    L = SSM_CHUNK
    a = lax.complex(a_re.astype(F32), a_im.astype(F32))
    dt = jnp.exp(log_dt.astype(F32))[:, None]
    a_bar = jnp.exp(a * dt)
    b_bar = ((a_bar - 1.0) / a)[:, :, None] * lax.complex(b_re.astype(F32), b_im.astype(F32))
    c = lax.complex(c_re.astype(F32), c_im.astype(F32))
    pows = [jnp.ones_like(a_bar)]
    for _ in range(L):
        pows.append(pows[-1] * a_bar)
    a_pow = jnp.stack(pows)
    kern = jnp.real(jnp.einsum("gcn,tgn,gnd->gtcd", c, a_pow[:L], b_bar, precision=hp))
    t1 = jnp.arange(L)[:, None]
    t2 = jnp.arange(L)[None, :]
    lag = jnp.clip(t2 - t1, 0, L - 1)
    wg = jnp.where((t2 >= t1)[None, :, :, None, None], kern[:, lag], 0.0)
    eye2 = jnp.eye(2, dtype=F32)
    wg = wg.reshape(npair, 2, L, L, SSM_GROUP, SSM_GROUP)
    w = jnp.einsum("pgtscd,gh->ptgdshc", wg, eye2).reshape(npair, L * 2 * SSM_GROUP, L * 2 * SSM_GROUP)
    pc = a_pow[L - 1 - jnp.arange(L)][:, :, :, None] * b_bar[None]
    pc = pc.reshape(L, npair, 2, n, SSM_GROUP)

    def p_mat(x):
        return jnp.einsum("tpgnd,gh->ptgdhn", x, eye2).reshape(npair, L * 2 * SSM_GROUP, 2 * n)

    qc = c[None] * a_pow[1:L + 1][:, :, None, :]
    qc = qc.reshape(L, npair, 2, SSM_GROUP, n)

    def q_mat(x):
        return jnp.einsum("spgcn,gh->pgnshc", x, eye2).reshape(npair, 2 * n, L * 2 * SSM_GROUP)

    def pair_row(x):
        return x.reshape(npair, 1, 2 * n)

    b1 = b_bar.reshape(npair, 2, n, SSM_GROUP)
    c1 = c.reshape(npair, 2, SSM_GROUP, n)

    def b1_mat(x):
        return jnp.einsum("pgnd,gh->pgdhn", x, eye2).reshape(npair, 2 * SSM_GROUP, 2 * n)

    def c1_mat(x):
        return jnp.einsum("pgcn,gh->pgnhc", x, eye2).reshape(npair, 2 * n, 2 * SSM_GROUP)

    return {
        "w": w.astype(BF16),
        "p_re": p_mat(jnp.real(pc)).astype(BF16), "p_im": p_mat(jnp.imag(pc)).astype(BF16),
        "q_re": q_mat(jnp.real(qc)).astype(BF16), "q_im": q_mat(-jnp.imag(qc)).astype(BF16),
        "a16_re": pair_row(jnp.real(a_pow[L])), "a16_im": pair_row(jnp.imag(a_pow[L])),
        "a1_re": pair_row(jnp.real(a_bar)), "a1_im": pair_row(jnp.imag(a_bar)),
        "b1_re": b1_mat(jnp.real(b_bar.reshape(npair, 2, n, SSM_GROUP))).astype(BF16),
        "b1_im": b1_mat(jnp.imag(b_bar.reshape(npair, 2, n, SSM_GROUP))).astype(BF16),
        "c1_re": c1_mat(jnp.real(c1)).astype(BF16), "c1_im": c1_mat(-jnp.imag(c1)).astype(BF16),
    }


def _compress_paged_kernel(pt_ref, *refs, pg, nsb):
    del pt_ref
    pages = refs[:pg]
    wbd_ref, pe_ref, w2_ref, out_ref, h0_ref, h1_ref, xs_ref = refs[pg:]
    j = pl.program_id(1)
    sbp = PAGE_SIZE // CMP_STRIDE
    nrows = pg * sbp
    per_half = _KV_CHUNKS // 2

    for i, r in enumerate(pages):
        for kv in range(2):
            x = r[0, kv].T
            for c in range(per_half):
                xs_ref[kv * per_half + c, i * PAGE_SIZE:(i + 1) * PAGE_SIZE, :] = x[:, c * LANES:(c + 1) * LANES]

    def load_rows(s, kv):
        return jnp.concatenate([xs_ref[kv * per_half + c, pl.ds(s, nrows, stride=CMP_STRIDE), :]
                                for c in range(per_half)], axis=1)

    @pl.when(j == 0)
    def _():
        h1_ref[:, nsb:nsb + SUBLANES, :] = jnp.zeros((2, SUBLANES, KV_W), F32)

    r0 = pl.multiple_of(j * nrows, nrows)
    for kv in range(2):
        pr0, pr1 = _compress_half(load_rows, kv, nrows, wbd_ref, pe_ref)
        h0_ref[kv, pl.ds(r0, nrows), :] = pr0
        h1_ref[kv, pl.ds(r0, nrows), :] = pr1

    @pl.when(j == pl.num_programs(1) - 1)
    def _():
        for kv in range(2):
            h = h0_ref[kv] + h1_ref[kv, pl.ds(1, nsb), :]
            out = _dot(jax.nn.gelu(h).astype(BF16), w2_ref[kv])
            out_ref[0, :, kv * KV_W:(kv + 1) * KV_W] = out.astype(BF16)


def _page_specs(pg, pages_per_sample):
    def spec(i):
        return pl.BlockSpec((1, 2, KV_W, PAGE_SIZE),
                            lambda b, j, pt: (pt[b * pages_per_sample + j * pg + i], 0, 0, 0))
    return [spec(i) for i in range(pg)]


def _compress_paged(pages, pt_flat, wbd, pe_t, w2bd, *, nb, pages_per_sample, pg):
    nsb = pages_per_sample * PAGE_SIZE // CMP_STRIDE
    c3 = lambda shape: pl.BlockSpec(shape, lambda b, j, pt: (0,) * len(shape))
    grid_spec = pltpu.PrefetchScalarGridSpec(
        num_scalar_prefetch=1,
        grid=(nb, pages_per_sample // pg),
        in_specs=_page_specs(pg, pages_per_sample) + [c3(wbd.shape), c3(pe_t.shape), c3(w2bd.shape)],
        out_specs=pl.BlockSpec((1, nsb, 2 * KV_W), lambda b, j, pt: (b, 0, 0)),
        scratch_shapes=[pltpu.VMEM((2, nsb, KV_W), F32), pltpu.VMEM((2, nsb + SUBLANES, KV_W), F32),
                        pltpu.VMEM((_KV_CHUNKS, pg * PAGE_SIZE, LANES), F32)],
    )
    return pl.pallas_call(
        functools.partial(_compress_paged_kernel, pg=pg, nsb=nsb),
        grid_spec=grid_spec,
        out_shape=jax.ShapeDtypeStruct((nb, nsb, 2 * KV_W), BF16),
        compiler_params=_params("parallel", "arbitrary"),
        name="nsa_compress_paged",
    )(pt_flat, *([pages] * pg), wbd, pe_t, w2bd)


def _topk_mask_lanes(imp, n_top, ns_valid):
    lane = lax.broadcasted_iota(jnp.int32, imp.shape, 1)
    cnt = jnp.zeros(imp.shape, F32)
    for sp in range(ns_valid):
        col = imp[:, sp:sp + 1]
        cnt = cnt + jnp.where(lane > sp, jnp.where(col >= imp, 1.0, 0.0), jnp.where(col > imp, 1.0, 0.0))
    return jnp.where((cnt < n_top) & (lane < ns_valid), 1.0, 0.0)


def _dot_nt(a, b):
    return lax.dot_general(a, b, (((1,), (1,)), ((), ())), preferred_element_type=F32)


def _attn_sample_kernel(pt_ref, *refs, pg, past, nc, ns_valid, n_top):
    del pt_ref
    pages = refs[:pg]
    (q_ref, qr_ref, g_ref, kcvc_ref, ksn_ref, kwn_ref, win_ref, ov_ref, e_ref, gs_ref, o_ref,
     m_ref, l_ref, acc_ref, sel_ref, oc_ref, ow_ref) = refs[pg:]
    j = pl.program_id(1)
    ncp = kcvc_ref.shape[1]
    nsp = ov_ref.shape[1]
    wlen = win_ref.shape[3]
    row = lax.broadcasted_iota(jnp.int32, (N_HEADS, KV_W), 0)
    lane = lax.broadcasted_iota(jnp.int32, (N_HEADS, KV_W), 1)
    own = (lane // HEAD_DIM) == (row // GROUP)

    def spread(ref):
        q = ref[0]
        return jnp.where(own, jnp.concatenate([q] * N_KV_HEADS, axis=1), jnp.zeros((N_HEADS, KV_W), BF16))

    def update(state, s, vs):
        m, l, acc = state
        m_new = jnp.maximum(m, jnp.max(s, axis=1, keepdims=True))
        alpha = jnp.exp(m - m_new)
        p = jnp.exp(s - m_new)
        l = alpha * l + jnp.sum(p, axis=1, keepdims=True)
        pv = None
        for st, sz, v, feature_major in vs:
            pb = p[:, st:st + sz].astype(BF16)
            t = _dot_nt(pb, v) if feature_major else _dot(pb, v)
            pv = t if pv is None else pv + t
        return m_new, l, alpha * acc + pv

    def init():
        return (jnp.full((N_HEADS, 1), NEG, F32), jnp.zeros((N_HEADS, 1), F32), jnp.zeros((N_HEADS, KV_W), F32))

    def new_row_update(state, qbd, new_ref):
        r8 = lax.broadcasted_iota(jnp.int32, (SUBLANES, 2 * KV_W), 0)
        tile = jnp.where(r8 == 0, jnp.broadcast_to(new_ref[0], (SUBLANES, 2 * KV_W)), 0.0).astype(BF16)
        s = _dot_nt(qbd, tile[:, 0:KV_W])
        l8 = lax.broadcasted_iota(jnp.int32, (N_HEADS, SUBLANES), 1)
        s = jnp.where(l8 == 0, s, MASKED)
        return update(state, s, [(0, SUBLANES, tile[:, KV_W:2 * KV_W], False)])

    def put(state):
        m, l, acc = state
        m_ref[...] = jnp.broadcast_to(m, m_ref.shape)
        l_ref[...] = jnp.broadcast_to(l, l_ref.shape)
        acc_ref[...] = acc

    qrbd = spread(qr_ref)

    @pl.when(j == 0)
    def _():
        s = _dot_nt(spread(q_ref), kcvc_ref[0, :, 0:KV_W])
        cidx = lax.broadcasted_iota(jnp.int32, (1, ncp), 1)
        valid = ((cidx * CMP_STRIDE + (CMP_BLOCK - 1)) <= past) & (cidx < nc)
        sm = jnp.where(valid, s, NEG)
        mx = jnp.max(sm, axis=1, keepdims=True)
        e = jnp.where(valid, jnp.exp(sm - mx), 0.0)
        p = e / jnp.maximum(jnp.sum(e, axis=1, keepdims=True), 1e-20)
        oc_ref[...] = _dot(p.astype(BF16), kcvc_ref[0, :, KV_W:2 * KV_W])
        imp = _dot_f32rhs(_dot_f32lhs(gs_ref[...], p), ov_ref[...])
        sidx = lax.broadcasted_iota(jnp.int32, (1, nsp), 1)
        cur = past // SEL_BLOCK
        forced = (sidx == 0) | (sidx == cur) | (sidx == cur - 1)
        causal = (sidx * SEL_BLOCK) <= past
        imp = jnp.where(forced, FORCE, jnp.where(causal, imp, NEG))
        imp = jnp.where(sidx < ns_valid, imp, MASKED)
        sel_ref[...] = _topk_mask_lanes(imp, n_top, ns_valid)
        sw = _dot(qrbd, win_ref[0, 0].astype(BF16))
        wpos = past - wlen + lax.broadcasted_iota(jnp.int32, (1, wlen), 1)
        sw = jnp.where((wpos >= 0) & (past - wpos <= WINDOW), sw, MASKED)
        st = update(init(), sw, [(0, wlen, win_ref[0, 1].astype(BF16), True)])
        st = new_row_update(st, qrbd, kwn_ref)
        ow_ref[...] = st[2] * (1.0 / jnp.maximum(st[1], 1e-20))
        put(new_row_update(init(), qrbd, ksn_ref))

    s = jnp.concatenate([_dot(qrbd, r[0, 0].astype(BF16)) for r in pages], axis=1)
    mask = _dot(sel_ref[...].astype(BF16), e_ref[0]) > 0.5
    s = jnp.where(mask, s, MASKED)
    vs = [(i * PAGE_SIZE, PAGE_SIZE, r[0, 1].astype(BF16), True) for i, r in enumerate(pages)]
    put(update((m_ref[:, 0:1], l_ref[:, 0:1], acc_ref[...]), s, vs))

    @pl.when(j == pl.num_programs(1) - 1)
    def _():
        g = g_ref[0]
        os_ = acc_ref[...] * (1.0 / jnp.maximum(l_ref[:, 0:1], 1e-20))
        o = g[:, 0:1] * oc_ref[...] + g[:, 1:2] * os_ + g[:, 2:3] * ow_ref[...]
        o = jnp.where(own, o, 0.0)
        out = o[:, 0:HEAD_DIM]
        for h in range(1, N_KV_HEADS):
            out = out + o[:, h * HEAD_DIM:(h + 1) * HEAD_DIM]
        o_ref[0] = out.astype(BF16)


def _attn_sample(pages, pt_flat, q3, qr3, g3, kcvc, ks_new, kw_new, win, ov, e_mat, gsum, *, nb, pages_per_sample,
                 pg, past, nc, ns_valid, n_top):
    nsp = ov.shape[1]
    per_b = lambda shape: pl.BlockSpec((1,) + shape, lambda b, j, pt: (b,) + (0,) * len(shape))
    const = lambda shape: pl.BlockSpec(shape, lambda b, j, pt: (0,) * len(shape))
    grid_spec = pltpu.PrefetchScalarGridSpec(
        num_scalar_prefetch=1,
        grid=(nb, pages_per_sample // pg),
        in_specs=_page_specs(pg, pages_per_sample) + [
            per_b((N_HEADS, HEAD_DIM)), per_b((N_HEADS, HEAD_DIM)), per_b((N_HEADS, 3)),
            per_b(kcvc.shape[1:]), per_b((1, 2 * KV_W)), per_b((1, 2 * KV_W)), per_b(win.shape[1:]),
            const(ov.shape), pl.BlockSpec((1,) + e_mat.shape[1:], lambda b, j, pt: (j, 0, 0)), const(gsum.shape)],
        out_specs=per_b((N_HEADS, HEAD_DIM)),
        scratch_shapes=[pltpu.VMEM((N_HEADS, LANES), F32), pltpu.VMEM((N_HEADS, LANES), F32),
                        pltpu.VMEM((N_HEADS, KV_W), F32), pltpu.VMEM((N_HEADS, nsp), F32),
                        pltpu.VMEM((N_HEADS, KV_W), F32), pltpu.VMEM((N_HEADS, KV_W), F32)],
    )
    return pl.pallas_call(
        functools.partial(_attn_sample_kernel, pg=pg, past=past, nc=nc, ns_valid=ns_valid, n_top=n_top),
        grid_spec=grid_spec,
        out_shape=jax.ShapeDtypeStruct((nb, N_HEADS, HEAD_DIM), BF16),
        compiler_params=_params("parallel", "arbitrary"),
        name="nsa_attn_sample",
    )(pt_flat, *([pages] * pg), q3, qr3, g3, kcvc, ks_new, kw_new, win, ov, e_mat, gsum)


def _rope_tables(pos):
    half = HEAD_DIM // 2
    inv = ROPE_THETA ** (-jnp.arange(half, dtype=F32) / half)
    ang = pos.astype(F32)[:, None] * inv[None, :]
    cos = jnp.cos(ang)
    sin = jnp.sin(ang)
    reps = LANES // HEAD_DIM
    return (jnp.tile(jnp.concatenate([cos, cos], axis=1), (1, reps)),
            jnp.tile(jnp.concatenate([-sin, sin], axis=1), (1, reps)))


def _compress_weights(cmp_w1, cmp_w2, cmp_pe):
    ratio = CMP_BLOCK // CMP_STRIDE
    eye = jnp.eye(N_KV_HEADS, dtype=F32)
    w1r = cmp_w1.reshape(2, ratio, CMP_STRIDE, HEAD_DIM, HEAD_DIM)
    wbd = jnp.einsum("krsde,hg->krshdge", w1r, eye).reshape(2 * ratio * CMP_STRIDE, KV_W, KV_W).astype(BF16)
    w2bd = jnp.einsum("kef,hg->khegf", cmp_w2, eye).reshape(2, KV_W, KV_W).astype(BF16)
    pe_t = jnp.tile(cmp_pe.reshape(2 * ratio * CMP_STRIDE, HEAD_DIM), (1, N_KV_HEADS)).astype(F32)
    return wbd, pe_t, w2bd


def _overlap(nc, ncp, nsel, nsp):
    c_start = jnp.arange(ncp, dtype=jnp.int32)[:, None] * CMP_STRIDE
    s_start = jnp.arange(nsp, dtype=jnp.int32)[None, :] * SEL_BLOCK
    ov = (c_start < s_start + SEL_BLOCK) & (c_start + CMP_BLOCK > s_start)
    ov = ov & (jnp.arange(ncp)[:, None] < nc) & (jnp.arange(nsp)[None, :] < nsel)
    return ov.astype(BF16)


def _pick(*cands):
    return next(c for c in cands if c)


def _tile(n, pref):
    t = min(n, pref)
    while n % t:
        t //= 2
    return t


def kernel(x_prompt, x_sample, cache_kv_cmp, cache_kv_sel, cache_kv_win, state_ssm, page_table, norm_g, mlp_w1,
           mlp_w2, nsa_w_in, nsa_w_o, nsa_cmp_w1, nsa_cmp_w2, nsa_cmp_pe, s5_a_re, s5_a_im, s5_log_dt, s5_b_re,
           s5_b_im, s5_c_re, s5_c_im, s5_d, s5_w_glu, s5_b_glu):
    b, t, d = x_prompt.shape
    nb = x_sample.shape[0]
    pages_per_sample = page_table.shape[1]
    past = pages_per_sample * PAGE_SIZE
    rows_p = b * t
    g = norm_g.reshape(norm_g.shape[0], 4, 1, d)

    w_in = nsa_w_in[0]
    w_main = w_in[:, :Q_W + 6 * KV_W].astype(BF16)
    w_gate = jnp.pad(w_in[:, Q_W + 6 * KV_W:], ((0, 0), (0, LANES - 3 * N_HEADS))).astype(BF16)
    w_o = nsa_w_o[0].astype(BF16)
    wbd, pe_t, w2bd = _compress_weights(nsa_cmp_w1[0], nsa_cmp_w2[0], nsa_cmp_pe[0])
    w1 = mlp_w1.astype(BF16)
    w2 = mlp_w2.astype(BF16)
    w_glu = s5_w_glu[0].astype(BF16)
    ops = _s5_operators(s5_a_re[0], s5_a_im[0], s5_log_dt[0], s5_b_re[0], s5_b_im[0], s5_c_re[0], s5_c_im[0])
    d_skip = s5_d[0].reshape(1, d)
    b_glu = s5_b_glu[0].reshape(1, d)

    tm = _tile(rows_p, 512)
    ff_chunk = _tile(mlp_w1.shape[2], 1024)

    xp = x_prompt.reshape(rows_p, d)
    cos_p, sin_p = _rope_tables(jnp.arange(t, dtype=jnp.int32))
    (qT, qrT, gT, kvc, kvcT, kvsT, kvwT, ksb, kwb, vsT, vwT) = _inproj(
        xp, g[0, 0], w_main, w_gate, cos_p, sin_p, tm=_tile(t, 512), pos_blocks=t // _tile(t, 512), transposed=True)
    nsb_p = t // CMP_STRIDE
    nc_p = nsb_p - CMP_BLOCK // CMP_STRIDE + 1
    nsel_p = t // SEL_BLOCK
    kc, vcT = _compress_prompt(kvc.reshape(b, t, 2 * KV_W), wbd, pe_t, w2bd)
    ovT = _overlap(nc_p, nsb_p, nsel_p, nsel_p).T
    tq = next(c for c in (2 * LANES, LANES) if t % c == 0 and WINDOW % c == 0)
    oT = _attn_prompt(qT, qrT, gT, kc, vcT, ksb, vsT, kwb, vwT, ovT, batch=b, seq=t, tq=tq, nc=nc_p,
                      n_top=min(TOP_N, nsel_p))
    hp, xm = _oproj(oT, w_o, xp, g[0, 1], g[0, 2], tm=tm, transposed=True)
    hp, xn1 = _mlp(xm, hp, w1[0], w2[0], g[0, 3], g[1, 0], tm=tm, ff_chunk=ff_chunk, next_norm=True)

    npair = d // (2 * SSM_GROUP)
    nk = t // SSM_CHUNK
    pw = 2 * SSM_GROUP
    u2 = xn1.astype(BF16).reshape(b, nk, SSM_CHUNK, npair, pw).transpose(3, 1, 0, 2, 4)
    u2 = u2.reshape(npair, nk * b, SSM_CHUNK * pw)
    y2, hfin = _s5_chunk(u2, ops, nb=b, nk=nk, pp=2)
    yc = y2.reshape(npair, nk, b, SSM_CHUNK, pw).transpose(2, 1, 3, 0, 4).reshape(rows_p, d)
    hp, xm = _s5_out(yc, xn1, d_skip, w_glu, b_glu, hp, g[1, 1], g[1, 2], tm=tm)
    (hp,) = _mlp(xm, hp, w1[1], w2[1], g[1, 3], g[1, 3], tm=tm, ff_chunk=ff_chunk, next_norm=False)
    ssm_p = hfin.reshape(npair, 2, b, 2, SSM_STATE).transpose(2, 1, 0, 3, 4).reshape(b, 2, d // SSM_GROUP, SSM_STATE)

    xs = x_sample.reshape(nb, d)
    cos_s, sin_s = _rope_tables(jnp.full((nb,), past, dtype=jnp.int32))
    q_s, qr_s, gates_s, kvc_s, kvs_s, kvw_s = _inproj(
        xs, g[0, 0], w_main, w_gate, cos_s, sin_s, tm=nb, pos_blocks=1, transposed=False)
    pt_flat = page_table.reshape(-1).astype(jnp.int32)
    pg = _tile(pages_per_sample, PAGE_GROUP)
    n_pool = cache_kv_cmp.shape[1]
    feature_major = lambda c, n, s: c.transpose(0, 2, 3, 4, 1).reshape(n, 2, KV_W, s)
    cmp_pages = feature_major(cache_kv_cmp[0], n_pool, PAGE_SIZE)
    sel_pages = feature_major(cache_kv_sel[0], n_pool, PAGE_SIZE)
    kcvc = _compress_paged(cmp_pages, pt_flat, wbd, pe_t, w2bd, nb=nb, pages_per_sample=pages_per_sample, pg=pg)
    l_all = past + 1
    nsb_s = l_all // CMP_STRIDE
    nc_s = nsb_s - CMP_BLOCK // CMP_STRIDE + 1
    nsel_s = -(-l_all // SEL_BLOCK)
    nsp = -(-nsel_s // LANES) * LANES
    ov_s = _overlap(nc_s, past // CMP_STRIDE, nsel_s, nsp)
    keys_per_step = pg * PAGE_SIZE
    key_blk = (jnp.arange(past, dtype=jnp.int32) // SEL_BLOCK).reshape(past // keys_per_step, 1, keys_per_step)
    e_mat = (jnp.arange(nsp, dtype=jnp.int32)[None, :, None] == key_blk).astype(BF16)
    hh = jnp.arange(N_HEADS)
    gsum = ((hh[:, None] // GROUP) == (hh[None, :] // GROUP)).astype(BF16)
    win = feature_major(cache_kv_win[0], nb, WINDOW)
    o_s = _attn_sample(sel_pages, pt_flat, q_s.reshape(nb, N_HEADS, HEAD_DIM), qr_s.reshape(nb, N_HEADS, HEAD_DIM),
                       gates_s[:, :3 * N_HEADS].reshape(nb, N_HEADS, 3), kcvc, kvs_s.reshape(nb, 1, 2 * KV_W),
                       kvw_s.reshape(nb, 1, 2 * KV_W), win, ov_s, e_mat, gsum, nb=nb,
                       pages_per_sample=pages_per_sample, pg=pg, past=past, nc=nc_s, ns_valid=nsel_s,
                       n_top=min(TOP_N, nsel_s))
    hs, xm_s = _oproj(o_s.reshape(nb, Q_W), w_o, xs, g[0, 1], g[0, 2], tm=nb, transposed=False)
    hs, xn1_s = _mlp(xm_s, hs, w1[0], w2[0], g[0, 3], g[1, 0], tm=nb, ff_chunk=ff_chunk, next_norm=True)

    u2_s = xn1_s.astype(BF16).reshape(nb, npair, pw).transpose(1, 0, 2)
    st = state_ssm[0].reshape(nb, 2, npair, 2 * SSM_STATE).transpose(1, 2, 0, 3)
    y2_s, hr_s, hi_s = _s5_step(u2_s, st[0], st[1], ops)
    yc_s = y2_s.transpose(1, 0, 2).reshape(nb, d)
    hs, xm_s = _s5_out(yc_s, xn1_s, d_skip, w_glu, b_glu, hs, g[1, 1], g[1, 2], tm=nb)
    (hs,) = _mlp(xm_s, hs, w1[1], w2[1], g[1, 3], g[1, 3], tm=nb, ff_chunk=ff_chunk, next_norm=False)
    ssm_s = jnp.stack([hr_s, hi_s], axis=0).transpose(2, 0, 1, 3).reshape(nb, 2, d // SSM_GROUP, SSM_STATE)

    kv5 = lambda a, n, s: a.reshape(1, n, s, 2, N_KV_HEADS, HEAD_DIM)
    from_fm = lambda a, n, s: a.reshape(n, 2, N_KV_HEADS, HEAD_DIM, s).transpose(0, 4, 1, 2, 3)[None]
    win_s = jnp.concatenate([win[..., 1:], kvw_s.reshape(nb, 2, KV_W, 1)], axis=-1)
    return (hp.reshape(b, t, d), hs.reshape(nb, 1, d),
            from_fm(kvcT, b, t), kv5(kvc_s, nb, 1), from_fm(kvsT, b, t), kv5(kvs_s, nb, 1),
            from_fm(kvwT[:, :, t - WINDOW:], b, WINDOW), from_fm(win_s, nb, WINDOW), ssm_p[None], ssm_s[None])
```

```python
import functools

import jax
import jax.numpy as jnp
from jax import lax
from jax.experimental import pallas as pl
from jax.experimental.pallas import tpu as pltpu

N_HEADS = 16
HEAD_DIM = 64
N_KV_HEADS = 4
GROUP = N_HEADS // N_KV_HEADS
CMP_BLOCK = 32
CMP_STRIDE = 16
SEL_BLOCK = 64
TOP_N = 16
WINDOW = 512
ROPE_THETA = 10000.0
PAGE_SIZE = 128
SSM_GROUP = 16
SSM_STATE = 64
SSM_CHUNK = 8
SSM_UNIT = 8
EPS = 1e-6
NEG = -1e30
FORCE = 1e9
MASKED = -3.0e38
Q_W = N_HEADS * HEAD_DIM
KV_W = N_KV_HEADS * HEAD_DIM
LANES = 128
SUBLANES = 8
PAGE_GROUP = 16
VMEM_LIMIT = 56 * 1024 * 1024

F32 = jnp.float32
BF16 = jnp.bfloat16


def _params(*sem):
    return pltpu.CompilerParams(dimension_semantics=sem, vmem_limit_bytes=VMEM_LIMIT)


def _full(shape):
    zeros = (0,) * len(shape)
    return pl.BlockSpec(shape, lambda *_: zeros)


def _rms(x, g):
    ms = jnp.mean(x * x, axis=-1, keepdims=True)
    return x * lax.rsqrt(ms + EPS) * g


def _dot(a, b):
    return jnp.dot(a, b, preferred_element_type=F32)


def _dot_f32lhs(w, x):
    hi = x.astype(BF16)
    r1 = x - hi.astype(F32)
    mid = r1.astype(BF16)
    lo = (r1 - mid.astype(F32)).astype(BF16)
    return _dot(w, hi) + _dot(w, mid) + _dot(w, lo)


def _dot_f32rhs(x, w):
    hi = x.astype(BF16)
    r1 = x - hi.astype(F32)
    mid = r1.astype(BF16)
    lo = (r1 - mid.astype(F32)).astype(BF16)
    return _dot(hi, w) + _dot(mid, w) + _dot(lo, w)


def _rope_nat(x, cos, sin):
    half = HEAD_DIM // 2
    lane = lax.broadcasted_iota(jnp.int32, (1, LANES), 1)
    first = (lane % HEAD_DIM) < half
    outs = []
    for c in range(x.shape[1] // LANES):
        xc = x[:, c * LANES:(c + 1) * LANES]
        rot = jnp.where(first, pltpu.roll(xc, LANES - half, 1), pltpu.roll(xc, half, 1))
        outs.append(xc * cos + rot * sin)
    return jnp.concatenate(outs, axis=1)


def _inproj_kernel(x_ref, g_ref, w_ref, wg_ref, cos_ref, sin_ref, *outs, transposed):
    xb = _rms(x_ref[...], g_ref[...]).astype(BF16)
    cos = cos_ref[...]
    sin = sin_ref[...]
    scale = HEAD_DIM ** -0.5
    q = _dot(xb, w_ref[:, 0:Q_W])
    qr = _rope_nat(q, cos, sin)
    kv = _dot(xb, w_ref[:, Q_W:Q_W + 6 * KV_W])
    gates = jax.nn.sigmoid(_dot(xb, wg_ref[...]))
    k_s = _rope_nat(kv[:, 2 * KV_W:3 * KV_W], cos, sin)
    v_s = kv[:, 3 * KV_W:4 * KV_W]
    k_w = _rope_nat(kv[:, 4 * KV_W:5 * KV_W], cos, sin)
    v_w = kv[:, 5 * KV_W:6 * KV_W]
    kvc_ref = outs[3]
    kvc_ref[...] = kv[:, 0:2 * KV_W]
    if transposed:
        qT_ref, qrT_ref, gT_ref, _, kvcT_ref, kvsT_ref, kvwT_ref, ksb_ref, kwb_ref, vsT_ref, vwT_ref = outs
        qT_ref[...] = (q * scale).T.astype(BF16)
        qrT_ref[...] = (qr * scale).T.astype(BF16)
        gT_ref[...] = gates.T
        for h in range(N_KV_HEADS):
            ksb_ref[h] = k_s[:, h * HEAD_DIM:(h + 1) * HEAD_DIM].astype(BF16)
            kwb_ref[h] = k_w[:, h * HEAD_DIM:(h + 1) * HEAD_DIM].astype(BF16)
        v_sT = v_s.T
        v_wT = v_w.T
        vsT_ref[...] = v_sT.astype(BF16)
        vwT_ref[...] = v_wT.astype(BF16)
        kvcT_ref[0] = kv[:, 0:2 * KV_W].T
        kvsT_ref[0, 0:KV_W, :] = k_s.T
        kvsT_ref[0, KV_W:2 * KV_W, :] = v_sT
        kvwT_ref[0, 0:KV_W, :] = k_w.T
        kvwT_ref[0, KV_W:2 * KV_W, :] = v_wT
    else:
        q_ref, qr_ref, gt_ref, _, kvs_ref, kvw_ref = outs
        q_ref[...] = (q * scale).astype(BF16)
        qr_ref[...] = (qr * scale).astype(BF16)
        gt_ref[...] = gates
        kvs_ref[:, 0:KV_W] = k_s
        kvs_ref[:, KV_W:2 * KV_W] = v_s
        kvw_ref[:, 0:KV_W] = k_w
        kvw_ref[:, KV_W:2 * KV_W] = v_w


def _inproj(x, g, w_main, w_gate, cos_t, sin_t, *, tm, pos_blocks, transposed):
    rows, d = x.shape
    n = rows // tm
    row_blk = lambda w: pl.BlockSpec((tm, w), lambda i: (i, 0))
    col_blk = lambda h: pl.BlockSpec((h, tm), lambda i: (0, i))
    tab = pl.BlockSpec((tm, LANES), lambda i: (i % pos_blocks, 0))
    kv_nat = jax.ShapeDtypeStruct((rows, 2 * KV_W), F32)
    if transposed:
        seqs = rows // (pos_blocks * tm)
        kh = pl.BlockSpec((N_KV_HEADS, tm, HEAD_DIM), lambda i: (0, i, 0))
        kvT = pl.BlockSpec((1, 2 * KV_W, tm), lambda i: (i // pos_blocks, 0, i % pos_blocks))
        out_shape = ([jax.ShapeDtypeStruct((Q_W, rows), BF16)] * 2 + [jax.ShapeDtypeStruct((LANES, rows), F32)]
                     + [kv_nat] + [jax.ShapeDtypeStruct((seqs, 2 * KV_W, pos_blocks * tm), F32)] * 3
                     + [jax.ShapeDtypeStruct((N_KV_HEADS, rows, HEAD_DIM), BF16)] * 2
                     + [jax.ShapeDtypeStruct((KV_W, rows), BF16)] * 2)
        out_specs = ([col_blk(Q_W)] * 2 + [col_blk(LANES)] + [row_blk(2 * KV_W)] + [kvT] * 3 + [kh] * 2
                     + [col_blk(KV_W)] * 2)
    else:
        out_shape = ([jax.ShapeDtypeStruct((rows, Q_W), BF16)] * 2 + [jax.ShapeDtypeStruct((rows, LANES), F32)]
                     + [kv_nat] * 3)
        out_specs = [row_blk(Q_W)] * 2 + [row_blk(LANES)] + [row_blk(2 * KV_W)] * 3
    return pl.pallas_call(
        functools.partial(_inproj_kernel, transposed=transposed),
        grid=(n,),
        in_specs=[row_blk(d), _full((1, d)), _full(w_main.shape), _full(w_gate.shape), tab, tab],
        out_specs=out_specs,
        out_shape=out_shape,
        compiler_params=_params("parallel"),
        name="nsa_inproj",
    )(x, g, w_main, w_gate, cos_t, sin_t)


_KV_CHUNKS = 2 * KV_W // LANES


def _compress_half(load_rows, kv, nrows, wbd_ref, pe_ref):
    accs = []
    for r in range(CMP_BLOCK // CMP_STRIDE):
        acc = jnp.zeros((nrows, KV_W), F32)
        for s in range(CMP_STRIDE):
            idx = (kv * 2 + r) * CMP_STRIDE + s
            lhs = (load_rows(s, kv) + pe_ref[idx:idx + 1, :]).astype(BF16)
            acc = acc + _dot(lhs, wbd_ref[idx])
        accs.append(acc)
    return accs


def _compress_prompt_kernel(*refs, nsb):
    x_refs = refs[:_KV_CHUNKS]
    wbd_ref, pe_ref, w2_ref, kc_ref, vcT_ref, sh_ref = refs[_KV_CHUNKS:]

    def load_rows(s, kv):
        per_half = _KV_CHUNKS // 2
        return jnp.concatenate([x_refs[kv * per_half + c][0, pl.ds(s, nsb, stride=CMP_STRIDE), :]
                                for c in range(per_half)], axis=1)

    sh_ref[nsb:nsb + SUBLANES, :] = jnp.zeros((SUBLANES, KV_W), F32)
    for kv in range(2):
        pr0, pr1 = _compress_half(load_rows, kv, nsb, wbd_ref, pe_ref)
        sh_ref[0:nsb, :] = pr1
        h = pr0 + sh_ref[pl.ds(1, nsb), :]
        out = _dot(jax.nn.gelu(h).astype(BF16), w2_ref[kv])
        if kv == 0:
            for hh in range(N_KV_HEADS):
                kc_ref[0, hh] = out[:, hh * HEAD_DIM:(hh + 1) * HEAD_DIM].astype(BF16)
        else:
            vcT_ref[0] = out.T.astype(BF16)


def _compress_prompt(kvc3, wbd, pe_t, w2bd):
    b, t, _ = kvc3.shape
    nsb = t // CMP_STRIDE
    return pl.pallas_call(
        functools.partial(_compress_prompt_kernel, nsb=nsb),
        grid=(b,),
        in_specs=[pl.BlockSpec((1, t, LANES), lambda i, c=c: (i, 0, c)) for c in range(_KV_CHUNKS)]
        + [_full(wbd.shape), _full(pe_t.shape), _full(w2bd.shape)],
        out_specs=[pl.BlockSpec((1, N_KV_HEADS, nsb, HEAD_DIM), lambda i: (i, 0, 0, 0)),
                   pl.BlockSpec((1, KV_W, nsb), lambda i: (i, 0, 0))],
        out_shape=[jax.ShapeDtypeStruct((b, N_KV_HEADS, nsb, HEAD_DIM), BF16),
                   jax.ShapeDtypeStruct((b, KV_W, nsb), BF16)],
        scratch_shapes=[pltpu.VMEM((nsb + SUBLANES, KV_W), F32)],
        compiler_params=_params("parallel"),
        name="nsa_compress_prompt",
    )(*([kvc3] * _KV_CHUNKS), wbd, pe_t, w2bd)


def _topk_mask_T(imp, n_top):
    ns, w = imp.shape
    nblk = ns // SUBLANES
    blocks = [imp[r * SUBLANES:(r + 1) * SUBLANES, :] for r in range(nblk)]
    cnts = [jnp.zeros((SUBLANES, w), F32) for _ in range(nblk)]
    sub = lax.broadcasted_iota(jnp.int32, (SUBLANES, w), 0)
    for sp in range(ns):
        row = blocks[sp // SUBLANES][sp % SUBLANES:sp % SUBLANES + 1, :]
        for r in range(nblk):
            blk = blocks[r]
            if sp < r * SUBLANES:
                beats = jnp.where(row >= blk, 1.0, 0.0)
            elif sp >= (r + 1) * SUBLANES:
                beats = jnp.where(row > blk, 1.0, 0.0)
            else:
                beats = jnp.where(sub > (sp - r * SUBLANES), jnp.where(row >= blk, 1.0, 0.0),
                                  jnp.where(row > blk, 1.0, 0.0))
            cnts[r] = cnts[r] + beats
    return jnp.concatenate([jnp.where(c < n_top, 1.0, 0.0) for c in cnts], axis=0)


def _online_chunks(states, k_cs, vT_cs, qTs, biases):
    scores = [_dot(k_c, qT) for k_c, qT in zip(k_cs, qTs)]
    mids = []
    for (m, l, _), s, bias in zip(states, scores, biases):
        s = s + bias
        m_new = jnp.maximum(m, jnp.max(s, axis=0, keepdims=True))
        alpha = jnp.exp(m - m_new)
        p = jnp.exp(s - m_new)
        mids.append((m_new, alpha * l + jnp.sum(p, axis=0, keepdims=True), alpha, p.astype(BF16)))
    return tuple((m_new, l_new, alpha * acc + _dot(vT_c, p))
                 for (m_new, l_new, alpha, p), (_, _, acc), vT_c in zip(mids, states, vT_cs))


def _softmax_init(w):
    return (jnp.full((1, w), NEG, F32), jnp.zeros((1, w), F32), jnp.zeros((HEAD_DIM, w), F32))


def _softmax_finish(carry):
    _, l, acc = carry
    return acc * (1.0 / jnp.maximum(l, 1e-20))


def _attn_prompt_kernel(qT_ref, qrT_ref, gT_ref, kc_ref, vcT_ref, ks_ref, vsT_ref, kw_ref, vwT_ref, ovT_ref,
                        o_ref, selb_ref, oc_ref, *, tq, nc, n_top):
    ck = tq
    i = pl.program_id(1)
    t0 = i * tq
    qpos = t0 + lax.broadcasted_iota(jnp.int32, (1, tq), 1)
    ncp = kc_ref.shape[2]
    ns = ovT_ref.shape[0]
    w = GROUP * tq
    bpc = ck // SEL_BLOCK
    sel_shift = SEL_BLOCK.bit_length() - 1
    kvhs = range(N_KV_HEADS)
    heads = [[kvh * GROUP + g for g in range(GROUP)] for kvh in kvhs]
    rows = [pl.ds(kvh * HEAD_DIM, HEAD_DIM) for kvh in kvhs]
    qrT = [jnp.concatenate([qrT_ref[h * HEAD_DIM:(h + 1) * HEAD_DIM, :] for h in heads[kvh]], axis=1)
           for kvh in kvhs]
    kl = lax.broadcasted_iota(jnp.int32, (ck, tq), 0)
    ql = lax.broadcasted_iota(jnp.int32, (ck, tq), 1)
    key_le_query = jnp.where(kl <= ql, 0.0, MASKED)
    key_ge_query = jnp.where(kl >= ql, 0.0, MASKED)
    tile4 = lambda b: jnp.concatenate([b] * GROUP, axis=1)

    cidx = lax.broadcasted_iota(jnp.int32, (ncp, 1), 0)
    valid = ((cidx * CMP_STRIDE + (CMP_BLOCK - 1)) <= qpos) & (cidx < nc)
    sidx = lax.broadcasted_iota(jnp.int32, (ns, 1), 0)
    cur = lax.shift_right_logical(qpos, sel_shift)
    forced = (sidx == 0) | (sidx == cur) | (sidx == cur - 1)
    causal = (sidx * SEL_BLOCK) <= qpos
    cmp_scores = [
        _dot(kc_ref[0, kvh], jnp.concatenate([qT_ref[h * HEAD_DIM:(h + 1) * HEAD_DIM, :] for h in heads[kvh]], axis=1))
        for kvh in kvhs]
    for kvh in kvhs:
        s = cmp_scores[kvh]
        probs = []
        for g in range(GROUP):
            sm = jnp.where(valid, s[:, g * tq:(g + 1) * tq], NEG)
            mx = jnp.max(sm, axis=0, keepdims=True)
            e = jnp.where(valid, jnp.exp(sm - mx), 0.0)
            den = jnp.maximum(jnp.sum(e, axis=0, keepdims=True), 1e-20)
            probs.append(e / den)
        oc_ref[kvh] = _dot(vcT_ref[0, rows[kvh], :], jnp.concatenate(probs, axis=1).astype(BF16))
        psum = probs[0]
        for g in range(1, GROUP):
            psum = psum + probs[g]
        imp = _dot_f32lhs(ovT_ref[...], psum)
        imp = jnp.where(forced, FORCE, jnp.where(causal, imp, NEG))
        selb = (_topk_mask_T(imp, n_top) - 1.0) * (-MASKED)
        for c in range(ns // bpc):
            selb_ref[kvh, c, 0:bpc, :] = selb[c * bpc:(c + 1) * bpc, :]

    n_back = WINDOW // ck
    states = tuple(_softmax_init(w) for _ in kvhs)
    for r in range(n_back + 1):
        a = i - n_back + r
        kst = pl.multiple_of(jnp.maximum(a, 0) * ck, ck)
        if r == n_back:
            bias = key_le_query
        else:
            skip = jnp.where(a < 0, MASKED, 0.0)
            bias = (jnp.minimum(key_ge_query, skip) if r == 0 else jnp.broadcast_to(skip, (ck, tq)))
        states = _online_chunks(states, [kw_ref[kvh, pl.ds(kst, ck), :] for kvh in kvhs],
                                [vwT_ref[rows[kvh], pl.ds(kst, ck)] for kvh in kvhs], qrT,
                                [tile4(bias)] * N_KV_HEADS)
    ow = [_softmax_finish(st) for st in states]

    def sel_bias(kvh, c):
        tile = selb_ref[kvh, c]
        return jnp.concatenate([jnp.broadcast_to(tile[b:b + 1, :], (SEL_BLOCK, tq)) for b in range(bpc)], axis=0)

    def chunk_step(c, states, diagonal):
        kst = pl.multiple_of(c * ck, ck)
        biases = [sel_bias(kvh, c) for kvh in kvhs]
        if diagonal:
            biases = [jnp.minimum(b, key_le_query) for b in biases]
        return _online_chunks(states, [ks_ref[kvh, pl.ds(kst, ck), :] for kvh in kvhs],
                              [vsT_ref[rows[kvh], pl.ds(kst, ck)] for kvh in kvhs], qrT,
                              [tile4(b) for b in biases])

    states = lax.fori_loop(0, i, lambda c, st: chunk_step(c, st, False), tuple(_softmax_init(w) for _ in kvhs))
    states = chunk_step(i, states, True)

    for kvh in kvhs:
        def gate_row(j, kvh=kvh):
            return jnp.concatenate([gT_ref[h * 3 + j:h * 3 + j + 1, :] for h in heads[kvh]], axis=1)

        oT = gate_row(0) * oc_ref[kvh] + gate_row(1) * _softmax_finish(states[kvh]) + gate_row(2) * ow[kvh]
        for g, h in enumerate(heads[kvh]):
            o_ref[h * HEAD_DIM:(h + 1) * HEAD_DIM, :] = oT[:, g * tq:(g + 1) * tq].astype(BF16)


def _attn_prompt(qT, qrT, gT, kc, vcT, ksb, vsT, kwb, vwT, ovT, *, batch, seq, tq, nc, n_top):
    nq = seq // tq
    nsb = kc.shape[2]
    ns = ovT.shape[0]
    col = lambda h: pl.BlockSpec((h, tq), lambda b, i: (0, b * nq + i))
    kh = pl.BlockSpec((N_KV_HEADS, seq, HEAD_DIM), lambda b, i: (0, b, 0))
    vt = pl.BlockSpec((KV_W, seq), lambda b, i: (0, b))
    return pl.pallas_call(
        functools.partial(_attn_prompt_kernel, tq=tq, nc=nc, n_top=n_top),
        grid=(batch, nq),
        in_specs=[col(Q_W), col(Q_W), col(LANES),
                  pl.BlockSpec((1, N_KV_HEADS, nsb, HEAD_DIM), lambda b, i: (b, 0, 0, 0)),
                  pl.BlockSpec((1, KV_W, nsb), lambda b, i: (b, 0, 0)),
                  kh, vt, kh, vt, _full(ovT.shape)],
        out_specs=col(Q_W),
        out_shape=jax.ShapeDtypeStruct((Q_W, batch * seq), BF16),
        scratch_shapes=[pltpu.VMEM((N_KV_HEADS, ns * SEL_BLOCK // tq, SUBLANES, tq), F32),
                        pltpu.VMEM((N_KV_HEADS, HEAD_DIM, GROUP * tq), F32)],
        compiler_params=_params("parallel", "arbitrary"),
        name="nsa_attn_prompt",
    )(qT, qrT, gT, kc, vcT, ksb, vsT, kwb, vwT, ovT)


def _oproj_kernel(o_ref, w_ref, x_ref, g1_ref, g2_ref, h_ref, xm_ref, *, transposed):
    if transposed:
        y = lax.dot_general(o_ref[...], w_ref[...], (((0,), (0,)), ((), ())), preferred_element_type=F32)
    else:
        y = _dot(o_ref[...], w_ref[...])
    h = x_ref[...] + _rms(y, g1_ref[...])
    h_ref[...] = h
    xm_ref[...] = _rms(h, g2_ref[...]).astype(BF16)


def _oproj(o, w_o, x, g1, g2, *, tm, transposed):
    rows, d = x.shape
    o_spec = (pl.BlockSpec((Q_W, tm), lambda i: (0, i)) if transposed else pl.BlockSpec((tm, Q_W), lambda i: (i, 0)))
    row = pl.BlockSpec((tm, d), lambda i: (i, 0))
    return pl.pallas_call(
        functools.partial(_oproj_kernel, transposed=transposed),
        grid=(rows // tm,),
        in_specs=[o_spec, _full(w_o.shape), row, _full((1, d)), _full((1, d))],
        out_specs=[row, row],
        out_shape=[jax.ShapeDtypeStruct((rows, d), F32), jax.ShapeDtypeStruct((rows, d), BF16)],
        compiler_params=_params("parallel"),
        name="nsa_oproj",
    )(o, w_o, x, g1, g2)


def _mlp_kernel(xm_ref, h_ref, w1_ref, w2_ref, g3_ref, gn_ref, *rest, ff_chunk, next_norm):
    if next_norm:
        h2_ref, xn_ref, acc_ref = rest
    else:
        h2_ref, acc_ref = rest
    xm = xm_ref[...]
    for c in range(w1_ref.shape[1] // ff_chunk):
        cols = slice(c * ff_chunk, (c + 1) * ff_chunk)
        hm = jnp.maximum(_dot(xm, w1_ref[:, cols]), 0.0)
        part = _dot((hm * hm).astype(BF16), w2_ref[cols, :])
        if c == 0:
            acc_ref[...] = part
        else:
            acc_ref[...] += part
    h2 = h_ref[...] + _rms(acc_ref[...], g3_ref[...])
    h2_ref[...] = h2
    if next_norm:
        xn_ref[...] = _rms(h2, gn_ref[...])


def _mlp(xm, h, w1, w2, g3, gn, *, tm, ff_chunk, next_norm):
    rows, d = h.shape
    row = pl.BlockSpec((tm, d), lambda i: (i, 0))
    out_shape = [jax.ShapeDtypeStruct((rows, d), F32)] * (2 if next_norm else 1)
    return pl.pallas_call(
        functools.partial(_mlp_kernel, ff_chunk=ff_chunk, next_norm=next_norm),
        grid=(rows // tm,),
        in_specs=[row, row, _full(w1.shape), _full(w2.shape), _full((1, d)), _full((1, d))],
        out_specs=[row] * len(out_shape),
        out_shape=out_shape,
        scratch_shapes=[pltpu.VMEM((tm, d), F32)],
        compiler_params=_params("parallel"),
        name="sq_relu_mlp",
    )(xm, h, w1, w2, g3, gn)


def _s5_chunk_kernel(u_ref, w_ref, pre_ref, pim_ref, qre_ref, qim_ref, are_ref, aim_ref, y_ref, hfin_ref,
                     sre, sim, hre, him, *, nb, nk):
    steps_per_iter = SUBLANES // nb
    sw = sre.shape[1]

    @pl.when(pl.program_id(1) == 0)
    def _():
        u = u_ref[0]
        sre[...] = _dot(u, pre_ref[0])
        sim[...] = _dot(u, pim_ref[0])
        ar = are_ref[0]
        ai = aim_ref[0]

        def body(it, carry):
            hr, hi = carry
            r0 = pl.multiple_of(it * SUBLANES, SUBLANES)
            sr8 = sre[pl.ds(r0, SUBLANES), :]
            si8 = sim[pl.ds(r0, SUBLANES), :]
            prev_r, prev_i = [], []
            for j in range(steps_per_iter):
                prev_r.append(hr)
                prev_i.append(hi)
                sr = sr8[j * nb:(j + 1) * nb, :]
                si = si8[j * nb:(j + 1) * nb, :]
                hr, hi = ar * hr - ai * hi + sr, ar * hi + ai * hr + si
            hre[pl.ds(r0, SUBLANES), :] = jnp.concatenate(prev_r, axis=0)
            him[pl.ds(r0, SUBLANES), :] = jnp.concatenate(prev_i, axis=0)
            return hr, hi

        zero = jnp.zeros((nb, sw), F32)
        hr, hi = lax.fori_loop(0, nk // steps_per_iter, body, (zero, zero))
        hfin_ref[0, 0] = hr
        hfin_ref[0, 1] = hi

    y_ref[0] = (_dot(u_ref[0], w_ref[0]) + _dot(hre[...].astype(BF16), qre_ref[0])
                + _dot(him[...].astype(BF16), qim_ref[0]))


def _s5_chunk(u2, ops, *, nb, nk):
    nunit, nch, width = u2.shape
    sw = ops["p_re"].shape[2]
    ncol = 2 if width % (2 * LANES) == 0 else 1
    wc = width // ncol
    per_unit = lambda a: pl.BlockSpec((1,) + a.shape[1:], lambda i, h: (i,) + (0,) * (a.ndim - 1))
    col_half = lambda rows: pl.BlockSpec((1, rows, wc), lambda i, h: (i, 0, h))
    return pl.pallas_call(
        functools.partial(_s5_chunk_kernel, nb=nb, nk=nk),
        grid=(nunit, ncol),
        in_specs=[per_unit(u2), col_half(width), per_unit(ops["p_re"]), per_unit(ops["p_im"]), col_half(sw),
                  col_half(sw), per_unit(ops["a_chunk_re"]), per_unit(ops["a_chunk_im"])],
        out_specs=[col_half(nch), pl.BlockSpec((1, 2, nb, sw), lambda i, h: (i, 0, 0, 0))],
        out_shape=[jax.ShapeDtypeStruct((nunit, nch, width), F32), jax.ShapeDtypeStruct((nunit, 2, nb, sw), F32)],
        scratch_shapes=[pltpu.VMEM((nch, sw), F32)] * 4,
        compiler_params=_params("parallel", "arbitrary"),
        name="s5_chunk_scan",
    )(u2, ops["w"], ops["p_re"], ops["p_im"], ops["q_re"], ops["q_im"], ops["a_chunk_re"], ops["a_chunk_im"])


def _s5_step_kernel(u_ref, h0r_ref, h0i_ref, bre_ref, bim_ref, cre_ref, cim_ref, are_ref, aim_ref,
                    y_ref, hr_ref, hi_ref, *, npair):
    for p in range(npair):
        u = u_ref[p]
        ar = are_ref[p]
        ai = aim_ref[p]
        h0r = h0r_ref[p]
        h0i = h0i_ref[p]
        hr = ar * h0r - ai * h0i + _dot(u, bre_ref[p])
        hi = ar * h0i + ai * h0r + _dot(u, bim_ref[p])
        hr_ref[p] = hr
        hi_ref[p] = hi
        y_ref[p] = _dot(hr.astype(BF16), cre_ref[p]) + _dot(hi.astype(BF16), cim_ref[p])


def _s5_step(u2, h0r, h0i, ops):
    npair, rows, width = u2.shape
    args = (u2, h0r, h0i, ops["b1_re"], ops["b1_im"], ops["c1_re"], ops["c1_im"], ops["a1_re"], ops["a1_im"])
    return pl.pallas_call(
        functools.partial(_s5_step_kernel, npair=npair),
        grid=(1,),
        in_specs=[_full(a.shape) for a in args],
        out_specs=[_full((npair, rows, width)), _full(h0r.shape), _full(h0r.shape)],
        out_shape=[jax.ShapeDtypeStruct((npair, rows, width), F32), jax.ShapeDtypeStruct(h0r.shape, F32),
                   jax.ShapeDtypeStruct(h0r.shape, F32)],
        compiler_params=_params("arbitrary"),
        name="s5_single_step",
    )(*args)


def _s5_out_kernel(yc_ref, u_ref, d_ref, wg_ref, bg_ref, h_ref, g1_ref, g2_ref, h3_ref, xm_ref):
    y = jax.nn.gelu(yc_ref[...] + d_ref[...] * u_ref[...])
    out = y * jax.nn.sigmoid(_dot(y.astype(BF16), wg_ref[...]) + bg_ref[...])
    h3 = h_ref[...] + _rms(out, g1_ref[...])
    h3_ref[...] = h3
    xm_ref[...] = _rms(h3, g2_ref[...]).astype(BF16)


def _s5_out(yc, u, d_skip, w_glu, b_glu, h, g1, g2, *, tm):
    rows, d = h.shape
    row = pl.BlockSpec((tm, d), lambda i: (i, 0))
    vec = _full((1, d))
    return pl.pallas_call(
        _s5_out_kernel,
        grid=(rows // tm,),
        in_specs=[row, row, vec, _full(w_glu.shape), vec, row, vec, vec],
        out_specs=[row, row],
        out_shape=[jax.ShapeDtypeStruct((rows, d), F32), jax.ShapeDtypeStruct((rows, d), BF16)],
        compiler_params=_params("parallel"),
        name="s5_glu_out",
    )(yc, u, d_skip, w_glu, b_glu, h, g1, g2)


def _s5_operators(a_re, a_im, log_dt, b_re, b_im, c_re, c_im):
    hp = lax.Precision.HIGHEST
    g, n = a_re.shape
    npair = g // SSM_UNIT
    L = SSM_CHUNK
    a = lax.complex(a_re.astype(F32), a_im.astype(F32))
    dt = jnp.exp(log_dt.astype(F32))[:, None]
    a_bar = jnp.exp(a * dt)
    b_bar = ((a_bar - 1.0) / a)[:, :, None] * lax.complex(b_re.astype(F32), b_im.astype(F32))
    c = lax.complex(c_re.astype(F32), c_im.astype(F32))
    pows = [jnp.ones_like(a_bar)]
    for _ in range(L):
        pows.append(pows[-1] * a_bar)
    a_pow = jnp.stack(pows)
    kern = jnp.real(jnp.einsum("gcn,tgn,gnd->gtcd", c, a_pow[:L], b_bar, precision=hp))
    t1 = jnp.arange(L)[:, None]
    t2 = jnp.arange(L)[None, :]
    lag = jnp.clip(t2 - t1, 0, L - 1)
    wg = jnp.where((t2 >= t1)[None, :, :, None, None], kern[:, lag], 0.0)
    gu = SSM_UNIT
    uw = gu * SSM_GROUP
    eye2 = jnp.eye(gu, dtype=F32)
    wg = wg.reshape(npair, gu, L, L, SSM_GROUP, SSM_GROUP)
    w = jnp.einsum("pgtscd,gh->ptgdshc", wg, eye2).reshape(npair, L * uw, L * uw)
    pc = a_pow[L - 1 - jnp.arange(L)][:, :, :, None] * b_bar[None]
    pc = pc.reshape(L, npair, gu, n, SSM_GROUP)

    def p_mat(x):
        return jnp.einsum("tpgnd,gh->ptgdhn", x, eye2).reshape(npair, L * uw, gu * n)

    qc = c[None] * a_pow[1:L + 1][:, :, None, :]
    qc = qc.reshape(L, npair, gu, SSM_GROUP, n)

    def q_mat(x):
        return jnp.einsum("spgcn,gh->pgnshc", x, eye2).reshape(npair, gu * n, L * uw)

    def pair_row(x):
        return x.reshape(npair, 1, gu * n)

    b1 = b_bar.reshape(npair, gu, n, SSM_GROUP)
    c1 = c.reshape(npair, gu, SSM_GROUP, n)

    def b1_mat(x):
        return jnp.einsum("pgnd,gh->pgdhn", x, eye2).reshape(npair, uw, gu * n)

    def c1_mat(x):
        return jnp.einsum("pgcn,gh->pgnhc", x, eye2).reshape(npair, gu * n, uw)

    return {
        "w": w.astype(BF16),
        "p_re": p_mat(jnp.real(pc)).astype(BF16), "p_im": p_mat(jnp.imag(pc)).astype(BF16),
        "q_re": q_mat(jnp.real(qc)).astype(BF16), "q_im": q_mat(-jnp.imag(qc)).astype(BF16),
        "a_chunk_re": pair_row(jnp.real(a_pow[L])), "a_chunk_im": pair_row(jnp.imag(a_pow[L])),
        "a1_re": pair_row(jnp.real(a_bar)), "a1_im": pair_row(jnp.imag(a_bar)),
        "b1_re": b1_mat(jnp.real(b1)).astype(BF16), "b1_im": b1_mat(jnp.imag(b1)).astype(BF16),
        "c1_re": c1_mat(jnp.real(c1)).astype(BF16), "c1_im": c1_mat(-jnp.imag(c1)).astype(BF16),
    }


def _compress_paged_kernel(pt_ref, *refs, pg, nsb):
    del pt_ref
    pages = refs[:pg]
    wbd_ref, pe_ref, w2_ref, out_ref, h0_ref, h1_ref, xs_ref = refs[pg:]
    j = pl.program_id(1)
    sbp = PAGE_SIZE // CMP_STRIDE
    nrows = pg * sbp
    per_half = _KV_CHUNKS // 2

    for i, r in enumerate(pages):
        for kv in range(2):
            x = r[0, kv].T
            for c in range(per_half):
                xs_ref[kv * per_half + c, i * PAGE_SIZE:(i + 1) * PAGE_SIZE, :] = x[:, c * LANES:(c + 1) * LANES]

    def load_rows(s, kv):
        return jnp.concatenate([xs_ref[kv * per_half + c, pl.ds(s, nrows, stride=CMP_STRIDE), :]
                                for c in range(per_half)], axis=1)

    @pl.when(j == 0)
    def _():
        h1_ref[:, nsb:nsb + SUBLANES, :] = jnp.zeros((2, SUBLANES, KV_W), F32)

    r0 = pl.multiple_of(j * nrows, nrows)
    for kv in range(2):
        pr0, pr1 = _compress_half(load_rows, kv, nrows, wbd_ref, pe_ref)
        h0_ref[kv, pl.ds(r0, nrows), :] = pr0
        h1_ref[kv, pl.ds(r0, nrows), :] = pr1

    @pl.when(j == pl.num_programs(1) - 1)
    def _():
        for kv in range(2):
            h = h0_ref[kv] + h1_ref[kv, pl.ds(1, nsb), :]
            out = _dot(jax.nn.gelu(h).astype(BF16), w2_ref[kv])
            out_ref[0, :, kv * KV_W:(kv + 1) * KV_W] = out.astype(BF16)


def _page_specs(pg, pages_per_sample):
    def spec(i):
        return pl.BlockSpec((1, 2, KV_W, PAGE_SIZE),
                            lambda b, j, pt: (pt[b * pages_per_sample + j * pg + i], 0, 0, 0))
    return [spec(i) for i in range(pg)]


def _compress_paged(pages, pt_flat, wbd, pe_t, w2bd, *, nb, pages_per_sample, pg):
    nsb = pages_per_sample * PAGE_SIZE // CMP_STRIDE
    c3 = lambda shape: pl.BlockSpec(shape, lambda b, j, pt: (0,) * len(shape))
    grid_spec = pltpu.PrefetchScalarGridSpec(
        num_scalar_prefetch=1,
        grid=(nb, pages_per_sample // pg),
        in_specs=_page_specs(pg, pages_per_sample) + [c3(wbd.shape), c3(pe_t.shape), c3(w2bd.shape)],
        out_specs=pl.BlockSpec((1, nsb, 2 * KV_W), lambda b, j, pt: (b, 0, 0)),
        scratch_shapes=[pltpu.VMEM((2, nsb, KV_W), F32), pltpu.VMEM((2, nsb + SUBLANES, KV_W), F32),
                        pltpu.VMEM((_KV_CHUNKS, pg * PAGE_SIZE, LANES), F32)],
    )
    return pl.pallas_call(
        functools.partial(_compress_paged_kernel, pg=pg, nsb=nsb),
        grid_spec=grid_spec,
        out_shape=jax.ShapeDtypeStruct((nb, nsb, 2 * KV_W), BF16),
        compiler_params=_params("parallel", "arbitrary"),
        name="nsa_compress_paged",
    )(pt_flat, *([pages] * pg), wbd, pe_t, w2bd)


def _topk_mask_lanes(imp, n_top, ns_valid):
    lane = lax.broadcasted_iota(jnp.int32, imp.shape, 1)
    cnt = jnp.zeros(imp.shape, F32)
    for sp in range(ns_valid):
        col = imp[:, sp:sp + 1]
        cnt = cnt + jnp.where(lane > sp, jnp.where(col >= imp, 1.0, 0.0), jnp.where(col > imp, 1.0, 0.0))
    return jnp.where((cnt < n_top) & (lane < ns_valid), 1.0, 0.0)


def _dot_nt(a, b):
    return lax.dot_general(a, b, (((1,), (1,)), ((), ())), preferred_element_type=F32)


def _attn_sample_kernel(pt_ref, *refs, pg, past, nc, ns_valid, n_top):
    del pt_ref
    pages = refs[:pg]
    (q_ref, qr_ref, g_ref, kcvc_ref, ksn_ref, kwn_ref, win_ref, ov_ref, e_ref, gs_ref, o_ref,
     m_ref, l_ref, acc_ref, sel_ref, oc_ref, ow_ref) = refs[pg:]
    j = pl.program_id(1)
    ncp = kcvc_ref.shape[1]
    nsp = ov_ref.shape[1]
    wlen = win_ref.shape[3]
    row = lax.broadcasted_iota(jnp.int32, (N_HEADS, KV_W), 0)
    lane = lax.broadcasted_iota(jnp.int32, (N_HEADS, KV_W), 1)
    own = (lane // HEAD_DIM) == (row // GROUP)

    def spread(ref):
        q = ref[0]
        return jnp.where(own, jnp.concatenate([q] * N_KV_HEADS, axis=1), jnp.zeros((N_HEADS, KV_W), BF16))

    def update(state, s, vs):
        m, l, acc = state
        m_new = jnp.maximum(m, jnp.max(s, axis=1, keepdims=True))
        alpha = jnp.exp(m - m_new)
        p = jnp.exp(s - m_new)
        l = alpha * l + jnp.sum(p, axis=1, keepdims=True)
        pv = None
        for st, sz, v, feature_major in vs:
            pb = p[:, st:st + sz].astype(BF16)
            t = _dot_nt(pb, v) if feature_major else _dot(pb, v)
            pv = t if pv is None else pv + t
        return m_new, l, alpha * acc + pv

    def init():
        return (jnp.full((N_HEADS, 1), NEG, F32), jnp.zeros((N_HEADS, 1), F32), jnp.zeros((N_HEADS, KV_W), F32))

    def new_row_update(state, qbd, new_ref):
        r8 = lax.broadcasted_iota(jnp.int32, (SUBLANES, 2 * KV_W), 0)
        tile = jnp.where(r8 == 0, jnp.broadcast_to(new_ref[0], (SUBLANES, 2 * KV_W)), 0.0).astype(BF16)
        s = _dot_nt(qbd, tile[:, 0:KV_W])
        l8 = lax.broadcasted_iota(jnp.int32, (N_HEADS, SUBLANES), 1)
        s = jnp.where(l8 == 0, s, MASKED)
        return update(state, s, [(0, SUBLANES, tile[:, KV_W:2 * KV_W], False)])

    def put(state):
        m, l, acc = state
        m_ref[...] = jnp.broadcast_to(m, m_ref.shape)
        l_ref[...] = jnp.broadcast_to(l, l_ref.shape)
        acc_ref[...] = acc

    qrbd = spread(qr_ref)

    @pl.when(j == 0)
    def _():
        s = _dot_nt(spread(q_ref), kcvc_ref[0, :, 0:KV_W])
        cidx = lax.broadcasted_iota(jnp.int32, (1, ncp), 1)
        valid = ((cidx * CMP_STRIDE + (CMP_BLOCK - 1)) <= past) & (cidx < nc)
        sm = jnp.where(valid, s, NEG)
        mx = jnp.max(sm, axis=1, keepdims=True)
        e = jnp.where(valid, jnp.exp(sm - mx), 0.0)
        p = e / jnp.maximum(jnp.sum(e, axis=1, keepdims=True), 1e-20)
        oc_ref[...] = _dot(p.astype(BF16), kcvc_ref[0, :, KV_W:2 * KV_W])
        imp = _dot_f32rhs(_dot_f32lhs(gs_ref[...], p), ov_ref[...])
        sidx = lax.broadcasted_iota(jnp.int32, (1, nsp), 1)
        cur = past // SEL_BLOCK
        forced = (sidx == 0) | (sidx == cur) | (sidx == cur - 1)
        causal = (sidx * SEL_BLOCK) <= past
        imp = jnp.where(forced, FORCE, jnp.where(causal, imp, NEG))
        imp = jnp.where(sidx < ns_valid, imp, MASKED)
        sel_ref[...] = _topk_mask_lanes(imp, n_top, ns_valid)
        sw = _dot(qrbd, win_ref[0, 0].astype(BF16))
        wpos = past - wlen + lax.broadcasted_iota(jnp.int32, (1, wlen), 1)
        sw = jnp.where((wpos >= 0) & (past - wpos <= WINDOW), sw, MASKED)
        st = update(init(), sw, [(0, wlen, win_ref[0, 1].astype(BF16), True)])
        st = new_row_update(st, qrbd, kwn_ref)
        ow_ref[...] = st[2] * (1.0 / jnp.maximum(st[1], 1e-20))
        put(new_row_update(init(), qrbd, ksn_ref))

    s = jnp.concatenate([_dot(qrbd, r[0, 0].astype(BF16)) for r in pages], axis=1)
    mask = _dot(sel_ref[...].astype(BF16), e_ref[0]) > 0.5
    s = jnp.where(mask, s, MASKED)
    vs = [(i * PAGE_SIZE, PAGE_SIZE, r[0, 1].astype(BF16), True) for i, r in enumerate(pages)]
    put(update((m_ref[:, 0:1], l_ref[:, 0:1], acc_ref[...]), s, vs))

    @pl.when(j == pl.num_programs(1) - 1)
    def _():
        g = g_ref[0]
        os_ = acc_ref[...] * (1.0 / jnp.maximum(l_ref[:, 0:1], 1e-20))
        o = g[:, 0:1] * oc_ref[...] + g[:, 1:2] * os_ + g[:, 2:3] * ow_ref[...]
        o = jnp.where(own, o, 0.0)
        out = o[:, 0:HEAD_DIM]
        for h in range(1, N_KV_HEADS):
            out = out + o[:, h * HEAD_DIM:(h + 1) * HEAD_DIM]
        o_ref[0] = out.astype(BF16)


def _attn_sample(pages, pt_flat, q3, qr3, g3, kcvc, ks_new, kw_new, win, ov, e_mat, gsum, *, nb, pages_per_sample,
                 pg, past, nc, ns_valid, n_top):
    nsp = ov.shape[1]
    per_b = lambda shape: pl.BlockSpec((1,) + shape, lambda b, j, pt: (b,) + (0,) * len(shape))
    const = lambda shape: pl.BlockSpec(shape, lambda b, j, pt: (0,) * len(shape))
    grid_spec = pltpu.PrefetchScalarGridSpec(
        num_scalar_prefetch=1,
        grid=(nb, pages_per_sample // pg),
        in_specs=_page_specs(pg, pages_per_sample) + [
            per_b((N_HEADS, HEAD_DIM)), per_b((N_HEADS, HEAD_DIM)), per_b((N_HEADS, 3)),
            per_b(kcvc.shape[1:]), per_b((1, 2 * KV_W)), per_b((1, 2 * KV_W)), per_b(win.shape[1:]),
            const(ov.shape), pl.BlockSpec((1,) + e_mat.shape[1:], lambda b, j, pt: (j, 0, 0)), const(gsum.shape)],
        out_specs=per_b((N_HEADS, HEAD_DIM)),
        scratch_shapes=[pltpu.VMEM((N_HEADS, LANES), F32), pltpu.VMEM((N_HEADS, LANES), F32),
                        pltpu.VMEM((N_HEADS, KV_W), F32), pltpu.VMEM((N_HEADS, nsp), F32),
                        pltpu.VMEM((N_HEADS, KV_W), F32), pltpu.VMEM((N_HEADS, KV_W), F32)],
    )
    return pl.pallas_call(
        functools.partial(_attn_sample_kernel, pg=pg, past=past, nc=nc, ns_valid=ns_valid, n_top=n_top),
        grid_spec=grid_spec,
        out_shape=jax.ShapeDtypeStruct((nb, N_HEADS, HEAD_DIM), BF16),
        compiler_params=_params("parallel", "arbitrary"),
        name="nsa_attn_sample",
    )(pt_flat, *([pages] * pg), q3, qr3, g3, kcvc, ks_new, kw_new, win, ov, e_mat, gsum)


def _rope_tables(pos):
    half = HEAD_DIM // 2
    inv = ROPE_THETA ** (-jnp.arange(half, dtype=F32) / half)
    ang = pos.astype(F32)[:, None] * inv[None, :]
    cos = jnp.cos(ang)
    sin = jnp.sin(ang)
    reps = LANES // HEAD_DIM
    return (jnp.tile(jnp.concatenate([cos, cos], axis=1), (1, reps)),
            jnp.tile(jnp.concatenate([-sin, sin], axis=1), (1, reps)))


def _compress_weights(cmp_w1, cmp_w2, cmp_pe):
    ratio = CMP_BLOCK // CMP_STRIDE
    eye = jnp.eye(N_KV_HEADS, dtype=F32)
    w1r = cmp_w1.reshape(2, ratio, CMP_STRIDE, HEAD_DIM, HEAD_DIM)
    wbd = jnp.einsum("krsde,hg->krshdge", w1r, eye).reshape(2 * ratio * CMP_STRIDE, KV_W, KV_W).astype(BF16)
    w2bd = jnp.einsum("kef,hg->khegf", cmp_w2, eye).reshape(2, KV_W, KV_W).astype(BF16)
    pe_t = jnp.tile(cmp_pe.reshape(2 * ratio * CMP_STRIDE, HEAD_DIM), (1, N_KV_HEADS)).astype(F32)
    return wbd, pe_t, w2bd


def _overlap(nc, ncp, nsel, nsp):
    c_start = jnp.arange(ncp, dtype=jnp.int32)[:, None] * CMP_STRIDE
    s_start = jnp.arange(nsp, dtype=jnp.int32)[None, :] * SEL_BLOCK
    ov = (c_start < s_start + SEL_BLOCK) & (c_start + CMP_BLOCK > s_start)
    ov = ov & (jnp.arange(ncp)[:, None] < nc) & (jnp.arange(nsp)[None, :] < nsel)
    return ov.astype(BF16)


def _pick(*cands):
    return next(c for c in cands if c)


def _tile(n, pref):
    t = min(n, pref)
    while n % t:
        t //= 2
    return t


def kernel(x_prompt, x_sample, cache_kv_cmp, cache_kv_sel, cache_kv_win, state_ssm, page_table, norm_g, mlp_w1,
           mlp_w2, nsa_w_in, nsa_w_o, nsa_cmp_w1, nsa_cmp_w2, nsa_cmp_pe, s5_a_re, s5_a_im, s5_log_dt, s5_b_re,
           s5_b_im, s5_c_re, s5_c_im, s5_d, s5_w_glu, s5_b_glu):
    b, t, d = x_prompt.shape
    nb = x_sample.shape[0]
    pages_per_sample = page_table.shape[1]
    past = pages_per_sample * PAGE_SIZE
    rows_p = b * t
    g = norm_g.reshape(norm_g.shape[0], 4, 1, d)

    w_in = nsa_w_in[0]
    w_main = w_in[:, :Q_W + 6 * KV_W].astype(BF16)
    w_gate = jnp.pad(w_in[:, Q_W + 6 * KV_W:], ((0, 0), (0, LANES - 3 * N_HEADS))).astype(BF16)
    w_o = nsa_w_o[0].astype(BF16)
    wbd, pe_t, w2bd = _compress_weights(nsa_cmp_w1[0], nsa_cmp_w2[0], nsa_cmp_pe[0])
    w1 = mlp_w1.astype(BF16)
    w2 = mlp_w2.astype(BF16)
    w_glu = s5_w_glu[0].astype(BF16)
    ops = _s5_operators(s5_a_re[0], s5_a_im[0], s5_log_dt[0], s5_b_re[0], s5_b_im[0], s5_c_re[0], s5_c_im[0])
    d_skip = s5_d[0].reshape(1, d)
    b_glu = s5_b_glu[0].reshape(1, d)

    tm = _tile(rows_p, 512)
    ff_chunk = _tile(mlp_w1.shape[2], 1024)

    xp = x_prompt.reshape(rows_p, d)
    cos_p, sin_p = _rope_tables(jnp.arange(t, dtype=jnp.int32))
    (qT, qrT, gT, kvc, kvcT, kvsT, kvwT, ksb, kwb, vsT, vwT) = _inproj(
        xp, g[0, 0], w_main, w_gate, cos_p, sin_p, tm=_tile(t, 512), pos_blocks=t // _tile(t, 512), transposed=True)
    nsb_p = t // CMP_STRIDE
    nc_p = nsb_p - CMP_BLOCK // CMP_STRIDE + 1
    nsel_p = t // SEL_BLOCK
    kc, vcT = _compress_prompt(kvc.reshape(b, t, 2 * KV_W), wbd, pe_t, w2bd)
    ovT = _overlap(nc_p, nsb_p, nsel_p, nsel_p).T
    tq = next(c for c in (2 * LANES, LANES) if t % c == 0 and WINDOW % c == 0)
    oT = _attn_prompt(qT, qrT, gT, kc, vcT, ksb, vsT, kwb, vwT, ovT, batch=b, seq=t, tq=tq, nc=nc_p,
                      n_top=min(TOP_N, nsel_p))
    hp, xm = _oproj(oT, w_o, xp, g[0, 1], g[0, 2], tm=tm, transposed=True)
    hp, xn1 = _mlp(xm, hp, w1[0], w2[0], g[0, 3], g[1, 0], tm=tm, ff_chunk=ff_chunk, next_norm=True)

    pw = SSM_UNIT * SSM_GROUP
    npair = d // pw
    nk = t // SSM_CHUNK
    u2 = xn1.astype(BF16).reshape(b, nk, SSM_CHUNK, npair, pw).transpose(3, 1, 0, 2, 4)
    u2 = u2.reshape(npair, nk * b, SSM_CHUNK * pw)
    y2, hfin = _s5_chunk(u2, ops, nb=b, nk=nk)
    yc = y2.reshape(npair, nk, b, SSM_CHUNK, pw).transpose(2, 1, 3, 0, 4).reshape(rows_p, d)
    hp, xm = _s5_out(yc, xn1, d_skip, w_glu, b_glu, hp, g[1, 1], g[1, 2], tm=tm)
    (hp,) = _mlp(xm, hp, w1[1], w2[1], g[1, 3], g[1, 3], tm=tm, ff_chunk=ff_chunk, next_norm=False)
    ssm_p = hfin.reshape(npair, 2, b, SSM_UNIT, SSM_STATE).transpose(2, 1, 0, 3, 4)
    ssm_p = ssm_p.reshape(b, 2, d // SSM_GROUP, SSM_STATE)

    xs = x_sample.reshape(nb, d)
    cos_s, sin_s = _rope_tables(jnp.full((nb,), past, dtype=jnp.int32))
    q_s, qr_s, gates_s, kvc_s, kvs_s, kvw_s = _inproj(
        xs, g[0, 0], w_main, w_gate, cos_s, sin_s, tm=nb, pos_blocks=1, transposed=False)
    pt_flat = page_table.reshape(-1).astype(jnp.int32)
    pg = _tile(pages_per_sample, PAGE_GROUP)
    n_pool = cache_kv_cmp.shape[1]
    feature_major = lambda c, n, s: c.transpose(0, 2, 3, 4, 1).reshape(n, 2, KV_W, s)
    cmp_pages = feature_major(cache_kv_cmp[0], n_pool, PAGE_SIZE)
    sel_pages = feature_major(cache_kv_sel[0], n_pool, PAGE_SIZE)
    kcvc = _compress_paged(cmp_pages, pt_flat, wbd, pe_t, w2bd, nb=nb, pages_per_sample=pages_per_sample, pg=pg)
    l_all = past + 1
    nsb_s = l_all // CMP_STRIDE
    nc_s = nsb_s - CMP_BLOCK // CMP_STRIDE + 1
    nsel_s = -(-l_all // SEL_BLOCK)
    nsp = -(-nsel_s // LANES) * LANES
    ov_s = _overlap(nc_s, past // CMP_STRIDE, nsel_s, nsp)
    keys_per_step = pg * PAGE_SIZE
    key_blk = (jnp.arange(past, dtype=jnp.int32) // SEL_BLOCK).reshape(past // keys_per_step, 1, keys_per_step)
    e_mat = (jnp.arange(nsp, dtype=jnp.int32)[None, :, None] == key_blk).astype(BF16)
    hh = jnp.arange(N_HEADS)
    gsum = ((hh[:, None] // GROUP) == (hh[None, :] // GROUP)).astype(BF16)
    win = feature_major(cache_kv_win[0], nb, WINDOW)
    o_s = _attn_sample(sel_pages, pt_flat, q_s.reshape(nb, N_HEADS, HEAD_DIM), qr_s.reshape(nb, N_HEADS, HEAD_DIM),
                       gates_s[:, :3 * N_HEADS].reshape(nb, N_HEADS, 3), kcvc, kvs_s.reshape(nb, 1, 2 * KV_W),
                       kvw_s.reshape(nb, 1, 2 * KV_W), win, ov_s, e_mat, gsum, nb=nb,
                       pages_per_sample=pages_per_sample, pg=pg, past=past, nc=nc_s, ns_valid=nsel_s,
                       n_top=min(TOP_N, nsel_s))
    hs, xm_s = _oproj(o_s.reshape(nb, Q_W), w_o, xs, g[0, 1], g[0, 2], tm=nb, transposed=False)
    hs, xn1_s = _mlp(xm_s, hs, w1[0], w2[0], g[0, 3], g[1, 0], tm=nb, ff_chunk=ff_chunk, next_norm=True)

    u2_s = xn1_s.astype(BF16).reshape(nb, npair, pw).transpose(1, 0, 2)
    st = state_ssm[0].reshape(nb, 2, npair, SSM_UNIT * SSM_STATE).transpose(1, 2, 0, 3)
    y2_s, hr_s, hi_s = _s5_step(u2_s, st[0], st[1], ops)
    yc_s = y2_s.transpose(1, 0, 2).reshape(nb, d)
    hs, xm_s = _s5_out(yc_s, xn1_s, d_skip, w_glu, b_glu, hs, g[1, 1], g[1, 2], tm=nb)
    (hs,) = _mlp(xm_s, hs, w1[1], w2[1], g[1, 3], g[1, 3], tm=nb, ff_chunk=ff_chunk, next_norm=False)
    ssm_s = jnp.stack([hr_s, hi_s], axis=0).transpose(2, 0, 1, 3).reshape(nb, 2, d // SSM_GROUP, SSM_STATE)

    kv5 = lambda a, n, s: a.reshape(1, n, s, 2, N_KV_HEADS, HEAD_DIM)
    from_fm = lambda a, n, s: a.reshape(n, 2, N_KV_HEADS, HEAD_DIM, s).transpose(0, 4, 1, 2, 3)[None]
    win_s = jnp.concatenate([win[..., 1:], kvw_s.reshape(nb, 2, KV_W, 1)], axis=-1)
    return (hp.reshape(b, t, d), hs.reshape(nb, 1, d),
            from_fm(kvcT, b, t), kv5(kvc_s, nb, 1), from_fm(kvsT, b, t), kv5(kvs_s, nb, 1),
            from_fm(kvwT[:, :, t - WINDOW:], b, WINDOW), from_fm(win_s, nb, WINDOW), ssm_p[None], ssm_s[None])
```

```python
import functools

import jax
import jax.numpy as jnp
from jax import lax
from jax.experimental import pallas as pl
from jax.experimental.pallas import tpu as pltpu

N_HEADS = 16
HEAD_DIM = 64
N_KV_HEADS = 4
GROUP = N_HEADS // N_KV_HEADS
CMP_BLOCK = 32
CMP_STRIDE = 16
SEL_BLOCK = 64
TOP_N = 16
WINDOW = 512
ROPE_THETA = 10000.0
PAGE_SIZE = 128
SSM_GROUP = 16
SSM_STATE = 64
SSM_CHUNK = 8
SSM_UNIT = 8
EPS = 1e-6
NEG = -1e30
FORCE = 1e9
MASKED = -3.0e38
Q_W = N_HEADS * HEAD_DIM
KV_W = N_KV_HEADS * HEAD_DIM
LANES = 128
SUBLANES = 8
PAGE_GROUP = 16
S5_ROW_BLOCK = 1024
VMEM_LIMIT = 56 * 1024 * 1024

F32 = jnp.float32
BF16 = jnp.bfloat16


def _params(*sem):
    return pltpu.CompilerParams(dimension_semantics=sem, vmem_limit_bytes=VMEM_LIMIT)


def _full(shape):
    zeros = (0,) * len(shape)
    return pl.BlockSpec(shape, lambda *_: zeros)


def _rms(x, g):
    ms = jnp.mean(x * x, axis=-1, keepdims=True)
    return x * lax.rsqrt(ms + EPS) * g


def _dot(a, b):
    return jnp.dot(a, b, preferred_element_type=F32)


def _dot_f32lhs(w, x):
    hi = x.astype(BF16)
    r1 = x - hi.astype(F32)
    mid = r1.astype(BF16)
    lo = (r1 - mid.astype(F32)).astype(BF16)
    return _dot(w, hi) + _dot(w, mid) + _dot(w, lo)


def _dot_f32rhs(x, w):
    hi = x.astype(BF16)
    r1 = x - hi.astype(F32)
    mid = r1.astype(BF16)
    lo = (r1 - mid.astype(F32)).astype(BF16)
    return _dot(hi, w) + _dot(mid, w) + _dot(lo, w)


def _rope_nat(x, cos, sin):
    half = HEAD_DIM // 2
    lane = lax.broadcasted_iota(jnp.int32, (1, LANES), 1)
    first = (lane % HEAD_DIM) < half
    outs = []
    for c in range(x.shape[1] // LANES):
        xc = x[:, c * LANES:(c + 1) * LANES]
        rot = jnp.where(first, pltpu.roll(xc, LANES - half, 1), pltpu.roll(xc, half, 1))
        outs.append(xc * cos + rot * sin)
    return jnp.concatenate(outs, axis=1)


def _inproj_kernel(x_ref, g_ref, w_ref, wg_ref, cos_ref, sin_ref, *outs, transposed):
    xb = _rms(x_ref[...], g_ref[...]).astype(BF16)
    cos = cos_ref[...]
    sin = sin_ref[...]
    scale = HEAD_DIM ** -0.5
    q = _dot(xb, w_ref[:, 0:Q_W])
    qr = _rope_nat(q, cos, sin)
    kv = _dot(xb, w_ref[:, Q_W:Q_W + 6 * KV_W])
    gates = jax.nn.sigmoid(_dot(xb, wg_ref[...]))
    k_s = _rope_nat(kv[:, 2 * KV_W:3 * KV_W], cos, sin)
    v_s = kv[:, 3 * KV_W:4 * KV_W]
    k_w = _rope_nat(kv[:, 4 * KV_W:5 * KV_W], cos, sin)
    v_w = kv[:, 5 * KV_W:6 * KV_W]
    kvc_ref = outs[3]
    kvc_ref[...] = kv[:, 0:2 * KV_W]
    if transposed:
        qT_ref, qrT_ref, gT_ref, _, kvcT_ref, kvsT_ref, kvwT_ref, ksb_ref, kwb_ref, vsT_ref, vwT_ref = outs
        qT_ref[...] = (q * scale).T.astype(BF16)
        qrT_ref[...] = (qr * scale).T.astype(BF16)
        gT_ref[...] = gates.T
        for h in range(N_KV_HEADS):
            ksb_ref[h] = k_s[:, h * HEAD_DIM:(h + 1) * HEAD_DIM].astype(BF16)
            kwb_ref[h] = k_w[:, h * HEAD_DIM:(h + 1) * HEAD_DIM].astype(BF16)
        v_sT = v_s.T
        v_wT = v_w.T
        vsT_ref[...] = v_sT.astype(BF16)
        vwT_ref[...] = v_wT.astype(BF16)
        kvcT_ref[0] = kv[:, 0:2 * KV_W].T
        kvsT_ref[0, 0:KV_W, :] = k_s.T
        kvsT_ref[0, KV_W:2 * KV_W, :] = v_sT
        kvwT_ref[0, 0:KV_W, :] = k_w.T
        kvwT_ref[0, KV_W:2 * KV_W, :] = v_wT
    else:
        q_ref, qr_ref, gt_ref, _, kvs_ref, kvw_ref = outs
        q_ref[...] = (q * scale).astype(BF16)
        qr_ref[...] = (qr * scale).astype(BF16)
        gt_ref[...] = gates
        kvs_ref[:, 0:KV_W] = k_s
        kvs_ref[:, KV_W:2 * KV_W] = v_s
        kvw_ref[:, 0:KV_W] = k_w
        kvw_ref[:, KV_W:2 * KV_W] = v_w


def _inproj(x, g, w_main, w_gate, cos_t, sin_t, *, tm, pos_blocks, transposed):
    rows, d = x.shape
    n = rows // tm
    row_blk = lambda w: pl.BlockSpec((tm, w), lambda i: (i, 0))
    col_blk = lambda h: pl.BlockSpec((h, tm), lambda i: (0, i))
    tab = pl.BlockSpec((tm, LANES), lambda i: (i % pos_blocks, 0))
    kv_nat = jax.ShapeDtypeStruct((rows, 2 * KV_W), F32)
    if transposed:
        seqs = rows // (pos_blocks * tm)
        kh = pl.BlockSpec((N_KV_HEADS, tm, HEAD_DIM), lambda i: (0, i, 0))
        kvT = pl.BlockSpec((1, 2 * KV_W, tm), lambda i: (i // pos_blocks, 0, i % pos_blocks))
        out_shape = ([jax.ShapeDtypeStruct((Q_W, rows), BF16)] * 2 + [jax.ShapeDtypeStruct((LANES, rows), F32)]
                     + [kv_nat] + [jax.ShapeDtypeStruct((seqs, 2 * KV_W, pos_blocks * tm), F32)] * 3
                     + [jax.ShapeDtypeStruct((N_KV_HEADS, rows, HEAD_DIM), BF16)] * 2
                     + [jax.ShapeDtypeStruct((KV_W, rows), BF16)] * 2)
        out_specs = ([col_blk(Q_W)] * 2 + [col_blk(LANES)] + [row_blk(2 * KV_W)] + [kvT] * 3 + [kh] * 2
                     + [col_blk(KV_W)] * 2)
    else:
        out_shape = ([jax.ShapeDtypeStruct((rows, Q_W), BF16)] * 2 + [jax.ShapeDtypeStruct((rows, LANES), F32)]
                     + [kv_nat] * 3)
        out_specs = [row_blk(Q_W)] * 2 + [row_blk(LANES)] + [row_blk(2 * KV_W)] * 3
    return pl.pallas_call(
        functools.partial(_inproj_kernel, transposed=transposed),
        grid=(n,),
        in_specs=[row_blk(d), _full((1, d)), _full(w_main.shape), _full(w_gate.shape), tab, tab],
        out_specs=out_specs,
        out_shape=out_shape,
        compiler_params=_params("parallel"),
        name="nsa_inproj",
    )(x, g, w_main, w_gate, cos_t, sin_t)


_KV_CHUNKS = 2 * KV_W // LANES


def _compress_half(load_rows, kv, nrows, wbd_ref, pe_ref):
    accs = []
    for r in range(CMP_BLOCK // CMP_STRIDE):
        acc = jnp.zeros((nrows, KV_W), F32)
        for s in range(CMP_STRIDE):
            idx = (kv * 2 + r) * CMP_STRIDE + s
            lhs = (load_rows(s, kv) + pe_ref[idx:idx + 1, :]).astype(BF16)
            acc = acc + _dot(lhs, wbd_ref[idx])
        accs.append(acc)
    return accs


def _compress_prompt_kernel(*refs, nsb):
    x_refs = refs[:_KV_CHUNKS]
    wbd_ref, pe_ref, w2_ref, kc_ref, vcT_ref, sh_ref = refs[_KV_CHUNKS:]

    def load_rows(s, kv):
        per_half = _KV_CHUNKS // 2
        return jnp.concatenate([x_refs[kv * per_half + c][0, pl.ds(s, nsb, stride=CMP_STRIDE), :]
                                for c in range(per_half)], axis=1)

    sh_ref[nsb:nsb + SUBLANES, :] = jnp.zeros((SUBLANES, KV_W), F32)
    for kv in range(2):
        pr0, pr1 = _compress_half(load_rows, kv, nsb, wbd_ref, pe_ref)
        sh_ref[0:nsb, :] = pr1
        h = pr0 + sh_ref[pl.ds(1, nsb), :]
        out = _dot(jax.nn.gelu(h).astype(BF16), w2_ref[kv])
        if kv == 0:
            for hh in range(N_KV_HEADS):
                kc_ref[0, hh] = out[:, hh * HEAD_DIM:(hh + 1) * HEAD_DIM].astype(BF16)
        else:
            vcT_ref[0] = out.T.astype(BF16)


def _compress_prompt(kvc3, wbd, pe_t, w2bd):
    b, t, _ = kvc3.shape
    nsb = t // CMP_STRIDE
    return pl.pallas_call(
        functools.partial(_compress_prompt_kernel, nsb=nsb),
        grid=(b,),
        in_specs=[pl.BlockSpec((1, t, LANES), lambda i, c=c: (i, 0, c)) for c in range(_KV_CHUNKS)]
        + [_full(wbd.shape), _full(pe_t.shape), _full(w2bd.shape)],
        out_specs=[pl.BlockSpec((1, N_KV_HEADS, nsb, HEAD_DIM), lambda i: (i, 0, 0, 0)),
                   pl.BlockSpec((1, KV_W, nsb), lambda i: (i, 0, 0))],
        out_shape=[jax.ShapeDtypeStruct((b, N_KV_HEADS, nsb, HEAD_DIM), BF16),
                   jax.ShapeDtypeStruct((b, KV_W, nsb), BF16)],
        scratch_shapes=[pltpu.VMEM((nsb + SUBLANES, KV_W), F32)],
        compiler_params=_params("parallel"),
        name="nsa_compress_prompt",
    )(*([kvc3] * _KV_CHUNKS), wbd, pe_t, w2bd)


def _topk_mask_T(imp, n_top):
    ns, w = imp.shape
    nblk = ns // SUBLANES
    blocks = [imp[r * SUBLANES:(r + 1) * SUBLANES, :] for r in range(nblk)]
    cnts = [jnp.zeros((SUBLANES, w), F32) for _ in range(nblk)]
    sub = lax.broadcasted_iota(jnp.int32, (SUBLANES, w), 0)
    for sp in range(ns):
        row = blocks[sp // SUBLANES][sp % SUBLANES:sp % SUBLANES + 1, :]
        for r in range(nblk):
            blk = blocks[r]
            if sp < r * SUBLANES:
                beats = jnp.where(row >= blk, 1.0, 0.0)
            elif sp >= (r + 1) * SUBLANES:
                beats = jnp.where(row > blk, 1.0, 0.0)
            else:
                beats = jnp.where(sub > (sp - r * SUBLANES), jnp.where(row >= blk, 1.0, 0.0),
                                  jnp.where(row > blk, 1.0, 0.0))
            cnts[r] = cnts[r] + beats
    return jnp.concatenate([jnp.where(c < n_top, 1.0, 0.0) for c in cnts], axis=0)


def _online_chunks(states, k_cs, vT_cs, qTs, biases):
    scores = [_dot(k_c, qT) for k_c, qT in zip(k_cs, qTs)]
    mids = []
    for (m, l, _), s, bias in zip(states, scores, biases):
        s = s + bias
        m_new = jnp.maximum(m, jnp.max(s, axis=0, keepdims=True))
        alpha = jnp.exp(m - m_new)
        p = jnp.exp(s - m_new)
        mids.append((m_new, alpha * l + jnp.sum(p, axis=0, keepdims=True), alpha, p.astype(BF16)))
    return tuple((m_new, l_new, alpha * acc + _dot(vT_c, p))
                 for (m_new, l_new, alpha, p), (_, _, acc), vT_c in zip(mids, states, vT_cs))


def _softmax_init(w):
    return (jnp.full((1, w), NEG, F32), jnp.zeros((1, w), F32), jnp.zeros((HEAD_DIM, w), F32))


def _softmax_finish(carry):
    _, l, acc = carry
    return acc * (1.0 / jnp.maximum(l, 1e-20))


def _attn_prompt_kernel(qT_ref, qrT_ref, gT_ref, kc_ref, vcT_ref, ks_ref, vsT_ref, kw_ref, vwT_ref, ovT_ref,
                        o_ref, selb_ref, oc_ref, *, tq, nc, n_top):
    ck = tq
    i = pl.program_id(1)
    t0 = i * tq
    qpos = t0 + lax.broadcasted_iota(jnp.int32, (1, tq), 1)
    ncp = kc_ref.shape[2]
    ns = ovT_ref.shape[0]
    w = GROUP * tq
    bpc = ck // SEL_BLOCK
    sel_shift = SEL_BLOCK.bit_length() - 1
    kvhs = range(N_KV_HEADS)
    heads = [[kvh * GROUP + g for g in range(GROUP)] for kvh in kvhs]
    rows = [pl.ds(kvh * HEAD_DIM, HEAD_DIM) for kvh in kvhs]
    qrT = [jnp.concatenate([qrT_ref[h * HEAD_DIM:(h + 1) * HEAD_DIM, :] for h in heads[kvh]], axis=1)
           for kvh in kvhs]
    kl = lax.broadcasted_iota(jnp.int32, (ck, tq), 0)
    ql = lax.broadcasted_iota(jnp.int32, (ck, tq), 1)
    key_le_query = jnp.where(kl <= ql, 0.0, MASKED)
    key_ge_query = jnp.where(kl >= ql, 0.0, MASKED)
    tile4 = lambda b: jnp.concatenate([b] * GROUP, axis=1)

    cidx = lax.broadcasted_iota(jnp.int32, (ncp, 1), 0)
    valid = ((cidx * CMP_STRIDE + (CMP_BLOCK - 1)) <= qpos) & (cidx < nc)
    sidx = lax.broadcasted_iota(jnp.int32, (ns, 1), 0)
    cur = lax.shift_right_logical(qpos, sel_shift)
    forced = (sidx == 0) | (sidx == cur) | (sidx == cur - 1)
    causal = (sidx * SEL_BLOCK) <= qpos
    cmp_scores = [
        _dot(kc_ref[0, kvh], jnp.concatenate([qT_ref[h * HEAD_DIM:(h + 1) * HEAD_DIM, :] for h in heads[kvh]], axis=1))
        for kvh in kvhs]
    for kvh in kvhs:
        s = cmp_scores[kvh]
        probs = []
        for g in range(GROUP):
            sm = jnp.where(valid, s[:, g * tq:(g + 1) * tq], NEG)
            mx = jnp.max(sm, axis=0, keepdims=True)
            e = jnp.where(valid, jnp.exp(sm - mx), 0.0)
            den = jnp.maximum(jnp.sum(e, axis=0, keepdims=True), 1e-20)
            probs.append(e / den)
        oc_ref[kvh] = _dot(vcT_ref[0, rows[kvh], :], jnp.concatenate(probs, axis=1).astype(BF16))
        psum = probs[0]
        for g in range(1, GROUP):
            psum = psum + probs[g]
        imp = _dot_f32lhs(ovT_ref[...], psum)
        imp = jnp.where(forced, FORCE, jnp.where(causal, imp, NEG))
        selb = (_topk_mask_T(imp, n_top) - 1.0) * (-MASKED)
        for c in range(ns // bpc):
            selb_ref[kvh, c, 0:bpc, :] = selb[c * bpc:(c + 1) * bpc, :]

    n_back = WINDOW // ck
    states = tuple(_softmax_init(w) for _ in kvhs)
    for r in range(n_back + 1):
        a = i - n_back + r
        kst = pl.multiple_of(jnp.maximum(a, 0) * ck, ck)
        if r == n_back:
            bias = key_le_query
        else:
            skip = jnp.where(a < 0, MASKED, 0.0)
            bias = (jnp.minimum(key_ge_query, skip) if r == 0 else jnp.broadcast_to(skip, (ck, tq)))
        states = _online_chunks(states, [kw_ref[kvh, pl.ds(kst, ck), :] for kvh in kvhs],
                                [vwT_ref[rows[kvh], pl.ds(kst, ck)] for kvh in kvhs], qrT,
                                [tile4(bias)] * N_KV_HEADS)
    ow = [_softmax_finish(st) for st in states]

    def sel_bias(kvh, c):
        tile = selb_ref[kvh, c]
        return jnp.concatenate([jnp.broadcast_to(tile[b:b + 1, :], (SEL_BLOCK, tq)) for b in range(bpc)], axis=0)

    def chunk_step(c, states, diagonal):
        kst = pl.multiple_of(c * ck, ck)
        biases = [sel_bias(kvh, c) for kvh in kvhs]
        if diagonal:
            biases = [jnp.minimum(b, key_le_query) for b in biases]
        return _online_chunks(states, [ks_ref[kvh, pl.ds(kst, ck), :] for kvh in kvhs],
                              [vsT_ref[rows[kvh], pl.ds(kst, ck)] for kvh in kvhs], qrT,
                              [tile4(b) for b in biases])

    states = lax.fori_loop(0, i, lambda c, st: chunk_step(c, st, False), tuple(_softmax_init(w) for _ in kvhs))
    states = chunk_step(i, states, True)

    for kvh in kvhs:
        def gate_row(j, kvh=kvh):
            return jnp.concatenate([gT_ref[h * 3 + j:h * 3 + j + 1, :] for h in heads[kvh]], axis=1)

        oT = gate_row(0) * oc_ref[kvh] + gate_row(1) * _softmax_finish(states[kvh]) + gate_row(2) * ow[kvh]
        for g, h in enumerate(heads[kvh]):
            o_ref[h * HEAD_DIM:(h + 1) * HEAD_DIM, :] = oT[:, g * tq:(g + 1) * tq].astype(BF16)


def _attn_prompt(qT, qrT, gT, kc, vcT, ksb, vsT, kwb, vwT, ovT, *, batch, seq, tq, nc, n_top):
    nq = seq // tq
    nsb = kc.shape[2]
    ns = ovT.shape[0]
    col = lambda h: pl.BlockSpec((h, tq), lambda b, i: (0, b * nq + i))
    kh = pl.BlockSpec((N_KV_HEADS, seq, HEAD_DIM), lambda b, i: (0, b, 0))
    vt = pl.BlockSpec((KV_W, seq), lambda b, i: (0, b))
    return pl.pallas_call(
        functools.partial(_attn_prompt_kernel, tq=tq, nc=nc, n_top=n_top),
        grid=(batch, nq),
        in_specs=[col(Q_W), col(Q_W), col(LANES),
                  pl.BlockSpec((1, N_KV_HEADS, nsb, HEAD_DIM), lambda b, i: (b, 0, 0, 0)),
                  pl.BlockSpec((1, KV_W, nsb), lambda b, i: (b, 0, 0)),
                  kh, vt, kh, vt, _full(ovT.shape)],
        out_specs=col(Q_W),
        out_shape=jax.ShapeDtypeStruct((Q_W, batch * seq), BF16),
        scratch_shapes=[pltpu.VMEM((N_KV_HEADS, ns * SEL_BLOCK // tq, SUBLANES, tq), F32),
                        pltpu.VMEM((N_KV_HEADS, HEAD_DIM, GROUP * tq), F32)],
        compiler_params=_params("parallel", "arbitrary"),
        name="nsa_attn_prompt",
    )(qT, qrT, gT, kc, vcT, ksb, vsT, kwb, vwT, ovT)


def _oproj_kernel(o_ref, w_ref, x_ref, g1_ref, g2_ref, h_ref, xm_ref, *, transposed):
    if transposed:
        y = lax.dot_general(o_ref[...], w_ref[...], (((0,), (0,)), ((), ())), preferred_element_type=F32)
    else:
        y = _dot(o_ref[...], w_ref[...])
    h = x_ref[...] + _rms(y, g1_ref[...])
    h_ref[...] = h
    xm_ref[...] = _rms(h, g2_ref[...]).astype(BF16)


def _oproj(o, w_o, x, g1, g2, *, tm, transposed):
    rows, d = x.shape
    o_spec = (pl.BlockSpec((Q_W, tm), lambda i: (0, i)) if transposed else pl.BlockSpec((tm, Q_W), lambda i: (i, 0)))
    row = pl.BlockSpec((tm, d), lambda i: (i, 0))
    return pl.pallas_call(
        functools.partial(_oproj_kernel, transposed=transposed),
        grid=(rows // tm,),
        in_specs=[o_spec, _full(w_o.shape), row, _full((1, d)), _full((1, d))],
        out_specs=[row, row],
        out_shape=[jax.ShapeDtypeStruct((rows, d), F32), jax.ShapeDtypeStruct((rows, d), BF16)],
        compiler_params=_params("parallel"),
        name="nsa_oproj",
    )(o, w_o, x, g1, g2)


def _mlp_kernel(xm_ref, h_ref, w1_ref, w2_ref, g3_ref, gn_ref, *rest, ff_chunk, next_norm):
    if next_norm:
        h2_ref, xn_ref, acc_ref = rest
    else:
        h2_ref, acc_ref = rest
    xm = xm_ref[...]
    for c in range(w1_ref.shape[1] // ff_chunk):
        cols = slice(c * ff_chunk, (c + 1) * ff_chunk)
        hm = jnp.maximum(_dot(xm, w1_ref[:, cols]), 0.0)
        part = _dot((hm * hm).astype(BF16), w2_ref[cols, :])
        if c == 0:
            acc_ref[...] = part
        else:
            acc_ref[...] += part
    h2 = h_ref[...] + _rms(acc_ref[...], g3_ref[...])
    h2_ref[...] = h2
    if next_norm:
        xn_ref[...] = _rms(h2, gn_ref[...])


def _mlp(xm, h, w1, w2, g3, gn, *, tm, ff_chunk, next_norm):
    rows, d = h.shape
    row = pl.BlockSpec((tm, d), lambda i: (i, 0))
    out_shape = [jax.ShapeDtypeStruct((rows, d), F32)] * (2 if next_norm else 1)
    return pl.pallas_call(
        functools.partial(_mlp_kernel, ff_chunk=ff_chunk, next_norm=next_norm),
        grid=(rows // tm,),
        in_specs=[row, row, _full(w1.shape), _full(w2.shape), _full((1, d)), _full((1, d))],
        out_specs=[row] * len(out_shape),
        out_shape=out_shape,
        scratch_shapes=[pltpu.VMEM((tm, d), F32)],
        compiler_params=_params("parallel"),
        name="sq_relu_mlp",
    )(xm, h, w1, w2, g3, gn)


def _s5_chunk_kernel(u_ref, brow_ref, pre_ref, pim_ref, qre_ref, qim_ref, are_ref, aim_ref, y_ref, hfin_ref,
                     sre, sim, hre, him, cre, cim, *, nb):
    steps_per_iter = SUBLANES // nb
    rows = sre.shape[0]
    uw = SSM_UNIT * SSM_GROUP
    nt = u_ref.shape[2] // uw

    @pl.when(pl.program_id(1) == 0)
    def _():
        cre[...] = jnp.zeros(cre.shape, F32)
        cim[...] = jnp.zeros(cim.shape, F32)

    u = u_ref[0]
    sre[...] = _dot(u, pre_ref[0])
    sim[...] = _dot(u, pim_ref[0])
    ar = are_ref[0]
    ai = aim_ref[0]

    def body(it, carry):
        hr, hi = carry
        r0 = pl.multiple_of(it * SUBLANES, SUBLANES)
        sr8 = sre[pl.ds(r0, SUBLANES), :]
        si8 = sim[pl.ds(r0, SUBLANES), :]
        prev_r, prev_i = [], []
        for j in range(steps_per_iter):
            prev_r.append(hr)
            prev_i.append(hi)
            sr = sr8[j * nb:(j + 1) * nb, :]
            si = si8[j * nb:(j + 1) * nb, :]
            hr, hi = ar * hr - ai * hi + sr, ar * hi + ai * hr + si
        hre[pl.ds(r0, SUBLANES), :] = jnp.concatenate(prev_r, axis=0)
        him[pl.ds(r0, SUBLANES), :] = jnp.concatenate(prev_i, axis=0)
        return hr, hi

    hr, hi = lax.fori_loop(0, rows // SUBLANES, body, (cre[...], cim[...]))
    cre[...] = hr
    cim[...] = hi
    hfin_ref[0, 0] = hr
    hfin_ref[0, 1] = hi

    hb_re = hre[...].astype(BF16)
    hb_im = him[...].astype(BF16)
    for t2 in range(0, nt, 2):
        cols = slice(t2 * uw, (t2 + 2) * uw)
        acc = _dot(hb_re, qre_ref[0, :, cols]) + _dot(hb_im, qim_ref[0, :, cols])
        for t1 in range(t2 + 2):
            lag0 = t2 - t1 + 1
            acc = acc + _dot(u[:, t1 * uw:(t1 + 1) * uw], brow_ref[0, :, lag0 * uw:(lag0 + 2) * uw])
        y_ref[0, :, cols] = acc


def _s5_chunk(u2, ops, *, nb):
    nunit, nch, width = u2.shape
    sw = ops["p_re"].shape[2]
    rows = _tile(nch, S5_ROW_BLOCK)
    per_unit = lambda a: pl.BlockSpec((1,) + a.shape[1:], lambda i, r: (i,) + (0,) * (a.ndim - 1))
    row_blk = pl.BlockSpec((1, rows, width), lambda i, r: (i, r, 0))
    args = (u2, ops["brow"], ops["p_re"], ops["p_im"], ops["q_re"], ops["q_im"], ops["a_chunk_re"], ops["a_chunk_im"])
    return pl.pallas_call(
        functools.partial(_s5_chunk_kernel, nb=nb),
        grid=(nunit, nch // rows),
        in_specs=[row_blk] + [per_unit(a) for a in args[1:]],
        out_specs=[row_blk, pl.BlockSpec((1, 2, nb, sw), lambda i, r: (i, 0, 0, 0))],
        out_shape=[jax.ShapeDtypeStruct((nunit, nch, width), F32), jax.ShapeDtypeStruct((nunit, 2, nb, sw), F32)],
        scratch_shapes=[pltpu.VMEM((rows, sw), F32)] * 4 + [pltpu.VMEM((nb, sw), F32)] * 2,
        compiler_params=_params("parallel", "arbitrary"),
        name="s5_chunk_scan",
    )(*args)


def _s5_step_kernel(u_ref, h0r_ref, h0i_ref, bre_ref, bim_ref, cre_ref, cim_ref, are_ref, aim_ref,
                    y_ref, hr_ref, hi_ref, *, npair):
    for p in range(npair):
        u = u_ref[p]
        ar = are_ref[p]
        ai = aim_ref[p]
        h0r = h0r_ref[p]
        h0i = h0i_ref[p]
        hr = ar * h0r - ai * h0i + _dot(u, bre_ref[p])
        hi = ar * h0i + ai * h0r + _dot(u, bim_ref[p])
        hr_ref[p] = hr
        hi_ref[p] = hi
        y_ref[p] = _dot(hr.astype(BF16), cre_ref[p]) + _dot(hi.astype(BF16), cim_ref[p])


def _s5_step(u2, h0r, h0i, ops):
    npair, rows, width = u2.shape
    args = (u2, h0r, h0i, ops["b1_re"], ops["b1_im"], ops["c1_re"], ops["c1_im"], ops["a1_re"], ops["a1_im"])
    return pl.pallas_call(
        functools.partial(_s5_step_kernel, npair=npair),
        grid=(1,),
        in_specs=[_full(a.shape) for a in args],
        out_specs=[_full((npair, rows, width)), _full(h0r.shape), _full(h0r.shape)],
        out_shape=[jax.ShapeDtypeStruct((npair, rows, width), F32), jax.ShapeDtypeStruct(h0r.shape, F32),
                   jax.ShapeDtypeStruct(h0r.shape, F32)],
        compiler_params=_params("arbitrary"),
        name="s5_single_step",
    )(*args)


def _s5_out_kernel(yc_ref, u_ref, d_ref, wg_ref, bg_ref, h_ref, g1_ref, g2_ref, h3_ref, xm_ref):
    y = jax.nn.gelu(yc_ref[...] + d_ref[...] * u_ref[...])
    out = y * jax.nn.sigmoid(_dot(y.astype(BF16), wg_ref[...]) + bg_ref[...])
    h3 = h_ref[...] + _rms(out, g1_ref[...])
    h3_ref[...] = h3
    xm_ref[...] = _rms(h3, g2_ref[...]).astype(BF16)


def _s5_out(yc, u, d_skip, w_glu, b_glu, h, g1, g2, *, tm):
    rows, d = h.shape
    row = pl.BlockSpec((tm, d), lambda i: (i, 0))
    vec = _full((1, d))
    return pl.pallas_call(
        _s5_out_kernel,
        grid=(rows // tm,),
        in_specs=[row, row, vec, _full(w_glu.shape), vec, row, vec, vec],
        out_specs=[row, row],
        out_shape=[jax.ShapeDtypeStruct((rows, d), F32), jax.ShapeDtypeStruct((rows, d), BF16)],
        compiler_params=_params("parallel"),
        name="s5_glu_out",
    )(yc, u, d_skip, w_glu, b_glu, h, g1, g2)


def _s5_operators(a_re, a_im, log_dt, b_re, b_im, c_re, c_im):
    hp = lax.Precision.HIGHEST
    g, n = a_re.shape
    gu = SSM_UNIT
    nunit = g // gu
    L = SSM_CHUNK
    uw = gu * SSM_GROUP
    sw = gu * n
    a = lax.complex(a_re.astype(F32), a_im.astype(F32))
    dt = jnp.exp(log_dt.astype(F32))[:, None]
    a_bar = jnp.exp(a * dt)
    b_bar = ((a_bar - 1.0) / a)[:, :, None] * lax.complex(b_re.astype(F32), b_im.astype(F32))
    c = lax.complex(c_re.astype(F32), c_im.astype(F32))
    pows = [jnp.ones_like(a_bar)]
    for _ in range(L):
        pows.append(pows[-1] * a_bar)
    a_pow = jnp.stack(pows).reshape(L + 1, nunit, sw)
    apr = jnp.real(a_pow)
    api = jnp.imag(a_pow)
    eye = jnp.eye(gu, dtype=F32)

    def bd_in(x):
        return jnp.einsum("pgnd,gh->pgdhn", x.reshape(nunit, gu, n, SSM_GROUP), eye).reshape(nunit, uw, sw)

    def bd_out(x):
        return jnp.einsum("pgcn,gh->pgnhc", x.reshape(nunit, gu, SSM_GROUP, n), eye).reshape(nunit, sw, uw)

    p0r, p0i = bd_in(jnp.real(b_bar)), bd_in(jnp.imag(b_bar))
    q0r, q0i = bd_out(jnp.real(c)), bd_out(jnp.imag(c))
    sr = apr[::-1][1:][:, :, None, :]
    si = api[::-1][1:][:, :, None, :]
    stack_rows = lambda x: x.transpose(1, 0, 2, 3).reshape(nunit, L * uw, sw)
    p_re = stack_rows(p0r[None] * sr - p0i[None] * si)
    p_im = stack_rows(p0r[None] * si + p0i[None] * sr)
    tr = apr[1:][:, :, :, None]
    ti = api[1:][:, :, :, None]
    stack_cols = lambda x: x.transpose(1, 2, 0, 3).reshape(nunit, sw, L * uw)
    q_re = stack_cols(q0r[None] * tr - q0i[None] * ti)
    q_im = stack_cols(-(q0r[None] * ti + q0i[None] * tr))
    xr = p0r[None] * apr[:L][:, :, None, :] - p0i[None] * api[:L][:, :, None, :]
    xi = p0r[None] * api[:L][:, :, None, :] + p0i[None] * apr[:L][:, :, None, :]
    b_lag = (jnp.einsum("tpus,psv->ptuv", xr, q0r, precision=hp)
             - jnp.einsum("tpus,psv->ptuv", xi, q0i, precision=hp))
    brow = jnp.concatenate([jnp.zeros((nunit, 1, uw, uw), F32), b_lag], axis=1)
    brow = brow.transpose(0, 2, 1, 3).reshape(nunit, uw, (L + 1) * uw)
    lane_row = lambda x: x.reshape(nunit, 1, sw)
    return {
        "brow": brow.astype(BF16),
        "p_re": p_re.astype(BF16), "p_im": p_im.astype(BF16),
        "q_re": q_re.astype(BF16), "q_im": q_im.astype(BF16),
        "a_chunk_re": lane_row(apr[L]), "a_chunk_im": lane_row(api[L]),
        "a1_re": lane_row(apr[1]), "a1_im": lane_row(api[1]),
        "b1_re": p0r.astype(BF16), "b1_im": p0i.astype(BF16),
        "c1_re": q0r.astype(BF16), "c1_im": (-q0i).astype(BF16),
    }


def _compress_paged_kernel(pt_ref, *refs, pg, nsb):
    del pt_ref
    pages = refs[:pg]
    wbd_ref, pe_ref, w2_ref, out_ref, h0_ref, h1_ref, xs_ref = refs[pg:]
    j = pl.program_id(1)
    sbp = PAGE_SIZE // CMP_STRIDE
    nrows = pg * sbp
    per_half = _KV_CHUNKS // 2

    for i, r in enumerate(pages):
        for kv in range(2):
            x = r[0, kv].T
            for c in range(per_half):
                xs_ref[kv * per_half + c, i * PAGE_SIZE:(i + 1) * PAGE_SIZE, :] = x[:, c * LANES:(c + 1) * LANES]

    def load_rows(s, kv):
        return jnp.concatenate([xs_ref[kv * per_half + c, pl.ds(s, nrows, stride=CMP_STRIDE), :]
                                for c in range(per_half)], axis=1)

    @pl.when(j == 0)
    def _():
        h1_ref[:, nsb:nsb + SUBLANES, :] = jnp.zeros((2, SUBLANES, KV_W), F32)

    r0 = pl.multiple_of(j * nrows, nrows)
    for kv in range(2):
        pr0, pr1 = _compress_half(load_rows, kv, nrows, wbd_ref, pe_ref)
        h0_ref[kv, pl.ds(r0, nrows), :] = pr0
        h1_ref[kv, pl.ds(r0, nrows), :] = pr1

    @pl.when(j == pl.num_programs(1) - 1)
    def _():
        for kv in range(2):
            h = h0_ref[kv] + h1_ref[kv, pl.ds(1, nsb), :]
            out = _dot(jax.nn.gelu(h).astype(BF16), w2_ref[kv])
            out_ref[0, :, kv * KV_W:(kv + 1) * KV_W] = out.astype(BF16)


def _page_specs(pg, pages_per_sample):
    def spec(i):
        return pl.BlockSpec((1, 2, KV_W, PAGE_SIZE),
                            lambda b, j, pt: (pt[b * pages_per_sample + j * pg + i], 0, 0, 0))
    return [spec(i) for i in range(pg)]


def _compress_paged(pages, pt_flat, wbd, pe_t, w2bd, *, nb, pages_per_sample, pg):
    nsb = pages_per_sample * PAGE_SIZE // CMP_STRIDE
    c3 = lambda shape: pl.BlockSpec(shape, lambda b, j, pt: (0,) * len(shape))
    grid_spec = pltpu.PrefetchScalarGridSpec(
        num_scalar_prefetch=1,
        grid=(nb, pages_per_sample // pg),
        in_specs=_page_specs(pg, pages_per_sample) + [c3(wbd.shape), c3(pe_t.shape), c3(w2bd.shape)],
        out_specs=pl.BlockSpec((1, nsb, 2 * KV_W), lambda b, j, pt: (b, 0, 0)),
        scratch_shapes=[pltpu.VMEM((2, nsb, KV_W), F32), pltpu.VMEM((2, nsb + SUBLANES, KV_W), F32),
                        pltpu.VMEM((_KV_CHUNKS, pg * PAGE_SIZE, LANES), F32)],
    )
    return pl.pallas_call(
        functools.partial(_compress_paged_kernel, pg=pg, nsb=nsb),
        grid_spec=grid_spec,
        out_shape=jax.ShapeDtypeStruct((nb, nsb, 2 * KV_W), BF16),
        compiler_params=_params("parallel", "arbitrary"),
        name="nsa_compress_paged",
    )(pt_flat, *([pages] * pg), wbd, pe_t, w2bd)


def _topk_mask_lanes(imp, n_top, ns_valid):
    lane = lax.broadcasted_iota(jnp.int32, imp.shape, 1)
    cnt = jnp.zeros(imp.shape, F32)
    for sp in range(ns_valid):
        col = imp[:, sp:sp + 1]
        cnt = cnt + jnp.where(lane > sp, jnp.where(col >= imp, 1.0, 0.0), jnp.where(col > imp, 1.0, 0.0))
    return jnp.where((cnt < n_top) & (lane < ns_valid), 1.0, 0.0)


def _dot_nt(a, b):
    return lax.dot_general(a, b, (((1,), (1,)), ((), ())), preferred_element_type=F32)


def _attn_sample_kernel(pt_ref, *refs, pg, past, nc, ns_valid, n_top):
    del pt_ref
    pages = refs[:pg]
    (q_ref, qr_ref, g_ref, kcvc_ref, ksn_ref, kwn_ref, win_ref, ov_ref, e_ref, gs_ref, o_ref,
     m_ref, l_ref, acc_ref, sel_ref, oc_ref, ow_ref) = refs[pg:]
    j = pl.program_id(1)
    ncp = kcvc_ref.shape[1]
    nsp = ov_ref.shape[1]
    wlen = win_ref.shape[3]
    row = lax.broadcasted_iota(jnp.int32, (N_HEADS, KV_W), 0)
    lane = lax.broadcasted_iota(jnp.int32, (N_HEADS, KV_W), 1)
    own = (lane // HEAD_DIM) == (row // GROUP)

    def spread(ref):
        q = ref[0]
        return jnp.where(own, jnp.concatenate([q] * N_KV_HEADS, axis=1), jnp.zeros((N_HEADS, KV_W), BF16))

    def update(state, s, vs):
        m, l, acc = state
        m_new = jnp.maximum(m, jnp.max(s, axis=1, keepdims=True))
        alpha = jnp.exp(m - m_new)
        p = jnp.exp(s - m_new)
        l = alpha * l + jnp.sum(p, axis=1, keepdims=True)
        pv = None
        for st, sz, v, feature_major in vs:
            pb = p[:, st:st + sz].astype(BF16)
            t = _dot_nt(pb, v) if feature_major else _dot(pb, v)
            pv = t if pv is None else pv + t
        return m_new, l, alpha * acc + pv

    def init():
        return (jnp.full((N_HEADS, 1), NEG, F32), jnp.zeros((N_HEADS, 1), F32), jnp.zeros((N_HEADS, KV_W), F32))

    def new_row_update(state, qbd, new_ref):
        r8 = lax.broadcasted_iota(jnp.int32, (SUBLANES, 2 * KV_W), 0)
        tile = jnp.where(r8 == 0, jnp.broadcast_to(new_ref[0], (SUBLANES, 2 * KV_W)), 0.0).astype(BF16)
        s = _dot_nt(qbd, tile[:, 0:KV_W])
        l8 = lax.broadcasted_iota(jnp.int32, (N_HEADS, SUBLANES), 1)
        s = jnp.where(l8 == 0, s, MASKED)
        return update(state, s, [(0, SUBLANES, tile[:, KV_W:2 * KV_W], False)])

    def put(state):
        m, l, acc = state
        m_ref[...] = jnp.broadcast_to(m, m_ref.shape)
        l_ref[...] = jnp.broadcast_to(l, l_ref.shape)
        acc_ref[...] = acc

    qrbd = spread(qr_ref)

    @pl.when(j == 0)
    def _():
        s = _dot_nt(spread(q_ref), kcvc_ref[0, :, 0:KV_W])
        cidx = lax.broadcasted_iota(jnp.int32, (1, ncp), 1)
        valid = ((cidx * CMP_STRIDE + (CMP_BLOCK - 1)) <= past) & (cidx < nc)
        sm = jnp.where(valid, s, NEG)
        mx = jnp.max(sm, axis=1, keepdims=True)
        e = jnp.where(valid, jnp.exp(sm - mx), 0.0)
        p = e / jnp.maximum(jnp.sum(e, axis=1, keepdims=True), 1e-20)
        oc_ref[...] = _dot(p.astype(BF16), kcvc_ref[0, :, KV_W:2 * KV_W])
        imp = _dot_f32rhs(_dot_f32lhs(gs_ref[...], p), ov_ref[...])
        sidx = lax.broadcasted_iota(jnp.int32, (1, nsp), 1)
        cur = past // SEL_BLOCK
        forced = (sidx == 0) | (sidx == cur) | (sidx == cur - 1)
        causal = (sidx * SEL_BLOCK) <= past
        imp = jnp.where(forced, FORCE, jnp.where(causal, imp, NEG))
        imp = jnp.where(sidx < ns_valid, imp, MASKED)
        sel_ref[...] = _topk_mask_lanes(imp, n_top, ns_valid)
        sw = _dot(qrbd, win_ref[0, 0].astype(BF16))
        wpos = past - wlen + lax.broadcasted_iota(jnp.int32, (1, wlen), 1)
        sw = jnp.where((wpos >= 0) & (past - wpos <= WINDOW), sw, MASKED)
        st = update(init(), sw, [(0, wlen, win_ref[0, 1].astype(BF16), True)])
        st = new_row_update(st, qrbd, kwn_ref)
        ow_ref[...] = st[2] * (1.0 / jnp.maximum(st[1], 1e-20))
        put(new_row_update(init(), qrbd, ksn_ref))

    s = jnp.concatenate([_dot(qrbd, r[0, 0].astype(BF16)) for r in pages], axis=1)
    mask = _dot(sel_ref[...].astype(BF16), e_ref[0]) > 0.5
    s = jnp.where(mask, s, MASKED)
    vs = [(i * PAGE_SIZE, PAGE_SIZE, r[0, 1].astype(BF16), True) for i, r in enumerate(pages)]
    put(update((m_ref[:, 0:1], l_ref[:, 0:1], acc_ref[...]), s, vs))

    @pl.when(j == pl.num_programs(1) - 1)
    def _():
        g = g_ref[0]
        os_ = acc_ref[...] * (1.0 / jnp.maximum(l_ref[:, 0:1], 1e-20))
        o = g[:, 0:1] * oc_ref[...] + g[:, 1:2] * os_ + g[:, 2:3] * ow_ref[...]
        o = jnp.where(own, o, 0.0)
        out = o[:, 0:HEAD_DIM]
        for h in range(1, N_KV_HEADS):
            out = out + o[:, h * HEAD_DIM:(h + 1) * HEAD_DIM]
        o_ref[0] = out.astype(BF16)


def _attn_sample(pages, pt_flat, q3, qr3, g3, kcvc, ks_new, kw_new, win, ov, e_mat, gsum, *, nb, pages_per_sample,
                 pg, past, nc, ns_valid, n_top):
    nsp = ov.shape[1]
    per_b = lambda shape: pl.BlockSpec((1,) + shape, lambda b, j, pt: (b,) + (0,) * len(shape))
    const = lambda shape: pl.BlockSpec(shape, lambda b, j, pt: (0,) * len(shape))
    grid_spec = pltpu.PrefetchScalarGridSpec(
        num_scalar_prefetch=1,
        grid=(nb, pages_per_sample // pg),
        in_specs=_page_specs(pg, pages_per_sample) + [
            per_b((N_HEADS, HEAD_DIM)), per_b((N_HEADS, HEAD_DIM)), per_b((N_HEADS, 3)),
            per_b(kcvc.shape[1:]), per_b((1, 2 * KV_W)), per_b((1, 2 * KV_W)), per_b(win.shape[1:]),
            const(ov.shape), pl.BlockSpec((1,) + e_mat.shape[1:], lambda b, j, pt: (j, 0, 0)), const(gsum.shape)],
        out_specs=per_b((N_HEADS, HEAD_DIM)),
        scratch_shapes=[pltpu.VMEM((N_HEADS, LANES), F32), pltpu.VMEM((N_HEADS, LANES), F32),
                        pltpu.VMEM((N_HEADS, KV_W), F32), pltpu.VMEM((N_HEADS, nsp), F32),
                        pltpu.VMEM((N_HEADS, KV_W), F32), pltpu.VMEM((N_HEADS, KV_W), F32)],
    )
    return pl.pallas_call(
        functools.partial(_attn_sample_kernel, pg=pg, past=past, nc=nc, ns_valid=ns_valid, n_top=n_top),
        grid_spec=grid_spec,
        out_shape=jax.ShapeDtypeStruct((nb, N_HEADS, HEAD_DIM), BF16),
        compiler_params=_params("parallel", "arbitrary"),
        name="nsa_attn_sample",
    )(pt_flat, *([pages] * pg), q3, qr3, g3, kcvc, ks_new, kw_new, win, ov, e_mat, gsum)


def _rope_tables(pos):
    half = HEAD_DIM // 2
    inv = ROPE_THETA ** (-jnp.arange(half, dtype=F32) / half)
    ang = pos.astype(F32)[:, None] * inv[None, :]
    cos = jnp.cos(ang)
    sin = jnp.sin(ang)
    reps = LANES // HEAD_DIM
    return (jnp.tile(jnp.concatenate([cos, cos], axis=1), (1, reps)),
            jnp.tile(jnp.concatenate([-sin, sin], axis=1), (1, reps)))


def _compress_weights(cmp_w1, cmp_w2, cmp_pe):
    ratio = CMP_BLOCK // CMP_STRIDE
    eye = jnp.eye(N_KV_HEADS, dtype=F32)
    w1r = cmp_w1.reshape(2, ratio, CMP_STRIDE, HEAD_DIM, HEAD_DIM)
    wbd = jnp.einsum("krsde,hg->krshdge", w1r, eye).reshape(2 * ratio * CMP_STRIDE, KV_W, KV_W).astype(BF16)
    w2bd = jnp.einsum("kef,hg->khegf", cmp_w2, eye).reshape(2, KV_W, KV_W).astype(BF16)
    pe_t = jnp.tile(cmp_pe.reshape(2 * ratio * CMP_STRIDE, HEAD_DIM), (1, N_KV_HEADS)).astype(F32)
    return wbd, pe_t, w2bd


def _overlap(nc, ncp, nsel, nsp):
    c_start = jnp.arange(ncp, dtype=jnp.int32)[:, None] * CMP_STRIDE
    s_start = jnp.arange(nsp, dtype=jnp.int32)[None, :] * SEL_BLOCK
    ov = (c_start < s_start + SEL_BLOCK) & (c_start + CMP_BLOCK > s_start)
    ov = ov & (jnp.arange(ncp)[:, None] < nc) & (jnp.arange(nsp)[None, :] < nsel)
    return ov.astype(BF16)


def _pick(*cands):
    return next(c for c in cands if c)


def _tile(n, pref):
    t = min(n, pref)
    while n % t:
        t //= 2
    return t


def kernel(x_prompt, x_sample, cache_kv_cmp, cache_kv_sel, cache_kv_win, state_ssm, page_table, norm_g, mlp_w1,
           mlp_w2, nsa_w_in, nsa_w_o, nsa_cmp_w1, nsa_cmp_w2, nsa_cmp_pe, s5_a_re, s5_a_im, s5_log_dt, s5_b_re,
           s5_b_im, s5_c_re, s5_c_im, s5_d, s5_w_glu, s5_b_glu):
    b, t, d = x_prompt.shape
    nb = x_sample.shape[0]
    pages_per_sample = page_table.shape[1]
    past = pages_per_sample * PAGE_SIZE
    rows_p = b * t
    g = norm_g.reshape(norm_g.shape[0], 4, 1, d)

    w_in = nsa_w_in[0]
    w_main = w_in[:, :Q_W + 6 * KV_W].astype(BF16)
    w_gate = jnp.pad(w_in[:, Q_W + 6 * KV_W:], ((0, 0), (0, LANES - 3 * N_HEADS))).astype(BF16)
    w_o = nsa_w_o[0].astype(BF16)
    wbd, pe_t, w2bd = _compress_weights(nsa_cmp_w1[0], nsa_cmp_w2[0], nsa_cmp_pe[0])
    w1 = mlp_w1.astype(BF16)
    w2 = mlp_w2.astype(BF16)
    w_glu = s5_w_glu[0].astype(BF16)
    ops = _s5_operators(s5_a_re[0], s5_a_im[0], s5_log_dt[0], s5_b_re[0], s5_b_im[0], s5_c_re[0], s5_c_im[0])
    d_skip = s5_d[0].reshape(1, d)
    b_glu = s5_b_glu[0].reshape(1, d)

    tm = _tile(rows_p, 512)
    ff_chunk = _tile(mlp_w1.shape[2], 1024)

    xp = x_prompt.reshape(rows_p, d)
    cos_p, sin_p = _rope_tables(jnp.arange(t, dtype=jnp.int32))
    (qT, qrT, gT, kvc, kvcT, kvsT, kvwT, ksb, kwb, vsT, vwT) = _inproj(
        xp, g[0, 0], w_main, w_gate, cos_p, sin_p, tm=_tile(t, 512), pos_blocks=t // _tile(t, 512), transposed=True)
    nsb_p = t // CMP_STRIDE
    nc_p = nsb_p - CMP_BLOCK // CMP_STRIDE + 1
    nsel_p = t // SEL_BLOCK
    kc, vcT = _compress_prompt(kvc.reshape(b, t, 2 * KV_W), wbd, pe_t, w2bd)
    ovT = _overlap(nc_p, nsb_p, nsel_p, nsel_p).T
    tq = next(c for c in (2 * LANES, LANES) if t % c == 0 and WINDOW % c == 0)
    oT = _attn_prompt(qT, qrT, gT, kc, vcT, ksb, vsT, kwb, vwT, ovT, batch=b, seq=t, tq=tq, nc=nc_p,
                      n_top=min(TOP_N, nsel_p))
    hp, xm = _oproj(oT, w_o, xp, g[0, 1], g[0, 2], tm=tm, transposed=True)
    hp, xn1 = _mlp(xm, hp, w1[0], w2[0], g[0, 3], g[1, 0], tm=tm, ff_chunk=ff_chunk, next_norm=True)

    pw = SSM_UNIT * SSM_GROUP
    npair = d // pw
    nk = t // SSM_CHUNK
    u2 = xn1.astype(BF16).reshape(b, nk, SSM_CHUNK, npair, pw).transpose(3, 1, 0, 2, 4)
    u2 = u2.reshape(npair, nk * b, SSM_CHUNK * pw)
    y2, hfin = _s5_chunk(u2, ops, nb=b)
    yc = y2.reshape(npair, nk, b, SSM_CHUNK, pw).transpose(2, 1, 3, 0, 4).reshape(rows_p, d)
    hp, xm = _s5_out(yc, xn1, d_skip, w_glu, b_glu, hp, g[1, 1], g[1, 2], tm=tm)
    (hp,) = _mlp(xm, hp, w1[1], w2[1], g[1, 3], g[1, 3], tm=tm, ff_chunk=ff_chunk, next_norm=False)
    ssm_p = hfin.reshape(npair, 2, b, SSM_UNIT, SSM_STATE).transpose(2, 1, 0, 3, 4)
    ssm_p = ssm_p.reshape(b, 2, d // SSM_GROUP, SSM_STATE)

    xs = x_sample.reshape(nb, d)
    cos_s, sin_s = _rope_tables(jnp.full((nb,), past, dtype=jnp.int32))
    q_s, qr_s, gates_s, kvc_s, kvs_s, kvw_s = _inproj(
        xs, g[0, 0], w_main, w_gate, cos_s, sin_s, tm=nb, pos_blocks=1, transposed=False)
    pt_flat = page_table.reshape(-1).astype(jnp.int32)
    pg = _tile(pages_per_sample, PAGE_GROUP)
    n_pool = cache_kv_cmp.shape[1]
    feature_major = lambda c, n, s: c.transpose(0, 2, 3, 4, 1).reshape(n, 2, KV_W, s)
    cmp_pages = feature_major(cache_kv_cmp[0], n_pool, PAGE_SIZE)
    sel_pages = feature_major(cache_kv_sel[0], n_pool, PAGE_SIZE)
    kcvc = _compress_paged(cmp_pages, pt_flat, wbd, pe_t, w2bd, nb=nb, pages_per_sample=pages_per_sample, pg=pg)
    l_all = past + 1
    nsb_s = l_all // CMP_STRIDE
    nc_s = nsb_s - CMP_BLOCK // CMP_STRIDE + 1
    nsel_s = -(-l_all // SEL_BLOCK)
    nsp = -(-nsel_s // LANES) * LANES
    ov_s = _overlap(nc_s, past // CMP_STRIDE, nsel_s, nsp)
    keys_per_step = pg * PAGE_SIZE
    key_blk = (jnp.arange(past, dtype=jnp.int32) // SEL_BLOCK).reshape(past // keys_per_step, 1, keys_per_step)
    e_mat = (jnp.arange(nsp, dtype=jnp.int32)[None, :, None] == key_blk).astype(BF16)
    hh = jnp.arange(N_HEADS)
    gsum = ((hh[:, None] // GROUP) == (hh[None, :] // GROUP)).astype(BF16)
    win = feature_major(cache_kv_win[0], nb, WINDOW)
    o_s = _attn_sample(sel_pages, pt_flat, q_s.reshape(nb, N_HEADS, HEAD_DIM), qr_s.reshape(nb, N_HEADS, HEAD_DIM),
                       gates_s[:, :3 * N_HEADS].reshape(nb, N_HEADS, 3), kcvc, kvs_s.reshape(nb, 1, 2 * KV_W),
                       kvw_s.reshape(nb, 1, 2 * KV_W), win, ov_s, e_mat, gsum, nb=nb,
                       pages_per_sample=pages_per_sample, pg=pg, past=past, nc=nc_s, ns_valid=nsel_s,
                       n_top=min(TOP_N, nsel_s))
    hs, xm_s = _oproj(o_s.reshape(nb, Q_W), w_o, xs, g[0, 1], g[0, 2], tm=nb, transposed=False)
    hs, xn1_s = _mlp(xm_s, hs, w1[0], w2[0], g[0, 3], g[1, 0], tm=nb, ff_chunk=ff_chunk, next_norm=True)

    u2_s = xn1_s.astype(BF16).reshape(nb, npair, pw).transpose(1, 0, 2)
    st = state_ssm[0].reshape(nb, 2, npair, SSM_UNIT * SSM_STATE).transpose(1, 2, 0, 3)
    y2_s, hr_s, hi_s = _s5_step(u2_s, st[0], st[1], ops)
    yc_s = y2_s.transpose(1, 0, 2).reshape(nb, d)
    hs, xm_s = _s5_out(yc_s, xn1_s, d_skip, w_glu, b_glu, hs, g[1, 1], g[1, 2], tm=nb)
    (hs,) = _mlp(xm_s, hs, w1[1], w2[1], g[1, 3], g[1, 3], tm=nb, ff_chunk=ff_chunk, next_norm=False)
    ssm_s = jnp.stack([hr_s, hi_s], axis=0).transpose(2, 0, 1, 3).reshape(nb, 2, d // SSM_GROUP, SSM_STATE)

    kv5 = lambda a, n, s: a.reshape(1, n, s, 2, N_KV_HEADS, HEAD_DIM)
    from_fm = lambda a, n, s: a.reshape(n, 2, N_KV_HEADS, HEAD_DIM, s).transpose(0, 4, 1, 2, 3)[None]
    win_s = jnp.concatenate([win[..., 1:], kvw_s.reshape(nb, 2, KV_W, 1)], axis=-1)
    return (hp.reshape(b, t, d), hs.reshape(nb, 1, d),
            from_fm(kvcT, b, t), kv5(kvc_s, nb, 1), from_fm(kvsT, b, t), kv5(kvs_s, nb, 1),
            from_fm(kvwT[:, :, t - WINDOW:], b, WINDOW), from_fm(win_s, nb, WINDOW), ssm_p[None], ssm_s[None])
```

```python
import functools

import jax
import jax.numpy as jnp
from jax import lax
from jax.experimental import pallas as pl
from jax.experimental.pallas import tpu as pltpu

N_HEADS = 16
HEAD_DIM = 64
N_KV_HEADS = 4
GROUP = N_HEADS // N_KV_HEADS
CMP_BLOCK = 32
CMP_STRIDE = 16
SEL_BLOCK = 64
TOP_N = 16
WINDOW = 512
ROPE_THETA = 10000.0
PAGE_SIZE = 128
SSM_GROUP = 16
SSM_STATE = 64
SSM_CHUNK = 8
SSM_UNIT = 8
EPS = 1e-6
NEG = -1e30
FORCE = 1e9
MASKED = -1.5e38
LOG2E = 1.4426950408889634
V_ROWS = HEAD_DIM + 16
Q_W = N_HEADS * HEAD_DIM
KV_W = N_KV_HEADS * HEAD_DIM
LANES = 128
SUBLANES = 8
PAGE_GROUP = 16
S5_ROW_BLOCK = 1024
VMEM_LIMIT = 56 * 1024 * 1024

F32 = jnp.float32
BF16 = jnp.bfloat16


def _params(*sem):
    return pltpu.CompilerParams(dimension_semantics=sem, vmem_limit_bytes=VMEM_LIMIT)


def _full(shape):
    zeros = (0,) * len(shape)
    return pl.BlockSpec(shape, lambda *_: zeros)


def _rms(x, g):
    ms = jnp.mean(x * x, axis=-1, keepdims=True)
    return x * lax.rsqrt(ms + EPS) * g


def _dot(a, b):
    return jnp.dot(a, b, preferred_element_type=F32)


def _dot_f32lhs(w, x):
    hi = x.astype(BF16)
    r1 = x - hi.astype(F32)
    mid = r1.astype(BF16)
    lo = (r1 - mid.astype(F32)).astype(BF16)
    return _dot(w, hi) + _dot(w, mid) + _dot(w, lo)


def _dot_f32rhs(x, w):
    hi = x.astype(BF16)
    r1 = x - hi.astype(F32)
    mid = r1.astype(BF16)
    lo = (r1 - mid.astype(F32)).astype(BF16)
    return _dot(hi, w) + _dot(mid, w) + _dot(lo, w)


def _rope_nat(x, cos, sin):
    half = HEAD_DIM // 2
    lane = lax.broadcasted_iota(jnp.int32, (1, LANES), 1)
    first = (lane % HEAD_DIM) < half
    outs = []
    for c in range(x.shape[1] // LANES):
        xc = x[:, c * LANES:(c + 1) * LANES]
        rot = jnp.where(first, pltpu.roll(xc, LANES - half, 1), pltpu.roll(xc, half, 1))
        outs.append(xc * cos + rot * sin)
    return jnp.concatenate(outs, axis=1)


def _inproj_kernel(x_ref, g_ref, w_ref, wg_ref, cos_ref, sin_ref, *outs, transposed, key_chunk):
    xb = _rms(x_ref[...], g_ref[...]).astype(BF16)
    cos = cos_ref[...]
    sin = sin_ref[...]
    scale = HEAD_DIM ** -0.5
    q = _dot(xb, w_ref[:, 0:Q_W])
    qr = _rope_nat(q, cos, sin)
    kv = _dot(xb, w_ref[:, Q_W:Q_W + 6 * KV_W])
    gates = jax.nn.sigmoid(_dot(xb, wg_ref[...]))
    k_s = _rope_nat(kv[:, 2 * KV_W:3 * KV_W], cos, sin)
    v_s = kv[:, 3 * KV_W:4 * KV_W]
    k_w = _rope_nat(kv[:, 4 * KV_W:5 * KV_W], cos, sin)
    v_w = kv[:, 5 * KV_W:6 * KV_W]
    kvc_ref = outs[3]
    kvc_ref[...] = kv[:, 0:2 * KV_W]
    if transposed:
        qT_ref, qrT_ref, gT_ref, _, kvcT_ref, kvsT_ref, kvwT_ref, ksb_ref, kwb_ref, vsT_ref, vwT_ref = outs
        qT_ref[...] = (q * (scale * LOG2E)).T.astype(BF16)
        qrT_ref[...] = (qr * (scale * LOG2E)).T.astype(BF16)
        gT_ref[...] = gates.T
        tm = x_ref.shape[0]
        rowi = lax.broadcasted_iota(jnp.int32, (tm, LANES), 0)
        lanei = lax.broadcasted_iota(jnp.int32, (tm, LANES), 1)
        blk = lax.shift_right_logical(rowi & (key_chunk - 1), SEL_BLOCK.bit_length() - 1)
        extra = lanei - HEAD_DIM
        aug = jnp.where((extra == blk) | (extra == key_chunk // SEL_BLOCK), 1.0, 0.0)
        for c in range(KV_W // LANES):
            for k_nat, k_ref in ((k_s, ksb_ref), (k_w, kwb_ref)):
                pair = k_nat[:, c * LANES:(c + 1) * LANES]
                k_ref[2 * c] = jnp.where(lanei < HEAD_DIM, pair, aug).astype(BF16)
                k_ref[2 * c + 1] = jnp.where(lanei < HEAD_DIM, pltpu.roll(pair, HEAD_DIM, 1), aug).astype(BF16)
        v_sT = v_s.T
        v_wT = v_w.T
        ones_rows = jnp.where(lax.broadcasted_iota(jnp.int32, (V_ROWS - HEAD_DIM, tm), 0) == 0, 1.0, 0.0)
        for h in range(N_KV_HEADS):
            for vT, v_ref in ((v_sT, vsT_ref), (v_wT, vwT_ref)):
                v_ref[h * V_ROWS:(h + 1) * V_ROWS, :] = jnp.concatenate(
                    [vT[h * HEAD_DIM:(h + 1) * HEAD_DIM, :], ones_rows], axis=0).astype(BF16)
        kvcT_ref[0] = kv[:, 0:2 * KV_W].T
        kvsT_ref[0, 0:KV_W, :] = k_s.T
        kvsT_ref[0, KV_W:2 * KV_W, :] = v_sT
        kvwT_ref[0, 0:KV_W, :] = k_w.T
        kvwT_ref[0, KV_W:2 * KV_W, :] = v_wT
    else:
        q_ref, qr_ref, gt_ref, _, kvs_ref, kvw_ref = outs
        q_ref[...] = (q * scale).astype(BF16)
        qr_ref[...] = (qr * scale).astype(BF16)
        gt_ref[...] = gates
        kvs_ref[:, 0:KV_W] = k_s
        kvs_ref[:, KV_W:2 * KV_W] = v_s
        kvw_ref[:, 0:KV_W] = k_w
        kvw_ref[:, KV_W:2 * KV_W] = v_w


def _inproj(x, g, w_main, w_gate, cos_t, sin_t, *, tm, pos_blocks, transposed, key_chunk=LANES):
    rows, d = x.shape
    assert tm % key_chunk == 0 or not transposed
    n = rows // tm
    row_blk = lambda w: pl.BlockSpec((tm, w), lambda i: (i, 0))
    col_blk = lambda h: pl.BlockSpec((h, tm), lambda i: (0, i))
    tab = pl.BlockSpec((tm, LANES), lambda i: (i % pos_blocks, 0))
    kv_nat = jax.ShapeDtypeStruct((rows, 2 * KV_W), F32)
    if transposed:
        seqs = rows // (pos_blocks * tm)
        kh = pl.BlockSpec((N_KV_HEADS, tm, LANES), lambda i: (0, i, 0))
        kvT = pl.BlockSpec((1, 2 * KV_W, tm), lambda i: (i // pos_blocks, 0, i % pos_blocks))
        out_shape = ([jax.ShapeDtypeStruct((Q_W, rows), BF16)] * 2 + [jax.ShapeDtypeStruct((LANES, rows), F32)]
                     + [kv_nat] + [jax.ShapeDtypeStruct((seqs, 2 * KV_W, pos_blocks * tm), F32)] * 3
                     + [jax.ShapeDtypeStruct((N_KV_HEADS, rows, LANES), BF16)] * 2
                     + [jax.ShapeDtypeStruct((N_KV_HEADS * V_ROWS, rows), BF16)] * 2)
        out_specs = ([col_blk(Q_W)] * 2 + [col_blk(LANES)] + [row_blk(2 * KV_W)] + [kvT] * 3 + [kh] * 2
                     + [col_blk(N_KV_HEADS * V_ROWS)] * 2)
    else:
        out_shape = ([jax.ShapeDtypeStruct((rows, Q_W), BF16)] * 2 + [jax.ShapeDtypeStruct((rows, LANES), F32)]
                     + [kv_nat] * 3)
        out_specs = [row_blk(Q_W)] * 2 + [row_blk(LANES)] + [row_blk(2 * KV_W)] * 3
    return pl.pallas_call(
        functools.partial(_inproj_kernel, transposed=transposed, key_chunk=key_chunk),
        grid=(n,),
        in_specs=[row_blk(d), _full((1, d)), _full(w_main.shape), _full(w_gate.shape), tab, tab],
        out_specs=out_specs,
        out_shape=out_shape,
        compiler_params=_params("parallel"),
        name="nsa_inproj",
    )(x, g, w_main, w_gate, cos_t, sin_t)


_KV_CHUNKS = 2 * KV_W // LANES


def _compress_half(load_rows, kv, nrows, wbd_ref, pe_ref):
    ratio = CMP_BLOCK // CMP_STRIDE
    accs = [jnp.broadcast_to(pe_ref[kv * ratio + r:kv * ratio + r + 1, :], (nrows, KV_W)) for r in range(ratio)]
    for s in range(CMP_STRIDE):
        lhs = load_rows(s, kv).astype(BF16)
        for r in range(ratio):
            accs[r] = accs[r] + _dot(lhs, wbd_ref[(kv * ratio + r) * CMP_STRIDE + s])
    return accs


def _compress_prompt_kernel(*refs, nsb):
    x_refs = refs[:_KV_CHUNKS]
    wbd_ref, pe_ref, w2_ref, kc_ref, vcT_ref, sh_ref = refs[_KV_CHUNKS:]

    def load_rows(s, kv):
        per_half = _KV_CHUNKS // 2
        return jnp.concatenate([x_refs[kv * per_half + c][0, pl.ds(s, nsb, stride=CMP_STRIDE), :]
                                for c in range(per_half)], axis=1)

    sh_ref[nsb:nsb + SUBLANES, :] = jnp.zeros((SUBLANES, KV_W), F32)
    for kv in range(2):
        pr0, pr1 = _compress_half(load_rows, kv, nsb, wbd_ref, pe_ref)
        sh_ref[0:nsb, :] = pr1
        h = pr0 + sh_ref[pl.ds(1, nsb), :]
        out = _dot(jax.nn.gelu(h).astype(BF16), w2_ref[kv])
        if kv == 0:
            for hh in range(N_KV_HEADS):
                kc_ref[0, hh] = out[:, hh * HEAD_DIM:(hh + 1) * HEAD_DIM].astype(BF16)
        else:
            vcT_ref[0] = out.T.astype(BF16)


def _compress_prompt(kvc3, wbd, pe_t, w2bd):
    b, t, _ = kvc3.shape
    nsb = t // CMP_STRIDE
    return pl.pallas_call(
        functools.partial(_compress_prompt_kernel, nsb=nsb),
        grid=(b,),
        in_specs=[pl.BlockSpec((1, t, LANES), lambda i, c=c: (i, 0, c)) for c in range(_KV_CHUNKS)]
        + [_full(wbd.shape), _full(pe_t.shape), _full(w2bd.shape)],
        out_specs=[pl.BlockSpec((1, N_KV_HEADS, nsb, HEAD_DIM), lambda i: (i, 0, 0, 0)),
                   pl.BlockSpec((1, KV_W, nsb), lambda i: (i, 0, 0))],
        out_shape=[jax.ShapeDtypeStruct((b, N_KV_HEADS, nsb, HEAD_DIM), BF16),
                   jax.ShapeDtypeStruct((b, KV_W, nsb), BF16)],
        scratch_shapes=[pltpu.VMEM((nsb + SUBLANES, KV_W), F32)],
        compiler_params=_params("parallel"),
        name="nsa_compress_prompt",
    )(*([kvc3] * _KV_CHUNKS), wbd, pe_t, w2bd)


def _topk_mask_T(imp, n_top):
    ns, w = imp.shape
    nblk = ns // SUBLANES
    blocks = [imp[r * SUBLANES:(r + 1) * SUBLANES, :] for r in range(nblk)]
    cnts = [jnp.zeros((SUBLANES, w), F32) for _ in range(nblk)]
    sub = lax.broadcasted_iota(jnp.int32, (SUBLANES, w), 0)
    for sp in range(ns):
        row = blocks[sp // SUBLANES][sp % SUBLANES:sp % SUBLANES + 1, :]
        for r in range(nblk):
            blk = blocks[r]
            if sp < r * SUBLANES:
                beats = jnp.where(row >= blk, 1.0, 0.0)
            elif sp >= (r + 1) * SUBLANES:
                beats = jnp.where(row > blk, 1.0, 0.0)
            else:
                beats = jnp.where(sub > (sp - r * SUBLANES), jnp.where(row >= blk, 1.0, 0.0),
                                  jnp.where(row > blk, 1.0, 0.0))
            cnts[r] = cnts[r] + beats
    return jnp.concatenate([jnp.where(c < n_top, 1.0, 0.0) for c in cnts], axis=0)


def _online_chunks(states, k_cs, vT_cs, qTs, bias):
    scores = [_dot(k_c, qT) for k_c, qT in zip(k_cs, qTs)]
    mids = []
    for (m, _), s in zip(states, scores):
        if bias is not None:
            s = s + bias
        m_new = jnp.maximum(m, jnp.max(s, axis=0, keepdims=True))
        mids.append((m_new, jnp.exp2(m - m_new), jnp.exp2(s - m_new).astype(BF16)))
    return tuple((m_new, alpha * acc + _dot(vT_c, p))
                 for (m_new, alpha, p), (_, acc), vT_c in zip(mids, states, vT_cs))


def _softmax_init(w):
    return (jnp.full((1, w), NEG, F32), jnp.zeros((V_ROWS, w), F32))


def _softmax_finish(carry):
    _, acc = carry
    return acc[0:HEAD_DIM, :] * (1.0 / jnp.maximum(acc[HEAD_DIM:HEAD_DIM + 1, :], 1e-20))


def _attn_prompt_kernel(qT_ref, qrT_ref, gT_ref, kc_ref, vcT_ref, ks_ref, vsT_ref, kw_ref, vwT_ref, ovT_ref,
                        o_ref, selb_ref, oc_ref, *, tq, nc, n_top):
    ck = tq
    i = pl.program_id(1)
    t0 = i * tq
    qpos = t0 + lax.broadcasted_iota(jnp.int32, (1, tq), 1)
    ncp = kc_ref.shape[2]
    ns = ovT_ref.shape[0]
    w = GROUP * tq
    bpc = ck // SEL_BLOCK
    sel_shift = SEL_BLOCK.bit_length() - 1
    kvhs = range(N_KV_HEADS)
    heads = [[kvh * GROUP + g for g in range(GROUP)] for kvh in kvhs]
    rows = [pl.ds(kvh * HEAD_DIM, HEAD_DIM) for kvh in kvhs]
    vrows = [pl.ds(kvh * V_ROWS, V_ROWS) for kvh in kvhs]
    qrT = [jnp.concatenate([qrT_ref[h * HEAD_DIM:(h + 1) * HEAD_DIM, :] for h in heads[kvh]], axis=1)
           for kvh in kvhs]
    kl = lax.broadcasted_iota(jnp.int32, (ck, tq), 0)
    ql = lax.broadcasted_iota(jnp.int32, (ck, tq), 1)
    tile4 = lambda b: jnp.concatenate([b] * GROUP, axis=1)
    key_le_query = tile4(jnp.where(kl <= ql, 0.0, MASKED))
    key_ge_query = tile4(jnp.where(kl >= ql, 0.0, MASKED))
    bias_rows = selb_ref.shape[2]
    zero_rows = jnp.zeros((LANES - HEAD_DIM - bias_rows, w), BF16)

    def with_bias_rows(kvh, tile):
        return jnp.concatenate([qrT[kvh], tile, zero_rows], axis=0)

    cidx = lax.broadcasted_iota(jnp.int32, (ncp, 1), 0)
    valid = ((cidx * CMP_STRIDE + (CMP_BLOCK - 1)) <= qpos) & (cidx < nc)
    sidx = lax.broadcasted_iota(jnp.int32, (ns, 1), 0)
    cur = lax.shift_right_logical(qpos, sel_shift)
    forced = (sidx == 0) | (sidx == cur) | (sidx == cur - 1)
    causal = (sidx * SEL_BLOCK) <= qpos
    cmp_scores = [
        _dot(kc_ref[0, kvh], jnp.concatenate([qT_ref[h * HEAD_DIM:(h + 1) * HEAD_DIM, :] for h in heads[kvh]], axis=1))
        for kvh in kvhs]
    for kvh in kvhs:
        s = cmp_scores[kvh]
        probs = []
        for g in range(GROUP):
            sm = jnp.where(valid, s[:, g * tq:(g + 1) * tq], NEG)
            mx = jnp.max(sm, axis=0, keepdims=True)
            e = jnp.where(valid, jnp.exp2(sm - mx), 0.0)
            den = jnp.maximum(jnp.sum(e, axis=0, keepdims=True), 1e-20)
            probs.append(e / den)
        oc_ref[kvh] = _dot(vcT_ref[0, rows[kvh], :], jnp.concatenate(probs, axis=1).astype(BF16))
        psum = probs[0]
        for g in range(1, GROUP):
            psum = psum + probs[g]
        imp = _dot_f32lhs(ovT_ref[...], psum)
        imp = jnp.where(forced, FORCE, jnp.where(causal, imp, NEG))
        selb = (_topk_mask_T(imp, n_top) - 1.0) * (-MASKED)
        fill = jnp.zeros((bias_rows - bpc, tq), F32)
        for c in range(ns // bpc):
            selb_ref[kvh, c] = jnp.concatenate([selb[c * bpc:(c + 1) * bpc, :], fill], axis=0).astype(BF16)

    n_back = WINDOW // ck
    states = tuple(_softmax_init(w) for _ in kvhs)
    rowb = lax.broadcasted_iota(jnp.int32, (bias_rows, w), 0)
    for r in range(n_back + 1):
        a = i - n_back + r
        kst = pl.multiple_of(jnp.maximum(a, 0) * ck, ck)
        skip = jnp.where(a < 0, MASKED, 0.0)
        tile = jnp.where(rowb == bpc, skip, 0.0).astype(BF16)
        bias = key_le_query if r == n_back else (key_ge_query if r == 0 else None)
        states = _online_chunks(states, [kw_ref[kvh, pl.ds(kst, ck), :] for kvh in kvhs],
                                [vwT_ref[vrows[kvh], pl.ds(kst, ck)] for kvh in kvhs],
                                [with_bias_rows(kvh, tile) for kvh in kvhs], bias)
    ow = [_softmax_finish(st) for st in states]

    def chunk_step(c, states, diagonal):
        kst = pl.multiple_of(c * ck, ck)
        return _online_chunks(states, [ks_ref[kvh, pl.ds(kst, ck), :] for kvh in kvhs],
                              [vsT_ref[vrows[kvh], pl.ds(kst, ck)] for kvh in kvhs],
                              [with_bias_rows(kvh, tile4(selb_ref[kvh, c])) for kvh in kvhs],
                              key_le_query if diagonal else None)

    states = lax.fori_loop(0, i, lambda c, st: chunk_step(c, st, False), tuple(_softmax_init(w) for _ in kvhs))
    states = chunk_step(i, states, True)

    for kvh in kvhs:
        def gate_row(j, kvh=kvh):
            return jnp.concatenate([gT_ref[h * 3 + j:h * 3 + j + 1, :] for h in heads[kvh]], axis=1)

        oT = gate_row(0) * oc_ref[kvh] + gate_row(1) * _softmax_finish(states[kvh]) + gate_row(2) * ow[kvh]
        for g, h in enumerate(heads[kvh]):
            o_ref[h * HEAD_DIM:(h + 1) * HEAD_DIM, :] = oT[:, g * tq:(g + 1) * tq].astype(BF16)


def _attn_prompt(qT, qrT, gT, kc, vcT, ksb, vsT, kwb, vwT, ovT, *, batch, seq, tq, nc, n_top):
    nq = seq // tq
    nsb = kc.shape[2]
    ns = ovT.shape[0]
    col = lambda h: pl.BlockSpec((h, tq), lambda b, i: (0, b * nq + i))
    kh = pl.BlockSpec((N_KV_HEADS, seq, LANES), lambda b, i: (0, b, 0))
    vt = pl.BlockSpec((N_KV_HEADS * V_ROWS, seq), lambda b, i: (0, b))
    bf16_sublanes = 2 * SUBLANES
    return pl.pallas_call(
        functools.partial(_attn_prompt_kernel, tq=tq, nc=nc, n_top=n_top),
        grid=(batch, nq),
        in_specs=[col(Q_W), col(Q_W), col(LANES),
                  pl.BlockSpec((1, N_KV_HEADS, nsb, HEAD_DIM), lambda b, i: (b, 0, 0, 0)),
                  pl.BlockSpec((1, KV_W, nsb), lambda b, i: (b, 0, 0)),
                  kh, vt, kh, vt, _full(ovT.shape)],
        out_specs=col(Q_W),
        out_shape=jax.ShapeDtypeStruct((Q_W, batch * seq), BF16),
        scratch_shapes=[pltpu.VMEM((N_KV_HEADS, ns * SEL_BLOCK // tq, bf16_sublanes, tq), BF16),
                        pltpu.VMEM((N_KV_HEADS, HEAD_DIM, GROUP * tq), F32)],
        compiler_params=_params("parallel", "arbitrary"),
        name="nsa_attn_prompt",
    )(qT, qrT, gT, kc, vcT, ksb, vsT, kwb, vwT, ovT)


def _oproj_kernel(o_ref, w_ref, x_ref, g1_ref, g2_ref, h_ref, xm_ref, *, transposed):
    if transposed:
        y = lax.dot_general(o_ref[...], w_ref[...], (((0,), (0,)), ((), ())), preferred_element_type=F32)
    else:
        y = _dot(o_ref[...], w_ref[...])
    h = x_ref[...] + _rms(y, g1_ref[...])
    h_ref[...] = h
    xm_ref[...] = _rms(h, g2_ref[...]).astype(BF16)


def _oproj(o, w_o, x, g1, g2, *, tm, transposed):
    rows, d = x.shape
    o_spec = (pl.BlockSpec((Q_W, tm), lambda i: (0, i)) if transposed else pl.BlockSpec((tm, Q_W), lambda i: (i, 0)))
    row = pl.BlockSpec((tm, d), lambda i: (i, 0))
    return pl.pallas_call(
        functools.partial(_oproj_kernel, transposed=transposed),
        grid=(rows // tm,),
        in_specs=[o_spec, _full(w_o.shape), row, _full((1, d)), _full((1, d))],
        out_specs=[row, row],
        out_shape=[jax.ShapeDtypeStruct((rows, d), F32), jax.ShapeDtypeStruct((rows, d), BF16)],
        compiler_params=_params("parallel"),
        name="nsa_oproj",
    )(o, w_o, x, g1, g2)


def _mlp_kernel(xm_ref, h_ref, w1_ref, w2_ref, g3_ref, gn_ref, *rest, ff_chunk, next_norm):
    if next_norm:
        h2_ref, xn_ref, acc_ref = rest
    else:
        h2_ref, acc_ref = rest
    xm = xm_ref[...]
    for c in range(w1_ref.shape[1] // ff_chunk):
        cols = slice(c * ff_chunk, (c + 1) * ff_chunk)
        hm = jnp.maximum(_dot(xm, w1_ref[:, cols]), 0.0)
        part = _dot((hm * hm).astype(BF16), w2_ref[cols, :])
        if c == 0:
            acc_ref[...] = part
        else:
            acc_ref[...] += part
    h2 = h_ref[...] + _rms(acc_ref[...], g3_ref[...])
    h2_ref[...] = h2
    if next_norm:
        xn_ref[...] = _rms(h2, gn_ref[...])


def _mlp(xm, h, w1, w2, g3, gn, *, tm, ff_chunk, next_norm):
    rows, d = h.shape
    row = pl.BlockSpec((tm, d), lambda i: (i, 0))
    out_shape = [jax.ShapeDtypeStruct((rows, d), F32)] * (2 if next_norm else 1)
    return pl.pallas_call(
        functools.partial(_mlp_kernel, ff_chunk=ff_chunk, next_norm=next_norm),
        grid=(rows // tm,),
        in_specs=[row, row, _full(w1.shape), _full(w2.shape), _full((1, d)), _full((1, d))],
        out_specs=[row] * len(out_shape),
        out_shape=out_shape,
        scratch_shapes=[pltpu.VMEM((tm, d), F32)],
        compiler_params=_params("parallel"),
        name="sq_relu_mlp",
    )(xm, h, w1, w2, g3, gn)


def _s5_chunk_kernel(u_ref, brow_ref, pre_ref, pim_ref, qre_ref, qim_ref, are_ref, aim_ref, y_ref, hfin_ref,
                     sre, sim, hre, him, cre, cim, *, nb):
    steps_per_iter = SUBLANES // nb
    rows = sre.shape[0]
    uw = SSM_UNIT * SSM_GROUP
    nt = u_ref.shape[2] // uw

    @pl.when(pl.program_id(1) == 0)
    def _():
        cre[...] = jnp.zeros(cre.shape, F32)
        cim[...] = jnp.zeros(cim.shape, F32)

    u = u_ref[0]
    sre[...] = _dot(u, pre_ref[0])
    sim[...] = _dot(u, pim_ref[0])
    ar = are_ref[0]
    ai = aim_ref[0]

    def body(it, carry):
        hr, hi = carry
        r0 = pl.multiple_of(it * SUBLANES, SUBLANES)
        sr8 = sre[pl.ds(r0, SUBLANES), :]
        si8 = sim[pl.ds(r0, SUBLANES), :]
        prev_r, prev_i = [], []
        for j in range(steps_per_iter):
            prev_r.append(hr)
            prev_i.append(hi)
            sr = sr8[j * nb:(j + 1) * nb, :]
            si = si8[j * nb:(j + 1) * nb, :]
            hr, hi = ar * hr - ai * hi + sr, ar * hi + ai * hr + si
        hre[pl.ds(r0, SUBLANES), :] = jnp.concatenate(prev_r, axis=0)
        him[pl.ds(r0, SUBLANES), :] = jnp.concatenate(prev_i, axis=0)
        return hr, hi

    hr, hi = lax.fori_loop(0, rows // SUBLANES, body, (cre[...], cim[...]))
    cre[...] = hr
    cim[...] = hi
    hfin_ref[0, 0] = hr
    hfin_ref[0, 1] = hi

    hb_re = hre[...].astype(BF16)
    hb_im = him[...].astype(BF16)
    for t2 in range(0, nt, 2):
        cols = slice(t2 * uw, (t2 + 2) * uw)
        acc = _dot(hb_re, qre_ref[0, :, cols]) + _dot(hb_im, qim_ref[0, :, cols])
        for t1 in range(t2 + 2):
            lag0 = t2 - t1 + 1
            acc = acc + _dot(u[:, t1 * uw:(t1 + 1) * uw], brow_ref[0, :, lag0 * uw:(lag0 + 2) * uw])
        y_ref[0, :, cols] = acc


def _s5_chunk(u2, ops, *, nb):
    nunit, nch, width = u2.shape
    sw = ops["p_re"].shape[2]
    rows = _tile(nch, S5_ROW_BLOCK)
    per_unit = lambda a: pl.BlockSpec((1,) + a.shape[1:], lambda i, r: (i,) + (0,) * (a.ndim - 1))
    row_blk = pl.BlockSpec((1, rows, width), lambda i, r: (i, r, 0))
    args = (u2, ops["brow"], ops["p_re"], ops["p_im"], ops["q_re"], ops["q_im"], ops["a_chunk_re"], ops["a_chunk_im"])
    return pl.pallas_call(
        functools.partial(_s5_chunk_kernel, nb=nb),
        grid=(nunit, nch // rows),
        in_specs=[row_blk] + [per_unit(a) for a in args[1:]],
        out_specs=[row_blk, pl.BlockSpec((1, 2, nb, sw), lambda i, r: (i, 0, 0, 0))],
        out_shape=[jax.ShapeDtypeStruct((nunit, nch, width), F32), jax.ShapeDtypeStruct((nunit, 2, nb, sw), F32)],
        scratch_shapes=[pltpu.VMEM((rows, sw), F32)] * 4 + [pltpu.VMEM((nb, sw), F32)] * 2,
        compiler_params=_params("parallel", "arbitrary"),
        name="s5_chunk_scan",
    )(*args)


def _s5_step_kernel(u_ref, h0r_ref, h0i_ref, bre_ref, bim_ref, cre_ref, cim_ref, are_ref, aim_ref,
                    y_ref, hr_ref, hi_ref, *, npair):
    for p in range(npair):
        u = u_ref[p]
        ar = are_ref[p]
        ai = aim_ref[p]
        h0r = h0r_ref[p]
        h0i = h0i_ref[p]
        hr = ar * h0r - ai * h0i + _dot(u, bre_ref[p])
        hi = ar * h0i + ai * h0r + _dot(u, bim_ref[p])
        hr_ref[p] = hr
        hi_ref[p] = hi
        y_ref[p] = _dot(hr.astype(BF16), cre_ref[p]) + _dot(hi.astype(BF16), cim_ref[p])


def _s5_step(u2, h0r, h0i, ops):
    npair, rows, width = u2.shape
    args = (u2, h0r, h0i, ops["b1_re"], ops["b1_im"], ops["c1_re"], ops["c1_im"], ops["a1_re"], ops["a1_im"])
    return pl.pallas_call(
        functools.partial(_s5_step_kernel, npair=npair),
        grid=(1,),
        in_specs=[_full(a.shape) for a in args],
        out_specs=[_full((npair, rows, width)), _full(h0r.shape), _full(h0r.shape)],
        out_shape=[jax.ShapeDtypeStruct((npair, rows, width), F32), jax.ShapeDtypeStruct(h0r.shape, F32),
                   jax.ShapeDtypeStruct(h0r.shape, F32)],
        compiler_params=_params("arbitrary"),
        name="s5_single_step",
    )(*args)


def _s5_out_kernel(yc_ref, u_ref, d_ref, wg_ref, bg_ref, h_ref, g1_ref, g2_ref, h3_ref, xm_ref):
    y = jax.nn.gelu(yc_ref[...] + d_ref[...] * u_ref[...])
    out = y * jax.nn.sigmoid(_dot(y.astype(BF16), wg_ref[...]) + bg_ref[...])
    h3 = h_ref[...] + _rms(out, g1_ref[...])
    h3_ref[...] = h3
    xm_ref[...] = _rms(h3, g2_ref[...]).astype(BF16)


def _s5_out(yc, u, d_skip, w_glu, b_glu, h, g1, g2, *, tm):
    rows, d = h.shape
    row = pl.BlockSpec((tm, d), lambda i: (i, 0))
    vec = _full((1, d))
    return pl.pallas_call(
        _s5_out_kernel,
        grid=(rows // tm,),
        in_specs=[row, row, vec, _full(w_glu.shape), vec, row, vec, vec],
        out_specs=[row, row],
        out_shape=[jax.ShapeDtypeStruct((rows, d), F32), jax.ShapeDtypeStruct((rows, d), BF16)],
        compiler_params=_params("parallel"),
        name="s5_glu_out",
    )(yc, u, d_skip, w_glu, b_glu, h, g1, g2)


def _s5_operators(a_re, a_im, log_dt, b_re, b_im, c_re, c_im):
    hp = lax.Precision.HIGHEST
    g, n = a_re.shape
    gu = SSM_UNIT
    nunit = g // gu
    L = SSM_CHUNK
    uw = gu * SSM_GROUP
    sw = gu * n
    a = lax.complex(a_re.astype(F32), a_im.astype(F32))
    dt = jnp.exp(log_dt.astype(F32))[:, None]
    a_bar = jnp.exp(a * dt)
    b_bar = ((a_bar - 1.0) / a)[:, :, None] * lax.complex(b_re.astype(F32), b_im.astype(F32))
    c = lax.complex(c_re.astype(F32), c_im.astype(F32))
    pows = [jnp.ones_like(a_bar)]
    for _ in range(L):
        pows.append(pows[-1] * a_bar)
    a_pow = jnp.stack(pows).reshape(L + 1, nunit, sw)
    apr = jnp.real(a_pow)
    api = jnp.imag(a_pow)
    eye = jnp.eye(gu, dtype=F32)

    def bd_in(x):
        return jnp.einsum("pgnd,gh->pgdhn", x.reshape(nunit, gu, n, SSM_GROUP), eye).reshape(nunit, uw, sw)

    def bd_out(x):
        return jnp.einsum("pgcn,gh->pgnhc", x.reshape(nunit, gu, SSM_GROUP, n), eye).reshape(nunit, sw, uw)

    p0r, p0i = bd_in(jnp.real(b_bar)), bd_in(jnp.imag(b_bar))
    q0r, q0i = bd_out(jnp.real(c)), bd_out(jnp.imag(c))
    sr = apr[::-1][1:][:, :, None, :]
    si = api[::-1][1:][:, :, None, :]
    stack_rows = lambda x: x.transpose(1, 0, 2, 3).reshape(nunit, L * uw, sw)
    p_re = stack_rows(p0r[None] * sr - p0i[None] * si)
    p_im = stack_rows(p0r[None] * si + p0i[None] * sr)
    tr = apr[1:][:, :, :, None]
    ti = api[1:][:, :, :, None]
    stack_cols = lambda x: x.transpose(1, 2, 0, 3).reshape(nunit, sw, L * uw)
    q_re = stack_cols(q0r[None] * tr - q0i[None] * ti)
    q_im = stack_cols(-(q0r[None] * ti + q0i[None] * tr))
    xr = p0r[None] * apr[:L][:, :, None, :] - p0i[None] * api[:L][:, :, None, :]
    xi = p0r[None] * api[:L][:, :, None, :] + p0i[None] * apr[:L][:, :, None, :]
    b_lag = (jnp.einsum("tpus,psv->ptuv", xr, q0r, precision=hp)
             - jnp.einsum("tpus,psv->ptuv", xi, q0i, precision=hp))
    brow = jnp.concatenate([jnp.zeros((nunit, 1, uw, uw), F32), b_lag], axis=1)
    brow = brow.transpose(0, 2, 1, 3).reshape(nunit, uw, (L + 1) * uw)
    lane_row = lambda x: x.reshape(nunit, 1, sw)
    return {
        "brow": brow.astype(BF16),
        "p_re": p_re.astype(BF16), "p_im": p_im.astype(BF16),
        "q_re": q_re.astype(BF16), "q_im": q_im.astype(BF16),
        "a_chunk_re": lane_row(apr[L]), "a_chunk_im": lane_row(api[L]),
        "a1_re": lane_row(apr[1]), "a1_im": lane_row(api[1]),
        "b1_re": p0r.astype(BF16), "b1_im": p0i.astype(BF16),
        "c1_re": q0r.astype(BF16), "c1_im": (-q0i).astype(BF16),
    }


def _compress_paged_kernel(pt_ref, *refs, pg, nsb):
    del pt_ref
    pages = refs[:pg]
    perm_ref, wbd_ref, pe_ref, w2_ref, out_ref, h0_ref, h1_ref = refs[pg:]
    j = pl.program_id(1)
    sbp = PAGE_SIZE // CMP_STRIDE
    nrows = pg * sbp
    pair_rows = 2 * sbp

    @pl.when(j == 0)
    def _():
        h1_ref[:, nsb:nsb + SUBLANES, :] = jnp.zeros((2, SUBLANES, KV_W), F32)

    r0 = pl.multiple_of(j * nrows, nrows)
    for kv in range(2):
        staged = []
        for q in range(pg // 2):
            z = jnp.concatenate([pages[2 * q][0, kv], pages[2 * q + 1][0, kv]], axis=1).astype(BF16)
            staged.append(_dot_nt(perm_ref[...], z).astype(BF16))

        def load_rows(s, kv, staged=staged):
            return jnp.concatenate([x[s * pair_rows:(s + 1) * pair_rows, :] for x in staged], axis=0)

        pr0, pr1 = _compress_half(load_rows, kv, nrows, wbd_ref, pe_ref)
        h0_ref[kv, pl.ds(r0, nrows), :] = pr0
        h1_ref[kv, pl.ds(r0, nrows), :] = pr1

    @pl.when(j == pl.num_programs(1) - 1)
    def _():
        for kv in range(2):
            h = h0_ref[kv] + h1_ref[kv, pl.ds(1, nsb), :]
            out = _dot(jax.nn.gelu(h).astype(BF16), w2_ref[kv])
            out_ref[0, :, kv * KV_W:(kv + 1) * KV_W] = out.astype(BF16)


def _page_specs(pg, pages_per_sample):
    def spec(i):
        return pl.BlockSpec((1, 2, KV_W, PAGE_SIZE),
                            lambda b, j, pt: (pt[b * pages_per_sample + j * pg + i], 0, 0, 0))
    return [spec(i) for i in range(pg)]


def _compress_paged(pages, pt_flat, wbd, pe_t, w2bd, *, nb, pages_per_sample, pg):
    nsb = pages_per_sample * PAGE_SIZE // CMP_STRIDE
    sbp = PAGE_SIZE // CMP_STRIDE
    i_out = jnp.arange(2 * PAGE_SIZE)
    s_i, pg_i, n_i = i_out // (2 * sbp), (i_out // sbp) % 2, i_out % sbp
    perm = (i_out[None, :] == (pg_i * PAGE_SIZE + n_i * CMP_STRIDE + s_i)[:, None]).astype(BF16)
    c3 = lambda shape: pl.BlockSpec(shape, lambda b, j, pt: (0,) * len(shape))
    grid_spec = pltpu.PrefetchScalarGridSpec(
        num_scalar_prefetch=1,
        grid=(nb, pages_per_sample // pg),
        in_specs=_page_specs(pg, pages_per_sample) + [c3(perm.shape), c3(wbd.shape), c3(pe_t.shape), c3(w2bd.shape)],
        out_specs=pl.BlockSpec((1, nsb, 2 * KV_W), lambda b, j, pt: (b, 0, 0)),
        scratch_shapes=[pltpu.VMEM((2, nsb, KV_W), F32), pltpu.VMEM((2, nsb + SUBLANES, KV_W), F32)],
    )
    return pl.pallas_call(
        functools.partial(_compress_paged_kernel, pg=pg, nsb=nsb),
        grid_spec=grid_spec,
        out_shape=jax.ShapeDtypeStruct((nb, nsb, 2 * KV_W), BF16),
        compiler_params=_params("parallel", "arbitrary"),
        name="nsa_compress_paged",
    )(pt_flat, *([pages] * pg), perm, wbd, pe_t, w2bd)


def _topk_mask_lanes(imp, n_top, ns_valid):
    lane = lax.broadcasted_iota(jnp.int32, imp.shape, 1)
    cnt = jnp.zeros(imp.shape, F32)
    for sp in range(ns_valid):
        col = imp[:, sp:sp + 1]
        cnt = cnt + jnp.where(lane > sp, jnp.where(col >= imp, 1.0, 0.0), jnp.where(col > imp, 1.0, 0.0))
    return jnp.where((cnt < n_top) & (lane < ns_valid), 1.0, 0.0)


def _dot_nt(a, b):
    return lax.dot_general(a, b, (((1,), (1,)), ((), ())), preferred_element_type=F32)


def _attn_sample_kernel(pt_ref, *refs, pg, past, nc, ns_valid, n_top):
    del pt_ref
    pages = refs[:pg]
    (q_ref, qr_ref, g_ref, kcvc_ref, ksn_ref, kwn_ref, win_ref, ov_ref, e_ref, gs_ref, o_ref,
     m_ref, l_ref, acc_ref, sel_ref, oc_ref, ow_ref) = refs[pg:]
    j = pl.program_id(1)
    ncp = kcvc_ref.shape[1]
    nsp = ov_ref.shape[1]
    wlen = win_ref.shape[3]
    row = lax.broadcasted_iota(jnp.int32, (N_HEADS, KV_W), 0)
    lane = lax.broadcasted_iota(jnp.int32, (N_HEADS, KV_W), 1)
    own = (lane // HEAD_DIM) == (row // GROUP)

    def spread(ref):
        q = ref[0]
        return jnp.where(own, jnp.concatenate([q] * N_KV_HEADS, axis=1), jnp.zeros((N_HEADS, KV_W), BF16))

    def update(state, s, vs):
        m, l, acc = state
        m_new = jnp.maximum(m, jnp.max(s, axis=1, keepdims=True))
        alpha = jnp.exp(m - m_new)
        p = jnp.exp(s - m_new)
        l = alpha * l + jnp.sum(p, axis=1, keepdims=True)
        pv = None
        for st, sz, v, feature_major in vs:
            pb = p[:, st:st + sz].astype(BF16)
            t = _dot_nt(pb, v) if feature_major else _dot(pb, v)
            pv = t if pv is None else pv + t
        return m_new, l, alpha * acc + pv

    def init():
        return (jnp.full((N_HEADS, 1), NEG, F32), jnp.zeros((N_HEADS, 1), F32), jnp.zeros((N_HEADS, KV_W), F32))

    def new_row_update(state, qbd, new_ref):
        r8 = lax.broadcasted_iota(jnp.int32, (SUBLANES, 2 * KV_W), 0)
        tile = jnp.where(r8 == 0, jnp.broadcast_to(new_ref[0], (SUBLANES, 2 * KV_W)), 0.0).astype(BF16)
        s = _dot_nt(qbd, tile[:, 0:KV_W])
        l8 = lax.broadcasted_iota(jnp.int32, (N_HEADS, SUBLANES), 1)
        s = jnp.where(l8 == 0, s, MASKED)
        return update(state, s, [(0, SUBLANES, tile[:, KV_W:2 * KV_W], False)])

    def put(state):
        m, l, acc = state
        m_ref[...] = jnp.broadcast_to(m, m_ref.shape)
        l_ref[...] = jnp.broadcast_to(l, l_ref.shape)
        acc_ref[...] = acc

    qrbd = spread(qr_ref)

    @pl.when(j == 0)
    def _():
        s = _dot_nt(spread(q_ref), kcvc_ref[0, :, 0:KV_W])
        cidx = lax.broadcasted_iota(jnp.int32, (1, ncp), 1)
        valid = ((cidx * CMP_STRIDE + (CMP_BLOCK - 1)) <= past) & (cidx < nc)
        sm = jnp.where(valid, s, NEG)
        mx = jnp.max(sm, axis=1, keepdims=True)
        e = jnp.where(valid, jnp.exp(sm - mx), 0.0)
        p = e / jnp.maximum(jnp.sum(e, axis=1, keepdims=True), 1e-20)
        oc_ref[...] = _dot(p.astype(BF16), kcvc_ref[0, :, KV_W:2 * KV_W])
        imp = _dot_f32rhs(_dot_f32lhs(gs_ref[...], p), ov_ref[...])
        sidx = lax.broadcasted_iota(jnp.int32, (1, nsp), 1)
        cur = past // SEL_BLOCK
        forced = (sidx == 0) | (sidx == cur) | (sidx == cur - 1)
        causal = (sidx * SEL_BLOCK) <= past
        imp = jnp.where(forced, FORCE, jnp.where(causal, imp, NEG))
        imp = jnp.where(sidx < ns_valid, imp, MASKED)
        sel_ref[...] = _topk_mask_lanes(imp, n_top, ns_valid)
        sw = _dot(qrbd, win_ref[0, 0].astype(BF16))
        wpos = past - wlen + lax.broadcasted_iota(jnp.int32, (1, wlen), 1)
        sw = jnp.where((wpos >= 0) & (past - wpos <= WINDOW), sw, MASKED)
        st = update(init(), sw, [(0, wlen, win_ref[0, 1].astype(BF16), True)])
        st = new_row_update(st, qrbd, kwn_ref)
        ow_ref[...] = st[2] * (1.0 / jnp.maximum(st[1], 1e-20))
        put(new_row_update(init(), qrbd, ksn_ref))

    s = jnp.concatenate([_dot(qrbd, r[0, 0].astype(BF16)) for r in pages], axis=1)
    mask = _dot(sel_ref[...].astype(BF16), e_ref[0]) > 0.5
    s = jnp.where(mask, s, MASKED)
    vs = [(i * PAGE_SIZE, PAGE_SIZE, r[0, 1].astype(BF16), True) for i, r in enumerate(pages)]
    put(update((m_ref[:, 0:1], l_ref[:, 0:1], acc_ref[...]), s, vs))

    @pl.when(j == pl.num_programs(1) - 1)
    def _():
        g = g_ref[0]
        os_ = acc_ref[...] * (1.0 / jnp.maximum(l_ref[:, 0:1], 1e-20))
        o = g[:, 0:1] * oc_ref[...] + g[:, 1:2] * os_ + g[:, 2:3] * ow_ref[...]
        o = jnp.where(own, o, 0.0)
        out = o[:, 0:HEAD_DIM]
        for h in range(1, N_KV_HEADS):
            out = out + o[:, h * HEAD_DIM:(h + 1) * HEAD_DIM]
        o_ref[0] = out.astype(BF16)


def _attn_sample(pages, pt_flat, q3, qr3, g3, kcvc, ks_new, kw_new, win, ov, e_mat, gsum, *, nb, pages_per_sample,
                 pg, past, nc, ns_valid, n_top):
    nsp = ov.shape[1]
    per_b = lambda shape: pl.BlockSpec((1,) + shape, lambda b, j, pt: (b,) + (0,) * len(shape))
    const = lambda shape: pl.BlockSpec(shape, lambda b, j, pt: (0,) * len(shape))
    grid_spec = pltpu.PrefetchScalarGridSpec(
        num_scalar_prefetch=1,
        grid=(nb, pages_per_sample // pg),
        in_specs=_page_specs(pg, pages_per_sample) + [
            per_b((N_HEADS, HEAD_DIM)), per_b((N_HEADS, HEAD_DIM)), per_b((N_HEADS, 3)),
            per_b(kcvc.shape[1:]), per_b((1, 2 * KV_W)), per_b((1, 2 * KV_W)), per_b(win.shape[1:]),
            const(ov.shape), pl.BlockSpec((1,) + e_mat.shape[1:], lambda b, j, pt: (j, 0, 0)), const(gsum.shape)],
        out_specs=per_b((N_HEADS, HEAD_DIM)),
        scratch_shapes=[pltpu.VMEM((N_HEADS, LANES), F32), pltpu.VMEM((N_HEADS, LANES), F32),
                        pltpu.VMEM((N_HEADS, KV_W), F32), pltpu.VMEM((N_HEADS, nsp), F32),
                        pltpu.VMEM((N_HEADS, KV_W), F32), pltpu.VMEM((N_HEADS, KV_W), F32)],
    )
    return pl.pallas_call(
        functools.partial(_attn_sample_kernel, pg=pg, past=past, nc=nc, ns_valid=ns_valid, n_top=n_top),
        grid_spec=grid_spec,
        out_shape=jax.ShapeDtypeStruct((nb, N_HEADS, HEAD_DIM), BF16),
        compiler_params=_params("parallel", "arbitrary"),
        name="nsa_attn_sample",
    )(pt_flat, *([pages] * pg), q3, qr3, g3, kcvc, ks_new, kw_new, win, ov, e_mat, gsum)


def _rope_tables(pos):
    half = HEAD_DIM // 2
    inv = ROPE_THETA ** (-jnp.arange(half, dtype=F32) / half)
    ang = pos.astype(F32)[:, None] * inv[None, :]
    cos = jnp.cos(ang)
    sin = jnp.sin(ang)
    reps = LANES // HEAD_DIM
    return (jnp.tile(jnp.concatenate([cos, cos], axis=1), (1, reps)),
            jnp.tile(jnp.concatenate([-sin, sin], axis=1), (1, reps)))


def _compress_weights(cmp_w1, cmp_w2, cmp_pe):
    ratio = CMP_BLOCK // CMP_STRIDE
    eye = jnp.eye(N_KV_HEADS, dtype=F32)
    w1r = cmp_w1.reshape(2, ratio, CMP_STRIDE, HEAD_DIM, HEAD_DIM)
    wbd = jnp.einsum("krsde,hg->krshdge", w1r, eye).reshape(2 * ratio * CMP_STRIDE, KV_W, KV_W).astype(BF16)
    w2bd = jnp.einsum("kef,hg->khegf", cmp_w2, eye).reshape(2, KV_W, KV_W).astype(BF16)
    pe_r = cmp_pe.reshape(2, ratio, CMP_STRIDE, HEAD_DIM).astype(F32)
    pe_w = jnp.einsum("krsd,krsde->kre", pe_r, w1r.astype(F32), precision=lax.Precision.HIGHEST)
    pe_t = jnp.tile(pe_w.reshape(2 * ratio, HEAD_DIM), (1, N_KV_HEADS))
    return wbd, pe_t, w2bd


def _overlap(nc, ncp, nsel, nsp):
    c_start = jnp.arange(ncp, dtype=jnp.int32)[:, None] * CMP_STRIDE
    s_start = jnp.arange(nsp, dtype=jnp.int32)[None, :] * SEL_BLOCK
    ov = (c_start < s_start + SEL_BLOCK) & (c_start + CMP_BLOCK > s_start)
    ov = ov & (jnp.arange(ncp)[:, None] < nc) & (jnp.arange(nsp)[None, :] < nsel)
    return ov.astype(BF16)


def _pick(*cands):
    return next(c for c in cands if c)


def _tile(n, pref):
    t = min(n, pref)
    while n % t:
        t //= 2
    return t


def kernel(x_prompt, x_sample, cache_kv_cmp, cache_kv_sel, cache_kv_win, state_ssm, page_table, norm_g, mlp_w1,
           mlp_w2, nsa_w_in, nsa_w_o, nsa_cmp_w1, nsa_cmp_w2, nsa_cmp_pe, s5_a_re, s5_a_im, s5_log_dt, s5_b_re,
           s5_b_im, s5_c_re, s5_c_im, s5_d, s5_w_glu, s5_b_glu):
    b, t, d = x_prompt.shape
    nb = x_sample.shape[0]
    pages_per_sample = page_table.shape[1]
    past = pages_per_sample * PAGE_SIZE
    rows_p = b * t
    g = norm_g.reshape(norm_g.shape[0], 4, 1, d)

    w_in = nsa_w_in[0]
    w_main = w_in[:, :Q_W + 6 * KV_W].astype(BF16)
    w_gate = jnp.pad(w_in[:, Q_W + 6 * KV_W:], ((0, 0), (0, LANES - 3 * N_HEADS))).astype(BF16)
    w_o = nsa_w_o[0].astype(BF16)
    wbd, pe_t, w2bd = _compress_weights(nsa_cmp_w1[0], nsa_cmp_w2[0], nsa_cmp_pe[0])
    w1 = mlp_w1.astype(BF16)
    w2 = mlp_w2.astype(BF16)
    w_glu = s5_w_glu[0].astype(BF16)
    ops = _s5_operators(s5_a_re[0], s5_a_im[0], s5_log_dt[0], s5_b_re[0], s5_b_im[0], s5_c_re[0], s5_c_im[0])
    d_skip = s5_d[0].reshape(1, d)
    b_glu = s5_b_glu[0].reshape(1, d)

    tm = _tile(rows_p, 512)
    ff_chunk = _tile(mlp_w1.shape[2], 1024)

    xp = x_prompt.reshape(rows_p, d)
    cos_p, sin_p = _rope_tables(jnp.arange(t, dtype=jnp.int32))
    tq = next(c for c in (2 * LANES, LANES) if t % c == 0 and WINDOW % c == 0)
    (qT, qrT, gT, kvc, kvcT, kvsT, kvwT, ksb, kwb, vsT, vwT) = _inproj(
        xp, g[0, 0], w_main, w_gate, cos_p, sin_p, tm=_tile(t, 512), pos_blocks=t // _tile(t, 512), transposed=True,
        key_chunk=tq)
    nsb_p = t // CMP_STRIDE
    nc_p = nsb_p - CMP_BLOCK // CMP_STRIDE + 1
    nsel_p = t // SEL_BLOCK
    kc, vcT = _compress_prompt(kvc.reshape(b, t, 2 * KV_W), wbd, pe_t, w2bd)
    ovT = _overlap(nc_p, nsb_p, nsel_p, nsel_p).T
    oT = _attn_prompt(qT, qrT, gT, kc, vcT, ksb, vsT, kwb, vwT, ovT, batch=b, seq=t, tq=tq, nc=nc_p,
                      n_top=min(TOP_N, nsel_p))
    hp, xm = _oproj(oT, w_o, xp, g[0, 1], g[0, 2], tm=tm, transposed=True)
    hp, xn1 = _mlp(xm, hp, w1[0], w2[0], g[0, 3], g[1, 0], tm=tm, ff_chunk=ff_chunk, next_norm=True)

    pw = SSM_UNIT * SSM_GROUP
    npair = d // pw
    nk = t // SSM_CHUNK
    u2 = xn1.astype(BF16).reshape(b, nk, SSM_CHUNK, npair, pw).transpose(3, 1, 0, 2, 4)
    u2 = u2.reshape(npair, nk * b, SSM_CHUNK * pw)
    y2, hfin = _s5_chunk(u2, ops, nb=b)
    yc = y2.reshape(npair, nk, b, SSM_CHUNK, pw).transpose(2, 1, 3, 0, 4).reshape(rows_p, d)
    hp, xm = _s5_out(yc, xn1, d_skip, w_glu, b_glu, hp, g[1, 1], g[1, 2], tm=tm)
    (hp,) = _mlp(xm, hp, w1[1], w2[1], g[1, 3], g[1, 3], tm=tm, ff_chunk=ff_chunk, next_norm=False)
    ssm_p = hfin.reshape(npair, 2, b, SSM_UNIT, SSM_STATE).transpose(2, 1, 0, 3, 4)
    ssm_p = ssm_p.reshape(b, 2, d // SSM_GROUP, SSM_STATE)

    xs = x_sample.reshape(nb, d)
    cos_s, sin_s = _rope_tables(jnp.full((nb,), past, dtype=jnp.int32))
    q_s, qr_s, gates_s, kvc_s, kvs_s, kvw_s = _inproj(
        xs, g[0, 0], w_main, w_gate, cos_s, sin_s, tm=nb, pos_blocks=1, transposed=False)
    pt_flat = page_table.reshape(-1).astype(jnp.int32)
    pg = _tile(pages_per_sample, PAGE_GROUP)
    n_pool = cache_kv_cmp.shape[1]
    feature_major = lambda c, n, s: c.transpose(0, 2, 3, 4, 1).reshape(n, 2, KV_W, s)
    cmp_pages = feature_major(cache_kv_cmp[0], n_pool, PAGE_SIZE)
    sel_pages = feature_major(cache_kv_sel[0], n_pool, PAGE_SIZE)
    kcvc = _compress_paged(cmp_pages, pt_flat, wbd, pe_t, w2bd, nb=nb, pages_per_sample=pages_per_sample, pg=pg)
    l_all = past + 1
    nsb_s = l_all // CMP_STRIDE
    nc_s = nsb_s - CMP_BLOCK // CMP_STRIDE + 1
    nsel_s = -(-l_all // SEL_BLOCK)
    nsp = -(-nsel_s // LANES) * LANES
    ov_s = _overlap(nc_s, past // CMP_STRIDE, nsel_s, nsp)
    keys_per_step = pg * PAGE_SIZE
    key_blk = (jnp.arange(past, dtype=jnp.int32) // SEL_BLOCK).reshape(past // keys_per_step, 1, keys_per_step)
    e_mat = (jnp.arange(nsp, dtype=jnp.int32)[None, :, None] == key_blk).astype(BF16)
    hh = jnp.arange(N_HEADS)
    gsum = ((hh[:, None] // GROUP) == (hh[None, :] // GROUP)).astype(BF16)
    win = feature_major(cache_kv_win[0], nb, WINDOW)
    o_s = _attn_sample(sel_pages, pt_flat, q_s.reshape(nb, N_HEADS, HEAD_DIM), qr_s.reshape(nb, N_HEADS, HEAD_DIM),
                       gates_s[:, :3 * N_HEADS].reshape(nb, N_HEADS, 3), kcvc, kvs_s.reshape(nb, 1, 2 * KV_W),
                       kvw_s.reshape(nb, 1, 2 * KV_W), win, ov_s, e_mat, gsum, nb=nb,
                       pages_per_sample=pages_per_sample, pg=pg, past=past, nc=nc_s, ns_valid=nsel_s,
                       n_top=min(TOP_N, nsel_s))
    hs, xm_s = _oproj(o_s.reshape(nb, Q_W), w_o, xs, g[0, 1], g[0, 2], tm=nb, transposed=False)
    hs, xn1_s = _mlp(xm_s, hs, w1[0], w2[0], g[0, 3], g[1, 0], tm=nb, ff_chunk=ff_chunk, next_norm=True)

    u2_s = xn1_s.astype(BF16).reshape(nb, npair, pw).transpose(1, 0, 2)
    st = state_ssm[0].reshape(nb, 2, npair, SSM_UNIT * SSM_STATE).transpose(1, 2, 0, 3)
    y2_s, hr_s, hi_s = _s5_step(u2_s, st[0], st[1], ops)
    yc_s = y2_s.transpose(1, 0, 2).reshape(nb, d)
    hs, xm_s = _s5_out(yc_s, xn1_s, d_skip, w_glu, b_glu, hs, g[1, 1], g[1, 2], tm=nb)
    (hs,) = _mlp(xm_s, hs, w1[1], w2[1], g[1, 3], g[1, 3], tm=nb, ff_chunk=ff_chunk, next_norm=False)
    ssm_s = jnp.stack([hr_s, hi_s], axis=0).transpose(2, 0, 1, 3).reshape(nb, 2, d // SSM_GROUP, SSM_STATE)

    kv5 = lambda a, n, s: a.reshape(1, n, s, 2, N_KV_HEADS, HEAD_DIM)
    from_fm = lambda a, n, s: a.reshape(n, 2, N_KV_HEADS, HEAD_DIM, s).transpose(0, 4, 1, 2, 3)[None]
    win_s = jnp.concatenate([win[..., 1:], kvw_s.reshape(nb, 2, KV_W, 1)], axis=-1)
    return (hp.reshape(b, t, d), hs.reshape(nb, 1, d),
            from_fm(kvcT, b, t), kv5(kvc_s, nb, 1), from_fm(kvsT, b, t), kv5(kvs_s, nb, 1),
            from_fm(kvwT[:, :, t - WINDOW:], b, WINDOW), from_fm(win_s, nb, WINDOW), ssm_p[None], ssm_s[None])
```

```python
import functools

import jax
import jax.numpy as jnp
from jax import lax
from jax.experimental import pallas as pl
from jax.experimental.pallas import tpu as pltpu

N_HEADS = 16
HEAD_DIM = 64
N_KV_HEADS = 4
GROUP = N_HEADS // N_KV_HEADS
CMP_BLOCK = 32
CMP_STRIDE = 16
SEL_BLOCK = 64
TOP_N = 16
WINDOW = 512
ROPE_THETA = 10000.0
PAGE_SIZE = 128
SSM_GROUP = 16
SSM_STATE = 64
SSM_CHUNK = 8
SSM_UNIT = 8
EPS = 1e-6
NEG = -1e30
FORCE = 1e9
MASKED = -1.5e38
LOG2E = 1.4426950408889634
V_ROWS = HEAD_DIM + 16
Q_W = N_HEADS * HEAD_DIM
KV_W = N_KV_HEADS * HEAD_DIM
LANES = 128
SUBLANES = 8
PAGE_GROUP = 16
S5_ROW_BLOCK = 1024
S5_TIME_BLOCK = 2048
VMEM_LIMIT = 56 * 1024 * 1024

F32 = jnp.float32
BF16 = jnp.bfloat16


def _params(*sem):
    return pltpu.CompilerParams(dimension_semantics=sem, vmem_limit_bytes=VMEM_LIMIT)


def _full(shape):
    zeros = (0,) * len(shape)
    return pl.BlockSpec(shape, lambda *_: zeros)


def _rms(x, g):
    ms = jnp.mean(x * x, axis=-1, keepdims=True)
    return x * lax.rsqrt(ms + EPS) * g


def _dot(a, b):
    return jnp.dot(a, b, preferred_element_type=F32)


def _dot_f32lhs(w, x):
    hi = x.astype(BF16)
    r1 = x - hi.astype(F32)
    mid = r1.astype(BF16)
    lo = (r1 - mid.astype(F32)).astype(BF16)
    return _dot(w, hi) + _dot(w, mid) + _dot(w, lo)


def _dot_f32rhs(x, w):
    hi = x.astype(BF16)
    r1 = x - hi.astype(F32)
    mid = r1.astype(BF16)
    lo = (r1 - mid.astype(F32)).astype(BF16)
    return _dot(hi, w) + _dot(mid, w) + _dot(lo, w)


def _rope_nat(x, cos, sin):
    half = HEAD_DIM // 2
    lane = lax.broadcasted_iota(jnp.int32, (1, LANES), 1)
    first = (lane % HEAD_DIM) < half
    outs = []
    for c in range(x.shape[1] // LANES):
        xc = x[:, c * LANES:(c + 1) * LANES]
        rot = jnp.where(first, pltpu.roll(xc, LANES - half, 1), pltpu.roll(xc, half, 1))
        outs.append(xc * cos + rot * sin)
    return jnp.concatenate(outs, axis=1)


def _inproj_kernel(x_ref, g_ref, w_ref, wg_ref, cos_ref, sin_ref, *outs, transposed, key_chunk):
    xb = _rms(x_ref[...], g_ref[...]).astype(BF16)
    cos = cos_ref[...]
    sin = sin_ref[...]
    scale = HEAD_DIM ** -0.5
    q = _dot(xb, w_ref[:, 0:Q_W])
    qr = _rope_nat(q, cos, sin)
    kv = _dot(xb, w_ref[:, Q_W:Q_W + 6 * KV_W])
    gates = jax.nn.sigmoid(_dot(xb, wg_ref[...]))
    k_s = _rope_nat(kv[:, 2 * KV_W:3 * KV_W], cos, sin)
    v_s = kv[:, 3 * KV_W:4 * KV_W]
    k_w = _rope_nat(kv[:, 4 * KV_W:5 * KV_W], cos, sin)
    v_w = kv[:, 5 * KV_W:6 * KV_W]
    kvc_ref = outs[3]
    kvc_ref[...] = kv[:, 0:2 * KV_W]
    if transposed:
        qT_ref, qrT_ref, gT_ref, _, kvcT_ref, kvsT_ref, kvwT_ref, ksb_ref, kwb_ref, vsT_ref, vwT_ref = outs
        qT_ref[...] = (q * (scale * LOG2E)).T.astype(BF16)
        qrT_ref[...] = (qr * (scale * LOG2E)).T.astype(BF16)
        gT_ref[...] = gates.T
        tm = x_ref.shape[0]
        rowi = lax.broadcasted_iota(jnp.int32, (tm, LANES), 0)
        lanei = lax.broadcasted_iota(jnp.int32, (tm, LANES), 1)
        blk = lax.shift_right_logical(rowi & (key_chunk - 1), SEL_BLOCK.bit_length() - 1)
        extra = lanei - HEAD_DIM
        aug = jnp.where((extra == blk) | (extra == key_chunk // SEL_BLOCK), 1.0, 0.0)
        for c in range(KV_W // LANES):
            for k_nat, k_ref in ((k_s, ksb_ref), (k_w, kwb_ref)):
                pair = k_nat[:, c * LANES:(c + 1) * LANES]
                k_ref[2 * c] = jnp.where(lanei < HEAD_DIM, pair, aug).astype(BF16)
                k_ref[2 * c + 1] = jnp.where(lanei < HEAD_DIM, pltpu.roll(pair, HEAD_DIM, 1), aug).astype(BF16)
        v_sT = v_s.T
        v_wT = v_w.T
        ones_rows = jnp.where(lax.broadcasted_iota(jnp.int32, (V_ROWS - HEAD_DIM, tm), 0) == 0, 1.0, 0.0)
        for h in range(N_KV_HEADS):
            for vT, v_ref in ((v_sT, vsT_ref), (v_wT, vwT_ref)):
                v_ref[h * V_ROWS:(h + 1) * V_ROWS, :] = jnp.concatenate(
                    [vT[h * HEAD_DIM:(h + 1) * HEAD_DIM, :], ones_rows], axis=0).astype(BF16)
        kvcT_ref[0] = kv[:, 0:2 * KV_W].T
        kvsT_ref[0, 0:KV_W, :] = k_s.T
        kvsT_ref[0, KV_W:2 * KV_W, :] = v_sT
        kvwT_ref[0, 0:KV_W, :] = k_w.T
        kvwT_ref[0, KV_W:2 * KV_W, :] = v_wT
    else:
        q_ref, qr_ref, gt_ref, _, kvs_ref, kvw_ref = outs
        q_ref[...] = (q * scale).astype(BF16)
        qr_ref[...] = (qr * scale).astype(BF16)
        gt_ref[...] = gates
        kvs_ref[:, 0:KV_W] = k_s
        kvs_ref[:, KV_W:2 * KV_W] = v_s
        kvw_ref[:, 0:KV_W] = k_w
        kvw_ref[:, KV_W:2 * KV_W] = v_w


def _inproj(x, g, w_main, w_gate, cos_t, sin_t, *, tm, pos_blocks, transposed, key_chunk=LANES):
    rows, d = x.shape
    assert tm % key_chunk == 0 or not transposed
    n = rows // tm
    row_blk = lambda w: pl.BlockSpec((tm, w), lambda i: (i, 0))
    col_blk = lambda h: pl.BlockSpec((h, tm), lambda i: (0, i))
    tab = pl.BlockSpec((tm, LANES), lambda i: (i % pos_blocks, 0))
    kv_nat = jax.ShapeDtypeStruct((rows, 2 * KV_W), F32)
    if transposed:
        seqs = rows // (pos_blocks * tm)
        kh = pl.BlockSpec((N_KV_HEADS, tm, LANES), lambda i: (0, i, 0))
        kvT = pl.BlockSpec((1, 2 * KV_W, tm), lambda i: (i // pos_blocks, 0, i % pos_blocks))
        out_shape = ([jax.ShapeDtypeStruct((Q_W, rows), BF16)] * 2 + [jax.ShapeDtypeStruct((LANES, rows), F32)]
                     + [kv_nat] + [jax.ShapeDtypeStruct((seqs, 2 * KV_W, pos_blocks * tm), F32)] * 3
                     + [jax.ShapeDtypeStruct((N_KV_HEADS, rows, LANES), BF16)] * 2
                     + [jax.ShapeDtypeStruct((N_KV_HEADS * V_ROWS, rows), BF16)] * 2)
        out_specs = ([col_blk(Q_W)] * 2 + [col_blk(LANES)] + [row_blk(2 * KV_W)] + [kvT] * 3 + [kh] * 2
                     + [col_blk(N_KV_HEADS * V_ROWS)] * 2)
    else:
        out_shape = ([jax.ShapeDtypeStruct((rows, Q_W), BF16)] * 2 + [jax.ShapeDtypeStruct((rows, LANES), F32)]
                     + [kv_nat] * 3)
        out_specs = [row_blk(Q_W)] * 2 + [row_blk(LANES)] + [row_blk(2 * KV_W)] * 3
    return pl.pallas_call(
        functools.partial(_inproj_kernel, transposed=transposed, key_chunk=key_chunk),
        grid=(n,),
        in_specs=[row_blk(d), _full((1, d)), _full(w_main.shape), _full(w_gate.shape), tab, tab],
        out_specs=out_specs,
        out_shape=out_shape,
        compiler_params=_params("parallel"),
        name="nsa_inproj",
    )(x, g, w_main, w_gate, cos_t, sin_t)


_KV_CHUNKS = 2 * KV_W // LANES


def _compress_half(load_rows, kv, nrows, wbd_ref, pe_ref):
    ratio = CMP_BLOCK // CMP_STRIDE
    accs = [jnp.broadcast_to(pe_ref[kv * ratio + r:kv * ratio + r + 1, :], (nrows, KV_W)) for r in range(ratio)]
    for s in range(CMP_STRIDE):
        lhs = load_rows(s, kv).astype(BF16)
        for r in range(ratio):
            accs[r] = accs[r] + _dot(lhs, wbd_ref[(kv * ratio + r) * CMP_STRIDE + s])
    return accs


def _compress_prompt_kernel(*refs, nsb):
    x_refs = refs[:_KV_CHUNKS]
    wbd_ref, pe_ref, w2_ref, kc_ref, vcT_ref, sh_ref = refs[_KV_CHUNKS:]

    def load_rows(s, kv):
        per_half = _KV_CHUNKS // 2
        return jnp.concatenate([x_refs[kv * per_half + c][0, pl.ds(s, nsb, stride=CMP_STRIDE), :]
                                for c in range(per_half)], axis=1)

    sh_ref[nsb:nsb + SUBLANES, :] = jnp.zeros((SUBLANES, KV_W), F32)
    for kv in range(2):
        pr0, pr1 = _compress_half(load_rows, kv, nsb, wbd_ref, pe_ref)
        sh_ref[0:nsb, :] = pr1
        h = pr0 + sh_ref[pl.ds(1, nsb), :]
        out = _dot(jax.nn.gelu(h).astype(BF16), w2_ref[kv])
        if kv == 0:
            for hh in range(N_KV_HEADS):
                kc_ref[0, hh] = out[:, hh * HEAD_DIM:(hh + 1) * HEAD_DIM].astype(BF16)
        else:
            vcT_ref[0] = out.T.astype(BF16)


def _compress_prompt(kvc3, wbd, pe_t, w2bd):
    b, t, _ = kvc3.shape
    nsb = t // CMP_STRIDE
    return pl.pallas_call(
        functools.partial(_compress_prompt_kernel, nsb=nsb),
        grid=(b,),
        in_specs=[pl.BlockSpec((1, t, LANES), lambda i, c=c: (i, 0, c)) for c in range(_KV_CHUNKS)]
        + [_full(wbd.shape), _full(pe_t.shape), _full(w2bd.shape)],
        out_specs=[pl.BlockSpec((1, N_KV_HEADS, nsb, HEAD_DIM), lambda i: (i, 0, 0, 0)),
                   pl.BlockSpec((1, KV_W, nsb), lambda i: (i, 0, 0))],
        out_shape=[jax.ShapeDtypeStruct((b, N_KV_HEADS, nsb, HEAD_DIM), BF16),
                   jax.ShapeDtypeStruct((b, KV_W, nsb), BF16)],
        scratch_shapes=[pltpu.VMEM((nsb + SUBLANES, KV_W), F32)],
        compiler_params=_params("parallel"),
        name="nsa_compress_prompt",
    )(*([kvc3] * _KV_CHUNKS), wbd, pe_t, w2bd)


def _topk_mask_T(imp, n_top):
    ns, w = imp.shape
    nblk = ns // SUBLANES
    blocks = [imp[r * SUBLANES:(r + 1) * SUBLANES, :] for r in range(nblk)]
    cnts = [jnp.zeros((SUBLANES, w), F32) for _ in range(nblk)]
    sub = lax.broadcasted_iota(jnp.int32, (SUBLANES, w), 0)
    for sp in range(ns):
        row = blocks[sp // SUBLANES][sp % SUBLANES:sp % SUBLANES + 1, :]
        for r in range(nblk):
            blk = blocks[r]
            if sp < r * SUBLANES:
                beats = jnp.where(row >= blk, 1.0, 0.0)
            elif sp >= (r + 1) * SUBLANES:
                beats = jnp.where(row > blk, 1.0, 0.0)
            else:
                beats = jnp.where(sub > (sp - r * SUBLANES), jnp.where(row >= blk, 1.0, 0.0),
                                  jnp.where(row > blk, 1.0, 0.0))
            cnts[r] = cnts[r] + beats
    return jnp.concatenate([jnp.where(c < n_top, 1.0, 0.0) for c in cnts], axis=0)


def _online_chunks(states, k_cs, vT_cs, qTs, bias):
    scores = [_dot(k_c, qT) for k_c, qT in zip(k_cs, qTs)]
    mids = []
    for (m, _), s in zip(states, scores):
        if bias is not None:
            s = s + bias
        m_new = jnp.maximum(m, jnp.max(s, axis=0, keepdims=True))
        mids.append((m_new, jnp.exp2(m - m_new), jnp.exp2(s - m_new).astype(BF16)))
    return tuple((m_new, alpha * acc + _dot(vT_c, p))
                 for (m_new, alpha, p), (_, acc), vT_c in zip(mids, states, vT_cs))


def _softmax_init(w):
    return (jnp.full((1, w), NEG, F32), jnp.zeros((V_ROWS, w), F32))


def _softmax_finish(carry):
    _, acc = carry
    return acc[0:HEAD_DIM, :] * (1.0 / jnp.maximum(acc[HEAD_DIM:HEAD_DIM + 1, :], 1e-20))


def _attn_prompt_kernel(qT_ref, qrT_ref, gT_ref, kc_ref, vcT_ref, ks_ref, vsT_ref, kw_ref, vwT_ref, ovT_ref,
                        o_ref, selb_ref, oc_ref, *, tq, nc, n_top):
    ck = tq
    i = pl.program_id(1)
    t0 = i * tq
    qpos = t0 + lax.broadcasted_iota(jnp.int32, (1, tq), 1)
    ncp = kc_ref.shape[2]
    ns = ovT_ref.shape[0]
    w = GROUP * tq
    bpc = ck // SEL_BLOCK
    sel_shift = SEL_BLOCK.bit_length() - 1
    kvhs = range(N_KV_HEADS)
    heads = [[kvh * GROUP + g for g in range(GROUP)] for kvh in kvhs]
    rows = [pl.ds(kvh * HEAD_DIM, HEAD_DIM) for kvh in kvhs]
    vrows = [pl.ds(kvh * V_ROWS, V_ROWS) for kvh in kvhs]
    qrT = [jnp.concatenate([qrT_ref[h * HEAD_DIM:(h + 1) * HEAD_DIM, :] for h in heads[kvh]], axis=1)
           for kvh in kvhs]
    kl = lax.broadcasted_iota(jnp.int32, (ck, tq), 0)
    ql = lax.broadcasted_iota(jnp.int32, (ck, tq), 1)
    tile4 = lambda b: jnp.concatenate([b] * GROUP, axis=1)
    key_le_query = tile4(jnp.where(kl <= ql, 0.0, MASKED))
    key_ge_query = tile4(jnp.where(kl >= ql, 0.0, MASKED))
    bias_rows = selb_ref.shape[2]
    zero_rows = jnp.zeros((LANES - HEAD_DIM - bias_rows, w), BF16)

    def with_bias_rows(kvh, tile):
        return jnp.concatenate([qrT[kvh], tile, zero_rows], axis=0)

    cidx = lax.broadcasted_iota(jnp.int32, (ncp, 1), 0)
    valid = ((cidx * CMP_STRIDE + (CMP_BLOCK - 1)) <= qpos) & (cidx < nc)
    sidx = lax.broadcasted_iota(jnp.int32, (ns, 1), 0)
    cur = lax.shift_right_logical(qpos, sel_shift)
    forced = (sidx == 0) | (sidx == cur) | (sidx == cur - 1)
    causal = (sidx * SEL_BLOCK) <= qpos
    cmp_scores = [
        _dot(kc_ref[0, kvh], jnp.concatenate([qT_ref[h * HEAD_DIM:(h + 1) * HEAD_DIM, :] for h in heads[kvh]], axis=1))
        for kvh in kvhs]
    for kvh in kvhs:
        s = cmp_scores[kvh]
        probs = []
        for g in range(GROUP):
            sm = jnp.where(valid, s[:, g * tq:(g + 1) * tq], NEG)
            mx = jnp.max(sm, axis=0, keepdims=True)
            e = jnp.where(valid, jnp.exp2(sm - mx), 0.0)
            den = jnp.maximum(jnp.sum(e, axis=0, keepdims=True), 1e-20)
            probs.append(e / den)
        oc_ref[kvh] = _dot(vcT_ref[0, rows[kvh], :], jnp.concatenate(probs, axis=1).astype(BF16))
        psum = probs[0]
        for g in range(1, GROUP):
            psum = psum + probs[g]
        imp = _dot_f32lhs(ovT_ref[...], psum)
        imp = jnp.where(forced, FORCE, jnp.where(causal, imp, NEG))
        selb = (_topk_mask_T(imp, n_top) - 1.0) * (-MASKED)
        fill = jnp.zeros((bias_rows - bpc, tq), F32)
        for c in range(ns // bpc):
            selb_ref[kvh, c] = jnp.concatenate([selb[c * bpc:(c + 1) * bpc, :], fill], axis=0).astype(BF16)

    n_back = WINDOW // ck
    states = tuple(_softmax_init(w) for _ in kvhs)
    rowb = lax.broadcasted_iota(jnp.int32, (bias_rows, w), 0)
    for r in range(n_back + 1):
        a = i - n_back + r
        kst = pl.multiple_of(jnp.maximum(a, 0) * ck, ck)
        skip = jnp.where(a < 0, MASKED, 0.0)
        tile = jnp.where(rowb == bpc, skip, 0.0).astype(BF16)
        bias = key_le_query if r == n_back else (key_ge_query if r == 0 else None)
        states = _online_chunks(states, [kw_ref[kvh, pl.ds(kst, ck), :] for kvh in kvhs],
                                [vwT_ref[vrows[kvh], pl.ds(kst, ck)] for kvh in kvhs],
                                [with_bias_rows(kvh, tile) for kvh in kvhs], bias)
    ow = [_softmax_finish(st) for st in states]

    def chunk_step(c, states, diagonal):
        kst = pl.multiple_of(c * ck, ck)
        return _online_chunks(states, [ks_ref[kvh, pl.ds(kst, ck), :] for kvh in kvhs],
                              [vsT_ref[vrows[kvh], pl.ds(kst, ck)] for kvh in kvhs],
                              [with_bias_rows(kvh, tile4(selb_ref[kvh, c])) for kvh in kvhs],
                              key_le_query if diagonal else None)

    states = lax.fori_loop(0, i, lambda c, st: chunk_step(c, st, False), tuple(_softmax_init(w) for _ in kvhs))
    states = chunk_step(i, states, True)

    for kvh in kvhs:
        def gate_row(j, kvh=kvh):
            return jnp.concatenate([gT_ref[h * 3 + j:h * 3 + j + 1, :] for h in heads[kvh]], axis=1)

        oT = gate_row(0) * oc_ref[kvh] + gate_row(1) * _softmax_finish(states[kvh]) + gate_row(2) * ow[kvh]
        for g, h in enumerate(heads[kvh]):
            o_ref[h * HEAD_DIM:(h + 1) * HEAD_DIM, :] = oT[:, g * tq:(g + 1) * tq].astype(BF16)


def _attn_prompt(qT, qrT, gT, kc, vcT, ksb, vsT, kwb, vwT, ovT, *, batch, seq, tq, nc, n_top):
    nq = seq // tq
    nsb = kc.shape[2]
    ns = ovT.shape[0]
    col = lambda h: pl.BlockSpec((h, tq), lambda b, i: (0, b * nq + i))
    kh = pl.BlockSpec((N_KV_HEADS, seq, LANES), lambda b, i: (0, b, 0))
    vt = pl.BlockSpec((N_KV_HEADS * V_ROWS, seq), lambda b, i: (0, b))
    bf16_sublanes = 2 * SUBLANES
    return pl.pallas_call(
        functools.partial(_attn_prompt_kernel, tq=tq, nc=nc, n_top=n_top),
        grid=(batch, nq),
        in_specs=[col(Q_W), col(Q_W), col(LANES),
                  pl.BlockSpec((1, N_KV_HEADS, nsb, HEAD_DIM), lambda b, i: (b, 0, 0, 0)),
                  pl.BlockSpec((1, KV_W, nsb), lambda b, i: (b, 0, 0)),
                  kh, vt, kh, vt, _full(ovT.shape)],
        out_specs=col(Q_W),
        out_shape=jax.ShapeDtypeStruct((Q_W, batch * seq), BF16),
        scratch_shapes=[pltpu.VMEM((N_KV_HEADS, ns * SEL_BLOCK // tq, bf16_sublanes, tq), BF16),
                        pltpu.VMEM((N_KV_HEADS, HEAD_DIM, GROUP * tq), F32)],
        compiler_params=_params("parallel", "arbitrary"),
        name="nsa_attn_prompt",
    )(qT, qrT, gT, kc, vcT, ksb, vsT, kwb, vwT, ovT)


def _oproj_kernel(o_ref, w_ref, x_ref, g1_ref, g2_ref, h_ref, xm_ref, *, transposed):
    if transposed:
        y = lax.dot_general(o_ref[...], w_ref[...], (((0,), (0,)), ((), ())), preferred_element_type=F32)
    else:
        y = _dot(o_ref[...], w_ref[...])
    h = x_ref[...] + _rms(y, g1_ref[...])
    h_ref[...] = h
    xm_ref[...] = _rms(h, g2_ref[...]).astype(BF16)


def _oproj(o, w_o, x, g1, g2, *, tm, transposed):
    rows, d = x.shape
    o_spec = (pl.BlockSpec((Q_W, tm), lambda i: (0, i)) if transposed else pl.BlockSpec((tm, Q_W), lambda i: (i, 0)))
    row = pl.BlockSpec((tm, d), lambda i: (i, 0))
    return pl.pallas_call(
        functools.partial(_oproj_kernel, transposed=transposed),
        grid=(rows // tm,),
        in_specs=[o_spec, _full(w_o.shape), row, _full((1, d)), _full((1, d))],
        out_specs=[row, row],
        out_shape=[jax.ShapeDtypeStruct((rows, d), F32), jax.ShapeDtypeStruct((rows, d), BF16)],
        compiler_params=_params("parallel"),
        name="nsa_oproj",
    )(o, w_o, x, g1, g2)


def _mlp_kernel(xm_ref, h_ref, w1_ref, w2_ref, g3_ref, gn_ref, *rest, ff_chunk, next_norm):
    if next_norm:
        h2_ref, xn_ref, acc_ref = rest
    else:
        h2_ref, acc_ref = rest
    xm = xm_ref[...]
    for c in range(w1_ref.shape[1] // ff_chunk):
        cols = slice(c * ff_chunk, (c + 1) * ff_chunk)
        hm = jnp.maximum(_dot(xm, w1_ref[:, cols]), 0.0)
        part = _dot((hm * hm).astype(BF16), w2_ref[cols, :])
        if c == 0:
            acc_ref[...] = part
        else:
            acc_ref[...] += part
    h2 = h_ref[...] + _rms(acc_ref[...], g3_ref[...])
    h2_ref[...] = h2
    if next_norm:
        xn_ref[...] = _rms(h2, gn_ref[...])


def _mlp(xm, h, w1, w2, g3, gn, *, tm, ff_chunk, next_norm):
    rows, d = h.shape
    row = pl.BlockSpec((tm, d), lambda i: (i, 0))
    out_shape = [jax.ShapeDtypeStruct((rows, d), F32)] * (2 if next_norm else 1)
    return pl.pallas_call(
        functools.partial(_mlp_kernel, ff_chunk=ff_chunk, next_norm=next_norm),
        grid=(rows // tm,),
        in_specs=[row, row, _full(w1.shape), _full(w2.shape), _full((1, d)), _full((1, d))],
        out_specs=[row] * len(out_shape),
        out_shape=out_shape,
        scratch_shapes=[pltpu.VMEM((tm, d), F32)],
        compiler_params=_params("parallel"),
        name="sq_relu_mlp",
    )(xm, h, w1, w2, g3, gn)


def _s5_chunk_kernel(u_ref, brow_ref, pre_ref, pim_ref, qre_ref, qim_ref, are_ref, aim_ref, y_ref, hfin_ref,
                     sre, sim, hre, him, cre, cim, *, nb):
    steps_per_iter = SUBLANES // nb
    rows = sre.shape[0]
    uw = SSM_UNIT * SSM_GROUP
    nt = u_ref.shape[2] // uw

    @pl.when(pl.program_id(1) == 0)
    def _():
        cre[...] = jnp.zeros(cre.shape, F32)
        cim[...] = jnp.zeros(cim.shape, F32)

    u = u_ref[0]
    sre[...] = _dot(u, pre_ref[0])
    sim[...] = _dot(u, pim_ref[0])
    ar = are_ref[0]
    ai = aim_ref[0]

    def body(it, carry):
        hr, hi = carry
        r0 = pl.multiple_of(it * SUBLANES, SUBLANES)
        sr8 = sre[pl.ds(r0, SUBLANES), :]
        si8 = sim[pl.ds(r0, SUBLANES), :]
        prev_r, prev_i = [], []
        for j in range(steps_per_iter):
            prev_r.append(hr)
            prev_i.append(hi)
            sr = sr8[j * nb:(j + 1) * nb, :]
            si = si8[j * nb:(j + 1) * nb, :]
            hr, hi = ar * hr - ai * hi + sr, ar * hi + ai * hr + si
        hre[pl.ds(r0, SUBLANES), :] = jnp.concatenate(prev_r, axis=0)
        him[pl.ds(r0, SUBLANES), :] = jnp.concatenate(prev_i, axis=0)
        return hr, hi

    hr, hi = lax.fori_loop(0, rows // SUBLANES, body, (cre[...], cim[...]))
    cre[...] = hr
    cim[...] = hi
    hfin_ref[0, 0] = hr
    hfin_ref[0, 1] = hi

    hb_re = hre[...].astype(BF16)
    hb_im = him[...].astype(BF16)
    for t2 in range(0, nt, 2):
        cols = slice(t2 * uw, (t2 + 2) * uw)
        acc = _dot(hb_re, qre_ref[0, :, cols]) + _dot(hb_im, qim_ref[0, :, cols])
        for t1 in range(t2 + 2):
            lag0 = t2 - t1 + 1
            acc = acc + _dot(u[:, t1 * uw:(t1 + 1) * uw], brow_ref[0, :, lag0 * uw:(lag0 + 2) * uw])
        y_ref[0, :, cols] = acc


def _s5_chunk(u2, ops, *, nb):
    nunit, nch, width = u2.shape
    sw = ops["p_re"].shape[2]
    rows = _tile(nch, S5_ROW_BLOCK)
    per_unit = lambda a: pl.BlockSpec((1,) + a.shape[1:], lambda i, r: (i,) + (0,) * (a.ndim - 1))
    row_blk = pl.BlockSpec((1, rows, width), lambda i, r: (i, r, 0))
    args = (u2, ops["brow"], ops["p_re"], ops["p_im"], ops["q_re"], ops["q_im"], ops["a_chunk_re"], ops["a_chunk_im"])
    return pl.pallas_call(
        functools.partial(_s5_chunk_kernel, nb=nb),
        grid=(nunit, nch // rows),
        in_specs=[row_blk] + [per_unit(a) for a in args[1:]],
        out_specs=[row_blk, pl.BlockSpec((1, 2, nb, sw), lambda i, r: (i, 0, 0, 0))],
        out_shape=[jax.ShapeDtypeStruct((nunit, nch, width), F32), jax.ShapeDtypeStruct((nunit, 2, nb, sw), F32)],
        scratch_shapes=[pltpu.VMEM((rows, sw), F32)] * 4 + [pltpu.VMEM((nb, sw), F32)] * 2,
        compiler_params=_params("parallel", "arbitrary"),
        name="s5_chunk_scan",
    )(*args)


def _s5_seq_kernel(x_ref, brow_ref, pre_ref, pim_ref, qre_ref, qim_ref, are_ref, aim_ref, y_ref, hfin_ref,
                   sre, sim, hre, him, cre, cim):
    nb, tb, uw = x_ref.shape
    nt = pre_ref.shape[1] // uw
    nk = tb // nt

    @pl.when(pl.program_id(1) == 0)
    def _():
        cre[...] = jnp.zeros(cre.shape, F32)
        cim[...] = jnp.zeros(cim.shape, F32)

    u = [jnp.concatenate([x_ref[b, pl.ds(t, nk, stride=nt), :] for b in range(nb)], axis=0).astype(BF16)
         for t in range(nt)]
    u2 = [jnp.concatenate([u[2 * j], u[2 * j + 1]], axis=1) for j in range(nt // 2)]
    s_re = _dot(u2[0], pre_ref[0, 0:2 * uw, :])
    s_im = _dot(u2[0], pim_ref[0, 0:2 * uw, :])
    for j in range(1, nt // 2):
        s_re = s_re + _dot(u2[j], pre_ref[0, 2 * j * uw:(2 * j + 2) * uw, :])
        s_im = s_im + _dot(u2[j], pim_ref[0, 2 * j * uw:(2 * j + 2) * uw, :])
    sre[...] = s_re
    sim[...] = s_im
    ar = are_ref[0]
    ai = aim_ref[0]

    def body(it, carry):
        out = []
        for b in range(nb):
            hr, hi = carry[b]
            r0 = pl.multiple_of(b * nk + it * SUBLANES, SUBLANES)
            sr8 = sre[pl.ds(r0, SUBLANES), :]
            si8 = sim[pl.ds(r0, SUBLANES), :]
            prev_r, prev_i = [], []
            for j in range(SUBLANES):
                prev_r.append(hr)
                prev_i.append(hi)
                hr, hi = (ar * hr - ai * hi + sr8[j:j + 1, :], ar * hi + ai * hr + si8[j:j + 1, :])
            hre[pl.ds(r0, SUBLANES), :] = jnp.concatenate(prev_r, axis=0)
            him[pl.ds(r0, SUBLANES), :] = jnp.concatenate(prev_i, axis=0)
            out.append((hr, hi))
        return tuple(out)

    init = tuple((cre[b:b + 1, :], cim[b:b + 1, :]) for b in range(nb))
    fin = lax.fori_loop(0, nk // SUBLANES, body, init)
    for b in range(nb):
        cre[b:b + 1, :] = fin[b][0]
        cim[b:b + 1, :] = fin[b][1]
    hfin_ref[0, 0] = cre[...]
    hfin_ref[0, 1] = cim[...]

    hb_re = hre[...].astype(BF16)
    hb_im = him[...].astype(BF16)
    for t2 in range(0, nt, 2):
        cols = slice(t2 * uw, (t2 + 2) * uw)
        acc = _dot(hb_re, qre_ref[0, :, cols]) + _dot(hb_im, qim_ref[0, :, cols])
        for j in range(t2 // 2 + 1):
            lag0 = t2 - 2 * j + 1
            wpair = jnp.concatenate([brow_ref[0, :, lag0 * uw:(lag0 + 2) * uw],
                                     brow_ref[0, :, (lag0 - 1) * uw:(lag0 + 1) * uw]], axis=0)
            acc = acc + _dot(u2[j], wpair)
        for b in range(nb):
            for dt in range(2):
                y_ref[b, pl.ds(t2 + dt, nk, stride=nt), :] = acc[b * nk:(b + 1) * nk, dt * uw:(dt + 1) * uw]


def _s5_seq(x3, ops):
    nb, t, d = x3.shape
    uw = SSM_UNIT * SSM_GROUP
    nunit = d // uw
    sw = ops["p_re"].shape[2]
    tb = _tile(t, S5_TIME_BLOCK)
    nk = tb // SSM_CHUNK
    per_unit = lambda a: pl.BlockSpec((1,) + a.shape[1:], lambda i, r: (i,) + (0,) * (a.ndim - 1))
    blk = pl.BlockSpec((nb, tb, uw), lambda i, r: (0, r, i))
    args = (x3, ops["brow"], ops["p_re"], ops["p_im"], ops["q_re"], ops["q_im"], ops["a_chunk_re"], ops["a_chunk_im"])
    return pl.pallas_call(
        _s5_seq_kernel,
        grid=(nunit, t // tb),
        in_specs=[blk] + [per_unit(a) for a in args[1:]],
        out_specs=[blk, pl.BlockSpec((1, 2, nb, sw), lambda i, r: (i, 0, 0, 0))],
        out_shape=[jax.ShapeDtypeStruct((nb, t, d), F32), jax.ShapeDtypeStruct((nunit, 2, nb, sw), F32)],
        scratch_shapes=[pltpu.VMEM((nb * nk, sw), F32)] * 4 + [pltpu.VMEM((nb, sw), F32)] * 2,
        compiler_params=_params("parallel", "arbitrary"),
        name="s5_seq_scan",
    )(*args)


def _s5_step_kernel(u_ref, h0r_ref, h0i_ref, bre_ref, bim_ref, cre_ref, cim_ref, are_ref, aim_ref,
                    y_ref, hr_ref, hi_ref, *, npair):
    for p in range(npair):
        u = u_ref[p]
        ar = are_ref[p]
        ai = aim_ref[p]
        h0r = h0r_ref[p]
        h0i = h0i_ref[p]
        hr = ar * h0r - ai * h0i + _dot(u, bre_ref[p])
        hi = ar * h0i + ai * h0r + _dot(u, bim_ref[p])
        hr_ref[p] = hr
        hi_ref[p] = hi
        y_ref[p] = _dot(hr.astype(BF16), cre_ref[p]) + _dot(hi.astype(BF16), cim_ref[p])


def _s5_step(u2, h0r, h0i, ops):
    npair, rows, width = u2.shape
    args = (u2, h0r, h0i, ops["b1_re"], ops["b1_im"], ops["c1_re"], ops["c1_im"], ops["a1_re"], ops["a1_im"])
    return pl.pallas_call(
        functools.partial(_s5_step_kernel, npair=npair),
        grid=(1,),
        in_specs=[_full(a.shape) for a in args],
        out_specs=[_full((npair, rows, width)), _full(h0r.shape), _full(h0r.shape)],
        out_shape=[jax.ShapeDtypeStruct((npair, rows, width), F32), jax.ShapeDtypeStruct(h0r.shape, F32),
                   jax.ShapeDtypeStruct(h0r.shape, F32)],
        compiler_params=_params("arbitrary"),
        name="s5_single_step",
    )(*args)


def _s5_out_kernel(yc_ref, u_ref, d_ref, wg_ref, bg_ref, h_ref, g1_ref, g2_ref, h3_ref, xm_ref):
    y = jax.nn.gelu(yc_ref[...] + d_ref[...] * u_ref[...])
    out = y * jax.nn.sigmoid(_dot(y.astype(BF16), wg_ref[...]) + bg_ref[...])
    h3 = h_ref[...] + _rms(out, g1_ref[...])
    h3_ref[...] = h3
    xm_ref[...] = _rms(h3, g2_ref[...]).astype(BF16)


def _s5_out(yc, u, d_skip, w_glu, b_glu, h, g1, g2, *, tm):
    rows, d = h.shape
    row = pl.BlockSpec((tm, d), lambda i: (i, 0))
    vec = _full((1, d))
    return pl.pallas_call(
        _s5_out_kernel,
        grid=(rows // tm,),
        in_specs=[row, row, vec, _full(w_glu.shape), vec, row, vec, vec],
        out_specs=[row, row],
        out_shape=[jax.ShapeDtypeStruct((rows, d), F32), jax.ShapeDtypeStruct((rows, d), BF16)],
        compiler_params=_params("parallel"),
        name="s5_glu_out",
    )(yc, u, d_skip, w_glu, b_glu, h, g1, g2)


def _s5_operators(a_re, a_im, log_dt, b_re, b_im, c_re, c_im):
    hp = lax.Precision.HIGHEST
    g, n = a_re.shape
    gu = SSM_UNIT
    nunit = g // gu
    L = SSM_CHUNK
    uw = gu * SSM_GROUP
    sw = gu * n
    a = lax.complex(a_re.astype(F32), a_im.astype(F32))
    dt = jnp.exp(log_dt.astype(F32))[:, None]
    a_bar = jnp.exp(a * dt)
    b_bar = ((a_bar - 1.0) / a)[:, :, None] * lax.complex(b_re.astype(F32), b_im.astype(F32))
    c = lax.complex(c_re.astype(F32), c_im.astype(F32))
    pows = [jnp.ones_like(a_bar)]
    for _ in range(L):
        pows.append(pows[-1] * a_bar)
    a_pow = jnp.stack(pows).reshape(L + 1, nunit, sw)
    apr = jnp.real(a_pow)
    api = jnp.imag(a_pow)
    eye = jnp.eye(gu, dtype=F32)

    def bd_in(x):
        return jnp.einsum("pgnd,gh->pgdhn", x.reshape(nunit, gu, n, SSM_GROUP), eye).reshape(nunit, uw, sw)

    def bd_out(x):
        return jnp.einsum("pgcn,gh->pgnhc", x.reshape(nunit, gu, SSM_GROUP, n), eye).reshape(nunit, sw, uw)

    p0r, p0i = bd_in(jnp.real(b_bar)), bd_in(jnp.imag(b_bar))
    q0r, q0i = bd_out(jnp.real(c)), bd_out(jnp.imag(c))
    sr = apr[::-1][1:][:, :, None, :]
    si = api[::-1][1:][:, :, None, :]
    stack_rows = lambda x: x.transpose(1, 0, 2, 3).reshape(nunit, L * uw, sw)
    p_re = stack_rows(p0r[None] * sr - p0i[None] * si)
    p_im = stack_rows(p0r[None] * si + p0i[None] * sr)
    tr = apr[1:][:, :, :, None]
    ti = api[1:][:, :, :, None]
    stack_cols = lambda x: x.transpose(1, 2, 0, 3).reshape(nunit, sw, L * uw)
    q_re = stack_cols(q0r[None] * tr - q0i[None] * ti)
    q_im = stack_cols(-(q0r[None] * ti + q0i[None] * tr))
    xr = p0r[None] * apr[:L][:, :, None, :] - p0i[None] * api[:L][:, :, None, :]
    xi = p0r[None] * api[:L][:, :, None, :] + p0i[None] * apr[:L][:, :, None, :]
    b_lag = (jnp.einsum("tpus,psv->ptuv", xr, q0r, precision=hp)
             - jnp.einsum("tpus,psv->ptuv", xi, q0i, precision=hp))
    brow = jnp.concatenate([jnp.zeros((nunit, 1, uw, uw), F32), b_lag], axis=1)
    brow = brow.transpose(0, 2, 1, 3).reshape(nunit, uw, (L + 1) * uw)
    lane_row = lambda x: x.reshape(nunit, 1, sw)
    return {
        "brow": brow.astype(BF16),
        "p_re": p_re.astype(BF16), "p_im": p_im.astype(BF16),
        "q_re": q_re.astype(BF16), "q_im": q_im.astype(BF16),
        "a_chunk_re": lane_row(apr[L]), "a_chunk_im": lane_row(api[L]),
        "a1_re": lane_row(apr[1]), "a1_im": lane_row(api[1]),
        "b1_re": p0r.astype(BF16), "b1_im": p0i.astype(BF16),
        "c1_re": q0r.astype(BF16), "c1_im": (-q0i).astype(BF16),
    }


def _compress_paged_kernel(pt_ref, *refs, pg, nsb):
    del pt_ref
    pages = refs[:pg]
    perm_ref, wbd_ref, pe_ref, w2_ref, out_ref, h0_ref, h1_ref = refs[pg:]
    j = pl.program_id(1)
    sbp = PAGE_SIZE // CMP_STRIDE
    nrows = pg * sbp
    pair_rows = 2 * sbp

    @pl.when(j == 0)
    def _():
        h1_ref[:, nsb:nsb + SUBLANES, :] = jnp.zeros((2, SUBLANES, KV_W), F32)

    r0 = pl.multiple_of(j * nrows, nrows)
    for kv in range(2):
        staged = []
        for q in range(pg // 2):
            z = jnp.concatenate([pages[2 * q][0, kv], pages[2 * q + 1][0, kv]], axis=1).astype(BF16)
            staged.append(_dot_nt(perm_ref[...], z).astype(BF16))

        def load_rows(s, kv, staged=staged):
            return jnp.concatenate([x[s * pair_rows:(s + 1) * pair_rows, :] for x in staged], axis=0)

        pr0, pr1 = _compress_half(load_rows, kv, nrows, wbd_ref, pe_ref)
        h0_ref[kv, pl.ds(r0, nrows), :] = pr0
        h1_ref[kv, pl.ds(r0, nrows), :] = pr1

    @pl.when(j == pl.num_programs(1) - 1)
    def _():
        for kv in range(2):
            h = h0_ref[kv] + h1_ref[kv, pl.ds(1, nsb), :]
            out = _dot(jax.nn.gelu(h).astype(BF16), w2_ref[kv])
            out_ref[0, :, kv * KV_W:(kv + 1) * KV_W] = out.astype(BF16)


def _page_specs(pg, pages_per_sample):
    def spec(i):
        return pl.BlockSpec((1, 2, KV_W, PAGE_SIZE),
                            lambda b, j, pt: (pt[b * pages_per_sample + j * pg + i], 0, 0, 0))
    return [spec(i) for i in range(pg)]


def _compress_paged(pages, pt_flat, wbd, pe_t, w2bd, *, nb, pages_per_sample, pg):
    nsb = pages_per_sample * PAGE_SIZE // CMP_STRIDE
    sbp = PAGE_SIZE // CMP_STRIDE
    i_out = jnp.arange(2 * PAGE_SIZE)
    s_i, pg_i, n_i = i_out // (2 * sbp), (i_out // sbp) % 2, i_out % sbp
    perm = (i_out[None, :] == (pg_i * PAGE_SIZE + n_i * CMP_STRIDE + s_i)[:, None]).astype(BF16)
    c3 = lambda shape: pl.BlockSpec(shape, lambda b, j, pt: (0,) * len(shape))
    grid_spec = pltpu.PrefetchScalarGridSpec(
        num_scalar_prefetch=1,
        grid=(nb, pages_per_sample // pg),
        in_specs=_page_specs(pg, pages_per_sample) + [c3(perm.shape), c3(wbd.shape), c3(pe_t.shape), c3(w2bd.shape)],
        out_specs=pl.BlockSpec((1, nsb, 2 * KV_W), lambda b, j, pt: (b, 0, 0)),
        scratch_shapes=[pltpu.VMEM((2, nsb, KV_W), F32), pltpu.VMEM((2, nsb + SUBLANES, KV_W), F32)],
    )
    return pl.pallas_call(
        functools.partial(_compress_paged_kernel, pg=pg, nsb=nsb),
        grid_spec=grid_spec,
        out_shape=jax.ShapeDtypeStruct((nb, nsb, 2 * KV_W), BF16),
        compiler_params=_params("parallel", "arbitrary"),
        name="nsa_compress_paged",
    )(pt_flat, *([pages] * pg), perm, wbd, pe_t, w2bd)


def _topk_mask_lanes(imp, n_top, ns_valid):
    lane = lax.broadcasted_iota(jnp.int32, imp.shape, 1)
    cnt = jnp.zeros(imp.shape, F32)
    for sp in range(ns_valid):
        col = imp[:, sp:sp + 1]
        cnt = cnt + jnp.where(lane > sp, jnp.where(col >= imp, 1.0, 0.0), jnp.where(col > imp, 1.0, 0.0))
    return jnp.where((cnt < n_top) & (lane < ns_valid), 1.0, 0.0)


def _dot_nt(a, b):
    return lax.dot_general(a, b, (((1,), (1,)), ((), ())), preferred_element_type=F32)


def _attn_sample_kernel(pt_ref, *refs, pg, past, nc, ns_valid, n_top):
    del pt_ref
    pages = refs[:pg]
    (q_ref, qr_ref, g_ref, kcvc_ref, ksn_ref, kwn_ref, win_ref, ov_ref, e_ref, gs_ref, o_ref,
     m_ref, l_ref, acc_ref, sel_ref, oc_ref, ow_ref) = refs[pg:]
    j = pl.program_id(1)
    ncp = kcvc_ref.shape[1]
    nsp = ov_ref.shape[1]
    wlen = win_ref.shape[3]
    row = lax.broadcasted_iota(jnp.int32, (N_HEADS, KV_W), 0)
    lane = lax.broadcasted_iota(jnp.int32, (N_HEADS, KV_W), 1)
    own = (lane // HEAD_DIM) == (row // GROUP)

    def spread(ref):
        q = ref[0]
        return jnp.where(own, jnp.concatenate([q] * N_KV_HEADS, axis=1), jnp.zeros((N_HEADS, KV_W), BF16))

    def update(state, s, vs):
        m, l, acc = state
        m_new = jnp.maximum(m, jnp.max(s, axis=1, keepdims=True))
        alpha = jnp.exp(m - m_new)
        p = jnp.exp(s - m_new)
        l = alpha * l + jnp.sum(p, axis=1, keepdims=True)
        pv = None
        for st, sz, v, feature_major in vs:
            pb = p[:, st:st + sz].astype(BF16)
            t = _dot_nt(pb, v) if feature_major else _dot(pb, v)
            pv = t if pv is None else pv + t
        return m_new, l, alpha * acc + pv

    def init():
        return (jnp.full((N_HEADS, 1), NEG, F32), jnp.zeros((N_HEADS, 1), F32), jnp.zeros((N_HEADS, KV_W), F32))

    def new_row_update(state, qbd, new_ref):
        r8 = lax.broadcasted_iota(jnp.int32, (SUBLANES, 2 * KV_W), 0)
        tile = jnp.where(r8 == 0, jnp.broadcast_to(new_ref[0], (SUBLANES, 2 * KV_W)), 0.0).astype(BF16)
        s = _dot_nt(qbd, tile[:, 0:KV_W])
        l8 = lax.broadcasted_iota(jnp.int32, (N_HEADS, SUBLANES), 1)
        s = jnp.where(l8 == 0, s, MASKED)
        return update(state, s, [(0, SUBLANES, tile[:, KV_W:2 * KV_W], False)])

    def put(state):
        m, l, acc = state
        m_ref[...] = jnp.broadcast_to(m, m_ref.shape)
        l_ref[...] = jnp.broadcast_to(l, l_ref.shape)
        acc_ref[...] = acc

    qrbd = spread(qr_ref)

    @pl.when(j == 0)
    def _():
        s = _dot_nt(spread(q_ref), kcvc_ref[0, :, 0:KV_W])
        cidx = lax.broadcasted_iota(jnp.int32, (1, ncp), 1)
        valid = ((cidx * CMP_STRIDE + (CMP_BLOCK - 1)) <= past) & (cidx < nc)
        sm = jnp.where(valid, s, NEG)
        mx = jnp.max(sm, axis=1, keepdims=True)
        e = jnp.where(valid, jnp.exp(sm - mx), 0.0)
        p = e / jnp.maximum(jnp.sum(e, axis=1, keepdims=True), 1e-20)
        oc_ref[...] = _dot(p.astype(BF16), kcvc_ref[0, :, KV_W:2 * KV_W])
        imp = _dot_f32rhs(_dot_f32lhs(gs_ref[...], p), ov_ref[...])
        sidx = lax.broadcasted_iota(jnp.int32, (1, nsp), 1)
        cur = past // SEL_BLOCK
        forced = (sidx == 0) | (sidx == cur) | (sidx == cur - 1)
        causal = (sidx * SEL_BLOCK) <= past
        imp = jnp.where(forced, FORCE, jnp.where(causal, imp, NEG))
        imp = jnp.where(sidx < ns_valid, imp, MASKED)
        sel_ref[...] = _topk_mask_lanes(imp, n_top, ns_valid)
        sw = _dot(qrbd, win_ref[0, 0].astype(BF16))
        wpos = past - wlen + lax.broadcasted_iota(jnp.int32, (1, wlen), 1)
        sw = jnp.where((wpos >= 0) & (past - wpos <= WINDOW), sw, MASKED)
        st = update(init(), sw, [(0, wlen, win_ref[0, 1].astype(BF16), True)])
        st = new_row_update(st, qrbd, kwn_ref)
        ow_ref[...] = st[2] * (1.0 / jnp.maximum(st[1], 1e-20))
        put(new_row_update(init(), qrbd, ksn_ref))

    halves = [pages[:pg // 2], pages[pg // 2:]] if pg > 1 else [pages]
    scores = [jnp.concatenate([_dot(qrbd, r[0, 0].astype(BF16)) for r in part], axis=1) for part in halves]
    mask = _dot(sel_ref[...].astype(BF16), e_ref[0]) > 0.5
    state = (m_ref[:, 0:1], l_ref[:, 0:1], acc_ref[...])
    k0 = 0
    for part, s in zip(halves, scores):
        width = len(part) * PAGE_SIZE
        s = jnp.where(mask[:, k0:k0 + width], s, MASKED)
        vs = [(i * PAGE_SIZE, PAGE_SIZE, r[0, 1].astype(BF16), True) for i, r in enumerate(part)]
        state = update(state, s, vs)
        k0 += width
    put(state)

    @pl.when(j == pl.num_programs(1) - 1)
    def _():
        g = g_ref[0]
        os_ = acc_ref[...] * (1.0 / jnp.maximum(l_ref[:, 0:1], 1e-20))
        o = g[:, 0:1] * oc_ref[...] + g[:, 1:2] * os_ + g[:, 2:3] * ow_ref[...]
        o = jnp.where(own, o, 0.0)
        out = o[:, 0:HEAD_DIM]
        for h in range(1, N_KV_HEADS):
            out = out + o[:, h * HEAD_DIM:(h + 1) * HEAD_DIM]
        o_ref[0] = out.astype(BF16)


def _attn_sample(pages, pt_flat, q3, qr3, g3, kcvc, ks_new, kw_new, win, ov, e_mat, gsum, *, nb, pages_per_sample,
                 pg, past, nc, ns_valid, n_top):
    nsp = ov.shape[1]
    per_b = lambda shape: pl.BlockSpec((1,) + shape, lambda b, j, pt: (b,) + (0,) * len(shape))
    const = lambda shape: pl.BlockSpec(shape, lambda b, j, pt: (0,) * len(shape))
    grid_spec = pltpu.PrefetchScalarGridSpec(
        num_scalar_prefetch=1,
        grid=(nb, pages_per_sample // pg),
        in_specs=_page_specs(pg, pages_per_sample) + [
            per_b((N_HEADS, HEAD_DIM)), per_b((N_HEADS, HEAD_DIM)), per_b((N_HEADS, 3)),
            per_b(kcvc.shape[1:]), per_b((1, 2 * KV_W)), per_b((1, 2 * KV_W)), per_b(win.shape[1:]),
            const(ov.shape), pl.BlockSpec((1,) + e_mat.shape[1:], lambda b, j, pt: (j, 0, 0)), const(gsum.shape)],
        out_specs=per_b((N_HEADS, HEAD_DIM)),
        scratch_shapes=[pltpu.VMEM((N_HEADS, LANES), F32), pltpu.VMEM((N_HEADS, LANES), F32),
                        pltpu.VMEM((N_HEADS, KV_W), F32), pltpu.VMEM((N_HEADS, nsp), F32),
                        pltpu.VMEM((N_HEADS, KV_W), F32), pltpu.VMEM((N_HEADS, KV_W), F32)],
    )
    return pl.pallas_call(
        functools.partial(_attn_sample_kernel, pg=pg, past=past, nc=nc, ns_valid=ns_valid, n_top=n_top),
        grid_spec=grid_spec,
        out_shape=jax.ShapeDtypeStruct((nb, N_HEADS, HEAD_DIM), BF16),
        compiler_params=_params("parallel", "arbitrary"),
        name="nsa_attn_sample",
    )(pt_flat, *([pages] * pg), q3, qr3, g3, kcvc, ks_new, kw_new, win, ov, e_mat, gsum)


def _rope_tables(pos):
    half = HEAD_DIM // 2
    inv = ROPE_THETA ** (-jnp.arange(half, dtype=F32) / half)
    ang = pos.astype(F32)[:, None] * inv[None, :]
    cos = jnp.cos(ang)
    sin = jnp.sin(ang)
    reps = LANES // HEAD_DIM
    return (jnp.tile(jnp.concatenate([cos, cos], axis=1), (1, reps)),
            jnp.tile(jnp.concatenate([-sin, sin], axis=1), (1, reps)))


def _compress_weights(cmp_w1, cmp_w2, cmp_pe):
    ratio = CMP_BLOCK // CMP_STRIDE
    eye = jnp.eye(N_KV_HEADS, dtype=F32)
    w1r = cmp_w1.reshape(2, ratio, CMP_STRIDE, HEAD_DIM, HEAD_DIM)
    wbd = jnp.einsum("krsde,hg->krshdge", w1r, eye).reshape(2 * ratio * CMP_STRIDE, KV_W, KV_W).astype(BF16)
    w2bd = jnp.einsum("kef,hg->khegf", cmp_w2, eye).reshape(2, KV_W, KV_W).astype(BF16)
    pe_r = cmp_pe.reshape(2, ratio, CMP_STRIDE, HEAD_DIM).astype(F32)
    pe_w = jnp.einsum("krsd,krsde->kre", pe_r, w1r.astype(F32), precision=lax.Precision.HIGHEST)
    pe_t = jnp.tile(pe_w.reshape(2 * ratio, HEAD_DIM), (1, N_KV_HEADS))
    return wbd, pe_t, w2bd


def _overlap(nc, ncp, nsel, nsp):
    c_start = jnp.arange(ncp, dtype=jnp.int32)[:, None] * CMP_STRIDE
    s_start = jnp.arange(nsp, dtype=jnp.int32)[None, :] * SEL_BLOCK
    ov = (c_start < s_start + SEL_BLOCK) & (c_start + CMP_BLOCK > s_start)
    ov = ov & (jnp.arange(ncp)[:, None] < nc) & (jnp.arange(nsp)[None, :] < nsel)
    return ov.astype(BF16)


def _pick(*cands):
    return next(c for c in cands if c)


def _tile(n, pref):
    t = min(n, pref)
    while n % t:
        t //= 2
    return t


def kernel(x_prompt, x_sample, cache_kv_cmp, cache_kv_sel, cache_kv_win, state_ssm, page_table, norm_g, mlp_w1,
           mlp_w2, nsa_w_in, nsa_w_o, nsa_cmp_w1, nsa_cmp_w2, nsa_cmp_pe, s5_a_re, s5_a_im, s5_log_dt, s5_b_re,
           s5_b_im, s5_c_re, s5_c_im, s5_d, s5_w_glu, s5_b_glu):
    b, t, d = x_prompt.shape
    nb = x_sample.shape[0]
    pages_per_sample = page_table.shape[1]
    past = pages_per_sample * PAGE_SIZE
    rows_p = b * t
    g = norm_g.reshape(norm_g.shape[0], 4, 1, d)

    w_in = nsa_w_in[0]
    w_main = w_in[:, :Q_W + 6 * KV_W].astype(BF16)
    w_gate = jnp.pad(w_in[:, Q_W + 6 * KV_W:], ((0, 0), (0, LANES - 3 * N_HEADS))).astype(BF16)
    w_o = nsa_w_o[0].astype(BF16)
    wbd, pe_t, w2bd = _compress_weights(nsa_cmp_w1[0], nsa_cmp_w2[0], nsa_cmp_pe[0])
    w1 = mlp_w1.astype(BF16)
    w2 = mlp_w2.astype(BF16)
    w_glu = s5_w_glu[0].astype(BF16)
    ops = _s5_operators(s5_a_re[0], s5_a_im[0], s5_log_dt[0], s5_b_re[0], s5_b_im[0], s5_c_re[0], s5_c_im[0])
    d_skip = s5_d[0].reshape(1, d)
    b_glu = s5_b_glu[0].reshape(1, d)

    tm = _tile(rows_p, 512)
    ff_chunk = _tile(mlp_w1.shape[2], 1024)

    xp = x_prompt.reshape(rows_p, d)
    cos_p, sin_p = _rope_tables(jnp.arange(t, dtype=jnp.int32))
    tq = next(c for c in (2 * LANES, LANES) if t % c == 0 and WINDOW % c == 0)
    (qT, qrT, gT, kvc, kvcT, kvsT, kvwT, ksb, kwb, vsT, vwT) = _inproj(
        xp, g[0, 0], w_main, w_gate, cos_p, sin_p, tm=_tile(t, 512), pos_blocks=t // _tile(t, 512), transposed=True,
        key_chunk=tq)
    nsb_p = t // CMP_STRIDE
    nc_p = nsb_p - CMP_BLOCK // CMP_STRIDE + 1
    nsel_p = t // SEL_BLOCK
    kc, vcT = _compress_prompt(kvc.reshape(b, t, 2 * KV_W), wbd, pe_t, w2bd)
    ovT = _overlap(nc_p, nsb_p, nsel_p, nsel_p).T
    oT = _attn_prompt(qT, qrT, gT, kc, vcT, ksb, vsT, kwb, vwT, ovT, batch=b, seq=t, tq=tq, nc=nc_p,
                      n_top=min(TOP_N, nsel_p))
    hp, xm = _oproj(oT, w_o, xp, g[0, 1], g[0, 2], tm=tm, transposed=True)
    hp, xn1 = _mlp(xm, hp, w1[0], w2[0], g[0, 3], g[1, 0], tm=tm, ff_chunk=ff_chunk, next_norm=True)

    pw = SSM_UNIT * SSM_GROUP
    npair = d // pw
    y3, hfin = _s5_seq(xn1.reshape(b, t, d), ops)
    yc = y3.reshape(rows_p, d)
    hp, xm = _s5_out(yc, xn1, d_skip, w_glu, b_glu, hp, g[1, 1], g[1, 2], tm=tm)
    (hp,) = _mlp(xm, hp, w1[1], w2[1], g[1, 3], g[1, 3], tm=tm, ff_chunk=ff_chunk, next_norm=False)
    ssm_p = hfin.reshape(npair, 2, b, SSM_UNIT, SSM_STATE).transpose(2, 1, 0, 3, 4)
    ssm_p = ssm_p.reshape(b, 2, d // SSM_GROUP, SSM_STATE)

    xs = x_sample.reshape(nb, d)
    cos_s, sin_s = _rope_tables(jnp.full((nb,), past, dtype=jnp.int32))
    q_s, qr_s, gates_s, kvc_s, kvs_s, kvw_s = _inproj(
        xs, g[0, 0], w_main, w_gate, cos_s, sin_s, tm=nb, pos_blocks=1, transposed=False)
    pt_flat = page_table.reshape(-1).astype(jnp.int32)
    pg = _tile(pages_per_sample, PAGE_GROUP)
    n_pool = cache_kv_cmp.shape[1]
    feature_major = lambda c, n, s: c.transpose(0, 2, 3, 4, 1).reshape(n, 2, KV_W, s)
    cmp_pages = feature_major(cache_kv_cmp[0], n_pool, PAGE_SIZE)
    sel_pages = feature_major(cache_kv_sel[0], n_pool, PAGE_SIZE)
    kcvc = _compress_paged(cmp_pages, pt_flat, wbd, pe_t, w2bd, nb=nb, pages_per_sample=pages_per_sample, pg=pg)
    l_all = past + 1
    nsb_s = l_all // CMP_STRIDE
    nc_s = nsb_s - CMP_BLOCK // CMP_STRIDE + 1
    nsel_s = -(-l_all // SEL_BLOCK)
    nsp = -(-nsel_s // LANES) * LANES
    ov_s = _overlap(nc_s, past // CMP_STRIDE, nsel_s, nsp)
    keys_per_step = pg * PAGE_SIZE
    key_blk = (jnp.arange(past, dtype=jnp.int32) // SEL_BLOCK).reshape(past // keys_per_step, 1, keys_per_step)
    e_mat = (jnp.arange(nsp, dtype=jnp.int32)[None, :, None] == key_blk).astype(BF16)
    hh = jnp.arange(N_HEADS)
    gsum = ((hh[:, None] // GROUP) == (hh[None, :] // GROUP)).astype(BF16)
    win = feature_major(cache_kv_win[0], nb, WINDOW)
    o_s = _attn_sample(sel_pages, pt_flat, q_s.reshape(nb, N_HEADS, HEAD_DIM), qr_s.reshape(nb, N_HEADS, HEAD_DIM),
                       gates_s[:, :3 * N_HEADS].reshape(nb, N_HEADS, 3), kcvc, kvs_s.reshape(nb, 1, 2 * KV_W),
                       kvw_s.reshape(nb, 1, 2 * KV_W), win, ov_s, e_mat, gsum, nb=nb,
                       pages_per_sample=pages_per_sample, pg=pg, past=past, nc=nc_s, ns_valid=nsel_s,
                       n_top=min(TOP_N, nsel_s))
    hs, xm_s = _oproj(o_s.reshape(nb, Q_W), w_o, xs, g[0, 1], g[0, 2], tm=nb, transposed=False)
    hs, xn1_s = _mlp(xm_s, hs, w1[0], w2[0], g[0, 3], g[1, 0], tm=nb, ff_chunk=ff_chunk, next_norm=True)

    u2_s = xn1_s.astype(BF16).reshape(nb, npair, pw).transpose(1, 0, 2)
    st = state_ssm[0].reshape(nb, 2, npair, SSM_UNIT * SSM_STATE).transpose(1, 2, 0, 3)
    y2_s, hr_s, hi_s = _s5_step(u2_s, st[0], st[1], ops)
    yc_s = y2_s.transpose(1, 0, 2).reshape(nb, d)
    hs, xm_s = _s5_out(yc_s, xn1_s, d_skip, w_glu, b_glu, hs, g[1, 1], g[1, 2], tm=nb)
    (hs,) = _mlp(xm_s, hs, w1[1], w2[1], g[1, 3], g[1, 3], tm=nb, ff_chunk=ff_chunk, next_norm=False)
    ssm_s = jnp.stack([hr_s, hi_s], axis=0).transpose(2, 0, 1, 3).reshape(nb, 2, d // SSM_GROUP, SSM_STATE)

    kv5 = lambda a, n, s: a.reshape(1, n, s, 2, N_KV_HEADS, HEAD_DIM)
    from_fm = lambda a, n, s: a.reshape(n, 2, N_KV_HEADS, HEAD_DIM, s).transpose(0, 4, 1, 2, 3)[None]
    win_s = jnp.concatenate([win[..., 1:], kvw_s.reshape(nb, 2, KV_W, 1)], axis=-1)
    return (hp.reshape(b, t, d), hs.reshape(nb, 1, d),
            from_fm(kvcT, b, t), kv5(kvc_s, nb, 1), from_fm(kvsT, b, t), kv5(kvs_s, nb, 1),
            from_fm(kvwT[:, :, t - WINDOW:], b, WINDOW), from_fm(win_s, nb, WINDOW), ssm_p[None], ssm_s[None])
```

```python
import functools

import jax
import jax.numpy as jnp
from jax import lax
from jax.experimental import pallas as pl
from jax.experimental.pallas import tpu as pltpu

N_HEADS = 16
HEAD_DIM = 64
N_KV_HEADS = 4
GROUP = N_HEADS // N_KV_HEADS
CMP_BLOCK = 32
CMP_STRIDE = 16
SEL_BLOCK = 64
TOP_N = 16
WINDOW = 512
ROPE_THETA = 10000.0
PAGE_SIZE = 128
SSM_GROUP = 16
SSM_STATE = 64
SSM_CHUNK = 8
SSM_UNIT = 8
EPS = 1e-6
NEG = -1e30
FORCE = 1e9
MASKED = -1.5e38
LOG2E = 1.4426950408889634
V_ROWS = HEAD_DIM + 16
Q_W = N_HEADS * HEAD_DIM
KV_W = N_KV_HEADS * HEAD_DIM
LANES = 128
SUBLANES = 8
PAGE_GROUP = 16
S5_ROW_BLOCK = 1024
S5_TIME_BLOCK = 2048
VMEM_LIMIT = 56 * 1024 * 1024

F32 = jnp.float32
BF16 = jnp.bfloat16


def _params(*sem):
    return pltpu.CompilerParams(dimension_semantics=sem, vmem_limit_bytes=VMEM_LIMIT)


def _full(shape):
    zeros = (0,) * len(shape)
    return pl.BlockSpec(shape, lambda *_: zeros)


def _rms(x, g):
    ms = jnp.mean(x * x, axis=-1, keepdims=True)
    return x * lax.rsqrt(ms + EPS) * g


def _dot(a, b):
    return jnp.dot(a, b, preferred_element_type=F32)


def _dot_f32lhs(w, x):
    hi = x.astype(BF16)
    r1 = x - hi.astype(F32)
    mid = r1.astype(BF16)
    lo = (r1 - mid.astype(F32)).astype(BF16)
    return _dot(w, hi) + _dot(w, mid) + _dot(w, lo)


def _dot_f32rhs(x, w):
    hi = x.astype(BF16)
    r1 = x - hi.astype(F32)
    mid = r1.astype(BF16)
    lo = (r1 - mid.astype(F32)).astype(BF16)
    return _dot(hi, w) + _dot(mid, w) + _dot(lo, w)


def _rope_nat(x, cos, sin):
    half = HEAD_DIM // 2
    lane = lax.broadcasted_iota(jnp.int32, (1, LANES), 1)
    first = (lane % HEAD_DIM) < half
    outs = []
    for c in range(x.shape[1] // LANES):
        xc = x[:, c * LANES:(c + 1) * LANES]
        rot = jnp.where(first, pltpu.roll(xc, LANES - half, 1), pltpu.roll(xc, half, 1))
        outs.append(xc * cos + rot * sin)
    return jnp.concatenate(outs, axis=1)


def _inproj_kernel(x_ref, g_ref, w_ref, wg_ref, cos_ref, sin_ref, *outs, transposed, key_chunk):
    xb = _rms(x_ref[...], g_ref[...]).astype(BF16)
    cos = cos_ref[...]
    sin = sin_ref[...]
    scale = HEAD_DIM ** -0.5
    q = _dot(xb, w_ref[:, 0:Q_W])
    qr = _rope_nat(q, cos, sin)
    kv = _dot(xb, w_ref[:, Q_W:Q_W + 6 * KV_W])
    gates = jax.nn.sigmoid(_dot(xb, wg_ref[...]))
    k_s = _rope_nat(kv[:, 2 * KV_W:3 * KV_W], cos, sin)
    v_s = kv[:, 3 * KV_W:4 * KV_W]
    k_w = _rope_nat(kv[:, 4 * KV_W:5 * KV_W], cos, sin)
    v_w = kv[:, 5 * KV_W:6 * KV_W]
    kvc_ref = outs[3]
    kvc_ref[...] = kv[:, 0:2 * KV_W]
    if transposed:
        qT_ref, qrT_ref, gT_ref, _, kvcT_ref, kvsT_ref, kvwT_ref, ksb_ref, kwb_ref, vsT_ref, vwT_ref = outs
        qT_ref[...] = (q * (scale * LOG2E)).T.astype(BF16)
        qrT_ref[...] = (qr * (scale * LOG2E)).T.astype(BF16)
        gT_ref[...] = gates.T
        tm = x_ref.shape[0]
        rowi = lax.broadcasted_iota(jnp.int32, (tm, LANES), 0)
        lanei = lax.broadcasted_iota(jnp.int32, (tm, LANES), 1)
        blk = lax.shift_right_logical(rowi & (key_chunk - 1), SEL_BLOCK.bit_length() - 1)
        extra = lanei - HEAD_DIM
        nblk = key_chunk // SEL_BLOCK
        aug = jnp.where((extra == blk) | ((extra >= nblk) & (extra < nblk + N_CONST_LANES)), 1.0, 0.0)
        for c in range(KV_W // LANES):
            for k_nat, k_ref in ((k_s, ksb_ref), (k_w, kwb_ref)):
                pair = k_nat[:, c * LANES:(c + 1) * LANES]
                k_ref[2 * c] = jnp.where(lanei < HEAD_DIM, pair, aug).astype(BF16)
                k_ref[2 * c + 1] = jnp.where(lanei < HEAD_DIM, pltpu.roll(pair, HEAD_DIM, 1), aug).astype(BF16)
        v_sT = v_s.T
        v_wT = v_w.T
        ones_rows = jnp.where(lax.broadcasted_iota(jnp.int32, (V_ROWS - HEAD_DIM, tm), 0) == 0, 1.0, 0.0)
        for h in range(N_KV_HEADS):
            for vT, v_ref in ((v_sT, vsT_ref), (v_wT, vwT_ref)):
                v_ref[h * V_ROWS:(h + 1) * V_ROWS, :] = jnp.concatenate(
                    [vT[h * HEAD_DIM:(h + 1) * HEAD_DIM, :], ones_rows], axis=0).astype(BF16)
        kvcT_ref[0] = kv[:, 0:2 * KV_W].T
        kvsT_ref[0, 0:KV_W, :] = k_s.T
        kvsT_ref[0, KV_W:2 * KV_W, :] = v_sT
        kvwT_ref[0, 0:KV_W, :] = k_w.T
        kvwT_ref[0, KV_W:2 * KV_W, :] = v_wT
    else:
        q_ref, qr_ref, gt_ref, _, kvs_ref, kvw_ref = outs
        q_ref[...] = (q * scale).astype(BF16)
        qr_ref[...] = (qr * scale).astype(BF16)
        gt_ref[...] = gates
        kvs_ref[:, 0:KV_W] = k_s
        kvs_ref[:, KV_W:2 * KV_W] = v_s
        kvw_ref[:, 0:KV_W] = k_w
        kvw_ref[:, KV_W:2 * KV_W] = v_w


def _inproj(x, g, w_main, w_gate, cos_t, sin_t, *, tm, pos_blocks, transposed, key_chunk=LANES):
    rows, d = x.shape
    assert tm % key_chunk == 0 or not transposed
    n = rows // tm
    row_blk = lambda w: pl.BlockSpec((tm, w), lambda i: (i, 0))
    col_blk = lambda h: pl.BlockSpec((h, tm), lambda i: (0, i))
    tab = pl.BlockSpec((tm, LANES), lambda i: (i % pos_blocks, 0))
    kv_nat = jax.ShapeDtypeStruct((rows, 2 * KV_W), F32)
    if transposed:
        seqs = rows // (pos_blocks * tm)
        kh = pl.BlockSpec((N_KV_HEADS, tm, LANES), lambda i: (0, i, 0))
        kvT = pl.BlockSpec((1, 2 * KV_W, tm), lambda i: (i // pos_blocks, 0, i % pos_blocks))
        out_shape = ([jax.ShapeDtypeStruct((Q_W, rows), BF16)] * 2 + [jax.ShapeDtypeStruct((LANES, rows), F32)]
                     + [kv_nat] + [jax.ShapeDtypeStruct((seqs, 2 * KV_W, pos_blocks * tm), F32)] * 3
                     + [jax.ShapeDtypeStruct((N_KV_HEADS, rows, LANES), BF16)] * 2
                     + [jax.ShapeDtypeStruct((N_KV_HEADS * V_ROWS, rows), BF16)] * 2)
        out_specs = ([col_blk(Q_W)] * 2 + [col_blk(LANES)] + [row_blk(2 * KV_W)] + [kvT] * 3 + [kh] * 2
                     + [col_blk(N_KV_HEADS * V_ROWS)] * 2)
    else:
        out_shape = ([jax.ShapeDtypeStruct((rows, Q_W), BF16)] * 2 + [jax.ShapeDtypeStruct((rows, LANES), F32)]
                     + [kv_nat] * 3)
        out_specs = [row_blk(Q_W)] * 2 + [row_blk(LANES)] + [row_blk(2 * KV_W)] * 3
    return pl.pallas_call(
        functools.partial(_inproj_kernel, transposed=transposed, key_chunk=key_chunk),
        grid=(n,),
        in_specs=[row_blk(d), _full((1, d)), _full(w_main.shape), _full(w_gate.shape), tab, tab],
        out_specs=out_specs,
        out_shape=out_shape,
        compiler_params=_params("parallel"),
        name="nsa_inproj",
    )(x, g, w_main, w_gate, cos_t, sin_t)


_KV_CHUNKS = 2 * KV_W // LANES


def _compress_half(load_rows, kv, nrows, wbd_ref, pe_ref):
    ratio = CMP_BLOCK // CMP_STRIDE
    accs = [jnp.broadcast_to(pe_ref[kv * ratio + r:kv * ratio + r + 1, :], (nrows, KV_W)) for r in range(ratio)]
    for s in range(CMP_STRIDE):
        lhs = load_rows(s, kv).astype(BF16)
        for r in range(ratio):
            accs[r] = accs[r] + _dot(lhs, wbd_ref[(kv * ratio + r) * CMP_STRIDE + s])
    return accs


def _compress_prompt_kernel(*refs, nsb):
    x_refs = refs[:_KV_CHUNKS]
    wbd_ref, pe_ref, w2_ref, kc_ref, vcT_ref, sh_ref = refs[_KV_CHUNKS:]

    def load_rows(s, kv):
        per_half = _KV_CHUNKS // 2
        return jnp.concatenate([x_refs[kv * per_half + c][0, pl.ds(s, nsb, stride=CMP_STRIDE), :]
                                for c in range(per_half)], axis=1)

    sh_ref[nsb:nsb + SUBLANES, :] = jnp.zeros((SUBLANES, KV_W), F32)
    for kv in range(2):
        pr0, pr1 = _compress_half(load_rows, kv, nsb, wbd_ref, pe_ref)
        sh_ref[0:nsb, :] = pr1
        h = pr0 + sh_ref[pl.ds(1, nsb), :]
        out = _dot(jax.nn.gelu(h).astype(BF16), w2_ref[kv])
        if kv == 0:
            for hh in range(N_KV_HEADS):
                kc_ref[0, hh] = out[:, hh * HEAD_DIM:(hh + 1) * HEAD_DIM].astype(BF16)
        else:
            vcT_ref[0] = out.T.astype(BF16)


def _compress_prompt(kvc3, wbd, pe_t, w2bd):
    b, t, _ = kvc3.shape
    nsb = t // CMP_STRIDE
    return pl.pallas_call(
        functools.partial(_compress_prompt_kernel, nsb=nsb),
        grid=(b,),
        in_specs=[pl.BlockSpec((1, t, LANES), lambda i, c=c: (i, 0, c)) for c in range(_KV_CHUNKS)]
        + [_full(wbd.shape), _full(pe_t.shape), _full(w2bd.shape)],
        out_specs=[pl.BlockSpec((1, N_KV_HEADS, nsb, HEAD_DIM), lambda i: (i, 0, 0, 0)),
                   pl.BlockSpec((1, KV_W, nsb), lambda i: (i, 0, 0))],
        out_shape=[jax.ShapeDtypeStruct((b, N_KV_HEADS, nsb, HEAD_DIM), BF16),
                   jax.ShapeDtypeStruct((b, KV_W, nsb), BF16)],
        scratch_shapes=[pltpu.VMEM((nsb + SUBLANES, KV_W), F32)],
        compiler_params=_params("parallel"),
        name="nsa_compress_prompt",
    )(*([kvc3] * _KV_CHUNKS), wbd, pe_t, w2bd)


def _topk_mask_T(imp, n_top):
    ns, w = imp.shape
    nblk = ns // SUBLANES
    blocks = [imp[r * SUBLANES:(r + 1) * SUBLANES, :] for r in range(nblk)]
    cnts = [jnp.zeros((SUBLANES, w), F32) for _ in range(nblk)]
    sub = lax.broadcasted_iota(jnp.int32, (SUBLANES, w), 0)
    for sp in range(ns):
        row = blocks[sp // SUBLANES][sp % SUBLANES:sp % SUBLANES + 1, :]
        for r in range(nblk):
            blk = blocks[r]
            if sp < r * SUBLANES:
                beats = jnp.where(row >= blk, 1.0, 0.0)
            elif sp >= (r + 1) * SUBLANES:
                beats = jnp.where(row > blk, 1.0, 0.0)
            else:
                beats = jnp.where(sub > (sp - r * SUBLANES), jnp.where(row >= blk, 1.0, 0.0),
                                  jnp.where(row > blk, 1.0, 0.0))
            cnts[r] = cnts[r] + beats
    return jnp.concatenate([jnp.where(c < n_top, 1.0, 0.0) for c in cnts], axis=0)


def _online_chunks(states, k_cs, vT_cs, qTs, bias):
    scores = [_dot(k_c, qT) for k_c, qT in zip(k_cs, qTs)]
    mids = []
    for (m, _), s in zip(states, scores):
        if bias is not None:
            s = s + bias
        m_new = jnp.maximum(m, jnp.max(s, axis=0, keepdims=True))
        mids.append((m_new, jnp.exp2(m - m_new), jnp.exp2(s - m_new).astype(BF16)))
    return tuple((m_new, alpha * acc + _dot(vT_c, p))
                 for (m_new, alpha, p), (_, acc), vT_c in zip(mids, states, vT_cs))


def _softmax_init(w):
    return (jnp.full((1, w), NEG, F32), jnp.zeros((V_ROWS, w), F32))


def _softmax_finish(carry):
    _, acc = carry
    return acc[0:HEAD_DIM, :] * (1.0 / jnp.maximum(acc[HEAD_DIM:HEAD_DIM + 1, :], 1e-20))


LAZY_LOG2_MAX = 60.0
N_CONST_LANES = 4


def _split3(x):
    hi = x.astype(BF16).astype(F32)
    r = x - hi
    mid = r.astype(BF16).astype(F32)
    return hi, mid, (r - mid).astype(BF16).astype(F32)


def _rescaling_chunks(m_ref, acc_ref, k_cs, vT_cs, qTs, base_tiles, bias):
    nh = len(k_cs)
    rows, w = base_tiles[0].shape
    zero_rows = jnp.zeros((LANES - HEAD_DIM - rows, w), BF16)
    new = _online_chunks(tuple((m_ref[h], acc_ref[h]) for h in range(nh)), k_cs, vT_cs,
                         [jnp.concatenate([qTs[h], base_tiles[h].astype(BF16), zero_rows], axis=0) for h in range(nh)],
                         bias)
    for h in range(nh):
        m_ref[h] = new[h][0]
        acc_ref[h] = new[h][1]


def _lazy_chunks(m_ref, acc_ref, k_cs, vT_cs, qTs, base_tiles, bias, ref_row):
    nh = len(k_cs)
    rows = base_tiles[0].shape[0]
    w = base_tiles[0].shape[1]
    rowt = lax.broadcasted_iota(jnp.int32, (rows, w), 0)
    zero_rows = jnp.zeros((LANES - HEAD_DIM - rows, w), BF16)

    def queries(h, with_reference):
        tile = base_tiles[h]
        if with_reference:
            hi, mid, lo = _split3(-m_ref[h])
            tile = jnp.where(rowt == ref_row, hi, jnp.where(rowt == ref_row + 1, mid,
                                                            jnp.where(rowt == ref_row + 2, lo, tile)))
        return jnp.concatenate([qTs[h], tile.astype(BF16), zero_rows], axis=0)

    scores = [_dot(k_cs[h], queries(h, True)) for h in range(nh)]
    peak = None
    probs = []
    for s in scores:
        if bias is not None:
            s = s + bias
        top = jnp.max(s, axis=0, keepdims=True)
        peak = top if peak is None else jnp.maximum(peak, top)
        probs.append(jnp.exp2(s).astype(BF16))
    pvs = [_dot(vT_cs[h], probs[h]) for h in range(nh)]
    in_range = jnp.max(peak) <= LAZY_LOG2_MAX

    @pl.when(in_range)
    def _():
        for h in range(nh):
            acc_ref[h] = acc_ref[h] + pvs[h]

    @pl.when(jnp.logical_not(in_range))
    def _():
        _rescaling_chunks(m_ref, acc_ref, k_cs, vT_cs, qTs, base_tiles, bias)


def _attn_prompt_kernel(qT_ref, qrT_ref, gT_ref, kc_ref, vcT_ref, ks_ref, vsT_ref, kw_ref, vwT_ref, ovT_ref,
                        o_ref, selb_ref, oc_ref, ow_ref, m_ref, acc_ref, *, tq, nc, n_top):
    ck = tq
    i = pl.program_id(1)
    t0 = i * tq
    qpos = t0 + lax.broadcasted_iota(jnp.int32, (1, tq), 1)
    ncp = kc_ref.shape[2]
    ns = ovT_ref.shape[0]
    w = GROUP * tq
    bpc = ck // SEL_BLOCK
    sel_shift = SEL_BLOCK.bit_length() - 1
    kvhs = range(N_KV_HEADS)
    heads = [[kvh * GROUP + g for g in range(GROUP)] for kvh in kvhs]
    rows = [pl.ds(kvh * HEAD_DIM, HEAD_DIM) for kvh in kvhs]
    vrows = [pl.ds(kvh * V_ROWS, V_ROWS) for kvh in kvhs]
    qrT = [jnp.concatenate([qrT_ref[h * HEAD_DIM:(h + 1) * HEAD_DIM, :] for h in heads[kvh]], axis=1)
           for kvh in kvhs]
    kl = lax.broadcasted_iota(jnp.int32, (ck, tq), 0)
    ql = lax.broadcasted_iota(jnp.int32, (ck, tq), 1)
    tile4 = lambda b: jnp.concatenate([b] * GROUP, axis=1)
    key_le_query = tile4(jnp.where(kl <= ql, 0.0, MASKED))
    key_ge_query = tile4(jnp.where(kl >= ql, 0.0, MASKED))
    bias_rows = selb_ref.shape[2]

    def reset_softmax():
        for kvh in kvhs:
            m_ref[kvh] = jnp.full((1, w), NEG, F32)
            acc_ref[kvh] = jnp.zeros((V_ROWS, w), F32)

    cidx = lax.broadcasted_iota(jnp.int32, (ncp, 1), 0)
    valid = ((cidx * CMP_STRIDE + (CMP_BLOCK - 1)) <= qpos) & (cidx < nc)
    sidx = lax.broadcasted_iota(jnp.int32, (ns, 1), 0)
    cur = lax.shift_right_logical(qpos, sel_shift)
    forced = (sidx == 0) | (sidx == cur) | (sidx == cur - 1)
    causal = (sidx * SEL_BLOCK) <= qpos
    cmp_scores = [
        _dot(kc_ref[0, kvh], jnp.concatenate([qT_ref[h * HEAD_DIM:(h + 1) * HEAD_DIM, :] for h in heads[kvh]], axis=1))
        for kvh in kvhs]
    for kvh in kvhs:
        s = cmp_scores[kvh]
        probs = []
        for g in range(GROUP):
            sm = jnp.where(valid, s[:, g * tq:(g + 1) * tq], NEG)
            mx = jnp.max(sm, axis=0, keepdims=True)
            e = jnp.where(valid, jnp.exp2(sm - mx), 0.0)
            den = jnp.maximum(jnp.sum(e, axis=0, keepdims=True), 1e-20)
            probs.append(e / den)
        oc_ref[kvh] = _dot(vcT_ref[0, rows[kvh], :], jnp.concatenate(probs, axis=1).astype(BF16))
        psum = probs[0]
        for g in range(1, GROUP):
            psum = psum + probs[g]
        imp = _dot_f32lhs(ovT_ref[...], psum)
        imp = jnp.where(forced, FORCE, jnp.where(causal, imp, NEG))
        selb = (_topk_mask_T(imp, n_top) - 1.0) * (-MASKED)
        fill = jnp.zeros((bias_rows - bpc, tq), F32)
        for c in range(ns // bpc):
            selb_ref[kvh, c] = jnp.concatenate([selb[c * bpc:(c + 1) * bpc, :], fill], axis=0)

    n_back = WINDOW // ck
    reset_softmax()
    rowb = lax.broadcasted_iota(jnp.int32, (bias_rows, w), 0)
    for r in range(n_back + 1):
        a = i - n_back + r
        kst = pl.multiple_of(jnp.maximum(a, 0) * ck, ck)
        skip = jnp.where(a < 0, MASKED, 0.0)
        tile = jnp.where(rowb == bpc, skip, 0.0)
        bias = key_le_query if r == n_back else (key_ge_query if r == 0 else None)
        k_cs = [kw_ref[kvh, pl.ds(kst, ck), :] for kvh in kvhs]
        vT_cs = [vwT_ref[vrows[kvh], pl.ds(kst, ck)] for kvh in kvhs]
        if r == 0:
            _rescaling_chunks(m_ref, acc_ref, k_cs, vT_cs, qrT, [tile] * N_KV_HEADS, bias)
        else:
            _lazy_chunks(m_ref, acc_ref, k_cs, vT_cs, qrT, [tile] * N_KV_HEADS, bias, bpc + 1)
    for kvh in kvhs:
        ow_ref[kvh] = _softmax_finish((None, acc_ref[kvh]))

    def chunk_step(c, diagonal, lazy=True):
        kst = c * ck if isinstance(c, int) else pl.multiple_of(c * ck, ck)
        args = (m_ref, acc_ref, [ks_ref[kvh, pl.ds(kst, ck), :] for kvh in kvhs],
                [vsT_ref[vrows[kvh], pl.ds(kst, ck)] for kvh in kvhs], qrT,
                [tile4(selb_ref[kvh, c]) for kvh in kvhs], key_le_query if diagonal else None)
        if lazy:
            _lazy_chunks(*args, bpc + 1)
        else:
            _rescaling_chunks(*args)

    reset_softmax()

    @pl.when(i > 0)
    def _():
        chunk_step(0, False, lazy=False)

    def loop_body(c, carry):
        chunk_step(c, False)
        return carry

    lax.fori_loop(1, i, loop_body, 0)
    chunk_step(i, True)

    for kvh in kvhs:
        def gate_row(j, kvh=kvh):
            return jnp.concatenate([gT_ref[h * 3 + j:h * 3 + j + 1, :] for h in heads[kvh]], axis=1)

        oT = (gate_row(0) * oc_ref[kvh] + gate_row(1) * _softmax_finish((None, acc_ref[kvh]))
              + gate_row(2) * ow_ref[kvh])
        for g, h in enumerate(heads[kvh]):
            o_ref[h * HEAD_DIM:(h + 1) * HEAD_DIM, :] = oT[:, g * tq:(g + 1) * tq].astype(BF16)


def _attn_prompt(qT, qrT, gT, kc, vcT, ksb, vsT, kwb, vwT, ovT, *, batch, seq, tq, nc, n_top):
    nq = seq // tq
    nsb = kc.shape[2]
    ns = ovT.shape[0]
    col = lambda h: pl.BlockSpec((h, tq), lambda b, i: (0, b * nq + i))
    kh = pl.BlockSpec((N_KV_HEADS, seq, LANES), lambda b, i: (0, b, 0))
    vt = pl.BlockSpec((N_KV_HEADS * V_ROWS, seq), lambda b, i: (0, b))
    bf16_sublanes = 2 * SUBLANES
    return pl.pallas_call(
        functools.partial(_attn_prompt_kernel, tq=tq, nc=nc, n_top=n_top),
        grid=(batch, nq),
        in_specs=[col(Q_W), col(Q_W), col(LANES),
                  pl.BlockSpec((1, N_KV_HEADS, nsb, HEAD_DIM), lambda b, i: (b, 0, 0, 0)),
                  pl.BlockSpec((1, KV_W, nsb), lambda b, i: (b, 0, 0)),
                  kh, vt, kh, vt, _full(ovT.shape)],
        out_specs=col(Q_W),
        out_shape=jax.ShapeDtypeStruct((Q_W, batch * seq), BF16),
        scratch_shapes=[pltpu.VMEM((N_KV_HEADS, ns * SEL_BLOCK // tq, bf16_sublanes, tq), F32),
                        pltpu.VMEM((N_KV_HEADS, HEAD_DIM, GROUP * tq), F32),
                        pltpu.VMEM((N_KV_HEADS, HEAD_DIM, GROUP * tq), F32),
                        pltpu.VMEM((N_KV_HEADS, 1, GROUP * tq), F32),
                        pltpu.VMEM((N_KV_HEADS, V_ROWS, GROUP * tq), F32)],
        compiler_params=_params("parallel", "arbitrary"),
        name="nsa_attn_prompt",
    )(qT, qrT, gT, kc, vcT, ksb, vsT, kwb, vwT, ovT)


def _oproj_kernel(o_ref, w_ref, x_ref, g1_ref, g2_ref, h_ref, xm_ref, *, transposed):
    if transposed:
        y = lax.dot_general(o_ref[...], w_ref[...], (((0,), (0,)), ((), ())), preferred_element_type=F32)
    else:
        y = _dot(o_ref[...], w_ref[...])
    h = x_ref[...] + _rms(y, g1_ref[...])
    h_ref[...] = h
    xm_ref[...] = _rms(h, g2_ref[...]).astype(BF16)


def _oproj(o, w_o, x, g1, g2, *, tm, transposed):
    rows, d = x.shape
    o_spec = (pl.BlockSpec((Q_W, tm), lambda i: (0, i)) if transposed else pl.BlockSpec((tm, Q_W), lambda i: (i, 0)))
    row = pl.BlockSpec((tm, d), lambda i: (i, 0))
    return pl.pallas_call(
        functools.partial(_oproj_kernel, transposed=transposed),
        grid=(rows // tm,),
        in_specs=[o_spec, _full(w_o.shape), row, _full((1, d)), _full((1, d))],
        out_specs=[row, row],
        out_shape=[jax.ShapeDtypeStruct((rows, d), F32), jax.ShapeDtypeStruct((rows, d), BF16)],
        compiler_params=_params("parallel"),
        name="nsa_oproj",
    )(o, w_o, x, g1, g2)


def _mlp_kernel(xm_ref, h_ref, w1_ref, w2_ref, g3_ref, gn_ref, *rest, ff_chunk, next_norm):
    if next_norm:
        h2_ref, xn_ref, acc_ref = rest
    else:
        h2_ref, acc_ref = rest
    xm = xm_ref[...]
    for c in range(w1_ref.shape[1] // ff_chunk):
        cols = slice(c * ff_chunk, (c + 1) * ff_chunk)
        hm = jnp.maximum(_dot(xm, w1_ref[:, cols]), 0.0)
        part = _dot((hm * hm).astype(BF16), w2_ref[cols, :])
        if c == 0:
            acc_ref[...] = part
        else:
            acc_ref[...] += part
    h2 = h_ref[...] + _rms(acc_ref[...], g3_ref[...])
    h2_ref[...] = h2
    if next_norm:
        xn_ref[...] = _rms(h2, gn_ref[...])


def _mlp(xm, h, w1, w2, g3, gn, *, tm, ff_chunk, next_norm):
    rows, d = h.shape
    row = pl.BlockSpec((tm, d), lambda i: (i, 0))
    out_shape = [jax.ShapeDtypeStruct((rows, d), F32)] * (2 if next_norm else 1)
    return pl.pallas_call(
        functools.partial(_mlp_kernel, ff_chunk=ff_chunk, next_norm=next_norm),
        grid=(rows // tm,),
        in_specs=[row, row, _full(w1.shape), _full(w2.shape), _full((1, d)), _full((1, d))],
        out_specs=[row] * len(out_shape),
        out_shape=out_shape,
        scratch_shapes=[pltpu.VMEM((tm, d), F32)],
        compiler_params=_params("parallel"),
        name="sq_relu_mlp",
    )(xm, h, w1, w2, g3, gn)


def _s5_chunk_kernel(u_ref, brow_ref, pre_ref, pim_ref, qre_ref, qim_ref, are_ref, aim_ref, y_ref, hfin_ref,
                     sre, sim, hre, him, cre, cim, *, nb):
    steps_per_iter = SUBLANES // nb
    rows = sre.shape[0]
    uw = SSM_UNIT * SSM_GROUP
    nt = u_ref.shape[2] // uw

    @pl.when(pl.program_id(1) == 0)
    def _():
        cre[...] = jnp.zeros(cre.shape, F32)
        cim[...] = jnp.zeros(cim.shape, F32)

    u = u_ref[0]
    sre[...] = _dot(u, pre_ref[0])
    sim[...] = _dot(u, pim_ref[0])
    ar = are_ref[0]
    ai = aim_ref[0]

    def body(it, carry):
        hr, hi = carry
        r0 = pl.multiple_of(it * SUBLANES, SUBLANES)
        sr8 = sre[pl.ds(r0, SUBLANES), :]
        si8 = sim[pl.ds(r0, SUBLANES), :]
        prev_r, prev_i = [], []
        for j in range(steps_per_iter):
            prev_r.append(hr)
            prev_i.append(hi)
            sr = sr8[j * nb:(j + 1) * nb, :]
            si = si8[j * nb:(j + 1) * nb, :]
            hr, hi = ar * hr - ai * hi + sr, ar * hi + ai * hr + si
        hre[pl.ds(r0, SUBLANES), :] = jnp.concatenate(prev_r, axis=0)
        him[pl.ds(r0, SUBLANES), :] = jnp.concatenate(prev_i, axis=0)
        return hr, hi

    hr, hi = lax.fori_loop(0, rows // SUBLANES, body, (cre[...], cim[...]))
    cre[...] = hr
    cim[...] = hi
    hfin_ref[0, 0] = hr
    hfin_ref[0, 1] = hi

    hb_re = hre[...].astype(BF16)
    hb_im = him[...].astype(BF16)
    for t2 in range(0, nt, 2):
        cols = slice(t2 * uw, (t2 + 2) * uw)
        acc = _dot(hb_re, qre_ref[0, :, cols]) + _dot(hb_im, qim_ref[0, :, cols])
        for t1 in range(t2 + 2):
            lag0 = t2 - t1 + 1
            acc = acc + _dot(u[:, t1 * uw:(t1 + 1) * uw], brow_ref[0, :, lag0 * uw:(lag0 + 2) * uw])
        y_ref[0, :, cols] = acc


def _s5_chunk(u2, ops, *, nb):
    nunit, nch, width = u2.shape
    sw = ops["p_re"].shape[2]
    rows = _tile(nch, S5_ROW_BLOCK)
    per_unit = lambda a: pl.BlockSpec((1,) + a.shape[1:], lambda i, r: (i,) + (0,) * (a.ndim - 1))
    row_blk = pl.BlockSpec((1, rows, width), lambda i, r: (i, r, 0))
    args = (u2, ops["brow"], ops["p_re"], ops["p_im"], ops["q_re"], ops["q_im"], ops["a_chunk_re"], ops["a_chunk_im"])
    return pl.pallas_call(
        functools.partial(_s5_chunk_kernel, nb=nb),
        grid=(nunit, nch // rows),
        in_specs=[row_blk] + [per_unit(a) for a in args[1:]],
        out_specs=[row_blk, pl.BlockSpec((1, 2, nb, sw), lambda i, r: (i, 0, 0, 0))],
        out_shape=[jax.ShapeDtypeStruct((nunit, nch, width), F32), jax.ShapeDtypeStruct((nunit, 2, nb, sw), F32)],
        scratch_shapes=[pltpu.VMEM((rows, sw), F32)] * 4 + [pltpu.VMEM((nb, sw), F32)] * 2,
        compiler_params=_params("parallel", "arbitrary"),
        name="s5_chunk_scan",
    )(*args)


def _s5_seq_kernel(x_ref, brow_ref, pre_ref, pim_ref, qre_ref, qim_ref, are_ref, aim_ref, y_ref, hfin_ref,
                   sre, sim, hre, him, cre, cim):
    nb, tb, uw = x_ref.shape
    nt = pre_ref.shape[1] // uw
    nk = tb // nt

    @pl.when(pl.program_id(1) == 0)
    def _():
        cre[...] = jnp.zeros(cre.shape, F32)
        cim[...] = jnp.zeros(cim.shape, F32)

    u = [jnp.concatenate([x_ref[b, pl.ds(t, nk, stride=nt), :] for b in range(nb)], axis=0).astype(BF16)
         for t in range(nt)]
    u2 = [jnp.concatenate([u[2 * j], u[2 * j + 1]], axis=1) for j in range(nt // 2)]
    s_re = _dot(u2[0], pre_ref[0, 0:2 * uw, :])
    s_im = _dot(u2[0], pim_ref[0, 0:2 * uw, :])
    for j in range(1, nt // 2):
        s_re = s_re + _dot(u2[j], pre_ref[0, 2 * j * uw:(2 * j + 2) * uw, :])
        s_im = s_im + _dot(u2[j], pim_ref[0, 2 * j * uw:(2 * j + 2) * uw, :])
    sre[...] = s_re
    sim[...] = s_im
    ar = are_ref[0]
    ai = aim_ref[0]

    def body(it, carry):
        out = []
        for b in range(nb):
            hr, hi = carry[b]
            r0 = pl.multiple_of(b * nk + it * SUBLANES, SUBLANES)
            sr8 = sre[pl.ds(r0, SUBLANES), :]
            si8 = sim[pl.ds(r0, SUBLANES), :]
            prev_r, prev_i = [], []
            for j in range(SUBLANES):
                prev_r.append(hr)
                prev_i.append(hi)
                hr, hi = (ar * hr - ai * hi + sr8[j:j + 1, :], ar * hi + ai * hr + si8[j:j + 1, :])
            hre[pl.ds(r0, SUBLANES), :] = jnp.concatenate(prev_r, axis=0)
            him[pl.ds(r0, SUBLANES), :] = jnp.concatenate(prev_i, axis=0)
            out.append((hr, hi))
        return tuple(out)

    init = tuple((cre[b:b + 1, :], cim[b:b + 1, :]) for b in range(nb))
    fin = lax.fori_loop(0, nk // SUBLANES, body, init)
    for b in range(nb):
        cre[b:b + 1, :] = fin[b][0]
        cim[b:b + 1, :] = fin[b][1]
    hfin_ref[0, 0] = cre[...]
    hfin_ref[0, 1] = cim[...]

    hb_re = hre[...].astype(BF16)
    hb_im = him[...].astype(BF16)
    for t2 in range(0, nt, 2):
        cols = slice(t2 * uw, (t2 + 2) * uw)
        acc = _dot(hb_re, qre_ref[0, :, cols]) + _dot(hb_im, qim_ref[0, :, cols])
        for j in range(t2 // 2 + 1):
            lag0 = t2 - 2 * j + 1
            wpair = jnp.concatenate([brow_ref[0, :, lag0 * uw:(lag0 + 2) * uw],
                                     brow_ref[0, :, (lag0 - 1) * uw:(lag0 + 1) * uw]], axis=0)
            acc = acc + _dot(u2[j], wpair)
        for b in range(nb):
            for dt in range(2):
                y_ref[b, pl.ds(t2 + dt, nk, stride=nt), :] = acc[b * nk:(b + 1) * nk, dt * uw:(dt + 1) * uw]


def _s5_seq(x3, ops):
    nb, t, d = x3.shape
    uw = SSM_UNIT * SSM_GROUP
    nunit = d // uw
    sw = ops["p_re"].shape[2]
    tb = _tile(t, S5_TIME_BLOCK)
    nk = tb // SSM_CHUNK
    per_unit = lambda a: pl.BlockSpec((1,) + a.shape[1:], lambda i, r: (i,) + (0,) * (a.ndim - 1))
    blk = pl.BlockSpec((nb, tb, uw), lambda i, r: (0, r, i))
    args = (x3, ops["brow"], ops["p_re"], ops["p_im"], ops["q_re"], ops["q_im"], ops["a_chunk_re"], ops["a_chunk_im"])
    return pl.pallas_call(
        _s5_seq_kernel,
        grid=(nunit, t // tb),
        in_specs=[blk] + [per_unit(a) for a in args[1:]],
        out_specs=[blk, pl.BlockSpec((1, 2, nb, sw), lambda i, r: (i, 0, 0, 0))],
        out_shape=[jax.ShapeDtypeStruct((nb, t, d), F32), jax.ShapeDtypeStruct((nunit, 2, nb, sw), F32)],
        scratch_shapes=[pltpu.VMEM((nb * nk, sw), F32)] * 4 + [pltpu.VMEM((nb, sw), F32)] * 2,
        compiler_params=_params("parallel", "arbitrary"),
        name="s5_seq_scan",
    )(*args)


def _s5_step_kernel(u_ref, h0r_ref, h0i_ref, bre_ref, bim_ref, cre_ref, cim_ref, are_ref, aim_ref,
                    y_ref, hr_ref, hi_ref, *, npair):
    for p in range(npair):
        u = u_ref[p]
        ar = are_ref[p]
        ai = aim_ref[p]
        h0r = h0r_ref[p]
        h0i = h0i_ref[p]
        hr = ar * h0r - ai * h0i + _dot(u, bre_ref[p])
        hi = ar * h0i + ai * h0r + _dot(u, bim_ref[p])
        hr_ref[p] = hr
        hi_ref[p] = hi
        y_ref[p] = _dot(hr.astype(BF16), cre_ref[p]) + _dot(hi.astype(BF16), cim_ref[p])


def _s5_step(u2, h0r, h0i, ops):
    npair, rows, width = u2.shape
    args = (u2, h0r, h0i, ops["b1_re"], ops["b1_im"], ops["c1_re"], ops["c1_im"], ops["a1_re"], ops["a1_im"])
    return pl.pallas_call(
        functools.partial(_s5_step_kernel, npair=npair),
        grid=(1,),
        in_specs=[_full(a.shape) for a in args],
        out_specs=[_full((npair, rows, width)), _full(h0r.shape), _full(h0r.shape)],
        out_shape=[jax.ShapeDtypeStruct((npair, rows, width), F32), jax.ShapeDtypeStruct(h0r.shape, F32),
                   jax.ShapeDtypeStruct(h0r.shape, F32)],
        compiler_params=_params("arbitrary"),
        name="s5_single_step",
    )(*args)


def _s5_out_kernel(yc_ref, u_ref, d_ref, wg_ref, bg_ref, h_ref, g1_ref, g2_ref, h3_ref, xm_ref):
    y = jax.nn.gelu(yc_ref[...] + d_ref[...] * u_ref[...])
    out = y * jax.nn.sigmoid(_dot(y.astype(BF16), wg_ref[...]) + bg_ref[...])
    h3 = h_ref[...] + _rms(out, g1_ref[...])
    h3_ref[...] = h3
    xm_ref[...] = _rms(h3, g2_ref[...]).astype(BF16)


def _s5_out(yc, u, d_skip, w_glu, b_glu, h, g1, g2, *, tm):
    rows, d = h.shape
    row = pl.BlockSpec((tm, d), lambda i: (i, 0))
    vec = _full((1, d))
    return pl.pallas_call(
        _s5_out_kernel,
        grid=(rows // tm,),
        in_specs=[row, row, vec, _full(w_glu.shape), vec, row, vec, vec],
        out_specs=[row, row],
        out_shape=[jax.ShapeDtypeStruct((rows, d), F32), jax.ShapeDtypeStruct((rows, d), BF16)],
        compiler_params=_params("parallel"),
        name="s5_glu_out",
    )(yc, u, d_skip, w_glu, b_glu, h, g1, g2)


def _s5_operators(a_re, a_im, log_dt, b_re, b_im, c_re, c_im):
    hp = lax.Precision.HIGHEST
    g, n = a_re.shape
    gu = SSM_UNIT
    nunit = g // gu
    L = SSM_CHUNK
    uw = gu * SSM_GROUP
    sw = gu * n
    a = lax.complex(a_re.astype(F32), a_im.astype(F32))
    dt = jnp.exp(log_dt.astype(F32))[:, None]
    a_bar = jnp.exp(a * dt)
    b_bar = ((a_bar - 1.0) / a)[:, :, None] * lax.complex(b_re.astype(F32), b_im.astype(F32))
    c = lax.complex(c_re.astype(F32), c_im.astype(F32))
    pows = [jnp.ones_like(a_bar)]
    for _ in range(L):
        pows.append(pows[-1] * a_bar)
    a_pow = jnp.stack(pows).reshape(L + 1, nunit, sw)
    apr = jnp.real(a_pow)
    api = jnp.imag(a_pow)
    eye = jnp.eye(gu, dtype=F32)

    def bd_in(x):
        return jnp.einsum("pgnd,gh->pgdhn", x.reshape(nunit, gu, n, SSM_GROUP), eye).reshape(nunit, uw, sw)

    def bd_out(x):
        return jnp.einsum("pgcn,gh->pgnhc", x.reshape(nunit, gu, SSM_GROUP, n), eye).reshape(nunit, sw, uw)

    p0r, p0i = bd_in(jnp.real(b_bar)), bd_in(jnp.imag(b_bar))
    q0r, q0i = bd_out(jnp.real(c)), bd_out(jnp.imag(c))
    sr = apr[::-1][1:][:, :, None, :]
    si = api[::-1][1:][:, :, None, :]
    stack_rows = lambda x: x.transpose(1, 0, 2, 3).reshape(nunit, L * uw, sw)
    p_re = stack_rows(p0r[None] * sr - p0i[None] * si)
    p_im = stack_rows(p0r[None] * si + p0i[None] * sr)
    tr = apr[1:][:, :, :, None]
    ti = api[1:][:, :, :, None]
    stack_cols = lambda x: x.transpose(1, 2, 0, 3).reshape(nunit, sw, L * uw)
    q_re = stack_cols(q0r[None] * tr - q0i[None] * ti)
    q_im = stack_cols(-(q0r[None] * ti + q0i[None] * tr))
    xr = p0r[None] * apr[:L][:, :, None, :] - p0i[None] * api[:L][:, :, None, :]
    xi = p0r[None] * api[:L][:, :, None, :] + p0i[None] * apr[:L][:, :, None, :]
    b_lag = (jnp.einsum("tpus,psv->ptuv", xr, q0r, precision=hp)
             - jnp.einsum("tpus,psv->ptuv", xi, q0i, precision=hp))
    brow = jnp.concatenate([jnp.zeros((nunit, 1, uw, uw), F32), b_lag], axis=1)
    brow = brow.transpose(0, 2, 1, 3).reshape(nunit, uw, (L + 1) * uw)
    lane_row = lambda x: x.reshape(nunit, 1, sw)
    return {
        "brow": brow.astype(BF16),
        "p_re": p_re.astype(BF16), "p_im": p_im.astype(BF16),
        "q_re": q_re.astype(BF16), "q_im": q_im.astype(BF16),
        "a_chunk_re": lane_row(apr[L]), "a_chunk_im": lane_row(api[L]),
        "a1_re": lane_row(apr[1]), "a1_im": lane_row(api[1]),
        "b1_re": p0r.astype(BF16), "b1_im": p0i.astype(BF16),
        "c1_re": q0r.astype(BF16), "c1_im": (-q0i).astype(BF16),
    }


def _compress_paged_kernel(pt_ref, *refs, pg, nsb):
    del pt_ref
    pages = refs[:pg]
    perm_ref, wbd_ref, pe_ref, w2_ref, out_ref, h0_ref, h1_ref = refs[pg:]
    j = pl.program_id(1)
    sbp = PAGE_SIZE // CMP_STRIDE
    nrows = pg * sbp
    pair_rows = 2 * sbp

    @pl.when(j == 0)
    def _():
        h1_ref[:, nsb:nsb + SUBLANES, :] = jnp.zeros((2, SUBLANES, KV_W), F32)

    r0 = pl.multiple_of(j * nrows, nrows)
    for kv in range(2):
        staged = []
        for q in range(pg // 2):
            z = jnp.concatenate([pages[2 * q][0, kv], pages[2 * q + 1][0, kv]], axis=1).astype(BF16)
            staged.append(_dot_nt(perm_ref[...], z).astype(BF16))

        def load_rows(s, kv, staged=staged):
            return jnp.concatenate([x[s * pair_rows:(s + 1) * pair_rows, :] for x in staged], axis=0)

        pr0, pr1 = _compress_half(load_rows, kv, nrows, wbd_ref, pe_ref)
        h0_ref[kv, pl.ds(r0, nrows), :] = pr0
        h1_ref[kv, pl.ds(r0, nrows), :] = pr1

    @pl.when(j == pl.num_programs(1) - 1)
    def _():
        for kv in range(2):
            h = h0_ref[kv] + h1_ref[kv, pl.ds(1, nsb), :]
            out = _dot(jax.nn.gelu(h).astype(BF16), w2_ref[kv])
            out_ref[0, :, kv * KV_W:(kv + 1) * KV_W] = out.astype(BF16)


def _page_specs(pg, pages_per_sample):
    def spec(i):
        return pl.BlockSpec((1, 2, KV_W, PAGE_SIZE),
                            lambda b, j, pt: (pt[b * pages_per_sample + j * pg + i], 0, 0, 0))
    return [spec(i) for i in range(pg)]


def _compress_paged(pages, pt_flat, wbd, pe_t, w2bd, *, nb, pages_per_sample, pg):
    nsb = pages_per_sample * PAGE_SIZE // CMP_STRIDE
    sbp = PAGE_SIZE // CMP_STRIDE
    i_out = jnp.arange(2 * PAGE_SIZE)
    s_i, pg_i, n_i = i_out // (2 * sbp), (i_out // sbp) % 2, i_out % sbp
    perm = (i_out[None, :] == (pg_i * PAGE_SIZE + n_i * CMP_STRIDE + s_i)[:, None]).astype(BF16)
    c3 = lambda shape: pl.BlockSpec(shape, lambda b, j, pt: (0,) * len(shape))
    grid_spec = pltpu.PrefetchScalarGridSpec(
        num_scalar_prefetch=1,
        grid=(nb, pages_per_sample // pg),
        in_specs=_page_specs(pg, pages_per_sample) + [c3(perm.shape), c3(wbd.shape), c3(pe_t.shape), c3(w2bd.shape)],
        out_specs=pl.BlockSpec((1, nsb, 2 * KV_W), lambda b, j, pt: (b, 0, 0)),
        scratch_shapes=[pltpu.VMEM((2, nsb, KV_W), F32), pltpu.VMEM((2, nsb + SUBLANES, KV_W), F32)],
    )
    return pl.pallas_call(
        functools.partial(_compress_paged_kernel, pg=pg, nsb=nsb),
        grid_spec=grid_spec,
        out_shape=jax.ShapeDtypeStruct((nb, nsb, 2 * KV_W), BF16),
        compiler_params=_params("parallel", "arbitrary"),
        name="nsa_compress_paged",
    )(pt_flat, *([pages] * pg), perm, wbd, pe_t, w2bd)


def _topk_mask_lanes(imp, n_top, ns_valid):
    lane = lax.broadcasted_iota(jnp.int32, imp.shape, 1)
    cnt = jnp.zeros(imp.shape, F32)
    for sp in range(ns_valid):
        col = imp[:, sp:sp + 1]
        cnt = cnt + jnp.where(lane > sp, jnp.where(col >= imp, 1.0, 0.0), jnp.where(col > imp, 1.0, 0.0))
    return jnp.where((cnt < n_top) & (lane < ns_valid), 1.0, 0.0)


def _dot_nt(a, b):
    return lax.dot_general(a, b, (((1,), (1,)), ((), ())), preferred_element_type=F32)


def _attn_sample_kernel(pt_ref, *refs, pg, past, nc, ns_valid, n_top):
    del pt_ref
    pages = refs[:pg]
    (q_ref, qr_ref, g_ref, kcvc_ref, ksn_ref, kwn_ref, win_ref, ov_ref, e_ref, gs_ref, o_ref,
     m_ref, l_ref, acc_ref, sel_ref, oc_ref, ow_ref) = refs[pg:]
    j = pl.program_id(1)
    ncp = kcvc_ref.shape[1]
    nsp = ov_ref.shape[1]
    wlen = win_ref.shape[3]
    row = lax.broadcasted_iota(jnp.int32, (N_HEADS, KV_W), 0)
    lane = lax.broadcasted_iota(jnp.int32, (N_HEADS, KV_W), 1)
    own = (lane // HEAD_DIM) == (row // GROUP)

    def spread(ref):
        q = ref[0]
        return jnp.where(own, jnp.concatenate([q] * N_KV_HEADS, axis=1), jnp.zeros((N_HEADS, KV_W), BF16))

    def update(state, s, vs):
        m, l, acc = state
        m_new = jnp.maximum(m, jnp.max(s, axis=1, keepdims=True))
        alpha = jnp.exp(m - m_new)
        p = jnp.exp(s - m_new)
        l = alpha * l + jnp.sum(p, axis=1, keepdims=True)
        pv = None
        for st, sz, v, feature_major in vs:
            pb = p[:, st:st + sz].astype(BF16)
            t = _dot_nt(pb, v) if feature_major else _dot(pb, v)
            pv = t if pv is None else pv + t
        return m_new, l, alpha * acc + pv

    def init():
        return (jnp.full((N_HEADS, 1), NEG, F32), jnp.zeros((N_HEADS, 1), F32), jnp.zeros((N_HEADS, KV_W), F32))

    def new_row_update(state, qbd, new_ref):
        r8 = lax.broadcasted_iota(jnp.int32, (SUBLANES, 2 * KV_W), 0)
        tile = jnp.where(r8 == 0, jnp.broadcast_to(new_ref[0], (SUBLANES, 2 * KV_W)), 0.0).astype(BF16)
        s = _dot_nt(qbd, tile[:, 0:KV_W])
        l8 = lax.broadcasted_iota(jnp.int32, (N_HEADS, SUBLANES), 1)
        s = jnp.where(l8 == 0, s, MASKED)
        return update(state, s, [(0, SUBLANES, tile[:, KV_W:2 * KV_W], False)])

    def put(state):
        m, l, acc = state
        m_ref[...] = jnp.broadcast_to(m, m_ref.shape)
        l_ref[...] = jnp.broadcast_to(l, l_ref.shape)
        acc_ref[...] = acc

    qrbd = spread(qr_ref)

    @pl.when(j == 0)
    def _():
        s = _dot_nt(spread(q_ref), kcvc_ref[0, :, 0:KV_W])
        cidx = lax.broadcasted_iota(jnp.int32, (1, ncp), 1)
        valid = ((cidx * CMP_STRIDE + (CMP_BLOCK - 1)) <= past) & (cidx < nc)
        sm = jnp.where(valid, s, NEG)
        mx = jnp.max(sm, axis=1, keepdims=True)
        e = jnp.where(valid, jnp.exp(sm - mx), 0.0)
        p = e / jnp.maximum(jnp.sum(e, axis=1, keepdims=True), 1e-20)
        oc_ref[...] = _dot(p.astype(BF16), kcvc_ref[0, :, KV_W:2 * KV_W])
        imp = _dot_f32rhs(_dot_f32lhs(gs_ref[...], p), ov_ref[...])
        sidx = lax.broadcasted_iota(jnp.int32, (1, nsp), 1)
        cur = past // SEL_BLOCK
        forced = (sidx == 0) | (sidx == cur) | (sidx == cur - 1)
        causal = (sidx * SEL_BLOCK) <= past
        imp = jnp.where(forced, FORCE, jnp.where(causal, imp, NEG))
        imp = jnp.where(sidx < ns_valid, imp, MASKED)
        sel_ref[...] = _topk_mask_lanes(imp, n_top, ns_valid)
        sw = _dot(qrbd, win_ref[0, 0].astype(BF16))
        wpos = past - wlen + lax.broadcasted_iota(jnp.int32, (1, wlen), 1)
        sw = jnp.where((wpos >= 0) & (past - wpos <= WINDOW), sw, MASKED)
        st = update(init(), sw, [(0, wlen, win_ref[0, 1].astype(BF16), True)])
        st = new_row_update(st, qrbd, kwn_ref)
        ow_ref[...] = st[2] * (1.0 / jnp.maximum(st[1], 1e-20))
        put(new_row_update(init(), qrbd, ksn_ref))

    halves = [pages[:pg // 2], pages[pg // 2:]] if pg > 1 else [pages]
    scores = [jnp.concatenate([_dot(qrbd, r[0, 0].astype(BF16)) for r in part], axis=1) for part in halves]
    mask = _dot(sel_ref[...].astype(BF16), e_ref[0]) > 0.5
    state = (m_ref[:, 0:1], l_ref[:, 0:1], acc_ref[...])
    k0 = 0
    for part, s in zip(halves, scores):
        width = len(part) * PAGE_SIZE
        s = jnp.where(mask[:, k0:k0 + width], s, MASKED)
        vs = [(i * PAGE_SIZE, PAGE_SIZE, r[0, 1].astype(BF16), True) for i, r in enumerate(part)]
        state = update(state, s, vs)
        k0 += width
    put(state)

    @pl.when(j == pl.num_programs(1) - 1)
    def _():
        g = g_ref[0]
        os_ = acc_ref[...] * (1.0 / jnp.maximum(l_ref[:, 0:1], 1e-20))
        o = g[:, 0:1] * oc_ref[...] + g[:, 1:2] * os_ + g[:, 2:3] * ow_ref[...]
        o = jnp.where(own, o, 0.0)
        out = o[:, 0:HEAD_DIM]
        for h in range(1, N_KV_HEADS):
            out = out + o[:, h * HEAD_DIM:(h + 1) * HEAD_DIM]
        o_ref[0] = out.astype(BF16)


def _attn_sample(pages, pt_flat, q3, qr3, g3, kcvc, ks_new, kw_new, win, ov, e_mat, gsum, *, nb, pages_per_sample,
                 pg, past, nc, ns_valid, n_top):
    nsp = ov.shape[1]
    per_b = lambda shape: pl.BlockSpec((1,) + shape, lambda b, j, pt: (b,) + (0,) * len(shape))
    const = lambda shape: pl.BlockSpec(shape, lambda b, j, pt: (0,) * len(shape))
    grid_spec = pltpu.PrefetchScalarGridSpec(
        num_scalar_prefetch=1,
        grid=(nb, pages_per_sample // pg),
        in_specs=_page_specs(pg, pages_per_sample) + [
            per_b((N_HEADS, HEAD_DIM)), per_b((N_HEADS, HEAD_DIM)), per_b((N_HEADS, 3)),
            per_b(kcvc.shape[1:]), per_b((1, 2 * KV_W)), per_b((1, 2 * KV_W)), per_b(win.shape[1:]),
            const(ov.shape), pl.BlockSpec((1,) + e_mat.shape[1:], lambda b, j, pt: (j, 0, 0)), const(gsum.shape)],
        out_specs=per_b((N_HEADS, HEAD_DIM)),
        scratch_shapes=[pltpu.VMEM((N_HEADS, LANES), F32), pltpu.VMEM((N_HEADS, LANES), F32),
                        pltpu.VMEM((N_HEADS, KV_W), F32), pltpu.VMEM((N_HEADS, nsp), F32),
                        pltpu.VMEM((N_HEADS, KV_W), F32), pltpu.VMEM((N_HEADS, KV_W), F32)],
    )
    return pl.pallas_call(
        functools.partial(_attn_sample_kernel, pg=pg, past=past, nc=nc, ns_valid=ns_valid, n_top=n_top),
        grid_spec=grid_spec,
        out_shape=jax.ShapeDtypeStruct((nb, N_HEADS, HEAD_DIM), BF16),
        compiler_params=_params("parallel", "arbitrary"),
        name="nsa_attn_sample",
    )(pt_flat, *([pages] * pg), q3, qr3, g3, kcvc, ks_new, kw_new, win, ov, e_mat, gsum)


def _rope_tables(pos):
    half = HEAD_DIM // 2
    inv = ROPE_THETA ** (-jnp.arange(half, dtype=F32) / half)
    ang = pos.astype(F32)[:, None] * inv[None, :]
    cos = jnp.cos(ang)
    sin = jnp.sin(ang)
    reps = LANES // HEAD_DIM
    return (jnp.tile(jnp.concatenate([cos, cos], axis=1), (1, reps)),
            jnp.tile(jnp.concatenate([-sin, sin], axis=1), (1, reps)))


def _compress_weights(cmp_w1, cmp_w2, cmp_pe):
    ratio = CMP_BLOCK // CMP_STRIDE
    eye = jnp.eye(N_KV_HEADS, dtype=F32)
    w1r = cmp_w1.reshape(2, ratio, CMP_STRIDE, HEAD_DIM, HEAD_DIM)
    wbd = jnp.einsum("krsde,hg->krshdge", w1r, eye).reshape(2 * ratio * CMP_STRIDE, KV_W, KV_W).astype(BF16)
    w2bd = jnp.einsum("kef,hg->khegf", cmp_w2, eye).reshape(2, KV_W, KV_W).astype(BF16)
    pe_r = cmp_pe.reshape(2, ratio, CMP_STRIDE, HEAD_DIM).astype(F32)
    pe_w = jnp.einsum("krsd,krsde->kre", pe_r, w1r.astype(F32), precision=lax.Precision.HIGHEST)
    pe_t = jnp.tile(pe_w.reshape(2 * ratio, HEAD_DIM), (1, N_KV_HEADS))
    return wbd, pe_t, w2bd


def _overlap(nc, ncp, nsel, nsp):
    c_start = jnp.arange(ncp, dtype=jnp.int32)[:, None] * CMP_STRIDE
    s_start = jnp.arange(nsp, dtype=jnp.int32)[None, :] * SEL_BLOCK
    ov = (c_start < s_start + SEL_BLOCK) & (c_start + CMP_BLOCK > s_start)
    ov = ov & (jnp.arange(ncp)[:, None] < nc) & (jnp.arange(nsp)[None, :] < nsel)
    return ov.astype(BF16)


def _pick(*cands):
    return next(c for c in cands if c)


def _tile(n, pref):
    t = min(n, pref)
    while n % t:
        t //= 2
    return t


def kernel(x_prompt, x_sample, cache_kv_cmp, cache_kv_sel, cache_kv_win, state_ssm, page_table, norm_g, mlp_w1,
           mlp_w2, nsa_w_in, nsa_w_o, nsa_cmp_w1, nsa_cmp_w2, nsa_cmp_pe, s5_a_re, s5_a_im, s5_log_dt, s5_b_re,
           s5_b_im, s5_c_re, s5_c_im, s5_d, s5_w_glu, s5_b_glu):
    b, t, d = x_prompt.shape
    nb = x_sample.shape[0]
    pages_per_sample = page_table.shape[1]
    past = pages_per_sample * PAGE_SIZE
    rows_p = b * t
    g = norm_g.reshape(norm_g.shape[0], 4, 1, d)

    w_in = nsa_w_in[0]
    w_main = w_in[:, :Q_W + 6 * KV_W].astype(BF16)
    w_gate = jnp.pad(w_in[:, Q_W + 6 * KV_W:], ((0, 0), (0, LANES - 3 * N_HEADS))).astype(BF16)
    w_o = nsa_w_o[0].astype(BF16)
    wbd, pe_t, w2bd = _compress_weights(nsa_cmp_w1[0], nsa_cmp_w2[0], nsa_cmp_pe[0])
    w1 = mlp_w1.astype(BF16)
    w2 = mlp_w2.astype(BF16)
    w_glu = s5_w_glu[0].astype(BF16)
    ops = _s5_operators(s5_a_re[0], s5_a_im[0], s5_log_dt[0], s5_b_re[0], s5_b_im[0], s5_c_re[0], s5_c_im[0])
    d_skip = s5_d[0].reshape(1, d)
    b_glu = s5_b_glu[0].reshape(1, d)

    tm = _tile(rows_p, 512)
    ff_chunk = _tile(mlp_w1.shape[2], 1024)

    xp = x_prompt.reshape(rows_p, d)
    cos_p, sin_p = _rope_tables(jnp.arange(t, dtype=jnp.int32))
    tq = next(c for c in (2 * LANES, LANES) if t % c == 0 and WINDOW % c == 0)
    (qT, qrT, gT, kvc, kvcT, kvsT, kvwT, ksb, kwb, vsT, vwT) = _inproj(
        xp, g[0, 0], w_main, w_gate, cos_p, sin_p, tm=_tile(t, 512), pos_blocks=t // _tile(t, 512), transposed=True,
        key_chunk=tq)
    nsb_p = t // CMP_STRIDE
    nc_p = nsb_p - CMP_BLOCK // CMP_STRIDE + 1
    nsel_p = t // SEL_BLOCK
    kc, vcT = _compress_prompt(kvc.reshape(b, t, 2 * KV_W), wbd, pe_t, w2bd)
    ovT = _overlap(nc_p, nsb_p, nsel_p, nsel_p).T
    oT = _attn_prompt(qT, qrT, gT, kc, vcT, ksb, vsT, kwb, vwT, ovT, batch=b, seq=t, tq=tq, nc=nc_p,
                      n_top=min(TOP_N, nsel_p))
    hp, xm = _oproj(oT, w_o, xp, g[0, 1], g[0, 2], tm=tm, transposed=True)
    hp, xn1 = _mlp(xm, hp, w1[0], w2[0], g[0, 3], g[1, 0], tm=tm, ff_chunk=ff_chunk, next_norm=True)

    pw = SSM_UNIT * SSM_GROUP
    npair = d // pw
    y3, hfin = _s5_seq(xn1.reshape(b, t, d), ops)
    yc = y3.reshape(rows_p, d)
    hp, xm = _s5_out(yc, xn1, d_skip, w_glu, b_glu, hp, g[1, 1], g[1, 2], tm=tm)
    (hp,) = _mlp(xm, hp, w1[1], w2[1], g[1, 3], g[1, 3], tm=tm, ff_chunk=ff_chunk, next_norm=False)
    ssm_p = hfin.reshape(npair, 2, b, SSM_UNIT, SSM_STATE).transpose(2, 1, 0, 3, 4)
    ssm_p = ssm_p.reshape(b, 2, d // SSM_GROUP, SSM_STATE)

    xs = x_sample.reshape(nb, d)
    cos_s, sin_s = _rope_tables(jnp.full((nb,), past, dtype=jnp.int32))
    q_s, qr_s, gates_s, kvc_s, kvs_s, kvw_s = _inproj(
        xs, g[0, 0], w_main, w_gate, cos_s, sin_s, tm=nb, pos_blocks=1, transposed=False)
    pt_flat = page_table.reshape(-1).astype(jnp.int32)
    pg = _tile(pages_per_sample, PAGE_GROUP)
    n_pool = cache_kv_cmp.shape[1]
    feature_major = lambda c, n, s: c.transpose(0, 2, 3, 4, 1).reshape(n, 2, KV_W, s)
    cmp_pages = feature_major(cache_kv_cmp[0], n_pool, PAGE_SIZE)
    sel_pages = feature_major(cache_kv_sel[0], n_pool, PAGE_SIZE)
    kcvc = _compress_paged(cmp_pages, pt_flat, wbd, pe_t, w2bd, nb=nb, pages_per_sample=pages_per_sample, pg=pg)
    l_all = past + 1
    nsb_s = l_all // CMP_STRIDE
    nc_s = nsb_s - CMP_BLOCK // CMP_STRIDE + 1
    nsel_s = -(-l_all // SEL_BLOCK)
    nsp = -(-nsel_s // LANES) * LANES
    ov_s = _overlap(nc_s, past // CMP_STRIDE, nsel_s, nsp)
    keys_per_step = pg * PAGE_SIZE
    key_blk = (jnp.arange(past, dtype=jnp.int32) // SEL_BLOCK).reshape(past // keys_per_step, 1, keys_per_step)
    e_mat = (jnp.arange(nsp, dtype=jnp.int32)[None, :, None] == key_blk).astype(BF16)
    hh = jnp.arange(N_HEADS)
    gsum = ((hh[:, None] // GROUP) == (hh[None, :] // GROUP)).astype(BF16)
    win = feature_major(cache_kv_win[0], nb, WINDOW)
    o_s = _attn_sample(sel_pages, pt_flat, q_s.reshape(nb, N_HEADS, HEAD_DIM), qr_s.reshape(nb, N_HEADS, HEAD_DIM),
                       gates_s[:, :3 * N_HEADS].reshape(nb, N_HEADS, 3), kcvc, kvs_s.reshape(nb, 1, 2 * KV_W),
                       kvw_s.reshape(nb, 1, 2 * KV_W), win, ov_s, e_mat, gsum, nb=nb,
                       pages_per_sample=pages_per_sample, pg=pg, past=past, nc=nc_s, ns_valid=nsel_s,
                       n_top=min(TOP_N, nsel_s))
    hs, xm_s = _oproj(o_s.reshape(nb, Q_W), w_o, xs, g[0, 1], g[0, 2], tm=nb, transposed=False)
    hs, xn1_s = _mlp(xm_s, hs, w1[0], w2[0], g[0, 3], g[1, 0], tm=nb, ff_chunk=ff_chunk, next_norm=True)

    u2_s = xn1_s.astype(BF16).reshape(nb, npair, pw).transpose(1, 0, 2)
    st = state_ssm[0].reshape(nb, 2, npair, SSM_UNIT * SSM_STATE).transpose(1, 2, 0, 3)
    y2_s, hr_s, hi_s = _s5_step(u2_s, st[0], st[1], ops)
    yc_s = y2_s.transpose(1, 0, 2).reshape(nb, d)
    hs, xm_s = _s5_out(yc_s, xn1_s, d_skip, w_glu, b_glu, hs, g[1, 1], g[1, 2], tm=nb)
    (hs,) = _mlp(xm_s, hs, w1[1], w2[1], g[1, 3], g[1, 3], tm=nb, ff_chunk=ff_chunk, next_norm=False)
    ssm_s = jnp.stack([hr_s, hi_s], axis=0).transpose(2, 0, 1, 3).reshape(nb, 2, d // SSM_GROUP, SSM_STATE)

    kv5 = lambda a, n, s: a.reshape(1, n, s, 2, N_KV_HEADS, HEAD_DIM)
    from_fm = lambda a, n, s: a.reshape(n, 2, N_KV_HEADS, HEAD_DIM, s).transpose(0, 4, 1, 2, 3)[None]
    win_s = jnp.concatenate([win[..., 1:], kvw_s.reshape(nb, 2, KV_W, 1)], axis=-1)
    return (hp.reshape(b, t, d), hs.reshape(nb, 1, d),
            from_fm(kvcT, b, t), kv5(kvc_s, nb, 1), from_fm(kvsT, b, t), kv5(kvs_s, nb, 1),
            from_fm(kvwT[:, :, t - WINDOW:], b, WINDOW), from_fm(win_s, nb, WINDOW), ssm_p[None], ssm_s[None])
```

```python
import functools

import jax
import jax.numpy as jnp
from jax import lax
from jax.experimental import pallas as pl
from jax.experimental.pallas import tpu as pltpu

N_HEADS = 16
HEAD_DIM = 64
N_KV_HEADS = 4
GROUP = N_HEADS // N_KV_HEADS
CMP_BLOCK = 32
CMP_STRIDE = 16
SEL_BLOCK = 64
TOP_N = 16
WINDOW = 512
ROPE_THETA = 10000.0
PAGE_SIZE = 128
SSM_GROUP = 16
SSM_STATE = 64
SSM_CHUNK = 8
SSM_UNIT = 8
EPS = 1e-6
NEG = -1e30
FORCE = 1e9
MASKED = -1.5e38
LOG2E = 1.4426950408889634
V_ROWS = HEAD_DIM + 16
Q_W = N_HEADS * HEAD_DIM
KV_W = N_KV_HEADS * HEAD_DIM
LANES = 128
SUBLANES = 8
PAGE_GROUP = 16
SAMPLES_PER_STEP = 2
S5_ROW_BLOCK = 1024
S5_TIME_BLOCK = 2048
VMEM_LIMIT = 56 * 1024 * 1024

F32 = jnp.float32
BF16 = jnp.bfloat16


def _params(*sem):
    return pltpu.CompilerParams(dimension_semantics=sem, vmem_limit_bytes=VMEM_LIMIT)


def _full(shape):
    zeros = (0,) * len(shape)
    return pl.BlockSpec(shape, lambda *_: zeros)


def _rms(x, g):
    ms = jnp.mean(x * x, axis=-1, keepdims=True)
    return x * lax.rsqrt(ms + EPS) * g


def _dot(a, b):
    return jnp.dot(a, b, preferred_element_type=F32)


def _dot_f32lhs(w, x):
    hi = x.astype(BF16)
    r1 = x - hi.astype(F32)
    mid = r1.astype(BF16)
    lo = (r1 - mid.astype(F32)).astype(BF16)
    return _dot(w, hi) + _dot(w, mid) + _dot(w, lo)


def _dot_f32rhs(x, w):
    hi = x.astype(BF16)
    r1 = x - hi.astype(F32)
    mid = r1.astype(BF16)
    lo = (r1 - mid.astype(F32)).astype(BF16)
    return _dot(hi, w) + _dot(mid, w) + _dot(lo, w)


def _rope_nat(x, cos, sin):
    half = HEAD_DIM // 2
    lane = lax.broadcasted_iota(jnp.int32, (1, LANES), 1)
    first = (lane % HEAD_DIM) < half
    outs = []
    for c in range(x.shape[1] // LANES):
        xc = x[:, c * LANES:(c + 1) * LANES]
        rot = jnp.where(first, pltpu.roll(xc, LANES - half, 1), pltpu.roll(xc, half, 1))
        outs.append(xc * cos + rot * sin)
    return jnp.concatenate(outs, axis=1)


def _inproj_kernel(x_ref, g_ref, w_ref, wg_ref, cos_ref, sin_ref, *outs, transposed, key_chunk):
    xb = _rms(x_ref[...], g_ref[...]).astype(BF16)
    cos = cos_ref[...]
    sin = sin_ref[...]
    scale = HEAD_DIM ** -0.5
    q = _dot(xb, w_ref[:, 0:Q_W])
    qr = _rope_nat(q, cos, sin)
    kv = _dot(xb, w_ref[:, Q_W:Q_W + 6 * KV_W])
    gates = jax.nn.sigmoid(_dot(xb, wg_ref[...]))
    k_s = _rope_nat(kv[:, 2 * KV_W:3 * KV_W], cos, sin)
    v_s = kv[:, 3 * KV_W:4 * KV_W]
    k_w = _rope_nat(kv[:, 4 * KV_W:5 * KV_W], cos, sin)
    v_w = kv[:, 5 * KV_W:6 * KV_W]
    kvc_ref = outs[3]
    kvc_ref[...] = kv[:, 0:2 * KV_W]
    if transposed:
        qT_ref, qrT_ref, gT_ref, _, kvcT_ref, kvsT_ref, kvwT_ref, ksb_ref, kwb_ref, vsT_ref, vwT_ref = outs
        qT_ref[...] = (q * (scale * LOG2E)).T.astype(BF16)
        qrT_ref[...] = (qr * (scale * LOG2E)).T.astype(BF16)
        gT_ref[...] = gates.T
        tm = x_ref.shape[0]
        rowi = lax.broadcasted_iota(jnp.int32, (tm, LANES), 0)
        lanei = lax.broadcasted_iota(jnp.int32, (tm, LANES), 1)
        blk = lax.shift_right_logical(rowi & (key_chunk - 1), SEL_BLOCK.bit_length() - 1)
        extra = lanei - HEAD_DIM
        nblk = key_chunk // SEL_BLOCK
        aug = jnp.where((extra == blk) | ((extra >= nblk) & (extra < nblk + N_CONST_LANES)), 1.0, 0.0)
        for c in range(KV_W // LANES):
            for k_nat, k_ref in ((k_s, ksb_ref), (k_w, kwb_ref)):
                pair = k_nat[:, c * LANES:(c + 1) * LANES]
                k_ref[2 * c] = jnp.where(lanei < HEAD_DIM, pair, aug).astype(BF16)
                k_ref[2 * c + 1] = jnp.where(lanei < HEAD_DIM, pltpu.roll(pair, HEAD_DIM, 1), aug).astype(BF16)
        v_sT = v_s.T
        v_wT = v_w.T
        ones_rows = jnp.where(lax.broadcasted_iota(jnp.int32, (V_ROWS - HEAD_DIM, tm), 0) == 0, 1.0, 0.0)
        for h in range(N_KV_HEADS):
            for vT, v_ref in ((v_sT, vsT_ref), (v_wT, vwT_ref)):
                v_ref[h * V_ROWS:(h + 1) * V_ROWS, :] = jnp.concatenate(
                    [vT[h * HEAD_DIM:(h + 1) * HEAD_DIM, :], ones_rows], axis=0).astype(BF16)
        kvcT_ref[0] = kv[:, 0:2 * KV_W].T
        kvsT_ref[0, 0:KV_W, :] = k_s.T
        kvsT_ref[0, KV_W:2 * KV_W, :] = v_sT
        kvwT_ref[0, 0:KV_W, :] = k_w.T
        kvwT_ref[0, KV_W:2 * KV_W, :] = v_wT
    else:
        q_ref, qr_ref, gt_ref, _, kvs_ref, kvw_ref = outs
        q_ref[...] = (q * scale).astype(BF16)
        qr_ref[...] = (qr * scale).astype(BF16)
        gt_ref[...] = gates
        kvs_ref[:, 0:KV_W] = k_s
        kvs_ref[:, KV_W:2 * KV_W] = v_s
        kvw_ref[:, 0:KV_W] = k_w
        kvw_ref[:, KV_W:2 * KV_W] = v_w


def _inproj(x, g, w_main, w_gate, cos_t, sin_t, *, tm, pos_blocks, transposed, key_chunk=LANES):
    rows, d = x.shape
    assert tm % key_chunk == 0 or not transposed
    n = rows // tm
    row_blk = lambda w: pl.BlockSpec((tm, w), lambda i: (i, 0))
    col_blk = lambda h: pl.BlockSpec((h, tm), lambda i: (0, i))
    tab = pl.BlockSpec((tm, LANES), lambda i: (i % pos_blocks, 0))
    kv_nat = jax.ShapeDtypeStruct((rows, 2 * KV_W), F32)
    if transposed:
        seqs = rows // (pos_blocks * tm)
        kh = pl.BlockSpec((N_KV_HEADS, tm, LANES), lambda i: (0, i, 0))
        kvT = pl.BlockSpec((1, 2 * KV_W, tm), lambda i: (i // pos_blocks, 0, i % pos_blocks))
        out_shape = ([jax.ShapeDtypeStruct((Q_W, rows), BF16)] * 2 + [jax.ShapeDtypeStruct((LANES, rows), F32)]
                     + [kv_nat] + [jax.ShapeDtypeStruct((seqs, 2 * KV_W, pos_blocks * tm), F32)] * 3
                     + [jax.ShapeDtypeStruct((N_KV_HEADS, rows, LANES), BF16)] * 2
                     + [jax.ShapeDtypeStruct((N_KV_HEADS * V_ROWS, rows), BF16)] * 2)
        out_specs = ([col_blk(Q_W)] * 2 + [col_blk(LANES)] + [row_blk(2 * KV_W)] + [kvT] * 3 + [kh] * 2
                     + [col_blk(N_KV_HEADS * V_ROWS)] * 2)
    else:
        out_shape = ([jax.ShapeDtypeStruct((rows, Q_W), BF16)] * 2 + [jax.ShapeDtypeStruct((rows, LANES), F32)]
                     + [kv_nat] * 3)
        out_specs = [row_blk(Q_W)] * 2 + [row_blk(LANES)] + [row_blk(2 * KV_W)] * 3
    return pl.pallas_call(
        functools.partial(_inproj_kernel, transposed=transposed, key_chunk=key_chunk),
        grid=(n,),
        in_specs=[row_blk(d), _full((1, d)), _full(w_main.shape), _full(w_gate.shape), tab, tab],
        out_specs=out_specs,
        out_shape=out_shape,
        compiler_params=_params("parallel"),
        name="nsa_inproj",
    )(x, g, w_main, w_gate, cos_t, sin_t)


_KV_CHUNKS = 2 * KV_W // LANES


def _compress_half(load_rows, kv, nrows, wbd_ref, pe_ref):
    ratio = CMP_BLOCK // CMP_STRIDE
    accs = [jnp.broadcast_to(pe_ref[kv * ratio + r:kv * ratio + r + 1, :], (nrows, KV_W)) for r in range(ratio)]
    for s in range(CMP_STRIDE):
        lhs = load_rows(s, kv).astype(BF16)
        for r in range(ratio):
            accs[r] = accs[r] + _dot(lhs, wbd_ref[(kv * ratio + r) * CMP_STRIDE + s])
    return accs


def _compress_prompt_kernel(*refs, nsb):
    x_refs = refs[:_KV_CHUNKS]
    wbd_ref, pe_ref, w2_ref, kc_ref, vcT_ref, sh_ref = refs[_KV_CHUNKS:]

    def load_rows(s, kv):
        per_half = _KV_CHUNKS // 2
        return jnp.concatenate([x_refs[kv * per_half + c][0, pl.ds(s, nsb, stride=CMP_STRIDE), :]
                                for c in range(per_half)], axis=1)

    sh_ref[nsb:nsb + SUBLANES, :] = jnp.zeros((SUBLANES, KV_W), F32)
    for kv in range(2):
        pr0, pr1 = _compress_half(load_rows, kv, nsb, wbd_ref, pe_ref)
        sh_ref[0:nsb, :] = pr1
        h = pr0 + sh_ref[pl.ds(1, nsb), :]
        out = _dot(jax.nn.gelu(h).astype(BF16), w2_ref[kv])
        if kv == 0:
            for hh in range(N_KV_HEADS):
                kc_ref[0, hh] = out[:, hh * HEAD_DIM:(hh + 1) * HEAD_DIM].astype(BF16)
        else:
            vcT_ref[0] = out.T.astype(BF16)


def _compress_prompt(kvc3, wbd, pe_t, w2bd):
    b, t, _ = kvc3.shape
    nsb = t // CMP_STRIDE
    return pl.pallas_call(
        functools.partial(_compress_prompt_kernel, nsb=nsb),
        grid=(b,),
        in_specs=[pl.BlockSpec((1, t, LANES), lambda i, c=c: (i, 0, c)) for c in range(_KV_CHUNKS)]
        + [_full(wbd.shape), _full(pe_t.shape), _full(w2bd.shape)],
        out_specs=[pl.BlockSpec((1, N_KV_HEADS, nsb, HEAD_DIM), lambda i: (i, 0, 0, 0)),
                   pl.BlockSpec((1, KV_W, nsb), lambda i: (i, 0, 0))],
        out_shape=[jax.ShapeDtypeStruct((b, N_KV_HEADS, nsb, HEAD_DIM), BF16),
                   jax.ShapeDtypeStruct((b, KV_W, nsb), BF16)],
        scratch_shapes=[pltpu.VMEM((nsb + SUBLANES, KV_W), F32)],
        compiler_params=_params("parallel"),
        name="nsa_compress_prompt",
    )(*([kvc3] * _KV_CHUNKS), wbd, pe_t, w2bd)


def _topk_mask_T(imp, n_top):
    ns, w = imp.shape
    nblk = ns // SUBLANES
    blocks = [imp[r * SUBLANES:(r + 1) * SUBLANES, :] for r in range(nblk)]
    cnts = [jnp.zeros((SUBLANES, w), F32) for _ in range(nblk)]
    sub = lax.broadcasted_iota(jnp.int32, (SUBLANES, w), 0)
    for sp in range(ns):
        row = blocks[sp // SUBLANES][sp % SUBLANES:sp % SUBLANES + 1, :]
        for r in range(nblk):
            blk = blocks[r]
            if sp < r * SUBLANES:
                beats = jnp.where(row >= blk, 1.0, 0.0)
            elif sp >= (r + 1) * SUBLANES:
                beats = jnp.where(row > blk, 1.0, 0.0)
            else:
                beats = jnp.where(sub > (sp - r * SUBLANES), jnp.where(row >= blk, 1.0, 0.0),
                                  jnp.where(row > blk, 1.0, 0.0))
            cnts[r] = cnts[r] + beats
    return jnp.concatenate([jnp.where(c < n_top, 1.0, 0.0) for c in cnts], axis=0)


def _online_chunks(states, k_cs, vT_cs, qTs, bias):
    scores = [_dot(k_c, qT) for k_c, qT in zip(k_cs, qTs)]
    mids = []
    for (m, _), s in zip(states, scores):
        if bias is not None:
            s = s + bias
        m_new = jnp.maximum(m, jnp.max(s, axis=0, keepdims=True))
        mids.append((m_new, jnp.exp2(m - m_new), jnp.exp2(s - m_new).astype(BF16)))
    return tuple((m_new, alpha * acc + _dot(vT_c, p))
                 for (m_new, alpha, p), (_, acc), vT_c in zip(mids, states, vT_cs))


def _softmax_init(w):
    return (jnp.full((1, w), NEG, F32), jnp.zeros((V_ROWS, w), F32))


def _softmax_finish(carry):
    _, acc = carry
    return acc[0:HEAD_DIM, :] * (1.0 / jnp.maximum(acc[HEAD_DIM:HEAD_DIM + 1, :], 1e-20))


LAZY_LOG2_MAX = 60.0
N_CONST_LANES = 4


def _split3(x):
    hi = x.astype(BF16).astype(F32)
    r = x - hi
    mid = r.astype(BF16).astype(F32)
    return hi, mid, (r - mid).astype(BF16).astype(F32)


def _rescaling_chunks(m_ref, acc_ref, k_cs, vT_cs, qTs, base_tiles, bias):
    nh = len(k_cs)
    rows, w = base_tiles[0].shape
    zero_rows = jnp.zeros((LANES - HEAD_DIM - rows, w), BF16)
    new = _online_chunks(tuple((m_ref[h], acc_ref[h]) for h in range(nh)), k_cs, vT_cs,
                         [jnp.concatenate([qTs[h], base_tiles[h].astype(BF16), zero_rows], axis=0) for h in range(nh)],
                         bias)
    for h in range(nh):
        m_ref[h] = new[h][0]
        acc_ref[h] = new[h][1]


def _lazy_chunks(m_ref, acc_ref, k_cs, vT_cs, qTs, base_tiles, bias, ref_row):
    nh = len(k_cs)
    rows = base_tiles[0].shape[0]
    w = base_tiles[0].shape[1]
    rowt = lax.broadcasted_iota(jnp.int32, (rows, w), 0)
    zero_rows = jnp.zeros((LANES - HEAD_DIM - rows, w), BF16)

    def queries(h, with_reference):
        tile = base_tiles[h]
        if with_reference:
            hi, mid, lo = _split3(-m_ref[h])
            tile = jnp.where(rowt == ref_row, hi, jnp.where(rowt == ref_row + 1, mid,
                                                            jnp.where(rowt == ref_row + 2, lo, tile)))
        return jnp.concatenate([qTs[h], tile.astype(BF16), zero_rows], axis=0)

    scores = [_dot(k_cs[h], queries(h, True)) for h in range(nh)]
    peak = None
    probs = []
    for s in scores:
        if bias is not None:
            s = s + bias
        top = jnp.max(s, axis=0, keepdims=True)
        peak = top if peak is None else jnp.maximum(peak, top)
        probs.append(jnp.exp2(s).astype(BF16))
    pvs = [_dot(vT_cs[h], probs[h]) for h in range(nh)]
    in_range = jnp.max(peak) <= LAZY_LOG2_MAX

    @pl.when(in_range)
    def _():
        for h in range(nh):
            acc_ref[h] = acc_ref[h] + pvs[h]

    @pl.when(jnp.logical_not(in_range))
    def _():
        _rescaling_chunks(m_ref, acc_ref, k_cs, vT_cs, qTs, base_tiles, bias)


def _attn_prompt_kernel(qT_ref, qrT_ref, gT_ref, kc_ref, vcT_ref, ks_ref, vsT_ref, kw_ref, vwT_ref, ovT_ref,
                        o_ref, selb_ref, oc_ref, ow_ref, m_ref, acc_ref, *, tq, nc, n_top):
    ck = tq
    i = pl.program_id(1)
    t0 = i * tq
    qpos = t0 + lax.broadcasted_iota(jnp.int32, (1, tq), 1)
    ncp = kc_ref.shape[2]
    ns = ovT_ref.shape[0]
    w = GROUP * tq
    bpc = ck // SEL_BLOCK
    sel_shift = SEL_BLOCK.bit_length() - 1
    kvhs = range(N_KV_HEADS)
    heads = [[kvh * GROUP + g for g in range(GROUP)] for kvh in kvhs]
    rows = [pl.ds(kvh * HEAD_DIM, HEAD_DIM) for kvh in kvhs]
    vrows = [pl.ds(kvh * V_ROWS, V_ROWS) for kvh in kvhs]
    qrT = [jnp.concatenate([qrT_ref[h * HEAD_DIM:(h + 1) * HEAD_DIM, :] for h in heads[kvh]], axis=1)
           for kvh in kvhs]
    kl = lax.broadcasted_iota(jnp.int32, (ck, tq), 0)
    ql = lax.broadcasted_iota(jnp.int32, (ck, tq), 1)
    tile4 = lambda b: jnp.concatenate([b] * GROUP, axis=1)
    key_le_query = tile4(jnp.where(kl <= ql, 0.0, MASKED))
    key_ge_query = tile4(jnp.where(kl >= ql, 0.0, MASKED))
    bias_rows = selb_ref.shape[2]

    def reset_softmax():
        for kvh in kvhs:
            m_ref[kvh] = jnp.full((1, w), NEG, F32)
            acc_ref[kvh] = jnp.zeros((V_ROWS, w), F32)

    cidx = lax.broadcasted_iota(jnp.int32, (ncp, 1), 0)
    valid = ((cidx * CMP_STRIDE + (CMP_BLOCK - 1)) <= qpos) & (cidx < nc)
    sidx = lax.broadcasted_iota(jnp.int32, (ns, 1), 0)
    cur = lax.shift_right_logical(qpos, sel_shift)
    forced = (sidx == 0) | (sidx == cur) | (sidx == cur - 1)
    causal = (sidx * SEL_BLOCK) <= qpos
    cmp_scores = [
        _dot(kc_ref[0, kvh], jnp.concatenate([qT_ref[h * HEAD_DIM:(h + 1) * HEAD_DIM, :] for h in heads[kvh]], axis=1))
        for kvh in kvhs]
    for kvh in kvhs:
        s = cmp_scores[kvh]
        probs = []
        for g in range(GROUP):
            sm = jnp.where(valid, s[:, g * tq:(g + 1) * tq], NEG)
            mx = jnp.max(sm, axis=0, keepdims=True)
            e = jnp.where(valid, jnp.exp2(sm - mx), 0.0)
            den = jnp.maximum(jnp.sum(e, axis=0, keepdims=True), 1e-20)
            probs.append(e / den)
        oc_ref[kvh] = _dot(vcT_ref[0, rows[kvh], :], jnp.concatenate(probs, axis=1).astype(BF16))
        psum = probs[0]
        for g in range(1, GROUP):
            psum = psum + probs[g]
        imp = _dot_f32lhs(ovT_ref[...], psum)
        imp = jnp.where(forced, FORCE, jnp.where(causal, imp, NEG))
        selb = (_topk_mask_T(imp, n_top) - 1.0) * (-MASKED)
        fill = jnp.zeros((bias_rows - bpc, tq), F32)
        for c in range(ns // bpc):
            selb_ref[kvh, c] = jnp.concatenate([selb[c * bpc:(c + 1) * bpc, :], fill], axis=0)

    n_back = WINDOW // ck
    reset_softmax()
    rowb = lax.broadcasted_iota(jnp.int32, (bias_rows, w), 0)
    for r in range(n_back + 1):
        a = i - n_back + r
        kst = pl.multiple_of(jnp.maximum(a, 0) * ck, ck)
        skip = jnp.where(a < 0, MASKED, 0.0)
        tile = jnp.where(rowb == bpc, skip, 0.0)
        bias = key_le_query if r == n_back else (key_ge_query if r == 0 else None)
        k_cs = [kw_ref[kvh, pl.ds(kst, ck), :] for kvh in kvhs]
        vT_cs = [vwT_ref[vrows[kvh], pl.ds(kst, ck)] for kvh in kvhs]
        if r == 0:
            _rescaling_chunks(m_ref, acc_ref, k_cs, vT_cs, qrT, [tile] * N_KV_HEADS, bias)
        else:
            _lazy_chunks(m_ref, acc_ref, k_cs, vT_cs, qrT, [tile] * N_KV_HEADS, bias, bpc + 1)
    for kvh in kvhs:
        ow_ref[kvh] = _softmax_finish((None, acc_ref[kvh]))

    def chunk_step(c, diagonal, lazy=True):
        kst = c * ck if isinstance(c, int) else pl.multiple_of(c * ck, ck)
        args = (m_ref, acc_ref, [ks_ref[kvh, pl.ds(kst, ck), :] for kvh in kvhs],
                [vsT_ref[vrows[kvh], pl.ds(kst, ck)] for kvh in kvhs], qrT,
                [tile4(selb_ref[kvh, c]) for kvh in kvhs], key_le_query if diagonal else None)
        if lazy:
            _lazy_chunks(*args, bpc + 1)
        else:
            _rescaling_chunks(*args)

    reset_softmax()

    @pl.when(i > 0)
    def _():
        chunk_step(0, False, lazy=False)

    def loop_body(c, carry):
        chunk_step(c, False)
        return carry

    lax.fori_loop(1, i, loop_body, 0)
    chunk_step(i, True)

    for kvh in kvhs:
        def gate_row(j, kvh=kvh):
            return jnp.concatenate([gT_ref[h * 3 + j:h * 3 + j + 1, :] for h in heads[kvh]], axis=1)

        oT = (gate_row(0) * oc_ref[kvh] + gate_row(1) * _softmax_finish((None, acc_ref[kvh]))
              + gate_row(2) * ow_ref[kvh])
        for g, h in enumerate(heads[kvh]):
            o_ref[h * HEAD_DIM:(h + 1) * HEAD_DIM, :] = oT[:, g * tq:(g + 1) * tq].astype(BF16)


def _attn_prompt(qT, qrT, gT, kc, vcT, ksb, vsT, kwb, vwT, ovT, *, batch, seq, tq, nc, n_top):
    nq = seq // tq
    nsb = kc.shape[2]
    ns = ovT.shape[0]
    col = lambda h: pl.BlockSpec((h, tq), lambda b, i: (0, b * nq + i))
    kh = pl.BlockSpec((N_KV_HEADS, seq, LANES), lambda b, i: (0, b, 0))
    vt = pl.BlockSpec((N_KV_HEADS * V_ROWS, seq), lambda b, i: (0, b))
    bf16_sublanes = 2 * SUBLANES
    return pl.pallas_call(
        functools.partial(_attn_prompt_kernel, tq=tq, nc=nc, n_top=n_top),
        grid=(batch, nq),
        in_specs=[col(Q_W), col(Q_W), col(LANES),
                  pl.BlockSpec((1, N_KV_HEADS, nsb, HEAD_DIM), lambda b, i: (b, 0, 0, 0)),
                  pl.BlockSpec((1, KV_W, nsb), lambda b, i: (b, 0, 0)),
                  kh, vt, kh, vt, _full(ovT.shape)],
        out_specs=col(Q_W),
        out_shape=jax.ShapeDtypeStruct((Q_W, batch * seq), BF16),
        scratch_shapes=[pltpu.VMEM((N_KV_HEADS, ns * SEL_BLOCK // tq, bf16_sublanes, tq), F32),
                        pltpu.VMEM((N_KV_HEADS, HEAD_DIM, GROUP * tq), F32),
                        pltpu.VMEM((N_KV_HEADS, HEAD_DIM, GROUP * tq), F32),
                        pltpu.VMEM((N_KV_HEADS, 1, GROUP * tq), F32),
                        pltpu.VMEM((N_KV_HEADS, V_ROWS, GROUP * tq), F32)],
        compiler_params=_params("parallel", "arbitrary"),
        name="nsa_attn_prompt",
    )(qT, qrT, gT, kc, vcT, ksb, vsT, kwb, vwT, ovT)


def _resident(shape):
    zeros = (0,) * len(shape)
    return pl.BlockSpec(shape, lambda *_: zeros, pipeline_mode=pl.Buffered(1))


def _layer_tail_kernel(*refs, mixer, transposed, ff_chunk, next_norm):
    if mixer == "nsa":
        o_ref, wo_ref, x_ref, g1_ref, g2_ref = refs[:5]
        rest = refs[5:]
        if transposed:
            y = lax.dot_general(o_ref[...], wo_ref[...], (((0,), (0,)), ((), ())), preferred_element_type=F32)
        else:
            y = _dot(o_ref[...], wo_ref[...])
    else:
        yc_ref, u_ref, d_ref, wg_ref, bg_ref, x_ref, g1_ref, g2_ref = refs[:8]
        rest = refs[8:]
        z = jax.nn.gelu(yc_ref[...] + d_ref[...] * u_ref[...])
        y = z * jax.nn.sigmoid(_dot(z.astype(BF16), wg_ref[...]) + bg_ref[...])
    w1_ref, w2_ref, g3_ref = rest[:3]
    if next_norm:
        gn_ref, h2_ref, xn_ref, acc_ref = rest[3:]
    else:
        h2_ref, acc_ref = rest[3:]
    h = x_ref[...] + _rms(y, g1_ref[...])
    xm = _rms(h, g2_ref[...]).astype(BF16)
    for c in range(w1_ref.shape[1] // ff_chunk):
        cols = slice(c * ff_chunk, (c + 1) * ff_chunk)
        hm = jnp.maximum(_dot(xm, w1_ref[:, cols]), 0.0)
        part = _dot((hm * hm).astype(BF16), w2_ref[cols, :])
        if c == 0:
            acc_ref[...] = part
        else:
            acc_ref[...] += part
    h2 = h + _rms(acc_ref[...], g3_ref[...])
    h2_ref[...] = h2
    if next_norm:
        xn_ref[...] = _rms(h2, gn_ref[...])


def _layer_tail(mixer_args, x, g1, g2, w1, w2, g3, gn, *, mixer, tm, ff_chunk, transposed=False):
    rows, d = x.shape
    row = pl.BlockSpec((tm, d), lambda i: (i, 0))
    vec = _resident((1, d))
    if mixer == "nsa":
        o, w_o = mixer_args
        o_spec = (pl.BlockSpec((Q_W, tm), lambda i: (0, i)) if transposed
                  else pl.BlockSpec((tm, Q_W), lambda i: (i, 0)))
        head_specs = [o_spec, _resident(w_o.shape)]
    else:
        w_glu = mixer_args[3]
        head_specs = [row, row, vec, _resident(w_glu.shape), vec]
    tail_args = (w1, w2, g3) + (() if gn is None else (gn,))
    tail_specs = [_resident(w1.shape), _resident(w2.shape), vec] + ([] if gn is None else [vec])
    n_out = 1 if gn is None else 2
    return pl.pallas_call(
        functools.partial(_layer_tail_kernel, mixer=mixer, transposed=transposed, ff_chunk=ff_chunk,
                          next_norm=gn is not None),
        grid=(rows // tm,),
        in_specs=head_specs + [row, vec, vec] + tail_specs,
        out_specs=[row] * n_out,
        out_shape=[jax.ShapeDtypeStruct((rows, d), F32)] * n_out,
        scratch_shapes=[pltpu.VMEM((tm, d), F32)],
        compiler_params=_params("parallel"),
        name=mixer + "_layer_tail",
    )(*mixer_args, x, g1, g2, *tail_args)


def _s5_chunk_kernel(u_ref, brow_ref, pre_ref, pim_ref, qre_ref, qim_ref, are_ref, aim_ref, y_ref, hfin_ref,
                     sre, sim, hre, him, cre, cim, *, nb):
    steps_per_iter = SUBLANES // nb
    rows = sre.shape[0]
    uw = SSM_UNIT * SSM_GROUP
    nt = u_ref.shape[2] // uw

    @pl.when(pl.program_id(1) == 0)
    def _():
        cre[...] = jnp.zeros(cre.shape, F32)
        cim[...] = jnp.zeros(cim.shape, F32)

    u = u_ref[0]
    sre[...] = _dot(u, pre_ref[0])
    sim[...] = _dot(u, pim_ref[0])
    ar = are_ref[0]
    ai = aim_ref[0]

    def body(it, carry):
        hr, hi = carry
        r0 = pl.multiple_of(it * SUBLANES, SUBLANES)
        sr8 = sre[pl.ds(r0, SUBLANES), :]
        si8 = sim[pl.ds(r0, SUBLANES), :]
        prev_r, prev_i = [], []
        for j in range(steps_per_iter):
            prev_r.append(hr)
            prev_i.append(hi)
            sr = sr8[j * nb:(j + 1) * nb, :]
            si = si8[j * nb:(j + 1) * nb, :]
            hr, hi = ar * hr - ai * hi + sr, ar * hi + ai * hr + si
        hre[pl.ds(r0, SUBLANES), :] = jnp.concatenate(prev_r, axis=0)
        him[pl.ds(r0, SUBLANES), :] = jnp.concatenate(prev_i, axis=0)
        return hr, hi

    hr, hi = lax.fori_loop(0, rows // SUBLANES, body, (cre[...], cim[...]))
    cre[...] = hr
    cim[...] = hi
    hfin_ref[0, 0] = hr
    hfin_ref[0, 1] = hi

    hb_re = hre[...].astype(BF16)
    hb_im = him[...].astype(BF16)
    for t2 in range(0, nt, 2):
        cols = slice(t2 * uw, (t2 + 2) * uw)
        acc = _dot(hb_re, qre_ref[0, :, cols]) + _dot(hb_im, qim_ref[0, :, cols])
        for t1 in range(t2 + 2):
            lag0 = t2 - t1 + 1
            acc = acc + _dot(u[:, t1 * uw:(t1 + 1) * uw], brow_ref[0, :, lag0 * uw:(lag0 + 2) * uw])
        y_ref[0, :, cols] = acc


def _s5_chunk(u2, ops, *, nb):
    nunit, nch, width = u2.shape
    sw = ops["p_re"].shape[2]
    rows = _tile(nch, S5_ROW_BLOCK)
    per_unit = lambda a: pl.BlockSpec((1,) + a.shape[1:], lambda i, r: (i,) + (0,) * (a.ndim - 1))
    row_blk = pl.BlockSpec((1, rows, width), lambda i, r: (i, r, 0))
    args = (u2, ops["brow"], ops["p_re"], ops["p_im"], ops["q_re"], ops["q_im"], ops["a_chunk_re"], ops["a_chunk_im"])
    return pl.pallas_call(
        functools.partial(_s5_chunk_kernel, nb=nb),
        grid=(nunit, nch // rows),
        in_specs=[row_blk] + [per_unit(a) for a in args[1:]],
        out_specs=[row_blk, pl.BlockSpec((1, 2, nb, sw), lambda i, r: (i, 0, 0, 0))],
        out_shape=[jax.ShapeDtypeStruct((nunit, nch, width), F32), jax.ShapeDtypeStruct((nunit, 2, nb, sw), F32)],
        scratch_shapes=[pltpu.VMEM((rows, sw), F32)] * 4 + [pltpu.VMEM((nb, sw), F32)] * 2,
        compiler_params=_params("parallel", "arbitrary"),
        name="s5_chunk_scan",
    )(*args)


def _s5_seq_kernel(x_ref, brow_ref, pre_ref, pim_ref, qre_ref, qim_ref, are_ref, aim_ref, y_ref, hfin_ref,
                   sre, sim, hre, him, cre, cim):
    nb, tb, uw = x_ref.shape
    nt = pre_ref.shape[1] // uw
    nk = tb // nt

    @pl.when(pl.program_id(1) == 0)
    def _():
        cre[...] = jnp.zeros(cre.shape, F32)
        cim[...] = jnp.zeros(cim.shape, F32)

    u = [jnp.concatenate([x_ref[b, pl.ds(t, nk, stride=nt), :] for b in range(nb)], axis=0).astype(BF16)
         for t in range(nt)]
    u2 = [jnp.concatenate([u[2 * j], u[2 * j + 1]], axis=1) for j in range(nt // 2)]
    s_re = _dot(u2[0], pre_ref[0, 0:2 * uw, :])
    s_im = _dot(u2[0], pim_ref[0, 0:2 * uw, :])
    for j in range(1, nt // 2):
        s_re = s_re + _dot(u2[j], pre_ref[0, 2 * j * uw:(2 * j + 2) * uw, :])
        s_im = s_im + _dot(u2[j], pim_ref[0, 2 * j * uw:(2 * j + 2) * uw, :])
    sre[...] = s_re
    sim[...] = s_im
    ar = are_ref[0]
    ai = aim_ref[0]

    def body(it, carry):
        out = []
        for b in range(nb):
            hr, hi = carry[b]
            r0 = pl.multiple_of(b * nk + it * SUBLANES, SUBLANES)
            sr8 = sre[pl.ds(r0, SUBLANES), :]
            si8 = sim[pl.ds(r0, SUBLANES), :]
            prev_r, prev_i = [], []
            for j in range(SUBLANES):
                prev_r.append(hr)
                prev_i.append(hi)
                hr, hi = (ar * hr - ai * hi + sr8[j:j + 1, :], ar * hi + ai * hr + si8[j:j + 1, :])
            hre[pl.ds(r0, SUBLANES), :] = jnp.concatenate(prev_r, axis=0)
            him[pl.ds(r0, SUBLANES), :] = jnp.concatenate(prev_i, axis=0)
            out.append((hr, hi))
        return tuple(out)

    init = tuple((cre[b:b + 1, :], cim[b:b + 1, :]) for b in range(nb))
    fin = lax.fori_loop(0, nk // SUBLANES, body, init)
    for b in range(nb):
        cre[b:b + 1, :] = fin[b][0]
        cim[b:b + 1, :] = fin[b][1]
    hfin_ref[0, 0] = cre[...]
    hfin_ref[0, 1] = cim[...]

    hb_re = hre[...].astype(BF16)
    hb_im = him[...].astype(BF16)
    for t2 in range(0, nt, 2):
        cols = slice(t2 * uw, (t2 + 2) * uw)
        acc = _dot(hb_re, qre_ref[0, :, cols]) + _dot(hb_im, qim_ref[0, :, cols])
        for j in range(t2 // 2 + 1):
            lag0 = t2 - 2 * j + 1
            wpair = jnp.concatenate([brow_ref[0, :, lag0 * uw:(lag0 + 2) * uw],
                                     brow_ref[0, :, (lag0 - 1) * uw:(lag0 + 1) * uw]], axis=0)
            acc = acc + _dot(u2[j], wpair)
        for b in range(nb):
            for dt in range(2):
                y_ref[b, pl.ds(t2 + dt, nk, stride=nt), :] = acc[b * nk:(b + 1) * nk, dt * uw:(dt + 1) * uw]


def _s5_seq(x3, ops):
    nb, t, d = x3.shape
    uw = SSM_UNIT * SSM_GROUP
    nunit = d // uw
    sw = ops["p_re"].shape[2]
    tb = _tile(t, S5_TIME_BLOCK)
    nk = tb // SSM_CHUNK
    per_unit = lambda a: pl.BlockSpec((1,) + a.shape[1:], lambda i, r: (i,) + (0,) * (a.ndim - 1))
    blk = pl.BlockSpec((nb, tb, uw), lambda i, r: (0, r, i))
    args = (x3, ops["brow"], ops["p_re"], ops["p_im"], ops["q_re"], ops["q_im"], ops["a_chunk_re"], ops["a_chunk_im"])
    return pl.pallas_call(
        _s5_seq_kernel,
        grid=(nunit, t // tb),
        in_specs=[blk] + [per_unit(a) for a in args[1:]],
        out_specs=[blk, pl.BlockSpec((1, 2, nb, sw), lambda i, r: (i, 0, 0, 0))],
        out_shape=[jax.ShapeDtypeStruct((nb, t, d), F32), jax.ShapeDtypeStruct((nunit, 2, nb, sw), F32)],
        scratch_shapes=[pltpu.VMEM((nb * nk, sw), F32)] * 4 + [pltpu.VMEM((nb, sw), F32)] * 2,
        compiler_params=_params("parallel", "arbitrary"),
        name="s5_seq_scan",
    )(*args)


def _s5_step_kernel(u_ref, h0r_ref, h0i_ref, bre_ref, bim_ref, cre_ref, cim_ref, are_ref, aim_ref,
                    y_ref, hr_ref, hi_ref, *, npair):
    for p in range(npair):
        u = u_ref[p]
        ar = are_ref[p]
        ai = aim_ref[p]
        h0r = h0r_ref[p]
        h0i = h0i_ref[p]
        hr = ar * h0r - ai * h0i + _dot(u, bre_ref[p])
        hi = ar * h0i + ai * h0r + _dot(u, bim_ref[p])
        hr_ref[p] = hr
        hi_ref[p] = hi
        y_ref[p] = _dot(hr.astype(BF16), cre_ref[p]) + _dot(hi.astype(BF16), cim_ref[p])


def _s5_step(u2, h0r, h0i, ops):
    npair, rows, width = u2.shape
    args = (u2, h0r, h0i, ops["b1_re"], ops["b1_im"], ops["c1_re"], ops["c1_im"], ops["a1_re"], ops["a1_im"])
    return pl.pallas_call(
        functools.partial(_s5_step_kernel, npair=npair),
        grid=(1,),
        in_specs=[_full(a.shape) for a in args],
        out_specs=[_full((npair, rows, width)), _full(h0r.shape), _full(h0r.shape)],
        out_shape=[jax.ShapeDtypeStruct((npair, rows, width), F32), jax.ShapeDtypeStruct(h0r.shape, F32),
                   jax.ShapeDtypeStruct(h0r.shape, F32)],
        compiler_params=_params("arbitrary"),
        name="s5_single_step",
    )(*args)


def _s5_out_kernel(yc_ref, u_ref, d_ref, wg_ref, bg_ref, h_ref, g1_ref, g2_ref, h3_ref, xm_ref):
    y = jax.nn.gelu(yc_ref[...] + d_ref[...] * u_ref[...])
    out = y * jax.nn.sigmoid(_dot(y.astype(BF16), wg_ref[...]) + bg_ref[...])
    h3 = h_ref[...] + _rms(out, g1_ref[...])
    h3_ref[...] = h3
    xm_ref[...] = _rms(h3, g2_ref[...]).astype(BF16)


def _s5_out(yc, u, d_skip, w_glu, b_glu, h, g1, g2, *, tm):
    rows, d = h.shape
    row = pl.BlockSpec((tm, d), lambda i: (i, 0))
    vec = _full((1, d))
    return pl.pallas_call(
        _s5_out_kernel,
        grid=(rows // tm,),
        in_specs=[row, row, vec, _full(w_glu.shape), vec, row, vec, vec],
        out_specs=[row, row],
        out_shape=[jax.ShapeDtypeStruct((rows, d), F32), jax.ShapeDtypeStruct((rows, d), BF16)],
        compiler_params=_params("parallel"),
        name="s5_glu_out",
    )(yc, u, d_skip, w_glu, b_glu, h, g1, g2)


def _s5_operators(a_re, a_im, log_dt, b_re, b_im, c_re, c_im):
    hp = lax.Precision.HIGHEST
    g, n = a_re.shape
    gu = SSM_UNIT
    nunit = g // gu
    L = SSM_CHUNK
    uw = gu * SSM_GROUP
    sw = gu * n
    a = lax.complex(a_re.astype(F32), a_im.astype(F32))
    dt = jnp.exp(log_dt.astype(F32))[:, None]
    a_bar = jnp.exp(a * dt)
    b_bar = ((a_bar - 1.0) / a)[:, :, None] * lax.complex(b_re.astype(F32), b_im.astype(F32))
    c = lax.complex(c_re.astype(F32), c_im.astype(F32))
    pows = [jnp.ones_like(a_bar)]
    for _ in range(L):
        pows.append(pows[-1] * a_bar)
    a_pow = jnp.stack(pows).reshape(L + 1, nunit, sw)
    apr = jnp.real(a_pow)
    api = jnp.imag(a_pow)
    eye = jnp.eye(gu, dtype=F32)

    def bd_in(x):
        return jnp.einsum("pgnd,gh->pgdhn", x.reshape(nunit, gu, n, SSM_GROUP), eye).reshape(nunit, uw, sw)

    def bd_out(x):
        return jnp.einsum("pgcn,gh->pgnhc", x.reshape(nunit, gu, SSM_GROUP, n), eye).reshape(nunit, sw, uw)

    p0r, p0i = bd_in(jnp.real(b_bar)), bd_in(jnp.imag(b_bar))
    q0r, q0i = bd_out(jnp.real(c)), bd_out(jnp.imag(c))
    sr = apr[::-1][1:][:, :, None, :]
    si = api[::-1][1:][:, :, None, :]
    stack_rows = lambda x: x.transpose(1, 0, 2, 3).reshape(nunit, L * uw, sw)
    p_re = stack_rows(p0r[None] * sr - p0i[None] * si)
    p_im = stack_rows(p0r[None] * si + p0i[None] * sr)
    tr = apr[1:][:, :, :, None]
    ti = api[1:][:, :, :, None]
    stack_cols = lambda x: x.transpose(1, 2, 0, 3).reshape(nunit, sw, L * uw)
    q_re = stack_cols(q0r[None] * tr - q0i[None] * ti)
    q_im = stack_cols(-(q0r[None] * ti + q0i[None] * tr))
    xr = p0r[None] * apr[:L][:, :, None, :] - p0i[None] * api[:L][:, :, None, :]
    xi = p0r[None] * api[:L][:, :, None, :] + p0i[None] * apr[:L][:, :, None, :]
    b_lag = (jnp.einsum("tpus,psv->ptuv", xr, q0r, precision=hp)
             - jnp.einsum("tpus,psv->ptuv", xi, q0i, precision=hp))
    brow = jnp.concatenate([jnp.zeros((nunit, 1, uw, uw), F32), b_lag], axis=1)
    brow = brow.transpose(0, 2, 1, 3).reshape(nunit, uw, (L + 1) * uw)
    lane_row = lambda x: x.reshape(nunit, 1, sw)
    return {
        "brow": brow.astype(BF16),
        "p_re": p_re.astype(BF16), "p_im": p_im.astype(BF16),
        "q_re": q_re.astype(BF16), "q_im": q_im.astype(BF16),
        "a_chunk_re": lane_row(apr[L]), "a_chunk_im": lane_row(api[L]),
        "a1_re": lane_row(apr[1]), "a1_im": lane_row(api[1]),
        "b1_re": p0r.astype(BF16), "b1_im": p0i.astype(BF16),
        "c1_re": q0r.astype(BF16), "c1_im": (-q0i).astype(BF16),
    }


def _compress_paged_kernel(pt_ref, *refs, pg, nsb):
    del pt_ref
    pages = refs[:pg]
    perm_ref, wbd_ref, pe_ref, w2_ref, out_ref, h0_ref, h1_ref = refs[pg:]
    j = pl.program_id(1)
    sbp = PAGE_SIZE // CMP_STRIDE
    nrows = pg * sbp
    pair_rows = 2 * sbp

    @pl.when(j == 0)
    def _():
        h1_ref[:, nsb:nsb + SUBLANES, :] = jnp.zeros((2, SUBLANES, KV_W), F32)

    r0 = pl.multiple_of(j * nrows, nrows)
    for kv in range(2):
        staged = []
        for q in range(pg // 2):
            z = jnp.concatenate([pages[2 * q][0, kv], pages[2 * q + 1][0, kv]], axis=1).astype(BF16)
            staged.append(_dot_nt(perm_ref[...], z).astype(BF16))

        def load_rows(s, kv, staged=staged):
            return jnp.concatenate([x[s * pair_rows:(s + 1) * pair_rows, :] for x in staged], axis=0)

        pr0, pr1 = _compress_half(load_rows, kv, nrows, wbd_ref, pe_ref)
        h0_ref[kv, pl.ds(r0, nrows), :] = pr0
        h1_ref[kv, pl.ds(r0, nrows), :] = pr1

    @pl.when(j == pl.num_programs(1) - 1)
    def _():
        for kv in range(2):
            h = h0_ref[kv] + h1_ref[kv, pl.ds(1, nsb), :]
            out = _dot(jax.nn.gelu(h).astype(BF16), w2_ref[kv])
            out_ref[0, :, kv * KV_W:(kv + 1) * KV_W] = out.astype(BF16)


def _page_specs(pg, pages_per_sample):
    def spec(i):
        return pl.BlockSpec((1, 2, KV_W, PAGE_SIZE),
                            lambda b, j, pt: (pt[b * pages_per_sample + j * pg + i], 0, 0, 0))
    return [spec(i) for i in range(pg)]


def _compress_paged(pages, pt_flat, wbd, pe_t, w2bd, *, nb, pages_per_sample, pg):
    nsb = pages_per_sample * PAGE_SIZE // CMP_STRIDE
    sbp = PAGE_SIZE // CMP_STRIDE
    i_out = jnp.arange(2 * PAGE_SIZE)
    s_i, pg_i, n_i = i_out // (2 * sbp), (i_out // sbp) % 2, i_out % sbp
    perm = (i_out[None, :] == (pg_i * PAGE_SIZE + n_i * CMP_STRIDE + s_i)[:, None]).astype(BF16)
    c3 = lambda shape: pl.BlockSpec(shape, lambda b, j, pt: (0,) * len(shape))
    grid_spec = pltpu.PrefetchScalarGridSpec(
        num_scalar_prefetch=1,
        grid=(nb, pages_per_sample // pg),
        in_specs=_page_specs(pg, pages_per_sample) + [c3(perm.shape), c3(wbd.shape), c3(pe_t.shape), c3(w2bd.shape)],
        out_specs=pl.BlockSpec((1, nsb, 2 * KV_W), lambda b, j, pt: (b, 0, 0)),
        scratch_shapes=[pltpu.VMEM((2, nsb, KV_W), F32), pltpu.VMEM((2, nsb + SUBLANES, KV_W), F32)],
    )
    return pl.pallas_call(
        functools.partial(_compress_paged_kernel, pg=pg, nsb=nsb),
        grid_spec=grid_spec,
        out_shape=jax.ShapeDtypeStruct((nb, nsb, 2 * KV_W), BF16),
        compiler_params=_params("parallel", "arbitrary"),
        name="nsa_compress_paged",
    )(pt_flat, *([pages] * pg), perm, wbd, pe_t, w2bd)


def _topk_mask_lanes(imp, n_top, ns_valid):
    lane = lax.broadcasted_iota(jnp.int32, imp.shape, 1)
    cnt = jnp.zeros(imp.shape, F32)
    for sp in range(ns_valid):
        col = imp[:, sp:sp + 1]
        cnt = cnt + jnp.where(lane > sp, jnp.where(col >= imp, 1.0, 0.0), jnp.where(col > imp, 1.0, 0.0))
    return jnp.where((cnt < n_top) & (lane < ns_valid), 1.0, 0.0)


def _dot_nt(a, b):
    return lax.dot_general(a, b, (((1,), (1,)), ((), ())), preferred_element_type=F32)


def _attn_sample_kernel(pt_ref, *refs, pg, ns, past, nc, ns_valid, n_top):
    del pt_ref
    all_pages = [refs[u * pg:(u + 1) * pg] for u in range(ns)]
    (q_ref, qr_ref, g_ref, kcvc_ref, ksn_ref, kwn_ref, win_ref, ov_ref, e_ref, gs_ref, o_ref,
     m_ref, l_ref, acc_ref, sel_ref, oc_ref, ow_ref) = refs[ns * pg:]
    j = pl.program_id(1)
    ncp = kcvc_ref.shape[1]
    nsp = ov_ref.shape[1]
    wlen = win_ref.shape[3]
    row = lax.broadcasted_iota(jnp.int32, (N_HEADS, KV_W), 0)
    lane = lax.broadcasted_iota(jnp.int32, (N_HEADS, KV_W), 1)
    own = (lane // HEAD_DIM) == (row // GROUP)

    def spread(ref, u):
        q = ref[u]
        return jnp.where(own, jnp.concatenate([q] * N_KV_HEADS, axis=1), jnp.zeros((N_HEADS, KV_W), BF16))

    def update(state, s, vs):
        m, l, acc = state
        m_new = jnp.maximum(m, jnp.max(s, axis=1, keepdims=True))
        alpha = jnp.exp(m - m_new)
        p = jnp.exp(s - m_new)
        l = alpha * l + jnp.sum(p, axis=1, keepdims=True)
        pv = None
        for st, sz, v, feature_major in vs:
            pb = p[:, st:st + sz].astype(BF16)
            t = _dot_nt(pb, v) if feature_major else _dot(pb, v)
            pv = t if pv is None else pv + t
        return m_new, l, alpha * acc + pv

    def init():
        return (jnp.full((N_HEADS, 1), NEG, F32), jnp.zeros((N_HEADS, 1), F32), jnp.zeros((N_HEADS, KV_W), F32))

    def new_row_update(state, qbd, new_row):
        r8 = lax.broadcasted_iota(jnp.int32, (SUBLANES, 2 * KV_W), 0)
        tile = jnp.where(r8 == 0, jnp.broadcast_to(new_row, (SUBLANES, 2 * KV_W)), 0.0).astype(BF16)
        s = _dot_nt(qbd, tile[:, 0:KV_W])
        l8 = lax.broadcasted_iota(jnp.int32, (N_HEADS, SUBLANES), 1)
        s = jnp.where(l8 == 0, s, MASKED)
        return update(state, s, [(0, SUBLANES, tile[:, KV_W:2 * KV_W], False)])

    def put(u, state):
        m, l, acc = state
        m_ref[u] = jnp.broadcast_to(m, m_ref.shape[1:])
        l_ref[u] = jnp.broadcast_to(l, l_ref.shape[1:])
        acc_ref[u] = acc

    qrbd = [spread(qr_ref, u) for u in range(ns)]

    @pl.when(j == 0)
    def _():
        cidx = lax.broadcasted_iota(jnp.int32, (1, ncp), 1)
        valid = ((cidx * CMP_STRIDE + (CMP_BLOCK - 1)) <= past) & (cidx < nc)
        sidx = lax.broadcasted_iota(jnp.int32, (1, nsp), 1)
        cur = past // SEL_BLOCK
        forced = (sidx == 0) | (sidx == cur) | (sidx == cur - 1)
        causal = (sidx * SEL_BLOCK) <= past
        wpos = past - wlen + lax.broadcasted_iota(jnp.int32, (1, wlen), 1)
        in_window = (wpos >= 0) & (past - wpos <= WINDOW)
        for u in range(ns):
            s = _dot_nt(spread(q_ref, u), kcvc_ref[u, :, 0:KV_W])
            sm = jnp.where(valid, s, NEG)
            mx = jnp.max(sm, axis=1, keepdims=True)
            e = jnp.where(valid, jnp.exp(sm - mx), 0.0)
            p = e / jnp.maximum(jnp.sum(e, axis=1, keepdims=True), 1e-20)
            oc_ref[u] = _dot(p.astype(BF16), kcvc_ref[u, :, KV_W:2 * KV_W])
            imp = _dot_f32rhs(_dot_f32lhs(gs_ref[...], p), ov_ref[...])
            imp = jnp.where(forced, FORCE, jnp.where(causal, imp, NEG))
            imp = jnp.where(sidx < ns_valid, imp, MASKED)
            sel_ref[u] = _topk_mask_lanes(imp, n_top, ns_valid)
            sw = jnp.where(in_window, _dot(qrbd[u], win_ref[u, 0].astype(BF16)), MASKED)
            st = update(init(), sw, [(0, wlen, win_ref[u, 1].astype(BF16), True)])
            st = new_row_update(st, qrbd[u], kwn_ref[u])
            ow_ref[u] = st[2] * (1.0 / jnp.maximum(st[1], 1e-20))
            put(u, new_row_update(init(), qrbd[u], ksn_ref[u]))

    def halves(pages):
        return [pages[:pg // 2], pages[pg // 2:]] if pg > 1 else [pages]

    scores = [[jnp.concatenate([_dot(qrbd[u], r[0, 0].astype(BF16)) for r in part], axis=1)
               for part in halves(all_pages[u])] for u in range(ns)]
    masks = [_dot(sel_ref[u].astype(BF16), e_ref[0]) > 0.5 for u in range(ns)]
    states = [(m_ref[u, :, 0:1], l_ref[u, :, 0:1], acc_ref[u]) for u in range(ns)]
    k0 = 0
    for hx, part0 in enumerate(halves(all_pages[0])):
        width = len(part0) * PAGE_SIZE
        for u in range(ns):
            part = halves(all_pages[u])[hx]
            s = jnp.where(masks[u][:, k0:k0 + width], scores[u][hx], MASKED)
            vs = [(i * PAGE_SIZE, PAGE_SIZE, r[0, 1].astype(BF16), True) for i, r in enumerate(part)]
            states[u] = update(states[u], s, vs)
        k0 += width
    for u in range(ns):
        put(u, states[u])

    @pl.when(j == pl.num_programs(1) - 1)
    def _():
        for u in range(ns):
            g = g_ref[u]
            os_ = acc_ref[u] * (1.0 / jnp.maximum(l_ref[u, :, 0:1], 1e-20))
            o = g[:, 0:1] * oc_ref[u] + g[:, 1:2] * os_ + g[:, 2:3] * ow_ref[u]
            o = jnp.where(own, o, 0.0)
            out = o[:, 0:HEAD_DIM]
            for h in range(1, N_KV_HEADS):
                out = out + o[:, h * HEAD_DIM:(h + 1) * HEAD_DIM]
            o_ref[u] = out.astype(BF16)


def _attn_sample(pages, pt_flat, q3, qr3, g3, kcvc, ks_new, kw_new, win, ov, e_mat, gsum, *, nb, pages_per_sample,
                 pg, past, nc, ns_valid, n_top):
    nsp = ov.shape[1]
    ns = SAMPLES_PER_STEP if nb % SAMPLES_PER_STEP == 0 else 1
    per_b = lambda shape: pl.BlockSpec((ns,) + shape, lambda b, j, pt: (b,) + (0,) * len(shape))
    const = lambda shape: pl.BlockSpec(shape, lambda b, j, pt: (0,) * len(shape))

    def page_spec(u, i):
        return pl.BlockSpec((1, 2, KV_W, PAGE_SIZE),
                            lambda b, j, pt: (pt[(b * ns + u) * pages_per_sample + j * pg + i], 0, 0, 0))

    per_sample = lambda *shape: pltpu.VMEM((ns,) + shape, F32)
    grid_spec = pltpu.PrefetchScalarGridSpec(
        num_scalar_prefetch=1,
        grid=(nb // ns, pages_per_sample // pg),
        in_specs=[page_spec(u, i) for u in range(ns) for i in range(pg)] + [
            per_b((N_HEADS, HEAD_DIM)), per_b((N_HEADS, HEAD_DIM)), per_b((N_HEADS, 3)),
            per_b(kcvc.shape[1:]), per_b((1, 2 * KV_W)), per_b((1, 2 * KV_W)), per_b(win.shape[1:]),
            const(ov.shape), pl.BlockSpec((1,) + e_mat.shape[1:], lambda b, j, pt: (j, 0, 0)), const(gsum.shape)],
        out_specs=per_b((N_HEADS, HEAD_DIM)),
        scratch_shapes=[per_sample(N_HEADS, LANES), per_sample(N_HEADS, LANES), per_sample(N_HEADS, KV_W),
                        per_sample(N_HEADS, nsp), per_sample(N_HEADS, KV_W), per_sample(N_HEADS, KV_W)],
    )
    return pl.pallas_call(
        functools.partial(_attn_sample_kernel, pg=pg, ns=ns, past=past, nc=nc, ns_valid=ns_valid, n_top=n_top),
        grid_spec=grid_spec,
        out_shape=jax.ShapeDtypeStruct((nb, N_HEADS, HEAD_DIM), BF16),
        compiler_params=_params("parallel", "arbitrary"),
        name="nsa_attn_sample",
    )(pt_flat, *([pages] * (ns * pg)), q3, qr3, g3, kcvc, ks_new, kw_new, win, ov, e_mat, gsum)


def _rope_tables(pos):
    half = HEAD_DIM // 2
    inv = ROPE_THETA ** (-jnp.arange(half, dtype=F32) / half)
    ang = pos.astype(F32)[:, None] * inv[None, :]
    cos = jnp.cos(ang)
    sin = jnp.sin(ang)
    reps = LANES // HEAD_DIM
    return (jnp.tile(jnp.concatenate([cos, cos], axis=1), (1, reps)),
            jnp.tile(jnp.concatenate([-sin, sin], axis=1), (1, reps)))


def _compress_weights(cmp_w1, cmp_w2, cmp_pe):
    ratio = CMP_BLOCK // CMP_STRIDE
    eye = jnp.eye(N_KV_HEADS, dtype=F32)
    w1r = cmp_w1.reshape(2, ratio, CMP_STRIDE, HEAD_DIM, HEAD_DIM)
    wbd = jnp.einsum("krsde,hg->krshdge", w1r, eye).reshape(2 * ratio * CMP_STRIDE, KV_W, KV_W).astype(BF16)
    w2bd = jnp.einsum("kef,hg->khegf", cmp_w2, eye).reshape(2, KV_W, KV_W).astype(BF16)
    pe_r = cmp_pe.reshape(2, ratio, CMP_STRIDE, HEAD_DIM).astype(F32)
    pe_w = jnp.einsum("krsd,krsde->kre", pe_r, w1r.astype(F32), precision=lax.Precision.HIGHEST)
    pe_t = jnp.tile(pe_w.reshape(2 * ratio, HEAD_DIM), (1, N_KV_HEADS))
    return wbd, pe_t, w2bd


def _overlap(nc, ncp, nsel, nsp):
    c_start = jnp.arange(ncp, dtype=jnp.int32)[:, None] * CMP_STRIDE
    s_start = jnp.arange(nsp, dtype=jnp.int32)[None, :] * SEL_BLOCK
    ov = (c_start < s_start + SEL_BLOCK) & (c_start + CMP_BLOCK > s_start)
    ov = ov & (jnp.arange(ncp)[:, None] < nc) & (jnp.arange(nsp)[None, :] < nsel)
    return ov.astype(BF16)


def _pick(*cands):
    return next(c for c in cands if c)


def _tile(n, pref):
    t = min(n, pref)
    while n % t:
        t //= 2
    return t


def kernel(x_prompt, x_sample, cache_kv_cmp, cache_kv_sel, cache_kv_win, state_ssm, page_table, norm_g, mlp_w1,
           mlp_w2, nsa_w_in, nsa_w_o, nsa_cmp_w1, nsa_cmp_w2, nsa_cmp_pe, s5_a_re, s5_a_im, s5_log_dt, s5_b_re,
           s5_b_im, s5_c_re, s5_c_im, s5_d, s5_w_glu, s5_b_glu):
    b, t, d = x_prompt.shape
    nb = x_sample.shape[0]
    pages_per_sample = page_table.shape[1]
    past = pages_per_sample * PAGE_SIZE
    rows_p = b * t
    g = norm_g.reshape(norm_g.shape[0], 4, 1, d)

    w_in = nsa_w_in[0]
    w_main = w_in[:, :Q_W + 6 * KV_W].astype(BF16)
    w_gate = jnp.pad(w_in[:, Q_W + 6 * KV_W:], ((0, 0), (0, LANES - 3 * N_HEADS))).astype(BF16)
    w_o = nsa_w_o[0].astype(BF16)
    wbd, pe_t, w2bd = _compress_weights(nsa_cmp_w1[0], nsa_cmp_w2[0], nsa_cmp_pe[0])
    w1 = mlp_w1.astype(BF16)
    w2 = mlp_w2.astype(BF16)
    w_glu = s5_w_glu[0].astype(BF16)
    ops = _s5_operators(s5_a_re[0], s5_a_im[0], s5_log_dt[0], s5_b_re[0], s5_b_im[0], s5_c_re[0], s5_c_im[0])
    d_skip = s5_d[0].reshape(1, d)
    b_glu = s5_b_glu[0].reshape(1, d)

    tm = _tile(rows_p, 512)
    ff_chunk = _tile(mlp_w1.shape[2], 1024)

    xp = x_prompt.reshape(rows_p, d)
    cos_p, sin_p = _rope_tables(jnp.arange(t, dtype=jnp.int32))
    tq = next(c for c in (2 * LANES, LANES) if t % c == 0 and WINDOW % c == 0)
    (qT, qrT, gT, kvc, kvcT, kvsT, kvwT, ksb, kwb, vsT, vwT) = _inproj(
        xp, g[0, 0], w_main, w_gate, cos_p, sin_p, tm=_tile(t, 512), pos_blocks=t // _tile(t, 512), transposed=True,
        key_chunk=tq)
    nsb_p = t // CMP_STRIDE
    nc_p = nsb_p - CMP_BLOCK // CMP_STRIDE + 1
    nsel_p = t // SEL_BLOCK
    kc, vcT = _compress_prompt(kvc.reshape(b, t, 2 * KV_W), wbd, pe_t, w2bd)
    ovT = _overlap(nc_p, nsb_p, nsel_p, nsel_p).T
    oT = _attn_prompt(qT, qrT, gT, kc, vcT, ksb, vsT, kwb, vwT, ovT, batch=b, seq=t, tq=tq, nc=nc_p,
                      n_top=min(TOP_N, nsel_p))
    hp, xn1 = _layer_tail((oT, w_o), xp, g[0, 1], g[0, 2], w1[0], w2[0], g[0, 3], g[1, 0], mixer="nsa", tm=tm,
                          ff_chunk=ff_chunk, transposed=True)

    pw = SSM_UNIT * SSM_GROUP
    npair = d // pw
    y3, hfin = _s5_seq(xn1.reshape(b, t, d), ops)
    yc = y3.reshape(rows_p, d)
    (hp,) = _layer_tail((yc, xn1, d_skip, w_glu, b_glu), hp, g[1, 1], g[1, 2], w1[1], w2[1], g[1, 3], None,
                        mixer="s5", tm=tm, ff_chunk=ff_chunk)
    ssm_p = hfin.reshape(npair, 2, b, SSM_UNIT, SSM_STATE).transpose(2, 1, 0, 3, 4)
    ssm_p = ssm_p.reshape(b, 2, d // SSM_GROUP, SSM_STATE)

    xs = x_sample.reshape(nb, d)
    cos_s, sin_s = _rope_tables(jnp.full((nb,), past, dtype=jnp.int32))
    q_s, qr_s, gates_s, kvc_s, kvs_s, kvw_s = _inproj(
        xs, g[0, 0], w_main, w_gate, cos_s, sin_s, tm=nb, pos_blocks=1, transposed=False)
    pt_flat = page_table.reshape(-1).astype(jnp.int32)
    pg = _tile(pages_per_sample, PAGE_GROUP)
    n_pool = cache_kv_cmp.shape[1]
    feature_major = lambda c, n, s: c.transpose(0, 2, 3, 4, 1).reshape(n, 2, KV_W, s)
    cmp_pages = feature_major(cache_kv_cmp[0], n_pool, PAGE_SIZE)
    sel_pages = feature_major(cache_kv_sel[0], n_pool, PAGE_SIZE)
    kcvc = _compress_paged(cmp_pages, pt_flat, wbd, pe_t, w2bd, nb=nb, pages_per_sample=pages_per_sample, pg=pg)
    l_all = past + 1
    nsb_s = l_all // CMP_STRIDE
    nc_s = nsb_s - CMP_BLOCK // CMP_STRIDE + 1
    nsel_s = -(-l_all // SEL_BLOCK)
    nsp = -(-nsel_s // LANES) * LANES
    ov_s = _overlap(nc_s, past // CMP_STRIDE, nsel_s, nsp)
    keys_per_step = pg * PAGE_SIZE
    key_blk = (jnp.arange(past, dtype=jnp.int32) // SEL_BLOCK).reshape(past // keys_per_step, 1, keys_per_step)
    e_mat = (jnp.arange(nsp, dtype=jnp.int32)[None, :, None] == key_blk).astype(BF16)
    hh = jnp.arange(N_HEADS)
    gsum = ((hh[:, None] // GROUP) == (hh[None, :] // GROUP)).astype(BF16)
    win = feature_major(cache_kv_win[0], nb, WINDOW)
    o_s = _attn_sample(sel_pages, pt_flat, q_s.reshape(nb, N_HEADS, HEAD_DIM), qr_s.reshape(nb, N_HEADS, HEAD_DIM),
                       gates_s[:, :3 * N_HEADS].reshape(nb, N_HEADS, 3), kcvc, kvs_s.reshape(nb, 1, 2 * KV_W),
                       kvw_s.reshape(nb, 1, 2 * KV_W), win, ov_s, e_mat, gsum, nb=nb,
                       pages_per_sample=pages_per_sample, pg=pg, past=past, nc=nc_s, ns_valid=nsel_s,
                       n_top=min(TOP_N, nsel_s))
    hs, xn1_s = _layer_tail((o_s.reshape(nb, Q_W), w_o), xs, g[0, 1], g[0, 2], w1[0], w2[0], g[0, 3], g[1, 0],
                            mixer="nsa", tm=nb, ff_chunk=ff_chunk)

    u2_s = xn1_s.astype(BF16).reshape(nb, npair, pw).transpose(1, 0, 2)
    st = state_ssm[0].reshape(nb, 2, npair, SSM_UNIT * SSM_STATE).transpose(1, 2, 0, 3)
    y2_s, hr_s, hi_s = _s5_step(u2_s, st[0], st[1], ops)
    yc_s = y2_s.transpose(1, 0, 2).reshape(nb, d)
    (hs,) = _layer_tail((yc_s, xn1_s, d_skip, w_glu, b_glu), hs, g[1, 1], g[1, 2], w1[1], w2[1], g[1, 3], None,
                        mixer="s5", tm=nb, ff_chunk=ff_chunk)
    ssm_s = jnp.stack([hr_s, hi_s], axis=0).transpose(2, 0, 1, 3).reshape(nb, 2, d // SSM_GROUP, SSM_STATE)

    kv5 = lambda a, n, s: a.reshape(1, n, s, 2, N_KV_HEADS, HEAD_DIM)
    from_fm = lambda a, n, s: a.reshape(n, 2, N_KV_HEADS, HEAD_DIM, s).transpose(0, 4, 1, 2, 3)[None]
    win_s = jnp.concatenate([win[..., 1:], kvw_s.reshape(nb, 2, KV_W, 1)], axis=-1)
    return (hp.reshape(b, t, d), hs.reshape(nb, 1, d),
            from_fm(kvcT, b, t), kv5(kvc_s, nb, 1), from_fm(kvsT, b, t), kv5(kvs_s, nb, 1),
            from_fm(kvwT[:, :, t - WINDOW:], b, WINDOW), from_fm(win_s, nb, WINDOW), ssm_p[None], ssm_s[None])
```

```python
import functools

import jax
import jax.numpy as jnp
import numpy as np
from jax import lax
from jax.experimental import pallas as pl
from jax.experimental.pallas import tpu as pltpu

N_HEADS = 16
HEAD_DIM = 64
N_KV_HEADS = 4
GROUP = N_HEADS // N_KV_HEADS
CMP_BLOCK = 32
CMP_STRIDE = 16
SEL_BLOCK = 64
TOP_N = 16
WINDOW = 512
ROPE_THETA = 10000.0
PAGE_SIZE = 128
SSM_GROUP = 16
SSM_STATE = 64
SSM_CHUNK = 8
SSM_UNIT = 8
EPS = 1e-6
NEG = -1e30
FORCE = 1e9
MASKED = -1.5e38
LOG2E = 1.4426950408889634
V_ROWS = HEAD_DIM + 16
Q_W = N_HEADS * HEAD_DIM
KV_W = N_KV_HEADS * HEAD_DIM
LANES = 128
SUBLANES = 8
PAGE_GROUP = 16
SAMPLES_PER_STEP = 2
S5_TIME_BLOCK = 2048
VMEM_LIMIT = 56 * 1024 * 1024

F32 = jnp.float32
BF16 = jnp.bfloat16


def _params(*sem):
    return pltpu.CompilerParams(dimension_semantics=sem, vmem_limit_bytes=VMEM_LIMIT)


def _full(shape):
    zeros = (0,) * len(shape)
    return pl.BlockSpec(shape, lambda *_: zeros)


def _rms(x, g):
    ms = jnp.mean(x * x, axis=-1, keepdims=True)
    return x * lax.rsqrt(ms + EPS) * g


def _dot(a, b):
    return jnp.dot(a, b, preferred_element_type=F32)


def _dot_f32lhs(w, x):
    hi = x.astype(BF16)
    r1 = x - hi.astype(F32)
    mid = r1.astype(BF16)
    lo = (r1 - mid.astype(F32)).astype(BF16)
    return _dot(w, hi) + _dot(w, mid) + _dot(w, lo)


def _dot_f32rhs(x, w):
    hi = x.astype(BF16)
    r1 = x - hi.astype(F32)
    mid = r1.astype(BF16)
    lo = (r1 - mid.astype(F32)).astype(BF16)
    return _dot(hi, w) + _dot(mid, w) + _dot(lo, w)


def _rope_nat(x, cos, sin):
    half = HEAD_DIM // 2
    lane = lax.broadcasted_iota(jnp.int32, (1, LANES), 1)
    first = (lane % HEAD_DIM) < half
    outs = []
    for c in range(x.shape[1] // LANES):
        xc = x[:, c * LANES:(c + 1) * LANES]
        rot = jnp.where(first, pltpu.roll(xc, LANES - half, 1), pltpu.roll(xc, half, 1))
        outs.append(xc * cos + rot * sin)
    return jnp.concatenate(outs, axis=1)


def _inproj_kernel(x_ref, g_ref, w_ref, wg_ref, cos_ref, sin_ref, *outs, transposed, key_chunk):
    xb = _rms(x_ref[...], g_ref[...]).astype(BF16)
    cos = cos_ref[...]
    sin = sin_ref[...]
    scale = HEAD_DIM ** -0.5
    q = _dot(xb, w_ref[:, 0:Q_W])
    qr = _rope_nat(q, cos, sin)
    kv = _dot(xb, w_ref[:, Q_W:Q_W + 6 * KV_W])
    gates = jax.nn.sigmoid(_dot(xb, wg_ref[...]))
    k_s = _rope_nat(kv[:, 2 * KV_W:3 * KV_W], cos, sin)
    v_s = kv[:, 3 * KV_W:4 * KV_W]
    k_w = _rope_nat(kv[:, 4 * KV_W:5 * KV_W], cos, sin)
    v_w = kv[:, 5 * KV_W:6 * KV_W]
    kvc_ref = outs[3]
    kvc_ref[...] = kv[:, 0:2 * KV_W]
    if transposed:
        qT_ref, qrT_ref, gT_ref, _, kvcT_ref, kvsT_ref, kvwT_ref, ksb_ref, kwb_ref, vsT_ref, vwT_ref = outs
        qT_ref[...] = (q * (scale * LOG2E)).T.astype(BF16)
        qrT_ref[...] = (qr * (scale * LOG2E)).T.astype(BF16)
        gT_ref[...] = gates.T
        tm = x_ref.shape[0]
        rowi = lax.broadcasted_iota(jnp.int32, (tm, LANES), 0)
        lanei = lax.broadcasted_iota(jnp.int32, (tm, LANES), 1)
        blk = lax.shift_right_logical(rowi & (key_chunk - 1), SEL_BLOCK.bit_length() - 1)
        extra = lanei - HEAD_DIM
        nblk = key_chunk // SEL_BLOCK
        aug = jnp.where((extra == blk) | ((extra >= nblk) & (extra < nblk + N_CONST_LANES)), 1.0, 0.0)
        for c in range(KV_W // LANES):
            for k_nat, k_ref in ((k_s, ksb_ref), (k_w, kwb_ref)):
                pair = k_nat[:, c * LANES:(c + 1) * LANES]
                k_ref[2 * c] = jnp.where(lanei < HEAD_DIM, pair, aug).astype(BF16)
                k_ref[2 * c + 1] = jnp.where(lanei < HEAD_DIM, pltpu.roll(pair, HEAD_DIM, 1), aug).astype(BF16)
        v_sT = v_s.T
        v_wT = v_w.T
        ones_rows = jnp.where(lax.broadcasted_iota(jnp.int32, (V_ROWS - HEAD_DIM, tm), 0) == 0, 1.0, 0.0)
        for h in range(N_KV_HEADS):
            for vT, v_ref in ((v_sT, vsT_ref), (v_wT, vwT_ref)):
                v_ref[h * V_ROWS:(h + 1) * V_ROWS, :] = jnp.concatenate(
                    [vT[h * HEAD_DIM:(h + 1) * HEAD_DIM, :], ones_rows], axis=0).astype(BF16)
        kvcT_ref[0] = kv[:, 0:2 * KV_W].T
        kvsT_ref[0, 0:KV_W, :] = k_s.T
        kvsT_ref[0, KV_W:2 * KV_W, :] = v_sT
        kvwT_ref[0, 0:KV_W, :] = k_w.T
        kvwT_ref[0, KV_W:2 * KV_W, :] = v_wT
    else:
        q_ref, qr_ref, gt_ref, _, kvs_ref, kvw_ref = outs
        q_ref[...] = (q * scale).astype(BF16)
        qr_ref[...] = (qr * scale).astype(BF16)
        gt_ref[...] = gates
        kvs_ref[:, 0:KV_W] = k_s
        kvs_ref[:, KV_W:2 * KV_W] = v_s
        kvw_ref[:, 0:KV_W] = k_w
        kvw_ref[:, KV_W:2 * KV_W] = v_w


def _inproj(x, g, w_main, w_gate, cos_t, sin_t, *, tm, pos_blocks, transposed, key_chunk=LANES):
    rows, d = x.shape
    assert tm % key_chunk == 0 or not transposed
    n = rows // tm
    row_blk = lambda w: pl.BlockSpec((tm, w), lambda i: (i, 0))
    col_blk = lambda h: pl.BlockSpec((h, tm), lambda i: (0, i))
    tab = pl.BlockSpec((tm, LANES), lambda i: (i % pos_blocks, 0))
    kv_nat = jax.ShapeDtypeStruct((rows, 2 * KV_W), F32)
    if transposed:
        seqs = rows // (pos_blocks * tm)
        kh = pl.BlockSpec((N_KV_HEADS, tm, LANES), lambda i: (0, i, 0))
        kvT = pl.BlockSpec((1, 2 * KV_W, tm), lambda i: (i // pos_blocks, 0, i % pos_blocks))
        out_shape = ([jax.ShapeDtypeStruct((Q_W, rows), BF16)] * 2 + [jax.ShapeDtypeStruct((LANES, rows), F32)]
                     + [kv_nat] + [jax.ShapeDtypeStruct((seqs, 2 * KV_W, pos_blocks * tm), F32)] * 3
                     + [jax.ShapeDtypeStruct((N_KV_HEADS, rows, LANES), BF16)] * 2
                     + [jax.ShapeDtypeStruct((N_KV_HEADS * V_ROWS, rows), BF16)] * 2)
        out_specs = ([col_blk(Q_W)] * 2 + [col_blk(LANES)] + [row_blk(2 * KV_W)] + [kvT] * 3 + [kh] * 2
                     + [col_blk(N_KV_HEADS * V_ROWS)] * 2)
    else:
        out_shape = ([jax.ShapeDtypeStruct((rows, Q_W), BF16)] * 2 + [jax.ShapeDtypeStruct((rows, LANES), F32)]
                     + [kv_nat] * 3)
        out_specs = [row_blk(Q_W)] * 2 + [row_blk(LANES)] + [row_blk(2 * KV_W)] * 3
    return pl.pallas_call(
        functools.partial(_inproj_kernel, transposed=transposed, key_chunk=key_chunk),
        grid=(n,),
        in_specs=[row_blk(d), _full((1, d)), _full(w_main.shape), _full(w_gate.shape), tab, tab],
        out_specs=out_specs,
        out_shape=out_shape,
        compiler_params=_params("parallel"),
        name="nsa_inproj",
    )(x, g, w_main, w_gate, cos_t, sin_t)


_KV_CHUNKS = 2 * KV_W // LANES


def _compress_half(load_rows, kv, nrows, wbd_ref, pe_ref):
    ratio = CMP_BLOCK // CMP_STRIDE
    accs = [jnp.broadcast_to(pe_ref[kv * ratio + r:kv * ratio + r + 1, :], (nrows, KV_W)) for r in range(ratio)]
    for s in range(CMP_STRIDE):
        lhs = load_rows(s, kv).astype(BF16)
        for r in range(ratio):
            accs[r] = accs[r] + _dot(lhs, wbd_ref[(kv * ratio + r) * CMP_STRIDE + s])
    return accs


def _compress_prompt_kernel(*refs, nsb):
    x_refs = refs[:_KV_CHUNKS]
    wbd_ref, pe_ref, w2_ref, kc_ref, vcT_ref, sh_ref = refs[_KV_CHUNKS:]

    def load_rows(s, kv):
        per_half = _KV_CHUNKS // 2
        return jnp.concatenate([x_refs[kv * per_half + c][0, pl.ds(s, nsb, stride=CMP_STRIDE), :]
                                for c in range(per_half)], axis=1)

    sh_ref[nsb:nsb + SUBLANES, :] = jnp.zeros((SUBLANES, KV_W), F32)
    for kv in range(2):
        pr0, pr1 = _compress_half(load_rows, kv, nsb, wbd_ref, pe_ref)
        sh_ref[0:nsb, :] = pr1
        h = pr0 + sh_ref[pl.ds(1, nsb), :]
        out = _dot(jax.nn.gelu(h).astype(BF16), w2_ref[kv])
        if kv == 0:
            for hh in range(N_KV_HEADS):
                kc_ref[0, hh] = out[:, hh * HEAD_DIM:(hh + 1) * HEAD_DIM].astype(BF16)
        else:
            vcT_ref[0] = out.T.astype(BF16)


def _compress_prompt(kvc3, wbd, pe_t, w2bd):
    b, t, _ = kvc3.shape
    nsb = t // CMP_STRIDE
    return pl.pallas_call(
        functools.partial(_compress_prompt_kernel, nsb=nsb),
        grid=(b,),
        in_specs=[pl.BlockSpec((1, t, LANES), lambda i, c=c: (i, 0, c)) for c in range(_KV_CHUNKS)]
        + [_full(wbd.shape), _full(pe_t.shape), _full(w2bd.shape)],
        out_specs=[pl.BlockSpec((1, N_KV_HEADS, nsb, HEAD_DIM), lambda i: (i, 0, 0, 0)),
                   pl.BlockSpec((1, KV_W, nsb), lambda i: (i, 0, 0))],
        out_shape=[jax.ShapeDtypeStruct((b, N_KV_HEADS, nsb, HEAD_DIM), BF16),
                   jax.ShapeDtypeStruct((b, KV_W, nsb), BF16)],
        scratch_shapes=[pltpu.VMEM((nsb + SUBLANES, KV_W), F32)],
        compiler_params=_params("parallel"),
        name="nsa_compress_prompt",
    )(*([kvc3] * _KV_CHUNKS), wbd, pe_t, w2bd)


def _topk_mask_T(imp, n_top):
    ns, w = imp.shape
    nblk = ns // SUBLANES
    blocks = [imp[r * SUBLANES:(r + 1) * SUBLANES, :] for r in range(nblk)]
    cnts = [jnp.zeros((SUBLANES, w), F32) for _ in range(nblk)]
    sub = lax.broadcasted_iota(jnp.int32, (SUBLANES, w), 0)
    for sp in range(ns):
        row = blocks[sp // SUBLANES][sp % SUBLANES:sp % SUBLANES + 1, :]
        for r in range(nblk):
            blk = blocks[r]
            if sp < r * SUBLANES:
                beats = jnp.where(row >= blk, 1.0, 0.0)
            elif sp >= (r + 1) * SUBLANES:
                beats = jnp.where(row > blk, 1.0, 0.0)
            else:
                beats = jnp.where(sub > (sp - r * SUBLANES), jnp.where(row >= blk, 1.0, 0.0),
                                  jnp.where(row > blk, 1.0, 0.0))
            cnts[r] = cnts[r] + beats
    return jnp.concatenate([jnp.where(c < n_top, 1.0, 0.0) for c in cnts], axis=0)


def _online_chunks(states, k_cs, vT_cs, qTs, bias):
    scores = [_dot(k_c, qT) for k_c, qT in zip(k_cs, qTs)]
    mids = []
    for (m, _), s in zip(states, scores):
        if bias is not None:
            s = s + bias
        m_new = jnp.maximum(m, jnp.max(s, axis=0, keepdims=True))
        mids.append((m_new, jnp.exp2(m - m_new), jnp.exp2(s - m_new).astype(BF16)))
    return tuple((m_new, alpha * acc + _dot(vT_c, p))
                 for (m_new, alpha, p), (_, acc), vT_c in zip(mids, states, vT_cs))


def _softmax_finish(carry):
    _, acc = carry
    return acc[0:HEAD_DIM, :] * (1.0 / jnp.maximum(acc[HEAD_DIM:HEAD_DIM + 1, :], 1e-20))


LAZY_LOG2_MAX = 60.0
N_CONST_LANES = 4


def _split3(x):
    hi = x.astype(BF16).astype(F32)
    r = x - hi
    mid = r.astype(BF16).astype(F32)
    return hi, mid, (r - mid).astype(BF16).astype(F32)


def _rescaling_chunks(m_ref, acc_ref, k_cs, vT_cs, qTs, base_tiles, bias):
    nh = len(k_cs)
    rows, w = base_tiles[0].shape
    zero_rows = jnp.zeros((LANES - HEAD_DIM - rows, w), BF16)
    new = _online_chunks(tuple((m_ref[h], acc_ref[h]) for h in range(nh)), k_cs, vT_cs,
                         [jnp.concatenate([qTs[h], base_tiles[h].astype(BF16), zero_rows], axis=0) for h in range(nh)],
                         bias)
    for h in range(nh):
        m_ref[h] = new[h][0]
        acc_ref[h] = new[h][1]


def _lazy_chunks(m_ref, acc_ref, k_cs, vT_cs, qTs, base_tiles, bias, ref_row):
    nh = len(k_cs)
    rows = base_tiles[0].shape[0]
    w = base_tiles[0].shape[1]
    rowt = lax.broadcasted_iota(jnp.int32, (rows, w), 0)
    zero_rows = jnp.zeros((LANES - HEAD_DIM - rows, w), BF16)

    def queries(h, with_reference):
        tile = base_tiles[h]
        if with_reference:
            hi, mid, lo = _split3(-m_ref[h])
            tile = jnp.where(rowt == ref_row, hi, jnp.where(rowt == ref_row + 1, mid,
                                                            jnp.where(rowt == ref_row + 2, lo, tile)))
        return jnp.concatenate([qTs[h], tile.astype(BF16), zero_rows], axis=0)

    scores = [_dot(k_cs[h], queries(h, True)) for h in range(nh)]
    peak = None
    probs = []
    for s in scores:
        if bias is not None:
            s = s + bias
        top = jnp.max(s, axis=0, keepdims=True)
        peak = top if peak is None else jnp.maximum(peak, top)
        probs.append(jnp.exp2(s).astype(BF16))
    pvs = [_dot(vT_cs[h], probs[h]) for h in range(nh)]
    in_range = jnp.max(peak) <= LAZY_LOG2_MAX

    @pl.when(in_range)
    def _():
        for h in range(nh):
            acc_ref[h] = acc_ref[h] + pvs[h]

    @pl.when(jnp.logical_not(in_range))
    def _():
        _rescaling_chunks(m_ref, acc_ref, k_cs, vT_cs, qTs, base_tiles, bias)


def _attn_prompt_kernel(qT_ref, qrT_ref, gT_ref, kc_ref, vcT_ref, ks_ref, vsT_ref, kw_ref, vwT_ref, ovT_ref,
                        o_ref, selb_ref, oc_ref, ow_ref, m_ref, acc_ref, *, tq, nc, n_top):
    ck = tq
    i = pl.program_id(1)
    t0 = i * tq
    qpos = t0 + lax.broadcasted_iota(jnp.int32, (1, tq), 1)
    ncp = kc_ref.shape[2]
    ns = ovT_ref.shape[0]
    w = GROUP * tq
    bpc = ck // SEL_BLOCK
    sel_shift = SEL_BLOCK.bit_length() - 1
    kvhs = range(N_KV_HEADS)
    heads = [[kvh * GROUP + g for g in range(GROUP)] for kvh in kvhs]
    rows = [pl.ds(kvh * HEAD_DIM, HEAD_DIM) for kvh in kvhs]
    vrows = [pl.ds(kvh * V_ROWS, V_ROWS) for kvh in kvhs]
    qrT = [jnp.concatenate([qrT_ref[h * HEAD_DIM:(h + 1) * HEAD_DIM, :] for h in heads[kvh]], axis=1)
           for kvh in kvhs]
    kl = lax.broadcasted_iota(jnp.int32, (ck, tq), 0)
    ql = lax.broadcasted_iota(jnp.int32, (ck, tq), 1)
    tile4 = lambda b: jnp.concatenate([b] * GROUP, axis=1)
    key_le_query = tile4(jnp.where(kl <= ql, 0.0, MASKED))
    key_ge_query = tile4(jnp.where(kl >= ql, 0.0, MASKED))
    bias_rows = selb_ref.shape[2]

    def reset_softmax():
        for kvh in kvhs:
            m_ref[kvh] = jnp.full((1, w), NEG, F32)
            acc_ref[kvh] = jnp.zeros((V_ROWS, w), F32)

    cidx = lax.broadcasted_iota(jnp.int32, (ncp, 1), 0)
    valid = ((cidx * CMP_STRIDE + (CMP_BLOCK - 1)) <= qpos) & (cidx < nc)
    sidx = lax.broadcasted_iota(jnp.int32, (ns, 1), 0)
    cur = lax.shift_right_logical(qpos, sel_shift)
    forced = (sidx == 0) | (sidx == cur) | (sidx == cur - 1)
    causal = (sidx * SEL_BLOCK) <= qpos
    cmp_scores = [
        _dot(kc_ref[0, kvh], jnp.concatenate([qT_ref[h * HEAD_DIM:(h + 1) * HEAD_DIM, :] for h in heads[kvh]], axis=1))
        for kvh in kvhs]
    for kvh in kvhs:
        s = cmp_scores[kvh]
        probs = []
        for g in range(GROUP):
            sm = jnp.where(valid, s[:, g * tq:(g + 1) * tq], NEG)
            mx = jnp.max(sm, axis=0, keepdims=True)
            e = jnp.where(valid, jnp.exp2(sm - mx), 0.0)
            den = jnp.maximum(jnp.sum(e, axis=0, keepdims=True), 1e-20)
            probs.append(e / den)
        oc_ref[kvh] = _dot(vcT_ref[0, rows[kvh], :], jnp.concatenate(probs, axis=1).astype(BF16))
        psum = probs[0]
        for g in range(1, GROUP):
            psum = psum + probs[g]
        imp = _dot_f32lhs(ovT_ref[...], psum)
        imp = jnp.where(forced, FORCE, jnp.where(causal, imp, NEG))
        selb = (_topk_mask_T(imp, n_top) - 1.0) * (-MASKED)
        fill = jnp.zeros((bias_rows - bpc, tq), F32)
        for c in range(ns // bpc):
            selb_ref[kvh, c] = jnp.concatenate([selb[c * bpc:(c + 1) * bpc, :], fill], axis=0)

    n_back = WINDOW // ck
    reset_softmax()
    rowb = lax.broadcasted_iota(jnp.int32, (bias_rows, w), 0)
    for r in range(n_back + 1):
        a = i - n_back + r
        kst = pl.multiple_of(jnp.maximum(a, 0) * ck, ck)
        skip = jnp.where(a < 0, MASKED, 0.0)
        tile = jnp.where(rowb == bpc, skip, 0.0)
        bias = key_le_query if r == n_back else (key_ge_query if r == 0 else None)
        k_cs = [kw_ref[kvh, pl.ds(kst, ck), :] for kvh in kvhs]
        vT_cs = [vwT_ref[vrows[kvh], pl.ds(kst, ck)] for kvh in kvhs]
        if r == 0:
            _rescaling_chunks(m_ref, acc_ref, k_cs, vT_cs, qrT, [tile] * N_KV_HEADS, bias)
        else:
            _lazy_chunks(m_ref, acc_ref, k_cs, vT_cs, qrT, [tile] * N_KV_HEADS, bias, bpc + 1)
    for kvh in kvhs:
        ow_ref[kvh] = _softmax_finish((None, acc_ref[kvh]))

    def chunk_step(c, diagonal, lazy=True):
        kst = c * ck if isinstance(c, int) else pl.multiple_of(c * ck, ck)
        args = (m_ref, acc_ref, [ks_ref[kvh, pl.ds(kst, ck), :] for kvh in kvhs],
                [vsT_ref[vrows[kvh], pl.ds(kst, ck)] for kvh in kvhs], qrT,
                [tile4(selb_ref[kvh, c]) for kvh in kvhs], key_le_query if diagonal else None)
        if lazy:
            _lazy_chunks(*args, bpc + 1)
        else:
            _rescaling_chunks(*args)

    reset_softmax()

    @pl.when(i > 0)
    def _():
        chunk_step(0, False, lazy=False)

    def loop_body(c, carry):
        chunk_step(c, False)
        return carry

    lax.fori_loop(1, i, loop_body, 0)
    chunk_step(i, True)

    for kvh in kvhs:
        def gate_row(j, kvh=kvh):
            return jnp.concatenate([gT_ref[h * 3 + j:h * 3 + j + 1, :] for h in heads[kvh]], axis=1)

        oT = (gate_row(0) * oc_ref[kvh] + gate_row(1) * _softmax_finish((None, acc_ref[kvh]))
              + gate_row(2) * ow_ref[kvh])
        for g, h in enumerate(heads[kvh]):
            o_ref[h * HEAD_DIM:(h + 1) * HEAD_DIM, :] = oT[:, g * tq:(g + 1) * tq].astype(BF16)


def _attn_prompt(qT, qrT, gT, kc, vcT, ksb, vsT, kwb, vwT, ovT, *, batch, seq, tq, nc, n_top):
    nq = seq // tq
    nsb = kc.shape[2]
    ns = ovT.shape[0]
    col = lambda h: pl.BlockSpec((h, tq), lambda b, i: (0, b * nq + i))
    kh = pl.BlockSpec((N_KV_HEADS, seq, LANES), lambda b, i: (0, b, 0))
    vt = pl.BlockSpec((N_KV_HEADS * V_ROWS, seq), lambda b, i: (0, b))
    bf16_sublanes = 2 * SUBLANES
    return pl.pallas_call(
        functools.partial(_attn_prompt_kernel, tq=tq, nc=nc, n_top=n_top),
        grid=(batch, nq),
        in_specs=[col(Q_W), col(Q_W), col(LANES),
                  pl.BlockSpec((1, N_KV_HEADS, nsb, HEAD_DIM), lambda b, i: (b, 0, 0, 0)),
                  pl.BlockSpec((1, KV_W, nsb), lambda b, i: (b, 0, 0)),
                  kh, vt, kh, vt, _full(ovT.shape)],
        out_specs=col(Q_W),
        out_shape=jax.ShapeDtypeStruct((Q_W, batch * seq), BF16),
        scratch_shapes=[pltpu.VMEM((N_KV_HEADS, ns * SEL_BLOCK // tq, bf16_sublanes, tq), F32),
                        pltpu.VMEM((N_KV_HEADS, HEAD_DIM, GROUP * tq), F32),
                        pltpu.VMEM((N_KV_HEADS, HEAD_DIM, GROUP * tq), F32),
                        pltpu.VMEM((N_KV_HEADS, 1, GROUP * tq), F32),
                        pltpu.VMEM((N_KV_HEADS, V_ROWS, GROUP * tq), F32)],
        compiler_params=_params("parallel", "arbitrary"),
        name="nsa_attn_prompt",
    )(qT, qrT, gT, kc, vcT, ksb, vsT, kwb, vwT, ovT)


def _resident(shape):
    zeros = (0,) * len(shape)
    return pl.BlockSpec(shape, lambda *_: zeros, pipeline_mode=pl.Buffered(1))


def _layer_tail_kernel(*refs, mixer, transposed, ff_chunk, next_norm):
    if mixer == "nsa":
        o_ref, wo_ref, x_ref, g1_ref, g2_ref = refs[:5]
        rest = refs[5:]
        if transposed:
            y = lax.dot_general(o_ref[...], wo_ref[...], (((0,), (0,)), ((), ())), preferred_element_type=F32)
        else:
            y = _dot(o_ref[...], wo_ref[...])
    else:
        yc_ref, u_ref, d_ref, wg_ref, bg_ref, x_ref, g1_ref, g2_ref = refs[:8]
        rest = refs[8:]
        z = jax.nn.gelu(yc_ref[...] + d_ref[...] * u_ref[...])
        y = z * jax.nn.sigmoid(_dot(z.astype(BF16), wg_ref[...]) + bg_ref[...])
    w1_ref, w2_ref, g3_ref = rest[:3]
    if next_norm:
        gn_ref, h2_ref, xn_ref, acc_ref = rest[3:]
    else:
        h2_ref, acc_ref = rest[3:]
    h = x_ref[...] + _rms(y, g1_ref[...])
    xm = _rms(h, g2_ref[...]).astype(BF16)
    for c in range(w1_ref.shape[1] // ff_chunk):
        cols = slice(c * ff_chunk, (c + 1) * ff_chunk)
        hm = jnp.maximum(_dot(xm, w1_ref[:, cols]), 0.0)
        part = _dot((hm * hm).astype(BF16), w2_ref[cols, :])
        if c == 0:
            acc_ref[...] = part
        else:
            acc_ref[...] += part
    h2 = h + _rms(acc_ref[...], g3_ref[...])
    h2_ref[...] = h2
    if next_norm:
        xn_ref[...] = _rms(h2, gn_ref[...])


def _layer_tail(mixer_args, x, g1, g2, w1, w2, g3, gn, *, mixer, tm, ff_chunk, transposed=False):
    rows, d = x.shape
    row = pl.BlockSpec((tm, d), lambda i: (i, 0))
    vec = _resident((1, d))
    if mixer == "nsa":
        o, w_o = mixer_args
        o_spec = (pl.BlockSpec((Q_W, tm), lambda i: (0, i)) if transposed
                  else pl.BlockSpec((tm, Q_W), lambda i: (i, 0)))
        head_specs = [o_spec, _resident(w_o.shape)]
    else:
        w_glu = mixer_args[3]
        head_specs = [row, row, vec, _resident(w_glu.shape), vec]
    tail_args = (w1, w2, g3) + (() if gn is None else (gn,))
    tail_specs = [_resident(w1.shape), _resident(w2.shape), vec] + ([] if gn is None else [vec])
    n_out = 1 if gn is None else 2
    return pl.pallas_call(
        functools.partial(_layer_tail_kernel, mixer=mixer, transposed=transposed, ff_chunk=ff_chunk,
                          next_norm=gn is not None),
        grid=(rows // tm,),
        in_specs=head_specs + [row, vec, vec] + tail_specs,
        out_specs=[row] * n_out,
        out_shape=[jax.ShapeDtypeStruct((rows, d), F32)] * n_out,
        scratch_shapes=[pltpu.VMEM((tm, d), F32)],
        compiler_params=_params("parallel"),
        name=mixer + "_layer_tail",
    )(*mixer_args, x, g1, g2, *tail_args)


def _s5_seq_kernel(x_ref, brow_ref, pre_ref, pim_ref, qre_ref, qim_ref, are_ref, aim_ref, y_ref, hfin_ref,
                   sre, sim, hre, him, cre, cim):
    nb, tb, uw = x_ref.shape
    nt = pre_ref.shape[1] // uw
    nk = tb // nt

    @pl.when(pl.program_id(1) == 0)
    def _():
        cre[...] = jnp.zeros(cre.shape, F32)
        cim[...] = jnp.zeros(cim.shape, F32)

    u = [jnp.concatenate([x_ref[b, pl.ds(t, nk, stride=nt), :] for b in range(nb)], axis=0).astype(BF16)
         for t in range(nt)]
    u2 = [jnp.concatenate([u[2 * j], u[2 * j + 1]], axis=1) for j in range(nt // 2)]
    s_re = _dot(u2[0], pre_ref[0, 0:2 * uw, :])
    s_im = _dot(u2[0], pim_ref[0, 0:2 * uw, :])
    for j in range(1, nt // 2):
        s_re = s_re + _dot(u2[j], pre_ref[0, 2 * j * uw:(2 * j + 2) * uw, :])
        s_im = s_im + _dot(u2[j], pim_ref[0, 2 * j * uw:(2 * j + 2) * uw, :])
    sre[...] = s_re
    sim[...] = s_im
    ar = are_ref[0]
    ai = aim_ref[0]

    def body(it, carry):
        out = []
        for b in range(nb):
            hr, hi = carry[b]
            r0 = pl.multiple_of(b * nk + it * SUBLANES, SUBLANES)
            sr8 = sre[pl.ds(r0, SUBLANES), :]
            si8 = sim[pl.ds(r0, SUBLANES), :]
            prev_r, prev_i = [], []
            for j in range(SUBLANES):
                prev_r.append(hr)
                prev_i.append(hi)
                hr, hi = (ar * hr - ai * hi + sr8[j:j + 1, :], ar * hi + ai * hr + si8[j:j + 1, :])
            hre[pl.ds(r0, SUBLANES), :] = jnp.concatenate(prev_r, axis=0)
            him[pl.ds(r0, SUBLANES), :] = jnp.concatenate(prev_i, axis=0)
            out.append((hr, hi))
        return tuple(out)

    init = tuple((cre[b:b + 1, :], cim[b:b + 1, :]) for b in range(nb))
    fin = lax.fori_loop(0, nk // SUBLANES, body, init)
    for b in range(nb):
        cre[b:b + 1, :] = fin[b][0]
        cim[b:b + 1, :] = fin[b][1]
    hfin_ref[0, 0] = cre[...]
    hfin_ref[0, 1] = cim[...]

    hb_re = hre[...].astype(BF16)
    hb_im = him[...].astype(BF16)
    for t2 in range(0, nt, 2):
        cols = slice(t2 * uw, (t2 + 2) * uw)
        acc = _dot(hb_re, qre_ref[0, :, cols]) + _dot(hb_im, qim_ref[0, :, cols])
        for j in range(t2 // 2 + 1):
            lag0 = t2 - 2 * j + 1
            wpair = jnp.concatenate([brow_ref[0, :, lag0 * uw:(lag0 + 2) * uw],
                                     brow_ref[0, :, (lag0 - 1) * uw:(lag0 + 1) * uw]], axis=0)
            acc = acc + _dot(u2[j], wpair)
        for b in range(nb):
            for dt in range(2):
                y_ref[b, pl.ds(t2 + dt, nk, stride=nt), :] = acc[b * nk:(b + 1) * nk, dt * uw:(dt + 1) * uw]


def _s5_seq(x3, ops):
    nb, t, d = x3.shape
    uw = SSM_UNIT * SSM_GROUP
    nunit = d // uw
    sw = ops["p_re"].shape[2]
    tb = _tile(t, S5_TIME_BLOCK)
    nk = tb // SSM_CHUNK
    per_unit = lambda a: pl.BlockSpec((1,) + a.shape[1:], lambda i, r: (i,) + (0,) * (a.ndim - 1))
    blk = pl.BlockSpec((nb, tb, uw), lambda i, r: (0, r, i))
    args = (x3, ops["brow"], ops["p_re"], ops["p_im"], ops["q_re"], ops["q_im"], ops["a_chunk_re"], ops["a_chunk_im"])
    return pl.pallas_call(
        _s5_seq_kernel,
        grid=(nunit, t // tb),
        in_specs=[blk] + [per_unit(a) for a in args[1:]],
        out_specs=[blk, pl.BlockSpec((1, 2, nb, sw), lambda i, r: (i, 0, 0, 0))],
        out_shape=[jax.ShapeDtypeStruct((nb, t, d), F32), jax.ShapeDtypeStruct((nunit, 2, nb, sw), F32)],
        scratch_shapes=[pltpu.VMEM((nb * nk, sw), F32)] * 4 + [pltpu.VMEM((nb, sw), F32)] * 2,
        compiler_params=_params("parallel", "arbitrary"),
        name="s5_seq_scan",
    )(*args)


def _s5_step_kernel(u_ref, h0r_ref, h0i_ref, bre_ref, bim_ref, cre_ref, cim_ref, are_ref, aim_ref,
                    y_ref, hr_ref, hi_ref, *, npair):
    for p in range(npair):
        u = u_ref[p]
        ar = are_ref[p]
        ai = aim_ref[p]
        h0r = h0r_ref[p]
        h0i = h0i_ref[p]
        hr = ar * h0r - ai * h0i + _dot(u, bre_ref[p])
        hi = ar * h0i + ai * h0r + _dot(u, bim_ref[p])
        hr_ref[p] = hr
        hi_ref[p] = hi
        y_ref[p] = _dot(hr.astype(BF16), cre_ref[p]) + _dot(hi.astype(BF16), cim_ref[p])


def _s5_step(u2, h0r, h0i, ops):
    npair, rows, width = u2.shape
    args = (u2, h0r, h0i, ops["b1_re"], ops["b1_im"], ops["c1_re"], ops["c1_im"], ops["a1_re"], ops["a1_im"])
    return pl.pallas_call(
        functools.partial(_s5_step_kernel, npair=npair),
        grid=(1,),
        in_specs=[_full(a.shape) for a in args],
        out_specs=[_full((npair, rows, width)), _full(h0r.shape), _full(h0r.shape)],
        out_shape=[jax.ShapeDtypeStruct((npair, rows, width), F32), jax.ShapeDtypeStruct(h0r.shape, F32),
                   jax.ShapeDtypeStruct(h0r.shape, F32)],
        compiler_params=_params("arbitrary"),
        name="s5_single_step",
    )(*args)


def _s5_operators(a_re, a_im, log_dt, b_re, b_im, c_re, c_im):
    hp = lax.Precision.HIGHEST
    g, n = a_re.shape
    gu = SSM_UNIT
    nunit = g // gu
    L = SSM_CHUNK
    uw = gu * SSM_GROUP
    sw = gu * n
    a = lax.complex(a_re.astype(F32), a_im.astype(F32))
    dt = jnp.exp(log_dt.astype(F32))[:, None]
    a_bar = jnp.exp(a * dt)
    b_bar = ((a_bar - 1.0) / a)[:, :, None] * lax.complex(b_re.astype(F32), b_im.astype(F32))
    c = lax.complex(c_re.astype(F32), c_im.astype(F32))
    pows = [jnp.ones_like(a_bar)]
    for _ in range(L):
        pows.append(pows[-1] * a_bar)
    a_pow = jnp.stack(pows).reshape(L + 1, nunit, sw)
    apr = jnp.real(a_pow)
    api = jnp.imag(a_pow)
    eye = jnp.eye(gu, dtype=F32)

    def bd_in(x):
        return jnp.einsum("pgnd,gh->pgdhn", x.reshape(nunit, gu, n, SSM_GROUP), eye).reshape(nunit, uw, sw)

    def bd_out(x):
        return jnp.einsum("pgcn,gh->pgnhc", x.reshape(nunit, gu, SSM_GROUP, n), eye).reshape(nunit, sw, uw)

    p0r, p0i = bd_in(jnp.real(b_bar)), bd_in(jnp.imag(b_bar))
    q0r, q0i = bd_out(jnp.real(c)), bd_out(jnp.imag(c))
    lanes = lambda x, k: x[k][:, None, :]
    rows_ = lambda x, k: x[k][:, :, None]
    p_re = jnp.concatenate([p0r * lanes(apr, L - 1 - t) - p0i * lanes(api, L - 1 - t) for t in range(L)], axis=1)
    p_im = jnp.concatenate([p0r * lanes(api, L - 1 - t) + p0i * lanes(apr, L - 1 - t) for t in range(L)], axis=1)
    q_re = jnp.concatenate([q0r * rows_(apr, t + 1) - q0i * rows_(api, t + 1) for t in range(L)], axis=2)
    q_im = jnp.concatenate([-(q0r * rows_(api, t + 1) + q0i * rows_(apr, t + 1)) for t in range(L)], axis=2)
    lag_blocks = [jnp.zeros((nunit, uw, uw), F32)]
    for t in range(L):
        xr = p0r * lanes(apr, t) - p0i * lanes(api, t)
        xi = p0r * lanes(api, t) + p0i * lanes(apr, t)
        lag_blocks.append(jnp.einsum("pus,psv->puv", xr, q0r, precision=hp)
                          - jnp.einsum("pus,psv->puv", xi, q0i, precision=hp))
    brow = jnp.concatenate(lag_blocks, axis=2)
    lane_row = lambda x: x.reshape(nunit, 1, sw)
    return {
        "brow": brow.astype(BF16),
        "p_re": p_re.astype(BF16), "p_im": p_im.astype(BF16),
        "q_re": q_re.astype(BF16), "q_im": q_im.astype(BF16),
        "a_chunk_re": lane_row(apr[L]), "a_chunk_im": lane_row(api[L]),
        "a1_re": lane_row(apr[1]), "a1_im": lane_row(api[1]),
        "b1_re": p0r.astype(BF16), "b1_im": p0i.astype(BF16),
        "c1_re": q0r.astype(BF16), "c1_im": (-q0i).astype(BF16),
    }


def _compress_paged_kernel(pt_ref, *refs, pg, nsb):
    del pt_ref
    pages = refs[:pg]
    perm_ref, wbd_ref, pe_ref, w2_ref, out_ref, h0_ref, h1_ref = refs[pg:]
    j = pl.program_id(1)
    sbp = PAGE_SIZE // CMP_STRIDE
    nrows = pg * sbp
    pair_rows = 2 * sbp

    @pl.when(j == 0)
    def _():
        h1_ref[:, nsb:nsb + SUBLANES, :] = jnp.zeros((2, SUBLANES, KV_W), F32)

    r0 = pl.multiple_of(j * nrows, nrows)
    for kv in range(2):
        staged = []
        for q in range(pg // 2):
            z = jnp.concatenate([pages[2 * q][0, kv], pages[2 * q + 1][0, kv]], axis=1).astype(BF16)
            staged.append(_dot_nt(perm_ref[...], z).astype(BF16))

        def load_rows(s, kv, staged=staged):
            return jnp.concatenate([x[s * pair_rows:(s + 1) * pair_rows, :] for x in staged], axis=0)

        pr0, pr1 = _compress_half(load_rows, kv, nrows, wbd_ref, pe_ref)
        h0_ref[kv, pl.ds(r0, nrows), :] = pr0
        h1_ref[kv, pl.ds(r0, nrows), :] = pr1

    @pl.when(j == pl.num_programs(1) - 1)
    def _():
        for kv in range(2):
            h = h0_ref[kv] + h1_ref[kv, pl.ds(1, nsb), :]
            out = _dot(jax.nn.gelu(h).astype(BF16), w2_ref[kv])
            out_ref[0, :, kv * KV_W:(kv + 1) * KV_W] = out.astype(BF16)


def _page_specs(pg, pages_per_sample):
    def spec(i):
        return pl.BlockSpec((1, 2, KV_W, PAGE_SIZE),
                            lambda b, j, pt: (pt[b * pages_per_sample + j * pg + i], 0, 0, 0))
    return [spec(i) for i in range(pg)]


def _compress_paged(pages, pt_flat, wbd, pe_t, w2bd, *, nb, pages_per_sample, pg):
    nsb = pages_per_sample * PAGE_SIZE // CMP_STRIDE
    sbp = PAGE_SIZE // CMP_STRIDE
    i_out = np.arange(2 * PAGE_SIZE)
    s_i, pg_i, n_i = i_out // (2 * sbp), (i_out // sbp) % 2, i_out % sbp
    perm = jnp.asarray(i_out[None, :] == (pg_i * PAGE_SIZE + n_i * CMP_STRIDE + s_i)[:, None], BF16)
    c3 = lambda shape: pl.BlockSpec(shape, lambda b, j, pt: (0,) * len(shape))
    grid_spec = pltpu.PrefetchScalarGridSpec(
        num_scalar_prefetch=1,
        grid=(nb, pages_per_sample // pg),
        in_specs=_page_specs(pg, pages_per_sample) + [c3(perm.shape), c3(wbd.shape), c3(pe_t.shape), c3(w2bd.shape)],
        out_specs=pl.BlockSpec((1, nsb, 2 * KV_W), lambda b, j, pt: (b, 0, 0)),
        scratch_shapes=[pltpu.VMEM((2, nsb, KV_W), F32), pltpu.VMEM((2, nsb + SUBLANES, KV_W), F32)],
    )
    return pl.pallas_call(
        functools.partial(_compress_paged_kernel, pg=pg, nsb=nsb),
        grid_spec=grid_spec,
        out_shape=jax.ShapeDtypeStruct((nb, nsb, 2 * KV_W), BF16),
        compiler_params=_params("parallel", "arbitrary"),
        name="nsa_compress_paged",
    )(pt_flat, *([pages] * pg), perm, wbd, pe_t, w2bd)


def _topk_mask_lanes(imp, n_top, ns_valid):
    lane = lax.broadcasted_iota(jnp.int32, imp.shape, 1)
    cnt = jnp.zeros(imp.shape, F32)
    for sp in range(ns_valid):
        col = imp[:, sp:sp + 1]
        cnt = cnt + jnp.where(lane > sp, jnp.where(col >= imp, 1.0, 0.0), jnp.where(col > imp, 1.0, 0.0))
    return jnp.where((cnt < n_top) & (lane < ns_valid), 1.0, 0.0)


def _dot_nt(a, b):
    return lax.dot_general(a, b, (((1,), (1,)), ((), ())), preferred_element_type=F32)


def _attn_sample_kernel(pt_ref, *refs, pg, ns, past, nc, ns_valid, n_top):
    del pt_ref
    all_pages = [refs[u * pg:(u + 1) * pg] for u in range(ns)]
    (q_ref, qr_ref, g_ref, kcvc_ref, ksn_ref, kwn_ref, win_ref, ov_ref, e_ref, gs_ref, o_ref,
     m_ref, l_ref, acc_ref, sel_ref, oc_ref, ow_ref) = refs[ns * pg:]
    j = pl.program_id(1)
    ncp = kcvc_ref.shape[1]
    nsp = ov_ref.shape[1]
    wlen = win_ref.shape[3]
    row = lax.broadcasted_iota(jnp.int32, (N_HEADS, KV_W), 0)
    lane = lax.broadcasted_iota(jnp.int32, (N_HEADS, KV_W), 1)
    own = (lane // HEAD_DIM) == (row // GROUP)

    def spread(ref, u):
        q = ref[u]
        return jnp.where(own, jnp.concatenate([q] * N_KV_HEADS, axis=1), jnp.zeros((N_HEADS, KV_W), BF16))

    def update(state, s, vs):
        m, l, acc = state
        m_new = jnp.maximum(m, jnp.max(s, axis=1, keepdims=True))
        alpha = jnp.exp(m - m_new)
        p = jnp.exp(s - m_new)
        l = alpha * l + jnp.sum(p, axis=1, keepdims=True)
        pv = None
        for st, sz, v, feature_major in vs:
            pb = p[:, st:st + sz].astype(BF16)
            t = _dot_nt(pb, v) if feature_major else _dot(pb, v)
            pv = t if pv is None else pv + t
        return m_new, l, alpha * acc + pv

    def init():
        return (jnp.full((N_HEADS, 1), NEG, F32), jnp.zeros((N_HEADS, 1), F32), jnp.zeros((N_HEADS, KV_W), F32))

    def new_row_update(state, qbd, new_row):
        r8 = lax.broadcasted_iota(jnp.int32, (SUBLANES, 2 * KV_W), 0)
        tile = jnp.where(r8 == 0, jnp.broadcast_to(new_row, (SUBLANES, 2 * KV_W)), 0.0).astype(BF16)
        s = _dot_nt(qbd, tile[:, 0:KV_W])
        l8 = lax.broadcasted_iota(jnp.int32, (N_HEADS, SUBLANES), 1)
        s = jnp.where(l8 == 0, s, MASKED)
        return update(state, s, [(0, SUBLANES, tile[:, KV_W:2 * KV_W], False)])

    def put(u, state):
        m, l, acc = state
        m_ref[u] = jnp.broadcast_to(m, m_ref.shape[1:])
        l_ref[u] = jnp.broadcast_to(l, l_ref.shape[1:])
        acc_ref[u] = acc

    qrbd = [spread(qr_ref, u) for u in range(ns)]

    @pl.when(j == 0)
    def _():
        cidx = lax.broadcasted_iota(jnp.int32, (1, ncp), 1)
        valid = ((cidx * CMP_STRIDE + (CMP_BLOCK - 1)) <= past) & (cidx < nc)
        sidx = lax.broadcasted_iota(jnp.int32, (1, nsp), 1)
        cur = past // SEL_BLOCK
        forced = (sidx == 0) | (sidx == cur) | (sidx == cur - 1)
        causal = (sidx * SEL_BLOCK) <= past
        wpos = past - wlen + lax.broadcasted_iota(jnp.int32, (1, wlen), 1)
        in_window = (wpos >= 0) & (past - wpos <= WINDOW)
        for u in range(ns):
            s = _dot_nt(spread(q_ref, u), kcvc_ref[u, :, 0:KV_W])
            sm = jnp.where(valid, s, NEG)
            mx = jnp.max(sm, axis=1, keepdims=True)
            e = jnp.where(valid, jnp.exp(sm - mx), 0.0)
            p = e / jnp.maximum(jnp.sum(e, axis=1, keepdims=True), 1e-20)
            oc_ref[u] = _dot(p.astype(BF16), kcvc_ref[u, :, KV_W:2 * KV_W])
            imp = _dot_f32rhs(_dot_f32lhs(gs_ref[...], p), ov_ref[...])
            imp = jnp.where(forced, FORCE, jnp.where(causal, imp, NEG))
            imp = jnp.where(sidx < ns_valid, imp, MASKED)
            sel_ref[u] = _topk_mask_lanes(imp, n_top, ns_valid)
            sw = jnp.where(in_window, _dot(qrbd[u], win_ref[u, 0].astype(BF16)), MASKED)
            st = update(init(), sw, [(0, wlen, win_ref[u, 1].astype(BF16), True)])
            st = new_row_update(st, qrbd[u], kwn_ref[u])
            ow_ref[u] = st[2] * (1.0 / jnp.maximum(st[1], 1e-20))
            put(u, new_row_update(init(), qrbd[u], ksn_ref[u]))

    def halves(pages):
        return [pages[:pg // 2], pages[pg // 2:]] if pg > 1 else [pages]

    scores = [[jnp.concatenate([_dot(qrbd[u], r[0, 0].astype(BF16)) for r in part], axis=1)
               for part in halves(all_pages[u])] for u in range(ns)]
    masks = [_dot(sel_ref[u].astype(BF16), e_ref[0]) > 0.5 for u in range(ns)]
    states = [(m_ref[u, :, 0:1], l_ref[u, :, 0:1], acc_ref[u]) for u in range(ns)]
    k0 = 0
    for hx, part0 in enumerate(halves(all_pages[0])):
        width = len(part0) * PAGE_SIZE
        for u in range(ns):
            part = halves(all_pages[u])[hx]
            s = jnp.where(masks[u][:, k0:k0 + width], scores[u][hx], MASKED)
            vs = [(i * PAGE_SIZE, PAGE_SIZE, r[0, 1].astype(BF16), True) for i, r in enumerate(part)]
            states[u] = update(states[u], s, vs)
        k0 += width
    for u in range(ns):
        put(u, states[u])

    @pl.when(j == pl.num_programs(1) - 1)
    def _():
        for u in range(ns):
            g = g_ref[u]
            os_ = acc_ref[u] * (1.0 / jnp.maximum(l_ref[u, :, 0:1], 1e-20))
            o = g[:, 0:1] * oc_ref[u] + g[:, 1:2] * os_ + g[:, 2:3] * ow_ref[u]
            o = jnp.where(own, o, 0.0)
            out = o[:, 0:HEAD_DIM]
            for h in range(1, N_KV_HEADS):
                out = out + o[:, h * HEAD_DIM:(h + 1) * HEAD_DIM]
            o_ref[u] = out.astype(BF16)


def _attn_sample(pages, pt_flat, q3, qr3, g3, kcvc, ks_new, kw_new, win, ov, e_mat, gsum, *, nb, pages_per_sample,
                 pg, past, nc, ns_valid, n_top):
    nsp = ov.shape[1]
    ns = SAMPLES_PER_STEP if nb % SAMPLES_PER_STEP == 0 else 1
    per_b = lambda shape: pl.BlockSpec((ns,) + shape, lambda b, j, pt: (b,) + (0,) * len(shape))
    const = lambda shape: pl.BlockSpec(shape, lambda b, j, pt: (0,) * len(shape))

    def page_spec(u, i):
        return pl.BlockSpec((1, 2, KV_W, PAGE_SIZE),
                            lambda b, j, pt: (pt[(b * ns + u) * pages_per_sample + j * pg + i], 0, 0, 0))

    per_sample = lambda *shape: pltpu.VMEM((ns,) + shape, F32)
    grid_spec = pltpu.PrefetchScalarGridSpec(
        num_scalar_prefetch=1,
        grid=(nb // ns, pages_per_sample // pg),
        in_specs=[page_spec(u, i) for u in range(ns) for i in range(pg)] + [
            per_b((N_HEADS, HEAD_DIM)), per_b((N_HEADS, HEAD_DIM)), per_b((N_HEADS, 3)),
            per_b(kcvc.shape[1:]), per_b((1, 2 * KV_W)), per_b((1, 2 * KV_W)), per_b(win.shape[1:]),
            const(ov.shape), pl.BlockSpec((1,) + e_mat.shape[1:], lambda b, j, pt: (j, 0, 0)), const(gsum.shape)],
        out_specs=per_b((N_HEADS, HEAD_DIM)),
        scratch_shapes=[per_sample(N_HEADS, LANES), per_sample(N_HEADS, LANES), per_sample(N_HEADS, KV_W),
                        per_sample(N_HEADS, nsp), per_sample(N_HEADS, KV_W), per_sample(N_HEADS, KV_W)],
    )
    return pl.pallas_call(
        functools.partial(_attn_sample_kernel, pg=pg, ns=ns, past=past, nc=nc, ns_valid=ns_valid, n_top=n_top),
        grid_spec=grid_spec,
        out_shape=jax.ShapeDtypeStruct((nb, N_HEADS, HEAD_DIM), BF16),
        compiler_params=_params("parallel", "arbitrary"),
        name="nsa_attn_sample",
    )(pt_flat, *([pages] * (ns * pg)), q3, qr3, g3, kcvc, ks_new, kw_new, win, ov, e_mat, gsum)


def _rope_tables(pos):
    half = HEAD_DIM // 2
    inv = ROPE_THETA ** (-jnp.arange(half, dtype=F32) / half)
    ang = pos.astype(F32)[:, None] * inv[None, :]
    cos = jnp.cos(ang)
    sin = jnp.sin(ang)
    reps = LANES // HEAD_DIM
    return (jnp.tile(jnp.concatenate([cos, cos], axis=1), (1, reps)),
            jnp.tile(jnp.concatenate([-sin, sin], axis=1), (1, reps)))


def _compress_weights(cmp_w1, cmp_w2, cmp_pe):
    ratio = CMP_BLOCK // CMP_STRIDE
    w1r = cmp_w1.reshape(2, ratio, CMP_STRIDE, HEAD_DIM, HEAD_DIM)
    hh = np.arange(KV_W) // HEAD_DIM
    same_head = jnp.asarray(hh[:, None] == hh[None, :])

    def block_diag(w):
        tiled = jnp.concatenate([jnp.concatenate([w] * N_KV_HEADS, axis=-1)] * N_KV_HEADS, axis=-2)
        return jnp.where(same_head, tiled, 0.0).astype(BF16)

    wbd = block_diag(w1r.reshape(2 * ratio * CMP_STRIDE, HEAD_DIM, HEAD_DIM))
    w2bd = block_diag(cmp_w2)
    pe_r = cmp_pe.reshape(2, ratio, CMP_STRIDE, HEAD_DIM).astype(F32)
    pe_w = jnp.einsum("krsd,krsde->kre", pe_r, w1r.astype(F32), precision=lax.Precision.HIGHEST)
    pe_t = jnp.tile(pe_w.reshape(2 * ratio, HEAD_DIM), (1, N_KV_HEADS))
    return wbd, pe_t, w2bd


def _overlap(nc, ncp, nsel, nsp):
    c_start = np.arange(ncp)[:, None] * CMP_STRIDE
    s_start = np.arange(nsp)[None, :] * SEL_BLOCK
    ov = (c_start < s_start + SEL_BLOCK) & (c_start + CMP_BLOCK > s_start)
    ov = ov & (np.arange(ncp)[:, None] < nc) & (np.arange(nsp)[None, :] < nsel)
    return jnp.asarray(ov, BF16)


def _tile(n, pref):
    t = min(n, pref)
    while n % t:
        t //= 2
    return t


def kernel(x_prompt, x_sample, cache_kv_cmp, cache_kv_sel, cache_kv_win, state_ssm, page_table, norm_g, mlp_w1,
           mlp_w2, nsa_w_in, nsa_w_o, nsa_cmp_w1, nsa_cmp_w2, nsa_cmp_pe, s5_a_re, s5_a_im, s5_log_dt, s5_b_re,
           s5_b_im, s5_c_re, s5_c_im, s5_d, s5_w_glu, s5_b_glu):
    b, t, d = x_prompt.shape
    nb = x_sample.shape[0]
    pages_per_sample = page_table.shape[1]
    past = pages_per_sample * PAGE_SIZE
    rows_p = b * t
    g = norm_g.reshape(norm_g.shape[0], 4, 1, d)

    w_in = nsa_w_in[0]
    w_main = w_in[:, :Q_W + 6 * KV_W].astype(BF16)
    w_gate = jnp.pad(w_in[:, Q_W + 6 * KV_W:], ((0, 0), (0, LANES - 3 * N_HEADS))).astype(BF16)
    w_o = nsa_w_o[0].astype(BF16)
    wbd, pe_t, w2bd = _compress_weights(nsa_cmp_w1[0], nsa_cmp_w2[0], nsa_cmp_pe[0])
    w1 = mlp_w1.astype(BF16)
    w2 = mlp_w2.astype(BF16)
    w_glu = s5_w_glu[0].astype(BF16)
    ops = _s5_operators(s5_a_re[0], s5_a_im[0], s5_log_dt[0], s5_b_re[0], s5_b_im[0], s5_c_re[0], s5_c_im[0])
    d_skip = s5_d[0].reshape(1, d)
    b_glu = s5_b_glu[0].reshape(1, d)

    tm = _tile(rows_p, 512)
    ff_chunk = _tile(mlp_w1.shape[2], 1024)

    xp = x_prompt.reshape(rows_p, d)
    cos_p, sin_p = _rope_tables(jnp.arange(t, dtype=jnp.int32))
    tq = next(c for c in (2 * LANES, LANES) if t % c == 0 and WINDOW % c == 0)
    (qT, qrT, gT, kvc, kvcT, kvsT, kvwT, ksb, kwb, vsT, vwT) = _inproj(
        xp, g[0, 0], w_main, w_gate, cos_p, sin_p, tm=_tile(t, 512), pos_blocks=t // _tile(t, 512), transposed=True,
        key_chunk=tq)
    nsb_p = t // CMP_STRIDE
    nc_p = nsb_p - CMP_BLOCK // CMP_STRIDE + 1
    nsel_p = t // SEL_BLOCK
    kc, vcT = _compress_prompt(kvc.reshape(b, t, 2 * KV_W), wbd, pe_t, w2bd)
    ovT = _overlap(nc_p, nsb_p, nsel_p, nsel_p).T
    oT = _attn_prompt(qT, qrT, gT, kc, vcT, ksb, vsT, kwb, vwT, ovT, batch=b, seq=t, tq=tq, nc=nc_p,
                      n_top=min(TOP_N, nsel_p))
    hp, xn1 = _layer_tail((oT, w_o), xp, g[0, 1], g[0, 2], w1[0], w2[0], g[0, 3], g[1, 0], mixer="nsa", tm=tm,
                          ff_chunk=ff_chunk, transposed=True)

    pw = SSM_UNIT * SSM_GROUP
    npair = d // pw
    y3, hfin = _s5_seq(xn1.reshape(b, t, d), ops)
    yc = y3.reshape(rows_p, d)
    (hp,) = _layer_tail((yc, xn1, d_skip, w_glu, b_glu), hp, g[1, 1], g[1, 2], w1[1], w2[1], g[1, 3], None,
                        mixer="s5", tm=tm, ff_chunk=ff_chunk)
    ssm_p = hfin.reshape(npair, 2, b, SSM_UNIT, SSM_STATE).transpose(2, 1, 0, 3, 4)
    ssm_p = ssm_p.reshape(b, 2, d // SSM_GROUP, SSM_STATE)

    xs = x_sample.reshape(nb, d)
    cos_s, sin_s = _rope_tables(jnp.full((nb,), past, dtype=jnp.int32))
    q_s, qr_s, gates_s, kvc_s, kvs_s, kvw_s = _inproj(
        xs, g[0, 0], w_main, w_gate, cos_s, sin_s, tm=nb, pos_blocks=1, transposed=False)
    pt_flat = page_table.reshape(-1).astype(jnp.int32)
    pg = _tile(pages_per_sample, PAGE_GROUP)
    n_pool = cache_kv_cmp.shape[1]
    feature_major = lambda c, n, s: c.transpose(0, 2, 3, 4, 1).reshape(n, 2, KV_W, s)
    cmp_pages = feature_major(cache_kv_cmp[0], n_pool, PAGE_SIZE)
    sel_pages = feature_major(cache_kv_sel[0], n_pool, PAGE_SIZE)
    kcvc = _compress_paged(cmp_pages, pt_flat, wbd, pe_t, w2bd, nb=nb, pages_per_sample=pages_per_sample, pg=pg)
    l_all = past + 1
    nsb_s = l_all // CMP_STRIDE
    nc_s = nsb_s - CMP_BLOCK // CMP_STRIDE + 1
    nsel_s = -(-l_all // SEL_BLOCK)
    nsp = -(-nsel_s // LANES) * LANES
    ov_s = _overlap(nc_s, past // CMP_STRIDE, nsel_s, nsp)
    keys_per_step = pg * PAGE_SIZE
    key_blk = (np.arange(past) // SEL_BLOCK).reshape(past // keys_per_step, 1, keys_per_step)
    e_mat = jnp.asarray(np.arange(nsp)[None, :, None] == key_blk, BF16)
    hh = np.arange(N_HEADS)
    gsum = jnp.asarray((hh[:, None] // GROUP) == (hh[None, :] // GROUP), BF16)
    win = feature_major(cache_kv_win[0], nb, WINDOW)
    o_s = _attn_sample(sel_pages, pt_flat, q_s.reshape(nb, N_HEADS, HEAD_DIM), qr_s.reshape(nb, N_HEADS, HEAD_DIM),
                       gates_s[:, :3 * N_HEADS].reshape(nb, N_HEADS, 3), kcvc, kvs_s.reshape(nb, 1, 2 * KV_W),
                       kvw_s.reshape(nb, 1, 2 * KV_W), win, ov_s, e_mat, gsum, nb=nb,
                       pages_per_sample=pages_per_sample, pg=pg, past=past, nc=nc_s, ns_valid=nsel_s,
                       n_top=min(TOP_N, nsel_s))
    hs, xn1_s = _layer_tail((o_s.reshape(nb, Q_W), w_o), xs, g[0, 1], g[0, 2], w1[0], w2[0], g[0, 3], g[1, 0],
                            mixer="nsa", tm=nb, ff_chunk=ff_chunk)

    u2_s = xn1_s.astype(BF16).reshape(nb, npair, pw).transpose(1, 0, 2)
    st = state_ssm[0].reshape(nb, 2, npair, SSM_UNIT * SSM_STATE).transpose(1, 2, 0, 3)
    y2_s, hr_s, hi_s = _s5_step(u2_s, st[0], st[1], ops)
    yc_s = y2_s.transpose(1, 0, 2).reshape(nb, d)
    (hs,) = _layer_tail((yc_s, xn1_s, d_skip, w_glu, b_glu), hs, g[1, 1], g[1, 2], w1[1], w2[1], g[1, 3], None,
                        mixer="s5", tm=nb, ff_chunk=ff_chunk)
    ssm_s = jnp.stack([hr_s, hi_s], axis=0).transpose(2, 0, 1, 3).reshape(nb, 2, d // SSM_GROUP, SSM_STATE)

    kv5 = lambda a, n, s: a.reshape(1, n, s, 2, N_KV_HEADS, HEAD_DIM)
    from_fm = lambda a, n, s: a.reshape(n, 2, N_KV_HEADS, HEAD_DIM, s).transpose(0, 4, 1, 2, 3)[None]
    win_s = jnp.concatenate([win[..., 1:], kvw_s.reshape(nb, 2, KV_W, 1)], axis=-1)
    return (hp.reshape(b, t, d), hs.reshape(nb, 1, d),
            from_fm(kvcT, b, t), kv5(kvc_s, nb, 1), from_fm(kvsT, b, t), kv5(kvs_s, nb, 1),
            from_fm(kvwT[:, :, t - WINDOW:], b, WINDOW), from_fm(win_s, nb, WINDOW), ssm_p[None], ssm_s[None])
```

```python
import functools

import jax
import jax.numpy as jnp
import numpy as np
from jax import lax
from jax.experimental import pallas as pl
from jax.experimental.pallas import tpu as pltpu

N_HEADS = 16
HEAD_DIM = 64
N_KV_HEADS = 4
GROUP = N_HEADS // N_KV_HEADS
CMP_BLOCK = 32
CMP_STRIDE = 16
SEL_BLOCK = 64
TOP_N = 16
WINDOW = 512
ROPE_THETA = 10000.0
PAGE_SIZE = 128
SSM_GROUP = 16
SSM_STATE = 64
SSM_CHUNK = 8
SSM_UNIT = 8
EPS = 1e-6
NEG = -1e30
FORCE = 1e9
MASKED = -1.5e38
LOG2E = 1.4426950408889634
V_ROWS = HEAD_DIM + 16
Q_W = N_HEADS * HEAD_DIM
KV_W = N_KV_HEADS * HEAD_DIM
LANES = 128
SUBLANES = 8
PAGE_GROUP = 16
SAMPLES_PER_STEP = 2
S5_TIME_BLOCK = 2048
VMEM_LIMIT = 56 * 1024 * 1024

F32 = jnp.float32
BF16 = jnp.bfloat16


def _params(*sem):
    return pltpu.CompilerParams(dimension_semantics=sem, vmem_limit_bytes=VMEM_LIMIT)


def _full(shape):
    zeros = (0,) * len(shape)
    return pl.BlockSpec(shape, lambda *_: zeros)


def _rms(x, g):
    ms = jnp.mean(x * x, axis=-1, keepdims=True)
    return x * lax.rsqrt(ms + EPS) * g


def _dot(a, b):
    return jnp.dot(a, b, preferred_element_type=F32)


def _dot_f32lhs(w, x):
    hi = x.astype(BF16)
    r1 = x - hi.astype(F32)
    mid = r1.astype(BF16)
    lo = (r1 - mid.astype(F32)).astype(BF16)
    return _dot(w, hi) + _dot(w, mid) + _dot(w, lo)


def _dot_f32rhs(x, w):
    hi = x.astype(BF16)
    r1 = x - hi.astype(F32)
    mid = r1.astype(BF16)
    lo = (r1 - mid.astype(F32)).astype(BF16)
    return _dot(hi, w) + _dot(mid, w) + _dot(lo, w)


def _rope_nat(x, cos, sin):
    half = HEAD_DIM // 2
    lane = lax.broadcasted_iota(jnp.int32, (1, LANES), 1)
    first = (lane % HEAD_DIM) < half
    outs = []
    for c in range(x.shape[1] // LANES):
        xc = x[:, c * LANES:(c + 1) * LANES]
        rot = jnp.where(first, pltpu.roll(xc, LANES - half, 1), pltpu.roll(xc, half, 1))
        outs.append(xc * cos + rot * sin)
    return jnp.concatenate(outs, axis=1)


def _inproj_kernel(x_ref, g_ref, wq_ref, wkv_ref, wg_ref, cos_ref, sin_ref, *rest, transposed, key_chunk):
    outs = rest[2:] if transposed else rest
    xb = _rms(x_ref[...], g_ref[...]).astype(BF16)
    cos = cos_ref[...]
    sin = sin_ref[...]
    scale = HEAD_DIM ** -0.5
    kv = _dot(xb, wkv_ref[...])
    gates = jax.nn.sigmoid(_dot(xb, wg_ref[...]))
    k_s = _rope_nat(kv[:, 2 * KV_W:3 * KV_W], cos, sin)
    v_s = kv[:, 3 * KV_W:4 * KV_W]
    k_w = _rope_nat(kv[:, 4 * KV_W:5 * KV_W], cos, sin)
    v_w = kv[:, 5 * KV_W:6 * KV_W]
    kvc_ref = outs[3]
    kvc_ref[...] = kv[:, 0:2 * KV_W]
    if transposed:
        qT_ref, qrT_ref, gT_ref, _, kvcT_ref, kvsT_ref, kvwT_ref, ksb_ref, kwb_ref, vsT_ref, vwT_ref = outs
        qT = _dot_nt(wq_ref[...], xb)
        cosT = rest[0][...]
        sinT = rest[1][...]
        half = HEAD_DIM // 2
        rotated = []
        for h in range(N_HEADS):
            x1 = qT[h * HEAD_DIM:h * HEAD_DIM + half, :]
            x2 = qT[h * HEAD_DIM + half:(h + 1) * HEAD_DIM, :]
            rotated += [x1 * cosT - x2 * sinT, x2 * cosT + x1 * sinT]
        qT_ref[...] = (qT * (scale * LOG2E)).astype(BF16)
        qrT_ref[...] = (jnp.concatenate(rotated, axis=0) * (scale * LOG2E)).astype(BF16)
        gT_ref[...] = gates.T
        tm = x_ref.shape[0]
        rowi = lax.broadcasted_iota(jnp.int32, (tm, LANES), 0)
        lanei = lax.broadcasted_iota(jnp.int32, (tm, LANES), 1)
        blk = lax.shift_right_logical(rowi & (key_chunk - 1), SEL_BLOCK.bit_length() - 1)
        extra = lanei - HEAD_DIM
        nblk = key_chunk // SEL_BLOCK
        aug = jnp.where((extra == blk) | ((extra >= nblk) & (extra < nblk + N_CONST_LANES)), 1.0, 0.0)
        for c in range(KV_W // LANES):
            for k_nat, k_ref in ((k_s, ksb_ref), (k_w, kwb_ref)):
                pair = k_nat[:, c * LANES:(c + 1) * LANES]
                k_ref[2 * c] = jnp.where(lanei < HEAD_DIM, pair, aug).astype(BF16)
                k_ref[2 * c + 1] = jnp.where(lanei < HEAD_DIM, pltpu.roll(pair, HEAD_DIM, 1), aug).astype(BF16)
        v_sT = v_s.T
        v_wT = v_w.T
        ones_rows = jnp.where(lax.broadcasted_iota(jnp.int32, (V_ROWS - HEAD_DIM, tm), 0) == 0, 1.0, 0.0)
        for h in range(N_KV_HEADS):
            for vT, v_ref in ((v_sT, vsT_ref), (v_wT, vwT_ref)):
                v_ref[h * V_ROWS:(h + 1) * V_ROWS, :] = jnp.concatenate(
                    [vT[h * HEAD_DIM:(h + 1) * HEAD_DIM, :], ones_rows], axis=0).astype(BF16)
        kvcT_ref[0] = kv[:, 0:2 * KV_W].T
        kvsT_ref[0, 0:KV_W, :] = k_s.T
        kvsT_ref[0, KV_W:2 * KV_W, :] = v_sT
        kvwT_ref[0, 0:KV_W, :] = k_w.T
        kvwT_ref[0, KV_W:2 * KV_W, :] = v_wT
    else:
        q_ref, qr_ref, gt_ref, _, kvs_ref, kvw_ref = outs
        q = _dot(xb, wq_ref[...])
        qr = _rope_nat(q, cos, sin)
        q_ref[...] = (q * scale).astype(BF16)
        qr_ref[...] = (qr * scale).astype(BF16)
        gt_ref[...] = gates
        kvs_ref[:, 0:KV_W] = k_s
        kvs_ref[:, KV_W:2 * KV_W] = v_s
        kvw_ref[:, 0:KV_W] = k_w
        kvw_ref[:, KV_W:2 * KV_W] = v_w


def _inproj(x, g, w_q, w_kv, w_gate, cos_t, sin_t, *, tm, pos_blocks, transposed, key_chunk=LANES):
    rows, d = x.shape
    assert tm % key_chunk == 0 or not transposed
    n = rows // tm
    row_blk = lambda w: pl.BlockSpec((tm, w), lambda i: (i, 0))
    col_blk = lambda h: pl.BlockSpec((h, tm), lambda i: (0, i))
    tab = pl.BlockSpec((tm, LANES), lambda i: (i % pos_blocks, 0))
    kv_nat = jax.ShapeDtypeStruct((rows, 2 * KV_W), F32)
    if transposed:
        seqs = rows // (pos_blocks * tm)
        kh = pl.BlockSpec((N_KV_HEADS, tm, LANES), lambda i: (0, i, 0))
        kvT = pl.BlockSpec((1, 2 * KV_W, tm), lambda i: (i // pos_blocks, 0, i % pos_blocks))
        out_shape = ([jax.ShapeDtypeStruct((Q_W, rows), BF16)] * 2 + [jax.ShapeDtypeStruct((LANES, rows), F32)]
                     + [kv_nat] + [jax.ShapeDtypeStruct((seqs, 2 * KV_W, pos_blocks * tm), F32)] * 3
                     + [jax.ShapeDtypeStruct((N_KV_HEADS, rows, LANES), BF16)] * 2
                     + [jax.ShapeDtypeStruct((N_KV_HEADS * V_ROWS, rows), BF16)] * 2)
        out_specs = ([col_blk(Q_W)] * 2 + [col_blk(LANES)] + [row_blk(2 * KV_W)] + [kvT] * 3 + [kh] * 2
                     + [col_blk(N_KV_HEADS * V_ROWS)] * 2)
    else:
        out_shape = ([jax.ShapeDtypeStruct((rows, Q_W), BF16)] * 2 + [jax.ShapeDtypeStruct((rows, LANES), F32)]
                     + [kv_nat] * 3)
        out_specs = [row_blk(Q_W)] * 2 + [row_blk(LANES)] + [row_blk(2 * KV_W)] * 3
    tables = [cos_t, sin_t]
    table_specs = [tab, tab]
    if transposed:
        half = HEAD_DIM // 2
        tables += [cos_t[:, 0:half].T, sin_t[:, half:HEAD_DIM].T]
        table_specs += [pl.BlockSpec((half, tm), lambda i: (0, i % pos_blocks))] * 2
    return pl.pallas_call(
        functools.partial(_inproj_kernel, transposed=transposed, key_chunk=key_chunk),
        grid=(n,),
        in_specs=[row_blk(d), _full((1, d)), _full(w_q.shape), _full(w_kv.shape), _full(w_gate.shape)] + table_specs,
        out_specs=out_specs,
        out_shape=out_shape,
        compiler_params=_params("parallel"),
        name="nsa_inproj",
    )(x, g, w_q, w_kv, w_gate, *tables)


_KV_CHUNKS = 2 * KV_W // LANES


def _compress_half(load_rows, kv, nrows, wbd_ref, pe_ref):
    ratio = CMP_BLOCK // CMP_STRIDE
    accs = [jnp.broadcast_to(pe_ref[kv * ratio + r:kv * ratio + r + 1, :], (nrows, KV_W)) for r in range(ratio)]
    for s in range(CMP_STRIDE):
        lhs = load_rows(s, kv).astype(BF16)
        for r in range(ratio):
            accs[r] = accs[r] + _dot(lhs, wbd_ref[(kv * ratio + r) * CMP_STRIDE + s])
    return accs


def _compress_prompt_kernel(*refs, nsb):
    x_refs = refs[:_KV_CHUNKS]
    wbd_ref, pe_ref, w2_ref, kc_ref, vcT_ref, sh_ref = refs[_KV_CHUNKS:]

    def load_rows(s, kv):
        per_half = _KV_CHUNKS // 2
        return jnp.concatenate([x_refs[kv * per_half + c][0, pl.ds(s, nsb, stride=CMP_STRIDE), :]
                                for c in range(per_half)], axis=1)

    sh_ref[nsb:nsb + SUBLANES, :] = jnp.zeros((SUBLANES, KV_W), F32)
    for kv in range(2):
        pr0, pr1 = _compress_half(load_rows, kv, nsb, wbd_ref, pe_ref)
        sh_ref[0:nsb, :] = pr1
        h = pr0 + sh_ref[pl.ds(1, nsb), :]
        out = _dot(jax.nn.gelu(h).astype(BF16), w2_ref[kv])
        if kv == 0:
            for hh in range(N_KV_HEADS):
                kc_ref[0, hh] = out[:, hh * HEAD_DIM:(hh + 1) * HEAD_DIM].astype(BF16)
        else:
            vcT_ref[0] = out.T.astype(BF16)


def _compress_prompt(kvc3, wbd, pe_t, w2bd):
    b, t, _ = kvc3.shape
    nsb = t // CMP_STRIDE
    return pl.pallas_call(
        functools.partial(_compress_prompt_kernel, nsb=nsb),
        grid=(b,),
        in_specs=[pl.BlockSpec((1, t, LANES), lambda i, c=c: (i, 0, c)) for c in range(_KV_CHUNKS)]
        + [_full(wbd.shape), _full(pe_t.shape), _full(w2bd.shape)],
        out_specs=[pl.BlockSpec((1, N_KV_HEADS, nsb, HEAD_DIM), lambda i: (i, 0, 0, 0)),
                   pl.BlockSpec((1, KV_W, nsb), lambda i: (i, 0, 0))],
        out_shape=[jax.ShapeDtypeStruct((b, N_KV_HEADS, nsb, HEAD_DIM), BF16),
                   jax.ShapeDtypeStruct((b, KV_W, nsb), BF16)],
        scratch_shapes=[pltpu.VMEM((nsb + SUBLANES, KV_W), F32)],
        compiler_params=_params("parallel"),
        name="nsa_compress_prompt",
    )(*([kvc3] * _KV_CHUNKS), wbd, pe_t, w2bd)


def _topk_mask_T(imp, n_top):
    ns, w = imp.shape
    nblk = ns // SUBLANES
    blocks = [imp[r * SUBLANES:(r + 1) * SUBLANES, :] for r in range(nblk)]
    cnts = [jnp.zeros((SUBLANES, w), F32) for _ in range(nblk)]
    sub = lax.broadcasted_iota(jnp.int32, (SUBLANES, w), 0)
    for sp in range(ns):
        row = blocks[sp // SUBLANES][sp % SUBLANES:sp % SUBLANES + 1, :]
        for r in range(nblk):
            blk = blocks[r]
            if sp < r * SUBLANES:
                beats = jnp.where(row >= blk, 1.0, 0.0)
            elif sp >= (r + 1) * SUBLANES:
                beats = jnp.where(row > blk, 1.0, 0.0)
            else:
                beats = jnp.where(sub > (sp - r * SUBLANES), jnp.where(row >= blk, 1.0, 0.0),
                                  jnp.where(row > blk, 1.0, 0.0))
            cnts[r] = cnts[r] + beats
    return jnp.concatenate([jnp.where(c < n_top, 1.0, 0.0) for c in cnts], axis=0)


def _online_chunks(states, k_cs, vT_cs, qTs, bias):
    scores = [_dot(k_c, qT) for k_c, qT in zip(k_cs, qTs)]
    mids = []
    for (m, _), s in zip(states, scores):
        if bias is not None:
            s = s + bias
        m_new = jnp.maximum(m, jnp.max(s, axis=0, keepdims=True))
        mids.append((m_new, jnp.exp2(m - m_new), jnp.exp2(s - m_new).astype(BF16)))
    return tuple((m_new, alpha * acc + _dot(vT_c, p))
                 for (m_new, alpha, p), (_, acc), vT_c in zip(mids, states, vT_cs))


def _softmax_finish(carry):
    _, acc = carry
    return acc[0:HEAD_DIM, :] * (1.0 / jnp.maximum(acc[HEAD_DIM:HEAD_DIM + 1, :], 1e-20))


LAZY_LOG2_MAX = 60.0
N_CONST_LANES = 4


def _split3(x):
    hi = x.astype(BF16).astype(F32)
    r = x - hi
    mid = r.astype(BF16).astype(F32)
    return hi, mid, (r - mid).astype(BF16).astype(F32)


def _rescaling_chunks(m_ref, acc_ref, k_cs, vT_cs, qTs, base_tiles, bias):
    nh = len(k_cs)
    rows, w = base_tiles[0].shape
    zero_rows = jnp.zeros((LANES - HEAD_DIM - rows, w), BF16)
    new = _online_chunks(tuple((m_ref[h], acc_ref[h]) for h in range(nh)), k_cs, vT_cs,
                         [jnp.concatenate([qTs[h], base_tiles[h].astype(BF16), zero_rows], axis=0) for h in range(nh)],
                         bias)
    for h in range(nh):
        m_ref[h] = new[h][0]
        acc_ref[h] = new[h][1]


def _lazy_chunks(m_ref, acc2_ref, slot_ref, k_cs, vT_cs, qTs, base_tiles, bias, ref_row, n_chunks=1):
    slot = slot_ref[0]
    acc_ref = acc2_ref.at[slot]
    nu = len(k_cs)
    nh = nu // n_chunks
    rows = base_tiles[0].shape[0]
    w = base_tiles[0].shape[1]
    rowt = lax.broadcasted_iota(jnp.int32, (rows, w), 0)
    zero_rows = jnp.zeros((LANES - HEAD_DIM - rows, w), BF16)

    def queries(u):
        hi, mid, lo = _split3(-m_ref[u % nh])
        tile = jnp.where(rowt == ref_row, hi, jnp.where(rowt == ref_row + 1, mid,
                                                        jnp.where(rowt == ref_row + 2, lo, base_tiles[u])))
        return jnp.concatenate([qTs[u % nh], tile.astype(BF16), zero_rows], axis=0)

    scores = [_dot(k_cs[u], queries(u)) for u in range(nu)]
    peak = None
    probs = []
    for s in scores:
        if bias is not None:
            s = s + bias
        top = jnp.max(s, axis=0, keepdims=True)
        peak = top if peak is None else jnp.maximum(peak, top)
        probs.append(jnp.exp2(s).astype(BF16))
    pvs = [_dot(vT_cs[u], probs[u]) for u in range(nu)]
    for h in range(nh):
        total = pvs[h]
        for c in range(1, n_chunks):
            total = total + pvs[c * nh + h]
        acc2_ref[1 - slot, h] = acc_ref[h] + total
    in_range = jnp.max(peak) <= LAZY_LOG2_MAX

    @pl.when(in_range)
    def _():
        slot_ref[0] = 1 - slot

    @pl.when(jnp.logical_not(in_range))
    def _():
        for c in range(n_chunks):
            part = slice(c * nh, (c + 1) * nh)
            _rescaling_chunks(m_ref, acc_ref, k_cs[part], vT_cs[part], qTs, base_tiles[part], bias)


def _attn_prompt_kernel(qT_ref, qrT_ref, gT_ref, kc_ref, vcT_ref, ks_ref, vsT_ref, kw_ref, vwT_ref, ovT_ref,
                        o_ref, selb_ref, oc_ref, ow_ref, m_ref, acc2_ref, slot_ref, *, tq, nc, n_top):
    ck = tq
    i = pl.program_id(1)
    t0 = i * tq
    qpos = t0 + lax.broadcasted_iota(jnp.int32, (1, tq), 1)
    ncp = kc_ref.shape[2]
    ns = ovT_ref.shape[0]
    w = GROUP * tq
    bpc = ck // SEL_BLOCK
    sel_shift = SEL_BLOCK.bit_length() - 1
    kvhs = range(N_KV_HEADS)
    heads = [[kvh * GROUP + g for g in range(GROUP)] for kvh in kvhs]
    rows = [pl.ds(kvh * HEAD_DIM, HEAD_DIM) for kvh in kvhs]
    vrows = [pl.ds(kvh * V_ROWS, V_ROWS) for kvh in kvhs]
    qrT = [jnp.concatenate([qrT_ref[h * HEAD_DIM:(h + 1) * HEAD_DIM, :] for h in heads[kvh]], axis=1)
           for kvh in kvhs]
    kl = lax.broadcasted_iota(jnp.int32, (ck, tq), 0)
    ql = lax.broadcasted_iota(jnp.int32, (ck, tq), 1)
    tile4 = lambda b: jnp.concatenate([b] * GROUP, axis=1)
    key_le_query = tile4(jnp.where(kl <= ql, 0.0, MASKED))
    key_ge_query = tile4(jnp.where(kl >= ql, 0.0, MASKED))
    bias_rows = selb_ref.shape[2]

    def reset_softmax():
        slot_ref[0] = 0
        for kvh in kvhs:
            m_ref[kvh] = jnp.full((1, w), NEG, F32)
            acc2_ref[0, kvh] = jnp.zeros((V_ROWS, w), F32)

    def current_acc(kvh):
        return acc2_ref[slot_ref[0], kvh]

    cidx = lax.broadcasted_iota(jnp.int32, (ncp, 1), 0)
    valid = ((cidx * CMP_STRIDE + (CMP_BLOCK - 1)) <= qpos) & (cidx < nc)
    sidx = lax.broadcasted_iota(jnp.int32, (ns, 1), 0)
    cur = lax.shift_right_logical(qpos, sel_shift)
    forced = (sidx == 0) | (sidx == cur) | (sidx == cur - 1)
    causal = (sidx * SEL_BLOCK) <= qpos
    cmp_scores = [
        _dot(kc_ref[0, kvh], jnp.concatenate([qT_ref[h * HEAD_DIM:(h + 1) * HEAD_DIM, :] for h in heads[kvh]], axis=1))
        for kvh in kvhs]
    for kvh in kvhs:
        s = cmp_scores[kvh]
        probs = []
        for g in range(GROUP):
            sm = jnp.where(valid, s[:, g * tq:(g + 1) * tq], NEG)
            mx = jnp.max(sm, axis=0, keepdims=True)
            e = jnp.where(valid, jnp.exp2(sm - mx), 0.0)
            den = jnp.maximum(jnp.sum(e, axis=0, keepdims=True), 1e-20)
            probs.append(e / den)
        oc_ref[kvh] = _dot(vcT_ref[0, rows[kvh], :], jnp.concatenate(probs, axis=1).astype(BF16))
        psum = probs[0]
        for g in range(1, GROUP):
            psum = psum + probs[g]
        imp = _dot_f32lhs(ovT_ref[...], psum)
        imp = jnp.where(forced, FORCE, jnp.where(causal, imp, NEG))
        selb = (_topk_mask_T(imp, n_top) - 1.0) * (-MASKED)
        fill = jnp.zeros((bias_rows - bpc, tq), F32)
        for c in range(ns // bpc):
            selb_ref[kvh, c] = jnp.concatenate([selb[c * bpc:(c + 1) * bpc, :], fill], axis=0)

    n_back = WINDOW // ck
    reset_softmax()
    rowb = lax.broadcasted_iota(jnp.int32, (bias_rows, w), 0)
    for r in range(n_back + 1):
        a = i - n_back + r
        kst = pl.multiple_of(jnp.maximum(a, 0) * ck, ck)
        skip = jnp.where(a < 0, MASKED, 0.0)
        tile = jnp.where(rowb == bpc, skip, 0.0)
        bias = key_le_query if r == n_back else (key_ge_query if r == 0 else None)
        k_cs = [kw_ref[kvh, pl.ds(kst, ck), :] for kvh in kvhs]
        vT_cs = [vwT_ref[vrows[kvh], pl.ds(kst, ck)] for kvh in kvhs]
        if r == 0:
            _rescaling_chunks(m_ref, acc2_ref.at[0], k_cs, vT_cs, qrT, [tile] * N_KV_HEADS, bias)
        else:
            _lazy_chunks(m_ref, acc2_ref, slot_ref, k_cs, vT_cs, qrT, [tile] * N_KV_HEADS, bias, bpc + 1)
    for kvh in kvhs:
        ow_ref[kvh] = _softmax_finish((None, current_acc(kvh)))

    def chunk_step(c, diagonal, lazy=True, n_chunks=1):
        ks, vs, tiles = [], [], []
        for dc in range(n_chunks):
            kst = (c + dc) * ck if isinstance(c, int) else pl.multiple_of((c + dc) * ck, ck)
            ks += [ks_ref[kvh, pl.ds(kst, ck), :] for kvh in kvhs]
            vs += [vsT_ref[vrows[kvh], pl.ds(kst, ck)] for kvh in kvhs]
            tiles += [tile4(selb_ref[kvh, c + dc]) for kvh in kvhs]
        bias = key_le_query if diagonal else None
        if lazy:
            _lazy_chunks(m_ref, acc2_ref, slot_ref, ks, vs, qrT, tiles, bias, bpc + 1, n_chunks)
        else:
            _rescaling_chunks(m_ref, acc2_ref.at[slot_ref[0]], ks, vs, qrT, tiles, bias)

    reset_softmax()

    @pl.when(i > 0)
    def _():
        chunk_step(0, False, lazy=False)

    n_mid = jnp.maximum(i - 1, 0)

    def loop_body(p, carry):
        chunk_step(1 + 2 * p, False, n_chunks=2)
        return carry

    lax.fori_loop(0, n_mid // 2, loop_body, 0)

    @pl.when(n_mid % 2 == 1)
    def _():
        chunk_step(i - 1, False)

    chunk_step(i, True)

    for kvh in kvhs:
        def gate_row(j, kvh=kvh):
            return jnp.concatenate([gT_ref[h * 3 + j:h * 3 + j + 1, :] for h in heads[kvh]], axis=1)

        oT = (gate_row(0) * oc_ref[kvh] + gate_row(1) * _softmax_finish((None, current_acc(kvh)))
              + gate_row(2) * ow_ref[kvh])
        for g, h in enumerate(heads[kvh]):
            o_ref[h * HEAD_DIM:(h + 1) * HEAD_DIM, :] = oT[:, g * tq:(g + 1) * tq].astype(BF16)


def _attn_prompt(qT, qrT, gT, kc, vcT, ksb, vsT, kwb, vwT, ovT, *, batch, seq, tq, nc, n_top):
    nq = seq // tq
    nsb = kc.shape[2]
    ns = ovT.shape[0]
    col = lambda h: pl.BlockSpec((h, tq), lambda b, i: (0, b * nq + i))
    kh = pl.BlockSpec((N_KV_HEADS, seq, LANES), lambda b, i: (0, b, 0))
    vt = pl.BlockSpec((N_KV_HEADS * V_ROWS, seq), lambda b, i: (0, b))
    bf16_sublanes = 2 * SUBLANES
    return pl.pallas_call(
        functools.partial(_attn_prompt_kernel, tq=tq, nc=nc, n_top=n_top),
        grid=(batch, nq),
        in_specs=[col(Q_W), col(Q_W), col(LANES),
                  pl.BlockSpec((1, N_KV_HEADS, nsb, HEAD_DIM), lambda b, i: (b, 0, 0, 0)),
                  pl.BlockSpec((1, KV_W, nsb), lambda b, i: (b, 0, 0)),
                  kh, vt, kh, vt, _full(ovT.shape)],
        out_specs=col(Q_W),
        out_shape=jax.ShapeDtypeStruct((Q_W, batch * seq), BF16),
        scratch_shapes=[pltpu.VMEM((N_KV_HEADS, ns * SEL_BLOCK // tq, bf16_sublanes, tq), F32),
                        pltpu.VMEM((N_KV_HEADS, HEAD_DIM, GROUP * tq), F32),
                        pltpu.VMEM((N_KV_HEADS, HEAD_DIM, GROUP * tq), F32),
                        pltpu.VMEM((N_KV_HEADS, 1, GROUP * tq), F32),
                        pltpu.VMEM((2, N_KV_HEADS, V_ROWS, GROUP * tq), F32),
                        pltpu.SMEM((1,), jnp.int32)],
        compiler_params=_params("parallel", "arbitrary"),
        name="nsa_attn_prompt",
    )(qT, qrT, gT, kc, vcT, ksb, vsT, kwb, vwT, ovT)


def _resident(shape):
    zeros = (0,) * len(shape)
    return pl.BlockSpec(shape, lambda *_: zeros, pipeline_mode=pl.Buffered(1))


def _layer_tail_kernel(*refs, mixer, transposed, ff_chunk, next_norm):
    if mixer == "nsa":
        o_ref, wo_ref, x_ref, g1_ref, g2_ref = refs[:5]
        rest = refs[5:]
        if transposed:
            y = lax.dot_general(o_ref[...], wo_ref[...], (((0,), (0,)), ((), ())), preferred_element_type=F32)
        else:
            y = _dot(o_ref[...], wo_ref[...])
    else:
        yc_ref, u_ref, d_ref, wg_ref, bg_ref, x_ref, g1_ref, g2_ref = refs[:8]
        rest = refs[8:]
        z = jax.nn.gelu(yc_ref[...] + d_ref[...] * u_ref[...])
        y = z * jax.nn.sigmoid(_dot(z.astype(BF16), wg_ref[...]) + bg_ref[...])
    w1_ref, w2_ref, g3_ref = rest[:3]
    if next_norm:
        gn_ref, h2_ref, xn_ref, acc_ref = rest[3:]
    else:
        h2_ref, acc_ref = rest[3:]
    h = x_ref[...] + _rms(y, g1_ref[...])
    xm = _rms(h, g2_ref[...]).astype(BF16)
    for c in range(w1_ref.shape[1] // ff_chunk):
        cols = slice(c * ff_chunk, (c + 1) * ff_chunk)
        hm = jnp.maximum(_dot(xm, w1_ref[:, cols]), 0.0)
        part = _dot((hm * hm).astype(BF16), w2_ref[cols, :])
        if c == 0:
            acc_ref[...] = part
        else:
            acc_ref[...] += part
    h2 = h + _rms(acc_ref[...], g3_ref[...])
    h2_ref[...] = h2
    if next_norm:
        xn_ref[...] = _rms(h2, gn_ref[...])


def _layer_tail(mixer_args, x, g1, g2, w1, w2, g3, gn, *, mixer, tm, ff_chunk, transposed=False):
    rows, d = x.shape
    row = pl.BlockSpec((tm, d), lambda i: (i, 0))
    vec = _resident((1, d))
    if mixer == "nsa":
        o, w_o = mixer_args
        o_spec = (pl.BlockSpec((Q_W, tm), lambda i: (0, i)) if transposed
                  else pl.BlockSpec((tm, Q_W), lambda i: (i, 0)))
        head_specs = [o_spec, _resident(w_o.shape)]
    else:
        w_glu = mixer_args[3]
        head_specs = [row, row, vec, _resident(w_glu.shape), vec]
    tail_args = (w1, w2, g3) + (() if gn is None else (gn,))
    tail_specs = [_resident(w1.shape), _resident(w2.shape), vec] + ([] if gn is None else [vec])
    n_out = 1 if gn is None else 2
    return pl.pallas_call(
        functools.partial(_layer_tail_kernel, mixer=mixer, transposed=transposed, ff_chunk=ff_chunk,
                          next_norm=gn is not None),
        grid=(rows // tm,),
        in_specs=head_specs + [row, vec, vec] + tail_specs,
        out_specs=[row] * n_out,
        out_shape=[jax.ShapeDtypeStruct((rows, d), F32)] * n_out,
        scratch_shapes=[pltpu.VMEM((tm, d), F32)],
        compiler_params=_params("parallel"),
        name=mixer + "_layer_tail",
    )(*mixer_args, x, g1, g2, *tail_args)


def _s5_seq_kernel(x_ref, brow_ref, pre_ref, pim_ref, qre_ref, qim_ref, are_ref, aim_ref, y_ref, hfin_ref,
                   sre, sim, hre, him, cre, cim):
    nb, tb, uw = x_ref.shape
    nt = pre_ref.shape[1] // uw
    nk = tb // nt

    @pl.when(pl.program_id(1) == 0)
    def _():
        cre[...] = jnp.zeros(cre.shape, F32)
        cim[...] = jnp.zeros(cim.shape, F32)

    u = [jnp.concatenate([x_ref[b, pl.ds(t, nk, stride=nt), :] for b in range(nb)], axis=0).astype(BF16)
         for t in range(nt)]
    u2 = [jnp.concatenate([u[2 * j], u[2 * j + 1]], axis=1) for j in range(nt // 2)]
    s_re = _dot(u2[0], pre_ref[0, 0:2 * uw, :])
    s_im = _dot(u2[0], pim_ref[0, 0:2 * uw, :])
    for j in range(1, nt // 2):
        s_re = s_re + _dot(u2[j], pre_ref[0, 2 * j * uw:(2 * j + 2) * uw, :])
        s_im = s_im + _dot(u2[j], pim_ref[0, 2 * j * uw:(2 * j + 2) * uw, :])
    sre[...] = s_re
    sim[...] = s_im
    ar = are_ref[0]
    ai = aim_ref[0]

    def body(it, carry):
        out = []
        for b in range(nb):
            hr, hi = carry[b]
            r0 = pl.multiple_of(b * nk + it * SUBLANES, SUBLANES)
            sr8 = sre[pl.ds(r0, SUBLANES), :]
            si8 = sim[pl.ds(r0, SUBLANES), :]
            prev_r, prev_i = [], []
            for j in range(SUBLANES):
                prev_r.append(hr)
                prev_i.append(hi)
                hr, hi = (ar * hr - ai * hi + sr8[j:j + 1, :], ar * hi + ai * hr + si8[j:j + 1, :])
            hre[pl.ds(r0, SUBLANES), :] = jnp.concatenate(prev_r, axis=0)
            him[pl.ds(r0, SUBLANES), :] = jnp.concatenate(prev_i, axis=0)
            out.append((hr, hi))
        return tuple(out)

    init = tuple((cre[b:b + 1, :], cim[b:b + 1, :]) for b in range(nb))
    fin = lax.fori_loop(0, nk // SUBLANES, body, init)
    for b in range(nb):
        cre[b:b + 1, :] = fin[b][0]
        cim[b:b + 1, :] = fin[b][1]
    hfin_ref[0, 0] = cre[...]
    hfin_ref[0, 1] = cim[...]

    hb_re = hre[...].astype(BF16)
    hb_im = him[...].astype(BF16)
    for t2 in range(0, nt, 2):
        cols = slice(t2 * uw, (t2 + 2) * uw)
        acc = _dot(hb_re, qre_ref[0, :, cols]) + _dot(hb_im, qim_ref[0, :, cols])
        for j in range(t2 // 2 + 1):
            lag0 = t2 - 2 * j + 1
            wpair = jnp.concatenate([brow_ref[0, :, lag0 * uw:(lag0 + 2) * uw],
                                     brow_ref[0, :, (lag0 - 1) * uw:(lag0 + 1) * uw]], axis=0)
            acc = acc + _dot(u2[j], wpair)
        for b in range(nb):
            for dt in range(2):
                y_ref[b, pl.ds(t2 + dt, nk, stride=nt), :] = acc[b * nk:(b + 1) * nk, dt * uw:(dt + 1) * uw]


def _s5_seq(x3, ops):
    nb, t, d = x3.shape
    uw = SSM_UNIT * SSM_GROUP
    nunit = d // uw
    sw = ops["p_re"].shape[2]
    tb = _tile(t, S5_TIME_BLOCK)
    nk = tb // SSM_CHUNK
    per_unit = lambda a: pl.BlockSpec((1,) + a.shape[1:], lambda i, r: (i,) + (0,) * (a.ndim - 1))
    blk = pl.BlockSpec((nb, tb, uw), lambda i, r: (0, r, i))
    args = (x3, ops["brow"], ops["p_re"], ops["p_im"], ops["q_re"], ops["q_im"], ops["a_chunk_re"], ops["a_chunk_im"])
    return pl.pallas_call(
        _s5_seq_kernel,
        grid=(nunit, t // tb),
        in_specs=[blk] + [per_unit(a) for a in args[1:]],
        out_specs=[blk, pl.BlockSpec((1, 2, nb, sw), lambda i, r: (i, 0, 0, 0))],
        out_shape=[jax.ShapeDtypeStruct((nb, t, d), F32), jax.ShapeDtypeStruct((nunit, 2, nb, sw), F32)],
        scratch_shapes=[pltpu.VMEM((nb * nk, sw), F32)] * 4 + [pltpu.VMEM((nb, sw), F32)] * 2,
        compiler_params=_params("parallel", "arbitrary"),
        name="s5_seq_scan",
    )(*args)


def _s5_step_kernel(u_ref, h0r_ref, h0i_ref, bre_ref, bim_ref, cre_ref, cim_ref, are_ref, aim_ref,
                    y_ref, hr_ref, hi_ref, *, npair):
    for p in range(npair):
        u = u_ref[p]
        ar = are_ref[p]
        ai = aim_ref[p]
        h0r = h0r_ref[p]
        h0i = h0i_ref[p]
        hr = ar * h0r - ai * h0i + _dot(u, bre_ref[p])
        hi = ar * h0i + ai * h0r + _dot(u, bim_ref[p])
        hr_ref[p] = hr
        hi_ref[p] = hi
        y_ref[p] = _dot(hr.astype(BF16), cre_ref[p]) + _dot(hi.astype(BF16), cim_ref[p])


def _s5_step(u2, h0r, h0i, ops):
    npair, rows, width = u2.shape
    args = (u2, h0r, h0i, ops["b1_re"], ops["b1_im"], ops["c1_re"], ops["c1_im"], ops["a1_re"], ops["a1_im"])
    return pl.pallas_call(
        functools.partial(_s5_step_kernel, npair=npair),
        grid=(1,),
        in_specs=[_full(a.shape) for a in args],
        out_specs=[_full((npair, rows, width)), _full(h0r.shape), _full(h0r.shape)],
        out_shape=[jax.ShapeDtypeStruct((npair, rows, width), F32), jax.ShapeDtypeStruct(h0r.shape, F32),
                   jax.ShapeDtypeStruct(h0r.shape, F32)],
        compiler_params=_params("arbitrary"),
        name="s5_single_step",
    )(*args)


def _s5_operators(a_re, a_im, log_dt, b_re, b_im, c_re, c_im):
    hp = lax.Precision.HIGHEST
    g, n = a_re.shape
    gu = SSM_UNIT
    nunit = g // gu
    L = SSM_CHUNK
    uw = gu * SSM_GROUP
    sw = gu * n
    a = lax.complex(a_re.astype(F32), a_im.astype(F32))
    dt = jnp.exp(log_dt.astype(F32))[:, None]
    a_bar = jnp.exp(a * dt)
    b_bar = ((a_bar - 1.0) / a)[:, :, None] * lax.complex(b_re.astype(F32), b_im.astype(F32))
    c = lax.complex(c_re.astype(F32), c_im.astype(F32))
    pows = [jnp.ones_like(a_bar)]
    for _ in range(L):
        pows.append(pows[-1] * a_bar)
    a_pow = jnp.stack(pows).reshape(L + 1, nunit, sw)
    apr = jnp.real(a_pow)
    api = jnp.imag(a_pow)
    eye = jnp.eye(gu, dtype=F32)

    def bd_in(x):
        return jnp.einsum("pgnd,gh->pgdhn", x.reshape(nunit, gu, n, SSM_GROUP), eye).reshape(nunit, uw, sw)

    def bd_out(x):
        return jnp.einsum("pgcn,gh->pgnhc", x.reshape(nunit, gu, SSM_GROUP, n), eye).reshape(nunit, sw, uw)

    p0r, p0i = bd_in(jnp.real(b_bar)), bd_in(jnp.imag(b_bar))
    q0r, q0i = bd_out(jnp.real(c)), bd_out(jnp.imag(c))
    lanes = lambda x, k: x[k][:, None, :]
    rows_ = lambda x, k: x[k][:, :, None]
    p_re = jnp.concatenate([p0r * lanes(apr, L - 1 - t) - p0i * lanes(api, L - 1 - t) for t in range(L)], axis=1)
    p_im = jnp.concatenate([p0r * lanes(api, L - 1 - t) + p0i * lanes(apr, L - 1 - t) for t in range(L)], axis=1)
    q_re = jnp.concatenate([q0r * rows_(apr, t + 1) - q0i * rows_(api, t + 1) for t in range(L)], axis=2)
    q_im = jnp.concatenate([-(q0r * rows_(api, t + 1) + q0i * rows_(apr, t + 1)) for t in range(L)], axis=2)
    lag_blocks = [jnp.zeros((nunit, uw, uw), F32)]
    for t in range(L):
        xr = p0r * lanes(apr, t) - p0i * lanes(api, t)
        xi = p0r * lanes(api, t) + p0i * lanes(apr, t)
        lag_blocks.append(jnp.einsum("pus,psv->puv", xr, q0r, precision=hp)
                          - jnp.einsum("pus,psv->puv", xi, q0i, precision=hp))
    brow = jnp.concatenate(lag_blocks, axis=2)
    lane_row = lambda x: x.reshape(nunit, 1, sw)
    return {
        "brow": brow.astype(BF16),
        "p_re": p_re.astype(BF16), "p_im": p_im.astype(BF16),
        "q_re": q_re.astype(BF16), "q_im": q_im.astype(BF16),
        "a_chunk_re": lane_row(apr[L]), "a_chunk_im": lane_row(api[L]),
        "a1_re": lane_row(apr[1]), "a1_im": lane_row(api[1]),
        "b1_re": p0r.astype(BF16), "b1_im": p0i.astype(BF16),
        "c1_re": q0r.astype(BF16), "c1_im": (-q0i).astype(BF16),
    }


def _compress_paged_kernel(pt_ref, *refs, pg, nsb):
    del pt_ref
    pages = refs[:pg]
    perm_ref, wbd_ref, pe_ref, w2_ref, out_ref, h0_ref, h1_ref = refs[pg:]
    j = pl.program_id(1)
    sbp = PAGE_SIZE // CMP_STRIDE
    nrows = pg * sbp
    pair_rows = 2 * sbp

    @pl.when(j == 0)
    def _():
        h1_ref[:, nsb:nsb + SUBLANES, :] = jnp.zeros((2, SUBLANES, KV_W), F32)

    r0 = pl.multiple_of(j * nrows, nrows)
    for kv in range(2):
        staged = []
        for q in range(pg // 2):
            z = jnp.concatenate([pages[2 * q][0, kv], pages[2 * q + 1][0, kv]], axis=1).astype(BF16)
            staged.append(_dot_nt(perm_ref[...], z).astype(BF16))

        def load_rows(s, kv, staged=staged):
            return jnp.concatenate([x[s * pair_rows:(s + 1) * pair_rows, :] for x in staged], axis=0)

        pr0, pr1 = _compress_half(load_rows, kv, nrows, wbd_ref, pe_ref)
        h0_ref[kv, pl.ds(r0, nrows), :] = pr0
        h1_ref[kv, pl.ds(r0, nrows), :] = pr1

    @pl.when(j == pl.num_programs(1) - 1)
    def _():
        for kv in range(2):
            h = h0_ref[kv] + h1_ref[kv, pl.ds(1, nsb), :]
            out = _dot(jax.nn.gelu(h).astype(BF16), w2_ref[kv])
            out_ref[0, :, kv * KV_W:(kv + 1) * KV_W] = out.astype(BF16)


def _page_specs(pg, pages_per_sample):
    def spec(i):
        return pl.BlockSpec((1, 2, KV_W, PAGE_SIZE),
                            lambda b, j, pt: (pt[b * pages_per_sample + j * pg + i], 0, 0, 0))
    return [spec(i) for i in range(pg)]


def _compress_paged(pages, pt_flat, wbd, pe_t, w2bd, *, nb, pages_per_sample, pg):
    nsb = pages_per_sample * PAGE_SIZE // CMP_STRIDE
    sbp = PAGE_SIZE // CMP_STRIDE
    i_out = np.arange(2 * PAGE_SIZE)
    s_i, pg_i, n_i = i_out // (2 * sbp), (i_out // sbp) % 2, i_out % sbp
    perm = jnp.asarray(i_out[None, :] == (pg_i * PAGE_SIZE + n_i * CMP_STRIDE + s_i)[:, None], BF16)
    c3 = lambda shape: pl.BlockSpec(shape, lambda b, j, pt: (0,) * len(shape))
    grid_spec = pltpu.PrefetchScalarGridSpec(
        num_scalar_prefetch=1,
        grid=(nb, pages_per_sample // pg),
        in_specs=_page_specs(pg, pages_per_sample) + [c3(perm.shape), c3(wbd.shape), c3(pe_t.shape), c3(w2bd.shape)],
        out_specs=pl.BlockSpec((1, nsb, 2 * KV_W), lambda b, j, pt: (b, 0, 0)),
        scratch_shapes=[pltpu.VMEM((2, nsb, KV_W), F32), pltpu.VMEM((2, nsb + SUBLANES, KV_W), F32)],
    )
    return pl.pallas_call(
        functools.partial(_compress_paged_kernel, pg=pg, nsb=nsb),
        grid_spec=grid_spec,
        out_shape=jax.ShapeDtypeStruct((nb, nsb, 2 * KV_W), BF16),
        compiler_params=_params("parallel", "arbitrary"),
        name="nsa_compress_paged",
    )(pt_flat, *([pages] * pg), perm, wbd, pe_t, w2bd)


def _topk_mask_lanes(imp, n_top, ns_valid):
    lane = lax.broadcasted_iota(jnp.int32, imp.shape, 1)
    cnt = jnp.zeros(imp.shape, F32)
    for sp in range(ns_valid):
        col = imp[:, sp:sp + 1]
        cnt = cnt + jnp.where(lane > sp, jnp.where(col >= imp, 1.0, 0.0), jnp.where(col > imp, 1.0, 0.0))
    return jnp.where((cnt < n_top) & (lane < ns_valid), 1.0, 0.0)


def _dot_nt(a, b):
    return lax.dot_general(a, b, (((1,), (1,)), ((), ())), preferred_element_type=F32)


def _attn_sample_kernel(pt_ref, *refs, pg, ns, past, nc, ns_valid, n_top):
    del pt_ref
    all_pages = [refs[u * pg:(u + 1) * pg] for u in range(ns)]
    (q_ref, qr_ref, g_ref, kcvc_ref, ksn_ref, kwn_ref, win_ref, ov_ref, e_ref, gs_ref, o_ref,
     m_ref, l_ref, acc_ref, sel_ref, oc_ref, ow_ref) = refs[ns * pg:]
    j = pl.program_id(1)
    ncp = kcvc_ref.shape[1]
    nsp = ov_ref.shape[1]
    wlen = win_ref.shape[3]
    row = lax.broadcasted_iota(jnp.int32, (N_HEADS, KV_W), 0)
    lane = lax.broadcasted_iota(jnp.int32, (N_HEADS, KV_W), 1)
    own = (lane // HEAD_DIM) == (row // GROUP)

    def spread(ref, u):
        q = ref[u]
        return jnp.where(own, jnp.concatenate([q] * N_KV_HEADS, axis=1), jnp.zeros((N_HEADS, KV_W), BF16))

    def update(state, s, vs):
        m, l, acc = state
        m_new = jnp.maximum(m, jnp.max(s, axis=1, keepdims=True))
        alpha = jnp.exp(m - m_new)
        p = jnp.exp(s - m_new)
        l = alpha * l + jnp.sum(p, axis=1, keepdims=True)
        pv = None
        for st, sz, v, feature_major in vs:
            pb = p[:, st:st + sz].astype(BF16)
            t = _dot_nt(pb, v) if feature_major else _dot(pb, v)
            pv = t if pv is None else pv + t
        return m_new, l, alpha * acc + pv

    def init():
        return (jnp.full((N_HEADS, 1), NEG, F32), jnp.zeros((N_HEADS, 1), F32), jnp.zeros((N_HEADS, KV_W), F32))

    def new_row_update(state, qbd, new_row):
        r8 = lax.broadcasted_iota(jnp.int32, (SUBLANES, 2 * KV_W), 0)
        tile = jnp.where(r8 == 0, jnp.broadcast_to(new_row, (SUBLANES, 2 * KV_W)), 0.0).astype(BF16)
        s = _dot_nt(qbd, tile[:, 0:KV_W])
        l8 = lax.broadcasted_iota(jnp.int32, (N_HEADS, SUBLANES), 1)
        s = jnp.where(l8 == 0, s, MASKED)
        return update(state, s, [(0, SUBLANES, tile[:, KV_W:2 * KV_W], False)])

    def put(u, state):
        m, l, acc = state
        m_ref[u] = jnp.broadcast_to(m, m_ref.shape[1:])
        l_ref[u] = jnp.broadcast_to(l, l_ref.shape[1:])
        acc_ref[u] = acc

    qrbd = [spread(qr_ref, u) for u in range(ns)]

    @pl.when(j == 0)
    def _():
        cidx = lax.broadcasted_iota(jnp.int32, (1, ncp), 1)
        valid = ((cidx * CMP_STRIDE + (CMP_BLOCK - 1)) <= past) & (cidx < nc)
        sidx = lax.broadcasted_iota(jnp.int32, (1, nsp), 1)
        cur = past // SEL_BLOCK
        forced = (sidx == 0) | (sidx == cur) | (sidx == cur - 1)
        causal = (sidx * SEL_BLOCK) <= past
        wpos = past - wlen + lax.broadcasted_iota(jnp.int32, (1, wlen), 1)
        in_window = (wpos >= 0) & (past - wpos <= WINDOW)
        for u in range(ns):
            s = _dot_nt(spread(q_ref, u), kcvc_ref[u, :, 0:KV_W])
            sm = jnp.where(valid, s, NEG)
            mx = jnp.max(sm, axis=1, keepdims=True)
            e = jnp.where(valid, jnp.exp(sm - mx), 0.0)
            p = e / jnp.maximum(jnp.sum(e, axis=1, keepdims=True), 1e-20)
            oc_ref[u] = _dot(p.astype(BF16), kcvc_ref[u, :, KV_W:2 * KV_W])
            imp = _dot_f32rhs(_dot_f32lhs(gs_ref[...], p), ov_ref[...])
            imp = jnp.where(forced, FORCE, jnp.where(causal, imp, NEG))
            imp = jnp.where(sidx < ns_valid, imp, MASKED)
            sel_ref[u] = _topk_mask_lanes(imp, n_top, ns_valid)
            sw = jnp.where(in_window, _dot(qrbd[u], win_ref[u, 0].astype(BF16)), MASKED)
            st = update(init(), sw, [(0, wlen, win_ref[u, 1].astype(BF16), True)])
            st = new_row_update(st, qrbd[u], kwn_ref[u])
            ow_ref[u] = st[2] * (1.0 / jnp.maximum(st[1], 1e-20))
            put(u, new_row_update(init(), qrbd[u], ksn_ref[u]))

    def halves(pages):
        return [pages[:pg // 2], pages[pg // 2:]] if pg > 1 else [pages]

    scores = [[jnp.concatenate([_dot(qrbd[u], r[0, 0].astype(BF16)) for r in part], axis=1)
               for part in halves(all_pages[u])] for u in range(ns)]
    masks = [_dot(sel_ref[u].astype(BF16), e_ref[0]) > 0.5 for u in range(ns)]
    states = [(m_ref[u, :, 0:1], l_ref[u, :, 0:1], acc_ref[u]) for u in range(ns)]
    k0 = 0
    for hx, part0 in enumerate(halves(all_pages[0])):
        width = len(part0) * PAGE_SIZE
        for u in range(ns):
            part = halves(all_pages[u])[hx]
            s = jnp.where(masks[u][:, k0:k0 + width], scores[u][hx], MASKED)
            vs = [(i * PAGE_SIZE, PAGE_SIZE, r[0, 1].astype(BF16), True) for i, r in enumerate(part)]
            states[u] = update(states[u], s, vs)
        k0 += width
    for u in range(ns):
        put(u, states[u])

    @pl.when(j == pl.num_programs(1) - 1)
    def _():
        for u in range(ns):
            g = g_ref[u]
            os_ = acc_ref[u] * (1.0 / jnp.maximum(l_ref[u, :, 0:1], 1e-20))
            o = g[:, 0:1] * oc_ref[u] + g[:, 1:2] * os_ + g[:, 2:3] * ow_ref[u]
            o = jnp.where(own, o, 0.0)
            out = o[:, 0:HEAD_DIM]
            for h in range(1, N_KV_HEADS):
                out = out + o[:, h * HEAD_DIM:(h + 1) * HEAD_DIM]
            o_ref[u] = out.astype(BF16)


def _attn_sample(pages, pt_flat, q3, qr3, g3, kcvc, ks_new, kw_new, win, ov, e_mat, gsum, *, nb, pages_per_sample,
                 pg, past, nc, ns_valid, n_top):
    nsp = ov.shape[1]
    ns = SAMPLES_PER_STEP if nb % SAMPLES_PER_STEP == 0 else 1
    per_b = lambda shape: pl.BlockSpec((ns,) + shape, lambda b, j, pt: (b,) + (0,) * len(shape))
    const = lambda shape: pl.BlockSpec(shape, lambda b, j, pt: (0,) * len(shape))

    def page_spec(u, i):
        return pl.BlockSpec((1, 2, KV_W, PAGE_SIZE),
                            lambda b, j, pt: (pt[(b * ns + u) * pages_per_sample + j * pg + i], 0, 0, 0))

    per_sample = lambda *shape: pltpu.VMEM((ns,) + shape, F32)
    grid_spec = pltpu.PrefetchScalarGridSpec(
        num_scalar_prefetch=1,
        grid=(nb // ns, pages_per_sample // pg),
        in_specs=[page_spec(u, i) for u in range(ns) for i in range(pg)] + [
            per_b((N_HEADS, HEAD_DIM)), per_b((N_HEADS, HEAD_DIM)), per_b((N_HEADS, 3)),
            per_b(kcvc.shape[1:]), per_b((1, 2 * KV_W)), per_b((1, 2 * KV_W)), per_b(win.shape[1:]),
            const(ov.shape), pl.BlockSpec((1,) + e_mat.shape[1:], lambda b, j, pt: (j, 0, 0)), const(gsum.shape)],
        out_specs=per_b((N_HEADS, HEAD_DIM)),
        scratch_shapes=[per_sample(N_HEADS, LANES), per_sample(N_HEADS, LANES), per_sample(N_HEADS, KV_W),
                        per_sample(N_HEADS, nsp), per_sample(N_HEADS, KV_W), per_sample(N_HEADS, KV_W)],
    )
    return pl.pallas_call(
        functools.partial(_attn_sample_kernel, pg=pg, ns=ns, past=past, nc=nc, ns_valid=ns_valid, n_top=n_top),
        grid_spec=grid_spec,
        out_shape=jax.ShapeDtypeStruct((nb, N_HEADS, HEAD_DIM), BF16),
        compiler_params=_params("parallel", "arbitrary"),
        name="nsa_attn_sample",
    )(pt_flat, *([pages] * (ns * pg)), q3, qr3, g3, kcvc, ks_new, kw_new, win, ov, e_mat, gsum)


def _rope_tables(pos):
    half = HEAD_DIM // 2
    inv = ROPE_THETA ** (-jnp.arange(half, dtype=F32) / half)
    ang = pos.astype(F32)[:, None] * inv[None, :]
    cos = jnp.cos(ang)
    sin = jnp.sin(ang)
    reps = LANES // HEAD_DIM
    return (jnp.tile(jnp.concatenate([cos, cos], axis=1), (1, reps)),
            jnp.tile(jnp.concatenate([-sin, sin], axis=1), (1, reps)))


def _compress_weights(cmp_w1, cmp_w2, cmp_pe):
    ratio = CMP_BLOCK // CMP_STRIDE
    w1r = cmp_w1.reshape(2, ratio, CMP_STRIDE, HEAD_DIM, HEAD_DIM)
    hh = np.arange(KV_W) // HEAD_DIM
    same_head = jnp.asarray(hh[:, None] == hh[None, :])

    def block_diag(w):
        tiled = jnp.concatenate([jnp.concatenate([w] * N_KV_HEADS, axis=-1)] * N_KV_HEADS, axis=-2)
        return jnp.where(same_head, tiled, 0.0).astype(BF16)

    wbd = block_diag(w1r.reshape(2 * ratio * CMP_STRIDE, HEAD_DIM, HEAD_DIM))
    w2bd = block_diag(cmp_w2)
    pe_r = cmp_pe.reshape(2, ratio, CMP_STRIDE, HEAD_DIM).astype(F32)
    pe_w = jnp.einsum("krsd,krsde->kre", pe_r, w1r.astype(F32), precision=lax.Precision.HIGHEST)
    pe_t = jnp.tile(pe_w.reshape(2 * ratio, HEAD_DIM), (1, N_KV_HEADS))
    return wbd, pe_t, w2bd


def _overlap(nc, ncp, nsel, nsp):
    c_start = np.arange(ncp)[:, None] * CMP_STRIDE
    s_start = np.arange(nsp)[None, :] * SEL_BLOCK
    ov = (c_start < s_start + SEL_BLOCK) & (c_start + CMP_BLOCK > s_start)
    ov = ov & (np.arange(ncp)[:, None] < nc) & (np.arange(nsp)[None, :] < nsel)
    return jnp.asarray(ov, BF16)


def _tile(n, pref):
    t = min(n, pref)
    while n % t:
        t //= 2
    return t


def kernel(x_prompt, x_sample, cache_kv_cmp, cache_kv_sel, cache_kv_win, state_ssm, page_table, norm_g, mlp_w1,
           mlp_w2, nsa_w_in, nsa_w_o, nsa_cmp_w1, nsa_cmp_w2, nsa_cmp_pe, s5_a_re, s5_a_im, s5_log_dt, s5_b_re,
           s5_b_im, s5_c_re, s5_c_im, s5_d, s5_w_glu, s5_b_glu):
    b, t, d = x_prompt.shape
    nb = x_sample.shape[0]
    pages_per_sample = page_table.shape[1]
    past = pages_per_sample * PAGE_SIZE
    rows_p = b * t
    g = norm_g.reshape(norm_g.shape[0], 4, 1, d)

    w_in = nsa_w_in[0]
    w_q = w_in[:, :Q_W].astype(BF16)
    w_kv = w_in[:, Q_W:Q_W + 6 * KV_W].astype(BF16)
    w_gate = jnp.pad(w_in[:, Q_W + 6 * KV_W:], ((0, 0), (0, LANES - 3 * N_HEADS))).astype(BF16)
    w_o = nsa_w_o[0].astype(BF16)
    wbd, pe_t, w2bd = _compress_weights(nsa_cmp_w1[0], nsa_cmp_w2[0], nsa_cmp_pe[0])
    w1 = mlp_w1.astype(BF16)
    w2 = mlp_w2.astype(BF16)
    w_glu = s5_w_glu[0].astype(BF16)
    ops = _s5_operators(s5_a_re[0], s5_a_im[0], s5_log_dt[0], s5_b_re[0], s5_b_im[0], s5_c_re[0], s5_c_im[0])
    d_skip = s5_d[0].reshape(1, d)
    b_glu = s5_b_glu[0].reshape(1, d)

    tm = _tile(rows_p, 512)
    ff_chunk = _tile(mlp_w1.shape[2], 1024)

    xp = x_prompt.reshape(rows_p, d)
    cos_p, sin_p = _rope_tables(jnp.arange(t, dtype=jnp.int32))
    tq = next(c for c in (2 * LANES, LANES) if t % c == 0 and WINDOW % c == 0)
    (qT, qrT, gT, kvc, kvcT, kvsT, kvwT, ksb, kwb, vsT, vwT) = _inproj(
        xp, g[0, 0], w_q.T, w_kv, w_gate, cos_p, sin_p, tm=_tile(t, 512), pos_blocks=t // _tile(t, 512), transposed=True,
        key_chunk=tq)
    nsb_p = t // CMP_STRIDE
    nc_p = nsb_p - CMP_BLOCK // CMP_STRIDE + 1
    nsel_p = t // SEL_BLOCK
    kc, vcT = _compress_prompt(kvc.reshape(b, t, 2 * KV_W), wbd, pe_t, w2bd)
    ovT = _overlap(nc_p, nsb_p, nsel_p, nsel_p).T
    oT = _attn_prompt(qT, qrT, gT, kc, vcT, ksb, vsT, kwb, vwT, ovT, batch=b, seq=t, tq=tq, nc=nc_p,
                      n_top=min(TOP_N, nsel_p))
    hp, xn1 = _layer_tail((oT, w_o), xp, g[0, 1], g[0, 2], w1[0], w2[0], g[0, 3], g[1, 0], mixer="nsa", tm=tm,
                          ff_chunk=ff_chunk, transposed=True)

    pw = SSM_UNIT * SSM_GROUP
    npair = d // pw
    y3, hfin = _s5_seq(xn1.reshape(b, t, d), ops)
    yc = y3.reshape(rows_p, d)
    (hp,) = _layer_tail((yc, xn1, d_skip, w_glu, b_glu), hp, g[1, 1], g[1, 2], w1[1], w2[1], g[1, 3], None,
                        mixer="s5", tm=tm, ff_chunk=ff_chunk)
    ssm_p = hfin.reshape(npair, 2, b, SSM_UNIT, SSM_STATE).transpose(2, 1, 0, 3, 4)
    ssm_p = ssm_p.reshape(b, 2, d // SSM_GROUP, SSM_STATE)

    xs = x_sample.reshape(nb, d)
    cos_s, sin_s = _rope_tables(jnp.full((nb,), past, dtype=jnp.int32))
    q_s, qr_s, gates_s, kvc_s, kvs_s, kvw_s = _inproj(
        xs, g[0, 0], w_q, w_kv, w_gate, cos_s, sin_s, tm=nb, pos_blocks=1, transposed=False)
    pt_flat = page_table.reshape(-1).astype(jnp.int32)
    pg = _tile(pages_per_sample, PAGE_GROUP)
    n_pool = cache_kv_cmp.shape[1]
    feature_major = lambda c, n, s: c.transpose(0, 2, 3, 4, 1).reshape(n, 2, KV_W, s)
    cmp_pages = feature_major(cache_kv_cmp[0], n_pool, PAGE_SIZE)
    sel_pages = feature_major(cache_kv_sel[0], n_pool, PAGE_SIZE)
    kcvc = _compress_paged(cmp_pages, pt_flat, wbd, pe_t, w2bd, nb=nb, pages_per_sample=pages_per_sample, pg=pg)
    l_all = past + 1
    nsb_s = l_all // CMP_STRIDE
    nc_s = nsb_s - CMP_BLOCK // CMP_STRIDE + 1
    nsel_s = -(-l_all // SEL_BLOCK)
    nsp = -(-nsel_s // LANES) * LANES
    ov_s = _overlap(nc_s, past // CMP_STRIDE, nsel_s, nsp)
    keys_per_step = pg * PAGE_SIZE
    key_blk = (np.arange(past) // SEL_BLOCK).reshape(past // keys_per_step, 1, keys_per_step)
    e_mat = jnp.asarray(np.arange(nsp)[None, :, None] == key_blk, BF16)
    hh = np.arange(N_HEADS)
    gsum = jnp.asarray((hh[:, None] // GROUP) == (hh[None, :] // GROUP), BF16)
    win = feature_major(cache_kv_win[0], nb, WINDOW)
    o_s = _attn_sample(sel_pages, pt_flat, q_s.reshape(nb, N_HEADS, HEAD_DIM), qr_s.reshape(nb, N_HEADS, HEAD_DIM),
                       gates_s[:, :3 * N_HEADS].reshape(nb, N_HEADS, 3), kcvc, kvs_s.reshape(nb, 1, 2 * KV_W),
                       kvw_s.reshape(nb, 1, 2 * KV_W), win, ov_s, e_mat, gsum, nb=nb,
                       pages_per_sample=pages_per_sample, pg=pg, past=past, nc=nc_s, ns_valid=nsel_s,
                       n_top=min(TOP_N, nsel_s))
    hs, xn1_s = _layer_tail((o_s.reshape(nb, Q_W), w_o), xs, g[0, 1], g[0, 2], w1[0], w2[0], g[0, 3], g[1, 0],
                            mixer="nsa", tm=nb, ff_chunk=ff_chunk)

    u2_s = xn1_s.astype(BF16).reshape(nb, npair, pw).transpose(1, 0, 2)
    st = state_ssm[0].reshape(nb, 2, npair, SSM_UNIT * SSM_STATE).transpose(1, 2, 0, 3)
    y2_s, hr_s, hi_s = _s5_step(u2_s, st[0], st[1], ops)
    yc_s = y2_s.transpose(1, 0, 2).reshape(nb, d)
    (hs,) = _layer_tail((yc_s, xn1_s, d_skip, w_glu, b_glu), hs, g[1, 1], g[1, 2], w1[1], w2[1], g[1, 3], None,
                        mixer="s5", tm=nb, ff_chunk=ff_chunk)
    ssm_s = jnp.stack([hr_s, hi_s], axis=0).transpose(2, 0, 1, 3).reshape(nb, 2, d // SSM_GROUP, SSM_STATE)

    kv5 = lambda a, n, s: a.reshape(1, n, s, 2, N_KV_HEADS, HEAD_DIM)
    from_fm = lambda a, n, s: a.reshape(n, 2, N_KV_HEADS, HEAD_DIM, s).transpose(0, 4, 1, 2, 3)[None]
    win_s = jnp.concatenate([win[..., 1:], kvw_s.reshape(nb, 2, KV_W, 1)], axis=-1)
    return (hp.reshape(b, t, d), hs.reshape(nb, 1, d),
            from_fm(kvcT, b, t), kv5(kvc_s, nb, 1), from_fm(kvsT, b, t), kv5(kvs_s, nb, 1),
            from_fm(kvwT[:, :, t - WINDOW:], b, WINDOW), from_fm(win_s, nb, WINDOW), ssm_p[None], ssm_s[None])
```

```python
import functools

import jax
import jax.numpy as jnp
import numpy as np
from jax import lax
from jax.experimental import pallas as pl
from jax.experimental.pallas import tpu as pltpu

N_HEADS = 16
HEAD_DIM = 64
N_KV_HEADS = 4
GROUP = N_HEADS // N_KV_HEADS
CMP_BLOCK = 32
CMP_STRIDE = 16
SEL_BLOCK = 64
TOP_N = 16
WINDOW = 512
ROPE_THETA = 10000.0
PAGE_SIZE = 128
SSM_GROUP = 16
SSM_STATE = 64
SSM_CHUNK = 8
SSM_UNIT = 8
EPS = 1e-6
NEG = -1e30
FORCE = 1e9
MASKED = -1.5e38
LOG2E = 1.4426950408889634
V_ROWS = HEAD_DIM + 16
Q_W = N_HEADS * HEAD_DIM
KV_W = N_KV_HEADS * HEAD_DIM
LANES = 128
SUBLANES = 8
PAGE_GROUP = 16
SAMPLES_PER_STEP = 2
S5_TIME_BLOCK = 2048
VMEM_LIMIT = 56 * 1024 * 1024

F32 = jnp.float32
BF16 = jnp.bfloat16


def _params(*sem):
    return pltpu.CompilerParams(dimension_semantics=sem, vmem_limit_bytes=VMEM_LIMIT)


def _full(shape):
    zeros = (0,) * len(shape)
    return pl.BlockSpec(shape, lambda *_: zeros)


def _rms(x, g):
    ms = jnp.mean(x * x, axis=-1, keepdims=True)
    return x * lax.rsqrt(ms + EPS) * g


def _dot(a, b):
    return jnp.dot(a, b, preferred_element_type=F32)


def _dot_f32lhs(w, x):
    hi = x.astype(BF16)
    r1 = x - hi.astype(F32)
    mid = r1.astype(BF16)
    lo = (r1 - mid.astype(F32)).astype(BF16)
    return _dot(w, hi) + _dot(w, mid) + _dot(w, lo)


def _dot_f32rhs(x, w):
    hi = x.astype(BF16)
    r1 = x - hi.astype(F32)
    mid = r1.astype(BF16)
    lo = (r1 - mid.astype(F32)).astype(BF16)
    return _dot(hi, w) + _dot(mid, w) + _dot(lo, w)


def _rope_nat(x, cos, sin):
    half = HEAD_DIM // 2
    lane = lax.broadcasted_iota(jnp.int32, (1, LANES), 1)
    first = (lane % HEAD_DIM) < half
    outs = []
    for c in range(x.shape[1] // LANES):
        xc = x[:, c * LANES:(c + 1) * LANES]
        rot = jnp.where(first, pltpu.roll(xc, LANES - half, 1), pltpu.roll(xc, half, 1))
        outs.append(xc * cos + rot * sin)
    return jnp.concatenate(outs, axis=1)


def _inproj_kernel(x_ref, g_ref, wq_ref, wkv_ref, wg_ref, cos_ref, sin_ref, *rest, transposed, key_chunk):
    outs = rest[2:] if transposed else rest
    xb = _rms(x_ref[...], g_ref[...]).astype(BF16)
    cos = cos_ref[...]
    sin = sin_ref[...]
    scale = HEAD_DIM ** -0.5
    kv = _dot(xb, wkv_ref[...])
    gates = jax.nn.sigmoid(_dot(xb, wg_ref[...]))
    k_s = _rope_nat(kv[:, 2 * KV_W:3 * KV_W], cos, sin)
    v_s = kv[:, 3 * KV_W:4 * KV_W]
    k_w = _rope_nat(kv[:, 4 * KV_W:5 * KV_W], cos, sin)
    v_w = kv[:, 5 * KV_W:6 * KV_W]
    kvc_ref = outs[3]
    kvc_ref[...] = kv[:, 0:2 * KV_W]
    if transposed:
        qT_ref, qrT_ref, gT_ref, _, kvcT_ref, kvsT_ref, kvwT_ref, ksb_ref, kwb_ref, vsT_ref, vwT_ref = outs
        qT = _dot_nt(wq_ref[...], xb)
        cosT = rest[0][...]
        sinT = rest[1][...]
        half = HEAD_DIM // 2
        rotated = []
        for h in range(N_HEADS):
            x1 = qT[h * HEAD_DIM:h * HEAD_DIM + half, :]
            x2 = qT[h * HEAD_DIM + half:(h + 1) * HEAD_DIM, :]
            rotated += [x1 * cosT - x2 * sinT, x2 * cosT + x1 * sinT]
        qT_ref[...] = (qT * (scale * LOG2E)).astype(BF16)
        qrT_ref[...] = (jnp.concatenate(rotated, axis=0) * (scale * LOG2E)).astype(BF16)
        gT_ref[...] = gates.T
        tm = x_ref.shape[0]
        rowi = lax.broadcasted_iota(jnp.int32, (tm, LANES), 0)
        lanei = lax.broadcasted_iota(jnp.int32, (tm, LANES), 1)
        blk = lax.shift_right_logical(rowi & (key_chunk - 1), SEL_BLOCK.bit_length() - 1)
        extra = lanei - HEAD_DIM
        nblk = key_chunk // SEL_BLOCK
        aug = jnp.where((extra == blk) | ((extra >= nblk) & (extra < nblk + N_CONST_LANES)), 1.0, 0.0)
        for c in range(KV_W // LANES):
            for k_nat, k_ref in ((k_s, ksb_ref), (k_w, kwb_ref)):
                pair = k_nat[:, c * LANES:(c + 1) * LANES]
                k_ref[2 * c] = jnp.where(lanei < HEAD_DIM, pair, aug).astype(BF16)
                k_ref[2 * c + 1] = jnp.where(lanei < HEAD_DIM, pltpu.roll(pair, HEAD_DIM, 1), aug).astype(BF16)
        v_sT = v_s.T
        v_wT = v_w.T
        ones_rows = jnp.where(lax.broadcasted_iota(jnp.int32, (V_ROWS - HEAD_DIM, tm), 0) == 0, 1.0, 0.0)
        for h in range(N_KV_HEADS):
            for vT, v_ref in ((v_sT, vsT_ref), (v_wT, vwT_ref)):
                v_ref[h * V_ROWS:(h + 1) * V_ROWS, :] = jnp.concatenate(
                    [vT[h * HEAD_DIM:(h + 1) * HEAD_DIM, :], ones_rows], axis=0).astype(BF16)
        kvcT_ref[0] = kv[:, 0:2 * KV_W].T
        kvsT_ref[0, 0:KV_W, :] = k_s.T
        kvsT_ref[0, KV_W:2 * KV_W, :] = v_sT
        kvwT_ref[0, 0:KV_W, :] = k_w.T
        kvwT_ref[0, KV_W:2 * KV_W, :] = v_wT
    else:
        q_ref, qr_ref, gt_ref, _, kvs_ref, kvw_ref = outs
        q = _dot(xb, wq_ref[...])
        qr = _rope_nat(q, cos, sin)
        q_ref[...] = (q * scale).astype(BF16)
        qr_ref[...] = (qr * scale).astype(BF16)
        gt_ref[...] = gates
        kvs_ref[:, 0:KV_W] = k_s
        kvs_ref[:, KV_W:2 * KV_W] = v_s
        kvw_ref[:, 0:KV_W] = k_w
        kvw_ref[:, KV_W:2 * KV_W] = v_w


def _inproj(x, g, w_q, w_kv, w_gate, cos_t, sin_t, *, tm, pos_blocks, transposed, key_chunk=LANES):
    rows, d = x.shape
    assert tm % key_chunk == 0 or not transposed
    n = rows // tm
    row_blk = lambda w: pl.BlockSpec((tm, w), lambda i: (i, 0))
    col_blk = lambda h: pl.BlockSpec((h, tm), lambda i: (0, i))
    tab = pl.BlockSpec((tm, LANES), lambda i: (i % pos_blocks, 0))
    kv_nat = jax.ShapeDtypeStruct((rows, 2 * KV_W), F32)
    if transposed:
        seqs = rows // (pos_blocks * tm)
        kh = pl.BlockSpec((N_KV_HEADS, tm, LANES), lambda i: (0, i, 0))
        kvT = pl.BlockSpec((1, 2 * KV_W, tm), lambda i: (i // pos_blocks, 0, i % pos_blocks))
        out_shape = ([jax.ShapeDtypeStruct((Q_W, rows), BF16)] * 2 + [jax.ShapeDtypeStruct((LANES, rows), F32)]
                     + [kv_nat] + [jax.ShapeDtypeStruct((seqs, 2 * KV_W, pos_blocks * tm), F32)] * 3
                     + [jax.ShapeDtypeStruct((N_KV_HEADS, rows, LANES), BF16)] * 2
                     + [jax.ShapeDtypeStruct((N_KV_HEADS * V_ROWS, rows), BF16)] * 2)
        out_specs = ([col_blk(Q_W)] * 2 + [col_blk(LANES)] + [row_blk(2 * KV_W)] + [kvT] * 3 + [kh] * 2
                     + [col_blk(N_KV_HEADS * V_ROWS)] * 2)
    else:
        out_shape = ([jax.ShapeDtypeStruct((rows, Q_W), BF16)] * 2 + [jax.ShapeDtypeStruct((rows, LANES), F32)]
                     + [kv_nat] * 3)
        out_specs = [row_blk(Q_W)] * 2 + [row_blk(LANES)] + [row_blk(2 * KV_W)] * 3
    tables = [cos_t, sin_t]
    table_specs = [tab, tab]
    if transposed:
        half = HEAD_DIM // 2
        tables += [cos_t[:, 0:half].T, sin_t[:, half:HEAD_DIM].T]
        table_specs += [pl.BlockSpec((half, tm), lambda i: (0, i % pos_blocks))] * 2
    return pl.pallas_call(
        functools.partial(_inproj_kernel, transposed=transposed, key_chunk=key_chunk),
        grid=(n,),
        in_specs=[row_blk(d), _full((1, d)), _full(w_q.shape), _full(w_kv.shape), _full(w_gate.shape)] + table_specs,
        out_specs=out_specs,
        out_shape=out_shape,
        compiler_params=_params("parallel"),
        name="nsa_inproj",
    )(x, g, w_q, w_kv, w_gate, *tables)


_KV_CHUNKS = 2 * KV_W // LANES


def _compress_half(load_rows, kv, nrows, wbd_ref, pe_ref):
    ratio = CMP_BLOCK // CMP_STRIDE
    accs = [jnp.broadcast_to(pe_ref[kv * ratio + r:kv * ratio + r + 1, :], (nrows, KV_W)) for r in range(ratio)]
    for s in range(CMP_STRIDE):
        lhs = load_rows(s, kv).astype(BF16)
        for r in range(ratio):
            accs[r] = accs[r] + _dot(lhs, wbd_ref[(kv * ratio + r) * CMP_STRIDE + s])
    return accs


def _compress_prompt_kernel(*refs, nsb):
    x_refs = refs[:_KV_CHUNKS]
    wbd_ref, pe_ref, w2_ref, kc_ref, vcT_ref, sh_ref = refs[_KV_CHUNKS:]

    def load_rows(s, kv):
        per_half = _KV_CHUNKS // 2
        return jnp.concatenate([x_refs[kv * per_half + c][0, pl.ds(s, nsb, stride=CMP_STRIDE), :]
                                for c in range(per_half)], axis=1)

    sh_ref[nsb:nsb + SUBLANES, :] = jnp.zeros((SUBLANES, KV_W), F32)
    for kv in range(2):
        pr0, pr1 = _compress_half(load_rows, kv, nsb, wbd_ref, pe_ref)
        sh_ref[0:nsb, :] = pr1
        h = pr0 + sh_ref[pl.ds(1, nsb), :]
        out = _dot(jax.nn.gelu(h).astype(BF16), w2_ref[kv])
        if kv == 0:
            for hh in range(N_KV_HEADS):
                kc_ref[0, hh] = out[:, hh * HEAD_DIM:(hh + 1) * HEAD_DIM].astype(BF16)
        else:
            vcT_ref[0] = out.T.astype(BF16)


def _compress_prompt(kvc3, wbd, pe_t, w2bd):
    b, t, _ = kvc3.shape
    nsb = t // CMP_STRIDE
    return pl.pallas_call(
        functools.partial(_compress_prompt_kernel, nsb=nsb),
        grid=(b,),
        in_specs=[pl.BlockSpec((1, t, LANES), lambda i, c=c: (i, 0, c)) for c in range(_KV_CHUNKS)]
        + [_full(wbd.shape), _full(pe_t.shape), _full(w2bd.shape)],
        out_specs=[pl.BlockSpec((1, N_KV_HEADS, nsb, HEAD_DIM), lambda i: (i, 0, 0, 0)),
                   pl.BlockSpec((1, KV_W, nsb), lambda i: (i, 0, 0))],
        out_shape=[jax.ShapeDtypeStruct((b, N_KV_HEADS, nsb, HEAD_DIM), BF16),
                   jax.ShapeDtypeStruct((b, KV_W, nsb), BF16)],
        scratch_shapes=[pltpu.VMEM((nsb + SUBLANES, KV_W), F32)],
        compiler_params=_params("parallel"),
        name="nsa_compress_prompt",
    )(*([kvc3] * _KV_CHUNKS), wbd, pe_t, w2bd)


def _topk_mask_T(imp, n_top):
    ns, w = imp.shape
    nblk = ns // SUBLANES
    blocks = [imp[r * SUBLANES:(r + 1) * SUBLANES, :] for r in range(nblk)]
    cnts = [jnp.zeros((SUBLANES, w), F32) for _ in range(nblk)]
    sub = lax.broadcasted_iota(jnp.int32, (SUBLANES, w), 0)
    for sp in range(ns):
        row = blocks[sp // SUBLANES][sp % SUBLANES:sp % SUBLANES + 1, :]
        for r in range(nblk):
            blk = blocks[r]
            if sp < r * SUBLANES:
                beats = jnp.where(row >= blk, 1.0, 0.0)
            elif sp >= (r + 1) * SUBLANES:
                beats = jnp.where(row > blk, 1.0, 0.0)
            else:
                beats = jnp.where(sub > (sp - r * SUBLANES), jnp.where(row >= blk, 1.0, 0.0),
                                  jnp.where(row > blk, 1.0, 0.0))
            cnts[r] = cnts[r] + beats
    return jnp.concatenate([jnp.where(c < n_top, 1.0, 0.0) for c in cnts], axis=0)


def _online_chunks(states, k_cs, vT_cs, qTs, bias):
    scores = [_dot(k_c, qT) for k_c, qT in zip(k_cs, qTs)]
    mids = []
    for (m, _), s in zip(states, scores):
        if bias is not None:
            s = s + bias
        m_new = jnp.maximum(m, jnp.max(s, axis=0, keepdims=True))
        mids.append((m_new, jnp.exp2(m - m_new), jnp.exp2(s - m_new).astype(BF16)))
    return tuple((m_new, alpha * acc + _dot(vT_c, p))
                 for (m_new, alpha, p), (_, acc), vT_c in zip(mids, states, vT_cs))


def _softmax_finish(carry):
    _, acc = carry
    return acc[0:HEAD_DIM, :] * (1.0 / jnp.maximum(acc[HEAD_DIM:HEAD_DIM + 1, :], 1e-20))


LAZY_LOG2_MAX = 60.0
N_CONST_LANES = 4


def _split3(x):
    hi = x.astype(BF16).astype(F32)
    r = x - hi
    mid = r.astype(BF16).astype(F32)
    return hi, mid, (r - mid).astype(BF16).astype(F32)


def _rescaling_chunks(m_ref, acc_ref, k_cs, vT_cs, qTs, base_tiles, bias):
    nh = len(k_cs)
    rows, w = base_tiles[0].shape
    zero_rows = jnp.zeros((LANES - HEAD_DIM - rows, w), BF16)
    new = _online_chunks(tuple((m_ref[h], acc_ref[h]) for h in range(nh)), k_cs, vT_cs,
                         [jnp.concatenate([qTs[h], base_tiles[h].astype(BF16), zero_rows], axis=0) for h in range(nh)],
                         bias)
    for h in range(nh):
        m_ref[h] = new[h][0]
        acc_ref[h] = new[h][1]


def _lazy_chunks(m_ref, acc2_ref, slot_ref, k_cs, vT_cs, qTs, base_tiles, bias, ref_row, n_chunks=1):
    slot = slot_ref[0]
    acc_ref = acc2_ref.at[slot]
    nu = len(k_cs)
    nh = nu // n_chunks
    biases = list(bias) if isinstance(bias, (list, tuple)) else [bias] * n_chunks
    rows = base_tiles[0].shape[0]
    w = base_tiles[0].shape[1]
    rowt = lax.broadcasted_iota(jnp.int32, (rows, w), 0)
    zero_rows = jnp.zeros((LANES - HEAD_DIM - rows, w), BF16)

    def queries(u):
        hi, mid, lo = _split3(-m_ref[u % nh])
        tile = jnp.where(rowt == ref_row, hi, jnp.where(rowt == ref_row + 1, mid,
                                                        jnp.where(rowt == ref_row + 2, lo, base_tiles[u])))
        return jnp.concatenate([qTs[u % nh], tile.astype(BF16), zero_rows], axis=0)

    scores = [_dot(k_cs[u], queries(u)) for u in range(nu)]
    peak = None
    probs = []
    for u, s in enumerate(scores):
        if biases[u // nh] is not None:
            s = s + biases[u // nh]
        top = jnp.max(s, axis=0, keepdims=True)
        peak = top if peak is None else jnp.maximum(peak, top)
        probs.append(jnp.exp2(s).astype(BF16))
    pvs = [_dot(vT_cs[u], probs[u]) for u in range(nu)]
    for h in range(nh):
        total = pvs[h]
        for c in range(1, n_chunks):
            total = total + pvs[c * nh + h]
        acc2_ref[1 - slot, h] = acc_ref[h] + total
    in_range = jnp.max(peak) <= LAZY_LOG2_MAX

    @pl.when(in_range)
    def _():
        slot_ref[0] = 1 - slot

    @pl.when(jnp.logical_not(in_range))
    def _():
        for c in range(n_chunks):
            part = slice(c * nh, (c + 1) * nh)
            _rescaling_chunks(m_ref, acc_ref, k_cs[part], vT_cs[part], qTs, base_tiles[part], biases[c])


def _attn_prompt_kernel(qT_ref, qrT_ref, gT_ref, kc_ref, vcT_ref, ks_ref, vsT_ref, kw_ref, vwT_ref, ovT_ref,
                        o_ref, selb_ref, oc_ref, ow_ref, m_ref, acc2_ref, slot_ref, *, tq, nc, n_top):
    ck = tq
    i = pl.program_id(1)
    t0 = i * tq
    qpos = t0 + lax.broadcasted_iota(jnp.int32, (1, tq), 1)
    ncp = kc_ref.shape[2]
    ns = ovT_ref.shape[0]
    w = GROUP * tq
    bpc = ck // SEL_BLOCK
    sel_shift = SEL_BLOCK.bit_length() - 1
    kvhs = range(N_KV_HEADS)
    heads = [[kvh * GROUP + g for g in range(GROUP)] for kvh in kvhs]
    rows = [pl.ds(kvh * HEAD_DIM, HEAD_DIM) for kvh in kvhs]
    vrows = [pl.ds(kvh * V_ROWS, V_ROWS) for kvh in kvhs]
    qrT = [jnp.concatenate([qrT_ref[h * HEAD_DIM:(h + 1) * HEAD_DIM, :] for h in heads[kvh]], axis=1)
           for kvh in kvhs]
    kl = lax.broadcasted_iota(jnp.int32, (ck, tq), 0)
    ql = lax.broadcasted_iota(jnp.int32, (ck, tq), 1)
    tile4 = lambda b: jnp.concatenate([b] * GROUP, axis=1)
    key_le_query = tile4(jnp.where(kl <= ql, 0.0, MASKED))
    key_ge_query = tile4(jnp.where(kl >= ql, 0.0, MASKED))
    bias_rows = selb_ref.shape[2]

    def reset_softmax():
        slot_ref[0] = 0
        for kvh in kvhs:
            m_ref[kvh] = jnp.full((1, w), NEG, F32)
            acc2_ref[0, kvh] = jnp.zeros((V_ROWS, w), F32)

    def current_acc(kvh):
        return acc2_ref[slot_ref[0], kvh]

    cidx = lax.broadcasted_iota(jnp.int32, (ncp, 1), 0)
    valid = ((cidx * CMP_STRIDE + (CMP_BLOCK - 1)) <= qpos) & (cidx < nc)
    sidx = lax.broadcasted_iota(jnp.int32, (ns, 1), 0)
    cur = lax.shift_right_logical(qpos, sel_shift)
    forced = (sidx == 0) | (sidx == cur) | (sidx == cur - 1)
    causal = (sidx * SEL_BLOCK) <= qpos
    cmp_scores = [
        _dot(kc_ref[0, kvh], jnp.concatenate([qT_ref[h * HEAD_DIM:(h + 1) * HEAD_DIM, :] for h in heads[kvh]], axis=1))
        for kvh in kvhs]
    importance = []
    for kvh in kvhs:
        s = cmp_scores[kvh]
        probs = []
        for g in range(GROUP):
            sm = jnp.where(valid, s[:, g * tq:(g + 1) * tq], NEG)
            mx = jnp.max(sm, axis=0, keepdims=True)
            e = jnp.where(valid, jnp.exp2(sm - mx), 0.0)
            den = jnp.maximum(jnp.sum(e, axis=0, keepdims=True), 1e-20)
            probs.append(e / den)
        oc_ref[kvh] = _dot(vcT_ref[0, rows[kvh], :], jnp.concatenate(probs, axis=1).astype(BF16))
        psum = probs[0]
        for g in range(1, GROUP):
            psum = psum + probs[g]
        imp = _dot_f32lhs(ovT_ref[...], psum)
        importance.append(jnp.where(forced, FORCE, jnp.where(causal, imp, NEG)))

    prefixes = sorted({r for r in (ns // 4, ns // 2, 3 * ns // 4) if r and r % SUBLANES == 0} | {ns})
    needed = (i + 1) * bpc
    fill = jnp.zeros((bias_rows - bpc, tq), F32)
    for lo, hi in zip([0] + prefixes[:-1], prefixes):
        @pl.when((needed > lo) & (needed <= hi))
        def _(hi=hi):
            for kvh in kvhs:
                selb = (_topk_mask_T(importance[kvh][0:hi, :], min(n_top, hi)) - 1.0) * (-MASKED)
                for c in range(hi // bpc):
                    selb_ref[kvh, c] = jnp.concatenate([selb[c * bpc:(c + 1) * bpc, :], fill], axis=0)

    n_back = WINDOW // ck
    reset_softmax()
    rowb = lax.broadcasted_iota(jnp.int32, (bias_rows, w), 0)
    def window_chunk(r):
        a = i - n_back + r
        kst = pl.multiple_of(jnp.maximum(a, 0) * ck, ck)
        skip = jnp.where(a < 0, MASKED, 0.0)
        tile = jnp.where(rowb == bpc, skip, 0.0)
        bias = key_le_query if r == n_back else (key_ge_query if r == 0 else None)
        return ([kw_ref[kvh, pl.ds(kst, ck), :] for kvh in kvhs],
                [vwT_ref[vrows[kvh], pl.ds(kst, ck)] for kvh in kvhs], [tile] * N_KV_HEADS, bias)

    k_cs, vT_cs, tiles, bias = window_chunk(0)
    _rescaling_chunks(m_ref, acc2_ref.at[0], k_cs, vT_cs, qrT, tiles, bias)
    for r0 in range(1, n_back + 1, 2):
        group = [window_chunk(r) for r in range(r0, min(r0 + 2, n_back + 1))]
        _lazy_chunks(m_ref, acc2_ref, slot_ref, sum((g[0] for g in group), []), sum((g[1] for g in group), []),
                     qrT, sum((g[2] for g in group), []), [g[3] for g in group], bpc + 1, len(group))
    for kvh in kvhs:
        ow_ref[kvh] = _softmax_finish((None, current_acc(kvh)))

    def chunk_step(c, diagonal, lazy=True, n_chunks=1):
        ks, vs, tiles = [], [], []
        for dc in range(n_chunks):
            kst = (c + dc) * ck if isinstance(c, int) else pl.multiple_of((c + dc) * ck, ck)
            ks += [ks_ref[kvh, pl.ds(kst, ck), :] for kvh in kvhs]
            vs += [vsT_ref[vrows[kvh], pl.ds(kst, ck)] for kvh in kvhs]
            tiles += [tile4(selb_ref[kvh, c + dc]) for kvh in kvhs]
        bias = key_le_query if diagonal else None
        if lazy:
            _lazy_chunks(m_ref, acc2_ref, slot_ref, ks, vs, qrT, tiles, bias, bpc + 1, n_chunks)
        else:
            _rescaling_chunks(m_ref, acc2_ref.at[slot_ref[0]], ks, vs, qrT, tiles, bias)

    reset_softmax()

    @pl.when(i > 0)
    def _():
        chunk_step(0, False, lazy=False)

    n_mid = jnp.maximum(i - 1, 0)

    def loop_body(p, carry):
        chunk_step(1 + 2 * p, False, n_chunks=2)
        return carry

    lax.fori_loop(0, n_mid // 2, loop_body, 0)

    @pl.when(n_mid % 2 == 1)
    def _():
        chunk_step(i - 1, False)

    chunk_step(i, True)

    for kvh in kvhs:
        def gate_row(j, kvh=kvh):
            return jnp.concatenate([gT_ref[h * 3 + j:h * 3 + j + 1, :] for h in heads[kvh]], axis=1)

        oT = (gate_row(0) * oc_ref[kvh] + gate_row(1) * _softmax_finish((None, current_acc(kvh)))
              + gate_row(2) * ow_ref[kvh])
        for g, h in enumerate(heads[kvh]):
            o_ref[h * HEAD_DIM:(h + 1) * HEAD_DIM, :] = oT[:, g * tq:(g + 1) * tq].astype(BF16)


def _attn_prompt(qT, qrT, gT, kc, vcT, ksb, vsT, kwb, vwT, ovT, *, batch, seq, tq, nc, n_top):
    nq = seq // tq
    nsb = kc.shape[2]
    ns = ovT.shape[0]
    col = lambda h: pl.BlockSpec((h, tq), lambda b, i: (0, b * nq + i))
    kh = pl.BlockSpec((N_KV_HEADS, seq, LANES), lambda b, i: (0, b, 0))
    vt = pl.BlockSpec((N_KV_HEADS * V_ROWS, seq), lambda b, i: (0, b))
    bf16_sublanes = 2 * SUBLANES
    return pl.pallas_call(
        functools.partial(_attn_prompt_kernel, tq=tq, nc=nc, n_top=n_top),
        grid=(batch, nq),
        in_specs=[col(Q_W), col(Q_W), col(LANES),
                  pl.BlockSpec((1, N_KV_HEADS, nsb, HEAD_DIM), lambda b, i: (b, 0, 0, 0)),
                  pl.BlockSpec((1, KV_W, nsb), lambda b, i: (b, 0, 0)),
                  kh, vt, kh, vt, _full(ovT.shape)],
        out_specs=col(Q_W),
        out_shape=jax.ShapeDtypeStruct((Q_W, batch * seq), BF16),
        scratch_shapes=[pltpu.VMEM((N_KV_HEADS, ns * SEL_BLOCK // tq, bf16_sublanes, tq), F32),
                        pltpu.VMEM((N_KV_HEADS, HEAD_DIM, GROUP * tq), F32),
                        pltpu.VMEM((N_KV_HEADS, HEAD_DIM, GROUP * tq), F32),
                        pltpu.VMEM((N_KV_HEADS, 1, GROUP * tq), F32),
                        pltpu.VMEM((2, N_KV_HEADS, V_ROWS, GROUP * tq), F32),
                        pltpu.SMEM((1,), jnp.int32)],
        compiler_params=_params("parallel", "arbitrary"),
        name="nsa_attn_prompt",
    )(qT, qrT, gT, kc, vcT, ksb, vsT, kwb, vwT, ovT)


def _resident(shape):
    zeros = (0,) * len(shape)
    return pl.BlockSpec(shape, lambda *_: zeros, pipeline_mode=pl.Buffered(1))


def _layer_tail_kernel(*refs, mixer, transposed, ff_chunk, next_norm):
    if mixer == "nsa":
        o_ref, wo_ref, x_ref, g1_ref, g2_ref = refs[:5]
        rest = refs[5:]
        if transposed:
            y = lax.dot_general(o_ref[...], wo_ref[...], (((0,), (0,)), ((), ())), preferred_element_type=F32)
        else:
            y = _dot(o_ref[...], wo_ref[...])
    else:
        yc_ref, u_ref, d_ref, wg_ref, bg_ref, x_ref, g1_ref, g2_ref = refs[:8]
        rest = refs[8:]
        z = jax.nn.gelu(yc_ref[...] + d_ref[...] * u_ref[...])
        y = z * jax.nn.sigmoid(_dot(z.astype(BF16), wg_ref[...]) + bg_ref[...])
    w1_ref, w2_ref, g3_ref = rest[:3]
    if next_norm:
        gn_ref, h2_ref, xn_ref, acc_ref = rest[3:]
    else:
        h2_ref, acc_ref = rest[3:]
    h = x_ref[...] + _rms(y, g1_ref[...])
    xm = _rms(h, g2_ref[...]).astype(BF16)
    for c in range(w1_ref.shape[1] // ff_chunk):
        cols = slice(c * ff_chunk, (c + 1) * ff_chunk)
        hm = jnp.maximum(_dot(xm, w1_ref[:, cols]), 0.0)
        part = _dot((hm * hm).astype(BF16), w2_ref[cols, :])
        if c == 0:
            acc_ref[...] = part
        else:
            acc_ref[...] += part
    h2 = h + _rms(acc_ref[...], g3_ref[...])
    h2_ref[...] = h2
    if next_norm:
        xn_ref[...] = _rms(h2, gn_ref[...])


def _layer_tail(mixer_args, x, g1, g2, w1, w2, g3, gn, *, mixer, tm, ff_chunk, transposed=False):
    rows, d = x.shape
    row = pl.BlockSpec((tm, d), lambda i: (i, 0))
    vec = _resident((1, d))
    if mixer == "nsa":
        o, w_o = mixer_args
        o_spec = (pl.BlockSpec((Q_W, tm), lambda i: (0, i)) if transposed
                  else pl.BlockSpec((tm, Q_W), lambda i: (i, 0)))
        head_specs = [o_spec, _resident(w_o.shape)]
    else:
        w_glu = mixer_args[3]
        head_specs = [row, row, vec, _resident(w_glu.shape), vec]
    tail_args = (w1, w2, g3) + (() if gn is None else (gn,))
    tail_specs = [_resident(w1.shape), _resident(w2.shape), vec] + ([] if gn is None else [vec])
    n_out = 1 if gn is None else 2
    return pl.pallas_call(
        functools.partial(_layer_tail_kernel, mixer=mixer, transposed=transposed, ff_chunk=ff_chunk,
                          next_norm=gn is not None),
        grid=(rows // tm,),
        in_specs=head_specs + [row, vec, vec] + tail_specs,
        out_specs=[row] * n_out,
        out_shape=[jax.ShapeDtypeStruct((rows, d), F32)] * n_out,
        scratch_shapes=[pltpu.VMEM((tm, d), F32)],
        compiler_params=_params("parallel"),
        name=mixer + "_layer_tail",
    )(*mixer_args, x, g1, g2, *tail_args)


def _s5_seq_kernel(x_ref, brow_ref, pre_ref, pim_ref, qre_ref, qim_ref, are_ref, aim_ref, y_ref, hfin_ref,
                   sre, sim, hre, him, cre, cim):
    nb, tb, uw = x_ref.shape
    nt = pre_ref.shape[1] // uw
    nk = tb // nt

    @pl.when(pl.program_id(1) == 0)
    def _():
        cre[...] = jnp.zeros(cre.shape, F32)
        cim[...] = jnp.zeros(cim.shape, F32)

    u = [jnp.concatenate([x_ref[b, pl.ds(t, nk, stride=nt), :] for b in range(nb)], axis=0).astype(BF16)
         for t in range(nt)]
    u2 = [jnp.concatenate([u[2 * j], u[2 * j + 1]], axis=1) for j in range(nt // 2)]
    s_re = _dot(u2[0], pre_ref[0, 0:2 * uw, :])
    s_im = _dot(u2[0], pim_ref[0, 0:2 * uw, :])
    for j in range(1, nt // 2):
        s_re = s_re + _dot(u2[j], pre_ref[0, 2 * j * uw:(2 * j + 2) * uw, :])
        s_im = s_im + _dot(u2[j], pim_ref[0, 2 * j * uw:(2 * j + 2) * uw, :])
    sre[...] = s_re
    sim[...] = s_im
    ar = are_ref[0]
    ai = aim_ref[0]

    def body(it, carry):
        out = []
        for b in range(nb):
            hr, hi = carry[b]
            r0 = pl.multiple_of(b * nk + it * SUBLANES, SUBLANES)
            sr8 = sre[pl.ds(r0, SUBLANES), :]
            si8 = sim[pl.ds(r0, SUBLANES), :]
            prev_r, prev_i = [], []
            for j in range(SUBLANES):
                prev_r.append(hr)
                prev_i.append(hi)
                hr, hi = (ar * hr - ai * hi + sr8[j:j + 1, :], ar * hi + ai * hr + si8[j:j + 1, :])
            hre[pl.ds(r0, SUBLANES), :] = jnp.concatenate(prev_r, axis=0)
            him[pl.ds(r0, SUBLANES), :] = jnp.concatenate(prev_i, axis=0)
            out.append((hr, hi))
        return tuple(out)

    init = tuple((cre[b:b + 1, :], cim[b:b + 1, :]) for b in range(nb))
    fin = lax.fori_loop(0, nk // SUBLANES, body, init)
    for b in range(nb):
        cre[b:b + 1, :] = fin[b][0]
        cim[b:b + 1, :] = fin[b][1]
    hfin_ref[0, 0] = cre[...]
    hfin_ref[0, 1] = cim[...]

    hb_re = hre[...].astype(BF16)
    hb_im = him[...].astype(BF16)
    for t2 in range(0, nt, 2):
        cols = slice(t2 * uw, (t2 + 2) * uw)
        acc = _dot(hb_re, qre_ref[0, :, cols]) + _dot(hb_im, qim_ref[0, :, cols])
        for j in range(t2 // 2 + 1):
            lag0 = t2 - 2 * j + 1
            wpair = jnp.concatenate([brow_ref[0, :, lag0 * uw:(lag0 + 2) * uw],
                                     brow_ref[0, :, (lag0 - 1) * uw:(lag0 + 1) * uw]], axis=0)
            acc = acc + _dot(u2[j], wpair)
        for b in range(nb):
            for dt in range(2):
                y_ref[b, pl.ds(t2 + dt, nk, stride=nt), :] = acc[b * nk:(b + 1) * nk, dt * uw:(dt + 1) * uw]


def _s5_seq(x3, ops):
    nb, t, d = x3.shape
    uw = SSM_UNIT * SSM_GROUP
    nunit = d // uw
    sw = ops["p_re"].shape[2]
    tb = _tile(t, S5_TIME_BLOCK)
    nk = tb // SSM_CHUNK
    per_unit = lambda a: pl.BlockSpec((1,) + a.shape[1:], lambda i, r: (i,) + (0,) * (a.ndim - 1))
    blk = pl.BlockSpec((nb, tb, uw), lambda i, r: (0, r, i))
    args = (x3, ops["brow"], ops["p_re"], ops["p_im"], ops["q_re"], ops["q_im"], ops["a_chunk_re"], ops["a_chunk_im"])
    return pl.pallas_call(
        _s5_seq_kernel,
        grid=(nunit, t // tb),
        in_specs=[blk] + [per_unit(a) for a in args[1:]],
        out_specs=[blk, pl.BlockSpec((1, 2, nb, sw), lambda i, r: (i, 0, 0, 0))],
        out_shape=[jax.ShapeDtypeStruct((nb, t, d), F32), jax.ShapeDtypeStruct((nunit, 2, nb, sw), F32)],
        scratch_shapes=[pltpu.VMEM((nb * nk, sw), F32)] * 4 + [pltpu.VMEM((nb, sw), F32)] * 2,
        compiler_params=_params("parallel", "arbitrary"),
        name="s5_seq_scan",
    )(*args)


def _s5_step_kernel(u_ref, h0r_ref, h0i_ref, bre_ref, bim_ref, cre_ref, cim_ref, are_ref, aim_ref,
                    y_ref, hr_ref, hi_ref, *, npair):
    for p in range(npair):
        u = u_ref[p]
        ar = are_ref[p]
        ai = aim_ref[p]
        h0r = h0r_ref[p]
        h0i = h0i_ref[p]
        hr = ar * h0r - ai * h0i + _dot(u, bre_ref[p])
        hi = ar * h0i + ai * h0r + _dot(u, bim_ref[p])
        hr_ref[p] = hr
        hi_ref[p] = hi
        y_ref[p] = _dot(hr.astype(BF16), cre_ref[p]) + _dot(hi.astype(BF16), cim_ref[p])


def _s5_step(u2, h0r, h0i, ops):
    npair, rows, width = u2.shape
    args = (u2, h0r, h0i, ops["b1_re"], ops["b1_im"], ops["c1_re"], ops["c1_im"], ops["a1_re"], ops["a1_im"])
    return pl.pallas_call(
        functools.partial(_s5_step_kernel, npair=npair),
        grid=(1,),
        in_specs=[_full(a.shape) for a in args],
        out_specs=[_full((npair, rows, width)), _full(h0r.shape), _full(h0r.shape)],
        out_shape=[jax.ShapeDtypeStruct((npair, rows, width), F32), jax.ShapeDtypeStruct(h0r.shape, F32),
                   jax.ShapeDtypeStruct(h0r.shape, F32)],
        compiler_params=_params("arbitrary"),
        name="s5_single_step",
    )(*args)


def _s5_operators(a_re, a_im, log_dt, b_re, b_im, c_re, c_im):
    hp = lax.Precision.HIGHEST
    g, n = a_re.shape
    gu = SSM_UNIT
    nunit = g // gu
    L = SSM_CHUNK
    uw = gu * SSM_GROUP
    sw = gu * n
    a = lax.complex(a_re.astype(F32), a_im.astype(F32))
    dt = jnp.exp(log_dt.astype(F32))[:, None]
    a_bar = jnp.exp(a * dt)
    b_bar = ((a_bar - 1.0) / a)[:, :, None] * lax.complex(b_re.astype(F32), b_im.astype(F32))
    c = lax.complex(c_re.astype(F32), c_im.astype(F32))
    pows = [jnp.ones_like(a_bar)]
    for _ in range(L):
        pows.append(pows[-1] * a_bar)
    a_pow = jnp.stack(pows).reshape(L + 1, nunit, sw)
    apr = jnp.real(a_pow)
    api = jnp.imag(a_pow)
    eye = jnp.eye(gu, dtype=F32)

    def bd_in(x):
        return jnp.einsum("pgnd,gh->pgdhn", x.reshape(nunit, gu, n, SSM_GROUP), eye).reshape(nunit, uw, sw)

    def bd_out(x):
        return jnp.einsum("pgcn,gh->pgnhc", x.reshape(nunit, gu, SSM_GROUP, n), eye).reshape(nunit, sw, uw)

    p0r, p0i = bd_in(jnp.real(b_bar)), bd_in(jnp.imag(b_bar))
    q0r, q0i = bd_out(jnp.real(c)), bd_out(jnp.imag(c))
    lanes = lambda x, k: x[k][:, None, :]
    rows_ = lambda x, k: x[k][:, :, None]
    p_re = jnp.concatenate([p0r * lanes(apr, L - 1 - t) - p0i * lanes(api, L - 1 - t) for t in range(L)], axis=1)
    p_im = jnp.concatenate([p0r * lanes(api, L - 1 - t) + p0i * lanes(apr, L - 1 - t) for t in range(L)], axis=1)
    q_re = jnp.concatenate([q0r * rows_(apr, t + 1) - q0i * rows_(api, t + 1) for t in range(L)], axis=2)
    q_im = jnp.concatenate([-(q0r * rows_(api, t + 1) + q0i * rows_(apr, t + 1)) for t in range(L)], axis=2)
    lag_blocks = [jnp.zeros((nunit, uw, uw), F32)]
    for t in range(L):
        xr = p0r * lanes(apr, t) - p0i * lanes(api, t)
        xi = p0r * lanes(api, t) + p0i * lanes(apr, t)
        lag_blocks.append(jnp.einsum("pus,psv->puv", xr, q0r, precision=hp)
                          - jnp.einsum("pus,psv->puv", xi, q0i, precision=hp))
    brow = jnp.concatenate(lag_blocks, axis=2)
    lane_row = lambda x: x.reshape(nunit, 1, sw)
    return {
        "brow": brow.astype(BF16),
        "p_re": p_re.astype(BF16), "p_im": p_im.astype(BF16),
        "q_re": q_re.astype(BF16), "q_im": q_im.astype(BF16),
        "a_chunk_re": lane_row(apr[L]), "a_chunk_im": lane_row(api[L]),
        "a1_re": lane_row(apr[1]), "a1_im": lane_row(api[1]),
        "b1_re": p0r.astype(BF16), "b1_im": p0i.astype(BF16),
        "c1_re": q0r.astype(BF16), "c1_im": (-q0i).astype(BF16),
    }


def _compress_paged_kernel(pt_ref, *refs, pg, nsb):
    del pt_ref
    pages = refs[:pg]
    perm_ref, wbd_ref, pe_ref, w2_ref, out_ref, h0_ref, h1_ref = refs[pg:]
    j = pl.program_id(1)
    sbp = PAGE_SIZE // CMP_STRIDE
    nrows = pg * sbp
    pair_rows = 2 * sbp

    @pl.when(j == 0)
    def _():
        h1_ref[:, nsb:nsb + SUBLANES, :] = jnp.zeros((2, SUBLANES, KV_W), F32)

    r0 = pl.multiple_of(j * nrows, nrows)
    for kv in range(2):
        staged = []
        for q in range(pg // 2):
            z = jnp.concatenate([pages[2 * q][0, kv], pages[2 * q + 1][0, kv]], axis=1).astype(BF16)
            staged.append(_dot_nt(perm_ref[...], z).astype(BF16))

        def load_rows(s, kv, staged=staged):
            return jnp.concatenate([x[s * pair_rows:(s + 1) * pair_rows, :] for x in staged], axis=0)

        pr0, pr1 = _compress_half(load_rows, kv, nrows, wbd_ref, pe_ref)
        h0_ref[kv, pl.ds(r0, nrows), :] = pr0
        h1_ref[kv, pl.ds(r0, nrows), :] = pr1

    @pl.when(j == pl.num_programs(1) - 1)
    def _():
        for kv in range(2):
            h = h0_ref[kv] + h1_ref[kv, pl.ds(1, nsb), :]
            out = _dot(jax.nn.gelu(h).astype(BF16), w2_ref[kv])
            out_ref[0, :, kv * KV_W:(kv + 1) * KV_W] = out.astype(BF16)


def _page_specs(pg, pages_per_sample):
    def spec(i):
        return pl.BlockSpec((1, 2, KV_W, PAGE_SIZE),
                            lambda b, j, pt: (pt[b * pages_per_sample + j * pg + i], 0, 0, 0))
    return [spec(i) for i in range(pg)]


def _compress_paged(pages, pt_flat, wbd, pe_t, w2bd, *, nb, pages_per_sample, pg):
    nsb = pages_per_sample * PAGE_SIZE // CMP_STRIDE
    sbp = PAGE_SIZE // CMP_STRIDE
    i_out = np.arange(2 * PAGE_SIZE)
    s_i, pg_i, n_i = i_out // (2 * sbp), (i_out // sbp) % 2, i_out % sbp
    perm = jnp.asarray(i_out[None, :] == (pg_i * PAGE_SIZE + n_i * CMP_STRIDE + s_i)[:, None], BF16)
    c3 = lambda shape: pl.BlockSpec(shape, lambda b, j, pt: (0,) * len(shape))
    grid_spec = pltpu.PrefetchScalarGridSpec(
        num_scalar_prefetch=1,
        grid=(nb, pages_per_sample // pg),
        in_specs=_page_specs(pg, pages_per_sample) + [c3(perm.shape), c3(wbd.shape), c3(pe_t.shape), c3(w2bd.shape)],
        out_specs=pl.BlockSpec((1, nsb, 2 * KV_W), lambda b, j, pt: (b, 0, 0)),
        scratch_shapes=[pltpu.VMEM((2, nsb, KV_W), F32), pltpu.VMEM((2, nsb + SUBLANES, KV_W), F32)],
    )
    return pl.pallas_call(
        functools.partial(_compress_paged_kernel, pg=pg, nsb=nsb),
        grid_spec=grid_spec,
        out_shape=jax.ShapeDtypeStruct((nb, nsb, 2 * KV_W), BF16),
        compiler_params=_params("parallel", "arbitrary"),
        name="nsa_compress_paged",
    )(pt_flat, *([pages] * pg), perm, wbd, pe_t, w2bd)


def _topk_mask_lanes(imp, n_top, ns_valid):
    lane = lax.broadcasted_iota(jnp.int32, imp.shape, 1)
    cnt = jnp.zeros(imp.shape, F32)
    for sp in range(ns_valid):
        col = imp[:, sp:sp + 1]
        cnt = cnt + jnp.where(lane > sp, jnp.where(col >= imp, 1.0, 0.0), jnp.where(col > imp, 1.0, 0.0))
    return jnp.where((cnt < n_top) & (lane < ns_valid), 1.0, 0.0)


def _dot_nt(a, b):
    return lax.dot_general(a, b, (((1,), (1,)), ((), ())), preferred_element_type=F32)


def _attn_sample_kernel(pt_ref, *refs, pg, ns, past, nc, ns_valid, n_top):
    del pt_ref
    all_pages = [refs[u * pg:(u + 1) * pg] for u in range(ns)]
    (q_ref, qr_ref, g_ref, kcvc_ref, ksn_ref, kwn_ref, win_ref, ov_ref, e_ref, gs_ref, o_ref,
     m_ref, l_ref, acc_ref, sel_ref, oc_ref, ow_ref) = refs[ns * pg:]
    j = pl.program_id(1)
    ncp = kcvc_ref.shape[1]
    nsp = ov_ref.shape[1]
    wlen = win_ref.shape[3]
    row = lax.broadcasted_iota(jnp.int32, (N_HEADS, KV_W), 0)
    lane = lax.broadcasted_iota(jnp.int32, (N_HEADS, KV_W), 1)
    own = (lane // HEAD_DIM) == (row // GROUP)

    def spread(ref, u):
        q = ref[u]
        return jnp.where(own, jnp.concatenate([q] * N_KV_HEADS, axis=1), jnp.zeros((N_HEADS, KV_W), BF16))

    def update(state, s, vs):
        m, l, acc = state
        m_new = jnp.maximum(m, jnp.max(s, axis=1, keepdims=True))
        alpha = jnp.exp(m - m_new)
        p = jnp.exp(s - m_new)
        l = alpha * l + jnp.sum(p, axis=1, keepdims=True)
        pv = None
        for st, sz, v, feature_major in vs:
            pb = p[:, st:st + sz].astype(BF16)
            t = _dot_nt(pb, v) if feature_major else _dot(pb, v)
            pv = t if pv is None else pv + t
        return m_new, l, alpha * acc + pv

    def init():
        return (jnp.full((N_HEADS, 1), NEG, F32), jnp.zeros((N_HEADS, 1), F32), jnp.zeros((N_HEADS, KV_W), F32))

    def new_row_update(state, qbd, new_row):
        r8 = lax.broadcasted_iota(jnp.int32, (SUBLANES, 2 * KV_W), 0)
        tile = jnp.where(r8 == 0, jnp.broadcast_to(new_row, (SUBLANES, 2 * KV_W)), 0.0).astype(BF16)
        s = _dot_nt(qbd, tile[:, 0:KV_W])
        l8 = lax.broadcasted_iota(jnp.int32, (N_HEADS, SUBLANES), 1)
        s = jnp.where(l8 == 0, s, MASKED)
        return update(state, s, [(0, SUBLANES, tile[:, KV_W:2 * KV_W], False)])

    def put(u, state):
        m, l, acc = state
        m_ref[u] = jnp.broadcast_to(m, m_ref.shape[1:])
        l_ref[u] = jnp.broadcast_to(l, l_ref.shape[1:])
        acc_ref[u] = acc

    qrbd = [spread(qr_ref, u) for u in range(ns)]

    @pl.when(j == 0)
    def _():
        cidx = lax.broadcasted_iota(jnp.int32, (1, ncp), 1)
        valid = ((cidx * CMP_STRIDE + (CMP_BLOCK - 1)) <= past) & (cidx < nc)
        sidx = lax.broadcasted_iota(jnp.int32, (1, nsp), 1)
        cur = past // SEL_BLOCK
        forced = (sidx == 0) | (sidx == cur) | (sidx == cur - 1)
        causal = (sidx * SEL_BLOCK) <= past
        wpos = past - wlen + lax.broadcasted_iota(jnp.int32, (1, wlen), 1)
        in_window = (wpos >= 0) & (past - wpos <= WINDOW)
        for u in range(ns):
            s = _dot_nt(spread(q_ref, u), kcvc_ref[u, :, 0:KV_W])
            sm = jnp.where(valid, s, NEG)
            mx = jnp.max(sm, axis=1, keepdims=True)
            e = jnp.where(valid, jnp.exp(sm - mx), 0.0)
            p = e / jnp.maximum(jnp.sum(e, axis=1, keepdims=True), 1e-20)
            oc_ref[u] = _dot(p.astype(BF16), kcvc_ref[u, :, KV_W:2 * KV_W])
            imp = _dot_f32rhs(_dot_f32lhs(gs_ref[...], p), ov_ref[...])
            imp = jnp.where(forced, FORCE, jnp.where(causal, imp, NEG))
            imp = jnp.where(sidx < ns_valid, imp, MASKED)
            sel_ref[u] = _topk_mask_lanes(imp, n_top, ns_valid)
            sw = jnp.where(in_window, _dot(qrbd[u], win_ref[u, 0].astype(BF16)), MASKED)
            st = update(init(), sw, [(0, wlen, win_ref[u, 1].astype(BF16), True)])
            st = new_row_update(st, qrbd[u], kwn_ref[u])
            ow_ref[u] = st[2] * (1.0 / jnp.maximum(st[1], 1e-20))
            put(u, new_row_update(init(), qrbd[u], ksn_ref[u]))

    def halves(pages):
        return [pages[:pg // 2], pages[pg // 2:]] if pg > 1 else [pages]

    scores = [[jnp.concatenate([_dot(qrbd[u], r[0, 0].astype(BF16)) for r in part], axis=1)
               for part in halves(all_pages[u])] for u in range(ns)]
    masks = [_dot(sel_ref[u].astype(BF16), e_ref[0]) > 0.5 for u in range(ns)]
    states = [(m_ref[u, :, 0:1], l_ref[u, :, 0:1], acc_ref[u]) for u in range(ns)]
    k0 = 0
    for hx, part0 in enumerate(halves(all_pages[0])):
        width = len(part0) * PAGE_SIZE
        for u in range(ns):
            part = halves(all_pages[u])[hx]
            s = jnp.where(masks[u][:, k0:k0 + width], scores[u][hx], MASKED)
            vs = [(i * PAGE_SIZE, PAGE_SIZE, r[0, 1].astype(BF16), True) for i, r in enumerate(part)]
            states[u] = update(states[u], s, vs)
        k0 += width
    for u in range(ns):
        put(u, states[u])

    @pl.when(j == pl.num_programs(1) - 1)
    def _():
        for u in range(ns):
            g = g_ref[u]
            os_ = acc_ref[u] * (1.0 / jnp.maximum(l_ref[u, :, 0:1], 1e-20))
            o = g[:, 0:1] * oc_ref[u] + g[:, 1:2] * os_ + g[:, 2:3] * ow_ref[u]
            o = jnp.where(own, o, 0.0)
            out = o[:, 0:HEAD_DIM]
            for h in range(1, N_KV_HEADS):
                out = out + o[:, h * HEAD_DIM:(h + 1) * HEAD_DIM]
            o_ref[u] = out.astype(BF16)


def _attn_sample(pages, pt_flat, q3, qr3, g3, kcvc, ks_new, kw_new, win, ov, e_mat, gsum, *, nb, pages_per_sample,
                 pg, past, nc, ns_valid, n_top):
    nsp = ov.shape[1]
    ns = SAMPLES_PER_STEP if nb % SAMPLES_PER_STEP == 0 else 1
    per_b = lambda shape: pl.BlockSpec((ns,) + shape, lambda b, j, pt: (b,) + (0,) * len(shape))
    const = lambda shape: pl.BlockSpec(shape, lambda b, j, pt: (0,) * len(shape))

    def page_spec(u, i):
        return pl.BlockSpec((1, 2, KV_W, PAGE_SIZE),
                            lambda b, j, pt: (pt[(b * ns + u) * pages_per_sample + j * pg + i], 0, 0, 0))

    per_sample = lambda *shape: pltpu.VMEM((ns,) + shape, F32)
    grid_spec = pltpu.PrefetchScalarGridSpec(
        num_scalar_prefetch=1,
        grid=(nb // ns, pages_per_sample // pg),
        in_specs=[page_spec(u, i) for u in range(ns) for i in range(pg)] + [
            per_b((N_HEADS, HEAD_DIM)), per_b((N_HEADS, HEAD_DIM)), per_b((N_HEADS, 3)),
            per_b(kcvc.shape[1:]), per_b((1, 2 * KV_W)), per_b((1, 2 * KV_W)), per_b(win.shape[1:]),
            const(ov.shape), pl.BlockSpec((1,) + e_mat.shape[1:], lambda b, j, pt: (j, 0, 0)), const(gsum.shape)],
        out_specs=per_b((N_HEADS, HEAD_DIM)),
        scratch_shapes=[per_sample(N_HEADS, LANES), per_sample(N_HEADS, LANES), per_sample(N_HEADS, KV_W),
                        per_sample(N_HEADS, nsp), per_sample(N_HEADS, KV_W), per_sample(N_HEADS, KV_W)],
    )
    return pl.pallas_call(
        functools.partial(_attn_sample_kernel, pg=pg, ns=ns, past=past, nc=nc, ns_valid=ns_valid, n_top=n_top),
        grid_spec=grid_spec,
        out_shape=jax.ShapeDtypeStruct((nb, N_HEADS, HEAD_DIM), BF16),
        compiler_params=_params("parallel", "arbitrary"),
        name="nsa_attn_sample",
    )(pt_flat, *([pages] * (ns * pg)), q3, qr3, g3, kcvc, ks_new, kw_new, win, ov, e_mat, gsum)


def _rope_tables(pos):
    half = HEAD_DIM // 2
    inv = ROPE_THETA ** (-jnp.arange(half, dtype=F32) / half)
    ang = pos.astype(F32)[:, None] * inv[None, :]
    cos = jnp.cos(ang)
    sin = jnp.sin(ang)
    reps = LANES // HEAD_DIM
    return (jnp.tile(jnp.concatenate([cos, cos], axis=1), (1, reps)),
            jnp.tile(jnp.concatenate([-sin, sin], axis=1), (1, reps)))


def _compress_weights(cmp_w1, cmp_w2, cmp_pe):
    ratio = CMP_BLOCK // CMP_STRIDE
    w1r = cmp_w1.reshape(2, ratio, CMP_STRIDE, HEAD_DIM, HEAD_DIM)
    hh = np.arange(KV_W) // HEAD_DIM
    same_head = jnp.asarray(hh[:, None] == hh[None, :])

    def block_diag(w):
        tiled = jnp.concatenate([jnp.concatenate([w] * N_KV_HEADS, axis=-1)] * N_KV_HEADS, axis=-2)
        return jnp.where(same_head, tiled, 0.0).astype(BF16)

    wbd = block_diag(w1r.reshape(2 * ratio * CMP_STRIDE, HEAD_DIM, HEAD_DIM))
    w2bd = block_diag(cmp_w2)
    pe_r = cmp_pe.reshape(2, ratio, CMP_STRIDE, HEAD_DIM).astype(F32)
    pe_w = jnp.einsum("krsd,krsde->kre", pe_r, w1r.astype(F32), precision=lax.Precision.HIGHEST)
    pe_t = jnp.tile(pe_w.reshape(2 * ratio, HEAD_DIM), (1, N_KV_HEADS))
    return wbd, pe_t, w2bd


def _overlap(nc, ncp, nsel, nsp):
    c_start = np.arange(ncp)[:, None] * CMP_STRIDE
    s_start = np.arange(nsp)[None, :] * SEL_BLOCK
    ov = (c_start < s_start + SEL_BLOCK) & (c_start + CMP_BLOCK > s_start)
    ov = ov & (np.arange(ncp)[:, None] < nc) & (np.arange(nsp)[None, :] < nsel)
    return jnp.asarray(ov, BF16)


def _tile(n, pref):
    t = min(n, pref)
    while n % t:
        t //= 2
    return t


def kernel(x_prompt, x_sample, cache_kv_cmp, cache_kv_sel, cache_kv_win, state_ssm, page_table, norm_g, mlp_w1,
           mlp_w2, nsa_w_in, nsa_w_o, nsa_cmp_w1, nsa_cmp_w2, nsa_cmp_pe, s5_a_re, s5_a_im, s5_log_dt, s5_b_re,
           s5_b_im, s5_c_re, s5_c_im, s5_d, s5_w_glu, s5_b_glu):
    b, t, d = x_prompt.shape
    nb = x_sample.shape[0]
    pages_per_sample = page_table.shape[1]
    past = pages_per_sample * PAGE_SIZE
    rows_p = b * t
    g = norm_g.reshape(norm_g.shape[0], 4, 1, d)

    w_in = nsa_w_in[0]
    w_q = w_in[:, :Q_W].astype(BF16)
    w_kv = w_in[:, Q_W:Q_W + 6 * KV_W].astype(BF16)
    w_gate = jnp.pad(w_in[:, Q_W + 6 * KV_W:], ((0, 0), (0, LANES - 3 * N_HEADS))).astype(BF16)
    w_o = nsa_w_o[0].astype(BF16)
    wbd, pe_t, w2bd = _compress_weights(nsa_cmp_w1[0], nsa_cmp_w2[0], nsa_cmp_pe[0])
    w1 = mlp_w1.astype(BF16)
    w2 = mlp_w2.astype(BF16)
    w_glu = s5_w_glu[0].astype(BF16)
    ops = _s5_operators(s5_a_re[0], s5_a_im[0], s5_log_dt[0], s5_b_re[0], s5_b_im[0], s5_c_re[0], s5_c_im[0])
    d_skip = s5_d[0].reshape(1, d)
    b_glu = s5_b_glu[0].reshape(1, d)

    tm = _tile(rows_p, 512)
    ff_chunk = _tile(mlp_w1.shape[2], 1024)

    xp = x_prompt.reshape(rows_p, d)
    cos_p, sin_p = _rope_tables(jnp.arange(t, dtype=jnp.int32))
    tq = next(c for c in (2 * LANES, LANES) if t % c == 0 and WINDOW % c == 0)
    (qT, qrT, gT, kvc, kvcT, kvsT, kvwT, ksb, kwb, vsT, vwT) = _inproj(
        xp, g[0, 0], w_q.T, w_kv, w_gate, cos_p, sin_p, tm=_tile(t, 512), pos_blocks=t // _tile(t, 512), transposed=True,
        key_chunk=tq)
    nsb_p = t // CMP_STRIDE
    nc_p = nsb_p - CMP_BLOCK // CMP_STRIDE + 1
    nsel_p = t // SEL_BLOCK
    kc, vcT = _compress_prompt(kvc.reshape(b, t, 2 * KV_W), wbd, pe_t, w2bd)
    ovT = _overlap(nc_p, nsb_p, nsel_p, nsel_p).T
    oT = _attn_prompt(qT, qrT, gT, kc, vcT, ksb, vsT, kwb, vwT, ovT, batch=b, seq=t, tq=tq, nc=nc_p,
                      n_top=min(TOP_N, nsel_p))
    hp, xn1 = _layer_tail((oT, w_o), xp, g[0, 1], g[0, 2], w1[0], w2[0], g[0, 3], g[1, 0], mixer="nsa", tm=tm,
                          ff_chunk=ff_chunk, transposed=True)

    pw = SSM_UNIT * SSM_GROUP
    npair = d // pw
    y3, hfin = _s5_seq(xn1.reshape(b, t, d), ops)
    yc = y3.reshape(rows_p, d)
    (hp,) = _layer_tail((yc, xn1, d_skip, w_glu, b_glu), hp, g[1, 1], g[1, 2], w1[1], w2[1], g[1, 3], None,
                        mixer="s5", tm=tm, ff_chunk=ff_chunk)
    ssm_p = hfin.reshape(npair, 2, b, SSM_UNIT, SSM_STATE).transpose(2, 1, 0, 3, 4)
    ssm_p = ssm_p.reshape(b, 2, d // SSM_GROUP, SSM_STATE)

    xs = x_sample.reshape(nb, d)
    cos_s, sin_s = _rope_tables(jnp.full((nb,), past, dtype=jnp.int32))
    q_s, qr_s, gates_s, kvc_s, kvs_s, kvw_s = _inproj(
        xs, g[0, 0], w_q, w_kv, w_gate, cos_s, sin_s, tm=nb, pos_blocks=1, transposed=False)
    pt_flat = page_table.reshape(-1).astype(jnp.int32)
    pg = _tile(pages_per_sample, PAGE_GROUP)
    n_pool = cache_kv_cmp.shape[1]
    feature_major = lambda c, n, s: c.transpose(0, 2, 3, 4, 1).reshape(n, 2, KV_W, s)
    cmp_pages = feature_major(cache_kv_cmp[0], n_pool, PAGE_SIZE)
    sel_pages = feature_major(cache_kv_sel[0], n_pool, PAGE_SIZE)
    kcvc = _compress_paged(cmp_pages, pt_flat, wbd, pe_t, w2bd, nb=nb, pages_per_sample=pages_per_sample, pg=pg)
    l_all = past + 1
    nsb_s = l_all // CMP_STRIDE
    nc_s = nsb_s - CMP_BLOCK // CMP_STRIDE + 1
    nsel_s = -(-l_all // SEL_BLOCK)
    nsp = -(-nsel_s // LANES) * LANES
    ov_s = _overlap(nc_s, past // CMP_STRIDE, nsel_s, nsp)
    keys_per_step = pg * PAGE_SIZE
    key_blk = (np.arange(past) // SEL_BLOCK).reshape(past // keys_per_step, 1, keys_per_step)
    e_mat = jnp.asarray(np.arange(nsp)[None, :, None] == key_blk, BF16)
    hh = np.arange(N_HEADS)
    gsum = jnp.asarray((hh[:, None] // GROUP) == (hh[None, :] // GROUP), BF16)
    win = feature_major(cache_kv_win[0], nb, WINDOW)
    o_s = _attn_sample(sel_pages, pt_flat, q_s.reshape(nb, N_HEADS, HEAD_DIM), qr_s.reshape(nb, N_HEADS, HEAD_DIM),
                       gates_s[:, :3 * N_HEADS].reshape(nb, N_HEADS, 3), kcvc, kvs_s.reshape(nb, 1, 2 * KV_W),
                       kvw_s.reshape(nb, 1, 2 * KV_W), win, ov_s, e_mat, gsum, nb=nb,
                       pages_per_sample=pages_per_sample, pg=pg, past=past, nc=nc_s, ns_valid=nsel_s,
                       n_top=min(TOP_N, nsel_s))
    hs, xn1_s = _layer_tail((o_s.reshape(nb, Q_W), w_o), xs, g[0, 1], g[0, 2], w1[0], w2[0], g[0, 3], g[1, 0],
                            mixer="nsa", tm=nb, ff_chunk=ff_chunk)

    u2_s = xn1_s.astype(BF16).reshape(nb, npair, pw).transpose(1, 0, 2)
    st = state_ssm[0].reshape(nb, 2, npair, SSM_UNIT * SSM_STATE).transpose(1, 2, 0, 3)
    y2_s, hr_s, hi_s = _s5_step(u2_s, st[0], st[1], ops)
    yc_s = y2_s.transpose(1, 0, 2).reshape(nb, d)
    (hs,) = _layer_tail((yc_s, xn1_s, d_skip, w_glu, b_glu), hs, g[1, 1], g[1, 2], w1[1], w2[1], g[1, 3], None,
                        mixer="s5", tm=nb, ff_chunk=ff_chunk)
    ssm_s = jnp.stack([hr_s, hi_s], axis=0).transpose(2, 0, 1, 3).reshape(nb, 2, d // SSM_GROUP, SSM_STATE)

    kv5 = lambda a, n, s: a.reshape(1, n, s, 2, N_KV_HEADS, HEAD_DIM)
    from_fm = lambda a, n, s: a.reshape(n, 2, N_KV_HEADS, HEAD_DIM, s).transpose(0, 4, 1, 2, 3)[None]
    win_s = jnp.concatenate([win[..., 1:], kvw_s.reshape(nb, 2, KV_W, 1)], axis=-1)
    return (hp.reshape(b, t, d), hs.reshape(nb, 1, d),
            from_fm(kvcT, b, t), kv5(kvc_s, nb, 1), from_fm(kvsT, b, t), kv5(kvs_s, nb, 1),
            from_fm(kvwT[:, :, t - WINDOW:], b, WINDOW), from_fm(win_s, nb, WINDOW), ssm_p[None], ssm_s[None])
```

```python
import functools

import jax
import jax.numpy as jnp
import numpy as np
from jax import lax
from jax.experimental import pallas as pl
from jax.experimental.pallas import tpu as pltpu

N_HEADS = 16
HEAD_DIM = 64
N_KV_HEADS = 4
GROUP = N_HEADS // N_KV_HEADS
CMP_BLOCK = 32
CMP_STRIDE = 16
SEL_BLOCK = 64
TOP_N = 16
WINDOW = 512
ROPE_THETA = 10000.0
PAGE_SIZE = 128
SSM_GROUP = 16
SSM_STATE = 64
SSM_CHUNK = 8
SSM_UNIT = 8
EPS = 1e-6
NEG = -1e30
FORCE = 1e9
MASKED = -1.5e38
LOG2E = 1.4426950408889634
V_ROWS = HEAD_DIM + 16
Q_W = N_HEADS * HEAD_DIM
KV_W = N_KV_HEADS * HEAD_DIM
LANES = 128
SUBLANES = 8
PAGE_GROUP = 16
SAMPLES_PER_STEP = 2
S5_TIME_BLOCK = 2048
VMEM_LIMIT = 56 * 1024 * 1024

F32 = jnp.float32
BF16 = jnp.bfloat16


def _params(*sem):
    return pltpu.CompilerParams(dimension_semantics=sem, vmem_limit_bytes=VMEM_LIMIT)


def _full(shape):
    zeros = (0,) * len(shape)
    return pl.BlockSpec(shape, lambda *_: zeros)


def _rms(x, g):
    ms = jnp.mean(x * x, axis=-1, keepdims=True)
    return x * lax.rsqrt(ms + EPS) * g


def _dot(a, b):
    return jnp.dot(a, b, preferred_element_type=F32)


def _dot_f32lhs(w, x):
    hi = x.astype(BF16)
    r1 = x - hi.astype(F32)
    mid = r1.astype(BF16)
    lo = (r1 - mid.astype(F32)).astype(BF16)
    return _dot(w, hi) + _dot(w, mid) + _dot(w, lo)


def _dot_f32rhs(x, w):
    hi = x.astype(BF16)
    r1 = x - hi.astype(F32)
    mid = r1.astype(BF16)
    lo = (r1 - mid.astype(F32)).astype(BF16)
    return _dot(hi, w) + _dot(mid, w) + _dot(lo, w)


def _rope_nat(x, cos, sin):
    half = HEAD_DIM // 2
    lane = lax.broadcasted_iota(jnp.int32, (1, LANES), 1)
    first = (lane % HEAD_DIM) < half
    outs = []
    for c in range(x.shape[1] // LANES):
        xc = x[:, c * LANES:(c + 1) * LANES]
        rot = jnp.where(first, pltpu.roll(xc, LANES - half, 1), pltpu.roll(xc, half, 1))
        outs.append(xc * cos + rot * sin)
    return jnp.concatenate(outs, axis=1)


def _inproj_kernel(x_ref, g_ref, wq_ref, wkv_ref, wg_ref, cos_ref, sin_ref, *rest, transposed, key_chunk):
    outs = rest[2:] if transposed else rest
    xb = _rms(x_ref[...], g_ref[...]).astype(BF16)
    cos = cos_ref[...]
    sin = sin_ref[...]
    scale = HEAD_DIM ** -0.5
    kv = _dot(xb, wkv_ref[...])
    gates = jax.nn.sigmoid(_dot(xb, wg_ref[...]))
    k_s = _rope_nat(kv[:, 2 * KV_W:3 * KV_W], cos, sin)
    v_s = kv[:, 3 * KV_W:4 * KV_W]
    k_w = _rope_nat(kv[:, 4 * KV_W:5 * KV_W], cos, sin)
    v_w = kv[:, 5 * KV_W:6 * KV_W]
    kvc_ref = outs[3]
    kvc_ref[...] = kv[:, 0:2 * KV_W]
    if transposed:
        qT_ref, qrT_ref, gT_ref, _, kvcT_ref, kvsT_ref, kvwT_ref, ksb_ref, kwb_ref, vsT_ref, vwT_ref = outs
        qT = _dot_nt(wq_ref[...], xb)
        cosT = rest[0][...]
        sinT = rest[1][...]
        half = HEAD_DIM // 2
        rotated = []
        for h in range(N_HEADS):
            x1 = qT[h * HEAD_DIM:h * HEAD_DIM + half, :]
            x2 = qT[h * HEAD_DIM + half:(h + 1) * HEAD_DIM, :]
            rotated += [x1 * cosT - x2 * sinT, x2 * cosT + x1 * sinT]
        qT_ref[...] = (qT * (scale * LOG2E)).astype(BF16)
        qrT_ref[...] = (jnp.concatenate(rotated, axis=0) * (scale * LOG2E)).astype(BF16)
        gT_ref[...] = gates.T
        tm = x_ref.shape[0]
        rowi = lax.broadcasted_iota(jnp.int32, (tm, LANES), 0)
        lanei = lax.broadcasted_iota(jnp.int32, (tm, LANES), 1)
        blk = lax.shift_right_logical(rowi & (key_chunk - 1), SEL_BLOCK.bit_length() - 1)
        extra = lanei - HEAD_DIM
        nblk = key_chunk // SEL_BLOCK
        aug = jnp.where((extra == blk) | ((extra >= nblk) & (extra < nblk + N_CONST_LANES)), 1.0, 0.0)
        for c in range(KV_W // LANES):
            for k_nat, k_ref in ((k_s, ksb_ref), (k_w, kwb_ref)):
                pair = k_nat[:, c * LANES:(c + 1) * LANES]
                k_ref[2 * c] = jnp.where(lanei < HEAD_DIM, pair, aug).astype(BF16)
                k_ref[2 * c + 1] = jnp.where(lanei < HEAD_DIM, pltpu.roll(pair, HEAD_DIM, 1), aug).astype(BF16)
        v_sT = v_s.T
        v_wT = v_w.T
        ones_rows = jnp.where(lax.broadcasted_iota(jnp.int32, (V_ROWS - HEAD_DIM, tm), 0) == 0, 1.0, 0.0)
        for h in range(N_KV_HEADS):
            for vT, v_ref in ((v_sT, vsT_ref), (v_wT, vwT_ref)):
                v_ref[h * V_ROWS:(h + 1) * V_ROWS, :] = jnp.concatenate(
                    [vT[h * HEAD_DIM:(h + 1) * HEAD_DIM, :], ones_rows], axis=0).astype(BF16)
        kvcT_ref[0] = kv[:, 0:2 * KV_W].T
        kvsT_ref[0, 0:KV_W, :] = k_s.T
        kvsT_ref[0, KV_W:2 * KV_W, :] = v_sT
        kvwT_ref[0, 0:KV_W, :] = k_w.T
        kvwT_ref[0, KV_W:2 * KV_W, :] = v_wT
    else:
        q_ref, qr_ref, gt_ref, _, kvs_ref, kvw_ref = outs
        q = _dot(xb, wq_ref[...])
        qr = _rope_nat(q, cos, sin)
        q_ref[...] = (q * scale).astype(BF16)
        qr_ref[...] = (qr * scale).astype(BF16)
        gt_ref[...] = gates
        kvs_ref[:, 0:KV_W] = k_s
        kvs_ref[:, KV_W:2 * KV_W] = v_s
        kvw_ref[:, 0:KV_W] = k_w
        kvw_ref[:, KV_W:2 * KV_W] = v_w


def _inproj(x, g, w_q, w_kv, w_gate, cos_t, sin_t, *, tm, pos_blocks, transposed, key_chunk=LANES):
    rows, d = x.shape
    assert tm % key_chunk == 0 or not transposed
    n = rows // tm
    row_blk = lambda w: pl.BlockSpec((tm, w), lambda i: (i, 0))
    col_blk = lambda h: pl.BlockSpec((h, tm), lambda i: (0, i))
    tab = pl.BlockSpec((tm, LANES), lambda i: (i % pos_blocks, 0))
    kv_nat = jax.ShapeDtypeStruct((rows, 2 * KV_W), F32)
    if transposed:
        seqs = rows // (pos_blocks * tm)
        kh = pl.BlockSpec((N_KV_HEADS, tm, LANES), lambda i: (0, i, 0))
        kvT = pl.BlockSpec((1, 2 * KV_W, tm), lambda i: (i // pos_blocks, 0, i % pos_blocks))
        out_shape = ([jax.ShapeDtypeStruct((Q_W, rows), BF16)] * 2 + [jax.ShapeDtypeStruct((LANES, rows), F32)]
                     + [kv_nat] + [jax.ShapeDtypeStruct((seqs, 2 * KV_W, pos_blocks * tm), F32)] * 3
                     + [jax.ShapeDtypeStruct((N_KV_HEADS, rows, LANES), BF16)] * 2
                     + [jax.ShapeDtypeStruct((N_KV_HEADS * V_ROWS, rows), BF16)] * 2)
        out_specs = ([col_blk(Q_W)] * 2 + [col_blk(LANES)] + [row_blk(2 * KV_W)] + [kvT] * 3 + [kh] * 2
                     + [col_blk(N_KV_HEADS * V_ROWS)] * 2)
    else:
        out_shape = ([jax.ShapeDtypeStruct((rows, Q_W), BF16)] * 2 + [jax.ShapeDtypeStruct((rows, LANES), F32)]
                     + [kv_nat] * 3)
        out_specs = [row_blk(Q_W)] * 2 + [row_blk(LANES)] + [row_blk(2 * KV_W)] * 3
    tables = [cos_t, sin_t]
    table_specs = [tab, tab]
    if transposed:
        half = HEAD_DIM // 2
        tables += [cos_t[:, 0:half].T, sin_t[:, half:HEAD_DIM].T]
        table_specs += [pl.BlockSpec((half, tm), lambda i: (0, i % pos_blocks))] * 2
    return pl.pallas_call(
        functools.partial(_inproj_kernel, transposed=transposed, key_chunk=key_chunk),
        grid=(n,),
        in_specs=[row_blk(d), _full((1, d)), _full(w_q.shape), _full(w_kv.shape), _full(w_gate.shape)] + table_specs,
        out_specs=out_specs,
        out_shape=out_shape,
        compiler_params=_params("parallel"),
        name="nsa_inproj",
    )(x, g, w_q, w_kv, w_gate, *tables)


_KV_CHUNKS = 2 * KV_W // LANES


def _compress_half(load_rows, kv, nrows, wbd_ref, pe_ref):
    ratio = CMP_BLOCK // CMP_STRIDE
    accs = [jnp.broadcast_to(pe_ref[kv * ratio + r:kv * ratio + r + 1, :], (nrows, KV_W)) for r in range(ratio)]
    for s in range(CMP_STRIDE):
        lhs = load_rows(s, kv).astype(BF16)
        for r in range(ratio):
            accs[r] = accs[r] + _dot(lhs, wbd_ref[(kv * ratio + r) * CMP_STRIDE + s])
    return accs


def _compress_prompt_kernel(*refs, nsb):
    x_refs = refs[:_KV_CHUNKS]
    wbd_ref, pe_ref, w2_ref, kc_ref, vcT_ref, sh_ref = refs[_KV_CHUNKS:]

    def load_rows(s, kv):
        per_half = _KV_CHUNKS // 2
        return jnp.concatenate([x_refs[kv * per_half + c][0, pl.ds(s, nsb, stride=CMP_STRIDE), :]
                                for c in range(per_half)], axis=1)

    sh_ref[nsb:nsb + SUBLANES, :] = jnp.zeros((SUBLANES, KV_W), F32)
    for kv in range(2):
        pr0, pr1 = _compress_half(load_rows, kv, nsb, wbd_ref, pe_ref)
        sh_ref[0:nsb, :] = pr1
        h = pr0 + sh_ref[pl.ds(1, nsb), :]
        out = _dot(jax.nn.gelu(h).astype(BF16), w2_ref[kv])
        if kv == 0:
            for hh in range(N_KV_HEADS):
                kc_ref[0, hh] = out[:, hh * HEAD_DIM:(hh + 1) * HEAD_DIM].astype(BF16)
        else:
            vcT_ref[0] = out.T.astype(BF16)


def _compress_prompt(kvc3, wbd, pe_t, w2bd):
    b, t, _ = kvc3.shape
    nsb = t // CMP_STRIDE
    return pl.pallas_call(
        functools.partial(_compress_prompt_kernel, nsb=nsb),
        grid=(b,),
        in_specs=[pl.BlockSpec((1, t, LANES), lambda i, c=c: (i, 0, c)) for c in range(_KV_CHUNKS)]
        + [_full(wbd.shape), _full(pe_t.shape), _full(w2bd.shape)],
        out_specs=[pl.BlockSpec((1, N_KV_HEADS, nsb, HEAD_DIM), lambda i: (i, 0, 0, 0)),
                   pl.BlockSpec((1, KV_W, nsb), lambda i: (i, 0, 0))],
        out_shape=[jax.ShapeDtypeStruct((b, N_KV_HEADS, nsb, HEAD_DIM), BF16),
                   jax.ShapeDtypeStruct((b, KV_W, nsb), BF16)],
        scratch_shapes=[pltpu.VMEM((nsb + SUBLANES, KV_W), F32)],
        compiler_params=_params("parallel"),
        name="nsa_compress_prompt",
    )(*([kvc3] * _KV_CHUNKS), wbd, pe_t, w2bd)


def _topk_mask_T(imp, n_top):
    ns, w = imp.shape
    nblk = ns // SUBLANES
    blocks = [imp[r * SUBLANES:(r + 1) * SUBLANES, :] for r in range(nblk)]
    cnts = [jnp.zeros((SUBLANES, w), F32) for _ in range(nblk)]
    sub = lax.broadcasted_iota(jnp.int32, (SUBLANES, w), 0)
    for sp in range(ns):
        row = blocks[sp // SUBLANES][sp % SUBLANES:sp % SUBLANES + 1, :]
        for r in range(nblk):
            blk = blocks[r]
            if sp < r * SUBLANES:
                beats = jnp.where(row >= blk, 1.0, 0.0)
            elif sp >= (r + 1) * SUBLANES:
                beats = jnp.where(row > blk, 1.0, 0.0)
            else:
                beats = jnp.where(sub > (sp - r * SUBLANES), jnp.where(row >= blk, 1.0, 0.0),
                                  jnp.where(row > blk, 1.0, 0.0))
            cnts[r] = cnts[r] + beats
    return jnp.concatenate([jnp.where(c < n_top, 1.0, 0.0) for c in cnts], axis=0)


def _online_chunks(states, k_cs, vT_cs, qTs, bias):
    scores = [_dot(k_c, qT) for k_c, qT in zip(k_cs, qTs)]
    mids = []
    for (m, _), s in zip(states, scores):
        if bias is not None:
            s = s + bias
        m_new = jnp.maximum(m, jnp.max(s, axis=0, keepdims=True))
        mids.append((m_new, jnp.exp2(m - m_new), jnp.exp2(s - m_new).astype(BF16)))
    return tuple((m_new, alpha * acc + _dot(vT_c, p))
                 for (m_new, alpha, p), (_, acc), vT_c in zip(mids, states, vT_cs))


def _softmax_finish(carry):
    _, acc = carry
    return acc[0:HEAD_DIM, :] * (1.0 / jnp.maximum(acc[HEAD_DIM:HEAD_DIM + 1, :], 1e-20))


LAZY_LOG2_MAX = 60.0
N_CONST_LANES = 4


def _split3(x):
    hi = x.astype(BF16).astype(F32)
    r = x - hi
    mid = r.astype(BF16).astype(F32)
    return hi, mid, (r - mid).astype(BF16).astype(F32)


def _rescaling_chunks(m_ref, acc_ref, k_cs, vT_cs, qTs, base_tiles, bias):
    nh = len(k_cs)
    rows, w = base_tiles[0].shape
    zero_rows = jnp.zeros((LANES - HEAD_DIM - rows, w), BF16)
    new = _online_chunks(tuple((m_ref[h], acc_ref[h]) for h in range(nh)), k_cs, vT_cs,
                         [jnp.concatenate([qTs[h], base_tiles[h].astype(BF16), zero_rows], axis=0) for h in range(nh)],
                         bias)
    for h in range(nh):
        m_ref[h] = new[h][0]
        acc_ref[h] = new[h][1]


def _lazy_chunks(m_ref, acc2_ref, slot_ref, k_cs, vT_cs, qTs, base_tiles, bias, ref_row, n_chunks=1, first=False):
    slot = slot_ref[0]
    acc_ref = acc2_ref.at[slot]
    nu = len(k_cs)
    nh = nu // n_chunks
    biases = list(bias) if isinstance(bias, (list, tuple)) else [bias] * n_chunks
    rows = base_tiles[0].shape[0]
    w = base_tiles[0].shape[1]
    rowt = lax.broadcasted_iota(jnp.int32, (rows, w), 0)
    zero_rows = jnp.zeros((LANES - HEAD_DIM - rows, w), BF16)

    def queries(u):
        hi, mid, lo = _split3(-m_ref[u % nh])
        tile = jnp.where(rowt == ref_row, hi, jnp.where(rowt == ref_row + 1, mid,
                                                        jnp.where(rowt == ref_row + 2, lo, base_tiles[u])))
        return jnp.concatenate([qTs[u % nh], tile.astype(BF16), zero_rows], axis=0)

    scores = [_dot(k_cs[u], queries(u)) for u in range(nu)]
    peaks = [None] * nh
    probs = []
    for u, s in enumerate(scores):
        if biases[u // nh] is not None:
            s = s + biases[u // nh]
        top = jnp.max(s, axis=0, keepdims=True)
        peaks[u % nh] = top if peaks[u % nh] is None else jnp.maximum(peaks[u % nh], top)
        probs.append(jnp.exp2(s).astype(BF16))
    pvs = [_dot(vT_cs[u], probs[u]) for u in range(nu)]
    for h in range(nh):
        total = pvs[h]
        for c in range(1, n_chunks):
            total = total + pvs[c * nh + h]
        acc2_ref[1 - slot, h] = acc_ref[h] + total
    highest = functools.reduce(jnp.maximum, peaks)
    lowest = functools.reduce(jnp.minimum, peaks)
    in_range = (jnp.max(highest) <= LAZY_LOG2_MAX) & (jnp.logical_not(first) | (jnp.min(lowest) >= -LAZY_LOG2_MAX))

    @pl.when(in_range)
    def _():
        slot_ref[0] = 1 - slot

    @pl.when(jnp.logical_not(in_range))
    def _():
        for h in range(nh):
            m_ref[h] = jnp.where(first, NEG, m_ref[h])
        for c in range(n_chunks):
            part = slice(c * nh, (c + 1) * nh)
            _rescaling_chunks(m_ref, acc_ref, k_cs[part], vT_cs[part], qTs, base_tiles[part], biases[c])


def _attn_prompt_kernel(qT_ref, qrT_ref, gT_ref, kc_ref, vcT_ref, ks_ref, vsT_ref, kw_ref, vwT_ref, ovT_ref,
                        o_ref, selb_ref, oc_ref, ow_ref, m_ref, acc2_ref, slot_ref, *, tq, nc, n_top):
    ck = tq
    i = pl.program_id(1)
    t0 = i * tq
    qpos = t0 + lax.broadcasted_iota(jnp.int32, (1, tq), 1)
    ncp = kc_ref.shape[2]
    ns = ovT_ref.shape[0]
    w = GROUP * tq
    bpc = ck // SEL_BLOCK
    sel_shift = SEL_BLOCK.bit_length() - 1
    kvhs = range(N_KV_HEADS)
    heads = [[kvh * GROUP + g for g in range(GROUP)] for kvh in kvhs]
    rows = [pl.ds(kvh * HEAD_DIM, HEAD_DIM) for kvh in kvhs]
    vrows = [pl.ds(kvh * V_ROWS, V_ROWS) for kvh in kvhs]
    qrT = [jnp.concatenate([qrT_ref[h * HEAD_DIM:(h + 1) * HEAD_DIM, :] for h in heads[kvh]], axis=1)
           for kvh in kvhs]
    kl = lax.broadcasted_iota(jnp.int32, (ck, tq), 0)
    ql = lax.broadcasted_iota(jnp.int32, (ck, tq), 1)
    tile4 = lambda b: jnp.concatenate([b] * GROUP, axis=1)
    key_le_query = tile4(jnp.where(kl <= ql, 0.0, MASKED))
    key_ge_query = tile4(jnp.where(kl >= ql, 0.0, MASKED))
    bias_rows = selb_ref.shape[2]

    def reset_softmax():
        slot_ref[0] = 0
        for kvh in kvhs:
            m_ref[kvh] = jnp.zeros((1, w), F32)
            acc2_ref[0, kvh] = jnp.zeros((V_ROWS, w), F32)

    def current_acc(kvh):
        return acc2_ref[slot_ref[0], kvh]

    cidx = lax.broadcasted_iota(jnp.int32, (ncp, 1), 0)
    valid = ((cidx * CMP_STRIDE + (CMP_BLOCK - 1)) <= qpos) & (cidx < nc)
    sidx = lax.broadcasted_iota(jnp.int32, (ns, 1), 0)
    cur = lax.shift_right_logical(qpos, sel_shift)
    forced = (sidx == 0) | (sidx == cur) | (sidx == cur - 1)
    causal = (sidx * SEL_BLOCK) <= qpos
    cmp_scores = [
        _dot(kc_ref[0, kvh], jnp.concatenate([qT_ref[h * HEAD_DIM:(h + 1) * HEAD_DIM, :] for h in heads[kvh]], axis=1))
        for kvh in kvhs]
    importance = []
    for kvh in kvhs:
        s = cmp_scores[kvh]
        probs = []
        for g in range(GROUP):
            sm = jnp.where(valid, s[:, g * tq:(g + 1) * tq], NEG)
            mx = jnp.max(sm, axis=0, keepdims=True)
            e = jnp.where(valid, jnp.exp2(sm - mx), 0.0)
            den = jnp.maximum(jnp.sum(e, axis=0, keepdims=True), 1e-20)
            probs.append(e / den)
        oc_ref[kvh] = _dot(vcT_ref[0, rows[kvh], :], jnp.concatenate(probs, axis=1).astype(BF16))
        psum = probs[0]
        for g in range(1, GROUP):
            psum = psum + probs[g]
        imp = _dot_f32lhs(ovT_ref[...], psum)
        importance.append(jnp.where(forced, FORCE, jnp.where(causal, imp, NEG)))

    prefixes = sorted({r for r in (ns // 4, ns // 2, 3 * ns // 4) if r and r % SUBLANES == 0} | {ns})
    needed = (i + 1) * bpc
    fill = jnp.zeros((bias_rows - bpc, tq), F32)
    for lo, hi in zip([0] + prefixes[:-1], prefixes):
        @pl.when((needed > lo) & (needed <= hi))
        def _(hi=hi):
            for kvh in kvhs:
                selb = (_topk_mask_T(importance[kvh][0:hi, :], min(n_top, hi)) - 1.0) * (-MASKED)
                for c in range(hi // bpc):
                    selb_ref[kvh, c] = jnp.concatenate([selb[c * bpc:(c + 1) * bpc, :], fill], axis=0)

    n_back = WINDOW // ck
    reset_softmax()
    rowb = lax.broadcasted_iota(jnp.int32, (bias_rows, w), 0)
    def window_chunk(r):
        a = i - n_back + r
        kst = pl.multiple_of(jnp.maximum(a, 0) * ck, ck)
        skip = jnp.where(a < 0, MASKED, 0.0)
        tile = jnp.where(rowb == bpc, skip, 0.0)
        bias = key_le_query if r == n_back else (key_ge_query if r == 0 else None)
        return ([kw_ref[kvh, pl.ds(kst, ck), :] for kvh in kvhs],
                [vwT_ref[vrows[kvh], pl.ds(kst, ck)] for kvh in kvhs], [tile] * N_KV_HEADS, bias)

    order = [n_back] + list(range(n_back))
    for g0 in [0] + list(range(1, n_back + 1, 2)):
        group = [window_chunk(r) for r in (order[g0:g0 + 1] if g0 == 0 else order[g0:g0 + 2])]
        _lazy_chunks(m_ref, acc2_ref, slot_ref, sum((g[0] for g in group), []), sum((g[1] for g in group), []),
                     qrT, sum((g[2] for g in group), []), [g[3] for g in group], bpc + 1, len(group), first=g0 == 0)
    for kvh in kvhs:
        ow_ref[kvh] = _softmax_finish((None, current_acc(kvh)))

    def chunk_step(c, diagonal, n_chunks=1, first=False):
        ks, vs, tiles = [], [], []
        for dc in range(n_chunks):
            kst = (c + dc) * ck if isinstance(c, int) else pl.multiple_of((c + dc) * ck, ck)
            ks += [ks_ref[kvh, pl.ds(kst, ck), :] for kvh in kvhs]
            vs += [vsT_ref[vrows[kvh], pl.ds(kst, ck)] for kvh in kvhs]
            tiles += [tile4(selb_ref[kvh, c + dc]) for kvh in kvhs]
        _lazy_chunks(m_ref, acc2_ref, slot_ref, ks, vs, qrT, tiles, key_le_query if diagonal else None, bpc + 1,
                     n_chunks, first)

    reset_softmax()

    @pl.when(i > 0)
    def _():
        chunk_step(0, False, first=True)

    n_mid = jnp.maximum(i - 1, 0)

    def loop_body(p, carry):
        chunk_step(1 + 2 * p, False, n_chunks=2)
        return carry

    lax.fori_loop(0, n_mid // 2, loop_body, 0)

    @pl.when(n_mid % 2 == 1)
    def _():
        chunk_step(i - 1, False)

    chunk_step(i, True, first=i == 0)

    for kvh in kvhs:
        def gate_row(j, kvh=kvh):
            return jnp.concatenate([gT_ref[h * 3 + j:h * 3 + j + 1, :] for h in heads[kvh]], axis=1)

        oT = (gate_row(0) * oc_ref[kvh] + gate_row(1) * _softmax_finish((None, current_acc(kvh)))
              + gate_row(2) * ow_ref[kvh])
        for g, h in enumerate(heads[kvh]):
            o_ref[h * HEAD_DIM:(h + 1) * HEAD_DIM, :] = oT[:, g * tq:(g + 1) * tq].astype(BF16)


def _attn_prompt(qT, qrT, gT, kc, vcT, ksb, vsT, kwb, vwT, ovT, *, batch, seq, tq, nc, n_top):
    nq = seq // tq
    nsb = kc.shape[2]
    ns = ovT.shape[0]
    col = lambda h: pl.BlockSpec((h, tq), lambda b, i: (0, b * nq + i))
    kh = pl.BlockSpec((N_KV_HEADS, seq, LANES), lambda b, i: (0, b, 0))
    vt = pl.BlockSpec((N_KV_HEADS * V_ROWS, seq), lambda b, i: (0, b))
    bf16_sublanes = 2 * SUBLANES
    return pl.pallas_call(
        functools.partial(_attn_prompt_kernel, tq=tq, nc=nc, n_top=n_top),
        grid=(batch, nq),
        in_specs=[col(Q_W), col(Q_W), col(LANES),
                  pl.BlockSpec((1, N_KV_HEADS, nsb, HEAD_DIM), lambda b, i: (b, 0, 0, 0)),
                  pl.BlockSpec((1, KV_W, nsb), lambda b, i: (b, 0, 0)),
                  kh, vt, kh, vt, _full(ovT.shape)],
        out_specs=col(Q_W),
        out_shape=jax.ShapeDtypeStruct((Q_W, batch * seq), BF16),
        scratch_shapes=[pltpu.VMEM((N_KV_HEADS, ns * SEL_BLOCK // tq, bf16_sublanes, tq), F32),
                        pltpu.VMEM((N_KV_HEADS, HEAD_DIM, GROUP * tq), F32),
                        pltpu.VMEM((N_KV_HEADS, HEAD_DIM, GROUP * tq), F32),
                        pltpu.VMEM((N_KV_HEADS, 1, GROUP * tq), F32),
                        pltpu.VMEM((2, N_KV_HEADS, V_ROWS, GROUP * tq), F32),
                        pltpu.SMEM((1,), jnp.int32)],
        compiler_params=_params("parallel", "arbitrary"),
        name="nsa_attn_prompt",
    )(qT, qrT, gT, kc, vcT, ksb, vsT, kwb, vwT, ovT)


def _resident(shape):
    zeros = (0,) * len(shape)
    return pl.BlockSpec(shape, lambda *_: zeros, pipeline_mode=pl.Buffered(1))


def _layer_tail_kernel(*refs, mixer, transposed, ff_chunk, next_norm):
    if mixer == "nsa":
        o_ref, wo_ref, x_ref, g1_ref, g2_ref = refs[:5]
        rest = refs[5:]
        if transposed:
            y = lax.dot_general(o_ref[...], wo_ref[...], (((0,), (0,)), ((), ())), preferred_element_type=F32)
        else:
            y = _dot(o_ref[...], wo_ref[...])
    else:
        yc_ref, u_ref, d_ref, wg_ref, bg_ref, x_ref, g1_ref, g2_ref = refs[:8]
        rest = refs[8:]
        z = jax.nn.gelu(yc_ref[...] + d_ref[...] * u_ref[...])
        y = z * jax.nn.sigmoid(_dot(z.astype(BF16), wg_ref[...]) + bg_ref[...])
    w1_ref, w2_ref, g3_ref = rest[:3]
    if next_norm:
        gn_ref, h2_ref, xn_ref, acc_ref = rest[3:]
    else:
        h2_ref, acc_ref = rest[3:]
    h = x_ref[...] + _rms(y, g1_ref[...])
    xm = _rms(h, g2_ref[...]).astype(BF16)
    for c in range(w1_ref.shape[1] // ff_chunk):
        cols = slice(c * ff_chunk, (c + 1) * ff_chunk)
        hm = jnp.maximum(_dot(xm, w1_ref[:, cols]), 0.0)
        part = _dot((hm * hm).astype(BF16), w2_ref[cols, :])
        if c == 0:
            acc_ref[...] = part
        else:
            acc_ref[...] += part
    h2 = h + _rms(acc_ref[...], g3_ref[...])
    h2_ref[...] = h2
    if next_norm:
        xn_ref[...] = _rms(h2, gn_ref[...])


def _layer_tail(mixer_args, x, g1, g2, w1, w2, g3, gn, *, mixer, tm, ff_chunk, transposed=False):
    rows, d = x.shape
    row = pl.BlockSpec((tm, d), lambda i: (i, 0))
    vec = _resident((1, d))
    if mixer == "nsa":
        o, w_o = mixer_args
        o_spec = (pl.BlockSpec((Q_W, tm), lambda i: (0, i)) if transposed
                  else pl.BlockSpec((tm, Q_W), lambda i: (i, 0)))
        head_specs = [o_spec, _resident(w_o.shape)]
    else:
        w_glu = mixer_args[3]
        head_specs = [row, row, vec, _resident(w_glu.shape), vec]
    tail_args = (w1, w2, g3) + (() if gn is None else (gn,))
    tail_specs = [_resident(w1.shape), _resident(w2.shape), vec] + ([] if gn is None else [vec])
    n_out = 1 if gn is None else 2
    return pl.pallas_call(
        functools.partial(_layer_tail_kernel, mixer=mixer, transposed=transposed, ff_chunk=ff_chunk,
                          next_norm=gn is not None),
        grid=(rows // tm,),
        in_specs=head_specs + [row, vec, vec] + tail_specs,
        out_specs=[row] * n_out,
        out_shape=[jax.ShapeDtypeStruct((rows, d), F32)] * n_out,
        scratch_shapes=[pltpu.VMEM((tm, d), F32)],
        compiler_params=_params("parallel"),
        name=mixer + "_layer_tail",
    )(*mixer_args, x, g1, g2, *tail_args)


def _s5_seq_kernel(x_ref, brow_ref, pre_ref, pim_ref, qre_ref, qim_ref, are_ref, aim_ref, y_ref, hfin_ref,
                   sre, sim, hre, him, cre, cim):
    nb, tb, uw = x_ref.shape
    nt = pre_ref.shape[1] // uw
    nk = tb // nt

    @pl.when(pl.program_id(1) == 0)
    def _():
        cre[...] = jnp.zeros(cre.shape, F32)
        cim[...] = jnp.zeros(cim.shape, F32)

    u = [jnp.concatenate([x_ref[b, pl.ds(t, nk, stride=nt), :] for b in range(nb)], axis=0).astype(BF16)
         for t in range(nt)]
    u2 = [jnp.concatenate([u[2 * j], u[2 * j + 1]], axis=1) for j in range(nt // 2)]
    s_re = _dot(u2[0], pre_ref[0, 0:2 * uw, :])
    s_im = _dot(u2[0], pim_ref[0, 0:2 * uw, :])
    for j in range(1, nt // 2):
        s_re = s_re + _dot(u2[j], pre_ref[0, 2 * j * uw:(2 * j + 2) * uw, :])
        s_im = s_im + _dot(u2[j], pim_ref[0, 2 * j * uw:(2 * j + 2) * uw, :])
    sre[...] = s_re
    sim[...] = s_im
    ar = are_ref[0]
    ai = aim_ref[0]

    def body(it, carry):
        out = []
        for b in range(nb):
            hr, hi = carry[b]
            r0 = pl.multiple_of(b * nk + it * SUBLANES, SUBLANES)
            sr8 = sre[pl.ds(r0, SUBLANES), :]
            si8 = sim[pl.ds(r0, SUBLANES), :]
            prev_r, prev_i = [], []
            for j in range(SUBLANES):
                prev_r.append(hr)
                prev_i.append(hi)
                hr, hi = (ar * hr - ai * hi + sr8[j:j + 1, :], ar * hi + ai * hr + si8[j:j + 1, :])
            hre[pl.ds(r0, SUBLANES), :] = jnp.concatenate(prev_r, axis=0)
            him[pl.ds(r0, SUBLANES), :] = jnp.concatenate(prev_i, axis=0)
            out.append((hr, hi))
        return tuple(out)

    init = tuple((cre[b:b + 1, :], cim[b:b + 1, :]) for b in range(nb))
    fin = lax.fori_loop(0, nk // SUBLANES, body, init)
    for b in range(nb):
        cre[b:b + 1, :] = fin[b][0]
        cim[b:b + 1, :] = fin[b][1]
    hfin_ref[0, 0] = cre[...]
    hfin_ref[0, 1] = cim[...]

    hb_re = hre[...].astype(BF16)
    hb_im = him[...].astype(BF16)
    for t2 in range(0, nt, 2):
        cols = slice(t2 * uw, (t2 + 2) * uw)
        acc = _dot(hb_re, qre_ref[0, :, cols]) + _dot(hb_im, qim_ref[0, :, cols])
        for j in range(t2 // 2 + 1):
            lag0 = t2 - 2 * j + 1
            wpair = jnp.concatenate([brow_ref[0, :, lag0 * uw:(lag0 + 2) * uw],
                                     brow_ref[0, :, (lag0 - 1) * uw:(lag0 + 1) * uw]], axis=0)
            acc = acc + _dot(u2[j], wpair)
        for b in range(nb):
            for dt in range(2):
                y_ref[b, pl.ds(t2 + dt, nk, stride=nt), :] = acc[b * nk:(b + 1) * nk, dt * uw:(dt + 1) * uw]


def _s5_seq(x3, ops):
    nb, t, d = x3.shape
    uw = SSM_UNIT * SSM_GROUP
    nunit = d // uw
    sw = ops["p_re"].shape[2]
    tb = _tile(t, S5_TIME_BLOCK)
    nk = tb // SSM_CHUNK
    per_unit = lambda a: pl.BlockSpec((1,) + a.shape[1:], lambda i, r: (i,) + (0,) * (a.ndim - 1))
    blk = pl.BlockSpec((nb, tb, uw), lambda i, r: (0, r, i))
    args = (x3, ops["brow"], ops["p_re"], ops["p_im"], ops["q_re"], ops["q_im"], ops["a_chunk_re"], ops["a_chunk_im"])
    return pl.pallas_call(
        _s5_seq_kernel,
        grid=(nunit, t // tb),
        in_specs=[blk] + [per_unit(a) for a in args[1:]],
        out_specs=[blk, pl.BlockSpec((1, 2, nb, sw), lambda i, r: (i, 0, 0, 0))],
        out_shape=[jax.ShapeDtypeStruct((nb, t, d), F32), jax.ShapeDtypeStruct((nunit, 2, nb, sw), F32)],
        scratch_shapes=[pltpu.VMEM((nb * nk, sw), F32)] * 4 + [pltpu.VMEM((nb, sw), F32)] * 2,
        compiler_params=_params("parallel", "arbitrary"),
        name="s5_seq_scan",
    )(*args)


def _s5_step_kernel(u_ref, h0r_ref, h0i_ref, bre_ref, bim_ref, cre_ref, cim_ref, are_ref, aim_ref,
                    y_ref, hr_ref, hi_ref, *, npair):
    for p in range(npair):
        u = u_ref[p]
        ar = are_ref[p]
        ai = aim_ref[p]
        h0r = h0r_ref[p]
        h0i = h0i_ref[p]
        hr = ar * h0r - ai * h0i + _dot(u, bre_ref[p])
        hi = ar * h0i + ai * h0r + _dot(u, bim_ref[p])
        hr_ref[p] = hr
        hi_ref[p] = hi
        y_ref[p] = _dot(hr.astype(BF16), cre_ref[p]) + _dot(hi.astype(BF16), cim_ref[p])


def _s5_step(u2, h0r, h0i, ops):
    npair, rows, width = u2.shape
    args = (u2, h0r, h0i, ops["b1_re"], ops["b1_im"], ops["c1_re"], ops["c1_im"], ops["a1_re"], ops["a1_im"])
    return pl.pallas_call(
        functools.partial(_s5_step_kernel, npair=npair),
        grid=(1,),
        in_specs=[_full(a.shape) for a in args],
        out_specs=[_full((npair, rows, width)), _full(h0r.shape), _full(h0r.shape)],
        out_shape=[jax.ShapeDtypeStruct((npair, rows, width), F32), jax.ShapeDtypeStruct(h0r.shape, F32),
                   jax.ShapeDtypeStruct(h0r.shape, F32)],
        compiler_params=_params("arbitrary"),
        name="s5_single_step",
    )(*args)


def _s5_operators(a_re, a_im, log_dt, b_re, b_im, c_re, c_im):
    hp = lax.Precision.HIGHEST
    g, n = a_re.shape
    gu = SSM_UNIT
    nunit = g // gu
    L = SSM_CHUNK
    uw = gu * SSM_GROUP
    sw = gu * n
    a = lax.complex(a_re.astype(F32), a_im.astype(F32))
    dt = jnp.exp(log_dt.astype(F32))[:, None]
    a_bar = jnp.exp(a * dt)
    b_bar = ((a_bar - 1.0) / a)[:, :, None] * lax.complex(b_re.astype(F32), b_im.astype(F32))
    c = lax.complex(c_re.astype(F32), c_im.astype(F32))
    pows = [jnp.ones_like(a_bar)]
    for _ in range(L):
        pows.append(pows[-1] * a_bar)
    a_pow = jnp.stack(pows).reshape(L + 1, nunit, sw)
    apr = jnp.real(a_pow)
    api = jnp.imag(a_pow)
    eye = jnp.eye(gu, dtype=F32)

    def bd_in(x):
        return jnp.einsum("pgnd,gh->pgdhn", x.reshape(nunit, gu, n, SSM_GROUP), eye).reshape(nunit, uw, sw)

    def bd_out(x):
        return jnp.einsum("pgcn,gh->pgnhc", x.reshape(nunit, gu, SSM_GROUP, n), eye).reshape(nunit, sw, uw)

    p0r, p0i = bd_in(jnp.real(b_bar)), bd_in(jnp.imag(b_bar))
    q0r, q0i = bd_out(jnp.real(c)), bd_out(jnp.imag(c))
    lanes = lambda x, k: x[k][:, None, :]
    rows_ = lambda x, k: x[k][:, :, None]
    p_re = jnp.concatenate([p0r * lanes(apr, L - 1 - t) - p0i * lanes(api, L - 1 - t) for t in range(L)], axis=1)
    p_im = jnp.concatenate([p0r * lanes(api, L - 1 - t) + p0i * lanes(apr, L - 1 - t) for t in range(L)], axis=1)
    q_re = jnp.concatenate([q0r * rows_(apr, t + 1) - q0i * rows_(api, t + 1) for t in range(L)], axis=2)
    q_im = jnp.concatenate([-(q0r * rows_(api, t + 1) + q0i * rows_(apr, t + 1)) for t in range(L)], axis=2)
    lag_blocks = [jnp.zeros((nunit, uw, uw), F32)]
    for t in range(L):
        xr = p0r * lanes(apr, t) - p0i * lanes(api, t)
        xi = p0r * lanes(api, t) + p0i * lanes(apr, t)
        lag_blocks.append(jnp.einsum("pus,psv->puv", xr, q0r, precision=hp)
                          - jnp.einsum("pus,psv->puv", xi, q0i, precision=hp))
    brow = jnp.concatenate(lag_blocks, axis=2)
    lane_row = lambda x: x.reshape(nunit, 1, sw)
    return {
        "brow": brow.astype(BF16),
        "p_re": p_re.astype(BF16), "p_im": p_im.astype(BF16),
        "q_re": q_re.astype(BF16), "q_im": q_im.astype(BF16),
        "a_chunk_re": lane_row(apr[L]), "a_chunk_im": lane_row(api[L]),
        "a1_re": lane_row(apr[1]), "a1_im": lane_row(api[1]),
        "b1_re": p0r.astype(BF16), "b1_im": p0i.astype(BF16),
        "c1_re": q0r.astype(BF16), "c1_im": (-q0i).astype(BF16),
    }


def _compress_paged_kernel(pt_ref, *refs, pg, nsb):
    del pt_ref
    pages = refs[:pg]
    perm_ref, wbd_ref, pe_ref, w2_ref, out_ref, h0_ref, h1_ref = refs[pg:]
    j = pl.program_id(1)
    sbp = PAGE_SIZE // CMP_STRIDE
    nrows = pg * sbp
    pair_rows = 2 * sbp

    @pl.when(j == 0)
    def _():
        h1_ref[:, nsb:nsb + SUBLANES, :] = jnp.zeros((2, SUBLANES, KV_W), F32)

    r0 = pl.multiple_of(j * nrows, nrows)
    for kv in range(2):
        staged = []
        for q in range(pg // 2):
            z = jnp.concatenate([pages[2 * q][0, kv], pages[2 * q + 1][0, kv]], axis=1).astype(BF16)
            staged.append(_dot_nt(perm_ref[...], z).astype(BF16))

        def load_rows(s, kv, staged=staged):
            return jnp.concatenate([x[s * pair_rows:(s + 1) * pair_rows, :] for x in staged], axis=0)

        pr0, pr1 = _compress_half(load_rows, kv, nrows, wbd_ref, pe_ref)
        h0_ref[kv, pl.ds(r0, nrows), :] = pr0
        h1_ref[kv, pl.ds(r0, nrows), :] = pr1

    @pl.when(j == pl.num_programs(1) - 1)
    def _():
        for kv in range(2):
            h = h0_ref[kv] + h1_ref[kv, pl.ds(1, nsb), :]
            out = _dot(jax.nn.gelu(h).astype(BF16), w2_ref[kv])
            out_ref[0, :, kv * KV_W:(kv + 1) * KV_W] = out.astype(BF16)


def _page_specs(pg, pages_per_sample):
    def spec(i):
        return pl.BlockSpec((1, 2, KV_W, PAGE_SIZE),
                            lambda b, j, pt: (pt[b * pages_per_sample + j * pg + i], 0, 0, 0))
    return [spec(i) for i in range(pg)]


def _compress_paged(pages, pt_flat, wbd, pe_t, w2bd, *, nb, pages_per_sample, pg):
    nsb = pages_per_sample * PAGE_SIZE // CMP_STRIDE
    sbp = PAGE_SIZE // CMP_STRIDE
    i_out = np.arange(2 * PAGE_SIZE)
    s_i, pg_i, n_i = i_out // (2 * sbp), (i_out // sbp) % 2, i_out % sbp
    perm = jnp.asarray(i_out[None, :] == (pg_i * PAGE_SIZE + n_i * CMP_STRIDE + s_i)[:, None], BF16)
    c3 = lambda shape: pl.BlockSpec(shape, lambda b, j, pt: (0,) * len(shape))
    grid_spec = pltpu.PrefetchScalarGridSpec(
        num_scalar_prefetch=1,
        grid=(nb, pages_per_sample // pg),
        in_specs=_page_specs(pg, pages_per_sample) + [c3(perm.shape), c3(wbd.shape), c3(pe_t.shape), c3(w2bd.shape)],
        out_specs=pl.BlockSpec((1, nsb, 2 * KV_W), lambda b, j, pt: (b, 0, 0)),
        scratch_shapes=[pltpu.VMEM((2, nsb, KV_W), F32), pltpu.VMEM((2, nsb + SUBLANES, KV_W), F32)],
    )
    return pl.pallas_call(
        functools.partial(_compress_paged_kernel, pg=pg, nsb=nsb),
        grid_spec=grid_spec,
        out_shape=jax.ShapeDtypeStruct((nb, nsb, 2 * KV_W), BF16),
        compiler_params=_params("parallel", "arbitrary"),
        name="nsa_compress_paged",
    )(pt_flat, *([pages] * pg), perm, wbd, pe_t, w2bd)


def _topk_mask_lanes(imp, n_top, ns_valid):
    lane = lax.broadcasted_iota(jnp.int32, imp.shape, 1)
    cnt = jnp.zeros(imp.shape, F32)
    for sp in range(ns_valid):
        col = imp[:, sp:sp + 1]
        cnt = cnt + jnp.where(lane > sp, jnp.where(col >= imp, 1.0, 0.0), jnp.where(col > imp, 1.0, 0.0))
    return jnp.where((cnt < n_top) & (lane < ns_valid), 1.0, 0.0)


def _dot_nt(a, b):
    return lax.dot_general(a, b, (((1,), (1,)), ((), ())), preferred_element_type=F32)


def _attn_sample_kernel(pt_ref, *refs, pg, ns, past, nc, ns_valid, n_top):
    del pt_ref
    all_pages = [refs[u * pg:(u + 1) * pg] for u in range(ns)]
    (q_ref, qr_ref, g_ref, kcvc_ref, ksn_ref, kwn_ref, win_ref, ov_ref, e_ref, gs_ref, o_ref,
     m_ref, l_ref, acc_ref, sel_ref, oc_ref, ow_ref) = refs[ns * pg:]
    j = pl.program_id(1)
    ncp = kcvc_ref.shape[1]
    nsp = ov_ref.shape[1]
    wlen = win_ref.shape[3]
    row = lax.broadcasted_iota(jnp.int32, (N_HEADS, KV_W), 0)
    lane = lax.broadcasted_iota(jnp.int32, (N_HEADS, KV_W), 1)
    own = (lane // HEAD_DIM) == (row // GROUP)

    def spread(ref, u):
        q = ref[u]
        return jnp.where(own, jnp.concatenate([q] * N_KV_HEADS, axis=1), jnp.zeros((N_HEADS, KV_W), BF16))

    def update(state, s, vs):
        m, l, acc = state
        m_new = jnp.maximum(m, jnp.max(s, axis=1, keepdims=True))
        alpha = jnp.exp(m - m_new)
        p = jnp.exp(s - m_new)
        l = alpha * l + jnp.sum(p, axis=1, keepdims=True)
        pv = None
        for st, sz, v, feature_major in vs:
            pb = p[:, st:st + sz].astype(BF16)
            t = _dot_nt(pb, v) if feature_major else _dot(pb, v)
            pv = t if pv is None else pv + t
        return m_new, l, alpha * acc + pv

    def init():
        return (jnp.full((N_HEADS, 1), NEG, F32), jnp.zeros((N_HEADS, 1), F32), jnp.zeros((N_HEADS, KV_W), F32))

    def new_row_update(state, qbd, new_row):
        r8 = lax.broadcasted_iota(jnp.int32, (SUBLANES, 2 * KV_W), 0)
        tile = jnp.where(r8 == 0, jnp.broadcast_to(new_row, (SUBLANES, 2 * KV_W)), 0.0).astype(BF16)
        s = _dot_nt(qbd, tile[:, 0:KV_W])
        l8 = lax.broadcasted_iota(jnp.int32, (N_HEADS, SUBLANES), 1)
        s = jnp.where(l8 == 0, s, MASKED)
        return update(state, s, [(0, SUBLANES, tile[:, KV_W:2 * KV_W], False)])

    def put(u, state):
        m, l, acc = state
        m_ref[u] = jnp.broadcast_to(m, m_ref.shape[1:])
        l_ref[u] = jnp.broadcast_to(l, l_ref.shape[1:])
        acc_ref[u] = acc

    qrbd = [spread(qr_ref, u) for u in range(ns)]

    @pl.when(j == 0)
    def _():
        cidx = lax.broadcasted_iota(jnp.int32, (1, ncp), 1)
        valid = ((cidx * CMP_STRIDE + (CMP_BLOCK - 1)) <= past) & (cidx < nc)
        sidx = lax.broadcasted_iota(jnp.int32, (1, nsp), 1)
        cur = past // SEL_BLOCK
        forced = (sidx == 0) | (sidx == cur) | (sidx == cur - 1)
        causal = (sidx * SEL_BLOCK) <= past
        wpos = past - wlen + lax.broadcasted_iota(jnp.int32, (1, wlen), 1)
        in_window = (wpos >= 0) & (past - wpos <= WINDOW)
        for u in range(ns):
            s = _dot_nt(spread(q_ref, u), kcvc_ref[u, :, 0:KV_W])
            sm = jnp.where(valid, s, NEG)
            mx = jnp.max(sm, axis=1, keepdims=True)
            e = jnp.where(valid, jnp.exp(sm - mx), 0.0)
            p = e / jnp.maximum(jnp.sum(e, axis=1, keepdims=True), 1e-20)
            oc_ref[u] = _dot(p.astype(BF16), kcvc_ref[u, :, KV_W:2 * KV_W])
            imp = _dot_f32rhs(_dot_f32lhs(gs_ref[...], p), ov_ref[...])
            imp = jnp.where(forced, FORCE, jnp.where(causal, imp, NEG))
            imp = jnp.where(sidx < ns_valid, imp, MASKED)
            sel_ref[u] = _topk_mask_lanes(imp, n_top, ns_valid)
            sw = jnp.where(in_window, _dot(qrbd[u], win_ref[u, 0].astype(BF16)), MASKED)
            st = update(init(), sw, [(0, wlen, win_ref[u, 1].astype(BF16), True)])
            st = new_row_update(st, qrbd[u], kwn_ref[u])
            ow_ref[u] = st[2] * (1.0 / jnp.maximum(st[1], 1e-20))
            put(u, new_row_update(init(), qrbd[u], ksn_ref[u]))

    def halves(pages):
        return [pages[:pg // 2], pages[pg // 2:]] if pg > 1 else [pages]

    scores = [[jnp.concatenate([_dot(qrbd[u], r[0, 0].astype(BF16)) for r in part], axis=1)
               for part in halves(all_pages[u])] for u in range(ns)]
    masks = [_dot(sel_ref[u].astype(BF16), e_ref[0]) > 0.5 for u in range(ns)]
    states = [(m_ref[u, :, 0:1], l_ref[u, :, 0:1], acc_ref[u]) for u in range(ns)]
    k0 = 0
    for hx, part0 in enumerate(halves(all_pages[0])):
        width = len(part0) * PAGE_SIZE
        for u in range(ns):
            part = halves(all_pages[u])[hx]
            s = jnp.where(masks[u][:, k0:k0 + width], scores[u][hx], MASKED)
            vs = [(i * PAGE_SIZE, PAGE_SIZE, r[0, 1].astype(BF16), True) for i, r in enumerate(part)]
            states[u] = update(states[u], s, vs)
        k0 += width
    for u in range(ns):
        put(u, states[u])

    @pl.when(j == pl.num_programs(1) - 1)
    def _():
        for u in range(ns):
            g = g_ref[u]
            os_ = acc_ref[u] * (1.0 / jnp.maximum(l_ref[u, :, 0:1], 1e-20))
            o = g[:, 0:1] * oc_ref[u] + g[:, 1:2] * os_ + g[:, 2:3] * ow_ref[u]
            o = jnp.where(own, o, 0.0)
            out = o[:, 0:HEAD_DIM]
            for h in range(1, N_KV_HEADS):
                out = out + o[:, h * HEAD_DIM:(h + 1) * HEAD_DIM]
            o_ref[u] = out.astype(BF16)


def _attn_sample(pages, pt_flat, q3, qr3, g3, kcvc, ks_new, kw_new, win, ov, e_mat, gsum, *, nb, pages_per_sample,
                 pg, past, nc, ns_valid, n_top):
    nsp = ov.shape[1]
    ns = SAMPLES_PER_STEP if nb % SAMPLES_PER_STEP == 0 else 1
    per_b = lambda shape: pl.BlockSpec((ns,) + shape, lambda b, j, pt: (b,) + (0,) * len(shape))
    const = lambda shape: pl.BlockSpec(shape, lambda b, j, pt: (0,) * len(shape))

    def page_spec(u, i):
        return pl.BlockSpec((1, 2, KV_W, PAGE_SIZE),
                            lambda b, j, pt: (pt[(b * ns + u) * pages_per_sample + j * pg + i], 0, 0, 0))

    per_sample = lambda *shape: pltpu.VMEM((ns,) + shape, F32)
    grid_spec = pltpu.PrefetchScalarGridSpec(
        num_scalar_prefetch=1,
        grid=(nb // ns, pages_per_sample // pg),
        in_specs=[page_spec(u, i) for u in range(ns) for i in range(pg)] + [
            per_b((N_HEADS, HEAD_DIM)), per_b((N_HEADS, HEAD_DIM)), per_b((N_HEADS, 3)),
            per_b(kcvc.shape[1:]), per_b((1, 2 * KV_W)), per_b((1, 2 * KV_W)), per_b(win.shape[1:]),
            const(ov.shape), pl.BlockSpec((1,) + e_mat.shape[1:], lambda b, j, pt: (j, 0, 0)), const(gsum.shape)],
        out_specs=per_b((N_HEADS, HEAD_DIM)),
        scratch_shapes=[per_sample(N_HEADS, LANES), per_sample(N_HEADS, LANES), per_sample(N_HEADS, KV_W),
                        per_sample(N_HEADS, nsp), per_sample(N_HEADS, KV_W), per_sample(N_HEADS, KV_W)],
    )
    return pl.pallas_call(
        functools.partial(_attn_sample_kernel, pg=pg, ns=ns, past=past, nc=nc, ns_valid=ns_valid, n_top=n_top),
        grid_spec=grid_spec,
        out_shape=jax.ShapeDtypeStruct((nb, N_HEADS, HEAD_DIM), BF16),
        compiler_params=_params("parallel", "arbitrary"),
        name="nsa_attn_sample",
    )(pt_flat, *([pages] * (ns * pg)), q3, qr3, g3, kcvc, ks_new, kw_new, win, ov, e_mat, gsum)


def _rope_tables(pos):
    half = HEAD_DIM // 2
    inv = ROPE_THETA ** (-jnp.arange(half, dtype=F32) / half)
    ang = pos.astype(F32)[:, None] * inv[None, :]
    cos = jnp.cos(ang)
    sin = jnp.sin(ang)
    reps = LANES // HEAD_DIM
    return (jnp.tile(jnp.concatenate([cos, cos], axis=1), (1, reps)),
            jnp.tile(jnp.concatenate([-sin, sin], axis=1), (1, reps)))


def _compress_weights(cmp_w1, cmp_w2, cmp_pe):
    ratio = CMP_BLOCK // CMP_STRIDE
    w1r = cmp_w1.reshape(2, ratio, CMP_STRIDE, HEAD_DIM, HEAD_DIM)
    hh = np.arange(KV_W) // HEAD_DIM
    same_head = jnp.asarray(hh[:, None] == hh[None, :])

    def block_diag(w):
        tiled = jnp.concatenate([jnp.concatenate([w] * N_KV_HEADS, axis=-1)] * N_KV_HEADS, axis=-2)
        return jnp.where(same_head, tiled, 0.0).astype(BF16)

    wbd = block_diag(w1r.reshape(2 * ratio * CMP_STRIDE, HEAD_DIM, HEAD_DIM))
    w2bd = block_diag(cmp_w2)
    pe_r = cmp_pe.reshape(2, ratio, CMP_STRIDE, HEAD_DIM).astype(F32)
    pe_w = jnp.einsum("krsd,krsde->kre", pe_r, w1r.astype(F32), precision=lax.Precision.HIGHEST)
    pe_t = jnp.tile(pe_w.reshape(2 * ratio, HEAD_DIM), (1, N_KV_HEADS))
    return wbd, pe_t, w2bd


def _overlap(nc, ncp, nsel, nsp):
    c_start = np.arange(ncp)[:, None] * CMP_STRIDE
    s_start = np.arange(nsp)[None, :] * SEL_BLOCK
    ov = (c_start < s_start + SEL_BLOCK) & (c_start + CMP_BLOCK > s_start)
    ov = ov & (np.arange(ncp)[:, None] < nc) & (np.arange(nsp)[None, :] < nsel)
    return jnp.asarray(ov, BF16)


def _tile(n, pref):
    t = min(n, pref)
    while n % t:
        t //= 2
    return t


def kernel(x_prompt, x_sample, cache_kv_cmp, cache_kv_sel, cache_kv_win, state_ssm, page_table, norm_g, mlp_w1,
           mlp_w2, nsa_w_in, nsa_w_o, nsa_cmp_w1, nsa_cmp_w2, nsa_cmp_pe, s5_a_re, s5_a_im, s5_log_dt, s5_b_re,
           s5_b_im, s5_c_re, s5_c_im, s5_d, s5_w_glu, s5_b_glu):
    b, t, d = x_prompt.shape
    nb = x_sample.shape[0]
    pages_per_sample = page_table.shape[1]
    past = pages_per_sample * PAGE_SIZE
    rows_p = b * t
    g = norm_g.reshape(norm_g.shape[0], 4, 1, d)

    w_in = nsa_w_in[0]
    w_q = w_in[:, :Q_W].astype(BF16)
    w_kv = w_in[:, Q_W:Q_W + 6 * KV_W].astype(BF16)
    w_gate = jnp.pad(w_in[:, Q_W + 6 * KV_W:], ((0, 0), (0, LANES - 3 * N_HEADS))).astype(BF16)
    w_o = nsa_w_o[0].astype(BF16)
    wbd, pe_t, w2bd = _compress_weights(nsa_cmp_w1[0], nsa_cmp_w2[0], nsa_cmp_pe[0])
    w1 = mlp_w1.astype(BF16)
    w2 = mlp_w2.astype(BF16)
    w_glu = s5_w_glu[0].astype(BF16)
    ops = _s5_operators(s5_a_re[0], s5_a_im[0], s5_log_dt[0], s5_b_re[0], s5_b_im[0], s5_c_re[0], s5_c_im[0])
    d_skip = s5_d[0].reshape(1, d)
    b_glu = s5_b_glu[0].reshape(1, d)

    tm = _tile(rows_p, 512)
    ff_chunk = _tile(mlp_w1.shape[2], 1024)

    xp = x_prompt.reshape(rows_p, d)
    cos_p, sin_p = _rope_tables(jnp.arange(t, dtype=jnp.int32))
    tq = next(c for c in (2 * LANES, LANES) if t % c == 0 and WINDOW % c == 0)
    (qT, qrT, gT, kvc, kvcT, kvsT, kvwT, ksb, kwb, vsT, vwT) = _inproj(
        xp, g[0, 0], w_q.T, w_kv, w_gate, cos_p, sin_p, tm=_tile(t, 512), pos_blocks=t // _tile(t, 512), transposed=True,
        key_chunk=tq)
    nsb_p = t // CMP_STRIDE
    nc_p = nsb_p - CMP_BLOCK // CMP_STRIDE + 1
    nsel_p = t // SEL_BLOCK
    kc, vcT = _compress_prompt(kvc.reshape(b, t, 2 * KV_W), wbd, pe_t, w2bd)
    ovT = _overlap(nc_p, nsb_p, nsel_p, nsel_p).T
    oT = _attn_prompt(qT, qrT, gT, kc, vcT, ksb, vsT, kwb, vwT, ovT, batch=b, seq=t, tq=tq, nc=nc_p,
                      n_top=min(TOP_N, nsel_p))
    hp, xn1 = _layer_tail((oT, w_o), xp, g[0, 1], g[0, 2], w1[0], w2[0], g[0, 3], g[1, 0], mixer="nsa", tm=tm,
                          ff_chunk=ff_chunk, transposed=True)

    pw = SSM_UNIT * SSM_GROUP
    npair = d // pw
    y3, hfin = _s5_seq(xn1.reshape(b, t, d), ops)
    yc = y3.reshape(rows_p, d)
    (hp,) = _layer_tail((yc, xn1, d_skip, w_glu, b_glu), hp, g[1, 1], g[1, 2], w1[1], w2[1], g[1, 3], None,
                        mixer="s5", tm=tm, ff_chunk=ff_chunk)
    ssm_p = hfin.reshape(npair, 2, b, SSM_UNIT, SSM_STATE).transpose(2, 1, 0, 3, 4)
    ssm_p = ssm_p.reshape(b, 2, d // SSM_GROUP, SSM_STATE)

    xs = x_sample.reshape(nb, d)
    cos_s, sin_s = _rope_tables(jnp.full((nb,), past, dtype=jnp.int32))
    q_s, qr_s, gates_s, kvc_s, kvs_s, kvw_s = _inproj(
        xs, g[0, 0], w_q, w_kv, w_gate, cos_s, sin_s, tm=nb, pos_blocks=1, transposed=False)
    pt_flat = page_table.reshape(-1).astype(jnp.int32)
    pg = _tile(pages_per_sample, PAGE_GROUP)
    n_pool = cache_kv_cmp.shape[1]
    feature_major = lambda c, n, s: c.transpose(0, 2, 3, 4, 1).reshape(n, 2, KV_W, s)
    cmp_pages = feature_major(cache_kv_cmp[0], n_pool, PAGE_SIZE)
    sel_pages = feature_major(cache_kv_sel[0], n_pool, PAGE_SIZE)
    kcvc = _compress_paged(cmp_pages, pt_flat, wbd, pe_t, w2bd, nb=nb, pages_per_sample=pages_per_sample, pg=pg)
    l_all = past + 1
    nsb_s = l_all // CMP_STRIDE
    nc_s = nsb_s - CMP_BLOCK // CMP_STRIDE + 1
    nsel_s = -(-l_all // SEL_BLOCK)
    nsp = -(-nsel_s // LANES) * LANES
    ov_s = _overlap(nc_s, past // CMP_STRIDE, nsel_s, nsp)
    keys_per_step = pg * PAGE_SIZE
    key_blk = (np.arange(past) // SEL_BLOCK).reshape(past // keys_per_step, 1, keys_per_step)
    e_mat = jnp.asarray(np.arange(nsp)[None, :, None] == key_blk, BF16)
    hh = np.arange(N_HEADS)
    gsum = jnp.asarray((hh[:, None] // GROUP) == (hh[None, :] // GROUP), BF16)
    win = feature_major(cache_kv_win[0], nb, WINDOW)
    o_s = _attn_sample(sel_pages, pt_flat, q_s.reshape(nb, N_HEADS, HEAD_DIM), qr_s.reshape(nb, N_HEADS, HEAD_DIM),
                       gates_s[:, :3 * N_HEADS].reshape(nb, N_HEADS, 3), kcvc, kvs_s.reshape(nb, 1, 2 * KV_W),
                       kvw_s.reshape(nb, 1, 2 * KV_W), win, ov_s, e_mat, gsum, nb=nb,
                       pages_per_sample=pages_per_sample, pg=pg, past=past, nc=nc_s, ns_valid=nsel_s,
                       n_top=min(TOP_N, nsel_s))
    hs, xn1_s = _layer_tail((o_s.reshape(nb, Q_W), w_o), xs, g[0, 1], g[0, 2], w1[0], w2[0], g[0, 3], g[1, 0],
                            mixer="nsa", tm=nb, ff_chunk=ff_chunk)

    u2_s = xn1_s.astype(BF16).reshape(nb, npair, pw).transpose(1, 0, 2)
    st = state_ssm[0].reshape(nb, 2, npair, SSM_UNIT * SSM_STATE).transpose(1, 2, 0, 3)
    y2_s, hr_s, hi_s = _s5_step(u2_s, st[0], st[1], ops)
    yc_s = y2_s.transpose(1, 0, 2).reshape(nb, d)
    (hs,) = _layer_tail((yc_s, xn1_s, d_skip, w_glu, b_glu), hs, g[1, 1], g[1, 2], w1[1], w2[1], g[1, 3], None,
                        mixer="s5", tm=nb, ff_chunk=ff_chunk)
    ssm_s = jnp.stack([hr_s, hi_s], axis=0).transpose(2, 0, 1, 3).reshape(nb, 2, d // SSM_GROUP, SSM_STATE)

    kv5 = lambda a, n, s: a.reshape(1, n, s, 2, N_KV_HEADS, HEAD_DIM)
    from_fm = lambda a, n, s: a.reshape(n, 2, N_KV_HEADS, HEAD_DIM, s).transpose(0, 4, 1, 2, 3)[None]
    win_s = jnp.concatenate([win[..., 1:], kvw_s.reshape(nb, 2, KV_W, 1)], axis=-1)
    return (hp.reshape(b, t, d), hs.reshape(nb, 1, d),
            from_fm(kvcT, b, t), kv5(kvc_s, nb, 1), from_fm(kvsT, b, t), kv5(kvs_s, nb, 1),
            from_fm(kvwT[:, :, t - WINDOW:], b, WINDOW), from_fm(win_s, nb, WINDOW), ssm_p[None], ssm_s[None])
```

```python
import functools

import jax
import jax.numpy as jnp
import numpy as np
from jax import lax
from jax.experimental import pallas as pl
from jax.experimental.pallas import tpu as pltpu

N_HEADS = 16
HEAD_DIM = 64
N_KV_HEADS = 4
GROUP = N_HEADS // N_KV_HEADS
CMP_BLOCK = 32
CMP_STRIDE = 16
SEL_BLOCK = 64
TOP_N = 16
WINDOW = 512
ROPE_THETA = 10000.0
PAGE_SIZE = 128
SSM_GROUP = 16
SSM_STATE = 64
SSM_CHUNK = 8
SSM_UNIT = 8
EPS = 1e-6
NEG = -1e30
FORCE = 1e9
MASKED = -1.5e38
LOG2E = 1.4426950408889634
V_ROWS = HEAD_DIM + 16
Q_W = N_HEADS * HEAD_DIM
KV_W = N_KV_HEADS * HEAD_DIM
LANES = 128
SUBLANES = 8
ROW_TILE = 512
FF_CHUNK = 1024
PAGE_GROUP = 16
SAMPLES_PER_STEP = 2
S5_TIME_BLOCK = 2048
VMEM_LIMIT = 56 * 1024 * 1024

F32 = jnp.float32
BF16 = jnp.bfloat16


def _params(*sem):
    return pltpu.CompilerParams(dimension_semantics=sem, vmem_limit_bytes=VMEM_LIMIT)


def _full(shape):
    zeros = (0,) * len(shape)
    return pl.BlockSpec(shape, lambda *_: zeros)


def _rms(x, g):
    ms = jnp.mean(x * x, axis=-1, keepdims=True)
    return x * lax.rsqrt(ms + EPS) * g


def _dot(a, b):
    return jnp.dot(a, b, preferred_element_type=F32)


def _dot_f32lhs(w, x):
    hi = x.astype(BF16)
    r1 = x - hi.astype(F32)
    mid = r1.astype(BF16)
    lo = (r1 - mid.astype(F32)).astype(BF16)
    return _dot(w, hi) + _dot(w, mid) + _dot(w, lo)


def _dot_f32rhs(x, w):
    hi = x.astype(BF16)
    r1 = x - hi.astype(F32)
    mid = r1.astype(BF16)
    lo = (r1 - mid.astype(F32)).astype(BF16)
    return _dot(hi, w) + _dot(mid, w) + _dot(lo, w)


def _rope_nat(x, cos, sin):
    half = HEAD_DIM // 2
    lane = lax.broadcasted_iota(jnp.int32, (1, LANES), 1)
    first = (lane % HEAD_DIM) < half
    outs = []
    for c in range(x.shape[1] // LANES):
        xc = x[:, c * LANES:(c + 1) * LANES]
        rot = jnp.where(first, pltpu.roll(xc, LANES - half, 1), pltpu.roll(xc, half, 1))
        outs.append(xc * cos + rot * sin)
    return jnp.concatenate(outs, axis=1)


def _inproj_kernel(x_ref, g_ref, wq_ref, wkv_ref, wg_ref, cos_ref, sin_ref, *rest, transposed, key_chunk):
    outs = rest[2:] if transposed else rest
    xb = _rms(x_ref[...], g_ref[...]).astype(BF16)
    cos = cos_ref[...]
    sin = sin_ref[...]
    scale = HEAD_DIM ** -0.5
    kv = _dot(xb, wkv_ref[...])
    gates = jax.nn.sigmoid(_dot(xb, wg_ref[...]))
    k_s = _rope_nat(kv[:, 2 * KV_W:3 * KV_W], cos, sin)
    v_s = kv[:, 3 * KV_W:4 * KV_W]
    k_w = _rope_nat(kv[:, 4 * KV_W:5 * KV_W], cos, sin)
    v_w = kv[:, 5 * KV_W:6 * KV_W]
    kvc_ref = outs[3]
    kvc_ref[...] = kv[:, 0:2 * KV_W]
    if transposed:
        qT_ref, qrT_ref, gT_ref, _, kvcT_ref, kvsT_ref, kvwT_ref, ksb_ref, kwb_ref, vsT_ref, vwT_ref = outs
        qT = _dot_nt(wq_ref[...], xb)
        cosT = rest[0][...]
        sinT = rest[1][...]
        half = HEAD_DIM // 2
        rotated = []
        for h in range(N_HEADS):
            x1 = qT[h * HEAD_DIM:h * HEAD_DIM + half, :]
            x2 = qT[h * HEAD_DIM + half:(h + 1) * HEAD_DIM, :]
            rotated += [x1 * cosT - x2 * sinT, x2 * cosT + x1 * sinT]
        qT_ref[...] = (qT * (scale * LOG2E)).astype(BF16)
        qrT_ref[...] = (jnp.concatenate(rotated, axis=0) * (scale * LOG2E)).astype(BF16)
        gT_ref[...] = gates.T
        tm = x_ref.shape[0]
        rowi = lax.broadcasted_iota(jnp.int32, (tm, LANES), 0)
        lanei = lax.broadcasted_iota(jnp.int32, (tm, LANES), 1)
        blk = lax.shift_right_logical(rowi & (key_chunk - 1), SEL_BLOCK.bit_length() - 1)
        extra = lanei - HEAD_DIM
        nblk = key_chunk // SEL_BLOCK
        aug = jnp.where((extra == blk) | ((extra >= nblk) & (extra < nblk + N_CONST_LANES)), 1.0, 0.0)
        for c in range(KV_W // LANES):
            for k_nat, k_ref in ((k_s, ksb_ref), (k_w, kwb_ref)):
                pair = k_nat[:, c * LANES:(c + 1) * LANES]
                k_ref[2 * c] = jnp.where(lanei < HEAD_DIM, pair, aug).astype(BF16)
                k_ref[2 * c + 1] = jnp.where(lanei < HEAD_DIM, pltpu.roll(pair, HEAD_DIM, 1), aug).astype(BF16)
        v_sT = v_s.T
        v_wT = v_w.T
        ones_rows = jnp.where(lax.broadcasted_iota(jnp.int32, (V_ROWS - HEAD_DIM, tm), 0) == 0, 1.0, 0.0)
        for h in range(N_KV_HEADS):
            for vT, v_ref in ((v_sT, vsT_ref), (v_wT, vwT_ref)):
                v_ref[h * V_ROWS:(h + 1) * V_ROWS, :] = jnp.concatenate(
                    [vT[h * HEAD_DIM:(h + 1) * HEAD_DIM, :], ones_rows], axis=0).astype(BF16)
        kvcT_ref[0] = kv[:, 0:2 * KV_W].T
        kvsT_ref[0, 0:KV_W, :] = k_s.T
        kvsT_ref[0, KV_W:2 * KV_W, :] = v_sT
        kvwT_ref[0, 0:KV_W, :] = k_w.T
        kvwT_ref[0, KV_W:2 * KV_W, :] = v_wT
    else:
        q_ref, qr_ref, gt_ref, _, kvs_ref, kvw_ref = outs
        q = _dot(xb, wq_ref[...])
        qr = _rope_nat(q, cos, sin)
        q_ref[...] = (q * scale).astype(BF16)
        qr_ref[...] = (qr * scale).astype(BF16)
        gt_ref[...] = gates
        kvs_ref[:, 0:KV_W] = k_s
        kvs_ref[:, KV_W:2 * KV_W] = v_s
        kvw_ref[:, 0:KV_W] = k_w
        kvw_ref[:, KV_W:2 * KV_W] = v_w


def _inproj(x, g, w_q, w_kv, w_gate, cos_t, sin_t, *, tm, pos_blocks, transposed, key_chunk=LANES):
    rows, d = x.shape
    assert tm % key_chunk == 0 or not transposed
    n = rows // tm
    row_blk = lambda w: pl.BlockSpec((tm, w), lambda i: (i, 0))
    col_blk = lambda h: pl.BlockSpec((h, tm), lambda i: (0, i))
    tab = pl.BlockSpec((tm, LANES), lambda i: (i % pos_blocks, 0))
    kv_nat = jax.ShapeDtypeStruct((rows, 2 * KV_W), F32)
    if transposed:
        seqs = rows // (pos_blocks * tm)
        kh = pl.BlockSpec((N_KV_HEADS, tm, LANES), lambda i: (0, i, 0))
        kvT = pl.BlockSpec((1, 2 * KV_W, tm), lambda i: (i // pos_blocks, 0, i % pos_blocks))
        out_shape = ([jax.ShapeDtypeStruct((Q_W, rows), BF16)] * 2 + [jax.ShapeDtypeStruct((LANES, rows), F32)]
                     + [kv_nat] + [jax.ShapeDtypeStruct((seqs, 2 * KV_W, pos_blocks * tm), F32)] * 3
                     + [jax.ShapeDtypeStruct((N_KV_HEADS, rows, LANES), BF16)] * 2
                     + [jax.ShapeDtypeStruct((N_KV_HEADS * V_ROWS, rows), BF16)] * 2)
        out_specs = ([col_blk(Q_W)] * 2 + [col_blk(LANES)] + [row_blk(2 * KV_W)] + [kvT] * 3 + [kh] * 2
                     + [col_blk(N_KV_HEADS * V_ROWS)] * 2)
    else:
        out_shape = ([jax.ShapeDtypeStruct((rows, Q_W), BF16)] * 2 + [jax.ShapeDtypeStruct((rows, LANES), F32)]
                     + [kv_nat] * 3)
        out_specs = [row_blk(Q_W)] * 2 + [row_blk(LANES)] + [row_blk(2 * KV_W)] * 3
    tables = [cos_t, sin_t]
    table_specs = [tab, tab]
    if transposed:
        half = HEAD_DIM // 2
        tables += [cos_t[:, 0:half].T, sin_t[:, half:HEAD_DIM].T]
        table_specs += [pl.BlockSpec((half, tm), lambda i: (0, i % pos_blocks))] * 2
    return pl.pallas_call(
        functools.partial(_inproj_kernel, transposed=transposed, key_chunk=key_chunk),
        grid=(n,),
        in_specs=[row_blk(d), _full((1, d)), _full(w_q.shape), _full(w_kv.shape), _full(w_gate.shape)] + table_specs,
        out_specs=out_specs,
        out_shape=out_shape,
        compiler_params=_params("parallel"),
        name="nsa_inproj",
    )(x, g, w_q, w_kv, w_gate, *tables)


_KV_CHUNKS = 2 * KV_W // LANES


def _compress_half(load_rows, kv, nrows, wbd_ref, pe_ref):
    ratio = CMP_BLOCK // CMP_STRIDE
    accs = [jnp.broadcast_to(pe_ref[kv * ratio + r:kv * ratio + r + 1, :], (nrows, KV_W)) for r in range(ratio)]
    for s in range(CMP_STRIDE):
        lhs = load_rows(s, kv).astype(BF16)
        for r in range(ratio):
            accs[r] = accs[r] + _dot(lhs, wbd_ref[(kv * ratio + r) * CMP_STRIDE + s])
    return accs


def _compress_prompt_kernel(*refs, nsb):
    x_refs = refs[:_KV_CHUNKS]
    wbd_ref, pe_ref, w2_ref, kc_ref, vcT_ref, sh_ref = refs[_KV_CHUNKS:]

    def load_rows(s, kv):
        per_half = _KV_CHUNKS // 2
        return jnp.concatenate([x_refs[kv * per_half + c][0, pl.ds(s, nsb, stride=CMP_STRIDE), :]
                                for c in range(per_half)], axis=1)

    sh_ref[nsb:nsb + SUBLANES, :] = jnp.zeros((SUBLANES, KV_W), F32)
    for kv in range(2):
        pr0, pr1 = _compress_half(load_rows, kv, nsb, wbd_ref, pe_ref)
        sh_ref[0:nsb, :] = pr1
        h = pr0 + sh_ref[pl.ds(1, nsb), :]
        out = _dot(jax.nn.gelu(h).astype(BF16), w2_ref[kv])
        if kv == 0:
            for hh in range(N_KV_HEADS):
                kc_ref[0, hh] = out[:, hh * HEAD_DIM:(hh + 1) * HEAD_DIM].astype(BF16)
        else:
            vcT_ref[0] = out.T.astype(BF16)


def _compress_prompt(kvc3, wbd, pe_t, w2bd):
    b, t, _ = kvc3.shape
    nsb = t // CMP_STRIDE
    return pl.pallas_call(
        functools.partial(_compress_prompt_kernel, nsb=nsb),
        grid=(b,),
        in_specs=[pl.BlockSpec((1, t, LANES), lambda i, c=c: (i, 0, c)) for c in range(_KV_CHUNKS)]
        + [_full(wbd.shape), _full(pe_t.shape), _full(w2bd.shape)],
        out_specs=[pl.BlockSpec((1, N_KV_HEADS, nsb, HEAD_DIM), lambda i: (i, 0, 0, 0)),
                   pl.BlockSpec((1, KV_W, nsb), lambda i: (i, 0, 0))],
        out_shape=[jax.ShapeDtypeStruct((b, N_KV_HEADS, nsb, HEAD_DIM), BF16),
                   jax.ShapeDtypeStruct((b, KV_W, nsb), BF16)],
        scratch_shapes=[pltpu.VMEM((nsb + SUBLANES, KV_W), F32)],
        compiler_params=_params("parallel"),
        name="nsa_compress_prompt",
    )(*([kvc3] * _KV_CHUNKS), wbd, pe_t, w2bd)


def _topk_mask_T(imp, n_top):
    ns, w = imp.shape
    nblk = ns // SUBLANES
    blocks = [imp[r * SUBLANES:(r + 1) * SUBLANES, :] for r in range(nblk)]
    cnts = [jnp.zeros((SUBLANES, w), F32) for _ in range(nblk)]
    sub = lax.broadcasted_iota(jnp.int32, (SUBLANES, w), 0)
    for sp in range(ns):
        row = blocks[sp // SUBLANES][sp % SUBLANES:sp % SUBLANES + 1, :]
        for r in range(nblk):
            blk = blocks[r]
            if sp < r * SUBLANES:
                beats = jnp.where(row >= blk, 1.0, 0.0)
            elif sp >= (r + 1) * SUBLANES:
                beats = jnp.where(row > blk, 1.0, 0.0)
            else:
                beats = jnp.where(sub > (sp - r * SUBLANES), jnp.where(row >= blk, 1.0, 0.0),
                                  jnp.where(row > blk, 1.0, 0.0))
            cnts[r] = cnts[r] + beats
    return jnp.concatenate([jnp.where(c < n_top, 1.0, 0.0) for c in cnts], axis=0)


def _online_chunks(states, k_cs, vT_cs, qTs, bias):
    scores = [_dot(k_c, qT) for k_c, qT in zip(k_cs, qTs)]
    mids = []
    for (m, _), s in zip(states, scores):
        if bias is not None:
            s = s + bias
        m_new = jnp.maximum(m, jnp.max(s, axis=0, keepdims=True))
        mids.append((m_new, jnp.exp2(m - m_new), jnp.exp2(s - m_new).astype(BF16)))
    return tuple((m_new, alpha * acc + _dot(vT_c, p))
                 for (m_new, alpha, p), (_, acc), vT_c in zip(mids, states, vT_cs))


def _softmax_finish(carry):
    _, acc = carry
    return acc[0:HEAD_DIM, :] * (1.0 / jnp.maximum(acc[HEAD_DIM:HEAD_DIM + 1, :], 1e-20))


LAZY_LOG2_MAX = 60.0
N_CONST_LANES = 4


def _split3(x):
    hi = x.astype(BF16).astype(F32)
    r = x - hi
    mid = r.astype(BF16).astype(F32)
    return hi, mid, (r - mid).astype(BF16).astype(F32)


def _rescaling_chunks(m_ref, acc_ref, k_cs, vT_cs, qTs, base_tiles, bias):
    nh = len(k_cs)
    rows, w = base_tiles[0].shape
    zero_rows = jnp.zeros((LANES - HEAD_DIM - rows, w), BF16)
    new = _online_chunks(tuple((m_ref[h], acc_ref[h]) for h in range(nh)), k_cs, vT_cs,
                         [jnp.concatenate([qTs[h], base_tiles[h].astype(BF16), zero_rows], axis=0) for h in range(nh)],
                         bias)
    for h in range(nh):
        m_ref[h] = new[h][0]
        acc_ref[h] = new[h][1]


def _lazy_chunks(m_ref, acc2_ref, slot_ref, k_cs, vT_cs, qTs, base_tiles, bias, ref_row, n_chunks=1, first=False):
    slot = slot_ref[0]
    acc_ref = acc2_ref.at[slot]
    nu = len(k_cs)
    nh = nu // n_chunks
    biases = list(bias) if isinstance(bias, (list, tuple)) else [bias] * n_chunks
    rows = base_tiles[0].shape[0]
    w = base_tiles[0].shape[1]
    rowt = lax.broadcasted_iota(jnp.int32, (rows, w), 0)
    zero_rows = jnp.zeros((LANES - HEAD_DIM - rows, w), BF16)

    def queries(u):
        hi, mid, lo = _split3(-m_ref[u % nh])
        tile = jnp.where(rowt == ref_row, hi, jnp.where(rowt == ref_row + 1, mid,
                                                        jnp.where(rowt == ref_row + 2, lo, base_tiles[u])))
        return jnp.concatenate([qTs[u % nh], tile.astype(BF16), zero_rows], axis=0)

    scores = [_dot(k_cs[u], queries(u)) for u in range(nu)]
    peaks = [None] * nh
    probs = []
    for u, s in enumerate(scores):
        if biases[u // nh] is not None:
            s = s + biases[u // nh]
        top = jnp.max(s, axis=0, keepdims=True)
        peaks[u % nh] = top if peaks[u % nh] is None else jnp.maximum(peaks[u % nh], top)
        probs.append(jnp.exp2(s).astype(BF16))
    pvs = [_dot(vT_cs[u], probs[u]) for u in range(nu)]
    for h in range(nh):
        total = pvs[h]
        for c in range(1, n_chunks):
            total = total + pvs[c * nh + h]
        acc2_ref[1 - slot, h] = acc_ref[h] + total
    highest = functools.reduce(jnp.maximum, peaks)
    lowest = functools.reduce(jnp.minimum, peaks)
    in_range = (jnp.max(highest) <= LAZY_LOG2_MAX) & (jnp.logical_not(first) | (jnp.min(lowest) >= -LAZY_LOG2_MAX))

    @pl.when(in_range)
    def _():
        slot_ref[0] = 1 - slot

    @pl.when(jnp.logical_not(in_range))
    def _():
        for h in range(nh):
            m_ref[h] = jnp.where(first, NEG, m_ref[h])
        for c in range(n_chunks):
            part = slice(c * nh, (c + 1) * nh)
            _rescaling_chunks(m_ref, acc_ref, k_cs[part], vT_cs[part], qTs, base_tiles[part], biases[c])


def _attn_prompt_kernel(qT_ref, qrT_ref, gT_ref, kc_ref, vcT_ref, ks_ref, vsT_ref, kw_ref, vwT_ref, ovT_ref,
                        o_ref, selb_ref, oc_ref, ow_ref, m_ref, acc2_ref, slot_ref, *, tq, nc, n_top):
    ck = tq
    i = pl.program_id(1)
    t0 = i * tq
    qpos = t0 + lax.broadcasted_iota(jnp.int32, (1, tq), 1)
    ncp = kc_ref.shape[2]
    ns = ovT_ref.shape[0]
    w = GROUP * tq
    bpc = ck // SEL_BLOCK
    sel_shift = SEL_BLOCK.bit_length() - 1
    kvhs = range(N_KV_HEADS)
    heads = [[kvh * GROUP + g for g in range(GROUP)] for kvh in kvhs]
    rows = [pl.ds(kvh * HEAD_DIM, HEAD_DIM) for kvh in kvhs]
    vrows = [pl.ds(kvh * V_ROWS, V_ROWS) for kvh in kvhs]
    qrT = [jnp.concatenate([qrT_ref[h * HEAD_DIM:(h + 1) * HEAD_DIM, :] for h in heads[kvh]], axis=1)
           for kvh in kvhs]
    kl = lax.broadcasted_iota(jnp.int32, (ck, tq), 0)
    ql = lax.broadcasted_iota(jnp.int32, (ck, tq), 1)
    tile4 = lambda b: jnp.concatenate([b] * GROUP, axis=1)
    key_le_query = tile4(jnp.where(kl <= ql, 0.0, MASKED))
    key_ge_query = tile4(jnp.where(kl >= ql, 0.0, MASKED))
    bias_rows = selb_ref.shape[2]

    def reset_softmax():
        slot_ref[0] = 0
        for kvh in kvhs:
            m_ref[kvh] = jnp.zeros((1, w), F32)
            acc2_ref[0, kvh] = jnp.zeros((V_ROWS, w), F32)

    def current_acc(kvh):
        return acc2_ref[slot_ref[0], kvh]

    cidx = lax.broadcasted_iota(jnp.int32, (ncp, 1), 0)
    valid = ((cidx * CMP_STRIDE + (CMP_BLOCK - 1)) <= qpos) & (cidx < nc)
    sidx = lax.broadcasted_iota(jnp.int32, (ns, 1), 0)
    cur = lax.shift_right_logical(qpos, sel_shift)
    forced = (sidx == 0) | (sidx == cur) | (sidx == cur - 1)
    causal = (sidx * SEL_BLOCK) <= qpos
    cmp_scores = [
        _dot(kc_ref[0, kvh], jnp.concatenate([qT_ref[h * HEAD_DIM:(h + 1) * HEAD_DIM, :] for h in heads[kvh]], axis=1))
        for kvh in kvhs]
    importance = []
    for kvh in kvhs:
        s = cmp_scores[kvh]
        probs = []
        for g in range(GROUP):
            sm = jnp.where(valid, s[:, g * tq:(g + 1) * tq], NEG)
            mx = jnp.max(sm, axis=0, keepdims=True)
            e = jnp.where(valid, jnp.exp2(sm - mx), 0.0)
            den = jnp.maximum(jnp.sum(e, axis=0, keepdims=True), 1e-20)
            probs.append(e / den)
        oc_ref[kvh] = _dot(vcT_ref[0, rows[kvh], :], jnp.concatenate(probs, axis=1).astype(BF16))
        psum = probs[0]
        for g in range(1, GROUP):
            psum = psum + probs[g]
        imp = _dot_f32lhs(ovT_ref[...], psum)
        importance.append(jnp.where(forced, FORCE, jnp.where(causal, imp, NEG)))

    prefixes = sorted({r for r in (ns // 4, ns // 2, 3 * ns // 4) if r and r % SUBLANES == 0} | {ns})
    needed = (i + 1) * bpc
    fill = jnp.zeros((bias_rows - bpc, tq), F32)
    for lo, hi in zip([0] + prefixes[:-1], prefixes):
        @pl.when((needed > lo) & (needed <= hi))
        def _(hi=hi):
            for kvh in kvhs:
                selb = (_topk_mask_T(importance[kvh][0:hi, :], min(n_top, hi)) - 1.0) * (-MASKED)
                for c in range(hi // bpc):
                    selb_ref[kvh, c] = jnp.concatenate([selb[c * bpc:(c + 1) * bpc, :], fill], axis=0)

    n_back = WINDOW // ck
    reset_softmax()
    rowb = lax.broadcasted_iota(jnp.int32, (bias_rows, w), 0)
    def window_chunk(r):
        a = i - n_back + r
        kst = pl.multiple_of(jnp.maximum(a, 0) * ck, ck)
        skip = jnp.where(a < 0, MASKED, 0.0)
        tile = jnp.where(rowb == bpc, skip, 0.0)
        bias = key_le_query if r == n_back else (key_ge_query if r == 0 else None)
        return ([kw_ref[kvh, pl.ds(kst, ck), :] for kvh in kvhs],
                [vwT_ref[vrows[kvh], pl.ds(kst, ck)] for kvh in kvhs], [tile] * N_KV_HEADS, bias)

    order = [n_back] + list(range(n_back))
    for g0 in [0] + list(range(1, n_back + 1, 2)):
        group = [window_chunk(r) for r in (order[g0:g0 + 1] if g0 == 0 else order[g0:g0 + 2])]
        _lazy_chunks(m_ref, acc2_ref, slot_ref, sum((g[0] for g in group), []), sum((g[1] for g in group), []),
                     qrT, sum((g[2] for g in group), []), [g[3] for g in group], bpc + 1, len(group), first=g0 == 0)
    for kvh in kvhs:
        ow_ref[kvh] = _softmax_finish((None, current_acc(kvh)))

    def chunk_step(c, diagonal, n_chunks=1, first=False):
        ks, vs, tiles = [], [], []
        for dc in range(n_chunks):
            kst = (c + dc) * ck if isinstance(c, int) else pl.multiple_of((c + dc) * ck, ck)
            ks += [ks_ref[kvh, pl.ds(kst, ck), :] for kvh in kvhs]
            vs += [vsT_ref[vrows[kvh], pl.ds(kst, ck)] for kvh in kvhs]
            tiles += [tile4(selb_ref[kvh, c + dc]) for kvh in kvhs]
        _lazy_chunks(m_ref, acc2_ref, slot_ref, ks, vs, qrT, tiles, key_le_query if diagonal else None, bpc + 1,
                     n_chunks, first)

    reset_softmax()

    @pl.when(i > 0)
    def _():
        chunk_step(0, False, first=True)

    n_mid = jnp.maximum(i - 1, 0)

    def loop_body(p, carry):
        chunk_step(1 + 2 * p, False, n_chunks=2)
        return carry

    lax.fori_loop(0, n_mid // 2, loop_body, 0)

    @pl.when(n_mid % 2 == 1)
    def _():
        chunk_step(i - 1, False)

    chunk_step(i, True, first=i == 0)

    for kvh in kvhs:
        def gate_row(j, kvh=kvh):
            return jnp.concatenate([gT_ref[h * 3 + j:h * 3 + j + 1, :] for h in heads[kvh]], axis=1)

        oT = (gate_row(0) * oc_ref[kvh] + gate_row(1) * _softmax_finish((None, current_acc(kvh)))
              + gate_row(2) * ow_ref[kvh])
        for g, h in enumerate(heads[kvh]):
            o_ref[h * HEAD_DIM:(h + 1) * HEAD_DIM, :] = oT[:, g * tq:(g + 1) * tq].astype(BF16)


def _attn_prompt(qT, qrT, gT, kc, vcT, ksb, vsT, kwb, vwT, ovT, *, batch, seq, tq, nc, n_top):
    nq = seq // tq
    nsb = kc.shape[2]
    ns = ovT.shape[0]
    col = lambda h: pl.BlockSpec((h, tq), lambda b, i: (0, b * nq + i))
    kh = pl.BlockSpec((N_KV_HEADS, seq, LANES), lambda b, i: (0, b, 0))
    vt = pl.BlockSpec((N_KV_HEADS * V_ROWS, seq), lambda b, i: (0, b))
    bf16_sublanes = 2 * SUBLANES
    return pl.pallas_call(
        functools.partial(_attn_prompt_kernel, tq=tq, nc=nc, n_top=n_top),
        grid=(batch, nq),
        in_specs=[col(Q_W), col(Q_W), col(LANES),
                  pl.BlockSpec((1, N_KV_HEADS, nsb, HEAD_DIM), lambda b, i: (b, 0, 0, 0)),
                  pl.BlockSpec((1, KV_W, nsb), lambda b, i: (b, 0, 0)),
                  kh, vt, kh, vt, _full(ovT.shape)],
        out_specs=col(Q_W),
        out_shape=jax.ShapeDtypeStruct((Q_W, batch * seq), BF16),
        scratch_shapes=[pltpu.VMEM((N_KV_HEADS, ns * SEL_BLOCK // tq, bf16_sublanes, tq), F32),
                        pltpu.VMEM((N_KV_HEADS, HEAD_DIM, GROUP * tq), F32),
                        pltpu.VMEM((N_KV_HEADS, HEAD_DIM, GROUP * tq), F32),
                        pltpu.VMEM((N_KV_HEADS, 1, GROUP * tq), F32),
                        pltpu.VMEM((2, N_KV_HEADS, V_ROWS, GROUP * tq), F32),
                        pltpu.SMEM((1,), jnp.int32)],
        compiler_params=_params("parallel", "arbitrary"),
        name="nsa_attn_prompt",
    )(qT, qrT, gT, kc, vcT, ksb, vsT, kwb, vwT, ovT)


def _resident(shape):
    zeros = (0,) * len(shape)
    return pl.BlockSpec(shape, lambda *_: zeros, pipeline_mode=pl.Buffered(1))


def _layer_tail_kernel(*refs, mixer, transposed, ff_chunk, next_norm):
    if mixer == "nsa":
        o_ref, wo_ref, x_ref, g1_ref, g2_ref = refs[:5]
        rest = refs[5:]
        if transposed:
            y = lax.dot_general(o_ref[...], wo_ref[...], (((0,), (0,)), ((), ())), preferred_element_type=F32)
        else:
            y = _dot(o_ref[...], wo_ref[...])
    else:
        yc_ref, u_ref, d_ref, wg_ref, bg_ref, x_ref, g1_ref, g2_ref = refs[:8]
        rest = refs[8:]
        z = jax.nn.gelu(yc_ref[...] + d_ref[...] * u_ref[...])
        y = z * jax.nn.sigmoid(_dot(z.astype(BF16), wg_ref[...]) + bg_ref[...])
    w1_ref, w2_ref, g3_ref = rest[:3]
    if next_norm:
        gn_ref, h2_ref, xn_ref, acc_ref = rest[3:]
    else:
        h2_ref, acc_ref = rest[3:]
    h = x_ref[...] + _rms(y, g1_ref[...])
    xm = _rms(h, g2_ref[...]).astype(BF16)
    for c in range(w1_ref.shape[1] // ff_chunk):
        cols = slice(c * ff_chunk, (c + 1) * ff_chunk)
        hm = jnp.maximum(_dot(xm, w1_ref[:, cols]), 0.0)
        part = _dot((hm * hm).astype(BF16), w2_ref[cols, :])
        if c == 0:
            acc_ref[...] = part
        else:
            acc_ref[...] += part
    h2 = h + _rms(acc_ref[...], g3_ref[...])
    h2_ref[...] = h2
    if next_norm:
        xn_ref[...] = _rms(h2, gn_ref[...])


def _layer_tail(mixer_args, x, g1, g2, w1, w2, g3, gn, *, mixer, tm, ff_chunk, transposed=False):
    rows, d = x.shape
    row = pl.BlockSpec((tm, d), lambda i: (i, 0))
    vec = _resident((1, d))
    if mixer == "nsa":
        o, w_o = mixer_args
        o_spec = (pl.BlockSpec((Q_W, tm), lambda i: (0, i)) if transposed
                  else pl.BlockSpec((tm, Q_W), lambda i: (i, 0)))
        head_specs = [o_spec, _resident(w_o.shape)]
    else:
        w_glu = mixer_args[3]
        head_specs = [row, row, vec, _resident(w_glu.shape), vec]
    tail_args = (w1, w2, g3) + (() if gn is None else (gn,))
    tail_specs = [_resident(w1.shape), _resident(w2.shape), vec] + ([] if gn is None else [vec])
    n_out = 1 if gn is None else 2
    return pl.pallas_call(
        functools.partial(_layer_tail_kernel, mixer=mixer, transposed=transposed, ff_chunk=ff_chunk,
                          next_norm=gn is not None),
        grid=(rows // tm,),
        in_specs=head_specs + [row, vec, vec] + tail_specs,
        out_specs=[row] * n_out,
        out_shape=[jax.ShapeDtypeStruct((rows, d), F32)] * n_out,
        scratch_shapes=[pltpu.VMEM((tm, d), F32)],
        compiler_params=_params("parallel"),
        name=mixer + "_layer_tail",
    )(*mixer_args, x, g1, g2, *tail_args)


def _s5_seq_kernel(x_ref, brow_ref, pre_ref, pim_ref, qre_ref, qim_ref, are_ref, aim_ref, y_ref, hfin_ref,
                   sre, sim, hre, him, cre, cim):
    nb, tb, uw = x_ref.shape
    nt = pre_ref.shape[1] // uw
    nk = tb // nt

    @pl.when(pl.program_id(1) == 0)
    def _():
        cre[...] = jnp.zeros(cre.shape, F32)
        cim[...] = jnp.zeros(cim.shape, F32)

    u = [jnp.concatenate([x_ref[b, pl.ds(t, nk, stride=nt), :] for b in range(nb)], axis=0).astype(BF16)
         for t in range(nt)]
    u2 = [jnp.concatenate([u[2 * j], u[2 * j + 1]], axis=1) for j in range(nt // 2)]
    s_re = _dot(u2[0], pre_ref[0, 0:2 * uw, :])
    s_im = _dot(u2[0], pim_ref[0, 0:2 * uw, :])
    for j in range(1, nt // 2):
        s_re = s_re + _dot(u2[j], pre_ref[0, 2 * j * uw:(2 * j + 2) * uw, :])
        s_im = s_im + _dot(u2[j], pim_ref[0, 2 * j * uw:(2 * j + 2) * uw, :])
    sre[...] = s_re
    sim[...] = s_im
    ar = are_ref[0]
    ai = aim_ref[0]

    def body(it, carry):
        out = []
        for b in range(nb):
            hr, hi = carry[b]
            r0 = pl.multiple_of(b * nk + it * SUBLANES, SUBLANES)
            sr8 = sre[pl.ds(r0, SUBLANES), :]
            si8 = sim[pl.ds(r0, SUBLANES), :]
            prev_r, prev_i = [], []
            for j in range(SUBLANES):
                prev_r.append(hr)
                prev_i.append(hi)
                hr, hi = (ar * hr - ai * hi + sr8[j:j + 1, :], ar * hi + ai * hr + si8[j:j + 1, :])
            hre[pl.ds(r0, SUBLANES), :] = jnp.concatenate(prev_r, axis=0)
            him[pl.ds(r0, SUBLANES), :] = jnp.concatenate(prev_i, axis=0)
            out.append((hr, hi))
        return tuple(out)

    init = tuple((cre[b:b + 1, :], cim[b:b + 1, :]) for b in range(nb))
    fin = lax.fori_loop(0, nk // SUBLANES, body, init)
    for b in range(nb):
        cre[b:b + 1, :] = fin[b][0]
        cim[b:b + 1, :] = fin[b][1]
    hfin_ref[0, 0] = cre[...]
    hfin_ref[0, 1] = cim[...]

    hb_re = hre[...].astype(BF16)
    hb_im = him[...].astype(BF16)
    for t2 in range(0, nt, 2):
        cols = slice(t2 * uw, (t2 + 2) * uw)
        acc = _dot(hb_re, qre_ref[0, :, cols]) + _dot(hb_im, qim_ref[0, :, cols])
        for j in range(t2 // 2 + 1):
            lag0 = t2 - 2 * j + 1
            wpair = jnp.concatenate([brow_ref[0, :, lag0 * uw:(lag0 + 2) * uw],
                                     brow_ref[0, :, (lag0 - 1) * uw:(lag0 + 1) * uw]], axis=0)
            acc = acc + _dot(u2[j], wpair)
        for b in range(nb):
            for dt in range(2):
                y_ref[b, pl.ds(t2 + dt, nk, stride=nt), :] = acc[b * nk:(b + 1) * nk, dt * uw:(dt + 1) * uw]


def _s5_seq(x3, ops):
    nb, t, d = x3.shape
    uw = SSM_UNIT * SSM_GROUP
    nunit = d // uw
    sw = ops["p_re"].shape[2]
    tb = _tile(t, S5_TIME_BLOCK)
    nk = tb // SSM_CHUNK
    per_unit = lambda a: pl.BlockSpec((1,) + a.shape[1:], lambda i, r: (i,) + (0,) * (a.ndim - 1))
    blk = pl.BlockSpec((nb, tb, uw), lambda i, r: (0, r, i))
    args = (x3, ops["brow"], ops["p_re"], ops["p_im"], ops["q_re"], ops["q_im"], ops["a_chunk_re"], ops["a_chunk_im"])
    return pl.pallas_call(
        _s5_seq_kernel,
        grid=(nunit, t // tb),
        in_specs=[blk] + [per_unit(a) for a in args[1:]],
        out_specs=[blk, pl.BlockSpec((1, 2, nb, sw), lambda i, r: (i, 0, 0, 0))],
        out_shape=[jax.ShapeDtypeStruct((nb, t, d), F32), jax.ShapeDtypeStruct((nunit, 2, nb, sw), F32)],
        scratch_shapes=[pltpu.VMEM((nb * nk, sw), F32)] * 4 + [pltpu.VMEM((nb, sw), F32)] * 2,
        compiler_params=_params("parallel", "arbitrary"),
        name="s5_seq_scan",
    )(*args)


def _s5_step_kernel(u_ref, h0r_ref, h0i_ref, bre_ref, bim_ref, cre_ref, cim_ref, are_ref, aim_ref,
                    y_ref, hr_ref, hi_ref, *, npair):
    for p in range(npair):
        u = u_ref[p]
        ar = are_ref[p]
        ai = aim_ref[p]
        h0r = h0r_ref[p]
        h0i = h0i_ref[p]
        hr = ar * h0r - ai * h0i + _dot(u, bre_ref[p])
        hi = ar * h0i + ai * h0r + _dot(u, bim_ref[p])
        hr_ref[p] = hr
        hi_ref[p] = hi
        y_ref[p] = _dot(hr.astype(BF16), cre_ref[p]) + _dot(hi.astype(BF16), cim_ref[p])


def _s5_step(u2, h0r, h0i, ops):
    npair, rows, width = u2.shape
    args = (u2, h0r, h0i, ops["b1_re"], ops["b1_im"], ops["c1_re"], ops["c1_im"], ops["a1_re"], ops["a1_im"])
    return pl.pallas_call(
        functools.partial(_s5_step_kernel, npair=npair),
        grid=(1,),
        in_specs=[_full(a.shape) for a in args],
        out_specs=[_full((npair, rows, width)), _full(h0r.shape), _full(h0r.shape)],
        out_shape=[jax.ShapeDtypeStruct((npair, rows, width), F32), jax.ShapeDtypeStruct(h0r.shape, F32),
                   jax.ShapeDtypeStruct(h0r.shape, F32)],
        compiler_params=_params("arbitrary"),
        name="s5_single_step",
    )(*args)


def _s5_operators(a_re, a_im, log_dt, b_re, b_im, c_re, c_im):
    hp = lax.Precision.HIGHEST
    g, n = a_re.shape
    gu = SSM_UNIT
    nunit = g // gu
    L = SSM_CHUNK
    uw = gu * SSM_GROUP
    sw = gu * n
    a = lax.complex(a_re.astype(F32), a_im.astype(F32))
    dt = jnp.exp(log_dt.astype(F32))[:, None]
    a_bar = jnp.exp(a * dt)
    b_bar = ((a_bar - 1.0) / a)[:, :, None] * lax.complex(b_re.astype(F32), b_im.astype(F32))
    c = lax.complex(c_re.astype(F32), c_im.astype(F32))
    pows = [jnp.ones_like(a_bar)]
    for _ in range(L):
        pows.append(pows[-1] * a_bar)
    a_pow = jnp.stack(pows).reshape(L + 1, nunit, sw)
    apr = jnp.real(a_pow)
    api = jnp.imag(a_pow)
    eye = jnp.eye(gu, dtype=F32)

    def bd_in(x):
        return jnp.einsum("pgnd,gh->pgdhn", x.reshape(nunit, gu, n, SSM_GROUP), eye).reshape(nunit, uw, sw)

    def bd_out(x):
        return jnp.einsum("pgcn,gh->pgnhc", x.reshape(nunit, gu, SSM_GROUP, n), eye).reshape(nunit, sw, uw)

    p0r, p0i = bd_in(jnp.real(b_bar)), bd_in(jnp.imag(b_bar))
    q0r, q0i = bd_out(jnp.real(c)), bd_out(jnp.imag(c))
    lanes = lambda x, k: x[k][:, None, :]
    rows_ = lambda x, k: x[k][:, :, None]
    p_re = jnp.concatenate([p0r * lanes(apr, L - 1 - t) - p0i * lanes(api, L - 1 - t) for t in range(L)], axis=1)
    p_im = jnp.concatenate([p0r * lanes(api, L - 1 - t) + p0i * lanes(apr, L - 1 - t) for t in range(L)], axis=1)
    q_re = jnp.concatenate([q0r * rows_(apr, t + 1) - q0i * rows_(api, t + 1) for t in range(L)], axis=2)
    q_im = jnp.concatenate([-(q0r * rows_(api, t + 1) + q0i * rows_(apr, t + 1)) for t in range(L)], axis=2)
    lag_blocks = [jnp.zeros((nunit, uw, uw), F32)]
    for t in range(L):
        xr = p0r * lanes(apr, t) - p0i * lanes(api, t)
        xi = p0r * lanes(api, t) + p0i * lanes(apr, t)
        lag_blocks.append(jnp.einsum("pus,psv->puv", xr, q0r, precision=hp)
                          - jnp.einsum("pus,psv->puv", xi, q0i, precision=hp))
    brow = jnp.concatenate(lag_blocks, axis=2)
    lane_row = lambda x: x.reshape(nunit, 1, sw)
    return {
        "brow": brow.astype(BF16),
        "p_re": p_re.astype(BF16), "p_im": p_im.astype(BF16),
        "q_re": q_re.astype(BF16), "q_im": q_im.astype(BF16),
        "a_chunk_re": lane_row(apr[L]), "a_chunk_im": lane_row(api[L]),
        "a1_re": lane_row(apr[1]), "a1_im": lane_row(api[1]),
        "b1_re": p0r.astype(BF16), "b1_im": p0i.astype(BF16),
        "c1_re": q0r.astype(BF16), "c1_im": (-q0i).astype(BF16),
    }


def _compress_paged_kernel(pt_ref, *refs, pg, ns, nsb):
    del pt_ref
    pages = refs[:ns * pg]
    perm_ref, wbd_ref, pe_ref, w2_ref, out_ref, h0_ref, h1_ref = refs[ns * pg:]
    j = pl.program_id(1)
    sbp = PAGE_SIZE // CMP_STRIDE
    nrows = pg * sbp
    pair_rows = 2 * sbp

    @pl.when(j == 0)
    def _():
        h1_ref[:, :, nsb:nsb + SUBLANES, :] = jnp.zeros((ns, 2, SUBLANES, KV_W), F32)

    r0 = pl.multiple_of(j * nrows, nrows)
    for kv in range(2):
        staged = []
        for q in range(ns * pg // 2):
            z = jnp.concatenate([pages[2 * q][0, kv], pages[2 * q + 1][0, kv]], axis=1).astype(BF16)
            staged.append(_dot_nt(perm_ref[...], z).astype(BF16))

        def load_rows(s, kv, staged=staged):
            return jnp.concatenate([x[s * pair_rows:(s + 1) * pair_rows, :] for x in staged], axis=0)

        pr0, pr1 = _compress_half(load_rows, kv, ns * nrows, wbd_ref, pe_ref)
        for u in range(ns):
            h0_ref[u, kv, pl.ds(r0, nrows), :] = pr0[u * nrows:(u + 1) * nrows, :]
            h1_ref[u, kv, pl.ds(r0, nrows), :] = pr1[u * nrows:(u + 1) * nrows, :]

    @pl.when(j == pl.num_programs(1) - 1)
    def _():
        for u in range(ns):
            for kv in range(2):
                h = h0_ref[u, kv] + h1_ref[u, kv, pl.ds(1, nsb), :]
                out = _dot(jax.nn.gelu(h).astype(BF16), w2_ref[kv])
                out_ref[u, :, kv * KV_W:(kv + 1) * KV_W] = out.astype(BF16)


def _page_specs(pg, ns, pages_per_sample):
    def spec(u, i):
        return pl.BlockSpec((1, 2, KV_W, PAGE_SIZE),
                            lambda b, j, pt: (pt[(b * ns + u) * pages_per_sample + j * pg + i], 0, 0, 0))
    return [spec(u, i) for u in range(ns) for i in range(pg)]


def _compress_paged(pages, pt_flat, wbd, pe_t, w2bd, *, nb, pages_per_sample, pg):
    nsb = pages_per_sample * PAGE_SIZE // CMP_STRIDE
    sbp = PAGE_SIZE // CMP_STRIDE
    ns = SAMPLES_PER_STEP if nb % SAMPLES_PER_STEP == 0 else 1
    i_out = np.arange(2 * PAGE_SIZE)
    s_i, pg_i, n_i = i_out // (2 * sbp), (i_out // sbp) % 2, i_out % sbp
    perm = jnp.asarray(i_out[None, :] == (pg_i * PAGE_SIZE + n_i * CMP_STRIDE + s_i)[:, None], BF16)
    c3 = lambda shape: pl.BlockSpec(shape, lambda b, j, pt: (0,) * len(shape))
    grid_spec = pltpu.PrefetchScalarGridSpec(
        num_scalar_prefetch=1,
        grid=(nb // ns, pages_per_sample // pg),
        in_specs=_page_specs(pg, ns, pages_per_sample) + [c3(perm.shape), c3(wbd.shape), c3(pe_t.shape),
                                                          c3(w2bd.shape)],
        out_specs=pl.BlockSpec((ns, nsb, 2 * KV_W), lambda b, j, pt: (b, 0, 0)),
        scratch_shapes=[pltpu.VMEM((ns, 2, nsb, KV_W), F32), pltpu.VMEM((ns, 2, nsb + SUBLANES, KV_W), F32)],
    )
    return pl.pallas_call(
        functools.partial(_compress_paged_kernel, pg=pg, ns=ns, nsb=nsb),
        grid_spec=grid_spec,
        out_shape=jax.ShapeDtypeStruct((nb, nsb, 2 * KV_W), BF16),
        compiler_params=_params("parallel", "arbitrary"),
        name="nsa_compress_paged",
    )(pt_flat, *([pages] * (ns * pg)), perm, wbd, pe_t, w2bd)


def _topk_mask_lanes(imp, n_top, ns_valid):
    lane = lax.broadcasted_iota(jnp.int32, imp.shape, 1)
    cnt = jnp.zeros(imp.shape, F32)
    for sp in range(ns_valid):
        col = imp[:, sp:sp + 1]
        cnt = cnt + jnp.where(lane > sp, jnp.where(col >= imp, 1.0, 0.0), jnp.where(col > imp, 1.0, 0.0))
    return jnp.where((cnt < n_top) & (lane < ns_valid), 1.0, 0.0)


def _dot_nt(a, b):
    return lax.dot_general(a, b, (((1,), (1,)), ((), ())), preferred_element_type=F32)


def _attn_sample_kernel(pt_ref, *refs, pg, ns, past, nc, ns_valid, n_top):
    del pt_ref
    all_pages = [refs[u * pg:(u + 1) * pg] for u in range(ns)]
    (q_ref, qr_ref, g_ref, kcvc_ref, ksn_ref, kwn_ref, win_ref, ov_ref, e_ref, gs_ref, o_ref,
     m_ref, l_ref, acc_ref, sel_ref, oc_ref, ow_ref) = refs[ns * pg:]
    j = pl.program_id(1)
    ncp = kcvc_ref.shape[1]
    nsp = ov_ref.shape[1]
    wlen = win_ref.shape[3]
    row = lax.broadcasted_iota(jnp.int32, (N_HEADS, KV_W), 0)
    lane = lax.broadcasted_iota(jnp.int32, (N_HEADS, KV_W), 1)
    own = (lane // HEAD_DIM) == (row // GROUP)

    def spread(ref, u):
        q = ref[u]
        return jnp.where(own, jnp.concatenate([q] * N_KV_HEADS, axis=1), jnp.zeros((N_HEADS, KV_W), BF16))

    def update(state, s, vs):
        m, l, acc = state
        m_new = jnp.maximum(m, jnp.max(s, axis=1, keepdims=True))
        alpha = jnp.exp(m - m_new)
        p = jnp.exp(s - m_new)
        l = alpha * l + jnp.sum(p, axis=1, keepdims=True)
        pv = None
        for st, sz, v, feature_major in vs:
            pb = p[:, st:st + sz].astype(BF16)
            t = _dot_nt(pb, v) if feature_major else _dot(pb, v)
            pv = t if pv is None else pv + t
        return m_new, l, alpha * acc + pv

    def init():
        return (jnp.full((N_HEADS, 1), NEG, F32), jnp.zeros((N_HEADS, 1), F32), jnp.zeros((N_HEADS, KV_W), F32))

    def new_row_update(state, qbd, new_row):
        r8 = lax.broadcasted_iota(jnp.int32, (SUBLANES, 2 * KV_W), 0)
        tile = jnp.where(r8 == 0, jnp.broadcast_to(new_row, (SUBLANES, 2 * KV_W)), 0.0).astype(BF16)
        s = _dot_nt(qbd, tile[:, 0:KV_W])
        l8 = lax.broadcasted_iota(jnp.int32, (N_HEADS, SUBLANES), 1)
        s = jnp.where(l8 == 0, s, MASKED)
        return update(state, s, [(0, SUBLANES, tile[:, KV_W:2 * KV_W], False)])

    def put(u, state):
        m, l, acc = state
        m_ref[u] = jnp.broadcast_to(m, m_ref.shape[1:])
        l_ref[u] = jnp.broadcast_to(l, l_ref.shape[1:])
        acc_ref[u] = acc

    qrbd = [spread(qr_ref, u) for u in range(ns)]

    @pl.when(j == 0)
    def _():
        cidx = lax.broadcasted_iota(jnp.int32, (1, ncp), 1)
        valid = ((cidx * CMP_STRIDE + (CMP_BLOCK - 1)) <= past) & (cidx < nc)
        sidx = lax.broadcasted_iota(jnp.int32, (1, nsp), 1)
        cur = past // SEL_BLOCK
        forced = (sidx == 0) | (sidx == cur) | (sidx == cur - 1)
        causal = (sidx * SEL_BLOCK) <= past
        wpos = past - wlen + lax.broadcasted_iota(jnp.int32, (1, wlen), 1)
        in_window = (wpos >= 0) & (past - wpos <= WINDOW)
        for u in range(ns):
            s = _dot_nt(spread(q_ref, u), kcvc_ref[u, :, 0:KV_W])
            sm = jnp.where(valid, s, NEG)
            mx = jnp.max(sm, axis=1, keepdims=True)
            e = jnp.where(valid, jnp.exp(sm - mx), 0.0)
            p = e / jnp.maximum(jnp.sum(e, axis=1, keepdims=True), 1e-20)
            oc_ref[u] = _dot(p.astype(BF16), kcvc_ref[u, :, KV_W:2 * KV_W])
            imp = _dot_f32rhs(_dot_f32lhs(gs_ref[...], p), ov_ref[...])
            imp = jnp.where(forced, FORCE, jnp.where(causal, imp, NEG))
            imp = jnp.where(sidx < ns_valid, imp, MASKED)
            sel_ref[u] = _topk_mask_lanes(imp, n_top, ns_valid)
            sw = jnp.where(in_window, _dot(qrbd[u], win_ref[u, 0].astype(BF16)), MASKED)
            st = update(init(), sw, [(0, wlen, win_ref[u, 1].astype(BF16), True)])
            st = new_row_update(st, qrbd[u], kwn_ref[u])
            ow_ref[u] = st[2] * (1.0 / jnp.maximum(st[1], 1e-20))
            put(u, new_row_update(init(), qrbd[u], ksn_ref[u]))

    def halves(pages):
        return [pages[:pg // 2], pages[pg // 2:]] if pg > 1 else [pages]

    scores = [[jnp.concatenate([_dot(qrbd[u], r[0, 0].astype(BF16)) for r in part], axis=1)
               for part in halves(all_pages[u])] for u in range(ns)]
    masks = [_dot(sel_ref[u].astype(BF16), e_ref[0]) > 0.5 for u in range(ns)]
    states = [(m_ref[u, :, 0:1], l_ref[u, :, 0:1], acc_ref[u]) for u in range(ns)]
    k0 = 0
    for hx, part0 in enumerate(halves(all_pages[0])):
        width = len(part0) * PAGE_SIZE
        for u in range(ns):
            part = halves(all_pages[u])[hx]
            s = jnp.where(masks[u][:, k0:k0 + width], scores[u][hx], MASKED)
            vs = [(i * PAGE_SIZE, PAGE_SIZE, r[0, 1].astype(BF16), True) for i, r in enumerate(part)]
            states[u] = update(states[u], s, vs)
        k0 += width
    for u in range(ns):
        put(u, states[u])

    @pl.when(j == pl.num_programs(1) - 1)
    def _():
        for u in range(ns):
            g = g_ref[u]
            os_ = acc_ref[u] * (1.0 / jnp.maximum(l_ref[u, :, 0:1], 1e-20))
            o = g[:, 0:1] * oc_ref[u] + g[:, 1:2] * os_ + g[:, 2:3] * ow_ref[u]
            o = jnp.where(own, o, 0.0)
            out = o[:, 0:HEAD_DIM]
            for h in range(1, N_KV_HEADS):
                out = out + o[:, h * HEAD_DIM:(h + 1) * HEAD_DIM]
            o_ref[u] = out.astype(BF16)


def _attn_sample(pages, pt_flat, q3, qr3, g3, kcvc, ks_new, kw_new, win, ov, e_mat, gsum, *, nb, pages_per_sample,
                 pg, past, nc, ns_valid, n_top):
    nsp = ov.shape[1]
    ns = SAMPLES_PER_STEP if nb % SAMPLES_PER_STEP == 0 else 1
    per_b = lambda shape: pl.BlockSpec((ns,) + shape, lambda b, j, pt: (b,) + (0,) * len(shape))
    const = lambda shape: pl.BlockSpec(shape, lambda b, j, pt: (0,) * len(shape))

    per_sample = lambda *shape: pltpu.VMEM((ns,) + shape, F32)
    grid_spec = pltpu.PrefetchScalarGridSpec(
        num_scalar_prefetch=1,
        grid=(nb // ns, pages_per_sample // pg),
        in_specs=_page_specs(pg, ns, pages_per_sample) + [
            per_b((N_HEADS, HEAD_DIM)), per_b((N_HEADS, HEAD_DIM)), per_b((N_HEADS, 3)),
            per_b(kcvc.shape[1:]), per_b((1, 2 * KV_W)), per_b((1, 2 * KV_W)), per_b(win.shape[1:]),
            const(ov.shape), pl.BlockSpec((1,) + e_mat.shape[1:], lambda b, j, pt: (j, 0, 0)), const(gsum.shape)],
        out_specs=per_b((N_HEADS, HEAD_DIM)),
        scratch_shapes=[per_sample(N_HEADS, LANES), per_sample(N_HEADS, LANES), per_sample(N_HEADS, KV_W),
                        per_sample(N_HEADS, nsp), per_sample(N_HEADS, KV_W), per_sample(N_HEADS, KV_W)],
    )
    return pl.pallas_call(
        functools.partial(_attn_sample_kernel, pg=pg, ns=ns, past=past, nc=nc, ns_valid=ns_valid, n_top=n_top),
        grid_spec=grid_spec,
        out_shape=jax.ShapeDtypeStruct((nb, N_HEADS, HEAD_DIM), BF16),
        compiler_params=_params("parallel", "arbitrary"),
        name="nsa_attn_sample",
    )(pt_flat, *([pages] * (ns * pg)), q3, qr3, g3, kcvc, ks_new, kw_new, win, ov, e_mat, gsum)


def _rope_tables(pos):
    half = HEAD_DIM // 2
    inv = ROPE_THETA ** (-jnp.arange(half, dtype=F32) / half)
    ang = pos.astype(F32)[:, None] * inv[None, :]
    cos = jnp.cos(ang)
    sin = jnp.sin(ang)
    reps = LANES // HEAD_DIM
    return (jnp.tile(jnp.concatenate([cos, cos], axis=1), (1, reps)),
            jnp.tile(jnp.concatenate([-sin, sin], axis=1), (1, reps)))


def _compress_weights(cmp_w1, cmp_w2, cmp_pe):
    ratio = CMP_BLOCK // CMP_STRIDE
    w1r = cmp_w1.reshape(2, ratio, CMP_STRIDE, HEAD_DIM, HEAD_DIM)
    hh = np.arange(KV_W) // HEAD_DIM
    same_head = jnp.asarray(hh[:, None] == hh[None, :])

    def block_diag(w):
        tiled = jnp.concatenate([jnp.concatenate([w] * N_KV_HEADS, axis=-1)] * N_KV_HEADS, axis=-2)
        return jnp.where(same_head, tiled, 0.0).astype(BF16)

    wbd = block_diag(w1r.reshape(2 * ratio * CMP_STRIDE, HEAD_DIM, HEAD_DIM))
    w2bd = block_diag(cmp_w2)
    pe_r = cmp_pe.reshape(2, ratio, CMP_STRIDE, HEAD_DIM).astype(F32)
    pe_w = jnp.einsum("krsd,krsde->kre", pe_r, w1r.astype(F32), precision=lax.Precision.HIGHEST)
    pe_t = jnp.tile(pe_w.reshape(2 * ratio, HEAD_DIM), (1, N_KV_HEADS))
    return wbd, pe_t, w2bd


def _overlap(nc, ncp, nsel, nsp):
    c_start = np.arange(ncp)[:, None] * CMP_STRIDE
    s_start = np.arange(nsp)[None, :] * SEL_BLOCK
    ov = (c_start < s_start + SEL_BLOCK) & (c_start + CMP_BLOCK > s_start)
    ov = ov & (np.arange(ncp)[:, None] < nc) & (np.arange(nsp)[None, :] < nsel)
    return jnp.asarray(ov, BF16)


def _tile(n, pref):
    t = min(n, pref)
    while n % t:
        t //= 2
    return t


def kernel(x_prompt, x_sample, cache_kv_cmp, cache_kv_sel, cache_kv_win, state_ssm, page_table, norm_g, mlp_w1,
           mlp_w2, nsa_w_in, nsa_w_o, nsa_cmp_w1, nsa_cmp_w2, nsa_cmp_pe, s5_a_re, s5_a_im, s5_log_dt, s5_b_re,
           s5_b_im, s5_c_re, s5_c_im, s5_d, s5_w_glu, s5_b_glu):
    b, t, d = x_prompt.shape
    nb = x_sample.shape[0]
    pages_per_sample = page_table.shape[1]
    past = pages_per_sample * PAGE_SIZE
    rows_p = b * t
    g = norm_g.reshape(norm_g.shape[0], 4, 1, d)

    w_in = nsa_w_in[0]
    w_q = w_in[:, :Q_W].astype(BF16)
    w_kv = w_in[:, Q_W:Q_W + 6 * KV_W].astype(BF16)
    w_gate = jnp.pad(w_in[:, Q_W + 6 * KV_W:], ((0, 0), (0, LANES - 3 * N_HEADS))).astype(BF16)
    w_o = nsa_w_o[0].astype(BF16)
    wbd, pe_t, w2bd = _compress_weights(nsa_cmp_w1[0], nsa_cmp_w2[0], nsa_cmp_pe[0])
    w1 = mlp_w1.astype(BF16)
    w2 = mlp_w2.astype(BF16)
    w_glu = s5_w_glu[0].astype(BF16)
    ops = _s5_operators(s5_a_re[0], s5_a_im[0], s5_log_dt[0], s5_b_re[0], s5_b_im[0], s5_c_re[0], s5_c_im[0])
    d_skip = s5_d[0].reshape(1, d)
    b_glu = s5_b_glu[0].reshape(1, d)

    tm = _tile(rows_p, ROW_TILE)
    ff_chunk = _tile(mlp_w1.shape[2], FF_CHUNK)

    xp = x_prompt.reshape(rows_p, d)
    cos_p, sin_p = _rope_tables(jnp.arange(t, dtype=jnp.int32))
    tq = next(c for c in (2 * LANES, LANES) if t % c == 0 and WINDOW % c == 0)
    (qT, qrT, gT, kvc, kvcT, kvsT, kvwT, ksb, kwb, vsT, vwT) = _inproj(
        xp, g[0, 0], w_q.T, w_kv, w_gate, cos_p, sin_p, tm=_tile(t, ROW_TILE), pos_blocks=t // _tile(t, ROW_TILE),
        transposed=True,
        key_chunk=tq)
    nsb_p = t // CMP_STRIDE
    nc_p = nsb_p - CMP_BLOCK // CMP_STRIDE + 1
    nsel_p = t // SEL_BLOCK
    kc, vcT = _compress_prompt(kvc.reshape(b, t, 2 * KV_W), wbd, pe_t, w2bd)
    ovT = _overlap(nc_p, nsb_p, nsel_p, nsel_p).T
    oT = _attn_prompt(qT, qrT, gT, kc, vcT, ksb, vsT, kwb, vwT, ovT, batch=b, seq=t, tq=tq, nc=nc_p,
                      n_top=min(TOP_N, nsel_p))
    hp, xn1 = _layer_tail((oT, w_o), xp, g[0, 1], g[0, 2], w1[0], w2[0], g[0, 3], g[1, 0], mixer="nsa", tm=tm,
                          ff_chunk=ff_chunk, transposed=True)

    pw = SSM_UNIT * SSM_GROUP
    npair = d // pw
    y3, hfin = _s5_seq(xn1.reshape(b, t, d), ops)
    yc = y3.reshape(rows_p, d)
    (hp,) = _layer_tail((yc, xn1, d_skip, w_glu, b_glu), hp, g[1, 1], g[1, 2], w1[1], w2[1], g[1, 3], None,
                        mixer="s5", tm=tm, ff_chunk=ff_chunk)
    ssm_p = hfin.reshape(npair, 2, b, SSM_UNIT, SSM_STATE).transpose(2, 1, 0, 3, 4)
    ssm_p = ssm_p.reshape(b, 2, d // SSM_GROUP, SSM_STATE)

    xs = x_sample.reshape(nb, d)
    cos_s, sin_s = _rope_tables(jnp.full((nb,), past, dtype=jnp.int32))
    q_s, qr_s, gates_s, kvc_s, kvs_s, kvw_s = _inproj(
        xs, g[0, 0], w_q, w_kv, w_gate, cos_s, sin_s, tm=nb, pos_blocks=1, transposed=False)
    pt_flat = page_table.reshape(-1).astype(jnp.int32)
    pg = _tile(pages_per_sample, PAGE_GROUP)
    n_pool = cache_kv_cmp.shape[1]
    feature_major = lambda c, n, s: c.transpose(0, 2, 3, 4, 1).reshape(n, 2, KV_W, s)
    cmp_pages = feature_major(cache_kv_cmp[0], n_pool, PAGE_SIZE)
    sel_pages = feature_major(cache_kv_sel[0], n_pool, PAGE_SIZE)
    kcvc = _compress_paged(cmp_pages, pt_flat, wbd, pe_t, w2bd, nb=nb, pages_per_sample=pages_per_sample, pg=pg)
    l_all = past + 1
    nsb_s = l_all // CMP_STRIDE
    nc_s = nsb_s - CMP_BLOCK // CMP_STRIDE + 1
    nsel_s = -(-l_all // SEL_BLOCK)
    nsp = -(-nsel_s // LANES) * LANES
    ov_s = _overlap(nc_s, past // CMP_STRIDE, nsel_s, nsp)
    keys_per_step = pg * PAGE_SIZE
    key_blk = (np.arange(past) // SEL_BLOCK).reshape(past // keys_per_step, 1, keys_per_step)
    e_mat = jnp.asarray(np.arange(nsp)[None, :, None] == key_blk, BF16)
    hh = np.arange(N_HEADS)
    gsum = jnp.asarray((hh[:, None] // GROUP) == (hh[None, :] // GROUP), BF16)
    win = feature_major(cache_kv_win[0], nb, WINDOW)
    o_s = _attn_sample(sel_pages, pt_flat, q_s.reshape(nb, N_HEADS, HEAD_DIM), qr_s.reshape(nb, N_HEADS, HEAD_DIM),
                       gates_s[:, :3 * N_HEADS].reshape(nb, N_HEADS, 3), kcvc, kvs_s.reshape(nb, 1, 2 * KV_W),
                       kvw_s.reshape(nb, 1, 2 * KV_W), win, ov_s, e_mat, gsum, nb=nb,
                       pages_per_sample=pages_per_sample, pg=pg, past=past, nc=nc_s, ns_valid=nsel_s,
                       n_top=min(TOP_N, nsel_s))
    hs, xn1_s = _layer_tail((o_s.reshape(nb, Q_W), w_o), xs, g[0, 1], g[0, 2], w1[0], w2[0], g[0, 3], g[1, 0],
                            mixer="nsa", tm=nb, ff_chunk=ff_chunk)

    u2_s = xn1_s.astype(BF16).reshape(nb, npair, pw).transpose(1, 0, 2)
    st = state_ssm[0].reshape(nb, 2, npair, SSM_UNIT * SSM_STATE).transpose(1, 2, 0, 3)
    y2_s, hr_s, hi_s = _s5_step(u2_s, st[0], st[1], ops)
    yc_s = y2_s.transpose(1, 0, 2).reshape(nb, d)
    (hs,) = _layer_tail((yc_s, xn1_s, d_skip, w_glu, b_glu), hs, g[1, 1], g[1, 2], w1[1], w2[1], g[1, 3], None,
                        mixer="s5", tm=nb, ff_chunk=ff_chunk)
    ssm_s = jnp.stack([hr_s, hi_s], axis=0).transpose(2, 0, 1, 3).reshape(nb, 2, d // SSM_GROUP, SSM_STATE)

    kv5 = lambda a, n, s: a.reshape(1, n, s, 2, N_KV_HEADS, HEAD_DIM)
    from_fm = lambda a, n, s: a.reshape(n, 2, N_KV_HEADS, HEAD_DIM, s).transpose(0, 4, 1, 2, 3)[None]
    win_s = jnp.concatenate([win[..., 1:], kvw_s.reshape(nb, 2, KV_W, 1)], axis=-1)
    return (hp.reshape(b, t, d), hs.reshape(nb, 1, d),
            from_fm(kvcT, b, t), kv5(kvc_s, nb, 1), from_fm(kvsT, b, t), kv5(kvs_s, nb, 1),
            from_fm(kvwT[:, :, t - WINDOW:], b, WINDOW), from_fm(win_s, nb, WINDOW), ssm_p[None], ssm_s[None])
```

```python
import functools

import jax
import jax.numpy as jnp
import numpy as np
from jax import lax
from jax.experimental import pallas as pl
from jax.experimental.pallas import tpu as pltpu

N_HEADS = 16
HEAD_DIM = 64
N_KV_HEADS = 4
GROUP = N_HEADS // N_KV_HEADS
CMP_BLOCK = 32
CMP_STRIDE = 16
SEL_BLOCK = 64
TOP_N = 16
WINDOW = 512
ROPE_THETA = 10000.0
PAGE_SIZE = 128
SSM_GROUP = 16
SSM_STATE = 64
SSM_CHUNK = 8
SSM_UNIT = 8
EPS = 1e-6
NEG = -1e30
FORCE = 1e9
MASKED = -1.5e38
LOG2E = 1.4426950408889634
V_ROWS = HEAD_DIM + 16
Q_W = N_HEADS * HEAD_DIM
KV_W = N_KV_HEADS * HEAD_DIM
LANES = 128
SUBLANES = 8
ROW_TILE = 512
FF_CHUNK = 1024
PAGE_GROUP = 16
SAMPLES_PER_STEP = 2
ATTN_SAMPLES_PER_STEP = 4
S5_TIME_BLOCK = 2048
VMEM_LIMIT = 56 * 1024 * 1024

F32 = jnp.float32
BF16 = jnp.bfloat16


def _params(*sem):
    return pltpu.CompilerParams(dimension_semantics=sem, vmem_limit_bytes=VMEM_LIMIT)


def _full(shape):
    zeros = (0,) * len(shape)
    return pl.BlockSpec(shape, lambda *_: zeros)


def _rms(x, g):
    ms = jnp.mean(x * x, axis=-1, keepdims=True)
    return x * lax.rsqrt(ms + EPS) * g


def _dot(a, b):
    return jnp.dot(a, b, preferred_element_type=F32)


def _dot_f32lhs(w, x):
    hi = x.astype(BF16)
    r1 = x - hi.astype(F32)
    mid = r1.astype(BF16)
    lo = (r1 - mid.astype(F32)).astype(BF16)
    return _dot(w, hi) + _dot(w, mid) + _dot(w, lo)


def _dot_f32rhs(x, w):
    hi = x.astype(BF16)
    r1 = x - hi.astype(F32)
    mid = r1.astype(BF16)
    lo = (r1 - mid.astype(F32)).astype(BF16)
    return _dot(hi, w) + _dot(mid, w) + _dot(lo, w)


def _rope_nat(x, cos, sin):
    half = HEAD_DIM // 2
    lane = lax.broadcasted_iota(jnp.int32, (1, LANES), 1)
    first = (lane % HEAD_DIM) < half
    outs = []
    for c in range(x.shape[1] // LANES):
        xc = x[:, c * LANES:(c + 1) * LANES]
        rot = jnp.where(first, pltpu.roll(xc, LANES - half, 1), pltpu.roll(xc, half, 1))
        outs.append(xc * cos + rot * sin)
    return jnp.concatenate(outs, axis=1)


def _inproj_kernel(x_ref, g_ref, wq_ref, wkv_ref, wg_ref, cos_ref, sin_ref, *rest, transposed, key_chunk):
    outs = rest[2:] if transposed else rest
    xb = _rms(x_ref[...], g_ref[...]).astype(BF16)
    cos = cos_ref[...]
    sin = sin_ref[...]
    scale = HEAD_DIM ** -0.5
    kv = _dot(xb, wkv_ref[...])
    gates = jax.nn.sigmoid(_dot(xb, wg_ref[...]))
    k_s = _rope_nat(kv[:, 2 * KV_W:3 * KV_W], cos, sin)
    v_s = kv[:, 3 * KV_W:4 * KV_W]
    k_w = _rope_nat(kv[:, 4 * KV_W:5 * KV_W], cos, sin)
    v_w = kv[:, 5 * KV_W:6 * KV_W]
    kvc_ref = outs[3]
    kvc_ref[...] = kv[:, 0:2 * KV_W]
    if transposed:
        qT_ref, qrT_ref, gT_ref, _, kvcT_ref, kvsT_ref, kvwT_ref, ksb_ref, kwb_ref, vsT_ref, vwT_ref = outs
        qT = _dot_nt(wq_ref[...], xb)
        cosT = rest[0][...]
        sinT = rest[1][...]
        half = HEAD_DIM // 2
        rotated = []
        for h in range(N_HEADS):
            x1 = qT[h * HEAD_DIM:h * HEAD_DIM + half, :]
            x2 = qT[h * HEAD_DIM + half:(h + 1) * HEAD_DIM, :]
            rotated += [x1 * cosT - x2 * sinT, x2 * cosT + x1 * sinT]
        qT_ref[...] = (qT * (scale * LOG2E)).astype(BF16)
        qrT_ref[...] = (jnp.concatenate(rotated, axis=0) * (scale * LOG2E)).astype(BF16)
        gT_ref[...] = gates.T
        tm = x_ref.shape[0]
        rowi = lax.broadcasted_iota(jnp.int32, (tm, LANES), 0)
        lanei = lax.broadcasted_iota(jnp.int32, (tm, LANES), 1)
        blk = lax.shift_right_logical(rowi & (key_chunk - 1), SEL_BLOCK.bit_length() - 1)
        extra = lanei - HEAD_DIM
        nblk = key_chunk // SEL_BLOCK
        aug = jnp.where((extra == blk) | ((extra >= nblk) & (extra < nblk + N_CONST_LANES)), 1.0, 0.0)
        for c in range(KV_W // LANES):
            for k_nat, k_ref in ((k_s, ksb_ref), (k_w, kwb_ref)):
                pair = k_nat[:, c * LANES:(c + 1) * LANES]
                k_ref[2 * c] = jnp.where(lanei < HEAD_DIM, pair, aug).astype(BF16)
                k_ref[2 * c + 1] = jnp.where(lanei < HEAD_DIM, pltpu.roll(pair, HEAD_DIM, 1), aug).astype(BF16)
        v_sT = v_s.T
        v_wT = v_w.T
        ones_rows = jnp.where(lax.broadcasted_iota(jnp.int32, (V_ROWS - HEAD_DIM, tm), 0) == 0, 1.0, 0.0)
        for h in range(N_KV_HEADS):
            for vT, v_ref in ((v_sT, vsT_ref), (v_wT, vwT_ref)):
                v_ref[h * V_ROWS:(h + 1) * V_ROWS, :] = jnp.concatenate(
                    [vT[h * HEAD_DIM:(h + 1) * HEAD_DIM, :], ones_rows], axis=0).astype(BF16)
        kvcT_ref[0] = kv[:, 0:2 * KV_W].T
        kvsT_ref[0, 0:KV_W, :] = k_s.T
        kvsT_ref[0, KV_W:2 * KV_W, :] = v_sT
        kvwT_ref[0, 0:KV_W, :] = k_w.T
        kvwT_ref[0, KV_W:2 * KV_W, :] = v_wT
    else:
        q_ref, qr_ref, gt_ref, _, kvs_ref, kvw_ref = outs
        q = _dot(xb, wq_ref[...])
        qr = _rope_nat(q, cos, sin)
        q_ref[...] = (q * scale).astype(BF16)
        qr_ref[...] = (qr * scale).astype(BF16)
        gt_ref[...] = gates
        kvs_ref[:, 0:KV_W] = k_s
        kvs_ref[:, KV_W:2 * KV_W] = v_s
        kvw_ref[:, 0:KV_W] = k_w
        kvw_ref[:, KV_W:2 * KV_W] = v_w


def _inproj(x, g, w_q, w_kv, w_gate, cos_t, sin_t, *, tm, pos_blocks, transposed, key_chunk=LANES):
    rows, d = x.shape
    assert tm % key_chunk == 0 or not transposed
    n = rows // tm
    row_blk = lambda w: pl.BlockSpec((tm, w), lambda i: (i, 0))
    col_blk = lambda h: pl.BlockSpec((h, tm), lambda i: (0, i))
    tab = pl.BlockSpec((tm, LANES), lambda i: (i % pos_blocks, 0))
    kv_nat = jax.ShapeDtypeStruct((rows, 2 * KV_W), F32)
    if transposed:
        seqs = rows // (pos_blocks * tm)
        kh = pl.BlockSpec((N_KV_HEADS, tm, LANES), lambda i: (0, i, 0))
        kvT = pl.BlockSpec((1, 2 * KV_W, tm), lambda i: (i // pos_blocks, 0, i % pos_blocks))
        out_shape = ([jax.ShapeDtypeStruct((Q_W, rows), BF16)] * 2 + [jax.ShapeDtypeStruct((LANES, rows), F32)]
                     + [kv_nat] + [jax.ShapeDtypeStruct((seqs, 2 * KV_W, pos_blocks * tm), F32)] * 3
                     + [jax.ShapeDtypeStruct((N_KV_HEADS, rows, LANES), BF16)] * 2
                     + [jax.ShapeDtypeStruct((N_KV_HEADS * V_ROWS, rows), BF16)] * 2)
        out_specs = ([col_blk(Q_W)] * 2 + [col_blk(LANES)] + [row_blk(2 * KV_W)] + [kvT] * 3 + [kh] * 2
                     + [col_blk(N_KV_HEADS * V_ROWS)] * 2)
    else:
        out_shape = ([jax.ShapeDtypeStruct((rows, Q_W), BF16)] * 2 + [jax.ShapeDtypeStruct((rows, LANES), F32)]
                     + [kv_nat] * 3)
        out_specs = [row_blk(Q_W)] * 2 + [row_blk(LANES)] + [row_blk(2 * KV_W)] * 3
    tables = [cos_t, sin_t]
    table_specs = [tab, tab]
    if transposed:
        half = HEAD_DIM // 2
        tables += [cos_t[:, 0:half].T, sin_t[:, half:HEAD_DIM].T]
        table_specs += [pl.BlockSpec((half, tm), lambda i: (0, i % pos_blocks))] * 2
    return pl.pallas_call(
        functools.partial(_inproj_kernel, transposed=transposed, key_chunk=key_chunk),
        grid=(n,),
        in_specs=[row_blk(d), _full((1, d)), _full(w_q.shape), _full(w_kv.shape), _full(w_gate.shape)] + table_specs,
        out_specs=out_specs,
        out_shape=out_shape,
        compiler_params=_params("parallel"),
        name="nsa_inproj",
    )(x, g, w_q, w_kv, w_gate, *tables)


_KV_CHUNKS = 2 * KV_W // LANES


def _compress_half(load_rows, kv, nrows, wbd_ref, pe_ref):
    ratio = CMP_BLOCK // CMP_STRIDE
    accs = [jnp.broadcast_to(pe_ref[kv * ratio + r:kv * ratio + r + 1, :], (nrows, KV_W)) for r in range(ratio)]
    for s in range(CMP_STRIDE):
        lhs = load_rows(s, kv).astype(BF16)
        for r in range(ratio):
            accs[r] = accs[r] + _dot(lhs, wbd_ref[(kv * ratio + r) * CMP_STRIDE + s])
    return accs


def _compress_prompt_kernel(*refs, nsb):
    x_refs = refs[:_KV_CHUNKS]
    wbd_ref, pe_ref, w2_ref, kc_ref, vcT_ref, sh_ref = refs[_KV_CHUNKS:]

    def load_rows(s, kv):
        per_half = _KV_CHUNKS // 2
        return jnp.concatenate([x_refs[kv * per_half + c][0, pl.ds(s, nsb, stride=CMP_STRIDE), :]
                                for c in range(per_half)], axis=1)

    sh_ref[nsb:nsb + SUBLANES, :] = jnp.zeros((SUBLANES, KV_W), F32)
    for kv in range(2):
        pr0, pr1 = _compress_half(load_rows, kv, nsb, wbd_ref, pe_ref)
        sh_ref[0:nsb, :] = pr1
        h = pr0 + sh_ref[pl.ds(1, nsb), :]
        out = _dot(jax.nn.gelu(h).astype(BF16), w2_ref[kv])
        if kv == 0:
            for hh in range(N_KV_HEADS):
                kc_ref[0, hh] = out[:, hh * HEAD_DIM:(hh + 1) * HEAD_DIM].astype(BF16)
        else:
            vcT_ref[0] = out.T.astype(BF16)


def _compress_prompt(kvc3, wbd, pe_t, w2bd):
    b, t, _ = kvc3.shape
    nsb = t // CMP_STRIDE
    return pl.pallas_call(
        functools.partial(_compress_prompt_kernel, nsb=nsb),
        grid=(b,),
        in_specs=[pl.BlockSpec((1, t, LANES), lambda i, c=c: (i, 0, c)) for c in range(_KV_CHUNKS)]
        + [_full(wbd.shape), _full(pe_t.shape), _full(w2bd.shape)],
        out_specs=[pl.BlockSpec((1, N_KV_HEADS, nsb, HEAD_DIM), lambda i: (i, 0, 0, 0)),
                   pl.BlockSpec((1, KV_W, nsb), lambda i: (i, 0, 0))],
        out_shape=[jax.ShapeDtypeStruct((b, N_KV_HEADS, nsb, HEAD_DIM), BF16),
                   jax.ShapeDtypeStruct((b, KV_W, nsb), BF16)],
        scratch_shapes=[pltpu.VMEM((nsb + SUBLANES, KV_W), F32)],
        compiler_params=_params("parallel"),
        name="nsa_compress_prompt",
    )(*([kvc3] * _KV_CHUNKS), wbd, pe_t, w2bd)


def _topk_mask_T(imp, n_top):
    ns, w = imp.shape
    nblk = ns // SUBLANES
    blocks = [imp[r * SUBLANES:(r + 1) * SUBLANES, :] for r in range(nblk)]
    cnts = [jnp.zeros((SUBLANES, w), F32) for _ in range(nblk)]
    sub = lax.broadcasted_iota(jnp.int32, (SUBLANES, w), 0)
    for sp in range(ns):
        row = blocks[sp // SUBLANES][sp % SUBLANES:sp % SUBLANES + 1, :]
        for r in range(nblk):
            blk = blocks[r]
            if sp < r * SUBLANES:
                beats = jnp.where(row >= blk, 1.0, 0.0)
            elif sp >= (r + 1) * SUBLANES:
                beats = jnp.where(row > blk, 1.0, 0.0)
            else:
                beats = jnp.where(sub > (sp - r * SUBLANES), jnp.where(row >= blk, 1.0, 0.0),
                                  jnp.where(row > blk, 1.0, 0.0))
            cnts[r] = cnts[r] + beats
    return jnp.concatenate([jnp.where(c < n_top, 1.0, 0.0) for c in cnts], axis=0)


def _online_chunks(states, k_cs, vT_cs, qTs, bias):
    scores = [_dot(k_c, qT) for k_c, qT in zip(k_cs, qTs)]
    mids = []
    for (m, _), s in zip(states, scores):
        if bias is not None:
            s = s + bias
        m_new = jnp.maximum(m, jnp.max(s, axis=0, keepdims=True))
        mids.append((m_new, jnp.exp2(m - m_new), jnp.exp2(s - m_new).astype(BF16)))
    return tuple((m_new, alpha * acc + _dot(vT_c, p))
                 for (m_new, alpha, p), (_, acc), vT_c in zip(mids, states, vT_cs))


def _softmax_finish(carry):
    _, acc = carry
    return acc[0:HEAD_DIM, :] * (1.0 / jnp.maximum(acc[HEAD_DIM:HEAD_DIM + 1, :], 1e-20))


LAZY_LOG2_MAX = 60.0
N_CONST_LANES = 4


def _split3(x):
    hi = x.astype(BF16).astype(F32)
    r = x - hi
    mid = r.astype(BF16).astype(F32)
    return hi, mid, (r - mid).astype(BF16).astype(F32)


def _rescaling_chunks(m_ref, acc_ref, k_cs, vT_cs, qTs, base_tiles, bias):
    nh = len(k_cs)
    rows, w = base_tiles[0].shape
    zero_rows = jnp.zeros((LANES - HEAD_DIM - rows, w), BF16)
    new = _online_chunks(tuple((m_ref[h], acc_ref[h]) for h in range(nh)), k_cs, vT_cs,
                         [jnp.concatenate([qTs[h], base_tiles[h].astype(BF16), zero_rows], axis=0) for h in range(nh)],
                         bias)
    for h in range(nh):
        m_ref[h] = new[h][0]
        acc_ref[h] = new[h][1]


def _lazy_chunks(m_ref, acc2_ref, slot_ref, k_cs, vT_cs, qTs, base_tiles, bias, ref_row, n_chunks=1, first=False):
    slot = slot_ref[0]
    acc_ref = acc2_ref.at[slot]
    nu = len(k_cs)
    nh = nu // n_chunks
    biases = list(bias) if isinstance(bias, (list, tuple)) else [bias] * n_chunks
    rows = base_tiles[0].shape[0]
    w = base_tiles[0].shape[1]
    rowt = lax.broadcasted_iota(jnp.int32, (rows, w), 0)
    zero_rows = jnp.zeros((LANES - HEAD_DIM - rows, w), BF16)

    def queries(u):
        hi, mid, lo = _split3(-m_ref[u % nh])
        tile = jnp.where(rowt == ref_row, hi, jnp.where(rowt == ref_row + 1, mid,
                                                        jnp.where(rowt == ref_row + 2, lo, base_tiles[u])))
        return jnp.concatenate([qTs[u % nh], tile.astype(BF16), zero_rows], axis=0)

    scores = [_dot(k_cs[u], queries(u)) for u in range(nu)]
    peaks = [None] * nh
    probs = []
    for u, s in enumerate(scores):
        if biases[u // nh] is not None:
            s = s + biases[u // nh]
        top = jnp.max(s, axis=0, keepdims=True)
        peaks[u % nh] = top if peaks[u % nh] is None else jnp.maximum(peaks[u % nh], top)
        probs.append(jnp.exp2(s).astype(BF16))
    pvs = [_dot(vT_cs[u], probs[u]) for u in range(nu)]
    for h in range(nh):
        total = pvs[h]
        for c in range(1, n_chunks):
            total = total + pvs[c * nh + h]
        acc2_ref[1 - slot, h] = acc_ref[h] + total
    highest = functools.reduce(jnp.maximum, peaks)
    lowest = functools.reduce(jnp.minimum, peaks)
    in_range = (jnp.max(highest) <= LAZY_LOG2_MAX) & (jnp.logical_not(first) | (jnp.min(lowest) >= -LAZY_LOG2_MAX))

    @pl.when(in_range)
    def _():
        slot_ref[0] = 1 - slot

    @pl.when(jnp.logical_not(in_range))
    def _():
        for h in range(nh):
            m_ref[h] = jnp.where(first, NEG, m_ref[h])
        for c in range(n_chunks):
            part = slice(c * nh, (c + 1) * nh)
            _rescaling_chunks(m_ref, acc_ref, k_cs[part], vT_cs[part], qTs, base_tiles[part], biases[c])


def _attn_prompt_kernel(qT_ref, qrT_ref, gT_ref, kc_ref, vcT_ref, ks_ref, vsT_ref, kw_ref, vwT_ref, ovT_ref,
                        o_ref, selb_ref, oc_ref, ow_ref, m_ref, acc2_ref, slot_ref, *, tq, nc, n_top):
    ck = tq
    i = pl.program_id(1)
    t0 = i * tq
    qpos = t0 + lax.broadcasted_iota(jnp.int32, (1, tq), 1)
    ncp = kc_ref.shape[2]
    ns = ovT_ref.shape[0]
    w = GROUP * tq
    bpc = ck // SEL_BLOCK
    sel_shift = SEL_BLOCK.bit_length() - 1
    kvhs = range(N_KV_HEADS)
    heads = [[kvh * GROUP + g for g in range(GROUP)] for kvh in kvhs]
    rows = [pl.ds(kvh * HEAD_DIM, HEAD_DIM) for kvh in kvhs]
    vrows = [pl.ds(kvh * V_ROWS, V_ROWS) for kvh in kvhs]
    qrT = [jnp.concatenate([qrT_ref[h * HEAD_DIM:(h + 1) * HEAD_DIM, :] for h in heads[kvh]], axis=1)
           for kvh in kvhs]
    kl = lax.broadcasted_iota(jnp.int32, (ck, tq), 0)
    ql = lax.broadcasted_iota(jnp.int32, (ck, tq), 1)
    tile4 = lambda b: jnp.concatenate([b] * GROUP, axis=1)
    key_le_query = tile4(jnp.where(kl <= ql, 0.0, MASKED))
    key_ge_query = tile4(jnp.where(kl >= ql, 0.0, MASKED))
    bias_rows = selb_ref.shape[2]

    def reset_softmax():
        slot_ref[0] = 0
        for kvh in kvhs:
            m_ref[kvh] = jnp.zeros((1, w), F32)
            acc2_ref[0, kvh] = jnp.zeros((V_ROWS, w), F32)

    def current_acc(kvh):
        return acc2_ref[slot_ref[0], kvh]

    cidx = lax.broadcasted_iota(jnp.int32, (ncp, 1), 0)
    valid = ((cidx * CMP_STRIDE + (CMP_BLOCK - 1)) <= qpos) & (cidx < nc)
    sidx = lax.broadcasted_iota(jnp.int32, (ns, 1), 0)
    cur = lax.shift_right_logical(qpos, sel_shift)
    forced = (sidx == 0) | (sidx == cur) | (sidx == cur - 1)
    causal = (sidx * SEL_BLOCK) <= qpos
    cmp_scores = [
        _dot(kc_ref[0, kvh], jnp.concatenate([qT_ref[h * HEAD_DIM:(h + 1) * HEAD_DIM, :] for h in heads[kvh]], axis=1))
        for kvh in kvhs]
    importance = []
    for kvh in kvhs:
        s = cmp_scores[kvh]
        probs = []
        for g in range(GROUP):
            sm = jnp.where(valid, s[:, g * tq:(g + 1) * tq], NEG)
            mx = jnp.max(sm, axis=0, keepdims=True)
            e = jnp.where(valid, jnp.exp2(sm - mx), 0.0)
            den = jnp.maximum(jnp.sum(e, axis=0, keepdims=True), 1e-20)
            probs.append(e / den)
        oc_ref[kvh] = _dot(vcT_ref[0, rows[kvh], :], jnp.concatenate(probs, axis=1).astype(BF16))
        psum = probs[0]
        for g in range(1, GROUP):
            psum = psum + probs[g]
        imp = _dot_f32lhs(ovT_ref[...], psum)
        importance.append(jnp.where(forced, FORCE, jnp.where(causal, imp, NEG)))

    prefixes = sorted({r for r in (ns // 4, ns // 2, 3 * ns // 4) if r and r % SUBLANES == 0} | {ns})
    needed = (i + 1) * bpc
    fill = jnp.zeros((bias_rows - bpc, tq), F32)
    for lo, hi in zip([0] + prefixes[:-1], prefixes):
        @pl.when((needed > lo) & (needed <= hi))
        def _(hi=hi):
            for kvh in kvhs:
                selb = (_topk_mask_T(importance[kvh][0:hi, :], min(n_top, hi)) - 1.0) * (-MASKED)
                for c in range(hi // bpc):
                    selb_ref[kvh, c] = jnp.concatenate([selb[c * bpc:(c + 1) * bpc, :], fill], axis=0)

    n_back = WINDOW // ck
    reset_softmax()
    rowb = lax.broadcasted_iota(jnp.int32, (bias_rows, w), 0)
    def window_chunk(r):
        a = i - n_back + r
        kst = pl.multiple_of(jnp.maximum(a, 0) * ck, ck)
        skip = jnp.where(a < 0, MASKED, 0.0)
        tile = jnp.where(rowb == bpc, skip, 0.0)
        bias = key_le_query if r == n_back else (key_ge_query if r == 0 else None)
        return ([kw_ref[kvh, pl.ds(kst, ck), :] for kvh in kvhs],
                [vwT_ref[vrows[kvh], pl.ds(kst, ck)] for kvh in kvhs], [tile] * N_KV_HEADS, bias)

    order = [n_back] + list(range(n_back))
    for g0 in [0] + list(range(1, n_back + 1, 2)):
        group = [window_chunk(r) for r in (order[g0:g0 + 1] if g0 == 0 else order[g0:g0 + 2])]
        _lazy_chunks(m_ref, acc2_ref, slot_ref, sum((g[0] for g in group), []), sum((g[1] for g in group), []),
                     qrT, sum((g[2] for g in group), []), [g[3] for g in group], bpc + 1, len(group), first=g0 == 0)
    for kvh in kvhs:
        ow_ref[kvh] = _softmax_finish((None, current_acc(kvh)))

    def chunk_step(c, diagonal, n_chunks=1, first=False):
        ks, vs, tiles = [], [], []
        for dc in range(n_chunks):
            kst = (c + dc) * ck if isinstance(c, int) else pl.multiple_of((c + dc) * ck, ck)
            ks += [ks_ref[kvh, pl.ds(kst, ck), :] for kvh in kvhs]
            vs += [vsT_ref[vrows[kvh], pl.ds(kst, ck)] for kvh in kvhs]
            tiles += [tile4(selb_ref[kvh, c + dc]) for kvh in kvhs]
        _lazy_chunks(m_ref, acc2_ref, slot_ref, ks, vs, qrT, tiles, key_le_query if diagonal else None, bpc + 1,
                     n_chunks, first)

    reset_softmax()

    @pl.when(i > 0)
    def _():
        chunk_step(0, False, first=True)

    n_mid = jnp.maximum(i - 1, 0)

    def loop_body(p, carry):
        chunk_step(1 + 2 * p, False, n_chunks=2)
        return carry

    lax.fori_loop(0, n_mid // 2, loop_body, 0)

    @pl.when(n_mid % 2 == 1)
    def _():
        chunk_step(i - 1, False)

    chunk_step(i, True, first=i == 0)

    for kvh in kvhs:
        def gate_row(j, kvh=kvh):
            return jnp.concatenate([gT_ref[h * 3 + j:h * 3 + j + 1, :] for h in heads[kvh]], axis=1)

        oT = (gate_row(0) * oc_ref[kvh] + gate_row(1) * _softmax_finish((None, current_acc(kvh)))
              + gate_row(2) * ow_ref[kvh])
        for g, h in enumerate(heads[kvh]):
            o_ref[h * HEAD_DIM:(h + 1) * HEAD_DIM, :] = oT[:, g * tq:(g + 1) * tq].astype(BF16)


def _attn_prompt(qT, qrT, gT, kc, vcT, ksb, vsT, kwb, vwT, ovT, *, batch, seq, tq, nc, n_top):
    nq = seq // tq
    nsb = kc.shape[2]
    ns = ovT.shape[0]
    col = lambda h: pl.BlockSpec((h, tq), lambda b, i: (0, b * nq + i))
    kh = pl.BlockSpec((N_KV_HEADS, seq, LANES), lambda b, i: (0, b, 0))
    vt = pl.BlockSpec((N_KV_HEADS * V_ROWS, seq), lambda b, i: (0, b))
    bf16_sublanes = 2 * SUBLANES
    return pl.pallas_call(
        functools.partial(_attn_prompt_kernel, tq=tq, nc=nc, n_top=n_top),
        grid=(batch, nq),
        in_specs=[col(Q_W), col(Q_W), col(LANES),
                  pl.BlockSpec((1, N_KV_HEADS, nsb, HEAD_DIM), lambda b, i: (b, 0, 0, 0)),
                  pl.BlockSpec((1, KV_W, nsb), lambda b, i: (b, 0, 0)),
                  kh, vt, kh, vt, _full(ovT.shape)],
        out_specs=col(Q_W),
        out_shape=jax.ShapeDtypeStruct((Q_W, batch * seq), BF16),
        scratch_shapes=[pltpu.VMEM((N_KV_HEADS, ns * SEL_BLOCK // tq, bf16_sublanes, tq), F32),
                        pltpu.VMEM((N_KV_HEADS, HEAD_DIM, GROUP * tq), F32),
                        pltpu.VMEM((N_KV_HEADS, HEAD_DIM, GROUP * tq), F32),
                        pltpu.VMEM((N_KV_HEADS, 1, GROUP * tq), F32),
                        pltpu.VMEM((2, N_KV_HEADS, V_ROWS, GROUP * tq), F32),
                        pltpu.SMEM((1,), jnp.int32)],
        compiler_params=_params("parallel", "arbitrary"),
        name="nsa_attn_prompt",
    )(qT, qrT, gT, kc, vcT, ksb, vsT, kwb, vwT, ovT)


def _resident(shape):
    zeros = (0,) * len(shape)
    return pl.BlockSpec(shape, lambda *_: zeros, pipeline_mode=pl.Buffered(1))


def _layer_tail_kernel(*refs, mixer, transposed, ff_chunk, next_norm):
    if mixer == "nsa":
        o_ref, wo_ref, x_ref, g1_ref, g2_ref = refs[:5]
        rest = refs[5:]
        if transposed:
            y = lax.dot_general(o_ref[...], wo_ref[...], (((0,), (0,)), ((), ())), preferred_element_type=F32)
        else:
            y = _dot(o_ref[...], wo_ref[...])
    else:
        yc_ref, u_ref, d_ref, wg_ref, bg_ref, x_ref, g1_ref, g2_ref = refs[:8]
        rest = refs[8:]
        z = jax.nn.gelu(yc_ref[...] + d_ref[...] * u_ref[...])
        y = z * jax.nn.sigmoid(_dot(z.astype(BF16), wg_ref[...]) + bg_ref[...])
    w1_ref, w2_ref, g3_ref = rest[:3]
    if next_norm:
        gn_ref, h2_ref, xn_ref, acc_ref = rest[3:]
    else:
        h2_ref, acc_ref = rest[3:]
    h = x_ref[...] + _rms(y, g1_ref[...])
    xm = _rms(h, g2_ref[...]).astype(BF16)
    for c in range(w1_ref.shape[1] // ff_chunk):
        cols = slice(c * ff_chunk, (c + 1) * ff_chunk)
        hm = jnp.maximum(_dot(xm, w1_ref[:, cols]), 0.0)
        part = _dot((hm * hm).astype(BF16), w2_ref[cols, :])
        if c == 0:
            acc_ref[...] = part
        else:
            acc_ref[...] += part
    h2 = h + _rms(acc_ref[...], g3_ref[...])
    h2_ref[...] = h2
    if next_norm:
        xn_ref[...] = _rms(h2, gn_ref[...])


def _layer_tail(mixer_args, x, g1, g2, w1, w2, g3, gn, *, mixer, tm, ff_chunk, transposed=False):
    rows, d = x.shape
    row = pl.BlockSpec((tm, d), lambda i: (i, 0))
    vec = _resident((1, d))
    if mixer == "nsa":
        o, w_o = mixer_args
        o_spec = (pl.BlockSpec((Q_W, tm), lambda i: (0, i)) if transposed
                  else pl.BlockSpec((tm, Q_W), lambda i: (i, 0)))
        head_specs = [o_spec, _resident(w_o.shape)]
    else:
        w_glu = mixer_args[3]
        head_specs = [row, row, vec, _resident(w_glu.shape), vec]
    tail_args = (w1, w2, g3) + (() if gn is None else (gn,))
    tail_specs = [_resident(w1.shape), _resident(w2.shape), vec] + ([] if gn is None else [vec])
    n_out = 1 if gn is None else 2
    return pl.pallas_call(
        functools.partial(_layer_tail_kernel, mixer=mixer, transposed=transposed, ff_chunk=ff_chunk,
                          next_norm=gn is not None),
        grid=(rows // tm,),
        in_specs=head_specs + [row, vec, vec] + tail_specs,
        out_specs=[row] * n_out,
        out_shape=[jax.ShapeDtypeStruct((rows, d), F32)] * n_out,
        scratch_shapes=[pltpu.VMEM((tm, d), F32)],
        compiler_params=_params("parallel"),
        name=mixer + "_layer_tail",
    )(*mixer_args, x, g1, g2, *tail_args)


def _s5_seq_kernel(x_ref, brow_ref, pre_ref, pim_ref, qre_ref, qim_ref, are_ref, aim_ref, y_ref, hfin_ref,
                   sre, sim, hre, him, cre, cim):
    nb, tb, uw = x_ref.shape
    nt = pre_ref.shape[1] // uw
    nk = tb // nt

    @pl.when(pl.program_id(1) == 0)
    def _():
        cre[...] = jnp.zeros(cre.shape, F32)
        cim[...] = jnp.zeros(cim.shape, F32)

    u = [jnp.concatenate([x_ref[b, pl.ds(t, nk, stride=nt), :] for b in range(nb)], axis=0).astype(BF16)
         for t in range(nt)]
    u2 = [jnp.concatenate([u[2 * j], u[2 * j + 1]], axis=1) for j in range(nt // 2)]
    s_re = _dot(u2[0], pre_ref[0, 0:2 * uw, :])
    s_im = _dot(u2[0], pim_ref[0, 0:2 * uw, :])
    for j in range(1, nt // 2):
        s_re = s_re + _dot(u2[j], pre_ref[0, 2 * j * uw:(2 * j + 2) * uw, :])
        s_im = s_im + _dot(u2[j], pim_ref[0, 2 * j * uw:(2 * j + 2) * uw, :])
    sre[...] = s_re
    sim[...] = s_im
    ar = are_ref[0]
    ai = aim_ref[0]

    def body(it, carry):
        out = []
        for b in range(nb):
            hr, hi = carry[b]
            r0 = pl.multiple_of(b * nk + it * SUBLANES, SUBLANES)
            sr8 = sre[pl.ds(r0, SUBLANES), :]
            si8 = sim[pl.ds(r0, SUBLANES), :]
            prev_r, prev_i = [], []
            for j in range(SUBLANES):
                prev_r.append(hr)
                prev_i.append(hi)
                hr, hi = (ar * hr - ai * hi + sr8[j:j + 1, :], ar * hi + ai * hr + si8[j:j + 1, :])
            hre[pl.ds(r0, SUBLANES), :] = jnp.concatenate(prev_r, axis=0)
            him[pl.ds(r0, SUBLANES), :] = jnp.concatenate(prev_i, axis=0)
            out.append((hr, hi))
        return tuple(out)

    init = tuple((cre[b:b + 1, :], cim[b:b + 1, :]) for b in range(nb))
    fin = lax.fori_loop(0, nk // SUBLANES, body, init)
    for b in range(nb):
        cre[b:b + 1, :] = fin[b][0]
        cim[b:b + 1, :] = fin[b][1]
    hfin_ref[0, 0] = cre[...]
    hfin_ref[0, 1] = cim[...]

    hb_re = hre[...].astype(BF16)
    hb_im = him[...].astype(BF16)
    for t2 in range(0, nt, 2):
        cols = slice(t2 * uw, (t2 + 2) * uw)
        acc = _dot(hb_re, qre_ref[0, :, cols]) + _dot(hb_im, qim_ref[0, :, cols])
        for j in range(t2 // 2 + 1):
            lag0 = t2 - 2 * j + 1
            wpair = jnp.concatenate([brow_ref[0, :, lag0 * uw:(lag0 + 2) * uw],
                                     brow_ref[0, :, (lag0 - 1) * uw:(lag0 + 1) * uw]], axis=0)
            acc = acc + _dot(u2[j], wpair)
        for b in range(nb):
            for dt in range(2):
                y_ref[b, pl.ds(t2 + dt, nk, stride=nt), :] = acc[b * nk:(b + 1) * nk, dt * uw:(dt + 1) * uw]


def _s5_seq(x3, ops):
    nb, t, d = x3.shape
    uw = SSM_UNIT * SSM_GROUP
    nunit = d // uw
    sw = ops["p_re"].shape[2]
    tb = _tile(t, S5_TIME_BLOCK)
    nk = tb // SSM_CHUNK
    per_unit = lambda a: pl.BlockSpec((1,) + a.shape[1:], lambda i, r: (i,) + (0,) * (a.ndim - 1))
    blk = pl.BlockSpec((nb, tb, uw), lambda i, r: (0, r, i))
    args = (x3, ops["brow"], ops["p_re"], ops["p_im"], ops["q_re"], ops["q_im"], ops["a_chunk_re"], ops["a_chunk_im"])
    return pl.pallas_call(
        _s5_seq_kernel,
        grid=(nunit, t // tb),
        in_specs=[blk] + [per_unit(a) for a in args[1:]],
        out_specs=[blk, pl.BlockSpec((1, 2, nb, sw), lambda i, r: (i, 0, 0, 0))],
        out_shape=[jax.ShapeDtypeStruct((nb, t, d), F32), jax.ShapeDtypeStruct((nunit, 2, nb, sw), F32)],
        scratch_shapes=[pltpu.VMEM((nb * nk, sw), F32)] * 4 + [pltpu.VMEM((nb, sw), F32)] * 2,
        compiler_params=_params("parallel", "arbitrary"),
        name="s5_seq_scan",
    )(*args)


def _s5_step_kernel(u_ref, h0r_ref, h0i_ref, bre_ref, bim_ref, cre_ref, cim_ref, are_ref, aim_ref,
                    y_ref, hr_ref, hi_ref, *, npair):
    for p in range(npair):
        u = u_ref[p]
        ar = are_ref[p]
        ai = aim_ref[p]
        h0r = h0r_ref[p]
        h0i = h0i_ref[p]
        hr = ar * h0r - ai * h0i + _dot(u, bre_ref[p])
        hi = ar * h0i + ai * h0r + _dot(u, bim_ref[p])
        hr_ref[p] = hr
        hi_ref[p] = hi
        y_ref[p] = _dot(hr.astype(BF16), cre_ref[p]) + _dot(hi.astype(BF16), cim_ref[p])


def _s5_step(u2, h0r, h0i, ops):
    npair, rows, width = u2.shape
    args = (u2, h0r, h0i, ops["b1_re"], ops["b1_im"], ops["c1_re"], ops["c1_im"], ops["a1_re"], ops["a1_im"])
    return pl.pallas_call(
        functools.partial(_s5_step_kernel, npair=npair),
        grid=(1,),
        in_specs=[_full(a.shape) for a in args],
        out_specs=[_full((npair, rows, width)), _full(h0r.shape), _full(h0r.shape)],
        out_shape=[jax.ShapeDtypeStruct((npair, rows, width), F32), jax.ShapeDtypeStruct(h0r.shape, F32),
                   jax.ShapeDtypeStruct(h0r.shape, F32)],
        compiler_params=_params("arbitrary"),
        name="s5_single_step",
    )(*args)


def _s5_operators(a_re, a_im, log_dt, b_re, b_im, c_re, c_im):
    hp = lax.Precision.HIGHEST
    g, n = a_re.shape
    gu = SSM_UNIT
    nunit = g // gu
    L = SSM_CHUNK
    uw = gu * SSM_GROUP
    sw = gu * n
    a = lax.complex(a_re.astype(F32), a_im.astype(F32))
    dt = jnp.exp(log_dt.astype(F32))[:, None]
    a_bar = jnp.exp(a * dt)
    b_bar = ((a_bar - 1.0) / a)[:, :, None] * lax.complex(b_re.astype(F32), b_im.astype(F32))
    c = lax.complex(c_re.astype(F32), c_im.astype(F32))
    pows = [jnp.ones_like(a_bar)]
    for _ in range(L):
        pows.append(pows[-1] * a_bar)
    a_pow = jnp.stack(pows).reshape(L + 1, nunit, sw)
    apr = jnp.real(a_pow)
    api = jnp.imag(a_pow)
    eye = jnp.eye(gu, dtype=F32)

    def bd_in(x):
        return jnp.einsum("pgnd,gh->pgdhn", x.reshape(nunit, gu, n, SSM_GROUP), eye).reshape(nunit, uw, sw)

    def bd_out(x):
        return jnp.einsum("pgcn,gh->pgnhc", x.reshape(nunit, gu, SSM_GROUP, n), eye).reshape(nunit, sw, uw)

    p0r, p0i = bd_in(jnp.real(b_bar)), bd_in(jnp.imag(b_bar))
    q0r, q0i = bd_out(jnp.real(c)), bd_out(jnp.imag(c))
    lanes = lambda x, k: x[k][:, None, :]
    rows_ = lambda x, k: x[k][:, :, None]
    p_re = jnp.concatenate([p0r * lanes(apr, L - 1 - t) - p0i * lanes(api, L - 1 - t) for t in range(L)], axis=1)
    p_im = jnp.concatenate([p0r * lanes(api, L - 1 - t) + p0i * lanes(apr, L - 1 - t) for t in range(L)], axis=1)
    q_re = jnp.concatenate([q0r * rows_(apr, t + 1) - q0i * rows_(api, t + 1) for t in range(L)], axis=2)
    q_im = jnp.concatenate([-(q0r * rows_(api, t + 1) + q0i * rows_(apr, t + 1)) for t in range(L)], axis=2)
    lag_blocks = [jnp.zeros((nunit, uw, uw), F32)]
    for t in range(L):
        xr = p0r * lanes(apr, t) - p0i * lanes(api, t)
        xi = p0r * lanes(api, t) + p0i * lanes(apr, t)
        lag_blocks.append(jnp.einsum("pus,psv->puv", xr, q0r, precision=hp)
                          - jnp.einsum("pus,psv->puv", xi, q0i, precision=hp))
    brow = jnp.concatenate(lag_blocks, axis=2)
    lane_row = lambda x: x.reshape(nunit, 1, sw)
    return {
        "brow": brow.astype(BF16),
        "p_re": p_re.astype(BF16), "p_im": p_im.astype(BF16),
        "q_re": q_re.astype(BF16), "q_im": q_im.astype(BF16),
        "a_chunk_re": lane_row(apr[L]), "a_chunk_im": lane_row(api[L]),
        "a1_re": lane_row(apr[1]), "a1_im": lane_row(api[1]),
        "b1_re": p0r.astype(BF16), "b1_im": p0i.astype(BF16),
        "c1_re": q0r.astype(BF16), "c1_im": (-q0i).astype(BF16),
    }


def _compress_paged_kernel(pt_ref, *refs, pg, ns, nsb):
    del pt_ref
    pages = refs[:ns * pg]
    perm_ref, wbd_ref, pe_ref, w2_ref, out_ref, h0_ref, h1_ref = refs[ns * pg:]
    j = pl.program_id(1)
    sbp = PAGE_SIZE // CMP_STRIDE
    nrows = pg * sbp
    pair_rows = 2 * sbp

    @pl.when(j == 0)
    def _():
        h1_ref[:, :, nsb:nsb + SUBLANES, :] = jnp.zeros((ns, 2, SUBLANES, KV_W), F32)

    r0 = pl.multiple_of(j * nrows, nrows)
    for kv in range(2):
        staged = []
        for q in range(ns * pg // 2):
            z = jnp.concatenate([pages[2 * q][0, kv], pages[2 * q + 1][0, kv]], axis=1).astype(BF16)
            staged.append(_dot_nt(perm_ref[...], z).astype(BF16))

        def load_rows(s, kv, staged=staged):
            return jnp.concatenate([x[s * pair_rows:(s + 1) * pair_rows, :] for x in staged], axis=0)

        pr0, pr1 = _compress_half(load_rows, kv, ns * nrows, wbd_ref, pe_ref)
        for u in range(ns):
            h0_ref[u, kv, pl.ds(r0, nrows), :] = pr0[u * nrows:(u + 1) * nrows, :]
            h1_ref[u, kv, pl.ds(r0, nrows), :] = pr1[u * nrows:(u + 1) * nrows, :]

    @pl.when(j == pl.num_programs(1) - 1)
    def _():
        for u in range(ns):
            for kv in range(2):
                h = h0_ref[u, kv] + h1_ref[u, kv, pl.ds(1, nsb), :]
                out = _dot(jax.nn.gelu(h).astype(BF16), w2_ref[kv])
                out_ref[u, :, kv * KV_W:(kv + 1) * KV_W] = out.astype(BF16)


def _page_specs(pg, ns, pages_per_sample):
    def spec(u, i):
        return pl.BlockSpec((1, 2, KV_W, PAGE_SIZE),
                            lambda b, j, pt: (pt[(b * ns + u) * pages_per_sample + j * pg + i], 0, 0, 0))
    return [spec(u, i) for u in range(ns) for i in range(pg)]


def _compress_paged(pages, pt_flat, wbd, pe_t, w2bd, *, nb, pages_per_sample, pg):
    nsb = pages_per_sample * PAGE_SIZE // CMP_STRIDE
    sbp = PAGE_SIZE // CMP_STRIDE
    ns = SAMPLES_PER_STEP if nb % SAMPLES_PER_STEP == 0 else 1
    i_out = np.arange(2 * PAGE_SIZE)
    s_i, pg_i, n_i = i_out // (2 * sbp), (i_out // sbp) % 2, i_out % sbp
    perm = jnp.asarray(i_out[None, :] == (pg_i * PAGE_SIZE + n_i * CMP_STRIDE + s_i)[:, None], BF16)
    c3 = lambda shape: pl.BlockSpec(shape, lambda b, j, pt: (0,) * len(shape))
    grid_spec = pltpu.PrefetchScalarGridSpec(
        num_scalar_prefetch=1,
        grid=(nb // ns, pages_per_sample // pg),
        in_specs=_page_specs(pg, ns, pages_per_sample) + [c3(perm.shape), c3(wbd.shape), c3(pe_t.shape),
                                                          c3(w2bd.shape)],
        out_specs=pl.BlockSpec((ns, nsb, 2 * KV_W), lambda b, j, pt: (b, 0, 0)),
        scratch_shapes=[pltpu.VMEM((ns, 2, nsb, KV_W), F32), pltpu.VMEM((ns, 2, nsb + SUBLANES, KV_W), F32)],
    )
    return pl.pallas_call(
        functools.partial(_compress_paged_kernel, pg=pg, ns=ns, nsb=nsb),
        grid_spec=grid_spec,
        out_shape=jax.ShapeDtypeStruct((nb, nsb, 2 * KV_W), BF16),
        compiler_params=_params("parallel", "arbitrary"),
        name="nsa_compress_paged",
    )(pt_flat, *([pages] * (ns * pg)), perm, wbd, pe_t, w2bd)


def _topk_mask_lanes(imp, n_top, ns_valid):
    lane = lax.broadcasted_iota(jnp.int32, imp.shape, 1)
    cnt = jnp.zeros(imp.shape, F32)
    for sp in range(ns_valid):
        col = imp[:, sp:sp + 1]
        cnt = cnt + jnp.where(lane > sp, jnp.where(col >= imp, 1.0, 0.0), jnp.where(col > imp, 1.0, 0.0))
    return jnp.where((cnt < n_top) & (lane < ns_valid), 1.0, 0.0)


def _dot_nt(a, b):
    return lax.dot_general(a, b, (((1,), (1,)), ((), ())), preferred_element_type=F32)


def _attn_sample_kernel(pt_ref, *refs, pg, ns, past, nc, ns_valid, n_top):
    del pt_ref
    all_pages = [refs[u * pg:(u + 1) * pg] for u in range(ns)]
    (q_ref, qr_ref, g_ref, kcvc_ref, ksn_ref, kwn_ref, win_ref, ov_ref, e_ref, gs_ref, o_ref,
     m_ref, l_ref, acc_ref, sel_ref, oc_ref, ow_ref) = refs[ns * pg:]
    j = pl.program_id(1)
    ncp = kcvc_ref.shape[1]
    nsp = ov_ref.shape[1]
    wlen = win_ref.shape[3]
    row = lax.broadcasted_iota(jnp.int32, (N_HEADS, KV_W), 0)
    lane = lax.broadcasted_iota(jnp.int32, (N_HEADS, KV_W), 1)
    own = (lane // HEAD_DIM) == (row // GROUP)

    def spread(ref, u):
        q = ref[u]
        return jnp.where(own, jnp.concatenate([q] * N_KV_HEADS, axis=1), jnp.zeros((N_HEADS, KV_W), BF16))

    def update(state, s, vs):
        m, l, acc = state
        m_new = jnp.maximum(m, jnp.max(s, axis=1, keepdims=True))
        alpha = jnp.exp(m - m_new)
        p = jnp.exp(s - m_new)
        l = alpha * l + jnp.sum(p, axis=1, keepdims=True)
        pv = None
        for st, sz, v, feature_major in vs:
            pb = p[:, st:st + sz].astype(BF16)
            t = _dot_nt(pb, v) if feature_major else _dot(pb, v)
            pv = t if pv is None else pv + t
        return m_new, l, alpha * acc + pv

    def init():
        return (jnp.full((N_HEADS, 1), NEG, F32), jnp.zeros((N_HEADS, 1), F32), jnp.zeros((N_HEADS, KV_W), F32))

    def new_row_update(state, qbd, new_row):
        r8 = lax.broadcasted_iota(jnp.int32, (SUBLANES, 2 * KV_W), 0)
        tile = jnp.where(r8 == 0, jnp.broadcast_to(new_row, (SUBLANES, 2 * KV_W)), 0.0).astype(BF16)
        s = _dot_nt(qbd, tile[:, 0:KV_W])
        l8 = lax.broadcasted_iota(jnp.int32, (N_HEADS, SUBLANES), 1)
        s = jnp.where(l8 == 0, s, MASKED)
        return update(state, s, [(0, SUBLANES, tile[:, KV_W:2 * KV_W], False)])

    def put(u, state):
        m, l, acc = state
        m_ref[u] = jnp.broadcast_to(m, m_ref.shape[1:])
        l_ref[u] = jnp.broadcast_to(l, l_ref.shape[1:])
        acc_ref[u] = acc

    qrbd = [spread(qr_ref, u) for u in range(ns)]

    @pl.when(j == 0)
    def _():
        cidx = lax.broadcasted_iota(jnp.int32, (1, ncp), 1)
        valid = ((cidx * CMP_STRIDE + (CMP_BLOCK - 1)) <= past) & (cidx < nc)
        sidx = lax.broadcasted_iota(jnp.int32, (1, nsp), 1)
        cur = past // SEL_BLOCK
        forced = (sidx == 0) | (sidx == cur) | (sidx == cur - 1)
        causal = (sidx * SEL_BLOCK) <= past
        wpos = past - wlen + lax.broadcasted_iota(jnp.int32, (1, wlen), 1)
        in_window = (wpos >= 0) & (past - wpos <= WINDOW)
        for u in range(ns):
            s = _dot_nt(spread(q_ref, u), kcvc_ref[u, :, 0:KV_W])
            sm = jnp.where(valid, s, NEG)
            mx = jnp.max(sm, axis=1, keepdims=True)
            e = jnp.where(valid, jnp.exp(sm - mx), 0.0)
            p = e / jnp.maximum(jnp.sum(e, axis=1, keepdims=True), 1e-20)
            oc_ref[u] = _dot(p.astype(BF16), kcvc_ref[u, :, KV_W:2 * KV_W])
            imp = _dot_f32rhs(_dot_f32lhs(gs_ref[...], p), ov_ref[...])
            imp = jnp.where(forced, FORCE, jnp.where(causal, imp, NEG))
            imp = jnp.where(sidx < ns_valid, imp, MASKED)
            sel_ref[u] = _topk_mask_lanes(imp, n_top, ns_valid)
            sw = jnp.where(in_window, _dot(qrbd[u], win_ref[u, 0].astype(BF16)), MASKED)
            st = update(init(), sw, [(0, wlen, win_ref[u, 1].astype(BF16), True)])
            st = new_row_update(st, qrbd[u], kwn_ref[u])
            ow_ref[u] = st[2] * (1.0 / jnp.maximum(st[1], 1e-20))
            put(u, new_row_update(init(), qrbd[u], ksn_ref[u]))

    def halves(pages):
        return [pages[:pg // 2], pages[pg // 2:]] if pg > 1 else [pages]

    scores = [[jnp.concatenate([_dot(qrbd[u], r[0, 0].astype(BF16)) for r in part], axis=1)
               for part in halves(all_pages[u])] for u in range(ns)]
    masks = [_dot(sel_ref[u].astype(BF16), e_ref[0]) > 0.5 for u in range(ns)]
    states = [(m_ref[u, :, 0:1], l_ref[u, :, 0:1], acc_ref[u]) for u in range(ns)]
    k0 = 0
    for hx, part0 in enumerate(halves(all_pages[0])):
        width = len(part0) * PAGE_SIZE
        for u in range(ns):
            part = halves(all_pages[u])[hx]
            s = jnp.where(masks[u][:, k0:k0 + width], scores[u][hx], MASKED)
            vs = [(i * PAGE_SIZE, PAGE_SIZE, r[0, 1].astype(BF16), True) for i, r in enumerate(part)]
            states[u] = update(states[u], s, vs)
        k0 += width
    for u in range(ns):
        put(u, states[u])

    @pl.when(j == pl.num_programs(1) - 1)
    def _():
        for u in range(ns):
            g = g_ref[u]
            os_ = acc_ref[u] * (1.0 / jnp.maximum(l_ref[u, :, 0:1], 1e-20))
            o = g[:, 0:1] * oc_ref[u] + g[:, 1:2] * os_ + g[:, 2:3] * ow_ref[u]
            o = jnp.where(own, o, 0.0)
            out = o[:, 0:HEAD_DIM]
            for h in range(1, N_KV_HEADS):
                out = out + o[:, h * HEAD_DIM:(h + 1) * HEAD_DIM]
            o_ref[u] = out.astype(BF16)


def _attn_sample(pages, pt_flat, q3, qr3, g3, kcvc, ks_new, kw_new, win, ov, e_mat, gsum, *, nb, pages_per_sample,
                 pg, past, nc, ns_valid, n_top):
    nsp = ov.shape[1]
    ns = ATTN_SAMPLES_PER_STEP if nb % ATTN_SAMPLES_PER_STEP == 0 else 1
    per_b = lambda shape: pl.BlockSpec((ns,) + shape, lambda b, j, pt: (b,) + (0,) * len(shape))
    const = lambda shape: pl.BlockSpec(shape, lambda b, j, pt: (0,) * len(shape))

    per_sample = lambda *shape: pltpu.VMEM((ns,) + shape, F32)
    grid_spec = pltpu.PrefetchScalarGridSpec(
        num_scalar_prefetch=1,
        grid=(nb // ns, pages_per_sample // pg),
        in_specs=_page_specs(pg, ns, pages_per_sample) + [
            per_b((N_HEADS, HEAD_DIM)), per_b((N_HEADS, HEAD_DIM)), per_b((N_HEADS, 3)),
            per_b(kcvc.shape[1:]), per_b((1, 2 * KV_W)), per_b((1, 2 * KV_W)), per_b(win.shape[1:]),
            const(ov.shape), pl.BlockSpec((1,) + e_mat.shape[1:], lambda b, j, pt: (j, 0, 0)), const(gsum.shape)],
        out_specs=per_b((N_HEADS, HEAD_DIM)),
        scratch_shapes=[per_sample(N_HEADS, LANES), per_sample(N_HEADS, LANES), per_sample(N_HEADS, KV_W),
                        per_sample(N_HEADS, nsp), per_sample(N_HEADS, KV_W), per_sample(N_HEADS, KV_W)],
    )
    return pl.pallas_call(
        functools.partial(_attn_sample_kernel, pg=pg, ns=ns, past=past, nc=nc, ns_valid=ns_valid, n_top=n_top),
        grid_spec=grid_spec,
        out_shape=jax.ShapeDtypeStruct((nb, N_HEADS, HEAD_DIM), BF16),
        compiler_params=_params("parallel", "arbitrary"),
        name="nsa_attn_sample",
    )(pt_flat, *([pages] * (ns * pg)), q3, qr3, g3, kcvc, ks_new, kw_new, win, ov, e_mat, gsum)


def _rope_tables(pos):
    half = HEAD_DIM // 2
    inv = ROPE_THETA ** (-jnp.arange(half, dtype=F32) / half)
    ang = pos.astype(F32)[:, None] * inv[None, :]
    cos = jnp.cos(ang)
    sin = jnp.sin(ang)
    reps = LANES // HEAD_DIM
    return (jnp.tile(jnp.concatenate([cos, cos], axis=1), (1, reps)),
            jnp.tile(jnp.concatenate([-sin, sin], axis=1), (1, reps)))


def _compress_weights(cmp_w1, cmp_w2, cmp_pe):
    ratio = CMP_BLOCK // CMP_STRIDE
    w1r = cmp_w1.reshape(2, ratio, CMP_STRIDE, HEAD_DIM, HEAD_DIM)
    hh = np.arange(KV_W) // HEAD_DIM
    same_head = jnp.asarray(hh[:, None] == hh[None, :])

    def block_diag(w):
        tiled = jnp.concatenate([jnp.concatenate([w] * N_KV_HEADS, axis=-1)] * N_KV_HEADS, axis=-2)
        return jnp.where(same_head, tiled, 0.0).astype(BF16)

    wbd = block_diag(w1r.reshape(2 * ratio * CMP_STRIDE, HEAD_DIM, HEAD_DIM))
    w2bd = block_diag(cmp_w2)
    pe_r = cmp_pe.reshape(2, ratio, CMP_STRIDE, HEAD_DIM).astype(F32)
    pe_w = jnp.einsum("krsd,krsde->kre", pe_r, w1r.astype(F32), precision=lax.Precision.HIGHEST)
    pe_t = jnp.tile(pe_w.reshape(2 * ratio, HEAD_DIM), (1, N_KV_HEADS))
    return wbd, pe_t, w2bd


def _overlap(nc, ncp, nsel, nsp):
    c_start = np.arange(ncp)[:, None] * CMP_STRIDE
    s_start = np.arange(nsp)[None, :] * SEL_BLOCK
    ov = (c_start < s_start + SEL_BLOCK) & (c_start + CMP_BLOCK > s_start)
    ov = ov & (np.arange(ncp)[:, None] < nc) & (np.arange(nsp)[None, :] < nsel)
    return jnp.asarray(ov, BF16)


def _tile(n, pref):
    t = min(n, pref)
    while n % t:
        t //= 2
    return t


def kernel(x_prompt, x_sample, cache_kv_cmp, cache_kv_sel, cache_kv_win, state_ssm, page_table, norm_g, mlp_w1,
           mlp_w2, nsa_w_in, nsa_w_o, nsa_cmp_w1, nsa_cmp_w2, nsa_cmp_pe, s5_a_re, s5_a_im, s5_log_dt, s5_b_re,
           s5_b_im, s5_c_re, s5_c_im, s5_d, s5_w_glu, s5_b_glu):
    b, t, d = x_prompt.shape
    nb = x_sample.shape[0]
    pages_per_sample = page_table.shape[1]
    past = pages_per_sample * PAGE_SIZE
    rows_p = b * t
    g = norm_g.reshape(norm_g.shape[0], 4, 1, d)

    w_in = nsa_w_in[0]
    w_q = w_in[:, :Q_W].astype(BF16)
    w_kv = w_in[:, Q_W:Q_W + 6 * KV_W].astype(BF16)
    w_gate = jnp.pad(w_in[:, Q_W + 6 * KV_W:], ((0, 0), (0, LANES - 3 * N_HEADS))).astype(BF16)
    w_o = nsa_w_o[0].astype(BF16)
    wbd, pe_t, w2bd = _compress_weights(nsa_cmp_w1[0], nsa_cmp_w2[0], nsa_cmp_pe[0])
    w1 = mlp_w1.astype(BF16)
    w2 = mlp_w2.astype(BF16)
    w_glu = s5_w_glu[0].astype(BF16)
    ops = _s5_operators(s5_a_re[0], s5_a_im[0], s5_log_dt[0], s5_b_re[0], s5_b_im[0], s5_c_re[0], s5_c_im[0])
    d_skip = s5_d[0].reshape(1, d)
    b_glu = s5_b_glu[0].reshape(1, d)

    tm = _tile(rows_p, ROW_TILE)
    ff_chunk = _tile(mlp_w1.shape[2], FF_CHUNK)

    xp = x_prompt.reshape(rows_p, d)
    cos_p, sin_p = _rope_tables(jnp.arange(t, dtype=jnp.int32))
    tq = next(c for c in (2 * LANES, LANES) if t % c == 0 and WINDOW % c == 0)
    (qT, qrT, gT, kvc, kvcT, kvsT, kvwT, ksb, kwb, vsT, vwT) = _inproj(
        xp, g[0, 0], w_q.T, w_kv, w_gate, cos_p, sin_p, tm=_tile(t, ROW_TILE), pos_blocks=t // _tile(t, ROW_TILE),
        transposed=True,
        key_chunk=tq)
    nsb_p = t // CMP_STRIDE
    nc_p = nsb_p - CMP_BLOCK // CMP_STRIDE + 1
    nsel_p = t // SEL_BLOCK
    kc, vcT = _compress_prompt(kvc.reshape(b, t, 2 * KV_W), wbd, pe_t, w2bd)
    ovT = _overlap(nc_p, nsb_p, nsel_p, nsel_p).T
    oT = _attn_prompt(qT, qrT, gT, kc, vcT, ksb, vsT, kwb, vwT, ovT, batch=b, seq=t, tq=tq, nc=nc_p,
                      n_top=min(TOP_N, nsel_p))
    hp, xn1 = _layer_tail((oT, w_o), xp, g[0, 1], g[0, 2], w1[0], w2[0], g[0, 3], g[1, 0], mixer="nsa", tm=tm,
                          ff_chunk=ff_chunk, transposed=True)

    pw = SSM_UNIT * SSM_GROUP
    npair = d // pw
    y3, hfin = _s5_seq(xn1.reshape(b, t, d), ops)
    yc = y3.reshape(rows_p, d)
    (hp,) = _layer_tail((yc, xn1, d_skip, w_glu, b_glu), hp, g[1, 1], g[1, 2], w1[1], w2[1], g[1, 3], None,
                        mixer="s5", tm=tm, ff_chunk=ff_chunk)
    ssm_p = hfin.reshape(npair, 2, b, SSM_UNIT, SSM_STATE).transpose(2, 1, 0, 3, 4)
    ssm_p = ssm_p.reshape(b, 2, d // SSM_GROUP, SSM_STATE)

    xs = x_sample.reshape(nb, d)
    cos_s, sin_s = _rope_tables(jnp.full((nb,), past, dtype=jnp.int32))
    q_s, qr_s, gates_s, kvc_s, kvs_s, kvw_s = _inproj(
        xs, g[0, 0], w_q, w_kv, w_gate, cos_s, sin_s, tm=nb, pos_blocks=1, transposed=False)
    pt_flat = page_table.reshape(-1).astype(jnp.int32)
    pg = _tile(pages_per_sample, PAGE_GROUP)
    n_pool = cache_kv_cmp.shape[1]
    feature_major = lambda c, n, s: c.transpose(0, 2, 3, 4, 1).reshape(n, 2, KV_W, s)
    cmp_pages = feature_major(cache_kv_cmp[0], n_pool, PAGE_SIZE)
    sel_pages = feature_major(cache_kv_sel[0], n_pool, PAGE_SIZE)
    kcvc = _compress_paged(cmp_pages, pt_flat, wbd, pe_t, w2bd, nb=nb, pages_per_sample=pages_per_sample, pg=pg)
    l_all = past + 1
    nsb_s = l_all // CMP_STRIDE
    nc_s = nsb_s - CMP_BLOCK // CMP_STRIDE + 1
    nsel_s = -(-l_all // SEL_BLOCK)
    nsp = -(-nsel_s // LANES) * LANES
    ov_s = _overlap(nc_s, past // CMP_STRIDE, nsel_s, nsp)
    keys_per_step = pg * PAGE_SIZE
    key_blk = (np.arange(past) // SEL_BLOCK).reshape(past // keys_per_step, 1, keys_per_step)
    e_mat = jnp.asarray(np.arange(nsp)[None, :, None] == key_blk, BF16)
    hh = np.arange(N_HEADS)
    gsum = jnp.asarray((hh[:, None] // GROUP) == (hh[None, :] // GROUP), BF16)
    win = feature_major(cache_kv_win[0], nb, WINDOW)
    o_s = _attn_sample(sel_pages, pt_flat, q_s.reshape(nb, N_HEADS, HEAD_DIM), qr_s.reshape(nb, N_HEADS, HEAD_DIM),
                       gates_s[:, :3 * N_HEADS].reshape(nb, N_HEADS, 3), kcvc, kvs_s.reshape(nb, 1, 2 * KV_W),
                       kvw_s.reshape(nb, 1, 2 * KV_W), win, ov_s, e_mat, gsum, nb=nb,
                       pages_per_sample=pages_per_sample, pg=pg, past=past, nc=nc_s, ns_valid=nsel_s,
                       n_top=min(TOP_N, nsel_s))
    hs, xn1_s = _layer_tail((o_s.reshape(nb, Q_W), w_o), xs, g[0, 1], g[0, 2], w1[0], w2[0], g[0, 3], g[1, 0],
                            mixer="nsa", tm=nb, ff_chunk=ff_chunk)

    u2_s = xn1_s.astype(BF16).reshape(nb, npair, pw).transpose(1, 0, 2)
    st = state_ssm[0].reshape(nb, 2, npair, SSM_UNIT * SSM_STATE).transpose(1, 2, 0, 3)
    y2_s, hr_s, hi_s = _s5_step(u2_s, st[0], st[1], ops)
    yc_s = y2_s.transpose(1, 0, 2).reshape(nb, d)
    (hs,) = _layer_tail((yc_s, xn1_s, d_skip, w_glu, b_glu), hs, g[1, 1], g[1, 2], w1[1], w2[1], g[1, 3], None,
                        mixer="s5", tm=nb, ff_chunk=ff_chunk)
    ssm_s = jnp.stack([hr_s, hi_s], axis=0).transpose(2, 0, 1, 3).reshape(nb, 2, d // SSM_GROUP, SSM_STATE)

    kv5 = lambda a, n, s: a.reshape(1, n, s, 2, N_KV_HEADS, HEAD_DIM)
    from_fm = lambda a, n, s: a.reshape(n, 2, N_KV_HEADS, HEAD_DIM, s).transpose(0, 4, 1, 2, 3)[None]
    win_s = jnp.concatenate([win[..., 1:], kvw_s.reshape(nb, 2, KV_W, 1)], axis=-1)
    return (hp.reshape(b, t, d), hs.reshape(nb, 1, d),
            from_fm(kvcT, b, t), kv5(kvc_s, nb, 1), from_fm(kvsT, b, t), kv5(kvs_s, nb, 1),
            from_fm(kvwT[:, :, t - WINDOW:], b, WINDOW), from_fm(win_s, nb, WINDOW), ssm_p[None], ssm_s[None])
```

```python
import functools

import jax
import jax.numpy as jnp
import numpy as np
from jax import lax
from jax.experimental import pallas as pl
from jax.experimental.pallas import tpu as pltpu

N_HEADS = 16
HEAD_DIM = 64
N_KV_HEADS = 4
GROUP = N_HEADS // N_KV_HEADS
CMP_BLOCK = 32
CMP_STRIDE = 16
SEL_BLOCK = 64
TOP_N = 16
WINDOW = 512
ROPE_THETA = 10000.0
PAGE_SIZE = 128
SSM_GROUP = 16
SSM_STATE = 64
SSM_CHUNK = 8
SSM_UNIT = 8
EPS = 1e-6
NEG = -1e30
FORCE = 1e9
MASKED = -1.5e38
LOG2E = 1.4426950408889634
V_ROWS = HEAD_DIM + 16
Q_W = N_HEADS * HEAD_DIM
KV_W = N_KV_HEADS * HEAD_DIM
LANES = 128
SUBLANES = 8
ROW_TILE = 512
FF_CHUNK = 1024
PAGE_GROUP = 16
SAMPLES_PER_STEP = 2
ATTN_SAMPLES_PER_STEP = 4
S5_TIME_BLOCK = 2048
VMEM_LIMIT = 56 * 1024 * 1024

F32 = jnp.float32
BF16 = jnp.bfloat16


def _params(*sem):
    return pltpu.CompilerParams(dimension_semantics=sem, vmem_limit_bytes=VMEM_LIMIT)


def _full(shape):
    zeros = (0,) * len(shape)
    return pl.BlockSpec(shape, lambda *_: zeros)


def _rms(x, g):
    ms = jnp.mean(x * x, axis=-1, keepdims=True)
    return x * lax.rsqrt(ms + EPS) * g


def _dot(a, b):
    return jnp.dot(a, b, preferred_element_type=F32)


def _dot_f32lhs(w, x):
    hi = x.astype(BF16)
    r1 = x - hi.astype(F32)
    mid = r1.astype(BF16)
    lo = (r1 - mid.astype(F32)).astype(BF16)
    return _dot(w, hi) + _dot(w, mid) + _dot(w, lo)


def _dot_f32rhs(x, w):
    hi = x.astype(BF16)
    r1 = x - hi.astype(F32)
    mid = r1.astype(BF16)
    lo = (r1 - mid.astype(F32)).astype(BF16)
    return _dot(hi, w) + _dot(mid, w) + _dot(lo, w)


def _rope_nat(x, cos, sin):
    half = HEAD_DIM // 2
    lane = lax.broadcasted_iota(jnp.int32, (1, LANES), 1)
    first = (lane % HEAD_DIM) < half
    outs = []
    for c in range(x.shape[1] // LANES):
        xc = x[:, c * LANES:(c + 1) * LANES]
        rot = jnp.where(first, pltpu.roll(xc, LANES - half, 1), pltpu.roll(xc, half, 1))
        outs.append(xc * cos + rot * sin)
    return jnp.concatenate(outs, axis=1)


def _inproj_kernel(x_ref, g_ref, wq_ref, wkv_ref, wg_ref, cos_ref, sin_ref, *rest, transposed, key_chunk):
    outs = rest[2:] if transposed else rest
    xb = _rms(x_ref[...], g_ref[...]).astype(BF16)
    cos = cos_ref[...]
    sin = sin_ref[...]
    scale = HEAD_DIM ** -0.5
    kv = _dot(xb, wkv_ref[...])
    gates = jax.nn.sigmoid(_dot(xb, wg_ref[...]))
    k_s = _rope_nat(kv[:, 2 * KV_W:3 * KV_W], cos, sin)
    v_s = kv[:, 3 * KV_W:4 * KV_W]
    k_w = _rope_nat(kv[:, 4 * KV_W:5 * KV_W], cos, sin)
    v_w = kv[:, 5 * KV_W:6 * KV_W]
    kvc_ref = outs[3]
    kvc_ref[...] = kv[:, 0:2 * KV_W]
    if transposed:
        qT_ref, qrT_ref, gT_ref, _, kvcT_ref, kvsT_ref, kvwT_ref, ksb_ref, kwb_ref, vsT_ref, vwT_ref = outs
        qT = _dot_nt(wq_ref[...], xb)
        cosT = rest[0][...]
        sinT = rest[1][...]
        half = HEAD_DIM // 2
        rotated = []
        for h in range(N_HEADS):
            x1 = qT[h * HEAD_DIM:h * HEAD_DIM + half, :]
            x2 = qT[h * HEAD_DIM + half:(h + 1) * HEAD_DIM, :]
            rotated += [x1 * cosT - x2 * sinT, x2 * cosT + x1 * sinT]
        qT_ref[...] = (qT * (scale * LOG2E)).astype(BF16)
        qrT_ref[...] = (jnp.concatenate(rotated, axis=0) * (scale * LOG2E)).astype(BF16)
        gT_ref[...] = gates.T
        tm = x_ref.shape[0]
        rowi = lax.broadcasted_iota(jnp.int32, (tm, LANES), 0)
        lanei = lax.broadcasted_iota(jnp.int32, (tm, LANES), 1)
        blk = lax.shift_right_logical(rowi & (key_chunk - 1), SEL_BLOCK.bit_length() - 1)
        extra = lanei - HEAD_DIM
        nblk = key_chunk // SEL_BLOCK
        aug = jnp.where((extra == blk) | ((extra >= nblk) & (extra < nblk + N_CONST_LANES)), 1.0, 0.0)
        for c in range(KV_W // LANES):
            for k_nat, k_ref in ((k_s, ksb_ref), (k_w, kwb_ref)):
                pair = k_nat[:, c * LANES:(c + 1) * LANES]
                k_ref[2 * c] = jnp.where(lanei < HEAD_DIM, pair, aug).astype(BF16)
                k_ref[2 * c + 1] = jnp.where(lanei < HEAD_DIM, pltpu.roll(pair, HEAD_DIM, 1), aug).astype(BF16)
        v_sT = v_s.T
        v_wT = v_w.T
        ones_rows = jnp.where(lax.broadcasted_iota(jnp.int32, (V_ROWS - HEAD_DIM, tm), 0) == 0, 1.0, 0.0)
        for h in range(N_KV_HEADS):
            for vT, v_ref in ((v_sT, vsT_ref), (v_wT, vwT_ref)):
                v_ref[h * V_ROWS:(h + 1) * V_ROWS, :] = jnp.concatenate(
                    [vT[h * HEAD_DIM:(h + 1) * HEAD_DIM, :], ones_rows], axis=0).astype(BF16)
        kvcT_ref[0] = kv[:, 0:2 * KV_W].T
        kvsT_ref[0, 0:KV_W, :] = k_s.T
        kvsT_ref[0, KV_W:2 * KV_W, :] = v_sT
        kvwT_ref[0, 0:KV_W, :] = k_w.T
        kvwT_ref[0, KV_W:2 * KV_W, :] = v_wT
    else:
        q_ref, qr_ref, gt_ref, _, kvs_ref, kvw_ref = outs
        q = _dot(xb, wq_ref[...])
        qr = _rope_nat(q, cos, sin)
        q_ref[...] = (q * scale).astype(BF16)
        qr_ref[...] = (qr * scale).astype(BF16)
        gt_ref[...] = gates
        kvs_ref[:, 0:KV_W] = k_s
        kvs_ref[:, KV_W:2 * KV_W] = v_s
        kvw_ref[:, 0:KV_W] = k_w
        kvw_ref[:, KV_W:2 * KV_W] = v_w


def _inproj(x, g, w_q, w_kv, w_gate, cos_t, sin_t, *, tm, pos_blocks, transposed, key_chunk=LANES):
    rows, d = x.shape
    assert tm % key_chunk == 0 or not transposed
    n = rows // tm
    row_blk = lambda w: pl.BlockSpec((tm, w), lambda i: (i, 0))
    col_blk = lambda h: pl.BlockSpec((h, tm), lambda i: (0, i))
    tab = pl.BlockSpec((tm, LANES), lambda i: (i % pos_blocks, 0))
    kv_nat = jax.ShapeDtypeStruct((rows, 2 * KV_W), F32)
    if transposed:
        seqs = rows // (pos_blocks * tm)
        kh = pl.BlockSpec((N_KV_HEADS, tm, LANES), lambda i: (0, i, 0))
        kvT = pl.BlockSpec((1, 2 * KV_W, tm), lambda i: (i // pos_blocks, 0, i % pos_blocks))
        out_shape = ([jax.ShapeDtypeStruct((Q_W, rows), BF16)] * 2 + [jax.ShapeDtypeStruct((LANES, rows), F32)]
                     + [kv_nat] + [jax.ShapeDtypeStruct((seqs, 2 * KV_W, pos_blocks * tm), F32)] * 3
                     + [jax.ShapeDtypeStruct((N_KV_HEADS, rows, LANES), BF16)] * 2
                     + [jax.ShapeDtypeStruct((N_KV_HEADS * V_ROWS, rows), BF16)] * 2)
        out_specs = ([col_blk(Q_W)] * 2 + [col_blk(LANES)] + [row_blk(2 * KV_W)] + [kvT] * 3 + [kh] * 2
                     + [col_blk(N_KV_HEADS * V_ROWS)] * 2)
    else:
        out_shape = ([jax.ShapeDtypeStruct((rows, Q_W), BF16)] * 2 + [jax.ShapeDtypeStruct((rows, LANES), F32)]
                     + [kv_nat] * 3)
        out_specs = [row_blk(Q_W)] * 2 + [row_blk(LANES)] + [row_blk(2 * KV_W)] * 3
    tables = [cos_t, sin_t]
    table_specs = [tab, tab]
    if transposed:
        half = HEAD_DIM // 2
        tables += [cos_t[:, 0:half].T, sin_t[:, half:HEAD_DIM].T]
        table_specs += [pl.BlockSpec((half, tm), lambda i: (0, i % pos_blocks))] * 2
    return pl.pallas_call(
        functools.partial(_inproj_kernel, transposed=transposed, key_chunk=key_chunk),
        grid=(n,),
        in_specs=[row_blk(d), _full((1, d)), _full(w_q.shape), _full(w_kv.shape), _full(w_gate.shape)] + table_specs,
        out_specs=out_specs,
        out_shape=out_shape,
        compiler_params=_params("parallel"),
        name="nsa_inproj",
    )(x, g, w_q, w_kv, w_gate, *tables)


_KV_CHUNKS = 2 * KV_W // LANES


def _compress_half(load_rows, kv, nrows, wbd_ref, pe_ref):
    ratio = CMP_BLOCK // CMP_STRIDE
    accs = [jnp.broadcast_to(pe_ref[kv * ratio + r:kv * ratio + r + 1, :], (nrows, KV_W)) for r in range(ratio)]
    for s in range(CMP_STRIDE):
        lhs = load_rows(s, kv).astype(BF16)
        for r in range(ratio):
            accs[r] = accs[r] + _dot(lhs, wbd_ref[(kv * ratio + r) * CMP_STRIDE + s])
    return accs


def _compress_prompt_kernel(*refs, nsb):
    x_refs = refs[:_KV_CHUNKS]
    wbd_ref, pe_ref, w2_ref, kc_ref, vcT_ref, sh_ref = refs[_KV_CHUNKS:]

    def load_rows(s, kv):
        per_half = _KV_CHUNKS // 2
        return jnp.concatenate([x_refs[kv * per_half + c][0, pl.ds(s, nsb, stride=CMP_STRIDE), :]
                                for c in range(per_half)], axis=1)

    sh_ref[nsb:nsb + SUBLANES, :] = jnp.zeros((SUBLANES, KV_W), F32)
    for kv in range(2):
        pr0, pr1 = _compress_half(load_rows, kv, nsb, wbd_ref, pe_ref)
        sh_ref[0:nsb, :] = pr1
        h = pr0 + sh_ref[pl.ds(1, nsb), :]
        out = _dot(jax.nn.gelu(h).astype(BF16), w2_ref[kv])
        if kv == 0:
            for hh in range(N_KV_HEADS):
                kc_ref[0, hh] = out[:, hh * HEAD_DIM:(hh + 1) * HEAD_DIM].astype(BF16)
        else:
            vcT_ref[0] = out.T.astype(BF16)


def _compress_prompt(kvc3, wbd, pe_t, w2bd):
    b, t, _ = kvc3.shape
    nsb = t // CMP_STRIDE
    return pl.pallas_call(
        functools.partial(_compress_prompt_kernel, nsb=nsb),
        grid=(b,),
        in_specs=[pl.BlockSpec((1, t, LANES), lambda i, c=c: (i, 0, c)) for c in range(_KV_CHUNKS)]
        + [_full(wbd.shape), _full(pe_t.shape), _full(w2bd.shape)],
        out_specs=[pl.BlockSpec((1, N_KV_HEADS, nsb, HEAD_DIM), lambda i: (i, 0, 0, 0)),
                   pl.BlockSpec((1, KV_W, nsb), lambda i: (i, 0, 0))],
        out_shape=[jax.ShapeDtypeStruct((b, N_KV_HEADS, nsb, HEAD_DIM), BF16),
                   jax.ShapeDtypeStruct((b, KV_W, nsb), BF16)],
        scratch_shapes=[pltpu.VMEM((nsb + SUBLANES, KV_W), F32)],
        compiler_params=_params("parallel"),
        name="nsa_compress_prompt",
    )(*([kvc3] * _KV_CHUNKS), wbd, pe_t, w2bd)


def _topk_mask_T(imp, n_top):
    ns, w = imp.shape
    nblk = ns // SUBLANES
    blocks = [imp[r * SUBLANES:(r + 1) * SUBLANES, :] for r in range(nblk)]
    cnts = [jnp.zeros((SUBLANES, w), F32) for _ in range(nblk)]
    sub = lax.broadcasted_iota(jnp.int32, (SUBLANES, w), 0)
    for sp in range(ns):
        row = blocks[sp // SUBLANES][sp % SUBLANES:sp % SUBLANES + 1, :]
        for r in range(nblk):
            blk = blocks[r]
            if sp < r * SUBLANES:
                beats = jnp.where(row >= blk, 1.0, 0.0)
            elif sp >= (r + 1) * SUBLANES:
                beats = jnp.where(row > blk, 1.0, 0.0)
            else:
                beats = jnp.where(sub > (sp - r * SUBLANES), jnp.where(row >= blk, 1.0, 0.0),
                                  jnp.where(row > blk, 1.0, 0.0))
            cnts[r] = cnts[r] + beats
    return jnp.concatenate([jnp.where(c < n_top, 1.0, 0.0) for c in cnts], axis=0)


def _online_chunks(states, k_cs, vT_cs, qTs, bias):
    scores = [_dot(k_c, qT) for k_c, qT in zip(k_cs, qTs)]
    mids = []
    for (m, _), s in zip(states, scores):
        if bias is not None:
            s = s + bias
        m_new = jnp.maximum(m, jnp.max(s, axis=0, keepdims=True))
        mids.append((m_new, jnp.exp2(m - m_new), jnp.exp2(s - m_new).astype(BF16)))
    return tuple((m_new, alpha * acc + _dot(vT_c, p))
                 for (m_new, alpha, p), (_, acc), vT_c in zip(mids, states, vT_cs))


def _softmax_finish(carry):
    _, acc = carry
    return acc[0:HEAD_DIM, :] * (1.0 / jnp.maximum(acc[HEAD_DIM:HEAD_DIM + 1, :], 1e-20))


LAZY_LOG2_MAX = 60.0
N_CONST_LANES = 4


def _split3(x):
    hi = x.astype(BF16).astype(F32)
    r = x - hi
    mid = r.astype(BF16).astype(F32)
    return hi, mid, (r - mid).astype(BF16).astype(F32)


def _rescaling_chunks(m_ref, acc_ref, k_cs, vT_cs, qTs, base_tiles, bias):
    nh = len(k_cs)
    rows, w = base_tiles[0].shape
    zero_rows = jnp.zeros((LANES - HEAD_DIM - rows, w), BF16)
    new = _online_chunks(tuple((m_ref[h], acc_ref[h]) for h in range(nh)), k_cs, vT_cs,
                         [jnp.concatenate([qTs[h], base_tiles[h].astype(BF16), zero_rows], axis=0) for h in range(nh)],
                         bias)
    for h in range(nh):
        m_ref[h] = new[h][0]
        acc_ref[h] = new[h][1]


def _lazy_chunks(m_ref, acc2_ref, slot_ref, k_cs, vT_cs, qTs, base_tiles, bias, ref_row, n_chunks=1, first=False):
    slot = slot_ref[0]
    acc_ref = acc2_ref.at[slot]
    nu = len(k_cs)
    nh = nu // n_chunks
    biases = list(bias) if isinstance(bias, (list, tuple)) else [bias] * n_chunks
    rows = base_tiles[0].shape[0]
    w = base_tiles[0].shape[1]
    rowt = lax.broadcasted_iota(jnp.int32, (rows, w), 0)
    zero_rows = jnp.zeros((LANES - HEAD_DIM - rows, w), BF16)

    def queries(u):
        hi, mid, lo = _split3(-m_ref[u % nh])
        tile = jnp.where(rowt == ref_row, hi, jnp.where(rowt == ref_row + 1, mid,
                                                        jnp.where(rowt == ref_row + 2, lo, base_tiles[u])))
        return jnp.concatenate([qTs[u % nh], tile.astype(BF16), zero_rows], axis=0)

    scores = [_dot(k_cs[u], queries(u)) for u in range(nu)]
    peaks = [None] * nh
    probs = []
    for u, s in enumerate(scores):
        if biases[u // nh] is not None:
            s = s + biases[u // nh]
        top = jnp.max(s, axis=0, keepdims=True)
        peaks[u % nh] = top if peaks[u % nh] is None else jnp.maximum(peaks[u % nh], top)
        probs.append(jnp.exp2(s).astype(BF16))
    pvs = [_dot(vT_cs[u], probs[u]) for u in range(nu)]
    for h in range(nh):
        total = pvs[h]
        for c in range(1, n_chunks):
            total = total + pvs[c * nh + h]
        acc2_ref[1 - slot, h] = acc_ref[h] + total
    highest = functools.reduce(jnp.maximum, peaks)
    lowest = functools.reduce(jnp.minimum, peaks)
    in_range = (jnp.max(highest) <= LAZY_LOG2_MAX) & (jnp.logical_not(first) | (jnp.min(lowest) >= -LAZY_LOG2_MAX))

    @pl.when(in_range)
    def _():
        slot_ref[0] = 1 - slot

    @pl.when(jnp.logical_not(in_range))
    def _():
        for h in range(nh):
            m_ref[h] = jnp.where(first, NEG, m_ref[h])
        for c in range(n_chunks):
            part = slice(c * nh, (c + 1) * nh)
            _rescaling_chunks(m_ref, acc_ref, k_cs[part], vT_cs[part], qTs, base_tiles[part], biases[c])


def _attn_prompt_kernel(qT_ref, qrT_ref, gT_ref, kc_ref, vcT_ref, ks_ref, vsT_ref, kw_ref, vwT_ref, ovT_ref,
                        o_ref, selb_ref, oc_ref, ow_ref, m_ref, acc2_ref, slot_ref, *, tq, nc, n_top):
    ck = tq
    i = pl.program_id(1)
    t0 = i * tq
    qpos = t0 + lax.broadcasted_iota(jnp.int32, (1, tq), 1)
    ncp = kc_ref.shape[2]
    ns = ovT_ref.shape[0]
    w = GROUP * tq
    bpc = ck // SEL_BLOCK
    sel_shift = SEL_BLOCK.bit_length() - 1
    kvhs = range(N_KV_HEADS)
    heads = [[kvh * GROUP + g for g in range(GROUP)] for kvh in kvhs]
    rows = [pl.ds(kvh * HEAD_DIM, HEAD_DIM) for kvh in kvhs]
    vrows = [pl.ds(kvh * V_ROWS, V_ROWS) for kvh in kvhs]
    qrT = [jnp.concatenate([qrT_ref[h * HEAD_DIM:(h + 1) * HEAD_DIM, :] for h in heads[kvh]], axis=1)
           for kvh in kvhs]
    kl = lax.broadcasted_iota(jnp.int32, (ck, tq), 0)
    ql = lax.broadcasted_iota(jnp.int32, (ck, tq), 1)
    tile4 = lambda b: jnp.concatenate([b] * GROUP, axis=1)
    key_le_query = tile4(jnp.where(kl <= ql, 0.0, MASKED))
    key_ge_query = tile4(jnp.where(kl >= ql, 0.0, MASKED))
    bias_rows = selb_ref.shape[2]

    def reset_softmax():
        slot_ref[0] = 0
        for kvh in kvhs:
            m_ref[kvh] = jnp.zeros((1, w), F32)
            acc2_ref[0, kvh] = jnp.zeros((V_ROWS, w), F32)

    def current_acc(kvh):
        return acc2_ref[slot_ref[0], kvh]

    cidx = lax.broadcasted_iota(jnp.int32, (ncp, 1), 0)
    valid = ((cidx * CMP_STRIDE + (CMP_BLOCK - 1)) <= qpos) & (cidx < nc)
    sidx = lax.broadcasted_iota(jnp.int32, (ns, 1), 0)
    cur = lax.shift_right_logical(qpos, sel_shift)
    forced = (sidx == 0) | (sidx == cur) | (sidx == cur - 1)
    causal = (sidx * SEL_BLOCK) <= qpos
    cmp_scores = [
        _dot(kc_ref[0, kvh], jnp.concatenate([qT_ref[h * HEAD_DIM:(h + 1) * HEAD_DIM, :] for h in heads[kvh]], axis=1))
        for kvh in kvhs]
    importance = []
    for kvh in kvhs:
        s = cmp_scores[kvh]
        probs = []
        for g in range(GROUP):
            sm = jnp.where(valid, s[:, g * tq:(g + 1) * tq], NEG)
            mx = jnp.max(sm, axis=0, keepdims=True)
            e = jnp.where(valid, jnp.exp2(sm - mx), 0.0)
            den = jnp.maximum(jnp.sum(e, axis=0, keepdims=True), 1e-20)
            probs.append(e / den)
        oc_ref[kvh] = _dot(vcT_ref[0, rows[kvh], :], jnp.concatenate(probs, axis=1).astype(BF16))
        psum = probs[0]
        for g in range(1, GROUP):
            psum = psum + probs[g]
        imp = _dot_f32lhs(ovT_ref[...], psum)
        importance.append(jnp.where(forced, FORCE, jnp.where(causal, imp, NEG)))

    prefixes = sorted({r for r in (ns // 4, ns // 2, 3 * ns // 4) if r and r % SUBLANES == 0} | {ns})
    needed = (i + 1) * bpc
    fill = jnp.zeros((bias_rows - bpc, tq), F32)
    for lo, hi in zip([0] + prefixes[:-1], prefixes):
        @pl.when((needed > lo) & (needed <= hi))
        def _(hi=hi):
            for kvh in kvhs:
                selb = (_topk_mask_T(importance[kvh][0:hi, :], min(n_top, hi)) - 1.0) * (-MASKED)
                for c in range(hi // bpc):
                    selb_ref[kvh, c] = jnp.concatenate([selb[c * bpc:(c + 1) * bpc, :], fill], axis=0)

    n_back = WINDOW // ck
    reset_softmax()
    rowb = lax.broadcasted_iota(jnp.int32, (bias_rows, w), 0)
    def window_chunk(r):
        a = i - n_back + r
        kst = pl.multiple_of(jnp.maximum(a, 0) * ck, ck)
        skip = jnp.where(a < 0, MASKED, 0.0)
        tile = jnp.where(rowb == bpc, skip, 0.0)
        bias = key_le_query if r == n_back else (key_ge_query if r == 0 else None)
        return ([kw_ref[kvh, pl.ds(kst, ck), :] for kvh in kvhs],
                [vwT_ref[vrows[kvh], pl.ds(kst, ck)] for kvh in kvhs], [tile] * N_KV_HEADS, bias)

    order = [n_back] + list(range(n_back))
    for g0 in [0] + list(range(1, n_back + 1, 2)):
        group = [window_chunk(r) for r in (order[g0:g0 + 1] if g0 == 0 else order[g0:g0 + 2])]
        _lazy_chunks(m_ref, acc2_ref, slot_ref, sum((g[0] for g in group), []), sum((g[1] for g in group), []),
                     qrT, sum((g[2] for g in group), []), [g[3] for g in group], bpc + 1, len(group), first=g0 == 0)
    for kvh in kvhs:
        ow_ref[kvh] = _softmax_finish((None, current_acc(kvh)))

    def chunk_step(c, diagonal, n_chunks=1, first=False):
        ks, vs, tiles = [], [], []
        for dc in range(n_chunks):
            kst = (c + dc) * ck if isinstance(c, int) else pl.multiple_of((c + dc) * ck, ck)
            ks += [ks_ref[kvh, pl.ds(kst, ck), :] for kvh in kvhs]
            vs += [vsT_ref[vrows[kvh], pl.ds(kst, ck)] for kvh in kvhs]
            tiles += [tile4(selb_ref[kvh, c + dc]) for kvh in kvhs]
        _lazy_chunks(m_ref, acc2_ref, slot_ref, ks, vs, qrT, tiles, key_le_query if diagonal else None, bpc + 1,
                     n_chunks, first)

    reset_softmax()

    @pl.when(i > 0)
    def _():
        chunk_step(0, False, first=True)

    n_mid = jnp.maximum(i - 1, 0)

    def loop_body(p, carry):
        chunk_step(1 + 2 * p, False, n_chunks=2)
        return carry

    lax.fori_loop(0, n_mid // 2, loop_body, 0)

    @pl.when(n_mid % 2 == 1)
    def _():
        chunk_step(i - 1, False)

    chunk_step(i, True, first=i == 0)

    for kvh in kvhs:
        def gate_row(j, kvh=kvh):
            return jnp.concatenate([gT_ref[h * 3 + j:h * 3 + j + 1, :] for h in heads[kvh]], axis=1)

        oT = (gate_row(0) * oc_ref[kvh] + gate_row(1) * _softmax_finish((None, current_acc(kvh)))
              + gate_row(2) * ow_ref[kvh])
        for g, h in enumerate(heads[kvh]):
            o_ref[h * HEAD_DIM:(h + 1) * HEAD_DIM, :] = oT[:, g * tq:(g + 1) * tq].astype(BF16)


def _attn_prompt(qT, qrT, gT, kc, vcT, ksb, vsT, kwb, vwT, ovT, *, batch, seq, tq, nc, n_top):
    nq = seq // tq
    nsb = kc.shape[2]
    ns = ovT.shape[0]
    col = lambda h: pl.BlockSpec((h, tq), lambda b, i: (0, b * nq + i))
    kh = pl.BlockSpec((N_KV_HEADS, seq, LANES), lambda b, i: (0, b, 0))
    vt = pl.BlockSpec((N_KV_HEADS * V_ROWS, seq), lambda b, i: (0, b))
    bf16_sublanes = 2 * SUBLANES
    return pl.pallas_call(
        functools.partial(_attn_prompt_kernel, tq=tq, nc=nc, n_top=n_top),
        grid=(batch, nq),
        in_specs=[col(Q_W), col(Q_W), col(LANES),
                  pl.BlockSpec((1, N_KV_HEADS, nsb, HEAD_DIM), lambda b, i: (b, 0, 0, 0)),
                  pl.BlockSpec((1, KV_W, nsb), lambda b, i: (b, 0, 0)),
                  kh, vt, kh, vt, _full(ovT.shape)],
        out_specs=col(Q_W),
        out_shape=jax.ShapeDtypeStruct((Q_W, batch * seq), BF16),
        scratch_shapes=[pltpu.VMEM((N_KV_HEADS, ns * SEL_BLOCK // tq, bf16_sublanes, tq), F32),
                        pltpu.VMEM((N_KV_HEADS, HEAD_DIM, GROUP * tq), F32),
                        pltpu.VMEM((N_KV_HEADS, HEAD_DIM, GROUP * tq), F32),
                        pltpu.VMEM((N_KV_HEADS, 1, GROUP * tq), F32),
                        pltpu.VMEM((2, N_KV_HEADS, V_ROWS, GROUP * tq), F32),
                        pltpu.SMEM((1,), jnp.int32)],
        compiler_params=_params("parallel", "arbitrary"),
        name="nsa_attn_prompt",
    )(qT, qrT, gT, kc, vcT, ksb, vsT, kwb, vwT, ovT)


def _resident(shape):
    zeros = (0,) * len(shape)
    return pl.BlockSpec(shape, lambda *_: zeros, pipeline_mode=pl.Buffered(1))


def _layer_tail_kernel(*refs, mixer, transposed, ff_chunk, next_norm):
    if mixer == "nsa":
        o_ref, wo_ref, x_ref, g1_ref, g2_ref = refs[:5]
        rest = refs[5:]
        if transposed:
            y = lax.dot_general(o_ref[...], wo_ref[...], (((0,), (0,)), ((), ())), preferred_element_type=F32)
        else:
            y = _dot(o_ref[...], wo_ref[...])
    else:
        yc_ref, u_ref, d_ref, wg_ref, bg_ref, x_ref, g1_ref, g2_ref = refs[:8]
        rest = refs[8:]
        z = jax.nn.gelu(yc_ref[...] + d_ref[...] * u_ref[...])
        y = z * jax.nn.sigmoid(_dot(z.astype(BF16), wg_ref[...]) + bg_ref[...])
    w1_ref, w2_ref, g3_ref = rest[:3]
    if next_norm:
        gn_ref, h2_ref, xn_ref, acc_ref = rest[3:]
    else:
        h2_ref, acc_ref = rest[3:]
    h = x_ref[...] + _rms(y, g1_ref[...])
    xm = _rms(h, g2_ref[...]).astype(BF16)
    for c in range(w1_ref.shape[1] // ff_chunk):
        cols = slice(c * ff_chunk, (c + 1) * ff_chunk)
        hm = jnp.maximum(_dot(xm, w1_ref[:, cols]), 0.0)
        part = _dot((hm * hm).astype(BF16), w2_ref[cols, :])
        if c == 0:
            acc_ref[...] = part
        else:
            acc_ref[...] += part
    h2 = h + _rms(acc_ref[...], g3_ref[...])
    h2_ref[...] = h2
    if next_norm:
        xn_ref[...] = _rms(h2, gn_ref[...])


def _layer_tail(mixer_args, x, g1, g2, w1, w2, g3, gn, *, mixer, tm, ff_chunk, transposed=False):
    rows, d = x.shape
    row = pl.BlockSpec((tm, d), lambda i: (i, 0))
    vec = _resident((1, d))
    if mixer == "nsa":
        o, w_o = mixer_args
        o_spec = (pl.BlockSpec((Q_W, tm), lambda i: (0, i)) if transposed
                  else pl.BlockSpec((tm, Q_W), lambda i: (i, 0)))
        head_specs = [o_spec, _resident(w_o.shape)]
    else:
        w_glu = mixer_args[3]
        head_specs = [row, row, vec, _resident(w_glu.shape), vec]
    tail_args = (w1, w2, g3) + (() if gn is None else (gn,))
    tail_specs = [_resident(w1.shape), _resident(w2.shape), vec] + ([] if gn is None else [vec])
    n_out = 1 if gn is None else 2
    return pl.pallas_call(
        functools.partial(_layer_tail_kernel, mixer=mixer, transposed=transposed, ff_chunk=ff_chunk,
                          next_norm=gn is not None),
        grid=(rows // tm,),
        in_specs=head_specs + [row, vec, vec] + tail_specs,
        out_specs=[row] * n_out,
        out_shape=[jax.ShapeDtypeStruct((rows, d), F32)] * n_out,
        scratch_shapes=[pltpu.VMEM((tm, d), F32)],
        compiler_params=_params("parallel"),
        name=mixer + "_layer_tail",
    )(*mixer_args, x, g1, g2, *tail_args)


def _s5_seq_kernel(x_ref, brow_ref, pre_ref, pim_ref, qre_ref, qim_ref, are_ref, aim_ref, y_ref, hfin_ref,
                   sre, sim, hre, him, cre, cim):
    nb, tb, uw = x_ref.shape
    nt = pre_ref.shape[1] // uw
    nk = tb // nt

    @pl.when(pl.program_id(1) == 0)
    def _():
        cre[...] = jnp.zeros(cre.shape, F32)
        cim[...] = jnp.zeros(cim.shape, F32)

    u = [jnp.concatenate([x_ref[b, pl.ds(t, nk, stride=nt), :] for b in range(nb)], axis=0).astype(BF16)
         for t in range(nt)]
    u2 = [jnp.concatenate([u[2 * j], u[2 * j + 1]], axis=1) for j in range(nt // 2)]
    s_re = _dot(u2[0], pre_ref[0, 0:2 * uw, :])
    s_im = _dot(u2[0], pim_ref[0, 0:2 * uw, :])
    for j in range(1, nt // 2):
        s_re = s_re + _dot(u2[j], pre_ref[0, 2 * j * uw:(2 * j + 2) * uw, :])
        s_im = s_im + _dot(u2[j], pim_ref[0, 2 * j * uw:(2 * j + 2) * uw, :])
    sre[...] = s_re
    sim[...] = s_im
    ar = are_ref[0]
    ai = aim_ref[0]

    within = []
    for t2 in range(0, nt, 2):
        acc = None
        for j in range(t2 // 2 + 1):
            lag0 = t2 - 2 * j + 1
            wpair = jnp.concatenate([brow_ref[0, :, lag0 * uw:(lag0 + 2) * uw],
                                     brow_ref[0, :, (lag0 - 1) * uw:(lag0 + 1) * uw]], axis=0)
            part = _dot(u2[j], wpair)
            acc = part if acc is None else acc + part
        within.append(acc)

    def body(it, carry):
        out = []
        for b in range(nb):
            hr, hi = carry[b]
            r0 = b * nk + it * SUBLANES
            sr8 = sre[pl.ds(r0, SUBLANES), :]
            si8 = sim[pl.ds(r0, SUBLANES), :]
            prev_r, prev_i = [], []
            for j in range(SUBLANES):
                prev_r.append(hr)
                prev_i.append(hi)
                hr, hi = (ar * hr - ai * hi + sr8[j:j + 1, :], ar * hi + ai * hr + si8[j:j + 1, :])
            hre[pl.ds(r0, SUBLANES), :] = jnp.concatenate(prev_r, axis=0)
            him[pl.ds(r0, SUBLANES), :] = jnp.concatenate(prev_i, axis=0)
            out.append((hr, hi))
        return tuple(out)

    fin = tuple((cre[b:b + 1, :], cim[b:b + 1, :]) for b in range(nb))
    for it in range(nk // SUBLANES):
        fin = body(it, fin)
    for b in range(nb):
        cre[b:b + 1, :] = fin[b][0]
        cim[b:b + 1, :] = fin[b][1]
    hfin_ref[0, 0] = cre[...]
    hfin_ref[0, 1] = cim[...]

    hb_re = hre[...].astype(BF16)
    hb_im = him[...].astype(BF16)
    for t2 in range(0, nt, 2):
        cols = slice(t2 * uw, (t2 + 2) * uw)
        acc = within[t2 // 2] + _dot(hb_re, qre_ref[0, :, cols]) + _dot(hb_im, qim_ref[0, :, cols])
        for b in range(nb):
            for dt in range(2):
                y_ref[b, pl.ds(t2 + dt, nk, stride=nt), :] = acc[b * nk:(b + 1) * nk, dt * uw:(dt + 1) * uw]


def _s5_seq(x3, ops):
    nb, t, d = x3.shape
    uw = SSM_UNIT * SSM_GROUP
    nunit = d // uw
    sw = ops["p_re"].shape[2]
    tb = _tile(t, S5_TIME_BLOCK)
    nk = tb // SSM_CHUNK
    per_unit = lambda a: pl.BlockSpec((1,) + a.shape[1:], lambda i, r: (i,) + (0,) * (a.ndim - 1))
    blk = pl.BlockSpec((nb, tb, uw), lambda i, r: (0, r, i))
    args = (x3, ops["brow"], ops["p_re"], ops["p_im"], ops["q_re"], ops["q_im"], ops["a_chunk_re"], ops["a_chunk_im"])
    return pl.pallas_call(
        _s5_seq_kernel,
        grid=(nunit, t // tb),
        in_specs=[blk] + [per_unit(a) for a in args[1:]],
        out_specs=[blk, pl.BlockSpec((1, 2, nb, sw), lambda i, r: (i, 0, 0, 0))],
        out_shape=[jax.ShapeDtypeStruct((nb, t, d), F32), jax.ShapeDtypeStruct((nunit, 2, nb, sw), F32)],
        scratch_shapes=[pltpu.VMEM((nb * nk, sw), F32)] * 4 + [pltpu.VMEM((nb, sw), F32)] * 2,
        compiler_params=_params("parallel", "arbitrary"),
        name="s5_seq_scan",
    )(*args)


def _s5_step_kernel(u_ref, h0r_ref, h0i_ref, bre_ref, bim_ref, cre_ref, cim_ref, are_ref, aim_ref,
                    y_ref, hr_ref, hi_ref, *, npair):
    for p in range(npair):
        u = u_ref[p]
        ar = are_ref[p]
        ai = aim_ref[p]
        h0r = h0r_ref[p]
        h0i = h0i_ref[p]
        hr = ar * h0r - ai * h0i + _dot(u, bre_ref[p])
        hi = ar * h0i + ai * h0r + _dot(u, bim_ref[p])
        hr_ref[p] = hr
        hi_ref[p] = hi
        y_ref[p] = _dot(hr.astype(BF16), cre_ref[p]) + _dot(hi.astype(BF16), cim_ref[p])


def _s5_step(u2, h0r, h0i, ops):
    npair, rows, width = u2.shape
    args = (u2, h0r, h0i, ops["b1_re"], ops["b1_im"], ops["c1_re"], ops["c1_im"], ops["a1_re"], ops["a1_im"])
    return pl.pallas_call(
        functools.partial(_s5_step_kernel, npair=npair),
        grid=(1,),
        in_specs=[_full(a.shape) for a in args],
        out_specs=[_full((npair, rows, width)), _full(h0r.shape), _full(h0r.shape)],
        out_shape=[jax.ShapeDtypeStruct((npair, rows, width), F32), jax.ShapeDtypeStruct(h0r.shape, F32),
                   jax.ShapeDtypeStruct(h0r.shape, F32)],
        compiler_params=_params("arbitrary"),
        name="s5_single_step",
    )(*args)


def _s5_operators(a_re, a_im, log_dt, b_re, b_im, c_re, c_im):
    hp = lax.Precision.HIGHEST
    g, n = a_re.shape
    gu = SSM_UNIT
    nunit = g // gu
    L = SSM_CHUNK
    uw = gu * SSM_GROUP
    sw = gu * n
    a = lax.complex(a_re.astype(F32), a_im.astype(F32))
    dt = jnp.exp(log_dt.astype(F32))[:, None]
    a_bar = jnp.exp(a * dt)
    b_bar = ((a_bar - 1.0) / a)[:, :, None] * lax.complex(b_re.astype(F32), b_im.astype(F32))
    c = lax.complex(c_re.astype(F32), c_im.astype(F32))
    pows = [jnp.ones_like(a_bar)]
    for _ in range(L):
        pows.append(pows[-1] * a_bar)
    a_pow = jnp.stack(pows).reshape(L + 1, nunit, sw)
    apr = jnp.real(a_pow)
    api = jnp.imag(a_pow)
    eye = jnp.eye(gu, dtype=F32)

    def bd_in(x):
        return jnp.einsum("pgnd,gh->pgdhn", x.reshape(nunit, gu, n, SSM_GROUP), eye).reshape(nunit, uw, sw)

    def bd_out(x):
        return jnp.einsum("pgcn,gh->pgnhc", x.reshape(nunit, gu, SSM_GROUP, n), eye).reshape(nunit, sw, uw)

    p0r, p0i = bd_in(jnp.real(b_bar)), bd_in(jnp.imag(b_bar))
    q0r, q0i = bd_out(jnp.real(c)), bd_out(jnp.imag(c))
    lanes = lambda x, k: x[k][:, None, :]
    rows_ = lambda x, k: x[k][:, :, None]
    p_re = jnp.concatenate([p0r * lanes(apr, L - 1 - t) - p0i * lanes(api, L - 1 - t) for t in range(L)], axis=1)
    p_im = jnp.concatenate([p0r * lanes(api, L - 1 - t) + p0i * lanes(apr, L - 1 - t) for t in range(L)], axis=1)
    q_re = jnp.concatenate([q0r * rows_(apr, t + 1) - q0i * rows_(api, t + 1) for t in range(L)], axis=2)
    q_im = jnp.concatenate([-(q0r * rows_(api, t + 1) + q0i * rows_(apr, t + 1)) for t in range(L)], axis=2)
    lag_blocks = [jnp.zeros((nunit, uw, uw), F32)]
    for t in range(L):
        xr = p0r * lanes(apr, t) - p0i * lanes(api, t)
        xi = p0r * lanes(api, t) + p0i * lanes(apr, t)
        lag_blocks.append(jnp.einsum("pus,psv->puv", xr, q0r, precision=hp)
                          - jnp.einsum("pus,psv->puv", xi, q0i, precision=hp))
    brow = jnp.concatenate(lag_blocks, axis=2)
    lane_row = lambda x: x.reshape(nunit, 1, sw)
    return {
        "brow": brow.astype(BF16),
        "p_re": p_re.astype(BF16), "p_im": p_im.astype(BF16),
        "q_re": q_re.astype(BF16), "q_im": q_im.astype(BF16),
        "a_chunk_re": lane_row(apr[L]), "a_chunk_im": lane_row(api[L]),
        "a1_re": lane_row(apr[1]), "a1_im": lane_row(api[1]),
        "b1_re": p0r.astype(BF16), "b1_im": p0i.astype(BF16),
        "c1_re": q0r.astype(BF16), "c1_im": (-q0i).astype(BF16),
    }


def _compress_paged_kernel(pt_ref, *refs, pg, ns, nsb):
    del pt_ref
    pages = refs[:ns * pg]
    perm_ref, wbd_ref, pe_ref, w2_ref, out_ref, h0_ref, h1_ref = refs[ns * pg:]
    j = pl.program_id(1)
    sbp = PAGE_SIZE // CMP_STRIDE
    nrows = pg * sbp
    pair_rows = 2 * sbp

    @pl.when(j == 0)
    def _():
        h1_ref[:, :, nsb:nsb + SUBLANES, :] = jnp.zeros((ns, 2, SUBLANES, KV_W), F32)

    r0 = pl.multiple_of(j * nrows, nrows)
    for kv in range(2):
        staged = []
        for q in range(ns * pg // 2):
            z = jnp.concatenate([pages[2 * q][0, kv], pages[2 * q + 1][0, kv]], axis=1).astype(BF16)
            staged.append(_dot_nt(perm_ref[...], z).astype(BF16))

        def load_rows(s, kv, staged=staged):
            return jnp.concatenate([x[s * pair_rows:(s + 1) * pair_rows, :] for x in staged], axis=0)

        pr0, pr1 = _compress_half(load_rows, kv, ns * nrows, wbd_ref, pe_ref)
        for u in range(ns):
            h0_ref[u, kv, pl.ds(r0, nrows), :] = pr0[u * nrows:(u + 1) * nrows, :]
            h1_ref[u, kv, pl.ds(r0, nrows), :] = pr1[u * nrows:(u + 1) * nrows, :]

    @pl.when(j == pl.num_programs(1) - 1)
    def _():
        for u in range(ns):
            for kv in range(2):
                h = h0_ref[u, kv] + h1_ref[u, kv, pl.ds(1, nsb), :]
                out = _dot(jax.nn.gelu(h).astype(BF16), w2_ref[kv])
                out_ref[u, :, kv * KV_W:(kv + 1) * KV_W] = out.astype(BF16)


def _page_specs(pg, ns, pages_per_sample):
    def spec(u, i):
        return pl.BlockSpec((1, 2, KV_W, PAGE_SIZE),
                            lambda b, j, pt: (pt[(b * ns + u) * pages_per_sample + j * pg + i], 0, 0, 0))
    return [spec(u, i) for u in range(ns) for i in range(pg)]


def _compress_paged(pages, pt_flat, wbd, pe_t, w2bd, *, nb, pages_per_sample, pg):
    nsb = pages_per_sample * PAGE_SIZE // CMP_STRIDE
    sbp = PAGE_SIZE // CMP_STRIDE
    ns = SAMPLES_PER_STEP if nb % SAMPLES_PER_STEP == 0 else 1
    i_out = np.arange(2 * PAGE_SIZE)
    s_i, pg_i, n_i = i_out // (2 * sbp), (i_out // sbp) % 2, i_out % sbp
    perm = jnp.asarray(i_out[None, :] == (pg_i * PAGE_SIZE + n_i * CMP_STRIDE + s_i)[:, None], BF16)
    c3 = lambda shape: pl.BlockSpec(shape, lambda b, j, pt: (0,) * len(shape))
    grid_spec = pltpu.PrefetchScalarGridSpec(
        num_scalar_prefetch=1,
        grid=(nb // ns, pages_per_sample // pg),
        in_specs=_page_specs(pg, ns, pages_per_sample) + [c3(perm.shape), c3(wbd.shape), c3(pe_t.shape),
                                                          c3(w2bd.shape)],
        out_specs=pl.BlockSpec((ns, nsb, 2 * KV_W), lambda b, j, pt: (b, 0, 0)),
        scratch_shapes=[pltpu.VMEM((ns, 2, nsb, KV_W), F32), pltpu.VMEM((ns, 2, nsb + SUBLANES, KV_W), F32)],
    )
    return pl.pallas_call(
        functools.partial(_compress_paged_kernel, pg=pg, ns=ns, nsb=nsb),
        grid_spec=grid_spec,
        out_shape=jax.ShapeDtypeStruct((nb, nsb, 2 * KV_W), BF16),
        compiler_params=_params("parallel", "arbitrary"),
        name="nsa_compress_paged",
    )(pt_flat, *([pages] * (ns * pg)), perm, wbd, pe_t, w2bd)


def _topk_mask_lanes(imp, n_top, ns_valid):
    lane = lax.broadcasted_iota(jnp.int32, imp.shape, 1)
    cnt = jnp.zeros(imp.shape, F32)
    for sp in range(ns_valid):
        col = imp[:, sp:sp + 1]
        cnt = cnt + jnp.where(lane > sp, jnp.where(col >= imp, 1.0, 0.0), jnp.where(col > imp, 1.0, 0.0))
    return jnp.where((cnt < n_top) & (lane < ns_valid), 1.0, 0.0)


def _dot_nt(a, b):
    return lax.dot_general(a, b, (((1,), (1,)), ((), ())), preferred_element_type=F32)


def _attn_sample_kernel(pt_ref, *refs, pg, ns, past, nc, ns_valid, n_top):
    del pt_ref
    all_pages = [refs[u * pg:(u + 1) * pg] for u in range(ns)]
    (q_ref, qr_ref, g_ref, kcvc_ref, ksn_ref, kwn_ref, win_ref, ov_ref, e_ref, gs_ref, o_ref,
     m_ref, l_ref, acc_ref, sel_ref, oc_ref, ow_ref) = refs[ns * pg:]
    j = pl.program_id(1)
    ncp = kcvc_ref.shape[1]
    nsp = ov_ref.shape[1]
    wlen = win_ref.shape[3]
    row = lax.broadcasted_iota(jnp.int32, (N_HEADS, KV_W), 0)
    lane = lax.broadcasted_iota(jnp.int32, (N_HEADS, KV_W), 1)
    own = (lane // HEAD_DIM) == (row // GROUP)

    def spread(ref, u):
        q = ref[u]
        return jnp.where(own, jnp.concatenate([q] * N_KV_HEADS, axis=1), jnp.zeros((N_HEADS, KV_W), BF16))

    def update(state, s, vs):
        m, l, acc = state
        m_new = jnp.maximum(m, jnp.max(s, axis=1, keepdims=True))
        alpha = jnp.exp(m - m_new)
        p = jnp.exp(s - m_new)
        l = alpha * l + jnp.sum(p, axis=1, keepdims=True)
        pv = None
        for st, sz, v, feature_major in vs:
            pb = p[:, st:st + sz].astype(BF16)
            t = _dot_nt(pb, v) if feature_major else _dot(pb, v)
            pv = t if pv is None else pv + t
        return m_new, l, alpha * acc + pv

    def init():
        return (jnp.full((N_HEADS, 1), NEG, F32), jnp.zeros((N_HEADS, 1), F32), jnp.zeros((N_HEADS, KV_W), F32))

    def new_row_update(state, qbd, new_row):
        r8 = lax.broadcasted_iota(jnp.int32, (SUBLANES, 2 * KV_W), 0)
        tile = jnp.where(r8 == 0, jnp.broadcast_to(new_row, (SUBLANES, 2 * KV_W)), 0.0).astype(BF16)
        s = _dot_nt(qbd, tile[:, 0:KV_W])
        l8 = lax.broadcasted_iota(jnp.int32, (N_HEADS, SUBLANES), 1)
        s = jnp.where(l8 == 0, s, MASKED)
        return update(state, s, [(0, SUBLANES, tile[:, KV_W:2 * KV_W], False)])

    def put(u, state):
        m, l, acc = state
        m_ref[u] = jnp.broadcast_to(m, m_ref.shape[1:])
        l_ref[u] = jnp.broadcast_to(l, l_ref.shape[1:])
        acc_ref[u] = acc

    qrbd = [spread(qr_ref, u) for u in range(ns)]

    @pl.when(j == 0)
    def _():
        cidx = lax.broadcasted_iota(jnp.int32, (1, ncp), 1)
        valid = ((cidx * CMP_STRIDE + (CMP_BLOCK - 1)) <= past) & (cidx < nc)
        sidx = lax.broadcasted_iota(jnp.int32, (1, nsp), 1)
        cur = past // SEL_BLOCK
        forced = (sidx == 0) | (sidx == cur) | (sidx == cur - 1)
        causal = (sidx * SEL_BLOCK) <= past
        wpos = past - wlen + lax.broadcasted_iota(jnp.int32, (1, wlen), 1)
        in_window = (wpos >= 0) & (past - wpos <= WINDOW)
        for u in range(ns):
            s = _dot_nt(spread(q_ref, u), kcvc_ref[u, :, 0:KV_W])
            sm = jnp.where(valid, s, NEG)
            mx = jnp.max(sm, axis=1, keepdims=True)
            e = jnp.where(valid, jnp.exp(sm - mx), 0.0)
            p = e / jnp.maximum(jnp.sum(e, axis=1, keepdims=True), 1e-20)
            oc_ref[u] = _dot(p.astype(BF16), kcvc_ref[u, :, KV_W:2 * KV_W])
            imp = _dot_f32rhs(_dot_f32lhs(gs_ref[...], p), ov_ref[...])
            imp = jnp.where(forced, FORCE, jnp.where(causal, imp, NEG))
            imp = jnp.where(sidx < ns_valid, imp, MASKED)
            sel_ref[u] = _topk_mask_lanes(imp, n_top, ns_valid)
            sw = jnp.where(in_window, _dot(qrbd[u], win_ref[u, 0].astype(BF16)), MASKED)
            st = update(init(), sw, [(0, wlen, win_ref[u, 1].astype(BF16), True)])
            st = new_row_update(st, qrbd[u], kwn_ref[u])
            ow_ref[u] = st[2] * (1.0 / jnp.maximum(st[1], 1e-20))
            put(u, new_row_update(init(), qrbd[u], ksn_ref[u]))

    def halves(pages):
        return [pages[:pg // 2], pages[pg // 2:]] if pg > 1 else [pages]

    scores = [[jnp.concatenate([_dot(qrbd[u], r[0, 0].astype(BF16)) for r in part], axis=1)
               for part in halves(all_pages[u])] for u in range(ns)]
    masks = [_dot(sel_ref[u].astype(BF16), e_ref[0]) > 0.5 for u in range(ns)]
    states = [(m_ref[u, :, 0:1], l_ref[u, :, 0:1], acc_ref[u]) for u in range(ns)]
    k0 = 0
    for hx, part0 in enumerate(halves(all_pages[0])):
        width = len(part0) * PAGE_SIZE
        for u in range(ns):
            part = halves(all_pages[u])[hx]
            s = jnp.where(masks[u][:, k0:k0 + width], scores[u][hx], MASKED)
            vs = [(i * PAGE_SIZE, PAGE_SIZE, r[0, 1].astype(BF16), True) for i, r in enumerate(part)]
            states[u] = update(states[u], s, vs)
        k0 += width
    for u in range(ns):
        put(u, states[u])

    @pl.when(j == pl.num_programs(1) - 1)
    def _():
        for u in range(ns):
            g = g_ref[u]
            os_ = acc_ref[u] * (1.0 / jnp.maximum(l_ref[u, :, 0:1], 1e-20))
            o = g[:, 0:1] * oc_ref[u] + g[:, 1:2] * os_ + g[:, 2:3] * ow_ref[u]
            o = jnp.where(own, o, 0.0)
            out = o[:, 0:HEAD_DIM]
            for h in range(1, N_KV_HEADS):
                out = out + o[:, h * HEAD_DIM:(h + 1) * HEAD_DIM]
            o_ref[u] = out.astype(BF16)


def _attn_sample(pages, pt_flat, q3, qr3, g3, kcvc, ks_new, kw_new, win, ov, e_mat, gsum, *, nb, pages_per_sample,
                 pg, past, nc, ns_valid, n_top):
    nsp = ov.shape[1]
    ns = ATTN_SAMPLES_PER_STEP if nb % ATTN_SAMPLES_PER_STEP == 0 else 1
    per_b = lambda shape: pl.BlockSpec((ns,) + shape, lambda b, j, pt: (b,) + (0,) * len(shape))
    const = lambda shape: pl.BlockSpec(shape, lambda b, j, pt: (0,) * len(shape))

    per_sample = lambda *shape: pltpu.VMEM((ns,) + shape, F32)
    grid_spec = pltpu.PrefetchScalarGridSpec(
        num_scalar_prefetch=1,
        grid=(nb // ns, pages_per_sample // pg),
        in_specs=_page_specs(pg, ns, pages_per_sample) + [
            per_b((N_HEADS, HEAD_DIM)), per_b((N_HEADS, HEAD_DIM)), per_b((N_HEADS, 3)),
            per_b(kcvc.shape[1:]), per_b((1, 2 * KV_W)), per_b((1, 2 * KV_W)), per_b(win.shape[1:]),
            const(ov.shape), pl.BlockSpec((1,) + e_mat.shape[1:], lambda b, j, pt: (j, 0, 0)), const(gsum.shape)],
        out_specs=per_b((N_HEADS, HEAD_DIM)),
        scratch_shapes=[per_sample(N_HEADS, LANES), per_sample(N_HEADS, LANES), per_sample(N_HEADS, KV_W),
                        per_sample(N_HEADS, nsp), per_sample(N_HEADS, KV_W), per_sample(N_HEADS, KV_W)],
    )
    return pl.pallas_call(
        functools.partial(_attn_sample_kernel, pg=pg, ns=ns, past=past, nc=nc, ns_valid=ns_valid, n_top=n_top),
        grid_spec=grid_spec,
        out_shape=jax.ShapeDtypeStruct((nb, N_HEADS, HEAD_DIM), BF16),
        compiler_params=_params("parallel", "arbitrary"),
        name="nsa_attn_sample",
    )(pt_flat, *([pages] * (ns * pg)), q3, qr3, g3, kcvc, ks_new, kw_new, win, ov, e_mat, gsum)


def _rope_tables(pos):
    half = HEAD_DIM // 2
    inv = ROPE_THETA ** (-jnp.arange(half, dtype=F32) / half)
    ang = pos.astype(F32)[:, None] * inv[None, :]
    cos = jnp.cos(ang)
    sin = jnp.sin(ang)
    reps = LANES // HEAD_DIM
    return (jnp.tile(jnp.concatenate([cos, cos], axis=1), (1, reps)),
            jnp.tile(jnp.concatenate([-sin, sin], axis=1), (1, reps)))


def _compress_weights(cmp_w1, cmp_w2, cmp_pe):
    ratio = CMP_BLOCK // CMP_STRIDE
    w1r = cmp_w1.reshape(2, ratio, CMP_STRIDE, HEAD_DIM, HEAD_DIM)
    hh = np.arange(KV_W) // HEAD_DIM
    same_head = jnp.asarray(hh[:, None] == hh[None, :])

    def block_diag(w):
        tiled = jnp.concatenate([jnp.concatenate([w] * N_KV_HEADS, axis=-1)] * N_KV_HEADS, axis=-2)
        return jnp.where(same_head, tiled, 0.0).astype(BF16)

    wbd = block_diag(w1r.reshape(2 * ratio * CMP_STRIDE, HEAD_DIM, HEAD_DIM))
    w2bd = block_diag(cmp_w2)
    pe_r = cmp_pe.reshape(2, ratio, CMP_STRIDE, HEAD_DIM).astype(F32)
    pe_w = jnp.einsum("krsd,krsde->kre", pe_r, w1r.astype(F32), precision=lax.Precision.HIGHEST)
    pe_t = jnp.tile(pe_w.reshape(2 * ratio, HEAD_DIM), (1, N_KV_HEADS))
    return wbd, pe_t, w2bd


def _overlap(nc, ncp, nsel, nsp):
    c_start = np.arange(ncp)[:, None] * CMP_STRIDE
    s_start = np.arange(nsp)[None, :] * SEL_BLOCK
    ov = (c_start < s_start + SEL_BLOCK) & (c_start + CMP_BLOCK > s_start)
    ov = ov & (np.arange(ncp)[:, None] < nc) & (np.arange(nsp)[None, :] < nsel)
    return jnp.asarray(ov, BF16)


def _tile(n, pref):
    t = min(n, pref)
    while n % t:
        t //= 2
    return t


def kernel(x_prompt, x_sample, cache_kv_cmp, cache_kv_sel, cache_kv_win, state_ssm, page_table, norm_g, mlp_w1,
           mlp_w2, nsa_w_in, nsa_w_o, nsa_cmp_w1, nsa_cmp_w2, nsa_cmp_pe, s5_a_re, s5_a_im, s5_log_dt, s5_b_re,
           s5_b_im, s5_c_re, s5_c_im, s5_d, s5_w_glu, s5_b_glu):
    b, t, d = x_prompt.shape
    nb = x_sample.shape[0]
    pages_per_sample = page_table.shape[1]
    past = pages_per_sample * PAGE_SIZE
    rows_p = b * t
    g = norm_g.reshape(norm_g.shape[0], 4, 1, d)

    w_in = nsa_w_in[0]
    w_q = w_in[:, :Q_W].astype(BF16)
    w_kv = w_in[:, Q_W:Q_W + 6 * KV_W].astype(BF16)
    w_gate = jnp.pad(w_in[:, Q_W + 6 * KV_W:], ((0, 0), (0, LANES - 3 * N_HEADS))).astype(BF16)
    w_o = nsa_w_o[0].astype(BF16)
    wbd, pe_t, w2bd = _compress_weights(nsa_cmp_w1[0], nsa_cmp_w2[0], nsa_cmp_pe[0])
    w1 = mlp_w1.astype(BF16)
    w2 = mlp_w2.astype(BF16)
    w_glu = s5_w_glu[0].astype(BF16)
    ops = _s5_operators(s5_a_re[0], s5_a_im[0], s5_log_dt[0], s5_b_re[0], s5_b_im[0], s5_c_re[0], s5_c_im[0])
    d_skip = s5_d[0].reshape(1, d)
    b_glu = s5_b_glu[0].reshape(1, d)

    tm = _tile(rows_p, ROW_TILE)
    ff_chunk = _tile(mlp_w1.shape[2], FF_CHUNK)

    xp = x_prompt.reshape(rows_p, d)
    cos_p, sin_p = _rope_tables(jnp.arange(t, dtype=jnp.int32))
    tq = next(c for c in (2 * LANES, LANES) if t % c == 0 and WINDOW % c == 0)
    (qT, qrT, gT, kvc, kvcT, kvsT, kvwT, ksb, kwb, vsT, vwT) = _inproj(
        xp, g[0, 0], w_q.T, w_kv, w_gate, cos_p, sin_p, tm=_tile(t, ROW_TILE), pos_blocks=t // _tile(t, ROW_TILE),
        transposed=True,
        key_chunk=tq)
    nsb_p = t // CMP_STRIDE
    nc_p = nsb_p - CMP_BLOCK // CMP_STRIDE + 1
    nsel_p = t // SEL_BLOCK
    kc, vcT = _compress_prompt(kvc.reshape(b, t, 2 * KV_W), wbd, pe_t, w2bd)
    ovT = _overlap(nc_p, nsb_p, nsel_p, nsel_p).T
    oT = _attn_prompt(qT, qrT, gT, kc, vcT, ksb, vsT, kwb, vwT, ovT, batch=b, seq=t, tq=tq, nc=nc_p,
                      n_top=min(TOP_N, nsel_p))
    hp, xn1 = _layer_tail((oT, w_o), xp, g[0, 1], g[0, 2], w1[0], w2[0], g[0, 3], g[1, 0], mixer="nsa", tm=tm,
                          ff_chunk=ff_chunk, transposed=True)

    pw = SSM_UNIT * SSM_GROUP
    npair = d // pw
    y3, hfin = _s5_seq(xn1.reshape(b, t, d), ops)
    yc = y3.reshape(rows_p, d)
    (hp,) = _layer_tail((yc, xn1, d_skip, w_glu, b_glu), hp, g[1, 1], g[1, 2], w1[1], w2[1], g[1, 3], None,
                        mixer="s5", tm=tm, ff_chunk=ff_chunk)
    ssm_p = hfin.reshape(npair, 2, b, SSM_UNIT, SSM_STATE).transpose(2, 1, 0, 3, 4)
    ssm_p = ssm_p.reshape(b, 2, d // SSM_GROUP, SSM_STATE)

    xs = x_sample.reshape(nb, d)
    cos_s, sin_s = _rope_tables(jnp.full((nb,), past, dtype=jnp.int32))
    q_s, qr_s, gates_s, kvc_s, kvs_s, kvw_s = _inproj(
        xs, g[0, 0], w_q, w_kv, w_gate, cos_s, sin_s, tm=nb, pos_blocks=1, transposed=False)
    pt_flat = page_table.reshape(-1).astype(jnp.int32)
    pg = _tile(pages_per_sample, PAGE_GROUP)
    n_pool = cache_kv_cmp.shape[1]
    feature_major = lambda c, n, s: c.transpose(0, 2, 3, 4, 1).reshape(n, 2, KV_W, s)
    cmp_pages = feature_major(cache_kv_cmp[0], n_pool, PAGE_SIZE)
    sel_pages = feature_major(cache_kv_sel[0], n_pool, PAGE_SIZE)
    kcvc = _compress_paged(cmp_pages, pt_flat, wbd, pe_t, w2bd, nb=nb, pages_per_sample=pages_per_sample, pg=pg)
    l_all = past + 1
    nsb_s = l_all // CMP_STRIDE
    nc_s = nsb_s - CMP_BLOCK // CMP_STRIDE + 1
    nsel_s = -(-l_all // SEL_BLOCK)
    nsp = -(-nsel_s // LANES) * LANES
    ov_s = _overlap(nc_s, past // CMP_STRIDE, nsel_s, nsp)
    keys_per_step = pg * PAGE_SIZE
    key_blk = (np.arange(past) // SEL_BLOCK).reshape(past // keys_per_step, 1, keys_per_step)
    e_mat = jnp.asarray(np.arange(nsp)[None, :, None] == key_blk, BF16)
    hh = np.arange(N_HEADS)
    gsum = jnp.asarray((hh[:, None] // GROUP) == (hh[None, :] // GROUP), BF16)
    win = feature_major(cache_kv_win[0], nb, WINDOW)
    o_s = _attn_sample(sel_pages, pt_flat, q_s.reshape(nb, N_HEADS, HEAD_DIM), qr_s.reshape(nb, N_HEADS, HEAD_DIM),
                       gates_s[:, :3 * N_HEADS].reshape(nb, N_HEADS, 3), kcvc, kvs_s.reshape(nb, 1, 2 * KV_W),
                       kvw_s.reshape(nb, 1, 2 * KV_W), win, ov_s, e_mat, gsum, nb=nb,
                       pages_per_sample=pages_per_sample, pg=pg, past=past, nc=nc_s, ns_valid=nsel_s,
                       n_top=min(TOP_N, nsel_s))
    hs, xn1_s = _layer_tail((o_s.reshape(nb, Q_W), w_o), xs, g[0, 1], g[0, 2], w1[0], w2[0], g[0, 3], g[1, 0],
                            mixer="nsa", tm=nb, ff_chunk=ff_chunk)

    u2_s = xn1_s.astype(BF16).reshape(nb, npair, pw).transpose(1, 0, 2)
    st = state_ssm[0].reshape(nb, 2, npair, SSM_UNIT * SSM_STATE).transpose(1, 2, 0, 3)
    y2_s, hr_s, hi_s = _s5_step(u2_s, st[0], st[1], ops)
    yc_s = y2_s.transpose(1, 0, 2).reshape(nb, d)
    (hs,) = _layer_tail((yc_s, xn1_s, d_skip, w_glu, b_glu), hs, g[1, 1], g[1, 2], w1[1], w2[1], g[1, 3], None,
                        mixer="s5", tm=nb, ff_chunk=ff_chunk)
    ssm_s = jnp.stack([hr_s, hi_s], axis=0).transpose(2, 0, 1, 3).reshape(nb, 2, d // SSM_GROUP, SSM_STATE)

    kv5 = lambda a, n, s: a.reshape(1, n, s, 2, N_KV_HEADS, HEAD_DIM)
    from_fm = lambda a, n, s: a.reshape(n, 2, N_KV_HEADS, HEAD_DIM, s).transpose(0, 4, 1, 2, 3)[None]
    win_s = jnp.concatenate([win[..., 1:], kvw_s.reshape(nb, 2, KV_W, 1)], axis=-1)
    return (hp.reshape(b, t, d), hs.reshape(nb, 1, d),
            from_fm(kvcT, b, t), kv5(kvc_s, nb, 1), from_fm(kvsT, b, t), kv5(kvs_s, nb, 1),
            from_fm(kvwT[:, :, t - WINDOW:], b, WINDOW), from_fm(win_s, nb, WINDOW), ssm_p[None], ssm_s[None])
```

```python
import functools

import jax
import jax.numpy as jnp
import numpy as np
from jax import lax
from jax.experimental import pallas as pl
from jax.experimental.pallas import tpu as pltpu

N_HEADS = 16
HEAD_DIM = 64
N_KV_HEADS = 4
GROUP = N_HEADS // N_KV_HEADS
CMP_BLOCK = 32
CMP_STRIDE = 16
SEL_BLOCK = 64
TOP_N = 16
WINDOW = 512
ROPE_THETA = 10000.0
PAGE_SIZE = 128
SSM_GROUP = 16
SSM_STATE = 64
SSM_CHUNK = 8
SSM_UNIT = 8
EPS = 1e-6
NEG = -1e30
FORCE = 1e9
MASKED = -1.5e38
LOG2E = 1.4426950408889634
V_ROWS = HEAD_DIM + 16
Q_W = N_HEADS * HEAD_DIM
KV_W = N_KV_HEADS * HEAD_DIM
LANES = 128
SUBLANES = 8
ROW_TILE = 512
FF_CHUNK = 1024
PAGE_GROUP = 16
SAMPLES_PER_STEP = 2
ATTN_SAMPLES_PER_STEP = 4
S5_TIME_BLOCK = 2048
VMEM_LIMIT = 56 * 1024 * 1024

F32 = jnp.float32
BF16 = jnp.bfloat16


def _params(*sem):
    return pltpu.CompilerParams(dimension_semantics=sem, vmem_limit_bytes=VMEM_LIMIT)


def _full(shape):
    zeros = (0,) * len(shape)
    return pl.BlockSpec(shape, lambda *_: zeros)


def _rms(x, g):
    ms = jnp.mean(x * x, axis=-1, keepdims=True)
    return x * lax.rsqrt(ms + EPS) * g


def _dot(a, b):
    return jnp.dot(a, b, preferred_element_type=F32)


def _dot_f32lhs(w, x):
    hi = x.astype(BF16)
    r1 = x - hi.astype(F32)
    mid = r1.astype(BF16)
    lo = (r1 - mid.astype(F32)).astype(BF16)
    return _dot(w, hi) + _dot(w, mid) + _dot(w, lo)


def _dot_f32rhs(x, w):
    hi = x.astype(BF16)
    r1 = x - hi.astype(F32)
    mid = r1.astype(BF16)
    lo = (r1 - mid.astype(F32)).astype(BF16)
    return _dot(hi, w) + _dot(mid, w) + _dot(lo, w)


def _rope_nat(x, cos, sin):
    half = HEAD_DIM // 2
    lane = lax.broadcasted_iota(jnp.int32, (1, LANES), 1)
    first = (lane % HEAD_DIM) < half
    outs = []
    for c in range(x.shape[1] // LANES):
        xc = x[:, c * LANES:(c + 1) * LANES]
        rot = jnp.where(first, pltpu.roll(xc, LANES - half, 1), pltpu.roll(xc, half, 1))
        outs.append(xc * cos + rot * sin)
    return jnp.concatenate(outs, axis=1)


def _inproj_kernel(x_ref, g_ref, wq_ref, wkv_ref, wg_ref, cos_ref, sin_ref, *rest, transposed, key_chunk):
    outs = rest[2:] if transposed else rest
    xb = _rms(x_ref[...], g_ref[...]).astype(BF16)
    cos = cos_ref[...]
    sin = sin_ref[...]
    scale = HEAD_DIM ** -0.5
    kv = _dot(xb, wkv_ref[...])
    gates = jax.nn.sigmoid(_dot(xb, wg_ref[...]))
    k_s = _rope_nat(kv[:, 2 * KV_W:3 * KV_W], cos, sin)
    v_s = kv[:, 3 * KV_W:4 * KV_W]
    k_w = _rope_nat(kv[:, 4 * KV_W:5 * KV_W], cos, sin)
    v_w = kv[:, 5 * KV_W:6 * KV_W]
    kvc_ref = outs[3]
    kvc_ref[...] = kv[:, 0:2 * KV_W]
    if transposed:
        qT_ref, qrT_ref, gT_ref, _, kvcT_ref, kvsT_ref, kvwT_ref, ksb_ref, kwb_ref, vsT_ref, vwT_ref = outs
        qT = _dot_nt(wq_ref[...], xb)
        cosT = rest[0][...]
        sinT = rest[1][...]
        half = HEAD_DIM // 2
        rotated = []
        for h in range(N_HEADS):
            x1 = qT[h * HEAD_DIM:h * HEAD_DIM + half, :]
            x2 = qT[h * HEAD_DIM + half:(h + 1) * HEAD_DIM, :]
            rotated += [x1 * cosT - x2 * sinT, x2 * cosT + x1 * sinT]
        qT_ref[...] = (qT * (scale * LOG2E)).astype(BF16)
        qrT_ref[...] = (jnp.concatenate(rotated, axis=0) * (scale * LOG2E)).astype(BF16)
        gT_ref[...] = gates.T
        tm = x_ref.shape[0]
        rowi = lax.broadcasted_iota(jnp.int32, (tm, LANES), 0)
        lanei = lax.broadcasted_iota(jnp.int32, (tm, LANES), 1)
        blk = lax.shift_right_logical(rowi & (key_chunk - 1), SEL_BLOCK.bit_length() - 1)
        extra = lanei - HEAD_DIM
        nblk = key_chunk // SEL_BLOCK
        aug = jnp.where((extra == blk) | ((extra >= nblk) & (extra < nblk + N_CONST_LANES)), 1.0, 0.0)
        for c in range(KV_W // LANES):
            for k_nat, k_ref in ((k_s, ksb_ref), (k_w, kwb_ref)):
                pair = k_nat[:, c * LANES:(c + 1) * LANES]
                k_ref[2 * c] = jnp.where(lanei < HEAD_DIM, pair, aug).astype(BF16)
                k_ref[2 * c + 1] = jnp.where(lanei < HEAD_DIM, pltpu.roll(pair, HEAD_DIM, 1), aug).astype(BF16)
        v_sT = v_s.T
        v_wT = v_w.T
        ones_rows = jnp.where(lax.broadcasted_iota(jnp.int32, (V_ROWS - HEAD_DIM, tm), 0) == 0, 1.0, 0.0)
        for h in range(N_KV_HEADS):
            for vT, v_ref in ((v_sT, vsT_ref), (v_wT, vwT_ref)):
                v_ref[h * V_ROWS:(h + 1) * V_ROWS, :] = jnp.concatenate(
                    [vT[h * HEAD_DIM:(h + 1) * HEAD_DIM, :], ones_rows], axis=0).astype(BF16)
        kvcT_ref[0] = kv[:, 0:2 * KV_W].T
        kvsT_ref[0, 0:KV_W, :] = k_s.T
        kvsT_ref[0, KV_W:2 * KV_W, :] = v_sT
        kvwT_ref[0, 0:KV_W, :] = k_w.T
        kvwT_ref[0, KV_W:2 * KV_W, :] = v_wT
    else:
        q_ref, qr_ref, gt_ref, _, kvs_ref, kvw_ref = outs
        q = _dot(xb, wq_ref[...])
        qr = _rope_nat(q, cos, sin)
        q_ref[...] = (q * scale).astype(BF16)
        qr_ref[...] = (qr * scale).astype(BF16)
        gt_ref[...] = gates
        kvs_ref[:, 0:KV_W] = k_s
        kvs_ref[:, KV_W:2 * KV_W] = v_s
        kvw_ref[:, 0:KV_W] = k_w
        kvw_ref[:, KV_W:2 * KV_W] = v_w


def _inproj(x, g, w_q, w_kv, w_gate, cos_t, sin_t, *, tm, pos_blocks, transposed, key_chunk=LANES):
    rows, d = x.shape
    assert tm % key_chunk == 0 or not transposed
    n = rows // tm
    row_blk = lambda w: pl.BlockSpec((tm, w), lambda i: (i, 0))
    col_blk = lambda h: pl.BlockSpec((h, tm), lambda i: (0, i))
    tab = pl.BlockSpec((tm, LANES), lambda i: (i % pos_blocks, 0))
    kv_nat = jax.ShapeDtypeStruct((rows, 2 * KV_W), F32)
    if transposed:
        seqs = rows // (pos_blocks * tm)
        kh = pl.BlockSpec((N_KV_HEADS, tm, LANES), lambda i: (0, i, 0))
        kvT = pl.BlockSpec((1, 2 * KV_W, tm), lambda i: (i // pos_blocks, 0, i % pos_blocks))
        out_shape = ([jax.ShapeDtypeStruct((Q_W, rows), BF16)] * 2 + [jax.ShapeDtypeStruct((LANES, rows), F32)]
                     + [kv_nat] + [jax.ShapeDtypeStruct((seqs, 2 * KV_W, pos_blocks * tm), F32)] * 3
                     + [jax.ShapeDtypeStruct((N_KV_HEADS, rows, LANES), BF16)] * 2
                     + [jax.ShapeDtypeStruct((N_KV_HEADS * V_ROWS, rows), BF16)] * 2)
        out_specs = ([col_blk(Q_W)] * 2 + [col_blk(LANES)] + [row_blk(2 * KV_W)] + [kvT] * 3 + [kh] * 2
                     + [col_blk(N_KV_HEADS * V_ROWS)] * 2)
    else:
        out_shape = ([jax.ShapeDtypeStruct((rows, Q_W), BF16)] * 2 + [jax.ShapeDtypeStruct((rows, LANES), F32)]
                     + [kv_nat] * 3)
        out_specs = [row_blk(Q_W)] * 2 + [row_blk(LANES)] + [row_blk(2 * KV_W)] * 3
    tables = [cos_t, sin_t]
    table_specs = [tab, tab]
    if transposed:
        half = HEAD_DIM // 2
        tables += [cos_t[:, 0:half].T, sin_t[:, half:HEAD_DIM].T]
        table_specs += [pl.BlockSpec((half, tm), lambda i: (0, i % pos_blocks))] * 2
    return pl.pallas_call(
        functools.partial(_inproj_kernel, transposed=transposed, key_chunk=key_chunk),
        grid=(n,),
        in_specs=[row_blk(d), _full((1, d)), _full(w_q.shape), _full(w_kv.shape), _full(w_gate.shape)] + table_specs,
        out_specs=out_specs,
        out_shape=out_shape,
        compiler_params=_params("parallel"),
        name="nsa_inproj",
    )(x, g, w_q, w_kv, w_gate, *tables)


_KV_CHUNKS = 2 * KV_W // LANES


def _compress_half(load_rows, kv, nrows, wbd_ref, pe_ref):
    ratio = CMP_BLOCK // CMP_STRIDE
    accs = [jnp.broadcast_to(pe_ref[kv * ratio + r:kv * ratio + r + 1, :], (nrows, KV_W)) for r in range(ratio)]
    for s in range(CMP_STRIDE):
        lhs = load_rows(s, kv).astype(BF16)
        for r in range(ratio):
            accs[r] = accs[r] + _dot(lhs, wbd_ref[(kv * ratio + r) * CMP_STRIDE + s])
    return accs


def _compress_prompt_kernel(*refs, nsb):
    x_refs = refs[:_KV_CHUNKS]
    wbd_ref, pe_ref, w2_ref, kc_ref, vcT_ref, sh_ref = refs[_KV_CHUNKS:]

    def load_rows(s, kv):
        per_half = _KV_CHUNKS // 2
        return jnp.concatenate([x_refs[kv * per_half + c][0, pl.ds(s, nsb, stride=CMP_STRIDE), :]
                                for c in range(per_half)], axis=1)

    sh_ref[nsb:nsb + SUBLANES, :] = jnp.zeros((SUBLANES, KV_W), F32)
    for kv in range(2):
        pr0, pr1 = _compress_half(load_rows, kv, nsb, wbd_ref, pe_ref)
        sh_ref[0:nsb, :] = pr1
        h = pr0 + sh_ref[pl.ds(1, nsb), :]
        out = _dot(jax.nn.gelu(h).astype(BF16), w2_ref[kv])
        if kv == 0:
            for hh in range(N_KV_HEADS):
                kc_ref[0, hh] = out[:, hh * HEAD_DIM:(hh + 1) * HEAD_DIM].astype(BF16)
        else:
            vcT_ref[0] = out.T.astype(BF16)


def _compress_prompt(kvc3, wbd, pe_t, w2bd):
    b, t, _ = kvc3.shape
    nsb = t // CMP_STRIDE
    return pl.pallas_call(
        functools.partial(_compress_prompt_kernel, nsb=nsb),
        grid=(b,),
        in_specs=[pl.BlockSpec((1, t, LANES), lambda i, c=c: (i, 0, c)) for c in range(_KV_CHUNKS)]
        + [_full(wbd.shape), _full(pe_t.shape), _full(w2bd.shape)],
        out_specs=[pl.BlockSpec((1, N_KV_HEADS, nsb, HEAD_DIM), lambda i: (i, 0, 0, 0)),
                   pl.BlockSpec((1, KV_W, nsb), lambda i: (i, 0, 0))],
        out_shape=[jax.ShapeDtypeStruct((b, N_KV_HEADS, nsb, HEAD_DIM), BF16),
                   jax.ShapeDtypeStruct((b, KV_W, nsb), BF16)],
        scratch_shapes=[pltpu.VMEM((nsb + SUBLANES, KV_W), F32)],
        compiler_params=_params("parallel"),
        name="nsa_compress_prompt",
    )(*([kvc3] * _KV_CHUNKS), wbd, pe_t, w2bd)


def _topk_mask_T(imp, n_top):
    ns, w = imp.shape
    nblk = ns // SUBLANES
    blocks = [imp[r * SUBLANES:(r + 1) * SUBLANES, :] for r in range(nblk)]
    cnts = [jnp.zeros((SUBLANES, w), F32) for _ in range(nblk)]
    sub = lax.broadcasted_iota(jnp.int32, (SUBLANES, w), 0)
    for sp in range(ns):
        row = blocks[sp // SUBLANES][sp % SUBLANES:sp % SUBLANES + 1, :]
        for r in range(nblk):
            blk = blocks[r]
            if sp < r * SUBLANES:
                beats = jnp.where(row >= blk, 1.0, 0.0)
            elif sp >= (r + 1) * SUBLANES:
                beats = jnp.where(row > blk, 1.0, 0.0)
            else:
                beats = jnp.where(sub > (sp - r * SUBLANES), jnp.where(row >= blk, 1.0, 0.0),
                                  jnp.where(row > blk, 1.0, 0.0))
            cnts[r] = cnts[r] + beats
    return jnp.concatenate([jnp.where(c < n_top, 1.0, 0.0) for c in cnts], axis=0)


def _online_chunks(states, k_cs, vT_cs, qTs, bias):
    scores = [_dot(k_c, qT) for k_c, qT in zip(k_cs, qTs)]
    mids = []
    for (m, _), s in zip(states, scores):
        if bias is not None:
            s = s + bias
        m_new = jnp.maximum(m, jnp.max(s, axis=0, keepdims=True))
        mids.append((m_new, jnp.exp2(m - m_new), jnp.exp2(s - m_new).astype(BF16)))
    return tuple((m_new, alpha * acc + _dot(vT_c, p))
                 for (m_new, alpha, p), (_, acc), vT_c in zip(mids, states, vT_cs))


def _softmax_finish(carry):
    _, acc = carry
    return acc[0:HEAD_DIM, :] * (1.0 / jnp.maximum(acc[HEAD_DIM:HEAD_DIM + 1, :], 1e-20))


LAZY_LOG2_MAX = 60.0
N_CONST_LANES = 4


def _split3(x):
    hi = x.astype(BF16).astype(F32)
    r = x - hi
    mid = r.astype(BF16).astype(F32)
    return hi, mid, (r - mid).astype(BF16).astype(F32)


def _rescaling_chunks(m_ref, acc_ref, k_cs, vT_cs, qTs, base_tiles, bias):
    nh = len(k_cs)
    rows, w = base_tiles[0].shape
    zero_rows = jnp.zeros((LANES - HEAD_DIM - rows, w), BF16)
    new = _online_chunks(tuple((m_ref[h], acc_ref[h]) for h in range(nh)), k_cs, vT_cs,
                         [jnp.concatenate([qTs[h], base_tiles[h].astype(BF16), zero_rows], axis=0) for h in range(nh)],
                         bias)
    for h in range(nh):
        m_ref[h] = new[h][0]
        acc_ref[h] = new[h][1]


def _lazy_chunks(m_ref, acc2_ref, slot_ref, k_cs, vT_cs, qTs, base_tiles, bias, ref_row, n_chunks=1, first=False):
    slot = slot_ref[0]
    acc_ref = acc2_ref.at[slot]
    nu = len(k_cs)
    nh = nu // n_chunks
    biases = list(bias) if isinstance(bias, (list, tuple)) else [bias] * n_chunks
    rows = base_tiles[0].shape[0]
    w = base_tiles[0].shape[1]
    rowt = lax.broadcasted_iota(jnp.int32, (rows, w), 0)
    zero_rows = jnp.zeros((LANES - HEAD_DIM - rows, w), BF16)

    def queries(u):
        hi, mid, lo = _split3(-m_ref[u % nh])
        tile = jnp.where(rowt == ref_row, hi, jnp.where(rowt == ref_row + 1, mid,
                                                        jnp.where(rowt == ref_row + 2, lo, base_tiles[u])))
        return jnp.concatenate([qTs[u % nh], tile.astype(BF16), zero_rows], axis=0)

    scores = [_dot(k_cs[u], queries(u)) for u in range(nu)]
    peaks = [None] * nh
    probs = []
    for u, s in enumerate(scores):
        if biases[u // nh] is not None:
            s = s + biases[u // nh]
        top = jnp.max(s, axis=0, keepdims=True)
        peaks[u % nh] = top if peaks[u % nh] is None else jnp.maximum(peaks[u % nh], top)
        probs.append(jnp.exp2(s).astype(BF16))
    pvs = [_dot(vT_cs[u], probs[u]) for u in range(nu)]
    for h in range(nh):
        total = pvs[h]
        for c in range(1, n_chunks):
            total = total + pvs[c * nh + h]
        acc2_ref[1 - slot, h] = acc_ref[h] + total
    highest = functools.reduce(jnp.maximum, peaks)
    lowest = functools.reduce(jnp.minimum, peaks)
    in_range = (jnp.max(highest) <= LAZY_LOG2_MAX) & (jnp.logical_not(first) | (jnp.min(lowest) >= -LAZY_LOG2_MAX))

    @pl.when(in_range)
    def _():
        slot_ref[0] = 1 - slot

    @pl.when(jnp.logical_not(in_range))
    def _():
        for h in range(nh):
            m_ref[h] = jnp.where(first, NEG, m_ref[h])
        for c in range(n_chunks):
            part = slice(c * nh, (c + 1) * nh)
            _rescaling_chunks(m_ref, acc_ref, k_cs[part], vT_cs[part], qTs, base_tiles[part], biases[c])


def _attn_prompt_kernel(qT_ref, qrT_ref, gT_ref, kc_ref, vcT_ref, ks_ref, vsT_ref, kw_ref, vwT_ref, ovT_ref,
                        o_ref, selb_ref, oc_ref, ow_ref, m_ref, acc2_ref, slot_ref, *, tq, nc, n_top):
    ck = tq
    i = pl.program_id(1)
    t0 = i * tq
    qpos = t0 + lax.broadcasted_iota(jnp.int32, (1, tq), 1)
    ncp = kc_ref.shape[2]
    ns = ovT_ref.shape[0]
    w = GROUP * tq
    bpc = ck // SEL_BLOCK
    sel_shift = SEL_BLOCK.bit_length() - 1
    kvhs = range(N_KV_HEADS)
    heads = [[kvh * GROUP + g for g in range(GROUP)] for kvh in kvhs]
    rows = [pl.ds(kvh * HEAD_DIM, HEAD_DIM) for kvh in kvhs]
    vrows = [pl.ds(kvh * V_ROWS, V_ROWS) for kvh in kvhs]
    qrT = [jnp.concatenate([qrT_ref[h * HEAD_DIM:(h + 1) * HEAD_DIM, :] for h in heads[kvh]], axis=1)
           for kvh in kvhs]
    kl = lax.broadcasted_iota(jnp.int32, (ck, tq), 0)
    ql = lax.broadcasted_iota(jnp.int32, (ck, tq), 1)
    tile4 = lambda b: jnp.concatenate([b] * GROUP, axis=1)
    key_le_query = tile4(jnp.where(kl <= ql, 0.0, MASKED))
    key_ge_query = tile4(jnp.where(kl >= ql, 0.0, MASKED))
    bias_rows = selb_ref.shape[2]

    def reset_softmax():
        slot_ref[0] = 0
        for kvh in kvhs:
            m_ref[kvh] = jnp.zeros((1, w), F32)
            acc2_ref[0, kvh] = jnp.zeros((V_ROWS, w), F32)

    def current_acc(kvh):
        return acc2_ref[slot_ref[0], kvh]

    cidx = lax.broadcasted_iota(jnp.int32, (ncp, 1), 0)
    valid = ((cidx * CMP_STRIDE + (CMP_BLOCK - 1)) <= qpos) & (cidx < nc)
    sidx = lax.broadcasted_iota(jnp.int32, (ns, 1), 0)
    cur = lax.shift_right_logical(qpos, sel_shift)
    forced = (sidx == 0) | (sidx == cur) | (sidx == cur - 1)
    causal = (sidx * SEL_BLOCK) <= qpos
    cmp_scores = [
        _dot(kc_ref[0, kvh], jnp.concatenate([qT_ref[h * HEAD_DIM:(h + 1) * HEAD_DIM, :] for h in heads[kvh]], axis=1))
        for kvh in kvhs]
    importance = []
    cmp_bias = jnp.where(valid, 0.0, MASKED)
    seen = jnp.where(jnp.max(cmp_bias, axis=0, keepdims=True) == 0.0, 1.0, 0.0)
    for kvh in kvhs:
        s = cmp_scores[kvh]
        probs = []
        for g in range(GROUP):
            sm = s[:, g * tq:(g + 1) * tq] + cmp_bias
            e = jnp.exp2(sm - jnp.max(sm, axis=0, keepdims=True))
            den = jnp.maximum(jnp.sum(e, axis=0, keepdims=True), 1e-20)
            probs.append(e * (seen / den))
        oc_ref[kvh] = _dot(vcT_ref[0, rows[kvh], :], jnp.concatenate(probs, axis=1).astype(BF16))
        psum = probs[0]
        for g in range(1, GROUP):
            psum = psum + probs[g]
        imp = _dot_f32lhs(ovT_ref[...], psum)
        importance.append(jnp.where(forced, FORCE, jnp.where(causal, imp, NEG)))

    prefixes = sorted({r for r in (ns // 4, ns // 2, 3 * ns // 4) if r and r % SUBLANES == 0} | {ns})
    needed = (i + 1) * bpc
    fill = jnp.zeros((bias_rows - bpc, tq), F32)
    for lo, hi in zip([0] + prefixes[:-1], prefixes):
        @pl.when((needed > lo) & (needed <= hi))
        def _(hi=hi):
            for kvh in kvhs:
                selb = (_topk_mask_T(importance[kvh][0:hi, :], min(n_top, hi)) - 1.0) * (-MASKED)
                for c in range(hi // bpc):
                    selb_ref[kvh, c] = jnp.concatenate([selb[c * bpc:(c + 1) * bpc, :], fill], axis=0)

    n_back = WINDOW // ck
    reset_softmax()
    rowb = lax.broadcasted_iota(jnp.int32, (bias_rows, w), 0)
    def window_chunk(r):
        a = i - n_back + r
        kst = pl.multiple_of(jnp.maximum(a, 0) * ck, ck)
        skip = jnp.where(a < 0, MASKED, 0.0)
        tile = jnp.where(rowb == bpc, skip, 0.0)
        bias = key_le_query if r == n_back else (key_ge_query if r == 0 else None)
        return ([kw_ref[kvh, pl.ds(kst, ck), :] for kvh in kvhs],
                [vwT_ref[vrows[kvh], pl.ds(kst, ck)] for kvh in kvhs], [tile] * N_KV_HEADS, bias)

    order = [n_back] + list(range(n_back))
    for g0 in [0] + list(range(1, n_back + 1, 2)):
        group = [window_chunk(r) for r in (order[g0:g0 + 1] if g0 == 0 else order[g0:g0 + 2])]
        _lazy_chunks(m_ref, acc2_ref, slot_ref, sum((g[0] for g in group), []), sum((g[1] for g in group), []),
                     qrT, sum((g[2] for g in group), []), [g[3] for g in group], bpc + 1, len(group), first=g0 == 0)
    for kvh in kvhs:
        ow_ref[kvh] = _softmax_finish((None, current_acc(kvh)))

    def chunk_step(c, diagonal, n_chunks=1, first=False):
        ks, vs, tiles = [], [], []
        for dc in range(n_chunks):
            kst = (c + dc) * ck if isinstance(c, int) else pl.multiple_of((c + dc) * ck, ck)
            ks += [ks_ref[kvh, pl.ds(kst, ck), :] for kvh in kvhs]
            vs += [vsT_ref[vrows[kvh], pl.ds(kst, ck)] for kvh in kvhs]
            tiles += [tile4(selb_ref[kvh, c + dc]) for kvh in kvhs]
        _lazy_chunks(m_ref, acc2_ref, slot_ref, ks, vs, qrT, tiles, key_le_query if diagonal else None, bpc + 1,
                     n_chunks, first)

    reset_softmax()

    @pl.when(i > 0)
    def _():
        chunk_step(0, False, first=True)

    n_mid = jnp.maximum(i - 1, 0)

    def loop_body(p, carry):
        chunk_step(1 + 2 * p, False, n_chunks=2)
        return carry

    lax.fori_loop(0, n_mid // 2, loop_body, 0)

    @pl.when(n_mid % 2 == 1)
    def _():
        chunk_step(i - 1, False)

    chunk_step(i, True, first=i == 0)

    for kvh in kvhs:
        def gate_row(j, kvh=kvh):
            return jnp.concatenate([gT_ref[h * 3 + j:h * 3 + j + 1, :] for h in heads[kvh]], axis=1)

        oT = (gate_row(0) * oc_ref[kvh] + gate_row(1) * _softmax_finish((None, current_acc(kvh)))
              + gate_row(2) * ow_ref[kvh])
        for g, h in enumerate(heads[kvh]):
            o_ref[h * HEAD_DIM:(h + 1) * HEAD_DIM, :] = oT[:, g * tq:(g + 1) * tq].astype(BF16)


def _attn_prompt(qT, qrT, gT, kc, vcT, ksb, vsT, kwb, vwT, ovT, *, batch, seq, tq, nc, n_top):
    nq = seq // tq
    nsb = kc.shape[2]
    ns = ovT.shape[0]
    col = lambda h: pl.BlockSpec((h, tq), lambda b, i: (0, b * nq + i))
    kh = pl.BlockSpec((N_KV_HEADS, seq, LANES), lambda b, i: (0, b, 0))
    vt = pl.BlockSpec((N_KV_HEADS * V_ROWS, seq), lambda b, i: (0, b))
    bf16_sublanes = 2 * SUBLANES
    return pl.pallas_call(
        functools.partial(_attn_prompt_kernel, tq=tq, nc=nc, n_top=n_top),
        grid=(batch, nq),
        in_specs=[col(Q_W), col(Q_W), col(LANES),
                  pl.BlockSpec((1, N_KV_HEADS, nsb, HEAD_DIM), lambda b, i: (b, 0, 0, 0)),
                  pl.BlockSpec((1, KV_W, nsb), lambda b, i: (b, 0, 0)),
                  kh, vt, kh, vt, _full(ovT.shape)],
        out_specs=col(Q_W),
        out_shape=jax.ShapeDtypeStruct((Q_W, batch * seq), BF16),
        scratch_shapes=[pltpu.VMEM((N_KV_HEADS, ns * SEL_BLOCK // tq, bf16_sublanes, tq), F32),
                        pltpu.VMEM((N_KV_HEADS, HEAD_DIM, GROUP * tq), F32),
                        pltpu.VMEM((N_KV_HEADS, HEAD_DIM, GROUP * tq), F32),
                        pltpu.VMEM((N_KV_HEADS, 1, GROUP * tq), F32),
                        pltpu.VMEM((2, N_KV_HEADS, V_ROWS, GROUP * tq), F32),
                        pltpu.SMEM((1,), jnp.int32)],
        compiler_params=_params("parallel", "arbitrary"),
        name="nsa_attn_prompt",
    )(qT, qrT, gT, kc, vcT, ksb, vsT, kwb, vwT, ovT)


def _resident(shape):
    zeros = (0,) * len(shape)
    return pl.BlockSpec(shape, lambda *_: zeros, pipeline_mode=pl.Buffered(1))


def _layer_tail_kernel(*refs, mixer, transposed, ff_chunk, next_norm):
    if mixer == "nsa":
        o_ref, wo_ref, x_ref, g1_ref, g2_ref = refs[:5]
        rest = refs[5:]
        if transposed:
            y = lax.dot_general(o_ref[...], wo_ref[...], (((0,), (0,)), ((), ())), preferred_element_type=F32)
        else:
            y = _dot(o_ref[...], wo_ref[...])
    else:
        yc_ref, u_ref, d_ref, wg_ref, bg_ref, x_ref, g1_ref, g2_ref = refs[:8]
        rest = refs[8:]
        z = jax.nn.gelu(yc_ref[...] + d_ref[...] * u_ref[...])
        y = z * jax.nn.sigmoid(_dot(z.astype(BF16), wg_ref[...]) + bg_ref[...])
    w1_ref, w2_ref, g3_ref = rest[:3]
    if next_norm:
        gn_ref, h2_ref, xn_ref, acc_ref = rest[3:]
    else:
        h2_ref, acc_ref = rest[3:]
    h = x_ref[...] + _rms(y, g1_ref[...])
    xm = _rms(h, g2_ref[...]).astype(BF16)
    for c in range(w1_ref.shape[1] // ff_chunk):
        cols = slice(c * ff_chunk, (c + 1) * ff_chunk)
        hm = jnp.maximum(_dot(xm, w1_ref[:, cols]), 0.0)
        part = _dot((hm * hm).astype(BF16), w2_ref[cols, :])
        if c == 0:
            acc_ref[...] = part
        else:
            acc_ref[...] += part
    h2 = h + _rms(acc_ref[...], g3_ref[...])
    h2_ref[...] = h2
    if next_norm:
        xn_ref[...] = _rms(h2, gn_ref[...])


def _layer_tail(mixer_args, x, g1, g2, w1, w2, g3, gn, *, mixer, tm, ff_chunk, transposed=False):
    rows, d = x.shape
    row = pl.BlockSpec((tm, d), lambda i: (i, 0))
    vec = _resident((1, d))
    if mixer == "nsa":
        o, w_o = mixer_args
        o_spec = (pl.BlockSpec((Q_W, tm), lambda i: (0, i)) if transposed
                  else pl.BlockSpec((tm, Q_W), lambda i: (i, 0)))
        head_specs = [o_spec, _resident(w_o.shape)]
    else:
        w_glu = mixer_args[3]
        head_specs = [row, row, vec, _resident(w_glu.shape), vec]
    tail_args = (w1, w2, g3) + (() if gn is None else (gn,))
    tail_specs = [_resident(w1.shape), _resident(w2.shape), vec] + ([] if gn is None else [vec])
    n_out = 1 if gn is None else 2
    return pl.pallas_call(
        functools.partial(_layer_tail_kernel, mixer=mixer, transposed=transposed, ff_chunk=ff_chunk,
                          next_norm=gn is not None),
        grid=(rows // tm,),
        in_specs=head_specs + [row, vec, vec] + tail_specs,
        out_specs=[row] * n_out,
        out_shape=[jax.ShapeDtypeStruct((rows, d), F32)] * n_out,
        scratch_shapes=[pltpu.VMEM((tm, d), F32)],
        compiler_params=_params("parallel"),
        name=mixer + "_layer_tail",
    )(*mixer_args, x, g1, g2, *tail_args)


def _s5_seq_kernel(x_ref, brow_ref, pre_ref, pim_ref, qre_ref, qim_ref, are_ref, aim_ref, y_ref, hfin_ref,
                   sre, sim, hre, him, cre, cim):
    nb, tb, uw = x_ref.shape
    nt = pre_ref.shape[1] // uw
    nk = tb // nt

    @pl.when(pl.program_id(1) == 0)
    def _():
        cre[...] = jnp.zeros(cre.shape, F32)
        cim[...] = jnp.zeros(cim.shape, F32)

    u = [jnp.concatenate([x_ref[b, pl.ds(t, nk, stride=nt), :] for b in range(nb)], axis=0).astype(BF16)
         for t in range(nt)]
    u2 = [jnp.concatenate([u[2 * j], u[2 * j + 1]], axis=1) for j in range(nt // 2)]
    s_re = _dot(u2[0], pre_ref[0, 0:2 * uw, :])
    s_im = _dot(u2[0], pim_ref[0, 0:2 * uw, :])
    for j in range(1, nt // 2):
        s_re = s_re + _dot(u2[j], pre_ref[0, 2 * j * uw:(2 * j + 2) * uw, :])
        s_im = s_im + _dot(u2[j], pim_ref[0, 2 * j * uw:(2 * j + 2) * uw, :])
    sre[...] = s_re
    sim[...] = s_im
    ar = are_ref[0]
    ai = aim_ref[0]

    within = []
    for t2 in range(0, nt, 2):
        acc = None
        for j in range(t2 // 2 + 1):
            lag0 = t2 - 2 * j + 1
            wpair = jnp.concatenate([brow_ref[0, :, lag0 * uw:(lag0 + 2) * uw],
                                     brow_ref[0, :, (lag0 - 1) * uw:(lag0 + 1) * uw]], axis=0)
            part = _dot(u2[j], wpair)
            acc = part if acc is None else acc + part
        within.append(acc)

    def body(it, carry):
        out = []
        for b in range(nb):
            hr, hi = carry[b]
            r0 = b * nk + it * SUBLANES
            sr8 = sre[pl.ds(r0, SUBLANES), :]
            si8 = sim[pl.ds(r0, SUBLANES), :]
            prev_r, prev_i = [], []
            for j in range(SUBLANES):
                prev_r.append(hr)
                prev_i.append(hi)
                hr, hi = (ar * hr - ai * hi + sr8[j:j + 1, :], ar * hi + ai * hr + si8[j:j + 1, :])
            hre[pl.ds(r0, SUBLANES), :] = jnp.concatenate(prev_r, axis=0)
            him[pl.ds(r0, SUBLANES), :] = jnp.concatenate(prev_i, axis=0)
            out.append((hr, hi))
        return tuple(out)

    fin = tuple((cre[b:b + 1, :], cim[b:b + 1, :]) for b in range(nb))
    for it in range(nk // SUBLANES):
        fin = body(it, fin)
    for b in range(nb):
        cre[b:b + 1, :] = fin[b][0]
        cim[b:b + 1, :] = fin[b][1]
    hfin_ref[0, 0] = cre[...]
    hfin_ref[0, 1] = cim[...]

    hb_re = hre[...].astype(BF16)
    hb_im = him[...].astype(BF16)
    for t2 in range(0, nt, 2):
        cols = slice(t2 * uw, (t2 + 2) * uw)
        acc = within[t2 // 2] + _dot(hb_re, qre_ref[0, :, cols]) + _dot(hb_im, qim_ref[0, :, cols])
        for b in range(nb):
            for dt in range(2):
                y_ref[b, pl.ds(t2 + dt, nk, stride=nt), :] = acc[b * nk:(b + 1) * nk, dt * uw:(dt + 1) * uw]


def _s5_seq(x3, ops):
    nb, t, d = x3.shape
    uw = SSM_UNIT * SSM_GROUP
    nunit = d // uw
    sw = ops["p_re"].shape[2]
    tb = _tile(t, S5_TIME_BLOCK)
    nk = tb // SSM_CHUNK
    per_unit = lambda a: pl.BlockSpec((1,) + a.shape[1:], lambda i, r: (i,) + (0,) * (a.ndim - 1))
    blk = pl.BlockSpec((nb, tb, uw), lambda i, r: (0, r, i))
    args = (x3, ops["brow"], ops["p_re"], ops["p_im"], ops["q_re"], ops["q_im"], ops["a_chunk_re"], ops["a_chunk_im"])
    return pl.pallas_call(
        _s5_seq_kernel,
        grid=(nunit, t // tb),
        in_specs=[blk] + [per_unit(a) for a in args[1:]],
        out_specs=[blk, pl.BlockSpec((1, 2, nb, sw), lambda i, r: (i, 0, 0, 0))],
        out_shape=[jax.ShapeDtypeStruct((nb, t, d), F32), jax.ShapeDtypeStruct((nunit, 2, nb, sw), F32)],
        scratch_shapes=[pltpu.VMEM((nb * nk, sw), F32)] * 4 + [pltpu.VMEM((nb, sw), F32)] * 2,
        compiler_params=_params("parallel", "arbitrary"),
        name="s5_seq_scan",
    )(*args)


def _s5_step_kernel(u_ref, h0r_ref, h0i_ref, bre_ref, bim_ref, cre_ref, cim_ref, are_ref, aim_ref,
                    y_ref, hr_ref, hi_ref, *, npair):
    for p in range(npair):
        u = u_ref[p]
        ar = are_ref[p]
        ai = aim_ref[p]
        h0r = h0r_ref[p]
        h0i = h0i_ref[p]
        hr = ar * h0r - ai * h0i + _dot(u, bre_ref[p])
        hi = ar * h0i + ai * h0r + _dot(u, bim_ref[p])
        hr_ref[p] = hr
        hi_ref[p] = hi
        y_ref[p] = _dot(hr.astype(BF16), cre_ref[p]) + _dot(hi.astype(BF16), cim_ref[p])


def _s5_step(u2, h0r, h0i, ops):
    npair, rows, width = u2.shape
    args = (u2, h0r, h0i, ops["b1_re"], ops["b1_im"], ops["c1_re"], ops["c1_im"], ops["a1_re"], ops["a1_im"])
    return pl.pallas_call(
        functools.partial(_s5_step_kernel, npair=npair),
        grid=(1,),
        in_specs=[_full(a.shape) for a in args],
        out_specs=[_full((npair, rows, width)), _full(h0r.shape), _full(h0r.shape)],
        out_shape=[jax.ShapeDtypeStruct((npair, rows, width), F32), jax.ShapeDtypeStruct(h0r.shape, F32),
                   jax.ShapeDtypeStruct(h0r.shape, F32)],
        compiler_params=_params("arbitrary"),
        name="s5_single_step",
    )(*args)


def _s5_operators(a_re, a_im, log_dt, b_re, b_im, c_re, c_im):
    hp = lax.Precision.HIGHEST
    g, n = a_re.shape
    gu = SSM_UNIT
    nunit = g // gu
    L = SSM_CHUNK
    uw = gu * SSM_GROUP
    sw = gu * n
    a = lax.complex(a_re.astype(F32), a_im.astype(F32))
    dt = jnp.exp(log_dt.astype(F32))[:, None]
    a_bar = jnp.exp(a * dt)
    b_bar = ((a_bar - 1.0) / a)[:, :, None] * lax.complex(b_re.astype(F32), b_im.astype(F32))
    c = lax.complex(c_re.astype(F32), c_im.astype(F32))
    pows = [jnp.ones_like(a_bar)]
    for _ in range(L):
        pows.append(pows[-1] * a_bar)
    a_pow = jnp.stack(pows).reshape(L + 1, nunit, sw)
    apr = jnp.real(a_pow)
    api = jnp.imag(a_pow)
    eye = jnp.eye(gu, dtype=F32)

    def bd_in(x):
        return jnp.einsum("pgnd,gh->pgdhn", x.reshape(nunit, gu, n, SSM_GROUP), eye).reshape(nunit, uw, sw)

    def bd_out(x):
        return jnp.einsum("pgcn,gh->pgnhc", x.reshape(nunit, gu, SSM_GROUP, n), eye).reshape(nunit, sw, uw)

    p0r, p0i = bd_in(jnp.real(b_bar)), bd_in(jnp.imag(b_bar))
    q0r, q0i = bd_out(jnp.real(c)), bd_out(jnp.imag(c))
    lanes = lambda x, k: x[k][:, None, :]
    rows_ = lambda x, k: x[k][:, :, None]
    p_re = jnp.concatenate([p0r * lanes(apr, L - 1 - t) - p0i * lanes(api, L - 1 - t) for t in range(L)], axis=1)
    p_im = jnp.concatenate([p0r * lanes(api, L - 1 - t) + p0i * lanes(apr, L - 1 - t) for t in range(L)], axis=1)
    q_re = jnp.concatenate([q0r * rows_(apr, t + 1) - q0i * rows_(api, t + 1) for t in range(L)], axis=2)
    q_im = jnp.concatenate([-(q0r * rows_(api, t + 1) + q0i * rows_(apr, t + 1)) for t in range(L)], axis=2)
    lag_blocks = [jnp.zeros((nunit, uw, uw), F32)]
    for t in range(L):
        xr = p0r * lanes(apr, t) - p0i * lanes(api, t)
        xi = p0r * lanes(api, t) + p0i * lanes(apr, t)
        lag_blocks.append(jnp.einsum("pus,psv->puv", xr, q0r, precision=hp)
                          - jnp.einsum("pus,psv->puv", xi, q0i, precision=hp))
    brow = jnp.concatenate(lag_blocks, axis=2)
    lane_row = lambda x: x.reshape(nunit, 1, sw)
    return {
        "brow": brow.astype(BF16),
        "p_re": p_re.astype(BF16), "p_im": p_im.astype(BF16),
        "q_re": q_re.astype(BF16), "q_im": q_im.astype(BF16),
        "a_chunk_re": lane_row(apr[L]), "a_chunk_im": lane_row(api[L]),
        "a1_re": lane_row(apr[1]), "a1_im": lane_row(api[1]),
        "b1_re": p0r.astype(BF16), "b1_im": p0i.astype(BF16),
        "c1_re": q0r.astype(BF16), "c1_im": (-q0i).astype(BF16),
    }


def _compress_paged_kernel(pt_ref, *refs, pg, ns, nsb):
    del pt_ref
    pages = refs[:ns * pg]
    perm_ref, wbd_ref, pe_ref, w2_ref, out_ref, h0_ref, h1_ref = refs[ns * pg:]
    j = pl.program_id(1)
    sbp = PAGE_SIZE // CMP_STRIDE
    nrows = pg * sbp
    pair_rows = 2 * sbp

    @pl.when(j == 0)
    def _():
        h1_ref[:, :, nsb:nsb + SUBLANES, :] = jnp.zeros((ns, 2, SUBLANES, KV_W), F32)

    r0 = pl.multiple_of(j * nrows, nrows)
    for kv in range(2):
        staged = []
        for q in range(ns * pg // 2):
            z = jnp.concatenate([pages[2 * q][0, kv], pages[2 * q + 1][0, kv]], axis=1).astype(BF16)
            staged.append(_dot_nt(perm_ref[...], z).astype(BF16))

        def load_rows(s, kv, staged=staged):
            return jnp.concatenate([x[s * pair_rows:(s + 1) * pair_rows, :] for x in staged], axis=0)

        pr0, pr1 = _compress_half(load_rows, kv, ns * nrows, wbd_ref, pe_ref)
        for u in range(ns):
            h0_ref[u, kv, pl.ds(r0, nrows), :] = pr0[u * nrows:(u + 1) * nrows, :]
            h1_ref[u, kv, pl.ds(r0, nrows), :] = pr1[u * nrows:(u + 1) * nrows, :]

    @pl.when(j == pl.num_programs(1) - 1)
    def _():
        for u in range(ns):
            for kv in range(2):
                h = h0_ref[u, kv] + h1_ref[u, kv, pl.ds(1, nsb), :]
                out = _dot(jax.nn.gelu(h).astype(BF16), w2_ref[kv])
                out_ref[u, :, kv * KV_W:(kv + 1) * KV_W] = out.astype(BF16)


def _page_specs(pg, ns, pages_per_sample):
    def spec(u, i):
        return pl.BlockSpec((1, 2, KV_W, PAGE_SIZE),
                            lambda b, j, pt: (pt[(b * ns + u) * pages_per_sample + j * pg + i], 0, 0, 0))
    return [spec(u, i) for u in range(ns) for i in range(pg)]


def _compress_paged(pages, pt_flat, wbd, pe_t, w2bd, *, nb, pages_per_sample, pg):
    nsb = pages_per_sample * PAGE_SIZE // CMP_STRIDE
    sbp = PAGE_SIZE // CMP_STRIDE
    ns = SAMPLES_PER_STEP if nb % SAMPLES_PER_STEP == 0 else 1
    i_out = np.arange(2 * PAGE_SIZE)
    s_i, pg_i, n_i = i_out // (2 * sbp), (i_out // sbp) % 2, i_out % sbp
    perm = jnp.asarray(i_out[None, :] == (pg_i * PAGE_SIZE + n_i * CMP_STRIDE + s_i)[:, None], BF16)
    c3 = lambda shape: pl.BlockSpec(shape, lambda b, j, pt: (0,) * len(shape))
    grid_spec = pltpu.PrefetchScalarGridSpec(
        num_scalar_prefetch=1,
        grid=(nb // ns, pages_per_sample // pg),
        in_specs=_page_specs(pg, ns, pages_per_sample) + [c3(perm.shape), c3(wbd.shape), c3(pe_t.shape),
                                                          c3(w2bd.shape)],
        out_specs=pl.BlockSpec((ns, nsb, 2 * KV_W), lambda b, j, pt: (b, 0, 0)),
        scratch_shapes=[pltpu.VMEM((ns, 2, nsb, KV_W), F32), pltpu.VMEM((ns, 2, nsb + SUBLANES, KV_W), F32)],
    )
    return pl.pallas_call(
        functools.partial(_compress_paged_kernel, pg=pg, ns=ns, nsb=nsb),
        grid_spec=grid_spec,
        out_shape=jax.ShapeDtypeStruct((nb, nsb, 2 * KV_W), BF16),
        compiler_params=_params("parallel", "arbitrary"),
        name="nsa_compress_paged",
    )(pt_flat, *([pages] * (ns * pg)), perm, wbd, pe_t, w2bd)


def _topk_mask_lanes(imp, n_top, ns_valid):
    lane = lax.broadcasted_iota(jnp.int32, imp.shape, 1)
    cnt = jnp.zeros(imp.shape, F32)
    for sp in range(ns_valid):
        col = imp[:, sp:sp + 1]
        cnt = cnt + jnp.where(lane > sp, jnp.where(col >= imp, 1.0, 0.0), jnp.where(col > imp, 1.0, 0.0))
    return jnp.where((cnt < n_top) & (lane < ns_valid), 1.0, 0.0)


def _dot_nt(a, b):
    return lax.dot_general(a, b, (((1,), (1,)), ((), ())), preferred_element_type=F32)


def _attn_sample_kernel(pt_ref, *refs, pg, ns, past, nc, ns_valid, n_top):
    del pt_ref
    all_pages = [refs[u * pg:(u + 1) * pg] for u in range(ns)]
    (q_ref, qr_ref, g_ref, kcvc_ref, ksn_ref, kwn_ref, win_ref, ov_ref, e_ref, gs_ref, o_ref,
     m_ref, l_ref, acc_ref, sel_ref, oc_ref, ow_ref) = refs[ns * pg:]
    j = pl.program_id(1)
    ncp = kcvc_ref.shape[1]
    nsp = ov_ref.shape[1]
    wlen = win_ref.shape[3]
    row = lax.broadcasted_iota(jnp.int32, (N_HEADS, KV_W), 0)
    lane = lax.broadcasted_iota(jnp.int32, (N_HEADS, KV_W), 1)
    own = (lane // HEAD_DIM) == (row // GROUP)

    def spread(ref, u):
        q = ref[u]
        return jnp.where(own, jnp.concatenate([q] * N_KV_HEADS, axis=1), jnp.zeros((N_HEADS, KV_W), BF16))

    def update(state, s, vs):
        m, l, acc = state
        m_new = jnp.maximum(m, jnp.max(s, axis=1, keepdims=True))
        alpha = jnp.exp(m - m_new)
        p = jnp.exp(s - m_new)
        l = alpha * l + jnp.sum(p, axis=1, keepdims=True)
        pv = None
        for st, sz, v, feature_major in vs:
            pb = p[:, st:st + sz].astype(BF16)
            t = _dot_nt(pb, v) if feature_major else _dot(pb, v)
            pv = t if pv is None else pv + t
        return m_new, l, alpha * acc + pv

    def init():
        return (jnp.full((N_HEADS, 1), NEG, F32), jnp.zeros((N_HEADS, 1), F32), jnp.zeros((N_HEADS, KV_W), F32))

    def new_row_update(state, qbd, new_row):
        r8 = lax.broadcasted_iota(jnp.int32, (SUBLANES, 2 * KV_W), 0)
        tile = jnp.where(r8 == 0, jnp.broadcast_to(new_row, (SUBLANES, 2 * KV_W)), 0.0).astype(BF16)
        s = _dot_nt(qbd, tile[:, 0:KV_W])
        l8 = lax.broadcasted_iota(jnp.int32, (N_HEADS, SUBLANES), 1)
        s = jnp.where(l8 == 0, s, MASKED)
        return update(state, s, [(0, SUBLANES, tile[:, KV_W:2 * KV_W], False)])

    def put(u, state):
        m, l, acc = state
        m_ref[u] = jnp.broadcast_to(m, m_ref.shape[1:])
        l_ref[u] = jnp.broadcast_to(l, l_ref.shape[1:])
        acc_ref[u] = acc

    qrbd = [spread(qr_ref, u) for u in range(ns)]

    @pl.when(j == 0)
    def _():
        cidx = lax.broadcasted_iota(jnp.int32, (1, ncp), 1)
        valid = ((cidx * CMP_STRIDE + (CMP_BLOCK - 1)) <= past) & (cidx < nc)
        sidx = lax.broadcasted_iota(jnp.int32, (1, nsp), 1)
        cur = past // SEL_BLOCK
        forced = (sidx == 0) | (sidx == cur) | (sidx == cur - 1)
        causal = (sidx * SEL_BLOCK) <= past
        wpos = past - wlen + lax.broadcasted_iota(jnp.int32, (1, wlen), 1)
        in_window = (wpos >= 0) & (past - wpos <= WINDOW)
        for u in range(ns):
            s = _dot_nt(spread(q_ref, u), kcvc_ref[u, :, 0:KV_W])
            sm = jnp.where(valid, s, NEG)
            mx = jnp.max(sm, axis=1, keepdims=True)
            e = jnp.where(valid, jnp.exp(sm - mx), 0.0)
            p = e / jnp.maximum(jnp.sum(e, axis=1, keepdims=True), 1e-20)
            oc_ref[u] = _dot(p.astype(BF16), kcvc_ref[u, :, KV_W:2 * KV_W])
            imp = _dot_f32rhs(_dot_f32lhs(gs_ref[...], p), ov_ref[...])
            imp = jnp.where(forced, FORCE, jnp.where(causal, imp, NEG))
            imp = jnp.where(sidx < ns_valid, imp, MASKED)
            sel_ref[u] = _topk_mask_lanes(imp, n_top, ns_valid)
            sw = jnp.where(in_window, _dot(qrbd[u], win_ref[u, 0].astype(BF16)), MASKED)
            st = update(init(), sw, [(0, wlen, win_ref[u, 1].astype(BF16), True)])
            st = new_row_update(st, qrbd[u], kwn_ref[u])
            ow_ref[u] = st[2] * (1.0 / jnp.maximum(st[1], 1e-20))
            put(u, new_row_update(init(), qrbd[u], ksn_ref[u]))

    def halves(pages):
        return [pages[:pg // 2], pages[pg // 2:]] if pg > 1 else [pages]

    scores = [[jnp.concatenate([_dot(qrbd[u], r[0, 0].astype(BF16)) for r in part], axis=1)
               for part in halves(all_pages[u])] for u in range(ns)]
    masks = [_dot(sel_ref[u].astype(BF16), e_ref[0]) > 0.5 for u in range(ns)]
    states = [(m_ref[u, :, 0:1], l_ref[u, :, 0:1], acc_ref[u]) for u in range(ns)]
    k0 = 0
    for hx, part0 in enumerate(halves(all_pages[0])):
        width = len(part0) * PAGE_SIZE
        for u in range(ns):
            part = halves(all_pages[u])[hx]
            s = jnp.where(masks[u][:, k0:k0 + width], scores[u][hx], MASKED)
            vs = [(i * PAGE_SIZE, PAGE_SIZE, r[0, 1].astype(BF16), True) for i, r in enumerate(part)]
            states[u] = update(states[u], s, vs)
        k0 += width
    for u in range(ns):
        put(u, states[u])

    @pl.when(j == pl.num_programs(1) - 1)
    def _():
        for u in range(ns):
            g = g_ref[u]
            os_ = acc_ref[u] * (1.0 / jnp.maximum(l_ref[u, :, 0:1], 1e-20))
            o = g[:, 0:1] * oc_ref[u] + g[:, 1:2] * os_ + g[:, 2:3] * ow_ref[u]
            o = jnp.where(own, o, 0.0)
            out = o[:, 0:HEAD_DIM]
            for h in range(1, N_KV_HEADS):
                out = out + o[:, h * HEAD_DIM:(h + 1) * HEAD_DIM]
            o_ref[u] = out.astype(BF16)


def _attn_sample(pages, pt_flat, q3, qr3, g3, kcvc, ks_new, kw_new, win, ov, e_mat, gsum, *, nb, pages_per_sample,
                 pg, past, nc, ns_valid, n_top):
    nsp = ov.shape[1]
    ns = ATTN_SAMPLES_PER_STEP if nb % ATTN_SAMPLES_PER_STEP == 0 else 1
    per_b = lambda shape: pl.BlockSpec((ns,) + shape, lambda b, j, pt: (b,) + (0,) * len(shape))
    const = lambda shape: pl.BlockSpec(shape, lambda b, j, pt: (0,) * len(shape))

    per_sample = lambda *shape: pltpu.VMEM((ns,) + shape, F32)
    grid_spec = pltpu.PrefetchScalarGridSpec(
        num_scalar_prefetch=1,
        grid=(nb // ns, pages_per_sample // pg),
        in_specs=_page_specs(pg, ns, pages_per_sample) + [
            per_b((N_HEADS, HEAD_DIM)), per_b((N_HEADS, HEAD_DIM)), per_b((N_HEADS, 3)),
            per_b(kcvc.shape[1:]), per_b((1, 2 * KV_W)), per_b((1, 2 * KV_W)), per_b(win.shape[1:]),
            const(ov.shape), pl.BlockSpec((1,) + e_mat.shape[1:], lambda b, j, pt: (j, 0, 0)), const(gsum.shape)],
        out_specs=per_b((N_HEADS, HEAD_DIM)),
        scratch_shapes=[per_sample(N_HEADS, LANES), per_sample(N_HEADS, LANES), per_sample(N_HEADS, KV_W),
                        per_sample(N_HEADS, nsp), per_sample(N_HEADS, KV_W), per_sample(N_HEADS, KV_W)],
    )
    return pl.pallas_call(
        functools.partial(_attn_sample_kernel, pg=pg, ns=ns, past=past, nc=nc, ns_valid=ns_valid, n_top=n_top),
        grid_spec=grid_spec,
        out_shape=jax.ShapeDtypeStruct((nb, N_HEADS, HEAD_DIM), BF16),
        compiler_params=_params("parallel", "arbitrary"),
        name="nsa_attn_sample",
    )(pt_flat, *([pages] * (ns * pg)), q3, qr3, g3, kcvc, ks_new, kw_new, win, ov, e_mat, gsum)


def _rope_tables(pos):
    half = HEAD_DIM // 2
    inv = ROPE_THETA ** (-jnp.arange(half, dtype=F32) / half)
    ang = pos.astype(F32)[:, None] * inv[None, :]
    cos = jnp.cos(ang)
    sin = jnp.sin(ang)
    reps = LANES // HEAD_DIM
    return (jnp.tile(jnp.concatenate([cos, cos], axis=1), (1, reps)),
            jnp.tile(jnp.concatenate([-sin, sin], axis=1), (1, reps)))


def _compress_weights(cmp_w1, cmp_w2, cmp_pe):
    ratio = CMP_BLOCK // CMP_STRIDE
    w1r = cmp_w1.reshape(2, ratio, CMP_STRIDE, HEAD_DIM, HEAD_DIM)
    hh = np.arange(KV_W) // HEAD_DIM
    same_head = jnp.asarray(hh[:, None] == hh[None, :])

    def block_diag(w):
        tiled = jnp.concatenate([jnp.concatenate([w] * N_KV_HEADS, axis=-1)] * N_KV_HEADS, axis=-2)
        return jnp.where(same_head, tiled, 0.0).astype(BF16)

    wbd = block_diag(w1r.reshape(2 * ratio * CMP_STRIDE, HEAD_DIM, HEAD_DIM))
    w2bd = block_diag(cmp_w2)
    pe_r = cmp_pe.reshape(2, ratio, CMP_STRIDE, HEAD_DIM).astype(F32)
    pe_w = jnp.einsum("krsd,krsde->kre", pe_r, w1r.astype(F32), precision=lax.Precision.HIGHEST)
    pe_t = jnp.tile(pe_w.reshape(2 * ratio, HEAD_DIM), (1, N_KV_HEADS))
    return wbd, pe_t, w2bd


def _overlap(nc, ncp, nsel, nsp):
    c_start = np.arange(ncp)[:, None] * CMP_STRIDE
    s_start = np.arange(nsp)[None, :] * SEL_BLOCK
    ov = (c_start < s_start + SEL_BLOCK) & (c_start + CMP_BLOCK > s_start)
    ov = ov & (np.arange(ncp)[:, None] < nc) & (np.arange(nsp)[None, :] < nsel)
    return jnp.asarray(ov, BF16)


def _tile(n, pref):
    t = min(n, pref)
    while n % t:
        t //= 2
    return t


def kernel(x_prompt, x_sample, cache_kv_cmp, cache_kv_sel, cache_kv_win, state_ssm, page_table, norm_g, mlp_w1,
           mlp_w2, nsa_w_in, nsa_w_o, nsa_cmp_w1, nsa_cmp_w2, nsa_cmp_pe, s5_a_re, s5_a_im, s5_log_dt, s5_b_re,
           s5_b_im, s5_c_re, s5_c_im, s5_d, s5_w_glu, s5_b_glu):
    b, t, d = x_prompt.shape
    nb = x_sample.shape[0]
    pages_per_sample = page_table.shape[1]
    past = pages_per_sample * PAGE_SIZE
    rows_p = b * t
    g = norm_g.reshape(norm_g.shape[0], 4, 1, d)

    w_in = nsa_w_in[0]
    w_q = w_in[:, :Q_W].astype(BF16)
    w_kv = w_in[:, Q_W:Q_W + 6 * KV_W].astype(BF16)
    w_gate = jnp.pad(w_in[:, Q_W + 6 * KV_W:], ((0, 0), (0, LANES - 3 * N_HEADS))).astype(BF16)
    w_o = nsa_w_o[0].astype(BF16)
    wbd, pe_t, w2bd = _compress_weights(nsa_cmp_w1[0], nsa_cmp_w2[0], nsa_cmp_pe[0])
    w1 = mlp_w1.astype(BF16)
    w2 = mlp_w2.astype(BF16)
    w_glu = s5_w_glu[0].astype(BF16)
    ops = _s5_operators(s5_a_re[0], s5_a_im[0], s5_log_dt[0], s5_b_re[0], s5_b_im[0], s5_c_re[0], s5_c_im[0])
    d_skip = s5_d[0].reshape(1, d)
    b_glu = s5_b_glu[0].reshape(1, d)

    tm = _tile(rows_p, ROW_TILE)
    ff_chunk = _tile(mlp_w1.shape[2], FF_CHUNK)

    xp = x_prompt.reshape(rows_p, d)
    cos_p, sin_p = _rope_tables(jnp.arange(t, dtype=jnp.int32))
    tq = next(c for c in (2 * LANES, LANES) if t % c == 0 and WINDOW % c == 0)
    (qT, qrT, gT, kvc, kvcT, kvsT, kvwT, ksb, kwb, vsT, vwT) = _inproj(
        xp, g[0, 0], w_q.T, w_kv, w_gate, cos_p, sin_p, tm=_tile(t, ROW_TILE), pos_blocks=t // _tile(t, ROW_TILE),
        transposed=True,
        key_chunk=tq)
    nsb_p = t // CMP_STRIDE
    nc_p = nsb_p - CMP_BLOCK // CMP_STRIDE + 1
    nsel_p = t // SEL_BLOCK
    kc, vcT = _compress_prompt(kvc.reshape(b, t, 2 * KV_W), wbd, pe_t, w2bd)
    ovT = _overlap(nc_p, nsb_p, nsel_p, nsel_p).T
    oT = _attn_prompt(qT, qrT, gT, kc, vcT, ksb, vsT, kwb, vwT, ovT, batch=b, seq=t, tq=tq, nc=nc_p,
                      n_top=min(TOP_N, nsel_p))
    hp, xn1 = _layer_tail((oT, w_o), xp, g[0, 1], g[0, 2], w1[0], w2[0], g[0, 3], g[1, 0], mixer="nsa", tm=tm,
                          ff_chunk=ff_chunk, transposed=True)

    pw = SSM_UNIT * SSM_GROUP
    npair = d // pw
    y3, hfin = _s5_seq(xn1.reshape(b, t, d), ops)
    yc = y3.reshape(rows_p, d)
    (hp,) = _layer_tail((yc, xn1, d_skip, w_glu, b_glu), hp, g[1, 1], g[1, 2], w1[1], w2[1], g[1, 3], None,
                        mixer="s5", tm=tm, ff_chunk=ff_chunk)
    ssm_p = hfin.reshape(npair, 2, b, SSM_UNIT, SSM_STATE).transpose(2, 1, 0, 3, 4)
    ssm_p = ssm_p.reshape(b, 2, d // SSM_GROUP, SSM_STATE)

    xs = x_sample.reshape(nb, d)
    cos_s, sin_s = _rope_tables(jnp.full((nb,), past, dtype=jnp.int32))
    q_s, qr_s, gates_s, kvc_s, kvs_s, kvw_s = _inproj(
        xs, g[0, 0], w_q, w_kv, w_gate, cos_s, sin_s, tm=nb, pos_blocks=1, transposed=False)
    pt_flat = page_table.reshape(-1).astype(jnp.int32)
    pg = _tile(pages_per_sample, PAGE_GROUP)
    n_pool = cache_kv_cmp.shape[1]
    feature_major = lambda c, n, s: c.transpose(0, 2, 3, 4, 1).reshape(n, 2, KV_W, s)
    cmp_pages = feature_major(cache_kv_cmp[0], n_pool, PAGE_SIZE)
    sel_pages = feature_major(cache_kv_sel[0], n_pool, PAGE_SIZE)
    kcvc = _compress_paged(cmp_pages, pt_flat, wbd, pe_t, w2bd, nb=nb, pages_per_sample=pages_per_sample, pg=pg)
    l_all = past + 1
    nsb_s = l_all // CMP_STRIDE
    nc_s = nsb_s - CMP_BLOCK // CMP_STRIDE + 1
    nsel_s = -(-l_all // SEL_BLOCK)
    nsp = -(-nsel_s // LANES) * LANES
    ov_s = _overlap(nc_s, past // CMP_STRIDE, nsel_s, nsp)
    keys_per_step = pg * PAGE_SIZE
    key_blk = (np.arange(past) // SEL_BLOCK).reshape(past // keys_per_step, 1, keys_per_step)
    e_mat = jnp.asarray(np.arange(nsp)[None, :, None] == key_blk, BF16)
    hh = np.arange(N_HEADS)
    gsum = jnp.asarray((hh[:, None] // GROUP) == (hh[None, :] // GROUP), BF16)
    win = feature_major(cache_kv_win[0], nb, WINDOW)
    o_s = _attn_sample(sel_pages, pt_flat, q_s.reshape(nb, N_HEADS, HEAD_DIM), qr_s.reshape(nb, N_HEADS, HEAD_DIM),
                       gates_s[:, :3 * N_HEADS].reshape(nb, N_HEADS, 3), kcvc, kvs_s.reshape(nb, 1, 2 * KV_W),
                       kvw_s.reshape(nb, 1, 2 * KV_W), win, ov_s, e_mat, gsum, nb=nb,
                       pages_per_sample=pages_per_sample, pg=pg, past=past, nc=nc_s, ns_valid=nsel_s,
                       n_top=min(TOP_N, nsel_s))
    hs, xn1_s = _layer_tail((o_s.reshape(nb, Q_W), w_o), xs, g[0, 1], g[0, 2], w1[0], w2[0], g[0, 3], g[1, 0],
                            mixer="nsa", tm=nb, ff_chunk=ff_chunk)

    u2_s = xn1_s.astype(BF16).reshape(nb, npair, pw).transpose(1, 0, 2)
    st = state_ssm[0].reshape(nb, 2, npair, SSM_UNIT * SSM_STATE).transpose(1, 2, 0, 3)
    y2_s, hr_s, hi_s = _s5_step(u2_s, st[0], st[1], ops)
    yc_s = y2_s.transpose(1, 0, 2).reshape(nb, d)
    (hs,) = _layer_tail((yc_s, xn1_s, d_skip, w_glu, b_glu), hs, g[1, 1], g[1, 2], w1[1], w2[1], g[1, 3], None,
                        mixer="s5", tm=nb, ff_chunk=ff_chunk)
    ssm_s = jnp.stack([hr_s, hi_s], axis=0).transpose(2, 0, 1, 3).reshape(nb, 2, d // SSM_GROUP, SSM_STATE)

    kv5 = lambda a, n, s: a.reshape(1, n, s, 2, N_KV_HEADS, HEAD_DIM)
    from_fm = lambda a, n, s: a.reshape(n, 2, N_KV_HEADS, HEAD_DIM, s).transpose(0, 4, 1, 2, 3)[None]
    win_s = jnp.concatenate([win[..., 1:], kvw_s.reshape(nb, 2, KV_W, 1)], axis=-1)
    return (hp.reshape(b, t, d), hs.reshape(nb, 1, d),
            from_fm(kvcT, b, t), kv5(kvc_s, nb, 1), from_fm(kvsT, b, t), kv5(kvs_s, nb, 1),
            from_fm(kvwT[:, :, t - WINDOW:], b, WINDOW), from_fm(win_s, nb, WINDOW), ssm_p[None], ssm_s[None])
```

```python
import functools

import jax
import jax.numpy as jnp
import numpy as np
from jax import lax
from jax.experimental import pallas as pl
from jax.experimental.pallas import tpu as pltpu

N_HEADS = 16
HEAD_DIM = 64
N_KV_HEADS = 4
GROUP = N_HEADS // N_KV_HEADS
CMP_BLOCK = 32
CMP_STRIDE = 16
SEL_BLOCK = 64
TOP_N = 16
WINDOW = 512
ROPE_THETA = 10000.0
PAGE_SIZE = 128
SSM_GROUP = 16
SSM_STATE = 64
SSM_CHUNK = 8
SSM_UNIT = 8
EPS = 1e-6
NEG = -1e30
FORCE = 1e9
MASKED = -1.5e38
LOG2E = 1.4426950408889634
V_ROWS = HEAD_DIM + 16
Q_W = N_HEADS * HEAD_DIM
KV_W = N_KV_HEADS * HEAD_DIM
LANES = 128
SUBLANES = 8
ROW_TILE = 512
FF_CHUNK = 1024
PAGE_GROUP = 16
SAMPLES_PER_STEP = 2
ATTN_SAMPLES_PER_STEP = 4
S5_TIME_BLOCK = 2048
VMEM_LIMIT = 56 * 1024 * 1024

F32 = jnp.float32
BF16 = jnp.bfloat16


def _params(*sem):
    return pltpu.CompilerParams(dimension_semantics=sem, vmem_limit_bytes=VMEM_LIMIT)


def _full(shape):
    zeros = (0,) * len(shape)
    return pl.BlockSpec(shape, lambda *_: zeros)


def _rms(x, g):
    ms = jnp.mean(x * x, axis=-1, keepdims=True)
    return x * lax.rsqrt(ms + EPS) * g


def _dot(a, b):
    return jnp.dot(a, b, preferred_element_type=F32)


def _dot_f32lhs(w, x):
    hi = x.astype(BF16)
    r1 = x - hi.astype(F32)
    mid = r1.astype(BF16)
    lo = (r1 - mid.astype(F32)).astype(BF16)
    return _dot(w, hi) + _dot(w, mid) + _dot(w, lo)


def _dot_f32rhs(x, w):
    hi = x.astype(BF16)
    r1 = x - hi.astype(F32)
    mid = r1.astype(BF16)
    lo = (r1 - mid.astype(F32)).astype(BF16)
    return _dot(hi, w) + _dot(mid, w) + _dot(lo, w)


def _rope_nat(x, cos, sin):
    half = HEAD_DIM // 2
    lane = lax.broadcasted_iota(jnp.int32, (1, LANES), 1)
    first = (lane % HEAD_DIM) < half
    outs = []
    for c in range(x.shape[1] // LANES):
        xc = x[:, c * LANES:(c + 1) * LANES]
        rot = jnp.where(first, pltpu.roll(xc, LANES - half, 1), pltpu.roll(xc, half, 1))
        outs.append(xc * cos + rot * sin)
    return jnp.concatenate(outs, axis=1)


def _inproj_kernel(x_ref, g_ref, wq_ref, wkv_ref, wg_ref, cos_ref, sin_ref, *rest, transposed, key_chunk):
    outs = rest[2:] if transposed else rest
    xb = _rms(x_ref[...], g_ref[...]).astype(BF16)
    cos = cos_ref[...]
    sin = sin_ref[...]
    scale = HEAD_DIM ** -0.5
    kv = _dot(xb, wkv_ref[...])
    gates = jax.nn.sigmoid(_dot(xb, wg_ref[...]))
    k_s = _rope_nat(kv[:, 2 * KV_W:3 * KV_W], cos, sin)
    v_s = kv[:, 3 * KV_W:4 * KV_W]
    k_w = _rope_nat(kv[:, 4 * KV_W:5 * KV_W], cos, sin)
    v_w = kv[:, 5 * KV_W:6 * KV_W]
    kvc_ref = outs[3]
    kvc_ref[...] = kv[:, 0:2 * KV_W]
    if transposed:
        qT_ref, qrT_ref, gT_ref, _, kvcT_ref, kvsT_ref, kvwT_ref, ksb_ref, kwb_ref, vsT_ref, vwT_ref = outs
        qT = _dot_nt(wq_ref[...], xb)
        cosT = rest[0][...]
        sinT = rest[1][...]
        half = HEAD_DIM // 2
        rotated = []
        for h in range(N_HEADS):
            x1 = qT[h * HEAD_DIM:h * HEAD_DIM + half, :]
            x2 = qT[h * HEAD_DIM + half:(h + 1) * HEAD_DIM, :]
            rotated += [x1 * cosT - x2 * sinT, x2 * cosT + x1 * sinT]
        qT_ref[...] = (qT * (scale * LOG2E)).astype(BF16)
        qrT_ref[...] = (jnp.concatenate(rotated, axis=0) * (scale * LOG2E)).astype(BF16)
        gT_ref[...] = gates.T
        tm = x_ref.shape[0]
        rowi = lax.broadcasted_iota(jnp.int32, (tm, LANES), 0)
        lanei = lax.broadcasted_iota(jnp.int32, (tm, LANES), 1)
        blk = lax.shift_right_logical(rowi & (key_chunk - 1), SEL_BLOCK.bit_length() - 1)
        extra = lanei - HEAD_DIM
        nblk = key_chunk // SEL_BLOCK
        aug = jnp.where((extra == blk) | ((extra >= nblk) & (extra < nblk + N_CONST_LANES)), 1.0, 0.0)
        for c in range(KV_W // LANES):
            for k_nat, k_ref in ((k_s, ksb_ref), (k_w, kwb_ref)):
                pair = k_nat[:, c * LANES:(c + 1) * LANES]
                k_ref[2 * c] = jnp.where(lanei < HEAD_DIM, pair, aug).astype(BF16)
                k_ref[2 * c + 1] = jnp.where(lanei < HEAD_DIM, pltpu.roll(pair, HEAD_DIM, 1), aug).astype(BF16)
        v_sT = v_s.T
        v_wT = v_w.T
        ones_rows = jnp.where(lax.broadcasted_iota(jnp.int32, (V_ROWS - HEAD_DIM, tm), 0) == 0, 1.0, 0.0)
        for h in range(N_KV_HEADS):
            for vT, v_ref in ((v_sT, vsT_ref), (v_wT, vwT_ref)):
                v_ref[h * V_ROWS:(h + 1) * V_ROWS, :] = jnp.concatenate(
                    [vT[h * HEAD_DIM:(h + 1) * HEAD_DIM, :], ones_rows], axis=0).astype(BF16)
        kvcT_ref[0] = kv[:, 0:2 * KV_W].T
        kvsT_ref[0, 0:KV_W, :] = k_s.T
        kvsT_ref[0, KV_W:2 * KV_W, :] = v_sT
        kvwT_ref[0, 0:KV_W, :] = k_w.T
        kvwT_ref[0, KV_W:2 * KV_W, :] = v_wT
    else:
        q_ref, qr_ref, gt_ref, _, kvs_ref, kvw_ref = outs
        q = _dot(xb, wq_ref[...])
        qr = _rope_nat(q, cos, sin)
        q_ref[...] = (q * scale).astype(BF16)
        qr_ref[...] = (qr * scale).astype(BF16)
        gt_ref[...] = gates
        kvs_ref[:, 0:KV_W] = k_s
        kvs_ref[:, KV_W:2 * KV_W] = v_s
        kvw_ref[:, 0:KV_W] = k_w
        kvw_ref[:, KV_W:2 * KV_W] = v_w


def _inproj(x, g, w_q, w_kv, w_gate, cos_t, sin_t, *, tm, pos_blocks, transposed, key_chunk=LANES):
    rows, d = x.shape
    assert tm % key_chunk == 0 or not transposed
    n = rows // tm
    row_blk = lambda w: pl.BlockSpec((tm, w), lambda i: (i, 0))
    col_blk = lambda h: pl.BlockSpec((h, tm), lambda i: (0, i))
    tab = pl.BlockSpec((tm, LANES), lambda i: (i % pos_blocks, 0))
    kv_nat = jax.ShapeDtypeStruct((rows, 2 * KV_W), F32)
    if transposed:
        seqs = rows // (pos_blocks * tm)
        kh = pl.BlockSpec((N_KV_HEADS, tm, LANES), lambda i: (0, i, 0))
        kvT = pl.BlockSpec((1, 2 * KV_W, tm), lambda i: (i // pos_blocks, 0, i % pos_blocks))
        out_shape = ([jax.ShapeDtypeStruct((Q_W, rows), BF16)] * 2 + [jax.ShapeDtypeStruct((LANES, rows), F32)]
                     + [kv_nat] + [jax.ShapeDtypeStruct((seqs, 2 * KV_W, pos_blocks * tm), F32)] * 3
                     + [jax.ShapeDtypeStruct((N_KV_HEADS, rows, LANES), BF16)] * 2
                     + [jax.ShapeDtypeStruct((N_KV_HEADS * V_ROWS, rows), BF16)] * 2)
        out_specs = ([col_blk(Q_W)] * 2 + [col_blk(LANES)] + [row_blk(2 * KV_W)] + [kvT] * 3 + [kh] * 2
                     + [col_blk(N_KV_HEADS * V_ROWS)] * 2)
    else:
        out_shape = ([jax.ShapeDtypeStruct((rows, Q_W), BF16)] * 2 + [jax.ShapeDtypeStruct((rows, LANES), F32)]
                     + [kv_nat] * 3)
        out_specs = [row_blk(Q_W)] * 2 + [row_blk(LANES)] + [row_blk(2 * KV_W)] * 3
    tables = [cos_t, sin_t]
    table_specs = [tab, tab]
    if transposed:
        half = HEAD_DIM // 2
        tables += [cos_t[:, 0:half].T, sin_t[:, half:HEAD_DIM].T]
        table_specs += [pl.BlockSpec((half, tm), lambda i: (0, i % pos_blocks))] * 2
    return pl.pallas_call(
        functools.partial(_inproj_kernel, transposed=transposed, key_chunk=key_chunk),
        grid=(n,),
        in_specs=[row_blk(d), _full((1, d)), _full(w_q.shape), _full(w_kv.shape), _full(w_gate.shape)] + table_specs,
        out_specs=out_specs,
        out_shape=out_shape,
        compiler_params=_params("parallel"),
        name="nsa_inproj",
    )(x, g, w_q, w_kv, w_gate, *tables)


_KV_CHUNKS = 2 * KV_W // LANES


def _compress_half(load_rows, kv, nrows, wbd_ref, pe_ref):
    ratio = CMP_BLOCK // CMP_STRIDE
    accs = [jnp.broadcast_to(pe_ref[kv * ratio + r:kv * ratio + r + 1, :], (nrows, KV_W)) for r in range(ratio)]
    for s in range(CMP_STRIDE):
        lhs = load_rows(s, kv).astype(BF16)
        for r in range(ratio):
            accs[r] = accs[r] + _dot(lhs, wbd_ref[(kv * ratio + r) * CMP_STRIDE + s])
    return accs


def _compress_prompt_kernel(*refs, nsb):
    x_refs = refs[:_KV_CHUNKS]
    wbd_ref, pe_ref, w2_ref, kc_ref, vcT_ref, sh_ref = refs[_KV_CHUNKS:]

    def load_rows(s, kv):
        per_half = _KV_CHUNKS // 2
        return jnp.concatenate([x_refs[kv * per_half + c][0, pl.ds(s, nsb, stride=CMP_STRIDE), :]
                                for c in range(per_half)], axis=1)

    sh_ref[nsb:nsb + SUBLANES, :] = jnp.zeros((SUBLANES, KV_W), F32)
    for kv in range(2):
        pr0, pr1 = _compress_half(load_rows, kv, nsb, wbd_ref, pe_ref)
        sh_ref[0:nsb, :] = pr1
        h = pr0 + sh_ref[pl.ds(1, nsb), :]
        out = _dot(jax.nn.gelu(h).astype(BF16), w2_ref[kv])
        if kv == 0:
            for hh in range(N_KV_HEADS):
                kc_ref[0, hh] = out[:, hh * HEAD_DIM:(hh + 1) * HEAD_DIM].astype(BF16)
        else:
            vcT_ref[0] = out.T.astype(BF16)


def _compress_prompt(kvc3, wbd, pe_t, w2bd):
    b, t, _ = kvc3.shape
    nsb = t // CMP_STRIDE
    return pl.pallas_call(
        functools.partial(_compress_prompt_kernel, nsb=nsb),
        grid=(b,),
        in_specs=[pl.BlockSpec((1, t, LANES), lambda i, c=c: (i, 0, c)) for c in range(_KV_CHUNKS)]
        + [_full(wbd.shape), _full(pe_t.shape), _full(w2bd.shape)],
        out_specs=[pl.BlockSpec((1, N_KV_HEADS, nsb, HEAD_DIM), lambda i: (i, 0, 0, 0)),
                   pl.BlockSpec((1, KV_W, nsb), lambda i: (i, 0, 0))],
        out_shape=[jax.ShapeDtypeStruct((b, N_KV_HEADS, nsb, HEAD_DIM), BF16),
                   jax.ShapeDtypeStruct((b, KV_W, nsb), BF16)],
        scratch_shapes=[pltpu.VMEM((nsb + SUBLANES, KV_W), F32)],
        compiler_params=_params("parallel"),
        name="nsa_compress_prompt",
    )(*([kvc3] * _KV_CHUNKS), wbd, pe_t, w2bd)


def _topk_mask_T(imp, n_top):
    ns, w = imp.shape
    nblk = ns // SUBLANES
    blocks = [imp[r * SUBLANES:(r + 1) * SUBLANES, :] for r in range(nblk)]
    cnts = [jnp.zeros((SUBLANES, w), F32) for _ in range(nblk)]
    sub = lax.broadcasted_iota(jnp.int32, (SUBLANES, w), 0)
    for sp in range(ns):
        row = blocks[sp // SUBLANES][sp % SUBLANES:sp % SUBLANES + 1, :]
        for r in range(nblk):
            blk = blocks[r]
            if sp < r * SUBLANES:
                beats = jnp.where(row >= blk, 1.0, 0.0)
            elif sp >= (r + 1) * SUBLANES:
                beats = jnp.where(row > blk, 1.0, 0.0)
            else:
                beats = jnp.where(sub > (sp - r * SUBLANES), jnp.where(row >= blk, 1.0, 0.0),
                                  jnp.where(row > blk, 1.0, 0.0))
            cnts[r] = cnts[r] + beats
    return jnp.concatenate([jnp.where(c < n_top, 1.0, 0.0) for c in cnts], axis=0)


def _online_chunks(states, k_cs, vT_cs, qTs, bias):
    scores = [_dot(k_c, qT) for k_c, qT in zip(k_cs, qTs)]
    mids = []
    for (m, _), s in zip(states, scores):
        if bias is not None:
            s = s + bias
        m_new = jnp.maximum(m, jnp.max(s, axis=0, keepdims=True))
        mids.append((m_new, jnp.exp2(m - m_new), jnp.exp2(s - m_new).astype(BF16)))
    return tuple((m_new, alpha * acc + _dot(vT_c, p))
                 for (m_new, alpha, p), (_, acc), vT_c in zip(mids, states, vT_cs))


def _softmax_finish(carry):
    _, acc = carry
    return acc[0:HEAD_DIM, :] * (1.0 / jnp.maximum(acc[HEAD_DIM:HEAD_DIM + 1, :], 1e-20))


LAZY_LOG2_MAX = 60.0
N_CONST_LANES = 4


def _split3(x):
    hi = x.astype(BF16).astype(F32)
    r = x - hi
    mid = r.astype(BF16).astype(F32)
    return hi, mid, (r - mid).astype(BF16).astype(F32)


def _rescaling_chunks(m_ref, acc_ref, k_cs, vT_cs, qTs, base_tiles, bias):
    nh = len(k_cs)
    rows, w = base_tiles[0].shape
    zero_rows = jnp.zeros((LANES - HEAD_DIM - rows, w), BF16)
    new = _online_chunks(tuple((m_ref[h], acc_ref[h]) for h in range(nh)), k_cs, vT_cs,
                         [jnp.concatenate([qTs[h], base_tiles[h].astype(BF16), zero_rows], axis=0) for h in range(nh)],
                         bias)
    for h in range(nh):
        m_ref[h] = new[h][0]
        acc_ref[h] = new[h][1]


def _lazy_chunks(m_ref, acc2_ref, slot_ref, k_cs, vT_cs, qTs, base_tiles, bias, ref_row, n_chunks=1, first=False):
    slot = slot_ref[0]
    acc_ref = acc2_ref.at[slot]
    nu = len(k_cs)
    nh = nu // n_chunks
    biases = list(bias) if isinstance(bias, (list, tuple)) else [bias] * n_chunks
    rows = base_tiles[0].shape[0]
    w = base_tiles[0].shape[1]
    rowt = lax.broadcasted_iota(jnp.int32, (rows, w), 0)
    zero_rows = jnp.zeros((LANES - HEAD_DIM - rows, w), BF16)

    def queries(u):
        hi, mid, lo = _split3(-m_ref[u % nh])
        tile = jnp.where(rowt == ref_row, hi, jnp.where(rowt == ref_row + 1, mid,
                                                        jnp.where(rowt == ref_row + 2, lo, base_tiles[u])))
        return jnp.concatenate([qTs[u % nh], tile.astype(BF16), zero_rows], axis=0)

    scores = [_dot(k_cs[u], queries(u)) for u in range(nu)]
    peaks = [None] * nh
    probs = []
    for u, s in enumerate(scores):
        if biases[u // nh] is not None:
            s = s + biases[u // nh]
        top = jnp.max(s, axis=0, keepdims=True)
        peaks[u % nh] = top if peaks[u % nh] is None else jnp.maximum(peaks[u % nh], top)
        probs.append(jnp.exp2(s).astype(BF16))
    pvs = [_dot(vT_cs[u], probs[u]) for u in range(nu)]
    for h in range(nh):
        total = pvs[h]
        for c in range(1, n_chunks):
            total = total + pvs[c * nh + h]
        acc2_ref[1 - slot, h] = acc_ref[h] + total
    highest = functools.reduce(jnp.maximum, peaks)
    lowest = functools.reduce(jnp.minimum, peaks)
    in_range = (jnp.max(highest) <= LAZY_LOG2_MAX) & (jnp.logical_not(first) | (jnp.min(lowest) >= -LAZY_LOG2_MAX))

    @pl.when(in_range)
    def _():
        slot_ref[0] = 1 - slot

    @pl.when(jnp.logical_not(in_range))
    def _():
        for h in range(nh):
            m_ref[h] = jnp.where(first, NEG, m_ref[h])
        for c in range(n_chunks):
            part = slice(c * nh, (c + 1) * nh)
            _rescaling_chunks(m_ref, acc_ref, k_cs[part], vT_cs[part], qTs, base_tiles[part], biases[c])


def _attn_prompt_kernel(qT_ref, qrT_ref, gT_ref, kc_ref, vcT_ref, ks_ref, vsT_ref, kw_ref, vwT_ref, ovT_ref,
                        o_ref, selb_ref, oc_ref, ow_ref, m_ref, acc2_ref, slot_ref, *, tq, nc, n_top):
    ck = tq
    i = pl.program_id(1)
    t0 = i * tq
    qpos = t0 + lax.broadcasted_iota(jnp.int32, (1, tq), 1)
    ncp = kc_ref.shape[2]
    ns = ovT_ref.shape[0]
    w = GROUP * tq
    bpc = ck // SEL_BLOCK
    sel_shift = SEL_BLOCK.bit_length() - 1
    kvhs = range(N_KV_HEADS)
    heads = [[kvh * GROUP + g for g in range(GROUP)] for kvh in kvhs]
    rows = [pl.ds(kvh * HEAD_DIM, HEAD_DIM) for kvh in kvhs]
    vrows = [pl.ds(kvh * V_ROWS, V_ROWS) for kvh in kvhs]
    qrT = [jnp.concatenate([qrT_ref[h * HEAD_DIM:(h + 1) * HEAD_DIM, :] for h in heads[kvh]], axis=1)
           for kvh in kvhs]
    kl = lax.broadcasted_iota(jnp.int32, (ck, tq), 0)
    ql = lax.broadcasted_iota(jnp.int32, (ck, tq), 1)
    tile4 = lambda b: jnp.concatenate([b] * GROUP, axis=1)
    key_le_query = tile4(jnp.where(kl <= ql, 0.0, MASKED))
    key_ge_query = tile4(jnp.where(kl >= ql, 0.0, MASKED))
    bias_rows = selb_ref.shape[2]

    def reset_softmax():
        slot_ref[0] = 0
        for kvh in kvhs:
            m_ref[kvh] = jnp.zeros((1, w), F32)
            acc2_ref[0, kvh] = jnp.zeros((V_ROWS, w), F32)

    def current_acc(kvh):
        return acc2_ref[slot_ref[0], kvh]

    cidx = lax.broadcasted_iota(jnp.int32, (ncp, 1), 0)
    valid = ((cidx * CMP_STRIDE + (CMP_BLOCK - 1)) <= qpos) & (cidx < nc)
    sidx = lax.broadcasted_iota(jnp.int32, (ns, 1), 0)
    cur = lax.shift_right_logical(qpos, sel_shift)
    forced = (sidx == 0) | (sidx == cur) | (sidx == cur - 1)
    causal = (sidx * SEL_BLOCK) <= qpos
    cmp_scores = [
        _dot(kc_ref[0, kvh], jnp.concatenate([qT_ref[h * HEAD_DIM:(h + 1) * HEAD_DIM, :] for h in heads[kvh]], axis=1))
        for kvh in kvhs]
    importance = []
    cmp_bias = jnp.where(valid, 0.0, MASKED)
    seen = jnp.where(jnp.max(cmp_bias, axis=0, keepdims=True) == 0.0, 1.0, 0.0)
    for kvh in kvhs:
        s = cmp_scores[kvh]
        probs = []
        for g in range(GROUP):
            sm = s[:, g * tq:(g + 1) * tq] + cmp_bias
            e = jnp.exp2(sm - jnp.max(sm, axis=0, keepdims=True))
            den = jnp.maximum(jnp.sum(e, axis=0, keepdims=True), 1e-20)
            probs.append(e * (seen / den))
        oc_ref[kvh] = _dot(vcT_ref[0, rows[kvh], :], jnp.concatenate(probs, axis=1).astype(BF16))
        psum = probs[0]
        for g in range(1, GROUP):
            psum = psum + probs[g]
        imp = _dot_f32lhs(ovT_ref[...], psum)
        importance.append(jnp.where(forced, FORCE, jnp.where(causal, imp, NEG)))

    prefixes = sorted({r for r in (ns // 4, ns // 2, 3 * ns // 4) if r and r % SUBLANES == 0} | {ns})
    needed = (i + 1) * bpc
    fill = jnp.zeros((bias_rows - bpc, tq), F32)
    for lo, hi in zip([0] + prefixes[:-1], prefixes):
        @pl.when((needed > lo) & (needed <= hi))
        def _(hi=hi):
            for kvh in kvhs:
                selb = (_topk_mask_T(importance[kvh][0:hi, :], min(n_top, hi)) - 1.0) * (-MASKED)
                for c in range(hi // bpc):
                    selb_ref[kvh, c] = jnp.concatenate([selb[c * bpc:(c + 1) * bpc, :], fill], axis=0)

    n_back = WINDOW // ck
    reset_softmax()
    rowb = lax.broadcasted_iota(jnp.int32, (bias_rows, w), 0)
    def window_chunk(r):
        a = i - n_back + r
        kst = pl.multiple_of(jnp.maximum(a, 0) * ck, ck)
        skip = jnp.where(a < 0, MASKED, 0.0)
        tile = jnp.where(rowb == bpc, skip, 0.0)
        bias = key_le_query if r == n_back else (key_ge_query if r == 0 else None)
        return ([kw_ref[kvh, pl.ds(kst, ck), :] for kvh in kvhs],
                [vwT_ref[vrows[kvh], pl.ds(kst, ck)] for kvh in kvhs], [tile] * N_KV_HEADS, bias)

    order = [n_back] + list(range(n_back))
    for g0 in [0] + list(range(1, n_back + 1, 2)):
        group = [window_chunk(r) for r in (order[g0:g0 + 1] if g0 == 0 else order[g0:g0 + 2])]
        _lazy_chunks(m_ref, acc2_ref, slot_ref, sum((g[0] for g in group), []), sum((g[1] for g in group), []),
                     qrT, sum((g[2] for g in group), []), [g[3] for g in group], bpc + 1, len(group), first=g0 == 0)
    for kvh in kvhs:
        ow_ref[kvh] = _softmax_finish((None, current_acc(kvh)))

    def chunk_step(c, diagonal, n_chunks=1, first=False):
        ks, vs, tiles = [], [], []
        for dc in range(n_chunks):
            kst = (c + dc) * ck if isinstance(c, int) else pl.multiple_of((c + dc) * ck, ck)
            ks += [ks_ref[kvh, pl.ds(kst, ck), :] for kvh in kvhs]
            vs += [vsT_ref[vrows[kvh], pl.ds(kst, ck)] for kvh in kvhs]
            tiles += [tile4(selb_ref[kvh, c + dc]) for kvh in kvhs]
        _lazy_chunks(m_ref, acc2_ref, slot_ref, ks, vs, qrT, tiles, key_le_query if diagonal else None, bpc + 1,
                     n_chunks, first)

    reset_softmax()

    @pl.when(i > 0)
    def _():
        chunk_step(0, False, first=True)

    n_mid = jnp.maximum(i - 1, 0)

    def loop_body(p, carry):
        chunk_step(1 + 2 * p, False, n_chunks=2)
        return carry

    lax.fori_loop(0, n_mid // 2, loop_body, 0)

    @pl.when(n_mid % 2 == 1)
    def _():
        chunk_step(i - 1, False)

    chunk_step(i, True, first=i == 0)

    for kvh in kvhs:
        def gate_row(j, kvh=kvh):
            return jnp.concatenate([gT_ref[h * 3 + j:h * 3 + j + 1, :] for h in heads[kvh]], axis=1)

        oT = (gate_row(0) * oc_ref[kvh] + gate_row(1) * _softmax_finish((None, current_acc(kvh)))
              + gate_row(2) * ow_ref[kvh])
        for g, h in enumerate(heads[kvh]):
            o_ref[h * HEAD_DIM:(h + 1) * HEAD_DIM, :] = oT[:, g * tq:(g + 1) * tq].astype(BF16)


def _attn_prompt(qT, qrT, gT, kc, vcT, ksb, vsT, kwb, vwT, ovT, *, batch, seq, tq, nc, n_top):
    nq = seq // tq
    nsb = kc.shape[2]
    ns = ovT.shape[0]
    col = lambda h: pl.BlockSpec((h, tq), lambda b, i: (0, b * nq + i))
    kh = pl.BlockSpec((N_KV_HEADS, seq, LANES), lambda b, i: (0, b, 0))
    vt = pl.BlockSpec((N_KV_HEADS * V_ROWS, seq), lambda b, i: (0, b))
    bf16_sublanes = 2 * SUBLANES
    return pl.pallas_call(
        functools.partial(_attn_prompt_kernel, tq=tq, nc=nc, n_top=n_top),
        grid=(batch, nq),
        in_specs=[col(Q_W), col(Q_W), col(LANES),
                  pl.BlockSpec((1, N_KV_HEADS, nsb, HEAD_DIM), lambda b, i: (b, 0, 0, 0)),
                  pl.BlockSpec((1, KV_W, nsb), lambda b, i: (b, 0, 0)),
                  kh, vt, kh, vt, _full(ovT.shape)],
        out_specs=col(Q_W),
        out_shape=jax.ShapeDtypeStruct((Q_W, batch * seq), BF16),
        scratch_shapes=[pltpu.VMEM((N_KV_HEADS, ns * SEL_BLOCK // tq, bf16_sublanes, tq), F32),
                        pltpu.VMEM((N_KV_HEADS, HEAD_DIM, GROUP * tq), F32),
                        pltpu.VMEM((N_KV_HEADS, HEAD_DIM, GROUP * tq), F32),
                        pltpu.VMEM((N_KV_HEADS, 1, GROUP * tq), F32),
                        pltpu.VMEM((2, N_KV_HEADS, V_ROWS, GROUP * tq), F32),
                        pltpu.SMEM((1,), jnp.int32)],
        compiler_params=_params("parallel", "arbitrary"),
        name="nsa_attn_prompt",
    )(qT, qrT, gT, kc, vcT, ksb, vsT, kwb, vwT, ovT)


def _resident(shape):
    zeros = (0,) * len(shape)
    return pl.BlockSpec(shape, lambda *_: zeros, pipeline_mode=pl.Buffered(1))


def _layer_tail_kernel(*refs, mixer, transposed, ff_chunk, next_norm):
    if mixer == "nsa":
        o_ref, wo_ref, x_ref, g1_ref, g2_ref = refs[:5]
        rest = refs[5:]
        if transposed:
            y = lax.dot_general(o_ref[...], wo_ref[...], (((0,), (0,)), ((), ())), preferred_element_type=F32)
        else:
            y = _dot(o_ref[...], wo_ref[...])
    else:
        yc_ref, u_ref, d_ref, wg_ref, bg_ref, x_ref, g1_ref, g2_ref = refs[:8]
        rest = refs[8:]
        z = jax.nn.gelu(yc_ref[...] + d_ref[...] * u_ref[...])
        y = z * jax.nn.sigmoid(_dot(z.astype(BF16), wg_ref[...]) + bg_ref[...])
    w1_ref, w2_ref, g3_ref = rest[:3]
    if next_norm:
        gn_ref, h2_ref, xn_ref, acc_ref = rest[3:]
    else:
        h2_ref, acc_ref = rest[3:]
    h = x_ref[...] + _rms(y, g1_ref[...])
    xm = _rms(h, g2_ref[...]).astype(BF16)
    for c in range(w1_ref.shape[1] // ff_chunk):
        cols = slice(c * ff_chunk, (c + 1) * ff_chunk)
        hm = jnp.maximum(_dot(xm, w1_ref[:, cols]), 0.0)
        part = _dot((hm * hm).astype(BF16), w2_ref[cols, :])
        if c == 0:
            acc_ref[...] = part
        else:
            acc_ref[...] += part
    h2 = h + _rms(acc_ref[...], g3_ref[...])
    h2_ref[...] = h2
    if next_norm:
        xn_ref[...] = _rms(h2, gn_ref[...])


def _layer_tail(mixer_args, x, g1, g2, w1, w2, g3, gn, *, mixer, tm, ff_chunk, transposed=False):
    rows, d = x.shape
    row = pl.BlockSpec((tm, d), lambda i: (i, 0))
    vec = _resident((1, d))
    if mixer == "nsa":
        o, w_o = mixer_args
        o_spec = (pl.BlockSpec((Q_W, tm), lambda i: (0, i)) if transposed
                  else pl.BlockSpec((tm, Q_W), lambda i: (i, 0)))
        head_specs = [o_spec, _resident(w_o.shape)]
    else:
        w_glu = mixer_args[3]
        head_specs = [row, row, vec, _resident(w_glu.shape), vec]
    tail_args = (w1, w2, g3) + (() if gn is None else (gn,))
    tail_specs = [_resident(w1.shape), _resident(w2.shape), vec] + ([] if gn is None else [vec])
    n_out = 1 if gn is None else 2
    return pl.pallas_call(
        functools.partial(_layer_tail_kernel, mixer=mixer, transposed=transposed, ff_chunk=ff_chunk,
                          next_norm=gn is not None),
        grid=(rows // tm,),
        in_specs=head_specs + [row, vec, vec] + tail_specs,
        out_specs=[row] * n_out,
        out_shape=[jax.ShapeDtypeStruct((rows, d), F32)] * n_out,
        scratch_shapes=[pltpu.VMEM((tm, d), F32)],
        compiler_params=_params("parallel"),
        name=mixer + "_layer_tail",
    )(*mixer_args, x, g1, g2, *tail_args)


def _s5_seq_kernel(x_ref, brow_ref, pre_ref, pim_ref, qre_ref, qim_ref, are_ref, aim_ref, y_ref, hfin_ref,
                   sre, sim, hre, him, cre, cim):
    nb, tb, uw = x_ref.shape
    nt = pre_ref.shape[1] // uw
    nk = tb // nt

    @pl.when(pl.program_id(1) == 0)
    def _():
        cre[...] = jnp.zeros(cre.shape, F32)
        cim[...] = jnp.zeros(cim.shape, F32)

    u = [jnp.concatenate([x_ref[b, pl.ds(t, nk, stride=nt), :] for b in range(nb)], axis=0).astype(BF16)
         for t in range(nt)]
    u2 = [jnp.concatenate([u[2 * j], u[2 * j + 1]], axis=1) for j in range(nt // 2)]
    s_re = _dot(u2[0], pre_ref[0, 0:2 * uw, :])
    s_im = _dot(u2[0], pim_ref[0, 0:2 * uw, :])
    for j in range(1, nt // 2):
        s_re = s_re + _dot(u2[j], pre_ref[0, 2 * j * uw:(2 * j + 2) * uw, :])
        s_im = s_im + _dot(u2[j], pim_ref[0, 2 * j * uw:(2 * j + 2) * uw, :])
    sre[...] = s_re
    sim[...] = s_im
    ar = are_ref[0]
    ai = aim_ref[0]

    within = []
    for t2 in range(0, nt, 2):
        acc = None
        for j in range(t2 // 2 + 1):
            lag0 = t2 - 2 * j + 1
            wpair = jnp.concatenate([brow_ref[0, :, lag0 * uw:(lag0 + 2) * uw],
                                     brow_ref[0, :, (lag0 - 1) * uw:(lag0 + 1) * uw]], axis=0)
            part = _dot(u2[j], wpair)
            acc = part if acc is None else acc + part
        within.append(acc)

    def body(it, carry):
        out = []
        for b in range(nb):
            hr, hi = carry[b]
            r0 = b * nk + it * SUBLANES
            sr8 = sre[pl.ds(r0, SUBLANES), :]
            si8 = sim[pl.ds(r0, SUBLANES), :]
            prev_r, prev_i = [], []
            for j in range(SUBLANES):
                prev_r.append(hr)
                prev_i.append(hi)
                hr, hi = (ar * hr - ai * hi + sr8[j:j + 1, :], ar * hi + ai * hr + si8[j:j + 1, :])
            hre[pl.ds(r0, SUBLANES), :] = jnp.concatenate(prev_r, axis=0)
            him[pl.ds(r0, SUBLANES), :] = jnp.concatenate(prev_i, axis=0)
            out.append((hr, hi))
        return tuple(out)

    fin = tuple((cre[b:b + 1, :], cim[b:b + 1, :]) for b in range(nb))
    for it in range(nk // SUBLANES):
        fin = body(it, fin)
    for b in range(nb):
        cre[b:b + 1, :] = fin[b][0]
        cim[b:b + 1, :] = fin[b][1]
    hfin_ref[0, 0] = cre[...]
    hfin_ref[0, 1] = cim[...]

    for b in range(nb):
        seq = slice(b * nk, (b + 1) * nk)
        hb_re = hre[seq, :].astype(BF16)
        hb_im = him[seq, :].astype(BF16)
        for t2 in range(0, nt, 2):
            cols = slice(t2 * uw, (t2 + 2) * uw)
            acc = within[t2 // 2][seq, :] + _dot(hb_re, qre_ref[0, :, cols]) + _dot(hb_im, qim_ref[0, :, cols])
            for dt in range(2):
                y_ref[b, pl.ds(t2 + dt, nk, stride=nt), :] = acc[:, dt * uw:(dt + 1) * uw]


def _s5_seq(x3, ops):
    nb, t, d = x3.shape
    uw = SSM_UNIT * SSM_GROUP
    nunit = d // uw
    sw = ops["p_re"].shape[2]
    tb = _tile(t, S5_TIME_BLOCK)
    nk = tb // SSM_CHUNK
    per_unit = lambda a: pl.BlockSpec((1,) + a.shape[1:], lambda i, r: (i,) + (0,) * (a.ndim - 1))
    blk = pl.BlockSpec((nb, tb, uw), lambda i, r: (0, r, i))
    args = (x3, ops["brow"], ops["p_re"], ops["p_im"], ops["q_re"], ops["q_im"], ops["a_chunk_re"], ops["a_chunk_im"])
    return pl.pallas_call(
        _s5_seq_kernel,
        grid=(nunit, t // tb),
        in_specs=[blk] + [per_unit(a) for a in args[1:]],
        out_specs=[blk, pl.BlockSpec((1, 2, nb, sw), lambda i, r: (i, 0, 0, 0))],
        out_shape=[jax.ShapeDtypeStruct((nb, t, d), F32), jax.ShapeDtypeStruct((nunit, 2, nb, sw), F32)],
        scratch_shapes=[pltpu.VMEM((nb * nk, sw), F32)] * 4 + [pltpu.VMEM((nb, sw), F32)] * 2,
        compiler_params=_params("parallel", "arbitrary"),
        name="s5_seq_scan",
    )(*args)


def _s5_step_kernel(u_ref, h0r_ref, h0i_ref, bre_ref, bim_ref, cre_ref, cim_ref, are_ref, aim_ref,
                    y_ref, hr_ref, hi_ref, *, npair):
    for p in range(npair):
        u = u_ref[p]
        ar = are_ref[p]
        ai = aim_ref[p]
        h0r = h0r_ref[p]
        h0i = h0i_ref[p]
        hr = ar * h0r - ai * h0i + _dot(u, bre_ref[p])
        hi = ar * h0i + ai * h0r + _dot(u, bim_ref[p])
        hr_ref[p] = hr
        hi_ref[p] = hi
        y_ref[p] = _dot(hr.astype(BF16), cre_ref[p]) + _dot(hi.astype(BF16), cim_ref[p])


def _s5_step(u2, h0r, h0i, ops):
    npair, rows, width = u2.shape
    args = (u2, h0r, h0i, ops["b1_re"], ops["b1_im"], ops["c1_re"], ops["c1_im"], ops["a1_re"], ops["a1_im"])
    return pl.pallas_call(
        functools.partial(_s5_step_kernel, npair=npair),
        grid=(1,),
        in_specs=[_full(a.shape) for a in args],
        out_specs=[_full((npair, rows, width)), _full(h0r.shape), _full(h0r.shape)],
        out_shape=[jax.ShapeDtypeStruct((npair, rows, width), F32), jax.ShapeDtypeStruct(h0r.shape, F32),
                   jax.ShapeDtypeStruct(h0r.shape, F32)],
        compiler_params=_params("arbitrary"),
        name="s5_single_step",
    )(*args)


def _s5_operators(a_re, a_im, log_dt, b_re, b_im, c_re, c_im):
    hp = lax.Precision.HIGHEST
    g, n = a_re.shape
    gu = SSM_UNIT
    nunit = g // gu
    L = SSM_CHUNK
    uw = gu * SSM_GROUP
    sw = gu * n
    a = lax.complex(a_re.astype(F32), a_im.astype(F32))
    dt = jnp.exp(log_dt.astype(F32))[:, None]
    a_bar = jnp.exp(a * dt)
    b_bar = ((a_bar - 1.0) / a)[:, :, None] * lax.complex(b_re.astype(F32), b_im.astype(F32))
    c = lax.complex(c_re.astype(F32), c_im.astype(F32))
    pows = [jnp.ones_like(a_bar)]
    for _ in range(L):
        pows.append(pows[-1] * a_bar)
    a_pow = jnp.stack(pows).reshape(L + 1, nunit, sw)
    apr = jnp.real(a_pow)
    api = jnp.imag(a_pow)
    eye = jnp.eye(gu, dtype=F32)

    def bd_in(x):
        return jnp.einsum("pgnd,gh->pgdhn", x.reshape(nunit, gu, n, SSM_GROUP), eye).reshape(nunit, uw, sw)

    def bd_out(x):
        return jnp.einsum("pgcn,gh->pgnhc", x.reshape(nunit, gu, SSM_GROUP, n), eye).reshape(nunit, sw, uw)

    p0r, p0i = bd_in(jnp.real(b_bar)), bd_in(jnp.imag(b_bar))
    q0r, q0i = bd_out(jnp.real(c)), bd_out(jnp.imag(c))
    lanes = lambda x, k: x[k][:, None, :]
    rows_ = lambda x, k: x[k][:, :, None]
    p_re = jnp.concatenate([p0r * lanes(apr, L - 1 - t) - p0i * lanes(api, L - 1 - t) for t in range(L)], axis=1)
    p_im = jnp.concatenate([p0r * lanes(api, L - 1 - t) + p0i * lanes(apr, L - 1 - t) for t in range(L)], axis=1)
    q_re = jnp.concatenate([q0r * rows_(apr, t + 1) - q0i * rows_(api, t + 1) for t in range(L)], axis=2)
    q_im = jnp.concatenate([-(q0r * rows_(api, t + 1) + q0i * rows_(apr, t + 1)) for t in range(L)], axis=2)
    lag_blocks = [jnp.zeros((nunit, uw, uw), F32)]
    for t in range(L):
        xr = p0r * lanes(apr, t) - p0i * lanes(api, t)
        xi = p0r * lanes(api, t) + p0i * lanes(apr, t)
        lag_blocks.append(jnp.einsum("pus,psv->puv", xr, q0r, precision=hp)
                          - jnp.einsum("pus,psv->puv", xi, q0i, precision=hp))
    brow = jnp.concatenate(lag_blocks, axis=2)
    lane_row = lambda x: x.reshape(nunit, 1, sw)
    return {
        "brow": brow.astype(BF16),
        "p_re": p_re.astype(BF16), "p_im": p_im.astype(BF16),
        "q_re": q_re.astype(BF16), "q_im": q_im.astype(BF16),
        "a_chunk_re": lane_row(apr[L]), "a_chunk_im": lane_row(api[L]),
        "a1_re": lane_row(apr[1]), "a1_im": lane_row(api[1]),
        "b1_re": p0r.astype(BF16), "b1_im": p0i.astype(BF16),
        "c1_re": q0r.astype(BF16), "c1_im": (-q0i).astype(BF16),
    }


def _compress_paged_kernel(pt_ref, *refs, pg, ns, nsb):
    del pt_ref
    pages = refs[:ns * pg]
    perm_ref, wbd_ref, pe_ref, w2_ref, out_ref, h0_ref, h1_ref = refs[ns * pg:]
    j = pl.program_id(1)
    sbp = PAGE_SIZE // CMP_STRIDE
    nrows = pg * sbp
    pair_rows = 2 * sbp

    @pl.when(j == 0)
    def _():
        h1_ref[:, :, nsb:nsb + SUBLANES, :] = jnp.zeros((ns, 2, SUBLANES, KV_W), F32)

    r0 = pl.multiple_of(j * nrows, nrows)
    for kv in range(2):
        staged = []
        for q in range(ns * pg // 2):
            z = jnp.concatenate([pages[2 * q][0, kv], pages[2 * q + 1][0, kv]], axis=1).astype(BF16)
            staged.append(_dot_nt(perm_ref[...], z).astype(BF16))

        def load_rows(s, kv, staged=staged):
            return jnp.concatenate([x[s * pair_rows:(s + 1) * pair_rows, :] for x in staged], axis=0)

        pr0, pr1 = _compress_half(load_rows, kv, ns * nrows, wbd_ref, pe_ref)
        for u in range(ns):
            h0_ref[u, kv, pl.ds(r0, nrows), :] = pr0[u * nrows:(u + 1) * nrows, :]
            h1_ref[u, kv, pl.ds(r0, nrows), :] = pr1[u * nrows:(u + 1) * nrows, :]

    @pl.when(j == pl.num_programs(1) - 1)
    def _():
        for u in range(ns):
            for kv in range(2):
                h = h0_ref[u, kv] + h1_ref[u, kv, pl.ds(1, nsb), :]
                out = _dot(jax.nn.gelu(h).astype(BF16), w2_ref[kv])
                out_ref[u, :, kv * KV_W:(kv + 1) * KV_W] = out.astype(BF16)


def _page_specs(pg, ns, pages_per_sample):
    def spec(u, i):
        return pl.BlockSpec((1, 2, KV_W, PAGE_SIZE),
                            lambda b, j, pt: (pt[(b * ns + u) * pages_per_sample + j * pg + i], 0, 0, 0))
    return [spec(u, i) for u in range(ns) for i in range(pg)]


def _compress_paged(pages, pt_flat, wbd, pe_t, w2bd, *, nb, pages_per_sample, pg):
    nsb = pages_per_sample * PAGE_SIZE // CMP_STRIDE
    sbp = PAGE_SIZE // CMP_STRIDE
    ns = SAMPLES_PER_STEP if nb % SAMPLES_PER_STEP == 0 else 1
    i_out = np.arange(2 * PAGE_SIZE)
    s_i, pg_i, n_i = i_out // (2 * sbp), (i_out // sbp) % 2, i_out % sbp
    perm = jnp.asarray(i_out[None, :] == (pg_i * PAGE_SIZE + n_i * CMP_STRIDE + s_i)[:, None], BF16)
    c3 = lambda shape: pl.BlockSpec(shape, lambda b, j, pt: (0,) * len(shape))
    grid_spec = pltpu.PrefetchScalarGridSpec(
        num_scalar_prefetch=1,
        grid=(nb // ns, pages_per_sample // pg),
        in_specs=_page_specs(pg, ns, pages_per_sample) + [c3(perm.shape), c3(wbd.shape), c3(pe_t.shape),
                                                          c3(w2bd.shape)],
        out_specs=pl.BlockSpec((ns, nsb, 2 * KV_W), lambda b, j, pt: (b, 0, 0)),
        scratch_shapes=[pltpu.VMEM((ns, 2, nsb, KV_W), F32), pltpu.VMEM((ns, 2, nsb + SUBLANES, KV_W), F32)],
    )
    return pl.pallas_call(
        functools.partial(_compress_paged_kernel, pg=pg, ns=ns, nsb=nsb),
        grid_spec=grid_spec,
        out_shape=jax.ShapeDtypeStruct((nb, nsb, 2 * KV_W), BF16),
        compiler_params=_params("parallel", "arbitrary"),
        name="nsa_compress_paged",
    )(pt_flat, *([pages] * (ns * pg)), perm, wbd, pe_t, w2bd)


def _topk_mask_lanes(imp, n_top, ns_valid):
    lane = lax.broadcasted_iota(jnp.int32, imp.shape, 1)
    cnt = jnp.zeros(imp.shape, F32)
    for sp in range(ns_valid):
        col = imp[:, sp:sp + 1]
        cnt = cnt + jnp.where(lane > sp, jnp.where(col >= imp, 1.0, 0.0), jnp.where(col > imp, 1.0, 0.0))
    return jnp.where((cnt < n_top) & (lane < ns_valid), 1.0, 0.0)


def _dot_nt(a, b):
    return lax.dot_general(a, b, (((1,), (1,)), ((), ())), preferred_element_type=F32)


def _attn_sample_kernel(pt_ref, *refs, pg, ns, past, nc, ns_valid, n_top):
    del pt_ref
    all_pages = [refs[u * pg:(u + 1) * pg] for u in range(ns)]
    (q_ref, qr_ref, g_ref, kcvc_ref, ksn_ref, kwn_ref, win_ref, ov_ref, e_ref, gs_ref, o_ref,
     m_ref, l_ref, acc_ref, sel_ref, oc_ref, ow_ref) = refs[ns * pg:]
    j = pl.program_id(1)
    ncp = kcvc_ref.shape[1]
    nsp = ov_ref.shape[1]
    wlen = win_ref.shape[3]
    row = lax.broadcasted_iota(jnp.int32, (N_HEADS, KV_W), 0)
    lane = lax.broadcasted_iota(jnp.int32, (N_HEADS, KV_W), 1)
    own = (lane // HEAD_DIM) == (row // GROUP)

    def spread(ref, u):
        q = ref[u]
        return jnp.where(own, jnp.concatenate([q] * N_KV_HEADS, axis=1), jnp.zeros((N_HEADS, KV_W), BF16))

    def update(state, s, vs):
        m, l, acc = state
        m_new = jnp.maximum(m, jnp.max(s, axis=1, keepdims=True))
        alpha = jnp.exp(m - m_new)
        p = jnp.exp(s - m_new)
        l = alpha * l + jnp.sum(p, axis=1, keepdims=True)
        pv = None
        for st, sz, v, feature_major in vs:
            pb = p[:, st:st + sz].astype(BF16)
            t = _dot_nt(pb, v) if feature_major else _dot(pb, v)
            pv = t if pv is None else pv + t
        return m_new, l, alpha * acc + pv

    def init():
        return (jnp.full((N_HEADS, 1), NEG, F32), jnp.zeros((N_HEADS, 1), F32), jnp.zeros((N_HEADS, KV_W), F32))

    def new_row_update(state, qbd, new_row):
        r8 = lax.broadcasted_iota(jnp.int32, (SUBLANES, 2 * KV_W), 0)
        tile = jnp.where(r8 == 0, jnp.broadcast_to(new_row, (SUBLANES, 2 * KV_W)), 0.0).astype(BF16)
        s = _dot_nt(qbd, tile[:, 0:KV_W])
        l8 = lax.broadcasted_iota(jnp.int32, (N_HEADS, SUBLANES), 1)
        s = jnp.where(l8 == 0, s, MASKED)
        return update(state, s, [(0, SUBLANES, tile[:, KV_W:2 * KV_W], False)])

    def put(u, state):
        m, l, acc = state
        m_ref[u] = jnp.broadcast_to(m, m_ref.shape[1:])
        l_ref[u] = jnp.broadcast_to(l, l_ref.shape[1:])
        acc_ref[u] = acc

    qrbd = [spread(qr_ref, u) for u in range(ns)]

    @pl.when(j == 0)
    def _():
        cidx = lax.broadcasted_iota(jnp.int32, (1, ncp), 1)
        valid = ((cidx * CMP_STRIDE + (CMP_BLOCK - 1)) <= past) & (cidx < nc)
        sidx = lax.broadcasted_iota(jnp.int32, (1, nsp), 1)
        cur = past // SEL_BLOCK
        forced = (sidx == 0) | (sidx == cur) | (sidx == cur - 1)
        causal = (sidx * SEL_BLOCK) <= past
        wpos = past - wlen + lax.broadcasted_iota(jnp.int32, (1, wlen), 1)
        in_window = (wpos >= 0) & (past - wpos <= WINDOW)
        for u in range(ns):
            s = _dot_nt(spread(q_ref, u), kcvc_ref[u, :, 0:KV_W])
            sm = jnp.where(valid, s, NEG)
            mx = jnp.max(sm, axis=1, keepdims=True)
            e = jnp.where(valid, jnp.exp(sm - mx), 0.0)
            p = e / jnp.maximum(jnp.sum(e, axis=1, keepdims=True), 1e-20)
            oc_ref[u] = _dot(p.astype(BF16), kcvc_ref[u, :, KV_W:2 * KV_W])
            imp = _dot_f32rhs(_dot_f32lhs(gs_ref[...], p), ov_ref[...])
            imp = jnp.where(forced, FORCE, jnp.where(causal, imp, NEG))
            imp = jnp.where(sidx < ns_valid, imp, MASKED)
            sel_ref[u] = _topk_mask_lanes(imp, n_top, ns_valid)
            sw = jnp.where(in_window, _dot(qrbd[u], win_ref[u, 0].astype(BF16)), MASKED)
            st = update(init(), sw, [(0, wlen, win_ref[u, 1].astype(BF16), True)])
            st = new_row_update(st, qrbd[u], kwn_ref[u])
            ow_ref[u] = st[2] * (1.0 / jnp.maximum(st[1], 1e-20))
            put(u, new_row_update(init(), qrbd[u], ksn_ref[u]))

    def halves(pages):
        return [pages[:pg // 2], pages[pg // 2:]] if pg > 1 else [pages]

    scores = [[jnp.concatenate([_dot(qrbd[u], r[0, 0].astype(BF16)) for r in part], axis=1)
               for part in halves(all_pages[u])] for u in range(ns)]
    masks = [_dot(sel_ref[u].astype(BF16), e_ref[0]) > 0.5 for u in range(ns)]
    states = [(m_ref[u, :, 0:1], l_ref[u, :, 0:1], acc_ref[u]) for u in range(ns)]
    k0 = 0
    for hx, part0 in enumerate(halves(all_pages[0])):
        width = len(part0) * PAGE_SIZE
        for u in range(ns):
            part = halves(all_pages[u])[hx]
            s = jnp.where(masks[u][:, k0:k0 + width], scores[u][hx], MASKED)
            vs = [(i * PAGE_SIZE, PAGE_SIZE, r[0, 1].astype(BF16), True) for i, r in enumerate(part)]
            states[u] = update(states[u], s, vs)
        k0 += width
    for u in range(ns):
        put(u, states[u])

    @pl.when(j == pl.num_programs(1) - 1)
    def _():
        for u in range(ns):
            g = g_ref[u]
            os_ = acc_ref[u] * (1.0 / jnp.maximum(l_ref[u, :, 0:1], 1e-20))
            o = g[:, 0:1] * oc_ref[u] + g[:, 1:2] * os_ + g[:, 2:3] * ow_ref[u]
            o = jnp.where(own, o, 0.0)
            out = o[:, 0:HEAD_DIM]
            for h in range(1, N_KV_HEADS):
                out = out + o[:, h * HEAD_DIM:(h + 1) * HEAD_DIM]
            o_ref[u] = out.astype(BF16)


def _attn_sample(pages, pt_flat, q3, qr3, g3, kcvc, ks_new, kw_new, win, ov, e_mat, gsum, *, nb, pages_per_sample,
                 pg, past, nc, ns_valid, n_top):
    nsp = ov.shape[1]
    ns = ATTN_SAMPLES_PER_STEP if nb % ATTN_SAMPLES_PER_STEP == 0 else 1
    per_b = lambda shape: pl.BlockSpec((ns,) + shape, lambda b, j, pt: (b,) + (0,) * len(shape))
    const = lambda shape: pl.BlockSpec(shape, lambda b, j, pt: (0,) * len(shape))

    per_sample = lambda *shape: pltpu.VMEM((ns,) + shape, F32)
    grid_spec = pltpu.PrefetchScalarGridSpec(
        num_scalar_prefetch=1,
        grid=(nb // ns, pages_per_sample // pg),
        in_specs=_page_specs(pg, ns, pages_per_sample) + [
            per_b((N_HEADS, HEAD_DIM)), per_b((N_HEADS, HEAD_DIM)), per_b((N_HEADS, 3)),
            per_b(kcvc.shape[1:]), per_b((1, 2 * KV_W)), per_b((1, 2 * KV_W)), per_b(win.shape[1:]),
            const(ov.shape), pl.BlockSpec((1,) + e_mat.shape[1:], lambda b, j, pt: (j, 0, 0)), const(gsum.shape)],
        out_specs=per_b((N_HEADS, HEAD_DIM)),
        scratch_shapes=[per_sample(N_HEADS, LANES), per_sample(N_HEADS, LANES), per_sample(N_HEADS, KV_W),
                        per_sample(N_HEADS, nsp), per_sample(N_HEADS, KV_W), per_sample(N_HEADS, KV_W)],
    )
    return pl.pallas_call(
        functools.partial(_attn_sample_kernel, pg=pg, ns=ns, past=past, nc=nc, ns_valid=ns_valid, n_top=n_top),
        grid_spec=grid_spec,
        out_shape=jax.ShapeDtypeStruct((nb, N_HEADS, HEAD_DIM), BF16),
        compiler_params=_params("parallel", "arbitrary"),
        name="nsa_attn_sample",
    )(pt_flat, *([pages] * (ns * pg)), q3, qr3, g3, kcvc, ks_new, kw_new, win, ov, e_mat, gsum)


def _rope_tables(pos):
    half = HEAD_DIM // 2
    inv = ROPE_THETA ** (-jnp.arange(half, dtype=F32) / half)
    ang = pos.astype(F32)[:, None] * inv[None, :]
    cos = jnp.cos(ang)
    sin = jnp.sin(ang)
    reps = LANES // HEAD_DIM
    return (jnp.tile(jnp.concatenate([cos, cos], axis=1), (1, reps)),
            jnp.tile(jnp.concatenate([-sin, sin], axis=1), (1, reps)))


def _compress_weights(cmp_w1, cmp_w2, cmp_pe):
    ratio = CMP_BLOCK // CMP_STRIDE
    w1r = cmp_w1.reshape(2, ratio, CMP_STRIDE, HEAD_DIM, HEAD_DIM)
    hh = np.arange(KV_W) // HEAD_DIM
    same_head = jnp.asarray(hh[:, None] == hh[None, :])

    def block_diag(w):
        tiled = jnp.concatenate([jnp.concatenate([w] * N_KV_HEADS, axis=-1)] * N_KV_HEADS, axis=-2)
        return jnp.where(same_head, tiled, 0.0).astype(BF16)

    wbd = block_diag(w1r.reshape(2 * ratio * CMP_STRIDE, HEAD_DIM, HEAD_DIM))
    w2bd = block_diag(cmp_w2)
    pe_r = cmp_pe.reshape(2, ratio, CMP_STRIDE, HEAD_DIM).astype(F32)
    pe_w = jnp.einsum("krsd,krsde->kre", pe_r, w1r.astype(F32), precision=lax.Precision.HIGHEST)
    pe_t = jnp.tile(pe_w.reshape(2 * ratio, HEAD_DIM), (1, N_KV_HEADS))
    return wbd, pe_t, w2bd


def _overlap(nc, ncp, nsel, nsp):
    c_start = np.arange(ncp)[:, None] * CMP_STRIDE
    s_start = np.arange(nsp)[None, :] * SEL_BLOCK
    ov = (c_start < s_start + SEL_BLOCK) & (c_start + CMP_BLOCK > s_start)
    ov = ov & (np.arange(ncp)[:, None] < nc) & (np.arange(nsp)[None, :] < nsel)
    return jnp.asarray(ov, BF16)


def _tile(n, pref):
    t = min(n, pref)
    while n % t:
        t //= 2
    return t


def kernel(x_prompt, x_sample, cache_kv_cmp, cache_kv_sel, cache_kv_win, state_ssm, page_table, norm_g, mlp_w1,
           mlp_w2, nsa_w_in, nsa_w_o, nsa_cmp_w1, nsa_cmp_w2, nsa_cmp_pe, s5_a_re, s5_a_im, s5_log_dt, s5_b_re,
           s5_b_im, s5_c_re, s5_c_im, s5_d, s5_w_glu, s5_b_glu):
    b, t, d = x_prompt.shape
    nb = x_sample.shape[0]
    pages_per_sample = page_table.shape[1]
    past = pages_per_sample * PAGE_SIZE
    rows_p = b * t
    g = norm_g.reshape(norm_g.shape[0], 4, 1, d)

    w_in = nsa_w_in[0]
    w_q = w_in[:, :Q_W].astype(BF16)
    w_kv = w_in[:, Q_W:Q_W + 6 * KV_W].astype(BF16)
    w_gate = jnp.pad(w_in[:, Q_W + 6 * KV_W:], ((0, 0), (0, LANES - 3 * N_HEADS))).astype(BF16)
    w_o = nsa_w_o[0].astype(BF16)
    wbd, pe_t, w2bd = _compress_weights(nsa_cmp_w1[0], nsa_cmp_w2[0], nsa_cmp_pe[0])
    w1 = mlp_w1.astype(BF16)
    w2 = mlp_w2.astype(BF16)
    w_glu = s5_w_glu[0].astype(BF16)
    ops = _s5_operators(s5_a_re[0], s5_a_im[0], s5_log_dt[0], s5_b_re[0], s5_b_im[0], s5_c_re[0], s5_c_im[0])
    d_skip = s5_d[0].reshape(1, d)
    b_glu = s5_b_glu[0].reshape(1, d)

    tm = _tile(rows_p, ROW_TILE)
    ff_chunk = _tile(mlp_w1.shape[2], FF_CHUNK)

    xp = x_prompt.reshape(rows_p, d)
    cos_p, sin_p = _rope_tables(jnp.arange(t, dtype=jnp.int32))
    tq = next(c for c in (2 * LANES, LANES) if t % c == 0 and WINDOW % c == 0)
    (qT, qrT, gT, kvc, kvcT, kvsT, kvwT, ksb, kwb, vsT, vwT) = _inproj(
        xp, g[0, 0], w_q.T, w_kv, w_gate, cos_p, sin_p, tm=_tile(t, ROW_TILE), pos_blocks=t // _tile(t, ROW_TILE),
        transposed=True,
        key_chunk=tq)
    nsb_p = t // CMP_STRIDE
    nc_p = nsb_p - CMP_BLOCK // CMP_STRIDE + 1
    nsel_p = t // SEL_BLOCK
    kc, vcT = _compress_prompt(kvc.reshape(b, t, 2 * KV_W), wbd, pe_t, w2bd)
    ovT = _overlap(nc_p, nsb_p, nsel_p, nsel_p).T
    oT = _attn_prompt(qT, qrT, gT, kc, vcT, ksb, vsT, kwb, vwT, ovT, batch=b, seq=t, tq=tq, nc=nc_p,
                      n_top=min(TOP_N, nsel_p))
    hp, xn1 = _layer_tail((oT, w_o), xp, g[0, 1], g[0, 2], w1[0], w2[0], g[0, 3], g[1, 0], mixer="nsa", tm=tm,
                          ff_chunk=ff_chunk, transposed=True)

    pw = SSM_UNIT * SSM_GROUP
    npair = d // pw
    y3, hfin = _s5_seq(xn1.reshape(b, t, d), ops)
    yc = y3.reshape(rows_p, d)
    (hp,) = _layer_tail((yc, xn1, d_skip, w_glu, b_glu), hp, g[1, 1], g[1, 2], w1[1], w2[1], g[1, 3], None,
                        mixer="s5", tm=tm, ff_chunk=ff_chunk)
    ssm_p = hfin.reshape(npair, 2, b, SSM_UNIT, SSM_STATE).transpose(2, 1, 0, 3, 4)
    ssm_p = ssm_p.reshape(b, 2, d // SSM_GROUP, SSM_STATE)

    xs = x_sample.reshape(nb, d)
    cos_s, sin_s = _rope_tables(jnp.full((nb,), past, dtype=jnp.int32))
    q_s, qr_s, gates_s, kvc_s, kvs_s, kvw_s = _inproj(
        xs, g[0, 0], w_q, w_kv, w_gate, cos_s, sin_s, tm=nb, pos_blocks=1, transposed=False)
    pt_flat = page_table.reshape(-1).astype(jnp.int32)
    pg = _tile(pages_per_sample, PAGE_GROUP)
    n_pool = cache_kv_cmp.shape[1]
    feature_major = lambda c, n, s: c.transpose(0, 2, 3, 4, 1).reshape(n, 2, KV_W, s)
    cmp_pages = feature_major(cache_kv_cmp[0], n_pool, PAGE_SIZE)
    sel_pages = feature_major(cache_kv_sel[0], n_pool, PAGE_SIZE)
    kcvc = _compress_paged(cmp_pages, pt_flat, wbd, pe_t, w2bd, nb=nb, pages_per_sample=pages_per_sample, pg=pg)
    l_all = past + 1
    nsb_s = l_all // CMP_STRIDE
    nc_s = nsb_s - CMP_BLOCK // CMP_STRIDE + 1
    nsel_s = -(-l_all // SEL_BLOCK)
    nsp = -(-nsel_s // LANES) * LANES
    ov_s = _overlap(nc_s, past // CMP_STRIDE, nsel_s, nsp)
    keys_per_step = pg * PAGE_SIZE
    key_blk = (np.arange(past) // SEL_BLOCK).reshape(past // keys_per_step, 1, keys_per_step)
    e_mat = jnp.asarray(np.arange(nsp)[None, :, None] == key_blk, BF16)
    hh = np.arange(N_HEADS)
    gsum = jnp.asarray((hh[:, None] // GROUP) == (hh[None, :] // GROUP), BF16)
    win = feature_major(cache_kv_win[0], nb, WINDOW)
    o_s = _attn_sample(sel_pages, pt_flat, q_s.reshape(nb, N_HEADS, HEAD_DIM), qr_s.reshape(nb, N_HEADS, HEAD_DIM),
                       gates_s[:, :3 * N_HEADS].reshape(nb, N_HEADS, 3), kcvc, kvs_s.reshape(nb, 1, 2 * KV_W),
                       kvw_s.reshape(nb, 1, 2 * KV_W), win, ov_s, e_mat, gsum, nb=nb,
                       pages_per_sample=pages_per_sample, pg=pg, past=past, nc=nc_s, ns_valid=nsel_s,
                       n_top=min(TOP_N, nsel_s))
    hs, xn1_s = _layer_tail((o_s.reshape(nb, Q_W), w_o), xs, g[0, 1], g[0, 2], w1[0], w2[0], g[0, 3], g[1, 0],
                            mixer="nsa", tm=nb, ff_chunk=ff_chunk)

    u2_s = xn1_s.astype(BF16).reshape(nb, npair, pw).transpose(1, 0, 2)
    st = state_ssm[0].reshape(nb, 2, npair, SSM_UNIT * SSM_STATE).transpose(1, 2, 0, 3)
    y2_s, hr_s, hi_s = _s5_step(u2_s, st[0], st[1], ops)
    yc_s = y2_s.transpose(1, 0, 2).reshape(nb, d)
    (hs,) = _layer_tail((yc_s, xn1_s, d_skip, w_glu, b_glu), hs, g[1, 1], g[1, 2], w1[1], w2[1], g[1, 3], None,
                        mixer="s5", tm=nb, ff_chunk=ff_chunk)
    ssm_s = jnp.stack([hr_s, hi_s], axis=0).transpose(2, 0, 1, 3).reshape(nb, 2, d // SSM_GROUP, SSM_STATE)

    kv5 = lambda a, n, s: a.reshape(1, n, s, 2, N_KV_HEADS, HEAD_DIM)
    from_fm = lambda a, n, s: a.reshape(n, 2, N_KV_HEADS, HEAD_DIM, s).transpose(0, 4, 1, 2, 3)[None]
    win_s = jnp.concatenate([win[..., 1:], kvw_s.reshape(nb, 2, KV_W, 1)], axis=-1)
    return (hp.reshape(b, t, d), hs.reshape(nb, 1, d),
            from_fm(kvcT, b, t), kv5(kvc_s, nb, 1), from_fm(kvsT, b, t), kv5(kvs_s, nb, 1),
            from_fm(kvwT[:, :, t - WINDOW:], b, WINDOW), from_fm(win_s, nb, WINDOW), ssm_p[None], ssm_s[None])
```
